```python
import math
import jax, jax.numpy as jnp
from jax import lax
import numpy as np

D_MODEL = 1024
BATCH = 8
SEQ = 2048
DEPTH = 4

D_HEAD = 64
A_WIDTH = D_MODEL // 4
B_HEADS = (D_MODEL // 2) // D_HEAD
B_WIDTH = B_HEADS * D_HEAD
C_WIDTH = D_MODEL // 4
D_MIX = A_WIDTH + B_WIDTH + C_WIDTH
SPLIT_SIZES = (A_WIDTH, A_WIDTH, A_WIDTH,
               B_WIDTH, B_WIDTH, B_WIDTH,
               C_WIDTH, C_WIDTH)
IN_COLS = sum(SPLIT_SIZES)
DILATED_BRANCHES = ((128, 1), (512, 4), (2048, 16))
BLK = 128
NUM_BUCKETS = 32
MAX_DISTANCE = 2048
SHORT_CONV = 3
CONFORMER_CONV = 31
FFN_CONV = 3
D_FF = ((8 * D_MODEL // 3 + 127) // 128) * 128
EPS = 1e-6
NEG = -1e30

kernel_name = 'hybrid_shortconv_dilatedattn_conformer_trunk'


def rmsnorm(x, g):
    xf = x.astype(jnp.float32)
    y = xf * lax.rsqrt(jnp.mean(xf * xf, axis=-1, keepdims=True) + EPS)
    return (y * g.astype(jnp.float32)).astype(x.dtype)


def layernorm(x, g, b):
    xf = x.astype(jnp.float32)
    mu = jnp.mean(xf, axis=-1, keepdims=True)
    var = jnp.mean(jnp.square(xf - mu), axis=-1, keepdims=True)
    y = (xf - mu) * lax.rsqrt(var + EPS)
    return (y * g.astype(jnp.float32) + b.astype(jnp.float32)).astype(x.dtype)


def causal_dwconv(x, w):
    k_width, ch = w.shape
    return lax.conv_general_dilated(
        x, w[:, None, :].astype(x.dtype), window_strides=(1,),
        padding=[(k_width - 1, 0)], dimension_numbers=('NWC', 'WIO', 'NWC'),
        feature_group_count=ch)


def t5_bucket(dist):
    max_exact = NUM_BUCKETS // 2
    d_f = jnp.maximum(dist, 1).astype(jnp.float32)
    large = max_exact + (jnp.log(d_f / max_exact) / math.log(MAX_DISTANCE / max_exact)
                         * (NUM_BUCKETS - max_exact)).astype(jnp.int32)
    large = jnp.minimum(large, NUM_BUCKETS - 1)
    return jnp.where(dist < max_exact, dist, large)


def dilated_branch(q, k, v, rel_bias, window, dilation):
    bsz, seq, heads, hd = q.shape
    n_keys = window // dilation
    sub_len = seq // dilation
    n_blk = -(-sub_len // BLK)
    pad = n_blk * BLK - sub_len

    def to_blocks(t):
        t = t.reshape(bsz, sub_len, dilation, heads, hd).transpose(0, 2, 3, 1, 4)
        t = jnp.pad(t, ((0, 0), (0, 0), (0, 0), (0, pad), (0, 0)))
        return t.reshape(bsz, dilation, heads, n_blk, BLK, hd)

    qb, kb, vb = to_blocks(q), to_blocks(k), to_blocks(v)

    def with_prev(t):
        prev = jnp.concatenate([jnp.zeros_like(t[:, :, :, :1]), t[:, :, :, :-1]], axis=3)
        return jnp.concatenate([prev, t], axis=4)

    kk, vv = with_prev(kb), with_prev(vb)
    s = jnp.einsum('bdhnqc,bdhnkc->bdhnqk', qb, kk).astype(jnp.float32) * (hd ** -0.5)
    rel = jnp.arange(BLK)[:, None] - jnp.arange(2 * BLK)[None, :] + BLK
    k_idx = jnp.arange(n_blk)[:, None] * BLK + jnp.arange(2 * BLK)[None, :] - BLK
    valid = ((rel >= 0) & (rel <= n_keys))[None] & (k_idx >= 0)[:, None, :]
    bias = rel_bias[t5_bucket(jnp.maximum(rel, 0) * dilation)]
    bias = bias.transpose(2, 0, 1).astype(jnp.float32)[:, None]
    s = jnp.where(valid, s + bias, NEG)
    m = jnp.max(s, axis=-1, keepdims=True)
    p = jnp.exp(s - m)
    den = jnp.sum(p, axis=-1, keepdims=True)
    o = jnp.einsum('bdhnqk,bdhnkc->bdhnqc', p.astype(v.dtype), vv).astype(jnp.float32) / den
    lse = (m + jnp.log(den))[..., 0]
    o = o.reshape(bsz, dilation, heads, n_blk * BLK, hd)[:, :, :, :sub_len]
    o = o.transpose(0, 3, 1, 2, 4).reshape(bsz, seq, heads, hd)
    lse = lse.reshape(bsz, dilation, heads, n_blk * BLK)[..., :sub_len]
    lse = lse.transpose(0, 3, 1, 2).reshape(bsz, seq, heads)
    return o, lse


def dilated_mixture(q, k, v, rel_bias):
    outs, lses = [], []
    for window, dilation in DILATED_BRANCHES:
        o, l = dilated_branch(q, k, v, rel_bias, window, dilation)
        outs.append(o)
        lses.append(l)
    wts = jax.nn.softmax(jnp.stack(lses, axis=0), axis=0)
    return jnp.sum(wts[..., None] * jnp.stack(outs, axis=0), axis=0)


def _fwd_setup_inputs(seed: int = 0) -> dict:
    key = jax.random.key(seed)
    ks = jax.random.split(key, 20)
    f32 = jnp.float32

    def nrm(k, shape, scale):
        return jax.random.normal(k, shape, f32) * scale

    return {
        'x': nrm(ks[0], (BATCH, SEQ, D_MODEL), 1.0),
        'norm_mix_g': 1.0 + nrm(ks[1], (DEPTH, D_MODEL), 0.02),
        'w_in': nrm(ks[2], (DEPTH, D_MODEL, IN_COLS), D_MODEL ** -0.5),
        'conv_a_w': nrm(ks[3], (DEPTH, SHORT_CONV, A_WIDTH), SHORT_CONV ** -0.5),
        'conv_c_w': nrm(ks[4], (DEPTH, CONFORMER_CONV, C_WIDTH), CONFORMER_CONV ** -0.5),
        'conv_c_b': nrm(ks[5], (DEPTH, C_WIDTH), 0.02),
        'ln_c_g': 1.0 + nrm(ks[6], (DEPTH, C_WIDTH), 0.02),
        'ln_c_b': nrm(ks[7], (DEPTH, C_WIDTH), 0.02),
        'out_norm_g': 1.0 + nrm(ks[8], (DEPTH, D_MIX), 0.02),
        'w_out': nrm(ks[9], (DEPTH, D_MIX, D_MODEL), D_MIX ** -0.5),
        'norm_ffn_g': 1.0 + nrm(ks[10], (DEPTH, D_MODEL), 0.02),
        'w_up': nrm(ks[11], (DEPTH, D_MODEL, 2 * D_FF), D_MODEL ** -0.5),
        'conv_f_w': nrm(ks[12], (DEPTH, FFN_CONV, 2 * D_FF), FFN_CONV ** -0.5),
        'w_down': nrm(ks[13], (DEPTH, D_FF, D_MODEL), D_FF ** -0.5),
        'rel_bias': nrm(ks[14], (NUM_BUCKETS, B_HEADS), 0.5),
        'final_g': 1.0 + nrm(ks[15], (D_MODEL,), 0.02),
    }


def _fwd_reference(x, norm_mix_g, w_in, conv_a_w, conv_c_w, conv_c_b, ln_c_g, ln_c_b,
              out_norm_g, w_out, norm_ffn_g, w_up, conv_f_w, w_down, rel_bias, final_g):
    bsz, seq, _ = x.shape
    split_idx = list(np.cumsum(SPLIT_SIZES)[:-1])
    g_idx = [A_WIDTH, A_WIDTH + B_WIDTH]
    for l in range(DEPTH):
        h = rmsnorm(x, norm_mix_g[l])
        z = h @ w_in[l]
        a_h, a_b, a_c, q, k, v, c_val, c_gate = jnp.split(z, split_idx, axis=-1)
        y_a = a_b * causal_dwconv(a_c * a_h, conv_a_w[l])
        hs = (bsz, seq, B_HEADS, D_HEAD)
        y_b = dilated_mixture(q.reshape(hs), k.reshape(hs), v.reshape(hs), rel_bias)
        y_b = y_b.reshape(bsz, seq, B_WIDTH).astype(x.dtype)
        u = c_val * jax.nn.sigmoid(c_gate)
        u = causal_dwconv(u, conv_c_w[l]) + conv_c_b[l].astype(u.dtype)
        y_c = jax.nn.silu(layernorm(u, ln_c_g[l], ln_c_b[l]))
        g_a, g_b, g_c = jnp.split(out_norm_g[l], g_idx)
        y = jnp.concatenate([rmsnorm(y_a, g_a), rmsnorm(y_b, g_b), rmsnorm(y_c, g_c)], axis=-1)
        x = x + y @ w_out[l]
        h = rmsnorm(x, norm_ffn_g[l])
        up = causal_dwconv(h @ w_up[l], conv_f_w[l])
        gate, val = jnp.split(up, 2, axis=-1)
        x = x + (jax.nn.silu(gate) * val) @ w_down[l]
    return rmsnorm(x, final_g)


import jax as _jax
import jax.numpy as _jnp

TWIN_FORMAT = 'train_step'
FWD_PARAMS = ['x', 'norm_mix_g', 'w_in', 'conv_a_w', 'conv_c_w', 'conv_c_b', 'ln_c_g', 'ln_c_b', 'out_norm_g', 'w_out', 'norm_ffn_g', 'w_up', 'conv_f_w', 'w_down', 'rel_bias', 'final_g']
TWIN_WEIGHTS = ['norm_mix_g', 'w_in', 'conv_a_w', 'conv_c_w', 'conv_c_b', 'ln_c_g', 'ln_c_b', 'out_norm_g', 'w_out', 'norm_ffn_g', 'w_up', 'conv_f_w', 'w_down', 'rel_bias', 'final_g']
TWIN_DIFF_INPUT = 'x'
TWIN_INPUTS = ['x', 'norm_mix_g', 'w_in', 'conv_a_w', 'conv_c_w', 'conv_c_b', 'ln_c_g', 'ln_c_b', 'out_norm_g', 'w_out', 'norm_ffn_g', 'w_up', 'conv_f_w', 'w_down', 'rel_bias', 'final_g', 'loss_target', 'm_norm_mix_g', 'm_w_in', 'm_conv_a_w', 'm_conv_c_w', 'm_conv_c_b', 'm_ln_c_g', 'm_ln_c_b', 'm_out_norm_g', 'm_w_out', 'm_norm_ffn_g', 'm_w_up', 'm_conv_f_w', 'm_w_down', 'm_rel_bias', 'm_final_g', 'v_norm_mix_g', 'v_w_in', 'v_conv_a_w', 'v_conv_c_w', 'v_conv_c_b', 'v_ln_c_g', 'v_ln_c_b', 'v_out_norm_g', 'v_w_out', 'v_norm_ffn_g', 'v_w_up', 'v_conv_f_w', 'v_w_down', 'v_rel_bias', 'v_final_g']
TWIN_OUTPUTS = ['loss', 'grad_x', 'grad_norm_mix_g', 'grad_w_in', 'grad_conv_a_w', 'grad_conv_c_w', 'grad_conv_c_b', 'grad_ln_c_g', 'grad_ln_c_b', 'grad_out_norm_g', 'grad_w_out', 'grad_norm_ffn_g', 'grad_w_up', 'grad_conv_f_w', 'grad_w_down', 'grad_rel_bias', 'grad_final_g', 'delta_norm_mix_g', 'delta_w_in', 'delta_conv_a_w', 'delta_conv_c_w', 'delta_conv_c_b', 'delta_ln_c_g', 'delta_ln_c_b', 'delta_out_norm_g', 'delta_w_out', 'delta_norm_ffn_g', 'delta_w_up', 'delta_conv_f_w', 'delta_w_down', 'delta_rel_bias', 'delta_final_g', 'new_m_norm_mix_g', 'new_m_w_in', 'new_m_conv_a_w', 'new_m_conv_c_w', 'new_m_conv_c_b', 'new_m_ln_c_g', 'new_m_ln_c_b', 'new_m_out_norm_g', 'new_m_w_out', 'new_m_norm_ffn_g', 'new_m_w_up', 'new_m_conv_f_w', 'new_m_w_down', 'new_m_rel_bias', 'new_m_final_g', 'new_v_norm_mix_g', 'new_v_w_in', 'new_v_conv_a_w', 'new_v_conv_c_w', 'new_v_conv_c_b', 'new_v_ln_c_g', 'new_v_ln_c_b', 'new_v_out_norm_g', 'new_v_w_out', 'new_v_norm_ffn_g', 'new_v_w_up', 'new_v_conv_f_w', 'new_v_w_down', 'new_v_rel_bias', 'new_v_final_g']
TWIN_LEAF_KINDS = {'loss': 'loss', 'grad_x': 'grad_x', 'grad_norm_mix_g': 'grad_w', 'grad_w_in': 'grad_w', 'grad_conv_a_w': 'grad_w', 'grad_conv_c_w': 'grad_w', 'grad_conv_c_b': 'grad_w', 'grad_ln_c_g': 'grad_w', 'grad_ln_c_b': 'grad_w', 'grad_out_norm_g': 'grad_w', 'grad_w_out': 'grad_w', 'grad_norm_ffn_g': 'grad_w', 'grad_w_up': 'grad_w', 'grad_conv_f_w': 'grad_w', 'grad_w_down': 'grad_w', 'grad_rel_bias': 'grad_w', 'grad_final_g': 'grad_w', 'delta_norm_mix_g': 'delta_w', 'delta_w_in': 'delta_w', 'delta_conv_a_w': 'delta_w', 'delta_conv_c_w': 'delta_w', 'delta_conv_c_b': 'delta_w', 'delta_ln_c_g': 'delta_w', 'delta_ln_c_b': 'delta_w', 'delta_out_norm_g': 'delta_w', 'delta_w_out': 'delta_w', 'delta_norm_ffn_g': 'delta_w', 'delta_w_up': 'delta_w', 'delta_conv_f_w': 'delta_w', 'delta_w_down': 'delta_w', 'delta_rel_bias': 'delta_w', 'delta_final_g': 'delta_w', 'new_m_norm_mix_g': 'new_m', 'new_m_w_in': 'new_m', 'new_m_conv_a_w': 'new_m', 'new_m_conv_c_w': 'new_m', 'new_m_conv_c_b': 'new_m', 'new_m_ln_c_g': 'new_m', 'new_m_ln_c_b': 'new_m', 'new_m_out_norm_g': 'new_m', 'new_m_w_out': 'new_m', 'new_m_norm_ffn_g': 'new_m', 'new_m_w_up': 'new_m', 'new_m_conv_f_w': 'new_m', 'new_m_w_down': 'new_m', 'new_m_rel_bias': 'new_m', 'new_m_final_g': 'new_m', 'new_v_norm_mix_g': 'new_v', 'new_v_w_in': 'new_v', 'new_v_conv_a_w': 'new_v', 'new_v_conv_c_w': 'new_v', 'new_v_conv_c_b': 'new_v', 'new_v_ln_c_g': 'new_v', 'new_v_ln_c_b': 'new_v', 'new_v_out_norm_g': 'new_v', 'new_v_w_out': 'new_v', 'new_v_norm_ffn_g': 'new_v', 'new_v_w_up': 'new_v', 'new_v_conv_f_w': 'new_v', 'new_v_w_down': 'new_v', 'new_v_rel_bias': 'new_v', 'new_v_final_g': 'new_v'}


def _forward(args):
    return _fwd_reference(*[args[k] for k in FWD_PARAMS])


def _output_shape():
    out = _jax.eval_shape(lambda: _forward(_fwd_setup_inputs(0)))
    return out.shape, out.dtype

N_MICROBATCH = 1
ADAM_LR = 0.001
ADAM_B1 = 0.9
ADAM_B2 = 0.999
ADAM_EPS = 1e-08
ADAM_WD = 0.01
ADAM_STEP = 10
PER_EXAMPLE_BATCH_AXIS = {'x': 0, 'loss_target': 0}
SHARED_INPUTS = []
_WEIGHT_DTYPES = {'norm_mix_g': _jnp.float32, 'w_in': _jnp.float32, 'conv_a_w': _jnp.float32, 'conv_c_w': _jnp.float32, 'conv_c_b': _jnp.float32, 'ln_c_g': _jnp.float32, 'ln_c_b': _jnp.float32, 'out_norm_g': _jnp.float32, 'w_out': _jnp.float32, 'norm_ffn_g': _jnp.float32, 'w_up': _jnp.float32, 'conv_f_w': _jnp.float32, 'w_down': _jnp.float32, 'rel_bias': _jnp.float32, 'final_g': _jnp.float32}
MOMENT_SCALE = {'norm_mix_g': 1.320087e-01, 'w_in': 7.886722e-02, 'conv_a_w': 8.920346e-02, 'conv_c_w': 8.874795e-02, 'conv_c_b': 3.349745e-01, 'ln_c_g': 1.493934e-01, 'ln_c_b': 2.089961e-01, 'out_norm_g': 1.008618e-01, 'w_out': 1.021528e-01, 'norm_ffn_g': 6.825770e-02, 'w_up': 2.840437e-02, 'conv_f_w': 2.827600e-02, 'w_down': 4.666450e-02, 'rel_bias': 1.666527e-01, 'final_g': 1.624111e+01}


def _to_microbatches(a, axis):
    t = _jnp.moveaxis(a, axis, 0)
    t = t.reshape((N_MICROBATCH, t.shape[0] // N_MICROBATCH) + t.shape[1:])
    return _jnp.moveaxis(t, 1, axis + 1)


def setup_inputs(seed: int = 0) -> dict:
    inp = _fwd_setup_inputs(seed)
    key = _jax.random.fold_in(_jax.random.key(seed), 7919)
    shape, _ = _output_shape()
    out = dict(inp)
    out["loss_target"] = _jax.random.normal(_jax.random.fold_in(key, 0), shape, _jnp.float32)
    for i, name in enumerate(TWIN_WEIGHTS):
        w = inp[name].astype(_jnp.float32)
        if MOMENT_SCALE is None:
            s = _jnp.sqrt(_jnp.mean(_jnp.square(w)) + 1e-30)
        else:
            s = MOMENT_SCALE[name]
        km, kv = _jax.random.split(_jax.random.fold_in(key, i + 1))
        out[name] = w
        out["m_" + name] = s * _jax.random.normal(km, w.shape, _jnp.float32)
        out["v_" + name] = (s * s) * _jax.random.uniform(kv, w.shape, _jnp.float32, 0.5, 1.5)
    if N_MICROBATCH > 1:
        for name, axis in PER_EXAMPLE_BATCH_AXIS.items():
            out[name] = _to_microbatches(out[name], axis)
    return {'x': out['x'], 'norm_mix_g': out['norm_mix_g'], 'w_in': out['w_in'], 'conv_a_w': out['conv_a_w'], 'conv_c_w': out['conv_c_w'], 'conv_c_b': out['conv_c_b'], 'ln_c_g': out['ln_c_g'], 'ln_c_b': out['ln_c_b'], 'out_norm_g': out['out_norm_g'], 'w_out': out['w_out'], 'norm_ffn_g': out['norm_ffn_g'], 'w_up': out['w_up'], 'conv_f_w': out['conv_f_w'], 'w_down': out['w_down'], 'rel_bias': out['rel_bias'], 'final_g': out['final_g'], 'loss_target': out['loss_target'], 'm_norm_mix_g': out['m_norm_mix_g'], 'm_w_in': out['m_w_in'], 'm_conv_a_w': out['m_conv_a_w'], 'm_conv_c_w': out['m_conv_c_w'], 'm_conv_c_b': out['m_conv_c_b'], 'm_ln_c_g': out['m_ln_c_g'], 'm_ln_c_b': out['m_ln_c_b'], 'm_out_norm_g': out['m_out_norm_g'], 'm_w_out': out['m_w_out'], 'm_norm_ffn_g': out['m_norm_ffn_g'], 'm_w_up': out['m_w_up'], 'm_conv_f_w': out['m_conv_f_w'], 'm_w_down': out['m_w_down'], 'm_rel_bias': out['m_rel_bias'], 'm_final_g': out['m_final_g'], 'v_norm_mix_g': out['v_norm_mix_g'], 'v_w_in': out['v_w_in'], 'v_conv_a_w': out['v_conv_a_w'], 'v_conv_c_w': out['v_conv_c_w'], 'v_conv_c_b': out['v_conv_c_b'], 'v_ln_c_g': out['v_ln_c_g'], 'v_ln_c_b': out['v_ln_c_b'], 'v_out_norm_g': out['v_out_norm_g'], 'v_w_out': out['v_w_out'], 'v_norm_ffn_g': out['v_norm_ffn_g'], 'v_w_up': out['v_w_up'], 'v_conv_f_w': out['v_conv_f_w'], 'v_w_down': out['v_w_down'], 'v_rel_bias': out['v_rel_bias'], 'v_final_g': out['v_final_g']}


def _loss(weights, diff, rest, loss_target):
    with _jax.named_scope("forward"):
        args = {**rest, TWIN_DIFF_INPUT: diff, **{k: w.astype(_WEIGHT_DTYPES[k]) for k, w in weights.items()}}
        y = _forward(args)
    with _jax.named_scope("loss_head"):
        err = _jnp.square(y.astype(_jnp.float32) - loss_target)
        return 0.5 * _jnp.sum(_jnp.mean(err, axis=-1)) if err.ndim else 0.5 * err


def _adamw(w, g, m, v):
    m = ADAM_B1 * m + (1.0 - ADAM_B1) * g
    v = ADAM_B2 * v + (1.0 - ADAM_B2) * _jnp.square(g)
    m_hat = m / (1.0 - ADAM_B1 ** ADAM_STEP)
    v_hat = v / (1.0 - ADAM_B2 ** ADAM_STEP)
    delta = -ADAM_LR * (m_hat / (_jnp.sqrt(v_hat) + ADAM_EPS) + ADAM_WD * w)
    return delta, m, v


def reference(x, norm_mix_g, w_in, conv_a_w, conv_c_w, conv_c_b, ln_c_g, ln_c_b, out_norm_g, w_out, norm_ffn_g, w_up, conv_f_w, w_down, rel_bias, final_g, loss_target, m_norm_mix_g, m_w_in, m_conv_a_w, m_conv_c_w, m_conv_c_b, m_ln_c_g, m_ln_c_b, m_out_norm_g, m_w_out, m_norm_ffn_g, m_w_up, m_conv_f_w, m_w_down, m_rel_bias, m_final_g, v_norm_mix_g, v_w_in, v_conv_a_w, v_conv_c_w, v_conv_c_b, v_ln_c_g, v_ln_c_b, v_out_norm_g, v_w_out, v_norm_ffn_g, v_w_up, v_conv_f_w, v_w_down, v_rel_bias, v_final_g):
    given = dict(x=x, norm_mix_g=norm_mix_g, w_in=w_in, conv_a_w=conv_a_w, conv_c_w=conv_c_w, conv_c_b=conv_c_b, ln_c_g=ln_c_g, ln_c_b=ln_c_b, out_norm_g=out_norm_g, w_out=w_out, norm_ffn_g=norm_ffn_g, w_up=w_up, conv_f_w=conv_f_w, w_down=w_down, rel_bias=rel_bias, final_g=final_g, loss_target=loss_target, m_norm_mix_g=m_norm_mix_g, m_w_in=m_w_in, m_conv_a_w=m_conv_a_w, m_conv_c_w=m_conv_c_w, m_conv_c_b=m_conv_c_b, m_ln_c_g=m_ln_c_g, m_ln_c_b=m_ln_c_b, m_out_norm_g=m_out_norm_g, m_w_out=m_w_out, m_norm_ffn_g=m_norm_ffn_g, m_w_up=m_w_up, m_conv_f_w=m_conv_f_w, m_w_down=m_w_down, m_rel_bias=m_rel_bias, m_final_g=m_final_g, v_norm_mix_g=v_norm_mix_g, v_w_in=v_w_in, v_conv_a_w=v_conv_a_w, v_conv_c_w=v_conv_c_w, v_conv_c_b=v_conv_c_b, v_ln_c_g=v_ln_c_g, v_ln_c_b=v_ln_c_b, v_out_norm_g=v_out_norm_g, v_w_out=v_w_out, v_norm_ffn_g=v_norm_ffn_g, v_w_up=v_w_up, v_conv_f_w=v_conv_f_w, v_w_down=v_w_down, v_rel_bias=v_rel_bias, v_final_g=v_final_g)
    weights = {n: given[n] for n in TWIN_WEIGHTS}
    shared = {n: given[n] for n in SHARED_INPUTS}
    per_example = {n: given[n] for n in ['x']}
    grad_fn = _jax.value_and_grad(_loss, argnums=(0, 1))

    def one_microbatch(ex, loss_target):
        ex = dict(ex)
        diff = ex.pop(TWIN_DIFF_INPUT)
        return grad_fn(weights, diff, {**shared, **ex}, loss_target)

    if N_MICROBATCH == 1:
        loss, (grad_w, grad_x) = one_microbatch(per_example, given["loss_target"])
    else:
        def body(carry, xs):
            loss_sum, grad_sum = carry
            l_k, (gw_k, gx_k) = one_microbatch(xs[0], xs[1])
            with _jax.named_scope("update"):
                return (loss_sum + l_k, _jax.tree.map(_jnp.add, grad_sum, gw_k)), gx_k

        init = (_jnp.zeros((), _jnp.float32), _jax.tree.map(_jnp.zeros_like, weights))
        (loss, grad_w), grad_x = _jax.lax.scan(body, init, (per_example, given["loss_target"]))
    with _jax.named_scope("update"):
        delta_w, new_m, new_v = {}, {}, {}
        for n in TWIN_WEIGHTS:
            delta_w[n], new_m[n], new_v[n] = _adamw(weights[n], grad_w[n], given["m_" + n], given["v_" + n])
    return (loss, grad_x, *[grad_w[n] for n in TWIN_WEIGHTS], *[delta_w[n] for n in TWIN_WEIGHTS],
            *[new_m[n] for n in TWIN_WEIGHTS], *[new_v[n] for n in TWIN_WEIGHTS])
```

```python
import functools
import math

import numpy as np
import jax
import jax.numpy as jnp
from jax import lax
from jax.experimental import pallas as pl
from jax.experimental.pallas import tpu as pltpu

F32 = jnp.float32
BF16 = jnp.bfloat16
I32 = jnp.int32

EPS = 1e-6
NEG = -1e30
D_HEAD = 64
LANES = 128
BLK = 128
DILATED_BRANCHES = ((128, 1), (512, 4), (2048, 16))
NUM_BUCKETS = 32
MAX_DISTANCE = 2048
SHORT_CONV = 3
CONFORMER_CONV = 31
FFN_CONV = 3
PAD_SHORT = 8
PAD_LONG = 32
ROW_CHUNK = 256
V7X_VMEM_BYTES = 64 * 1024 * 1024
VMEM_REQUEST = V7X_VMEM_BYTES * 7 // 8

ADAM_LR = 0.001
ADAM_B1 = 0.9
ADAM_B2 = 0.999
ADAM_EPS = 1e-08
ADAM_WD = 0.01
ADAM_STEP = 10

MESH = pl.DeviceIdType.MESH
ANY = pl.BlockSpec(memory_space=pl.ANY)


def _sds(shape, dtype):
    return jax.ShapeDtypeStruct(tuple(shape), dtype)


def _pcall(body, *, name, out_shape, grid=(), in_specs=None, out_specs=None, scratch_shapes=(), vmem=VMEM_REQUEST,
           aliases=None, prefetch=0):
    params = pltpu.CompilerParams(dimension_semantics=("arbitrary",) * len(grid), vmem_limit_bytes=vmem)
    if prefetch:
        spec = pltpu.PrefetchScalarGridSpec(num_scalar_prefetch=prefetch, grid=grid, in_specs=in_specs,
                                            out_specs=out_specs, scratch_shapes=list(scratch_shapes))
        return pl.pallas_call(body, name=name, out_shape=out_shape, grid_spec=spec, compiler_params=params)
    kwargs = {}
    if in_specs is not None:
        kwargs["in_specs"] = in_specs
    if out_specs is not None:
        kwargs["out_specs"] = out_specs
    return pl.pallas_call(
        body, name=name, out_shape=out_shape, grid=grid, scratch_shapes=list(scratch_shapes),
        input_output_aliases=aliases or {}, compiler_params=params, **kwargs)


def _dot(a, b):
    return lax.dot_general(a, b, (((1,), (0,)), ((), ())), preferred_element_type=F32)


def _dot_nt(a, b):
    return lax.dot_general(a, b, (((1,), (1,)), ((), ())), preferred_element_type=F32)


def _dot_tn(a, b):
    return lax.dot_general(a, b, (((0,), (0,)), ((), ())), preferred_element_type=F32)


def _sigmoid(x):
    return 1.0 / (1.0 + jnp.exp(-x))


def _rstd(x):
    return lax.rsqrt(jnp.mean(x * x, axis=-1, keepdims=True) + EPS)


def _rms_fwd(x, g, name):
    s, d = x.shape
    tm = ROW_CHUNK

    def body(x_ref, g_ref, o_ref):
        xv = x_ref[...]
        o_ref[...] = (xv * _rstd(xv) * g_ref[...]).astype(BF16)

    return _pcall(body, name=name, out_shape=_sds((s, d), BF16), grid=(s // tm,),
                  in_specs=[pl.BlockSpec((tm, d), lambda i: (i, 0)), pl.BlockSpec((1, d), lambda i: (0, 0))],
                  out_specs=pl.BlockSpec((tm, d), lambda i: (i, 0)))(x, g)


def _rms_bwd(x, g, dh, dres, name):
    s, d = x.shape
    tm = ROW_CHUNK

    def body(x_ref, g_ref, dh_ref, dres_ref, dx_ref, dxb_ref, dg_ref):
        i = pl.program_id(0)
        xv = x_ref[...]
        r = _rstd(xv)
        xh = xv * r
        dhv = dh_ref[...]
        gd = dhv * g_ref[...]
        dx = dres_ref[...] + r * (gd - xh * jnp.mean(gd * xh, axis=-1, keepdims=True))
        dx_ref[...] = dx
        dxb_ref[...] = dx.astype(BF16)
        part = jnp.sum(dhv * xh, axis=0, keepdims=True)

        @pl.when(i == 0)
        def _():
            dg_ref[...] = part

        @pl.when(i > 0)
        def _():
            dg_ref[...] += part

    row = pl.BlockSpec((tm, d), lambda i: (i, 0))
    vec = pl.BlockSpec((1, d), lambda i: (0, 0))
    return _pcall(body, name=name, out_shape=(_sds((s, d), F32), _sds((s, d), BF16), _sds((1, d), F32)),
                  grid=(s // tm,), in_specs=[row, vec, row, row], out_specs=(row, row, vec))(x, g, dh, dres)


def _final_loss(x, g, tgt, name):
    s, d = x.shape
    tm = ROW_CHUNK

    def body(x_ref, g_ref, t_ref, loss_ref, dx_ref, dxb_ref, dg_ref):
        i = pl.program_id(0)
        xv = x_ref[...]
        r = _rstd(xv)
        xh = xv * r
        e = xh * g_ref[...] - t_ref[...]
        lpart = 0.5 * jnp.sum(jnp.mean(e * e, axis=-1, keepdims=True), axis=0, keepdims=True)
        dy = e * (1.0 / d)
        gd = dy * g_ref[...]
        dx = r * (gd - xh * jnp.mean(gd * xh, axis=-1, keepdims=True))
        dx_ref[...] = dx
        dxb_ref[...] = dx.astype(BF16)
        part = jnp.sum(dy * xh, axis=0, keepdims=True)
        lrow = jnp.broadcast_to(lpart, (1, LANES))

        @pl.when(i == 0)
        def _():
            dg_ref[...] = part
            loss_ref[...] = lrow

        @pl.when(i > 0)
        def _():
            dg_ref[...] += part
            loss_ref[...] += lrow

    row = pl.BlockSpec((tm, d), lambda i: (i, 0))
    vec = pl.BlockSpec((1, d), lambda i: (0, 0))
    return _pcall(body, name=name,
                  out_shape=(_sds((1, LANES), F32), _sds((s, d), F32), _sds((s, d), BF16), _sds((1, d), F32)),
                  grid=(s // tm,), in_specs=[row, vec, row],
                  out_specs=(pl.BlockSpec((1, LANES), lambda i: (0, 0)), row, row, vec))(x, g, tgt)


def _mm_n(a, b, *, nt, tn, out_dtype, name):
    s, k = a.shape
    n = b.shape[0] if nt else b.shape[1]
    rows = 512

    def body(a_ref, b_ref, o_ref):
        bv = b_ref[...]
        for r0 in range(0, s, rows):
            av = a_ref[r0:r0 + rows, :]
            o_ref[r0:r0 + rows, :] = (_dot_nt(av, bv) if nt else _dot(av, bv)).astype(out_dtype)

    b_spec = pl.BlockSpec((tn, k), lambda j: (j, 0)) if nt else pl.BlockSpec((k, tn), lambda j: (0, j))
    return _pcall(body, name=name, out_shape=_sds((s, n), out_dtype), grid=(n // tn,),
                  in_specs=[pl.BlockSpec((s, k), lambda j: (0, 0)), b_spec],
                  out_specs=pl.BlockSpec((s, tn), lambda j: (0, j)))(a, b)


def _mm_k(a, b, resid, *, nt, tk, b_off, name):
    s, ka = a.shape
    n = b.shape[0] if nt else b.shape[1]
    rows = 512

    def body(a_ref, b_ref, *refs):
        o_ref = refs[-1]
        kk = pl.program_id(0)
        bv = b_ref[...]

        @pl.when(kk == 0)
        def _():
            o_ref[...] = jnp.zeros((s, n), F32) if resid is None else refs[0][...]

        for r0 in range(0, s, rows):
            av = a_ref[r0:r0 + rows, :]
            o_ref[r0:r0 + rows, :] += _dot_nt(av, bv) if nt else _dot(av, bv)

    b_spec = (pl.BlockSpec((n, tk), lambda kk: (0, kk + b_off)) if nt
              else pl.BlockSpec((tk, n), lambda kk: (kk + b_off, 0)))
    full = pl.BlockSpec((s, n), lambda kk: (0, 0))
    extra = () if resid is None else (resid,)
    return _pcall(body, name=name, out_shape=_sds((s, n), F32), grid=(ka // tk,),
                  in_specs=[pl.BlockSpec((s, tk), lambda kk: (0, kk)), b_spec] + [full] * len(extra),
                  out_specs=full)(a, b, *extra)


def _mm_tn(a, b, *, t, name):
    s, ka = a.shape
    n = b.shape[1]

    def body(a_ref, b_ref, o_ref):
        o_ref[...] = _dot_tn(a_ref[...], b_ref[...]).astype(BF16)

    return _pcall(body, name=name, out_shape=_sds((ka, n), BF16), grid=(ka // t,),
                  in_specs=[pl.BlockSpec((s, t), lambda i: (0, i)), pl.BlockSpec((s, n), lambda i: (0, 0))],
                  out_specs=pl.BlockSpec((t, n), lambda i: (i, 0)))(a, b)


def _mm_tn2(a, b_lo, b_hi, *, t, name):
    s, ka = a.shape
    half = b_lo.shape[1]
    nb = half // t

    def body(a_ref, lo_ref, hi_ref, o_ref):
        j = pl.program_id(0)

        @pl.when(j < nb)
        def _():
            o_ref[...] = _dot_tn(a_ref[...], lo_ref[...]).astype(BF16)

        @pl.when(j >= nb)
        def _():
            o_ref[...] = _dot_tn(a_ref[...], hi_ref[...]).astype(BF16)

    return _pcall(body, name=name, out_shape=_sds((ka, 2 * half), BF16), grid=(2 * nb,),
                  in_specs=[pl.BlockSpec((s, ka), lambda j: (0, 0)),
                            pl.BlockSpec((s, t), lambda j: (0, jnp.minimum(j, nb - 1))),
                            pl.BlockSpec((s, t), lambda j: (0, jnp.maximum(j - nb, 0)))],
                  out_specs=pl.BlockSpec((ka, t), lambda j: (0, j)))(a, b_lo, b_hi)


def _conv_taps(win, w_ref, width, pad, rows):
    acc = None
    for k in range(width):
        off = pad - (width - 1) + k
        term = w_ref[pl.ds(k, 1), :] * win[off:off + rows, :]
        acc = term if acc is None else acc + term
    return acc


def _conv_taps_t(win, w_ref, width, rows):
    acc = None
    for k in range(width):
        off = (width - 1) - k
        term = w_ref[pl.ds(k, 1), :] * win[off:off + rows, :]
        acc = term if acc is None else acc + term
    return acc


def _conv_wgrad(dw_ref, g, win, width, pad, rows):
    for k in range(width):
        off = pad - (width - 1) + k
        dw_ref[pl.ds(k, 1), :] += jnp.sum(g * win[off:off + rows, :], axis=0, keepdims=True)


def _mixer_a_fwd(ah, ab, ac, win_t, wa_ref, rows):
    ct = _conv_taps(win_t, wa_ref, SHORT_CONV, PAD_SHORT, rows)
    return ab * ct, ct


def _mixer_c_fwd(win_u, wc_ref, cb_ref, lg_ref, lb_ref, rows):
    u = _conv_taps(win_u, wc_ref, CONFORMER_CONV, PAD_LONG, rows) + cb_ref[...]
    mu = jnp.mean(u, axis=-1, keepdims=True)
    uc = u - mu
    rs = lax.rsqrt(jnp.mean(uc * uc, axis=-1, keepdims=True) + EPS)
    uh = uc * rs
    ln = uh * lg_ref[...] + lb_ref[...]
    sg = _sigmoid(ln)
    return ln * sg, ln, sg, uh, rs


def _mix_fwd(z, wa, wc, cb, lg, lb, ga, gc, name):
    s = z.shape[0]
    w = wa.shape[1]
    nblk = z.shape[1] // w
    rc = ROW_CHUNK

    def body(ah_ref, ab_ref, ac_ref, cv_ref, cg_ref, wa_ref, wc_ref, cb_ref, lg_ref, lb_ref, ga_ref, gc_ref,
             ya_ref, yc_ref, tpad, upad):
        tpad[pl.ds(0, PAD_SHORT), :] = jnp.zeros((PAD_SHORT, w), F32)
        upad[pl.ds(0, PAD_LONG), :] = jnp.zeros((PAD_LONG, w), F32)

        def chunk(i, carry):
            base = pl.multiple_of(i * rc, rc)
            rows = pl.ds(base, rc)
            ah, ab, ac = ah_ref[rows, :], ab_ref[rows, :], ac_ref[rows, :]
            tpad[pl.ds(base + PAD_SHORT, rc), :] = ac * ah
            ya, _ = _mixer_a_fwd(ah, ab, ac, tpad[pl.ds(base, rc + PAD_SHORT), :], wa_ref, rc)
            ya_ref[rows, :] = (ya * _rstd(ya) * ga_ref[...]).astype(BF16)
            upad[pl.ds(base + PAD_LONG, rc), :] = cv_ref[rows, :] * _sigmoid(cg_ref[rows, :])
            yc = _mixer_c_fwd(upad[pl.ds(base, rc + PAD_LONG), :], wc_ref, cb_ref, lg_ref, lb_ref, rc)[0]
            yc_ref[rows, :] = (yc * _rstd(yc) * gc_ref[...]).astype(BF16)
            return carry

        lax.fori_loop(0, s // rc, chunk, 0)

    def zblk(j):
        return pl.BlockSpec((s, w), lambda i: (0, j))

    def whole(a):
        return pl.BlockSpec(a.shape, lambda i: (0, 0))

    return _pcall(
        body, name=name, out_shape=(_sds((s, w), BF16), _sds((s, w), BF16)), grid=(1,),
        in_specs=[zblk(0), zblk(1), zblk(2), zblk(nblk - 2), zblk(nblk - 1)] + [whole(a) for a in (wa, wc, cb, lg, lb, ga, gc)],
        out_specs=(pl.BlockSpec((s, w), lambda i: (0, 0)), pl.BlockSpec((s, w), lambda i: (0, 0))),
        scratch_shapes=[pltpu.VMEM((s + PAD_SHORT, w), F32), pltpu.VMEM((s + PAD_LONG, w), F32)],
    )(z, z, z, z, z, wa, wc, cb, lg, lb, ga, gc)


def _mix_bwd(z, dy, wa, wc, cb, lg, lb, ga, gc, name):
    s = z.shape[0]
    w = wa.shape[1]
    nblk = z.shape[1] // w
    nyb = dy.shape[1] // w
    rc = ROW_CHUNK

    def body(ah_ref, ab_ref, ac_ref, cv_ref, cg_ref, dya_ref, dyc_ref,
             wa_ref, wc_ref, cb_ref, lg_ref, lb_ref, ga_ref, gc_ref,
             dza_ref, dzc_ref, dwa_ref, dwc_ref, dcb_ref, dlg_ref, dlb_ref, dga_ref, dgc_ref,
             tpad, upad, dctp, dup):
        tpad[pl.ds(0, PAD_SHORT), :] = jnp.zeros((PAD_SHORT, w), F32)
        upad[pl.ds(0, PAD_LONG), :] = jnp.zeros((PAD_LONG, w), F32)
        dctp[pl.ds(s, PAD_SHORT), :] = jnp.zeros((PAD_SHORT, w), F32)
        dup[pl.ds(s, PAD_LONG), :] = jnp.zeros((PAD_LONG, w), F32)
        for ref in (dwa_ref, dwc_ref, dcb_ref, dlg_ref, dlb_ref, dga_ref, dgc_ref):
            ref[...] = jnp.zeros(ref.shape, F32)

        def rms_bwd(y, g_ref, dyn, dg_ref):
            r = _rstd(y)
            yh = y * r
            gd = dyn * g_ref[...]
            dg_ref[...] += jnp.sum(dyn * yh, axis=0, keepdims=True)
            return r * (gd - yh * jnp.mean(gd * yh, axis=-1, keepdims=True))

        def first(i, carry):
            base = pl.multiple_of(i * rc, rc)
            rows = pl.ds(base, rc)
            ah, ab, ac = ah_ref[rows, :], ab_ref[rows, :], ac_ref[rows, :]
            tpad[pl.ds(base + PAD_SHORT, rc), :] = ac * ah
            win_t = tpad[pl.ds(base, rc + PAD_SHORT), :]
            ya, ct = _mixer_a_fwd(ah, ab, ac, win_t, wa_ref, rc)
            dya = rms_bwd(ya, ga_ref, dya_ref[rows, :], dga_ref)
            dza_ref[rows, w:2 * w] = (dya * ct).astype(BF16)
            dct = dya * ab
            dctp[rows, :] = dct
            _conv_wgrad(dwa_ref, dct, win_t, SHORT_CONV, PAD_SHORT, rc)

            upad[pl.ds(base + PAD_LONG, rc), :] = cv_ref[rows, :] * _sigmoid(cg_ref[rows, :])
            win_u = upad[pl.ds(base, rc + PAD_LONG), :]
            yc, ln, sg, uh, rs = _mixer_c_fwd(win_u, wc_ref, cb_ref, lg_ref, lb_ref, rc)
            dyc = rms_bwd(yc, gc_ref, dyc_ref[rows, :], dgc_ref)
            dln = dyc * (sg * (1.0 + ln * (1.0 - sg)))
            dlg_ref[...] += jnp.sum(dln * uh, axis=0, keepdims=True)
            dlb_ref[...] += jnp.sum(dln, axis=0, keepdims=True)
            duh = dln * lg_ref[...]
            du = rs * (duh - jnp.mean(duh, axis=-1, keepdims=True) - uh * jnp.mean(duh * uh, axis=-1, keepdims=True))
            dcb_ref[...] += jnp.sum(du, axis=0, keepdims=True)
            dup[rows, :] = du
            _conv_wgrad(dwc_ref, du, win_u, CONFORMER_CONV, PAD_LONG, rc)
            return carry

        lax.fori_loop(0, s // rc, first, 0)

        def second(i, carry):
            base = pl.multiple_of(i * rc, rc)
            rows = pl.ds(base, rc)
            dt = _conv_taps_t(dctp[pl.ds(base, rc + PAD_SHORT), :], wa_ref, SHORT_CONV, rc)
            dza_ref[rows, 0:w] = (dt * ac_ref[rows, :]).astype(BF16)
            dza_ref[rows, 2 * w:3 * w] = (dt * ah_ref[rows, :]).astype(BF16)
            du0 = _conv_taps_t(dup[pl.ds(base, rc + PAD_LONG), :], wc_ref, CONFORMER_CONV, rc)
            sg = _sigmoid(cg_ref[rows, :])
            dzc_ref[rows, 0:w] = (du0 * sg).astype(BF16)
            dzc_ref[rows, w:2 * w] = (du0 * cv_ref[rows, :] * sg * (1.0 - sg)).astype(BF16)
            return carry

        lax.fori_loop(0, s // rc, second, 0)

    def blk(j):
        return pl.BlockSpec((s, w), lambda i: (0, j))

    def whole(a):
        return pl.BlockSpec(tuple(a.shape), lambda i: (0, 0))

    params = (wa, wc, cb, lg, lb, ga, gc)
    outs = (_sds((s, 3 * w), BF16), _sds((s, 2 * w), BF16)) + tuple(_sds(p.shape, F32) for p in params)
    return _pcall(
        body, name=name, out_shape=outs, grid=(1,),
        in_specs=[blk(0), blk(1), blk(2), blk(nblk - 2), blk(nblk - 1), blk(0), blk(nyb - 1)] + [whole(p) for p in params],
        out_specs=tuple(whole(o) for o in outs),
        scratch_shapes=[pltpu.VMEM((s + PAD_SHORT, w), F32), pltpu.VMEM((s + PAD_LONG, w), F32),
                        pltpu.VMEM((s + PAD_SHORT, w), F32), pltpu.VMEM((s + PAD_LONG, w), F32)],
    )(z, z, z, z, z, dy, dy, *params)


def _ffn_act_fwd(up, wf, name):
    s, f2 = up.shape
    f = f2 // 2
    tc = 256
    nb = f // tc
    rc = ROW_CHUNK

    def body(g_ref, v_ref, wg_ref, wv_ref, o_ref, gpad, vpad):
        gpad[pl.ds(0, PAD_SHORT), :] = jnp.zeros((PAD_SHORT, tc), F32)
        vpad[pl.ds(0, PAD_SHORT), :] = jnp.zeros((PAD_SHORT, tc), F32)

        def chunk(i, carry):
            base = pl.multiple_of(i * rc, rc)
            rows = pl.ds(base, rc)
            gpad[pl.ds(base + PAD_SHORT, rc), :] = g_ref[rows, :].astype(F32)
            vpad[pl.ds(base + PAD_SHORT, rc), :] = v_ref[rows, :].astype(F32)
            gc = _conv_taps(gpad[pl.ds(base, rc + PAD_SHORT), :], wg_ref, FFN_CONV, PAD_SHORT, rc)
            vc = _conv_taps(vpad[pl.ds(base, rc + PAD_SHORT), :], wv_ref, FFN_CONV, PAD_SHORT, rc)
            o_ref[rows, :] = (gc * _sigmoid(gc) * vc).astype(BF16)
            return carry

        lax.fori_loop(0, s // rc, chunk, 0)

    return _pcall(
        body, name=name, out_shape=_sds((s, f), BF16), grid=(nb,),
        in_specs=[pl.BlockSpec((s, tc), lambda j: (0, j)), pl.BlockSpec((s, tc), lambda j: (0, j + nb)),
                  pl.BlockSpec((FFN_CONV, tc), lambda j: (0, j)), pl.BlockSpec((FFN_CONV, tc), lambda j: (0, j + nb))],
        out_specs=pl.BlockSpec((s, tc), lambda j: (0, j)),
        scratch_shapes=[pltpu.VMEM((s + PAD_SHORT, tc), F32), pltpu.VMEM((s + PAD_SHORT, tc), F32)],
    )(up, up, wf, wf)


def _ffn_act_bwd(up, dact, wf, name):
    s, f2 = up.shape
    f = f2 // 2
    tc = 256
    nb = f // tc
    rc = ROW_CHUNK

    def body(g_ref, v_ref, da_ref, wg_ref, wv_ref, act_ref, dg_ref, dv_ref, dwg_ref, dwv_ref, gpad, vpad, dgp, dvp):
        gpad[pl.ds(0, PAD_SHORT), :] = jnp.zeros((PAD_SHORT, tc), F32)
        vpad[pl.ds(0, PAD_SHORT), :] = jnp.zeros((PAD_SHORT, tc), F32)
        dgp[pl.ds(s, PAD_SHORT), :] = jnp.zeros((PAD_SHORT, tc), F32)
        dvp[pl.ds(s, PAD_SHORT), :] = jnp.zeros((PAD_SHORT, tc), F32)
        dwg_ref[...] = jnp.zeros((FFN_CONV, tc), F32)
        dwv_ref[...] = jnp.zeros((FFN_CONV, tc), F32)

        def first(i, carry):
            base = pl.multiple_of(i * rc, rc)
            rows = pl.ds(base, rc)
            gpad[pl.ds(base + PAD_SHORT, rc), :] = g_ref[rows, :].astype(F32)
            vpad[pl.ds(base + PAD_SHORT, rc), :] = v_ref[rows, :].astype(F32)
            win_g = gpad[pl.ds(base, rc + PAD_SHORT), :]
            win_v = vpad[pl.ds(base, rc + PAD_SHORT), :]
            gc = _conv_taps(win_g, wg_ref, FFN_CONV, PAD_SHORT, rc)
            vc = _conv_taps(win_v, wv_ref, FFN_CONV, PAD_SHORT, rc)
            sg = _sigmoid(gc)
            silu = gc * sg
            act_ref[rows, :] = (silu * vc).astype(BF16)
            da = da_ref[rows, :].astype(F32)
            dgc = da * vc * (sg * (1.0 + gc * (1.0 - sg)))
            dvc = da * silu
            dgp[rows, :] = dgc
            dvp[rows, :] = dvc
            _conv_wgrad(dwg_ref, dgc, win_g, FFN_CONV, PAD_SHORT, rc)
            _conv_wgrad(dwv_ref, dvc, win_v, FFN_CONV, PAD_SHORT, rc)
            return carry

        lax.fori_loop(0, s // rc, first, 0)

        def second(i, carry):
            base = pl.multiple_of(i * rc, rc)
            rows = pl.ds(base, rc)
            dg_ref[rows, :] = _conv_taps_t(dgp[pl.ds(base, rc + PAD_SHORT), :], wg_ref, FFN_CONV, rc).astype(BF16)
            dv_ref[rows, :] = _conv_taps_t(dvp[pl.ds(base, rc + PAD_SHORT), :], wv_ref, FFN_CONV, rc).astype(BF16)
            return carry

        lax.fori_loop(0, s // rc, second, 0)

    lo = pl.BlockSpec((s, tc), lambda j: (0, j))
    hi = pl.BlockSpec((s, tc), lambda j: (0, j + nb))
    wlo = pl.BlockSpec((FFN_CONV, tc), lambda j: (0, j))
    whi = pl.BlockSpec((FFN_CONV, tc), lambda j: (0, j + nb))
    act, dgate, dval, dwg, dwv = _pcall(
        body, name=name,
        out_shape=(_sds((s, f), BF16), _sds((s, f), BF16), _sds((s, f), BF16), _sds((FFN_CONV, f), F32), _sds((FFN_CONV, f), F32)),
        grid=(nb,), in_specs=[lo, hi, lo, wlo, whi], out_specs=(lo, lo, lo, wlo, wlo),
        scratch_shapes=[pltpu.VMEM((s + PAD_SHORT, tc), F32) for _ in range(4)],
    )(up, up, dact, wf, wf)
    return act, dgate, dval, jnp.concatenate([dwg, dwv], axis=1)


def _y_assemble(yan, yb, ycn, gb, name):
    s, w = yan.shape
    wb = yb.shape[1]
    tm = ROW_CHUNK

    def body(ya_ref, yb_ref, yc_ref, g_ref, o_ref):
        ybv = yb_ref[...]
        o_ref[:, 0:w] = ya_ref[...]
        o_ref[:, w:w + wb] = (ybv * _rstd(ybv) * g_ref[...]).astype(BF16)
        o_ref[:, w + wb:] = yc_ref[...]

    return _pcall(body, name=name, out_shape=_sds((s, 2 * w + wb), BF16), grid=(s // tm,),
                  in_specs=[pl.BlockSpec((tm, w), lambda i: (i, 0)), pl.BlockSpec((tm, wb), lambda i: (i, 0)),
                            pl.BlockSpec((tm, w), lambda i: (i, 0)), pl.BlockSpec((1, wb), lambda i: (0, 0))],
                  out_specs=pl.BlockSpec((tm, 2 * w + wb), lambda i: (i, 0)))(yan, yb, ycn, gb)


def _yb_norm_bwd(yb, dy, gb, name):
    s, wb = yb.shape
    w = wb // 2
    heads = wb // D_HEAD
    tm = ROW_CHUNK

    def body(yb_ref, d1_ref, d2_ref, g_ref, dyb_ref, dl_ref, dg_ref):
        i = pl.program_id(0)
        y = yb_ref[...]
        dyn = jnp.concatenate([d1_ref[...], d2_ref[...]], axis=1)
        r = _rstd(y)
        yh = y * r
        gd = dyn * g_ref[...]
        dyb = r * (gd - yh * jnp.mean(gd * yh, axis=-1, keepdims=True))
        dyb_ref[...] = dyb
        part = jnp.sum(dyn * yh, axis=0, keepdims=True)
        prod = dyb * y
        even = lax.broadcasted_iota(I32, (tm, LANES), 1) < D_HEAD
        for p in range(heads // 2):
            blk = prod[:, p * LANES:(p + 1) * LANES]
            ev = jnp.sum(jnp.where(even, blk, 0.0), axis=1, keepdims=True)
            od = jnp.sum(jnp.where(even, 0.0, blk), axis=1, keepdims=True)
            dl_ref[2 * p] = jnp.broadcast_to(ev, (tm, LANES))
            dl_ref[2 * p + 1] = jnp.broadcast_to(od, (tm, LANES))

        @pl.when(i == 0)
        def _():
            dg_ref[...] = part

        @pl.when(i > 0)
        def _():
            dg_ref[...] += part

    return _pcall(
        body, name=name, out_shape=(_sds((s, wb), F32), _sds((heads, s, LANES), F32), _sds((1, wb), F32)),
        grid=(s // tm,),
        in_specs=[pl.BlockSpec((tm, wb), lambda i: (i, 0)), pl.BlockSpec((tm, w), lambda i: (i, 1)),
                  pl.BlockSpec((tm, w), lambda i: (i, 2)), pl.BlockSpec((1, wb), lambda i: (0, 0))],
        out_specs=(pl.BlockSpec((tm, wb), lambda i: (i, 0)), pl.BlockSpec((heads, tm, LANES), lambda i: (0, i, 0)),
                   pl.BlockSpec((1, wb), lambda i: (0, 0))),
    )(yb, dy, dy, gb)


def _t5_bucket_table():
    max_exact = NUM_BUCKETS // 2
    out = np.full((len(DILATED_BRANCHES), BLK, 2 * BLK), -1, np.int32)
    rel = np.arange(BLK)[:, None] - np.arange(2 * BLK)[None, :] + BLK
    for b, (window, dilation) in enumerate(DILATED_BRANCHES):
        n_keys = window // dilation
        dist = np.maximum(rel, 0) * dilation
        d_f = np.maximum(dist, 1).astype(np.float32)
        large = max_exact + (np.log(d_f / np.float32(max_exact)) / np.float32(math.log(MAX_DISTANCE / max_exact))
                             * np.float32(NUM_BUCKETS - max_exact)).astype(np.int32)
        large = np.minimum(large, NUM_BUCKETS - 1)
        bucket = np.where(dist < max_exact, dist, large)
        out[b] = np.where((rel >= 0) & (rel <= n_keys), bucket, -1)
    return out


def _bias_tiles(rel_bias, buckets, name):
    nbk, heads = rel_bias.shape
    nbr = buckets.shape[0]

    def body(rb_ref, bk_ref, o_ref):
        for br in range(nbr):
            bk = bk_ref[br]
            tiles = [jnp.full((BLK, 2 * BLK), NEG, F32) for _ in range(heads)]
            for b in range(nbk):
                hit = bk == b
                tiles = [jnp.where(hit, rb_ref[b, h], tiles[h]) for h in range(heads)]
            for h in range(heads):
                o_ref[br, h] = tiles[h]

    return _pcall(body, name=name, out_shape=_sds((nbr, heads, BLK, 2 * BLK), F32), grid=(1,),
                  in_specs=[pl.BlockSpec(memory_space=pltpu.SMEM), pl.BlockSpec(buckets.shape, lambda i: (0, 0, 0))],
                  out_specs=pl.BlockSpec((nbr, heads, BLK, 2 * BLK), lambda i: (0, 0, 0, 0)))(rel_bias, buckets)


def _bias_grad(dtiles, buckets, nbk, name):
    nbr, heads = dtiles.shape[:2]

    def body(dt_ref, bk_ref, o_ref):
        row = lax.broadcasted_iota(I32, (nbk, LANES), 0)
        col = lax.broadcasted_iota(I32, (nbk, LANES), 1)
        out = jnp.zeros((nbk, LANES), F32)
        for h in range(heads):
            for b in range(nbk):
                tot = jnp.zeros((), F32)
                for br in range(nbr):
                    tot = tot + jnp.sum(jnp.where(bk_ref[br] == b, dt_ref[br, h], 0.0))
                out = jnp.where((row == b) & (col == h), tot, out)
        o_ref[...] = out

    return _pcall(body, name=name, out_shape=_sds((nbk, LANES), F32), grid=(1,),
                  in_specs=[pl.BlockSpec(dtiles.shape, lambda i: (0, 0, 0, 0)), pl.BlockSpec(buckets.shape, lambda i: (0, 0, 0))],
                  out_specs=pl.BlockSpec((nbk, LANES), lambda i: (0, 0)))(dtiles, buckets)


def _attn_blocks(s, visit):
    for br, (window, d) in enumerate(DILATED_BRANCHES):
        n_blk = (s // d) // BLK
        span = BLK * d

        def firsts(r, carry, br=br, d=d):
            visit(br, d, r, False)
            return carry

        lax.fori_loop(0, d, firsts, 0)
        if n_blk > 1:
            def rest(idx, carry, br=br, d=d, n_blk=n_blk, span=span):
                r = idx // (n_blk - 1)
                n = 1 + idx % (n_blk - 1)
                visit(br, d, r + n * span, True)
                return carry

            lax.fori_loop(0, d * (n_blk - 1), rest, 0)


def _rows(start, size, d):
    return pl.ds(pl.multiple_of(start, BLK), size) if d == 1 else pl.ds(start, size, stride=d)


def _attn_fwd(z, btiles, col0, name):
    s = z.shape[0]
    nbr, heads = btiles.shape[:2]
    pairs = heads // 2
    scale = D_HEAD ** -0.5
    rc = ROW_CHUNK

    def body(q_ref, k_ref, v_ref, bt_ref, yb_ref, lse_ref, acc_ref, m_ref, l_ref):
        even = lax.broadcasted_iota(I32, (BLK, LANES), 1) < D_HEAD
        even2 = lax.broadcasted_iota(I32, (2 * BLK, LANES), 1) < D_HEAD

        def visit(br, d, start, prev):
            kw = 2 * BLK if prev else BLK
            rows_q = _rows(start, BLK, d)
            rows_k = _rows(start - BLK * d, kw, d) if prev else rows_q
            qb = q_ref[rows_q, :]
            kb = k_ref[rows_k, :].astype(BF16)
            vw = v_ref[rows_k, :]
            ev_k = even2 if prev else even
            acc = jnp.zeros((BLK, LANES), F32)
            for e in range(2):
                sel = even if e == 0 else ~even
                sel_k = ev_k if e == 0 else ~ev_k
                qm = jnp.where(sel, qb, 0.0).astype(BF16)
                bias = bt_ref[br, e] if prev else bt_ref[br, e, :, BLK:]
                sc = _dot_nt(qm, kb) * scale + bias
                m = jnp.max(sc, axis=1, keepdims=True)
                p = jnp.exp(sc - m)
                l = jnp.sum(p, axis=1, keepdims=True)
                vm = jnp.where(sel_k, vw, 0.0).astype(BF16)
                acc = acc + _dot(p.astype(BF16), vm)
                m_ref.at[br, e][rows_q, :] = jnp.broadcast_to(m, (BLK, LANES))
                l_ref.at[br, e][rows_q, :] = jnp.broadcast_to(l, (BLK, LANES))
            acc_ref.at[br][rows_q, :] = acc

        _attn_blocks(s, visit)

        ev_c = lax.broadcasted_iota(I32, (rc, LANES), 1) < D_HEAD

        def merge(i, carry):
            rows = pl.ds(pl.multiple_of(i * rc, rc), rc)
            wts, dens = [], []
            for e in range(2):
                ms = [m_ref[br, e, rows, :] for br in range(nbr)]
                top = functools.reduce(jnp.maximum, ms)
                w = [jnp.exp(mb - top) for mb in ms]
                den = functools.reduce(lambda a, b: a + b, [w[br] * l_ref[br, e, rows, :] for br in range(nbr)])
                lse_ref[e, rows, :] = top + jnp.log(den)
                wts.append(w)
                dens.append(den)
            num = functools.reduce(lambda a, b: a + b,
                                   [jnp.where(ev_c, wts[0][br], wts[1][br]) * acc_ref[br, rows, :] for br in range(nbr)])
            yb_ref[rows, :] = num / jnp.where(ev_c, dens[0], dens[1])
            return carry

        lax.fori_loop(0, s // rc, merge, 0)

    def zcol(j):
        return pl.BlockSpec((s, LANES), lambda p, j=j: (0, col0 + j + p))

    return _pcall(
        body, name=name, out_shape=(_sds((s, pairs * LANES), F32), _sds((heads, s, LANES), F32)), grid=(pairs,),
        in_specs=[zcol(0), zcol(pairs), zcol(2 * pairs), pl.BlockSpec((nbr, 2, BLK, 2 * BLK), lambda p: (0, p, 0, 0))],
        out_specs=(pl.BlockSpec((s, LANES), lambda p: (0, p)), pl.BlockSpec((2, s, LANES), lambda p: (p, 0, 0))),
        scratch_shapes=[pltpu.VMEM((nbr, s, LANES), F32), pltpu.VMEM((nbr, 2, s, LANES), F32), pltpu.VMEM((nbr, 2, s, LANES), F32)],
    )(z, z, z, btiles)


def _attn_bwd(z, btiles, dyb, lse, delta, dbias_in, col0, name):
    s = z.shape[0]
    nbr, heads = btiles.shape[:2]
    pairs = heads // 2
    scale = D_HEAD ** -0.5

    def body(q_ref, k_ref, v_ref, bt_ref, dy_ref, lse_ref, dl_ref, dbi_ref,
             dq_ref, dk_ref, dv_ref, db_ref, dqa, dka, dva):
        even = lax.broadcasted_iota(I32, (BLK, LANES), 1) < D_HEAD
        even2 = lax.broadcasted_iota(I32, (2 * BLK, LANES), 1) < D_HEAD
        for ref in (dqa, dka, dva):
            ref[...] = jnp.zeros((s, LANES), F32)
        db_ref[...] = dbi_ref[...]

        def visit(br, d, start, prev):
            kw = 2 * BLK if prev else BLK
            rows_q = _rows(start, BLK, d)
            rows_k = _rows(start - BLK * d, kw, d) if prev else rows_q
            qb = q_ref[rows_q, :]
            dyv = dy_ref[rows_q, :]
            kwin = k_ref[rows_k, :]
            kb = kwin.astype(BF16)
            vb = v_ref[rows_k, :].astype(BF16)
            ev_k = even2 if prev else even
            dq = jnp.zeros((BLK, LANES), F32)
            dk = jnp.zeros((kw, LANES), F32)
            dv = jnp.zeros((kw, LANES), F32)
            for e in range(2):
                sel = even if e == 0 else ~even
                sel_k = ev_k if e == 0 else ~ev_k
                qm = jnp.where(sel, qb, 0.0).astype(BF16)
                dym = jnp.where(sel, dyv, 0.0).astype(BF16)
                bias = bt_ref[br, e] if prev else bt_ref[br, e, :, BLK:]
                sc = _dot_nt(qm, kb) * scale + bias
                lt = lse_ref.at[e][rows_q, :]
                dt = dl_ref.at[e][rows_q, :]
                if prev:
                    lt = jnp.concatenate([lt, lt], axis=1)
                    dt = jnp.concatenate([dt, dt], axis=1)
                p = jnp.exp(sc - lt)
                ds = p * (_dot_nt(dym, vb) - dt)
                if prev:
                    db_ref[br, e] += ds
                else:
                    db_ref[br, e, :, BLK:] += ds
                dsb = ds.astype(BF16)
                km = jnp.where(sel_k, kwin, 0.0).astype(BF16)
                dq = dq + _dot(dsb, km)
                dk = dk + _dot_tn(dsb, qm)
                dv = dv + _dot_tn(p.astype(BF16), dym)
            dqa[rows_q, :] += dq * scale
            dka[rows_k, :] += dk * scale
            dva[rows_k, :] += dv

        _attn_blocks(s, visit)
        dq_ref[...] = dqa[...].astype(BF16)
        dk_ref[...] = dka[...].astype(BF16)
        dv_ref[...] = dva[...].astype(BF16)

    def zcol(j):
        return pl.BlockSpec((s, LANES), lambda p, j=j: (0, col0 + j + p))

    col = pl.BlockSpec((s, LANES), lambda p: (0, p))
    stat = pl.BlockSpec((2, s, LANES), lambda p: (p, 0, 0))
    tile = pl.BlockSpec((nbr, 2, BLK, 2 * BLK), lambda p: (0, p, 0, 0))
    wide = _sds((s, pairs * LANES), BF16)
    return _pcall(
        body, name=name, out_shape=(wide, wide, wide, _sds(btiles.shape, F32)), grid=(pairs,),
        in_specs=[zcol(0), zcol(pairs), zcol(2 * pairs), tile, col, stat, stat, tile],
        out_specs=(col, col, col, tile),
        scratch_shapes=[pltpu.VMEM((s, LANES), F32) for _ in range(3)],
    )(z, z, z, btiles, dyb, lse, delta, dbias_in)


def _row(v):
    return v.reshape(1, -1)


def _layer_fwd(l, x, wts, prm, btiles):
    d = x.shape[1]
    wq = d // 4
    gout = prm["out_norm_g"][l]
    h = _rms_fwd(x, _row(prm["norm_mix_g"][l]), "rms_mix_fwd")
    z = _mm_n(h, wts["in_t"][l], nt=True, tn=256, out_dtype=F32, name="in_proj")
    yan, ycn = _mix_fwd(z, prm["conv_a_w"][l], prm["conv_c_w"][l], _row(prm["conv_c_b"][l]), _row(prm["ln_c_g"][l]),
                        _row(prm["ln_c_b"][l]), _row(gout[:wq]), _row(gout[3 * wq:]), "mix_fwd")
    yb, lse = _attn_fwd(z, btiles, 3 * wq // LANES, "attn_fwd")
    y = _y_assemble(yan, yb, ycn, _row(gout[wq:3 * wq]), "y_assemble")
    x_mid = _mm_k(y, wts["out"][l], x, nt=False, tk=512, b_off=0, name="out_proj")
    h2 = _rms_fwd(x_mid, _row(prm["norm_ffn_g"][l]), "rms_ffn_fwd")
    up = _mm_n(h2, wts["up"][l], nt=False, tn=512, out_dtype=BF16, name="up_proj")
    act = _ffn_act_fwd(up, prm["conv_f_w"][l], "ffn_act_fwd")
    x_out = _mm_k(act, wts["down"][l], x_mid, nt=False, tk=256, b_off=0, name="down_proj")
    return x_out, (x, h, z, yb, lse, y, x_mid, h2, up)


def _layer_bwd(l, dxo, dxo_b, saved, wts, prm, btiles, dbias):
    x, h, z, yb, lse, y, x_mid, h2, up = saved
    d = x.shape[1]
    wq = d // 4
    f = up.shape[1] // 2
    gout = prm["out_norm_g"][l]
    dact = _mm_n(dxo_b, wts["down"][l], nt=True, tn=256, out_dtype=BF16, name="down_proj_dx")
    act, dgate, dval, dwf = _ffn_act_bwd(up, dact, prm["conv_f_w"][l], "ffn_act_bwd")
    g_down = _mm_tn(act, dxo_b, t=256, name="down_proj_dw")
    dh2 = _mm_k(dgate, wts["up"][l], None, nt=True, tk=256, b_off=0, name="up_proj_dx_gate")
    dh2 = _mm_k(dval, wts["up"][l], dh2, nt=True, tk=256, b_off=f // 256, name="up_proj_dx_val")
    dxm, dxm_b, dg_ffn = _rms_bwd(x_mid, _row(prm["norm_ffn_g"][l]), dh2, dxo, "rms_ffn_bwd")
    g_up = _mm_tn2(h2, dgate, dval, t=256, name="up_proj_dw")
    dy = _mm_k(dxm_b, wts["out"][l], None, nt=True, tk=512, b_off=0, name="out_proj_dx")
    g_out = _mm_tn(y, dxm_b, t=256, name="out_proj_dw")
    dza, dzc, dwa, dwc, dcb, dlg, dlb, dga, dgc = _mix_bwd(
        z, dy, prm["conv_a_w"][l], prm["conv_c_w"][l], _row(prm["conv_c_b"][l]), _row(prm["ln_c_g"][l]),
        _row(prm["ln_c_b"][l]), _row(gout[:wq]), _row(gout[3 * wq:]), "mix_bwd")
    dyb, delta, dgb = _yb_norm_bwd(yb, dy, _row(gout[wq:3 * wq]), "yb_norm_bwd")
    dq, dk, dv, dbias = _attn_bwd(z, btiles, dyb, lse, delta, dbias, 3 * wq // LANES, "attn_bwd")
    dz = jnp.concatenate([dza, dq, dk, dv, dzc], axis=1)
    dh = _mm_k(dz, wts["in_t"][l], None, nt=False, tk=256, b_off=0, name="in_proj_dx")
    dx, dx_b, dg_mix = _rms_bwd(x, _row(prm["norm_mix_g"][l]), dh, dxm, "rms_mix_bwd")
    g_in_t = _mm_tn(dz, h, t=256, name="in_proj_dw")
    big = {"in_t": g_in_t, "out": g_out, "up": g_up, "down": g_down}
    small = {"norm_mix_g": dg_mix[0], "conv_a_w": dwa, "conv_c_w": dwc, "conv_c_b": dcb[0], "ln_c_g": dlg[0],
             "ln_c_b": dlb[0], "out_norm_g": jnp.concatenate([dga[0], dgb[0], dgc[0]]), "norm_ffn_g": dg_ffn[0],
             "conv_f_w": dwf}
    return dx, dx_b, big, small, dbias


def _local_step(x, tgt, wts, prm, reduce_big=lambda l, big, carry: big, carry=None):
    depth = wts["out"].shape[0]
    buckets = jnp.asarray(_t5_bucket_table())
    btiles = _bias_tiles(prm["rel_bias"], buckets, "bias_tiles")
    saved = []
    for l in range(depth):
        x, sv = _layer_fwd(l, x, wts, prm, btiles)
        saved.append(sv)
    loss, dx, dx_b, dg_final = _final_loss(x, _row(prm["final_g"]), tgt, "final_loss")
    dbias = jnp.zeros(btiles.shape, F32)
    big, small = [None] * depth, [None] * depth
    for l in reversed(range(depth)):
        dx, dx_b, grads, small[l], dbias = _layer_bwd(l, dx, dx_b, saved[l], wts, prm, btiles, dbias)
        carry = big[l] = reduce_big(l, grads, carry)
    nbk, heads = prm["rel_bias"].shape
    d_rel = _bias_grad(dbias, buckets, nbk, "bias_grad")[:, :heads]
    return loss, dx, big, small, d_rel, dg_final[0]


BIG = ("in_t", "out", "up", "down")
COL_SHARDED = ("up",)
N_CHIPS = 4
N_DEV = 8
BF16_ROWS = 16


def _me():
    return lax.axis_index("x"), lax.axis_index("y"), lax.axis_index("c")


def _chip_of(x, y):
    return 2 * x + y


def _other_chips(x, y):
    return ((1 - x, y), (x, 1 - y), (1 - x, 1 - y))


def _remote(src, dst, send_sem, recv_sem, device):
    return pltpu.make_async_remote_copy(src_ref=src, dst_ref=dst, send_sem=send_sem, recv_sem=recv_sem,
                                        device_id=device, device_id_type=MESH)


def _ag_weights(shards, small_sh):
    geo = {k: shards[k].shape for k in BIG}

    def gathered_shape(k):
        l, r, cc = geo[k]
        return (l, r, N_CHIPS * cc) if k in COL_SHARDED else (l, N_CHIPS * r, cc)

    nw = len(BIG)

    def body(*refs):
        s_refs = dict(zip(BIG, refs[:nw]))
        sm_ref = refs[nw]
        g_refs = dict(zip(BIG, refs[nw + 1:2 * nw + 1]))
        gsm_ref = refs[2 * nw + 1]
        isend, irecv, dsend, drecv, lsem, ssend, srecv = refs[2 * nw + 2:]
        x, y, c = _me()
        mine = _chip_of(x, y)
        others = _other_chips(x, y)

        def region(k, chip, half):
            l, r, cc = geo[k]
            h = r // 2
            if k in COL_SHARDED:
                rows = pl.ds(0, r) if half is None else pl.ds(pl.multiple_of(half * h, BF16_ROWS), h)
                return g_refs[k].at[:, rows, pl.ds(pl.multiple_of(chip * cc, LANES), cc)]
            if half is None:
                return g_refs[k].at[:, pl.ds(pl.multiple_of(chip * r, BF16_ROWS), r), :]
            return g_refs[k].at[:, pl.ds(pl.multiple_of(chip * r + half * h, BF16_ROWS), h), :]

        def own_half(k, half):
            h = geo[k][1] // 2
            return s_refs[k].at[:, pl.ds(pl.multiple_of(half * h, BF16_ROWS), h), :]

        local = [pltpu.make_async_copy(s_refs[k], region(k, mine, None), lsem.at[i]) for i, k in enumerate(BIG)]
        local.append(pltpu.make_async_copy(sm_ref, gsm_ref.at[mine], lsem.at[nw]))
        for cp in local:
            cp.start()
        sends = []
        for i, k in enumerate(BIG):
            for f, chip in enumerate(others):
                cp = _remote(own_half(k, c), region(k, mine, c), isend.at[i, f], irecv.at[i, f], (*chip, c))
                cp.start()
                sends.append(cp)
        for f, chip in enumerate(others):
            cp = _remote(sm_ref, gsm_ref.at[mine], ssend.at[f], srecv.at[f], (*chip, c))
            cp.start()
            sends.append(cp)
        for i, k in enumerate(BIG):
            for f, chip in enumerate(others):
                landed = region(k, _chip_of(*chip), c)
                _remote(own_half(k, c), landed, isend.at[i, f], irecv.at[i, f], (*chip, c)).wait_recv()
                cp = _remote(landed, landed, dsend.at[i, f], drecv.at[i, f], (x, y, 1 - c))
                cp.start()
                sends.append(cp)
        for f, chip in enumerate(others):
            _remote(sm_ref, gsm_ref.at[_chip_of(*chip)], ssend.at[f], srecv.at[f], (*chip, c)).wait_recv()
        for i, k in enumerate(BIG):
            for f, chip in enumerate(others):
                theirs = region(k, _chip_of(*chip), 1 - c)
                _remote(theirs, theirs, dsend.at[i, f], drecv.at[i, f], (x, y, 1 - c)).wait_recv()
        for cp in sends:
            cp.wait_send()
        for cp in local:
            cp.wait()

    ins = [shards[k] for k in BIG] + [small_sh]
    outs = [_sds(gathered_shape(k), BF16) for k in BIG] + [_sds((N_CHIPS,) + small_sh.shape, F32)]
    res = _pcall(
        body, name="ag_weights", out_shape=tuple(outs), in_specs=[ANY] * len(ins), out_specs=tuple([ANY] * len(outs)),
        scratch_shapes=[pltpu.SemaphoreType.DMA((nw, 3)) for _ in range(4)] + [pltpu.SemaphoreType.DMA((nw + 1,))]
        + [pltpu.SemaphoreType.DMA((3,)) for _ in range(2)],
    )(*ins)
    return dict(zip(BIG, res[:nw])), res[nw]


def _piece_geo(g):
    geo = {}
    for k in BIG:
        rows, cols = g[k].shape
        geo[k] = (rows // 2, cols // N_CHIPS) if k in COL_SHARDED else (rows // (2 * N_CHIPS), cols)
    return geo


def _rs_swap_halves(g):
    geo = _piece_geo(g)
    n_copies = sum(N_CHIPS if k in COL_SHARDED else 1 for k in BIG)

    def body(*refs):
        g_refs = dict(zip(BIG, refs[:len(BIG)]))
        t_refs = dict(zip(BIG, refs[len(BIG):2 * len(BIG)]))
        send, recv = refs[2 * len(BIG):]
        x, y, c = _me()
        copies = []
        for k in BIG:
            h, cc = geo[k]
            if k in COL_SHARDED:
                rows = pl.ds(pl.multiple_of((1 - c) * h, BF16_ROWS), h)
                for j in range(N_CHIPS):
                    copies.append((g_refs[k].at[rows, pl.ds(j * cc, cc)], t_refs[k].at[j]))
            else:
                copies.append((g_refs[k].at[:, 1 - c], t_refs[k]))
        started = []
        for i, (src, dst) in enumerate(copies):
            cp = _remote(src, dst, send.at[i], recv.at[i], (x, y, 1 - c))
            cp.start()
            started.append(cp)
        for cp in started:
            cp.wait()

    ins = [g[k] if k in COL_SHARDED else g[k].reshape(N_CHIPS, 2, geo[k][0], geo[k][1]) for k in BIG]
    outs = tuple(_sds((N_CHIPS,) + geo[k], BF16) for k in BIG)
    res = _pcall(body, name="rs_swap_halves", out_shape=outs, in_specs=[ANY] * len(ins), out_specs=tuple([ANY] * len(outs)),
                 scratch_shapes=[pltpu.SemaphoreType.DMA((n_copies,)) for _ in range(2)])(*ins)
    return dict(zip(BIG, res))


def _pair_sum(g, theirs, c_arr):
    geo = _piece_geo(g)

    def body(c_ref, *refs):
        nk = len(BIG)
        for i in range(nk):
            refs[2 * nk + i][...] = (refs[i][...].astype(F32) + refs[nk + i][...].astype(F32)).astype(BF16)

    in_specs, ins = [], []
    for k in BIG:
        h, cc = geo[k]
        if k in COL_SHARDED:
            in_specs.append(pl.BlockSpec((h, cc), lambda j, c_ref: (c_ref[0], j)))
            ins.append(g[k])
        else:
            in_specs.append(pl.BlockSpec((None, h, cc), lambda j, c_ref: (2 * j + c_ref[0], 0, 0)))
            ins.append(g[k].reshape(2 * N_CHIPS, h, cc))
    slab = [pl.BlockSpec((None,) + geo[k], lambda j, c_ref: (j, 0, 0)) for k in BIG]
    res = _pcall(body, name="rs_pair_sum", out_shape=tuple(_sds((N_CHIPS,) + geo[k], BF16) for k in BIG), grid=(N_CHIPS,),
                 in_specs=in_specs + slab, out_specs=tuple(slab), prefetch=1)(c_arr, *ins, *[theirs[k] for k in BIG])
    return dict(zip(BIG, res))


def _rs_to_owners(p):
    def body(*refs):
        nk = len(BIG)
        p_refs, b_refs = refs[:nk], refs[nk:2 * nk]
        send, recv = refs[2 * nk:]
        x, y, c = _me()
        started = []
        for i in range(nk):
            for f, chip in enumerate(_other_chips(x, y)):
                cp = _remote(p_refs[i].at[_chip_of(*chip)], b_refs[i].at[f], send.at[i, f], recv.at[i, f], (*chip, c))
                cp.start()
                started.append(cp)
        for cp in started:
            cp.wait()

    outs = tuple(_sds((3,) + p[k].shape[1:], BF16) for k in BIG)
    res = _pcall(body, name="rs_to_owners", out_shape=outs, in_specs=[ANY] * len(BIG), out_specs=tuple([ANY] * len(BIG)),
                 scratch_shapes=[pltpu.SemaphoreType.DMA((len(BIG), 3)) for _ in range(2)])(*[p[k] for k in BIG])
    return dict(zip(BIG, res))


def _quad_sum(p, b, chip_arr):
    parts = 2

    def body(chip_ref, *refs):
        nk = len(BIG)
        for i in range(nk):
            acc = refs[i][...].astype(F32)
            for f in range(3):
                acc = acc + refs[nk + 3 * i + f][...].astype(F32)
            refs[4 * nk + i][...] = acc

    own, recv, outs, shapes = [], [], [], []
    for k in BIG:
        h, cc = p[k].shape[1:]
        th = h // parts
        own.append(pl.BlockSpec((None, th, cc), lambda i, chip_ref: (chip_ref[0], i, 0)))
        recv += [pl.BlockSpec((None, th, cc), lambda i, chip_ref, f=f: (f, i, 0)) for f in range(3)]
        outs.append(pl.BlockSpec((th, cc), lambda i, chip_ref: (i, 0)))
        shapes.append(_sds((h, cc), F32))
    args = [p[k] for k in BIG] + [b[k] for k in BIG for _ in range(3)]
    res = _pcall(body, name="rs_quad_sum", out_shape=tuple(shapes), grid=(parts,), in_specs=own + recv,
                 out_specs=tuple(outs), prefetch=1)(chip_arr, *args)
    return dict(zip(BIG, res))


def _rs_share(l, halves, full):
    def body(*refs):
        nk = len(BIG)
        h_refs, f_refs = refs[:nk], refs[2 * nk:3 * nk]
        send, recv, lsem = refs[3 * nk:]
        x, y, c = _me()
        started = []
        for i in range(nk):
            cp = pltpu.make_async_copy(h_refs[i], f_refs[i].at[l, c], lsem.at[i])
            cp.start()
            started.append(cp)
        for i in range(nk):
            cp = _remote(h_refs[i], f_refs[i].at[l, c], send.at[i], recv.at[i], (x, y, 1 - c))
            cp.start()
            _remote(h_refs[i], f_refs[i].at[l, 1 - c], send.at[i], recv.at[i], (x, y, 1 - c)).wait_recv()
            cp.wait_send()
        for cp in started:
            cp.wait()

    nk = len(BIG)
    res = _pcall(body, name="rs_share", out_shape=tuple(_sds(full[k].shape, F32) for k in BIG),
                 in_specs=[ANY] * (2 * nk), out_specs=tuple([ANY] * nk), aliases={nk + i: i for i in range(nk)},
                 scratch_shapes=[pltpu.SemaphoreType.DMA((nk,)) for _ in range(3)],
                 )(*[halves[k] for k in BIG], *[full[k] for k in BIG])
    return dict(zip(BIG, res))


def _gather_partials(slab):
    def body(s_ref, o_ref, send, recv, lsem):
        x, y, c = _me()
        me = 4 * x + 2 * y + c
        own = pltpu.make_async_copy(s_ref, o_ref.at[me], lsem)
        own.start()
        started = []
        peers = []
        for mask in range(1, N_DEV):
            peer = (x ^ (mask >> 2), y ^ ((mask >> 1) & 1), c ^ (mask & 1))
            peers.append(peer)
            cp = _remote(s_ref, o_ref.at[me], send.at[mask - 1], recv.at[mask - 1], peer)
            cp.start()
            started.append(cp)
        for i, peer in enumerate(peers):
            _remote(s_ref, o_ref.at[4 * peer[0] + 2 * peer[1] + peer[2]], send.at[i], recv.at[i], peer).wait_recv()
        for cp in started:
            cp.wait_send()
        own.wait()

    return _pcall(body, name="gather_partials", out_shape=_sds((N_DEV,) + slab.shape, F32), in_specs=[ANY], out_specs=ANY,
                  scratch_shapes=[pltpu.SemaphoreType.DMA((N_DEV - 1,)), pltpu.SemaphoreType.DMA((N_DEV - 1,)),
                                  pltpu.SemaphoreType.DMA])(slab)


def _sum_slabs(slabs):
    n, r, lanes = slabs.shape
    tr = r // 2

    def body(s_ref, o_ref):
        acc = s_ref[0]
        for i in range(1, n):
            acc = acc + s_ref[i]
        o_ref[...] = acc

    return _pcall(body, name="sum_partials", out_shape=_sds((r, lanes), F32), grid=(2,),
                  in_specs=[pl.BlockSpec((n, tr, lanes), lambda i: (0, i, 0))],
                  out_specs=pl.BlockSpec((tr, lanes), lambda i: (i, 0)))(slabs)


def _cast_bf16(w, name):
    l, r, c = w.shape

    def body(w_ref, o_ref):
        o_ref[...] = w_ref[...].astype(BF16)

    blk = pl.BlockSpec((None, r, c), lambda i: (i, 0, 0))
    return _pcall(body, name=name, out_shape=_sds(w.shape, BF16), grid=(l,), in_specs=[blk], out_specs=blk)(w)


def _adamw(w, g, m, v, name, tr):
    r, c = w.shape

    def body(w_ref, g_ref, m_ref, v_ref, d_ref, mo_ref, vo_ref):
        gv = g_ref[...]
        mn = ADAM_B1 * m_ref[...] + (1.0 - ADAM_B1) * gv
        vn = ADAM_B2 * v_ref[...] + (1.0 - ADAM_B2) * (gv * gv)
        m_hat = mn / (1.0 - ADAM_B1 ** ADAM_STEP)
        v_hat = vn / (1.0 - ADAM_B2 ** ADAM_STEP)
        d_ref[...] = -ADAM_LR * (m_hat / (jnp.sqrt(v_hat) + ADAM_EPS) + ADAM_WD * w_ref[...])
        mo_ref[...] = mn
        vo_ref[...] = vn

    blk = pl.BlockSpec((tr, c), lambda i: (i, 0))
    return _pcall(body, name=name, out_shape=tuple(_sds((r, c), F32) for _ in range(3)), grid=(r // tr,),
                  in_specs=[blk] * 4, out_specs=(blk, blk, blk))(w, g, m, v)


SHARDED_SMALL = ("conv_a_w", "conv_c_w", "conv_f_w")
SMALL = ("norm_mix_g", "conv_a_w", "conv_c_w", "conv_c_b", "ln_c_g", "ln_c_b", "out_norm_g", "norm_ffn_g",
         "conv_f_w", "rel_bias", "final_g")
SLAB_ROWS = 16


def _pack(arrays):
    flat = jnp.concatenate([a.reshape(-1) for a in arrays])
    unit = SLAB_ROWS * LANES
    total = -(-flat.shape[0] // unit) * unit
    return jnp.pad(flat, (0, total - flat.shape[0])).reshape(-1, LANES)


def _unpack(slab, shapes):
    flat = slab.reshape(-1)
    out, off = [], 0
    for shp in shapes:
        size = math.prod(shp)
        out.append(flat[off:off + size].reshape(shp))
        off += size
    return out


def kernel(x, norm_mix_g, w_in, conv_a_w, conv_c_w, conv_c_b, ln_c_g, ln_c_b, out_norm_g, w_out, norm_ffn_g, w_up, conv_f_w, w_down, rel_bias, final_g, loss_target, m_norm_mix_g, m_w_in, m_conv_a_w, m_conv_c_w, m_conv_c_b, m_ln_c_g, m_ln_c_b, m_out_norm_g, m_w_out, m_norm_ffn_g, m_w_up, m_conv_f_w, m_w_down, m_rel_bias, m_final_g, v_norm_mix_g, v_w_in, v_conv_a_w, v_conv_c_w, v_conv_c_b, v_ln_c_g, v_ln_c_b, v_out_norm_g, v_w_out, v_norm_ffn_g, v_w_up, v_conv_f_w, v_w_down, v_rel_bias, v_final_g):
    weights = dict(norm_mix_g=norm_mix_g, w_in=w_in, conv_a_w=conv_a_w, conv_c_w=conv_c_w, conv_c_b=conv_c_b,
                   ln_c_g=ln_c_g, ln_c_b=ln_c_b, out_norm_g=out_norm_g, w_out=w_out, norm_ffn_g=norm_ffn_g, w_up=w_up,
                   conv_f_w=conv_f_w, w_down=w_down, rel_bias=rel_bias, final_g=final_g)
    mom_m = dict(norm_mix_g=m_norm_mix_g, w_in=m_w_in, conv_a_w=m_conv_a_w, conv_c_w=m_conv_c_w, conv_c_b=m_conv_c_b,
                 ln_c_g=m_ln_c_g, ln_c_b=m_ln_c_b, out_norm_g=m_out_norm_g, w_out=m_w_out, norm_ffn_g=m_norm_ffn_g,
                 w_up=m_w_up, conv_f_w=m_conv_f_w, w_down=m_w_down, rel_bias=m_rel_bias, final_g=m_final_g)
    mom_v = dict(norm_mix_g=v_norm_mix_g, w_in=v_w_in, conv_a_w=v_conv_a_w, conv_c_w=v_conv_c_w, conv_c_b=v_conv_c_b,
                 ln_c_g=v_ln_c_g, ln_c_b=v_ln_c_b, out_norm_g=v_out_norm_g, w_out=v_w_out, norm_ffn_g=v_norm_ffn_g,
                 w_up=v_w_up, conv_f_w=v_conv_f_w, w_down=v_w_down, rel_bias=v_rel_bias, final_g=v_final_g)
    xi, yi, ci = _me()
    chip = _chip_of(xi, yi)
    c_arr = jnp.reshape(ci, (1,)).astype(I32)
    chip_arr = jnp.reshape(chip, (1,)).astype(I32)
    depth = w_out.shape[0]

    shards = {"in_t": _cast_bf16(jnp.swapaxes(w_in, 1, 2), "cast_in"), "out": _cast_bf16(w_out, "cast_out"),
              "up": _cast_bf16(w_up, "cast_up"), "down": _cast_bf16(w_down, "cast_down")}
    wts, small_all = _ag_weights(shards, _pack([weights[n] for n in SHARDED_SMALL]))
    prm = {n: weights[n] for n in SMALL if n not in SHARDED_SMALL}
    per_chip = [_unpack(small_all[j], [weights[n].shape for n in SHARDED_SMALL]) for j in range(N_CHIPS)]
    for i, n in enumerate(SHARDED_SMALL):
        prm[n] = jnp.concatenate([per_chip[j][i] for j in range(N_CHIPS)], axis=-1)

    def reduce_big(l, g, full):
        if full is None:
            geo = _piece_geo(g)
            full = {k: jnp.zeros((depth, 2) + geo[k], F32) for k in BIG}
        pairs = _pair_sum(g, _rs_swap_halves(g), c_arr)
        halves = _quad_sum(pairs, _rs_to_owners(pairs), chip_arr)
        return _rs_share(l, halves, full)

    loss_row, dx, big, small, d_rel, d_final = _local_step(x[0], loss_target[0], wts, prm, reduce_big)
    reduced = big[0]
    loss = lax.psum(loss_row[0, 0], ("x", "y", "c"))

    stacked = {n: jnp.stack([small[l][n] for l in range(depth)]) for n in small[0]}
    stacked["rel_bias"] = d_rel
    stacked["final_g"] = d_final
    full_shapes = [stacked[n].shape for n in SMALL]
    summed = _unpack(_sum_slabs(_gather_partials(_pack([stacked[n] for n in SMALL]))), full_shapes)
    grads = {}
    for n, g in zip(SMALL, summed):
        if n in SHARDED_SMALL:
            width = weights[n].shape[-1]
            g = lax.dynamic_slice_in_dim(g, chip * width, width, axis=g.ndim - 1)
        grads[n] = g

    shard_shapes = {"in_t": jnp.swapaxes(w_in, 1, 2).shape, "out": w_out.shape, "up": w_up.shape, "down": w_down.shape}
    red = {k: reduced[k].reshape(shard_shapes[k]) for k in BIG}
    grads["w_in"] = jnp.swapaxes(red["in_t"], 1, 2)
    grads["w_out"], grads["w_up"], grads["w_down"] = red["out"], red["up"], red["down"]

    delta, new_m, new_v = {}, {}, {}
    for n in ("w_in", "w_out", "w_up", "w_down"):
        shp = weights[n].shape
        flat = lambda a, shp=shp: a.reshape(shp[0] * shp[1], shp[2])
        tile = max(t for t in range(8, 257, 8) if shp[1] % t == 0)
        d, mn, vn = _adamw(flat(weights[n]), flat(grads[n]), flat(mom_m[n]), flat(mom_v[n]), "adamw_" + n, tile)
        delta[n], new_m[n], new_v[n] = d.reshape(shp), mn.reshape(shp), vn.reshape(shp)
    shapes = [weights[n].shape for n in SMALL]
    packed = [_pack([src[n] for n in SMALL]) for src in (weights, grads, mom_m, mom_v)]
    d, mn, vn = _adamw(*packed, "adamw_small", packed[0].shape[0] // 2)
    for n, a, b, c in zip(SMALL, _unpack(d, shapes), _unpack(mn, shapes), _unpack(vn, shapes)):
        delta[n], new_m[n], new_v[n] = a, b, c

    order = ("norm_mix_g", "w_in", "conv_a_w", "conv_c_w", "conv_c_b", "ln_c_g", "ln_c_b", "out_norm_g", "w_out",
             "norm_ffn_g", "w_up", "conv_f_w", "w_down", "rel_bias", "final_g")
    return (loss, dx[None], *[grads[n] for n in order], *[delta[n] for n in order], *[new_m[n] for n in order],
            *[new_v[n] for n in order])
```

```python
import functools
import math

import numpy as np
import jax
import jax.numpy as jnp
from jax import lax
from jax.experimental import pallas as pl
from jax.experimental.pallas import tpu as pltpu

F32 = jnp.float32
BF16 = jnp.bfloat16
I32 = jnp.int32

EPS = 1e-6
NEG = -1e30
D_HEAD = 64
LANES = 128
BLK = 128
DILATED_BRANCHES = ((128, 1), (512, 4), (2048, 16))
NUM_BUCKETS = 32
MAX_DISTANCE = 2048
SHORT_CONV = 3
CONFORMER_CONV = 31
FFN_CONV = 3
PAD_SHORT = 8
PAD_LONG = 32
ROW_CHUNK = 256
V7X_VMEM_BYTES = 64 * 1024 * 1024
VMEM_REQUEST = V7X_VMEM_BYTES * 7 // 8

ADAM_LR = 0.001
ADAM_B1 = 0.9
ADAM_B2 = 0.999
ADAM_EPS = 1e-08
ADAM_WD = 0.01
ADAM_STEP = 10

MESH = pl.DeviceIdType.MESH
ANY = pl.BlockSpec(memory_space=pl.ANY)


def _sds(shape, dtype):
    return jax.ShapeDtypeStruct(tuple(shape), dtype)


class _Comm:
    def __init__(self, ins, out_shapes, aliases, sems, start, finish, done):
        self.ins, self.out_shapes, self.aliases, self.sems = list(ins), list(out_shapes), dict(aliases), list(sems)
        self.start, self.finish, self.done = start, finish, done


def _pcall(body, *, name, out_shape, grid=(), in_specs=None, out_specs=None, scratch_shapes=(), vmem=VMEM_REQUEST,
           aliases=None, prefetch=0, comm=None):
    params = pltpu.CompilerParams(dimension_semantics=("arbitrary",) * len(grid), vmem_limit_bytes=vmem)
    single = not isinstance(out_shape, (tuple, list))
    outs = [out_shape] if single else list(out_shape)
    ospecs = [out_specs] if single else list(out_specs)
    ispecs, scratch, aliases = list(in_specs), list(scratch_shapes), dict(aliases or {})
    n_in, n_out, n_scr = len(ispecs), len(outs), len(scratch)
    kernel_body = body
    if comm is not None:
        n_ci, n_co = len(comm.ins), len(comm.out_shapes)

        def kernel_body(*refs):
            pre, rest = refs[:prefetch], refs[prefetch:]
            core_in, c_in = rest[:n_in], rest[n_in:n_in + n_ci]
            o0 = n_in + n_ci
            core_out, c_out = rest[o0:o0 + n_out], rest[o0 + n_out:o0 + n_out + n_co]
            s0 = o0 + n_out + n_co
            core_scr, c_sem = rest[s0:s0 + n_scr], rest[s0 + n_scr:]
            first = functools.reduce(jnp.logical_and, [pl.program_id(a) == 0 for a in range(len(grid))])
            last = functools.reduce(jnp.logical_and, [pl.program_id(a) == grid[a] - 1 for a in range(len(grid))])
            pl.when(first)(lambda: comm.start(c_in, c_out, c_sem))
            body(*pre, *core_in, *core_out, *core_scr)
            pl.when(last)(lambda: comm.finish(c_in, c_out, c_sem))

        for i, o in comm.aliases.items():
            aliases[prefetch + n_in + i] = n_out + o
        ispecs += [ANY] * n_ci
        ospecs += [ANY] * n_co
        outs += comm.out_shapes
        scratch += comm.sems
    if prefetch:
        spec = pltpu.PrefetchScalarGridSpec(num_scalar_prefetch=prefetch, grid=grid, in_specs=ispecs,
                                            out_specs=tuple(ospecs), scratch_shapes=scratch)
        call = pl.pallas_call(kernel_body, name=name, out_shape=tuple(outs), grid_spec=spec,
                              input_output_aliases=aliases, compiler_params=params)
    else:
        call = pl.pallas_call(kernel_body, name=name, out_shape=tuple(outs), grid=grid, in_specs=ispecs,
                              out_specs=tuple(ospecs), scratch_shapes=scratch, input_output_aliases=aliases,
                              compiler_params=params)

    def run(*args):
        res = call(*args, *(comm.ins if comm is not None else ()))
        if comm is not None:
            comm.done(res[n_out:])
        return res[0] if single else tuple(res[:n_out])

    return run


def _run_comm(comm, name):
    def body(*refs):
        n_ci, n_co = len(comm.ins), len(comm.out_shapes)
        c_in, c_out, c_sem = refs[:n_ci], refs[n_ci:n_ci + n_co], refs[n_ci + n_co:]
        comm.start(c_in, c_out, c_sem)
        comm.finish(c_in, c_out, c_sem)

    res = pl.pallas_call(body, name=name, out_shape=tuple(comm.out_shapes), in_specs=[ANY] * len(comm.ins),
                         out_specs=tuple([ANY] * len(comm.out_shapes)), scratch_shapes=comm.sems,
                         input_output_aliases=comm.aliases)(*comm.ins)
    comm.done(res)


def _dot(a, b):
    return lax.dot_general(a, b, (((1,), (0,)), ((), ())), preferred_element_type=F32)


def _dot_nt(a, b):
    return lax.dot_general(a, b, (((1,), (1,)), ((), ())), preferred_element_type=F32)


def _dot_tn(a, b):
    return lax.dot_general(a, b, (((0,), (0,)), ((), ())), preferred_element_type=F32)


def _sigmoid(x):
    return 1.0 / (1.0 + jnp.exp(-x))


def _rstd(x):
    return lax.rsqrt(jnp.mean(x * x, axis=-1, keepdims=True) + EPS)


def _rms_fwd(x, g, name):
    s, d = x.shape
    tm = ROW_CHUNK

    def body(x_ref, g_ref, o_ref):
        xv = x_ref[...]
        o_ref[...] = (xv * _rstd(xv) * g_ref[...]).astype(BF16)

    return _pcall(body, name=name, out_shape=_sds((s, d), BF16), grid=(s // tm,),
                  in_specs=[pl.BlockSpec((tm, d), lambda i: (i, 0)), pl.BlockSpec((1, d), lambda i: (0, 0))],
                  out_specs=pl.BlockSpec((tm, d), lambda i: (i, 0)))(x, g)


def _rms_bwd(x, g, dh, dres, name):
    s, d = x.shape
    tm = ROW_CHUNK

    def body(x_ref, g_ref, dh_ref, dres_ref, dx_ref, dxb_ref, dg_ref):
        i = pl.program_id(0)
        xv = x_ref[...]
        r = _rstd(xv)
        xh = xv * r
        dhv = dh_ref[...]
        gd = dhv * g_ref[...]
        dx = dres_ref[...] + r * (gd - xh * jnp.mean(gd * xh, axis=-1, keepdims=True))
        dx_ref[...] = dx
        dxb_ref[...] = dx.astype(BF16)
        part = jnp.sum(dhv * xh, axis=0, keepdims=True)

        @pl.when(i == 0)
        def _():
            dg_ref[...] = part

        @pl.when(i > 0)
        def _():
            dg_ref[...] += part

    row = pl.BlockSpec((tm, d), lambda i: (i, 0))
    vec = pl.BlockSpec((1, d), lambda i: (0, 0))
    return _pcall(body, name=name, out_shape=(_sds((s, d), F32), _sds((s, d), BF16), _sds((1, d), F32)),
                  grid=(s // tm,), in_specs=[row, vec, row, row], out_specs=(row, row, vec))(x, g, dh, dres)


def _final_loss(x, g, tgt, name):
    s, d = x.shape
    tm = ROW_CHUNK

    def body(x_ref, g_ref, t_ref, loss_ref, dx_ref, dxb_ref, dg_ref):
        i = pl.program_id(0)
        xv = x_ref[...]
        r = _rstd(xv)
        xh = xv * r
        e = xh * g_ref[...] - t_ref[...]
        lpart = 0.5 * jnp.sum(jnp.mean(e * e, axis=-1, keepdims=True), axis=0, keepdims=True)
        dy = e * (1.0 / d)
        gd = dy * g_ref[...]
        dx = r * (gd - xh * jnp.mean(gd * xh, axis=-1, keepdims=True))
        dx_ref[...] = dx
        dxb_ref[...] = dx.astype(BF16)
        part = jnp.sum(dy * xh, axis=0, keepdims=True)
        lrow = jnp.broadcast_to(lpart, (1, LANES))

        @pl.when(i == 0)
        def _():
            dg_ref[...] = part
            loss_ref[...] = lrow

        @pl.when(i > 0)
        def _():
            dg_ref[...] += part
            loss_ref[...] += lrow

    row = pl.BlockSpec((tm, d), lambda i: (i, 0))
    vec = pl.BlockSpec((1, d), lambda i: (0, 0))
    return _pcall(body, name=name,
                  out_shape=(_sds((1, LANES), F32), _sds((s, d), F32), _sds((s, d), BF16), _sds((1, d), F32)),
                  grid=(s // tm,), in_specs=[row, vec, row],
                  out_specs=(pl.BlockSpec((1, LANES), lambda i: (0, 0)), row, row, vec))(x, g, tgt)


def _mm_n(a, b, layer, *, nt, tn, out_dtype, name, comm=None):
    s, k = a.shape
    n = b.shape[1] if nt else b.shape[2]
    rows = 512

    def body(a_ref, b_ref, o_ref):
        bv = b_ref[...]
        for r0 in range(0, s, rows):
            av = a_ref[r0:r0 + rows, :]
            o_ref[r0:r0 + rows, :] = (_dot_nt(av, bv) if nt else _dot(av, bv)).astype(out_dtype)

    b_spec = (pl.BlockSpec((None, tn, k), lambda j: (layer, j, 0)) if nt
              else pl.BlockSpec((None, k, tn), lambda j: (layer, 0, j)))
    return _pcall(body, name=name, out_shape=_sds((s, n), out_dtype), grid=(n // tn,),
                  in_specs=[pl.BlockSpec((s, k), lambda j: (0, 0)), b_spec],
                  out_specs=pl.BlockSpec((s, tn), lambda j: (0, j)), comm=comm)(a, b)


def _mm_k(a, b, layer, resid, *, nt, tk, b_off, name, comm=None):
    s, ka = a.shape
    n = b.shape[1] if nt else b.shape[2]
    rows = 512

    def body(a_ref, b_ref, *refs):
        o_ref = refs[-1]
        kk = pl.program_id(0)
        bv = b_ref[...]

        @pl.when(kk == 0)
        def _():
            o_ref[...] = jnp.zeros((s, n), F32) if resid is None else refs[0][...]

        for r0 in range(0, s, rows):
            av = a_ref[r0:r0 + rows, :]
            o_ref[r0:r0 + rows, :] += _dot_nt(av, bv) if nt else _dot(av, bv)

    b_spec = (pl.BlockSpec((None, n, tk), lambda kk: (layer, 0, kk + b_off)) if nt
              else pl.BlockSpec((None, tk, n), lambda kk: (layer, kk + b_off, 0)))
    full = pl.BlockSpec((s, n), lambda kk: (0, 0))
    extra = () if resid is None else (resid,)
    return _pcall(body, name=name, out_shape=_sds((s, n), F32), grid=(ka // tk,),
                  in_specs=[pl.BlockSpec((s, tk), lambda kk: (0, kk)), b_spec] + [full] * len(extra),
                  out_specs=full, comm=comm)(a, b, *extra)


def _mm_tn(a, b, *, t, name):
    s, ka = a.shape
    n = b.shape[1]

    def body(a_ref, b_ref, o_ref):
        o_ref[...] = _dot_tn(a_ref[...], b_ref[...]).astype(BF16)

    return _pcall(body, name=name, out_shape=_sds((ka, n), BF16), grid=(ka // t,),
                  in_specs=[pl.BlockSpec((s, t), lambda i: (0, i)), pl.BlockSpec((s, n), lambda i: (0, 0))],
                  out_specs=pl.BlockSpec((t, n), lambda i: (i, 0)))(a, b)


def _mm_tn2(a, b_lo, b_hi, *, t, name):
    s, ka = a.shape
    half = b_lo.shape[1]
    nb = half // t

    def body(a_ref, lo_ref, hi_ref, o_ref):
        j = pl.program_id(0)

        @pl.when(j < nb)
        def _():
            o_ref[...] = _dot_tn(a_ref[...], lo_ref[...]).astype(BF16)

        @pl.when(j >= nb)
        def _():
            o_ref[...] = _dot_tn(a_ref[...], hi_ref[...]).astype(BF16)

    return _pcall(body, name=name, out_shape=_sds((ka, 2 * half), BF16), grid=(2 * nb,),
                  in_specs=[pl.BlockSpec((s, ka), lambda j: (0, 0)),
                            pl.BlockSpec((s, t), lambda j: (0, jnp.minimum(j, nb - 1))),
                            pl.BlockSpec((s, t), lambda j: (0, jnp.maximum(j - nb, 0)))],
                  out_specs=pl.BlockSpec((ka, t), lambda j: (0, j)))(a, b_lo, b_hi)


def _conv_taps(win, w_ref, width, pad, rows):
    acc = None
    for k in range(width):
        off = pad - (width - 1) + k
        term = w_ref[pl.ds(k, 1), :] * win[off:off + rows, :]
        acc = term if acc is None else acc + term
    return acc


def _conv_taps_t(win, w_ref, width, rows):
    acc = None
    for k in range(width):
        off = (width - 1) - k
        term = w_ref[pl.ds(k, 1), :] * win[off:off + rows, :]
        acc = term if acc is None else acc + term
    return acc


def _conv_wgrad(dw_ref, g, win, width, pad, rows):
    for k in range(width):
        off = pad - (width - 1) + k
        dw_ref[pl.ds(k, 1), :] += jnp.sum(g * win[off:off + rows, :], axis=0, keepdims=True)


def _mixer_a_fwd(ah, ab, ac, win_t, wa_ref, rows):
    ct = _conv_taps(win_t, wa_ref, SHORT_CONV, PAD_SHORT, rows)
    return ab * ct, ct


def _mixer_c_fwd(win_u, wc_ref, cb_ref, lg_ref, lb_ref, rows):
    u = _conv_taps(win_u, wc_ref, CONFORMER_CONV, PAD_LONG, rows) + cb_ref[...]
    mu = jnp.mean(u, axis=-1, keepdims=True)
    uc = u - mu
    rs = lax.rsqrt(jnp.mean(uc * uc, axis=-1, keepdims=True) + EPS)
    uh = uc * rs
    ln = uh * lg_ref[...] + lb_ref[...]
    sg = _sigmoid(ln)
    return ln * sg, ln, sg, uh, rs


def _mix_fwd(z, wa, wc, cb, lg, lb, ga, gc, name):
    s = z.shape[0]
    w = wa.shape[1]
    nblk = z.shape[1] // w
    rc = ROW_CHUNK

    def body(ah_ref, ab_ref, ac_ref, cv_ref, cg_ref, wa_ref, wc_ref, cb_ref, lg_ref, lb_ref, ga_ref, gc_ref,
             ya_ref, yc_ref, tpad, upad):
        tpad[pl.ds(0, PAD_SHORT), :] = jnp.zeros((PAD_SHORT, w), F32)
        upad[pl.ds(0, PAD_LONG), :] = jnp.zeros((PAD_LONG, w), F32)

        def chunk(i, carry):
            base = pl.multiple_of(i * rc, rc)
            rows = pl.ds(base, rc)
            ah, ab, ac = ah_ref[rows, :], ab_ref[rows, :], ac_ref[rows, :]
            tpad[pl.ds(base + PAD_SHORT, rc), :] = ac * ah
            ya, _ = _mixer_a_fwd(ah, ab, ac, tpad[pl.ds(base, rc + PAD_SHORT), :], wa_ref, rc)
            ya_ref[rows, :] = (ya * _rstd(ya) * ga_ref[...]).astype(BF16)
            upad[pl.ds(base + PAD_LONG, rc), :] = cv_ref[rows, :] * _sigmoid(cg_ref[rows, :])
            yc = _mixer_c_fwd(upad[pl.ds(base, rc + PAD_LONG), :], wc_ref, cb_ref, lg_ref, lb_ref, rc)[0]
            yc_ref[rows, :] = (yc * _rstd(yc) * gc_ref[...]).astype(BF16)
            return carry

        lax.fori_loop(0, s // rc, chunk, 0)

    def zblk(j):
        return pl.BlockSpec((s, w), lambda i: (0, j))

    def whole(a):
        return pl.BlockSpec(a.shape, lambda i: (0, 0))

    return _pcall(
        body, name=name, out_shape=(_sds((s, w), BF16), _sds((s, w), BF16)), grid=(1,),
        in_specs=[zblk(0), zblk(1), zblk(2), zblk(nblk - 2), zblk(nblk - 1)] + [whole(a) for a in (wa, wc, cb, lg, lb, ga, gc)],
        out_specs=(pl.BlockSpec((s, w), lambda i: (0, 0)), pl.BlockSpec((s, w), lambda i: (0, 0))),
        scratch_shapes=[pltpu.VMEM((s + PAD_SHORT, w), F32), pltpu.VMEM((s + PAD_LONG, w), F32)],
    )(z, z, z, z, z, wa, wc, cb, lg, lb, ga, gc)


def _mix_bwd(z, dy, wa, wc, cb, lg, lb, ga, gc, name):
    s = z.shape[0]
    w = wa.shape[1]
    nblk = z.shape[1] // w
    nyb = dy.shape[1] // w
    rc = ROW_CHUNK

    def body(ah_ref, ab_ref, ac_ref, cv_ref, cg_ref, dya_ref, dyc_ref,
             wa_ref, wc_ref, cb_ref, lg_ref, lb_ref, ga_ref, gc_ref,
             dza_ref, dzc_ref, dwa_ref, dwc_ref, dcb_ref, dlg_ref, dlb_ref, dga_ref, dgc_ref,
             tpad, upad, dctp, dup):
        tpad[pl.ds(0, PAD_SHORT), :] = jnp.zeros((PAD_SHORT, w), F32)
        upad[pl.ds(0, PAD_LONG), :] = jnp.zeros((PAD_LONG, w), F32)
        dctp[pl.ds(s, PAD_SHORT), :] = jnp.zeros((PAD_SHORT, w), F32)
        dup[pl.ds(s, PAD_LONG), :] = jnp.zeros((PAD_LONG, w), F32)
        for ref in (dwa_ref, dwc_ref, dcb_ref, dlg_ref, dlb_ref, dga_ref, dgc_ref):
            ref[...] = jnp.zeros(ref.shape, F32)

        def rms_bwd(y, g_ref, dyn, dg_ref):
            r = _rstd(y)
            yh = y * r
            gd = dyn * g_ref[...]
            dg_ref[...] += jnp.sum(dyn * yh, axis=0, keepdims=True)
            return r * (gd - yh * jnp.mean(gd * yh, axis=-1, keepdims=True))

        def first(i, carry):
            base = pl.multiple_of(i * rc, rc)
            rows = pl.ds(base, rc)
            ah, ab, ac = ah_ref[rows, :], ab_ref[rows, :], ac_ref[rows, :]
            tpad[pl.ds(base + PAD_SHORT, rc), :] = ac * ah
            win_t = tpad[pl.ds(base, rc + PAD_SHORT), :]
            ya, ct = _mixer_a_fwd(ah, ab, ac, win_t, wa_ref, rc)
            dya = rms_bwd(ya, ga_ref, dya_ref[rows, :], dga_ref)
            dza_ref[rows, w:2 * w] = (dya * ct).astype(BF16)
            dct = dya * ab
            dctp[rows, :] = dct
            _conv_wgrad(dwa_ref, dct, win_t, SHORT_CONV, PAD_SHORT, rc)

            upad[pl.ds(base + PAD_LONG, rc), :] = cv_ref[rows, :] * _sigmoid(cg_ref[rows, :])
            win_u = upad[pl.ds(base, rc + PAD_LONG), :]
            yc, ln, sg, uh, rs = _mixer_c_fwd(win_u, wc_ref, cb_ref, lg_ref, lb_ref, rc)
            dyc = rms_bwd(yc, gc_ref, dyc_ref[rows, :], dgc_ref)
            dln = dyc * (sg * (1.0 + ln * (1.0 - sg)))
            dlg_ref[...] += jnp.sum(dln * uh, axis=0, keepdims=True)
            dlb_ref[...] += jnp.sum(dln, axis=0, keepdims=True)
            duh = dln * lg_ref[...]
            du = rs * (duh - jnp.mean(duh, axis=-1, keepdims=True) - uh * jnp.mean(duh * uh, axis=-1, keepdims=True))
            dcb_ref[...] += jnp.sum(du, axis=0, keepdims=True)
            dup[rows, :] = du
            _conv_wgrad(dwc_ref, du, win_u, CONFORMER_CONV, PAD_LONG, rc)
            return carry

        lax.fori_loop(0, s // rc, first, 0)

        def second(i, carry):
            base = pl.multiple_of(i * rc, rc)
            rows = pl.ds(base, rc)
            dt = _conv_taps_t(dctp[pl.ds(base, rc + PAD_SHORT), :], wa_ref, SHORT_CONV, rc)
            dza_ref[rows, 0:w] = (dt * ac_ref[rows, :]).astype(BF16)
            dza_ref[rows, 2 * w:3 * w] = (dt * ah_ref[rows, :]).astype(BF16)
            du0 = _conv_taps_t(dup[pl.ds(base, rc + PAD_LONG), :], wc_ref, CONFORMER_CONV, rc)
            sg = _sigmoid(cg_ref[rows, :])
            dzc_ref[rows, 0:w] = (du0 * sg).astype(BF16)
            dzc_ref[rows, w:2 * w] = (du0 * cv_ref[rows, :] * sg * (1.0 - sg)).astype(BF16)
            return carry

        lax.fori_loop(0, s // rc, second, 0)

    def blk(j):
        return pl.BlockSpec((s, w), lambda i: (0, j))

    def whole(a):
        return pl.BlockSpec(tuple(a.shape), lambda i: (0, 0))

    params = (wa, wc, cb, lg, lb, ga, gc)
    outs = (_sds((s, 3 * w), BF16), _sds((s, 2 * w), BF16)) + tuple(_sds(p.shape, F32) for p in params)
    return _pcall(
        body, name=name, out_shape=outs, grid=(1,),
        in_specs=[blk(0), blk(1), blk(2), blk(nblk - 2), blk(nblk - 1), blk(0), blk(nyb - 1)] + [whole(p) for p in params],
        out_specs=tuple(whole(o) for o in outs),
        scratch_shapes=[pltpu.VMEM((s + PAD_SHORT, w), F32), pltpu.VMEM((s + PAD_LONG, w), F32),
                        pltpu.VMEM((s + PAD_SHORT, w), F32), pltpu.VMEM((s + PAD_LONG, w), F32)],
    )(z, z, z, z, z, dy, dy, *params)


def _ffn_act_fwd(up, wf, name):
    s, f2 = up.shape
    f = f2 // 2
    tc = 256
    nb = f // tc
    rc = ROW_CHUNK

    def body(g_ref, v_ref, wg_ref, wv_ref, o_ref, gpad, vpad):
        gpad[pl.ds(0, PAD_SHORT), :] = jnp.zeros((PAD_SHORT, tc), F32)
        vpad[pl.ds(0, PAD_SHORT), :] = jnp.zeros((PAD_SHORT, tc), F32)

        def chunk(i, carry):
            base = pl.multiple_of(i * rc, rc)
            rows = pl.ds(base, rc)
            gpad[pl.ds(base + PAD_SHORT, rc), :] = g_ref[rows, :].astype(F32)
            vpad[pl.ds(base + PAD_SHORT, rc), :] = v_ref[rows, :].astype(F32)
            gc = _conv_taps(gpad[pl.ds(base, rc + PAD_SHORT), :], wg_ref, FFN_CONV, PAD_SHORT, rc)
            vc = _conv_taps(vpad[pl.ds(base, rc + PAD_SHORT), :], wv_ref, FFN_CONV, PAD_SHORT, rc)
            o_ref[rows, :] = (gc * _sigmoid(gc) * vc).astype(BF16)
            return carry

        lax.fori_loop(0, s // rc, chunk, 0)

    return _pcall(
        body, name=name, out_shape=_sds((s, f), BF16), grid=(nb,),
        in_specs=[pl.BlockSpec((s, tc), lambda j: (0, j)), pl.BlockSpec((s, tc), lambda j: (0, j + nb)),
                  pl.BlockSpec((FFN_CONV, tc), lambda j: (0, j)), pl.BlockSpec((FFN_CONV, tc), lambda j: (0, j + nb))],
        out_specs=pl.BlockSpec((s, tc), lambda j: (0, j)),
        scratch_shapes=[pltpu.VMEM((s + PAD_SHORT, tc), F32), pltpu.VMEM((s + PAD_SHORT, tc), F32)],
    )(up, up, wf, wf)


def _ffn_act_bwd(up, dact, wf, name, comm=None):
    s, f2 = up.shape
    f = f2 // 2
    tc = 256
    nb = f // tc
    rc = ROW_CHUNK

    def body(g_ref, v_ref, da_ref, wg_ref, wv_ref, act_ref, dg_ref, dv_ref, dwg_ref, dwv_ref, gpad, vpad, dgp, dvp):
        gpad[pl.ds(0, PAD_SHORT), :] = jnp.zeros((PAD_SHORT, tc), F32)
        vpad[pl.ds(0, PAD_SHORT), :] = jnp.zeros((PAD_SHORT, tc), F32)
        dgp[pl.ds(s, PAD_SHORT), :] = jnp.zeros((PAD_SHORT, tc), F32)
        dvp[pl.ds(s, PAD_SHORT), :] = jnp.zeros((PAD_SHORT, tc), F32)
        dwg_ref[...] = jnp.zeros((FFN_CONV, tc), F32)
        dwv_ref[...] = jnp.zeros((FFN_CONV, tc), F32)

        def first(i, carry):
            base = pl.multiple_of(i * rc, rc)
            rows = pl.ds(base, rc)
            gpad[pl.ds(base + PAD_SHORT, rc), :] = g_ref[rows, :].astype(F32)
            vpad[pl.ds(base + PAD_SHORT, rc), :] = v_ref[rows, :].astype(F32)
            win_g = gpad[pl.ds(base, rc + PAD_SHORT), :]
            win_v = vpad[pl.ds(base, rc + PAD_SHORT), :]
            gc = _conv_taps(win_g, wg_ref, FFN_CONV, PAD_SHORT, rc)
            vc = _conv_taps(win_v, wv_ref, FFN_CONV, PAD_SHORT, rc)
            sg = _sigmoid(gc)
            silu = gc * sg
            act_ref[rows, :] = (silu * vc).astype(BF16)
            da = da_ref[rows, :].astype(F32)
            dgc = da * vc * (sg * (1.0 + gc * (1.0 - sg)))
            dvc = da * silu
            dgp[rows, :] = dgc
            dvp[rows, :] = dvc
            _conv_wgrad(dwg_ref, dgc, win_g, FFN_CONV, PAD_SHORT, rc)
            _conv_wgrad(dwv_ref, dvc, win_v, FFN_CONV, PAD_SHORT, rc)
            return carry

        lax.fori_loop(0, s // rc, first, 0)

        def second(i, carry):
            base = pl.multiple_of(i * rc, rc)
            rows = pl.ds(base, rc)
            dg_ref[rows, :] = _conv_taps_t(dgp[pl.ds(base, rc + PAD_SHORT), :], wg_ref, FFN_CONV, rc).astype(BF16)
            dv_ref[rows, :] = _conv_taps_t(dvp[pl.ds(base, rc + PAD_SHORT), :], wv_ref, FFN_CONV, rc).astype(BF16)
            return carry

        lax.fori_loop(0, s // rc, second, 0)

    lo = pl.BlockSpec((s, tc), lambda j: (0, j))
    hi = pl.BlockSpec((s, tc), lambda j: (0, j + nb))
    wlo = pl.BlockSpec((FFN_CONV, tc), lambda j: (0, j))
    whi = pl.BlockSpec((FFN_CONV, tc), lambda j: (0, j + nb))
    act, dgate, dval, dwg, dwv = _pcall(
        body, name=name,
        out_shape=(_sds((s, f), BF16), _sds((s, f), BF16), _sds((s, f), BF16), _sds((FFN_CONV, f), F32), _sds((FFN_CONV, f), F32)),
        grid=(nb,), in_specs=[lo, hi, lo, wlo, whi], out_specs=(lo, lo, lo, wlo, wlo),
        scratch_shapes=[pltpu.VMEM((s + PAD_SHORT, tc), F32) for _ in range(4)], comm=comm,
    )(up, up, dact, wf, wf)
    return act, dgate, dval, jnp.concatenate([dwg, dwv], axis=1)


def _y_assemble(yan, yb, ycn, gb, name):
    s, w = yan.shape
    wb = yb.shape[1]
    tm = ROW_CHUNK

    def body(ya_ref, yb_ref, yc_ref, g_ref, o_ref):
        ybv = yb_ref[...]
        o_ref[:, 0:w] = ya_ref[...]
        o_ref[:, w:w + wb] = (ybv * _rstd(ybv) * g_ref[...]).astype(BF16)
        o_ref[:, w + wb:] = yc_ref[...]

    return _pcall(body, name=name, out_shape=_sds((s, 2 * w + wb), BF16), grid=(s // tm,),
                  in_specs=[pl.BlockSpec((tm, w), lambda i: (i, 0)), pl.BlockSpec((tm, wb), lambda i: (i, 0)),
                            pl.BlockSpec((tm, w), lambda i: (i, 0)), pl.BlockSpec((1, wb), lambda i: (0, 0))],
                  out_specs=pl.BlockSpec((tm, 2 * w + wb), lambda i: (i, 0)))(yan, yb, ycn, gb)


def _yb_norm_bwd(yb, dy, gb, name):
    s, wb = yb.shape
    w = wb // 2
    heads = wb // D_HEAD
    tm = ROW_CHUNK

    def body(yb_ref, d1_ref, d2_ref, g_ref, dyb_ref, dl_ref, dg_ref):
        i = pl.program_id(0)
        y = yb_ref[...]
        dyn = jnp.concatenate([d1_ref[...], d2_ref[...]], axis=1)
        r = _rstd(y)
        yh = y * r
        gd = dyn * g_ref[...]
        dyb = r * (gd - yh * jnp.mean(gd * yh, axis=-1, keepdims=True))
        dyb_ref[...] = dyb
        part = jnp.sum(dyn * yh, axis=0, keepdims=True)
        prod = dyb * y
        even = lax.broadcasted_iota(I32, (tm, LANES), 1) < D_HEAD
        for p in range(heads // 2):
            blk = prod[:, p * LANES:(p + 1) * LANES]
            ev = jnp.sum(jnp.where(even, blk, 0.0), axis=1, keepdims=True)
            od = jnp.sum(jnp.where(even, 0.0, blk), axis=1, keepdims=True)
            dl_ref[2 * p] = jnp.broadcast_to(ev, (tm, LANES))
            dl_ref[2 * p + 1] = jnp.broadcast_to(od, (tm, LANES))

        @pl.when(i == 0)
        def _():
            dg_ref[...] = part

        @pl.when(i > 0)
        def _():
            dg_ref[...] += part

    return _pcall(
        body, name=name, out_shape=(_sds((s, wb), F32), _sds((heads, s, LANES), F32), _sds((1, wb), F32)),
        grid=(s // tm,),
        in_specs=[pl.BlockSpec((tm, wb), lambda i: (i, 0)), pl.BlockSpec((tm, w), lambda i: (i, 1)),
                  pl.BlockSpec((tm, w), lambda i: (i, 2)), pl.BlockSpec((1, wb), lambda i: (0, 0))],
        out_specs=(pl.BlockSpec((tm, wb), lambda i: (i, 0)), pl.BlockSpec((heads, tm, LANES), lambda i: (0, i, 0)),
                   pl.BlockSpec((1, wb), lambda i: (0, 0))),
    )(yb, dy, dy, gb)


def _t5_bucket_table():
    max_exact = NUM_BUCKETS // 2
    out = np.full((len(DILATED_BRANCHES), BLK, 2 * BLK), -1, np.int32)
    rel = np.arange(BLK)[:, None] - np.arange(2 * BLK)[None, :] + BLK
    for b, (window, dilation) in enumerate(DILATED_BRANCHES):
        n_keys = window // dilation
        dist = np.maximum(rel, 0) * dilation
        d_f = np.maximum(dist, 1).astype(np.float32)
        large = max_exact + (np.log(d_f / np.float32(max_exact)) / np.float32(math.log(MAX_DISTANCE / max_exact))
                             * np.float32(NUM_BUCKETS - max_exact)).astype(np.int32)
        large = np.minimum(large, NUM_BUCKETS - 1)
        bucket = np.where(dist < max_exact, dist, large)
        out[b] = np.where((rel >= 0) & (rel <= n_keys), bucket, -1)
    return out


def _bias_tiles(rel_bias, buckets, name):
    nbk, heads = rel_bias.shape
    nbr = buckets.shape[0]

    def body(rb_ref, bk_ref, o_ref):
        for br in range(nbr):
            bk = bk_ref[br]
            tiles = [jnp.full((BLK, 2 * BLK), NEG, F32) for _ in range(heads)]
            for b in range(nbk):
                hit = bk == b
                tiles = [jnp.where(hit, rb_ref[b, h], tiles[h]) for h in range(heads)]
            for h in range(heads):
                o_ref[br, h] = tiles[h]

    return _pcall(body, name=name, out_shape=_sds((nbr, heads, BLK, 2 * BLK), F32), grid=(1,),
                  in_specs=[pl.BlockSpec(memory_space=pltpu.SMEM), pl.BlockSpec(buckets.shape, lambda i: (0, 0, 0))],
                  out_specs=pl.BlockSpec((nbr, heads, BLK, 2 * BLK), lambda i: (0, 0, 0, 0)))(rel_bias, buckets)


def _bias_grad(dtiles, buckets, nbk, name):
    nbr, heads = dtiles.shape[:2]

    def body(dt_ref, bk_ref, o_ref):
        row = lax.broadcasted_iota(I32, (nbk, LANES), 0)
        col = lax.broadcasted_iota(I32, (nbk, LANES), 1)
        out = jnp.zeros((nbk, LANES), F32)
        for h in range(heads):
            for b in range(nbk):
                tot = jnp.zeros((), F32)
                for br in range(nbr):
                    tot = tot + jnp.sum(jnp.where(bk_ref[br] == b, dt_ref[br, h], 0.0))
                out = jnp.where((row == b) & (col == h), tot, out)
        o_ref[...] = out

    return _pcall(body, name=name, out_shape=_sds((nbk, LANES), F32), grid=(1,),
                  in_specs=[pl.BlockSpec(dtiles.shape, lambda i: (0, 0, 0, 0)), pl.BlockSpec(buckets.shape, lambda i: (0, 0, 0))],
                  out_specs=pl.BlockSpec((nbk, LANES), lambda i: (0, 0)))(dtiles, buckets)


def _attn_blocks(s, visit):
    for br, (window, d) in enumerate(DILATED_BRANCHES):
        n_blk = (s // d) // BLK
        span = BLK * d

        def firsts(r, carry, br=br, d=d):
            visit(br, d, r, False)
            return carry

        lax.fori_loop(0, d, firsts, 0)
        if n_blk > 1:
            def rest(idx, carry, br=br, d=d, n_blk=n_blk, span=span):
                r = idx // (n_blk - 1)
                n = 1 + idx % (n_blk - 1)
                visit(br, d, r + n * span, True)
                return carry

            lax.fori_loop(0, d * (n_blk - 1), rest, 0)


def _rows(start, size, d):
    return pl.ds(pl.multiple_of(start, BLK), size) if d == 1 else pl.ds(start, size, stride=d)


def _attn_fwd(z, btiles, col0, name, comm=None):
    s = z.shape[0]
    nbr, heads = btiles.shape[:2]
    pairs = heads // 2
    scale = D_HEAD ** -0.5
    rc = ROW_CHUNK

    def body(q_ref, k_ref, v_ref, bt_ref, yb_ref, lse_ref, acc_ref, m_ref, l_ref):
        even = lax.broadcasted_iota(I32, (BLK, LANES), 1) < D_HEAD
        even2 = lax.broadcasted_iota(I32, (2 * BLK, LANES), 1) < D_HEAD

        def visit(br, d, start, prev):
            kw = 2 * BLK if prev else BLK
            rows_q = _rows(start, BLK, d)
            rows_k = _rows(start - BLK * d, kw, d) if prev else rows_q
            qb = q_ref[rows_q, :]
            kb = k_ref[rows_k, :].astype(BF16)
            vw = v_ref[rows_k, :]
            ev_k = even2 if prev else even
            acc = jnp.zeros((BLK, LANES), F32)
            for e in range(2):
                sel = even if e == 0 else ~even
                sel_k = ev_k if e == 0 else ~ev_k
                qm = jnp.where(sel, qb, 0.0).astype(BF16)
                bias = bt_ref[br, e] if prev else bt_ref[br, e, :, BLK:]
                sc = _dot_nt(qm, kb) * scale + bias
                m = jnp.max(sc, axis=1, keepdims=True)
                p = jnp.exp(sc - m)
                l = jnp.sum(p, axis=1, keepdims=True)
                vm = jnp.where(sel_k, vw, 0.0).astype(BF16)
                acc = acc + _dot(p.astype(BF16), vm)
                m_ref.at[br, e][rows_q, :] = jnp.broadcast_to(m, (BLK, LANES))
                l_ref.at[br, e][rows_q, :] = jnp.broadcast_to(l, (BLK, LANES))
            acc_ref.at[br][rows_q, :] = acc

        _attn_blocks(s, visit)

        ev_c = lax.broadcasted_iota(I32, (rc, LANES), 1) < D_HEAD

        def merge(i, carry):
            rows = pl.ds(pl.multiple_of(i * rc, rc), rc)
            wts, dens = [], []
            for e in range(2):
                ms = [m_ref[br, e, rows, :] for br in range(nbr)]
                top = functools.reduce(jnp.maximum, ms)
                w = [jnp.exp(mb - top) for mb in ms]
                den = functools.reduce(lambda a, b: a + b, [w[br] * l_ref[br, e, rows, :] for br in range(nbr)])
                lse_ref[e, rows, :] = top + jnp.log(den)
                wts.append(w)
                dens.append(den)
            num = functools.reduce(lambda a, b: a + b,
                                   [jnp.where(ev_c, wts[0][br], wts[1][br]) * acc_ref[br, rows, :] for br in range(nbr)])
            yb_ref[rows, :] = num / jnp.where(ev_c, dens[0], dens[1])
            return carry

        lax.fori_loop(0, s // rc, merge, 0)

    def zcol(j):
        return pl.BlockSpec((s, LANES), lambda p, j=j: (0, col0 + j + p))

    return _pcall(
        body, name=name, out_shape=(_sds((s, pairs * LANES), F32), _sds((heads, s, LANES), F32)), grid=(pairs,),
        in_specs=[zcol(0), zcol(pairs), zcol(2 * pairs), pl.BlockSpec((nbr, 2, BLK, 2 * BLK), lambda p: (0, p, 0, 0))],
        out_specs=(pl.BlockSpec((s, LANES), lambda p: (0, p)), pl.BlockSpec((2, s, LANES), lambda p: (p, 0, 0))),
        scratch_shapes=[pltpu.VMEM((nbr, s, LANES), F32), pltpu.VMEM((nbr, 2, s, LANES), F32), pltpu.VMEM((nbr, 2, s, LANES), F32)],
        comm=comm,
    )(z, z, z, btiles)


def _attn_bwd(z, btiles, dyb, lse, delta, dbias_in, col0, name, comm=None):
    s = z.shape[0]
    nbr, heads = btiles.shape[:2]
    pairs = heads // 2
    scale = D_HEAD ** -0.5

    def body(q_ref, k_ref, v_ref, bt_ref, dy_ref, lse_ref, dl_ref, dbi_ref,
             dq_ref, dk_ref, dv_ref, db_ref, dqa, dka, dva):
        even = lax.broadcasted_iota(I32, (BLK, LANES), 1) < D_HEAD
        even2 = lax.broadcasted_iota(I32, (2 * BLK, LANES), 1) < D_HEAD
        for ref in (dqa, dka, dva):
            ref[...] = jnp.zeros((s, LANES), F32)
        db_ref[...] = dbi_ref[...]

        def visit(br, d, start, prev):
            kw = 2 * BLK if prev else BLK
            rows_q = _rows(start, BLK, d)
            rows_k = _rows(start - BLK * d, kw, d) if prev else rows_q
            qb = q_ref[rows_q, :]
            dyv = dy_ref[rows_q, :]
            kwin = k_ref[rows_k, :]
            kb = kwin.astype(BF16)
            vb = v_ref[rows_k, :].astype(BF16)
            ev_k = even2 if prev else even
            dq = jnp.zeros((BLK, LANES), F32)
            dk = jnp.zeros((kw, LANES), F32)
            dv = jnp.zeros((kw, LANES), F32)
            for e in range(2):
                sel = even if e == 0 else ~even
                sel_k = ev_k if e == 0 else ~ev_k
                qm = jnp.where(sel, qb, 0.0).astype(BF16)
                dym = jnp.where(sel, dyv, 0.0).astype(BF16)
                bias = bt_ref[br, e] if prev else bt_ref[br, e, :, BLK:]
                sc = _dot_nt(qm, kb) * scale + bias
                lt = lse_ref.at[e][rows_q, :]
                dt = dl_ref.at[e][rows_q, :]
                if prev:
                    lt = jnp.concatenate([lt, lt], axis=1)
                    dt = jnp.concatenate([dt, dt], axis=1)
                p = jnp.exp(sc - lt)
                ds = p * (_dot_nt(dym, vb) - dt)
                if prev:
                    db_ref[br, e] += ds
                else:
                    db_ref[br, e, :, BLK:] += ds
                dsb = ds.astype(BF16)
                km = jnp.where(sel_k, kwin, 0.0).astype(BF16)
                dq = dq + _dot(dsb, km)
                dk = dk + _dot_tn(dsb, qm)
                dv = dv + _dot_tn(p.astype(BF16), dym)
            dqa[rows_q, :] += dq * scale
            dka[rows_k, :] += dk * scale
            dva[rows_k, :] += dv

        _attn_blocks(s, visit)
        dq_ref[...] = dqa[...].astype(BF16)
        dk_ref[...] = dka[...].astype(BF16)
        dv_ref[...] = dva[...].astype(BF16)

    def zcol(j):
        return pl.BlockSpec((s, LANES), lambda p, j=j: (0, col0 + j + p))

    col = pl.BlockSpec((s, LANES), lambda p: (0, p))
    stat = pl.BlockSpec((2, s, LANES), lambda p: (p, 0, 0))
    tile = pl.BlockSpec((nbr, 2, BLK, 2 * BLK), lambda p: (0, p, 0, 0))
    wide = _sds((s, pairs * LANES), BF16)
    return _pcall(
        body, name=name, out_shape=(wide, wide, wide, _sds(btiles.shape, F32)), grid=(pairs,),
        in_specs=[zcol(0), zcol(pairs), zcol(2 * pairs), tile, col, stat, stat, tile],
        out_specs=(col, col, col, tile),
        scratch_shapes=[pltpu.VMEM((s, LANES), F32) for _ in range(3)], comm=comm,
    )(z, z, z, btiles, dyb, lse, delta, dbias_in)


def _row(v):
    return v.reshape(1, -1)


class _LocalSchedule:
    def __init__(self):
        self.big = {}

    def fwd_comms(self, l):
        return {}

    def bwd_comms(self, l):
        return {}

    def after_bwd(self, l, grads):
        self.big[l] = grads


def _layer_fwd(l, x, wts, prm, btiles, comms):
    d = x.shape[1]
    wq = d // 4
    gout = prm["out_norm_g"][l]
    h = _rms_fwd(x, _row(prm["norm_mix_g"][l]), "rms_mix_fwd")
    z = _mm_n(h, wts["in_t"], l, nt=True, tn=256, out_dtype=F32, name="in_proj")
    yan, ycn = _mix_fwd(z, prm["conv_a_w"][l], prm["conv_c_w"][l], _row(prm["conv_c_b"][l]), _row(prm["ln_c_g"][l]),
                        _row(prm["ln_c_b"][l]), _row(gout[:wq]), _row(gout[3 * wq:]), "mix_fwd")
    yb, lse = _attn_fwd(z, btiles, 3 * wq // LANES, "attn_fwd", comm=comms.get("attn_fwd"))
    y = _y_assemble(yan, yb, ycn, _row(gout[wq:3 * wq]), "y_assemble")
    x_mid = _mm_k(y, wts["out"], l, x, nt=False, tk=512, b_off=0, name="out_proj")
    h2 = _rms_fwd(x_mid, _row(prm["norm_ffn_g"][l]), "rms_ffn_fwd")
    up = _mm_n(h2, wts["up"], l, nt=False, tn=512, out_dtype=BF16, name="up_proj", comm=comms.get("up_proj"))
    act = _ffn_act_fwd(up, prm["conv_f_w"][l], "ffn_act_fwd")
    x_out = _mm_k(act, wts["down"], l, x_mid, nt=False, tk=256, b_off=0, name="down_proj", comm=comms.get("down_proj"))
    return x_out, (x, h, z, yb, lse, y, x_mid, h2, up)


def _layer_bwd(l, dxo, dxo_b, saved, wts, prm, btiles, dbias, comms):
    x, h, z, yb, lse, y, x_mid, h2, up = saved
    d = x.shape[1]
    wq = d // 4
    f = up.shape[1] // 2
    gout = prm["out_norm_g"][l]
    dact = _mm_n(dxo_b, wts["down"], l, nt=True, tn=256, out_dtype=BF16, name="down_proj_dx")
    act, dgate, dval, dwf = _ffn_act_bwd(up, dact, prm["conv_f_w"][l], "ffn_act_bwd", comm=comms.get("ffn_act_bwd"))
    g_down = _mm_tn(act, dxo_b, t=256, name="down_proj_dw")
    dh2 = _mm_k(dgate, wts["up"], l, None, nt=True, tk=256, b_off=0, name="up_proj_dx_gate")
    dh2 = _mm_k(dval, wts["up"], l, dh2, nt=True, tk=256, b_off=f // 256, name="up_proj_dx_val")
    dxm, dxm_b, dg_ffn = _rms_bwd(x_mid, _row(prm["norm_ffn_g"][l]), dh2, dxo, "rms_ffn_bwd")
    g_up = _mm_tn2(h2, dgate, dval, t=256, name="up_proj_dw")
    dy = _mm_k(dxm_b, wts["out"], l, None, nt=True, tk=512, b_off=0, name="out_proj_dx")
    g_out = _mm_tn(y, dxm_b, t=256, name="out_proj_dw")
    dza, dzc, dwa, dwc, dcb, dlg, dlb, dga, dgc = _mix_bwd(
        z, dy, prm["conv_a_w"][l], prm["conv_c_w"][l], _row(prm["conv_c_b"][l]), _row(prm["ln_c_g"][l]),
        _row(prm["ln_c_b"][l]), _row(gout[:wq]), _row(gout[3 * wq:]), "mix_bwd")
    dyb, delta, dgb = _yb_norm_bwd(yb, dy, _row(gout[wq:3 * wq]), "yb_norm_bwd")
    dq, dk, dv, dbias = _attn_bwd(z, btiles, dyb, lse, delta, dbias, 3 * wq // LANES, "attn_bwd",
                                  comm=comms.get("attn_bwd"))
    dz = jnp.concatenate([dza, dq, dk, dv, dzc], axis=1)
    dh = _mm_k(dz, wts["in_t"], l, None, nt=False, tk=256, b_off=0, name="in_proj_dx")
    dx, dx_b, dg_mix = _rms_bwd(x, _row(prm["norm_mix_g"][l]), dh, dxm, "rms_mix_bwd")
    g_in_t = _mm_tn(dz, h, t=256, name="in_proj_dw")
    big = {"in_t": g_in_t, "out": g_out, "up": g_up, "down": g_down}
    small = {"norm_mix_g": dg_mix[0], "conv_a_w": dwa, "conv_c_w": dwc, "conv_c_b": dcb[0], "ln_c_g": dlg[0],
             "ln_c_b": dlb[0], "out_norm_g": jnp.concatenate([dga[0], dgb[0], dgc[0]]), "norm_ffn_g": dg_ffn[0],
             "conv_f_w": dwf}
    return dx, dx_b, big, small, dbias


def _local_step(x, tgt, wts, prm, sched):
    depth = prm["norm_mix_g"].shape[0]
    buckets = jnp.asarray(_t5_bucket_table())
    btiles = _bias_tiles(prm["rel_bias"], buckets, "bias_tiles")
    saved = []
    for l in range(depth):
        x, sv = _layer_fwd(l, x, wts, prm, btiles, sched.fwd_comms(l))
        saved.append(sv)
    loss, dx, dx_b, dg_final = _final_loss(x, _row(prm["final_g"]), tgt, "final_loss")
    dbias = jnp.zeros(btiles.shape, F32)
    small = [None] * depth
    for l in reversed(range(depth)):
        dx, dx_b, grads, small[l], dbias = _layer_bwd(l, dx, dx_b, saved[l], wts, prm, btiles, dbias, sched.bwd_comms(l))
        sched.after_bwd(l, grads)
    nbk, heads = prm["rel_bias"].shape
    d_rel = _bias_grad(dbias, buckets, nbk, "bias_grad")[:, :heads]
    return loss, dx, small, d_rel, dg_final[0]


BIG = ("in_t", "out", "up", "down")
COL_SHARDED = ("up",)
N_CHIPS = 4
N_DEV = 8
BF16_ROWS = 16


def _me():
    return lax.axis_index("x"), lax.axis_index("y"), lax.axis_index("c")


def _chip_of(x, y):
    return 2 * x + y


def _other_chips(x, y):
    return ((1 - x, y), (x, 1 - y), (1 - x, 1 - y))


def _remote(src, dst, send_sem, recv_sem, device):
    return pltpu.make_async_remote_copy(src_ref=src, dst_ref=dst, send_sem=send_sem, recv_sem=recv_sem,
                                        device_id=device, device_id_type=MESH)


def _ag_comm(wts, layer, ici_keys, fwd_keys):
    keys = tuple(k for k in BIG if k in ici_keys or k in fwd_keys)

    def geo(k):
        _, rows, cols = wts[k].shape
        return (rows, cols // N_CHIPS) if k in COL_SHARDED else (rows // N_CHIPS, cols)

    def copies(refs, sems):
        g = dict(zip(keys, refs))
        isend, irecv, dsend, drecv = sems
        x, y, c = _me()
        mine = _chip_of(x, y)

        def region(k, chip, half):
            r, cc = geo(k)
            h = r // 2
            if k in COL_SHARDED:
                return g[k].at[layer, pl.ds(pl.multiple_of(half * h, BF16_ROWS), h), pl.ds(pl.multiple_of(chip * cc, LANES), cc)]
            return g[k].at[layer, pl.ds(pl.multiple_of(chip * r + half * h, BF16_ROWS), h), :]

        def ici(k, f, landing):
            chip = _other_chips(x, y)[f]
            where = region(k, _chip_of(*chip) if landing else mine, c)
            i = keys.index(k)
            return _remote(where, where, isend.at[i, f], irecv.at[i, f], (*chip, c))

        def fwd(k, f, landing):
            chip = _other_chips(x, y)[f]
            where = region(k, _chip_of(*chip), 1 - c if landing else c)
            i = keys.index(k)
            return _remote(where, where, dsend.at[i, f], drecv.at[i, f], (x, y, 1 - c))

        return ici, fwd

    def start(ins, outs, sems):
        ici, fwd = copies(outs, sems)
        for k in keys:
            for f in range(3):
                if k in ici_keys:
                    ici(k, f, False).start()
                else:
                    fwd(k, f, False).start()

    def finish(ins, outs, sems):
        ici, fwd = copies(outs, sems)
        for k in keys:
            for f in range(3):
                if k in ici_keys:
                    ici(k, f, True).wait_recv()
                    if k in fwd_keys:
                        fwd(k, f, False).start()
        for k in keys:
            for f in range(3):
                if k in fwd_keys:
                    fwd(k, f, True).wait_recv()
                    fwd(k, f, False).wait_send()
                if k in ici_keys:
                    ici(k, f, False).wait_send()

    def done(res):
        wts.update(zip(keys, res))

    n = len(keys)
    return _Comm([wts[k] for k in keys], [_sds(wts[k].shape, BF16) for k in keys], {i: i for i in range(n)},
                 [pltpu.SemaphoreType.DMA((n, 3)) for _ in range(4)], start, finish, done)


def _small_gather_comm(slab, store):
    def copies(ins, outs, sems):
        send, recv, lsem = sems
        x, y, c = _me()
        mine = _chip_of(x, y)
        own = pltpu.make_async_copy(ins[0], outs[0].at[mine], lsem)
        pairs = []
        for f, chip in enumerate(_other_chips(x, y)):
            out = _remote(ins[0], outs[0].at[mine], send.at[f], recv.at[f], (*chip, c))
            land = _remote(ins[0], outs[0].at[_chip_of(*chip)], send.at[f], recv.at[f], (*chip, c))
            pairs.append((out, land))
        return own, pairs

    def start(ins, outs, sems):
        own, pairs = copies(ins, outs, sems)
        own.start()
        for out, _ in pairs:
            out.start()

    def finish(ins, outs, sems):
        own, pairs = copies(ins, outs, sems)
        for out, land in pairs:
            land.wait_recv()
            out.wait_send()
        own.wait()

    def done(res):
        store["small"] = res[0]

    return _Comm([slab], [_sds((N_CHIPS,) + slab.shape, F32)], {},
                 [pltpu.SemaphoreType.DMA((3,)), pltpu.SemaphoreType.DMA((3,)), pltpu.SemaphoreType.DMA], start, finish, done)


def _piece_geo(g):
    geo = {}
    for k in BIG:
        rows, cols = g[k].shape
        geo[k] = (rows // 2, cols // N_CHIPS) if k in COL_SHARDED else (rows // (2 * N_CHIPS), cols)
    return geo


def _rs_swap_halves(g):
    geo = _piece_geo(g)
    n_copies = sum(N_CHIPS if k in COL_SHARDED else 1 for k in BIG)

    def body(*refs):
        g_refs = dict(zip(BIG, refs[:len(BIG)]))
        t_refs = dict(zip(BIG, refs[len(BIG):2 * len(BIG)]))
        send, recv = refs[2 * len(BIG):]
        x, y, c = _me()
        copies = []
        for k in BIG:
            h, cc = geo[k]
            if k in COL_SHARDED:
                rows = pl.ds(pl.multiple_of((1 - c) * h, BF16_ROWS), h)
                for j in range(N_CHIPS):
                    copies.append((g_refs[k].at[rows, pl.ds(j * cc, cc)], t_refs[k].at[j]))
            else:
                copies.append((g_refs[k].at[:, 1 - c], t_refs[k]))
        started = []
        for i, (src, dst) in enumerate(copies):
            cp = _remote(src, dst, send.at[i], recv.at[i], (x, y, 1 - c))
            cp.start()
            started.append(cp)
        for cp in started:
            cp.wait()

    ins = [g[k] if k in COL_SHARDED else g[k].reshape(N_CHIPS, 2, geo[k][0], geo[k][1]) for k in BIG]
    outs = tuple(_sds((N_CHIPS,) + geo[k], BF16) for k in BIG)
    res = _pcall(body, name="rs_swap_halves", out_shape=outs, in_specs=[ANY] * len(ins), out_specs=tuple([ANY] * len(outs)),
                 scratch_shapes=[pltpu.SemaphoreType.DMA((n_copies,)) for _ in range(2)])(*ins)
    return dict(zip(BIG, res))


def _pair_sum(g, theirs, c_arr):
    geo = _piece_geo(g)

    def body(c_ref, *refs):
        nk = len(BIG)
        for i in range(nk):
            refs[2 * nk + i][...] = (refs[i][...].astype(F32) + refs[nk + i][...].astype(F32)).astype(BF16)

    in_specs, ins = [], []
    for k in BIG:
        h, cc = geo[k]
        if k in COL_SHARDED:
            in_specs.append(pl.BlockSpec((h, cc), lambda j, c_ref: (c_ref[0], j)))
            ins.append(g[k])
        else:
            in_specs.append(pl.BlockSpec((None, h, cc), lambda j, c_ref: (2 * j + c_ref[0], 0, 0)))
            ins.append(g[k].reshape(2 * N_CHIPS, h, cc))
    slab = [pl.BlockSpec((None,) + geo[k], lambda j, c_ref: (j, 0, 0)) for k in BIG]
    res = _pcall(body, name="rs_pair_sum", out_shape=tuple(_sds((N_CHIPS,) + geo[k], BF16) for k in BIG), grid=(N_CHIPS,),
                 in_specs=in_specs + slab, out_specs=tuple(slab), prefetch=1)(c_arr, *ins, *[theirs[k] for k in BIG])
    return dict(zip(BIG, res))


def _rs_comm(p, keys, store):
    def copies(ins, outs, sems):
        send, recv = sems
        x, y, c = _me()
        return [_remote(ins[i].at[_chip_of(*chip)], outs[i].at[f], send.at[i, f], recv.at[i, f], (*chip, c))
                for i in range(len(keys)) for f, chip in enumerate(_other_chips(x, y))]

    def start(ins, outs, sems):
        for cp in copies(ins, outs, sems):
            cp.start()

    def finish(ins, outs, sems):
        for cp in copies(ins, outs, sems):
            cp.wait()

    def done(res):
        store.update(zip(keys, res))

    return _Comm([p[k] for k in keys], [_sds((3,) + p[k].shape[1:], BF16) for k in keys], {},
                 [pltpu.SemaphoreType.DMA((len(keys), 3)) for _ in range(2)], start, finish, done)


def _quad_sum(p, b, where, l, full):
    parts = 2
    nk = len(BIG)

    def body(where_ref, *refs):
        for i in range(nk):
            acc = refs[i][...].astype(F32)
            for f in range(3):
                acc = acc + refs[nk + 3 * i + f][...].astype(F32)
            refs[5 * nk + i][...] = acc

    own, recv, outs = [], [], []
    for k in BIG:
        h, cc = p[k].shape[1:]
        th = h // parts
        own.append(pl.BlockSpec((None, th, cc), lambda i, w_ref: (w_ref[0], i, 0)))
        recv += [pl.BlockSpec((None, th, cc), lambda i, w_ref, f=f: (f, i, 0)) for f in range(3)]
        outs.append(pl.BlockSpec((None, None, th, cc), lambda i, w_ref: (l, w_ref[1], i, 0)))
    args = [p[k] for k in BIG] + [b[k] for k in BIG for _ in range(3)] + [full[k] for k in BIG]
    res = _pcall(body, name="rs_quad_sum", out_shape=tuple(_sds(full[k].shape, F32) for k in BIG), grid=(parts,),
                 in_specs=own + recv + [ANY] * nk, out_specs=tuple(outs), prefetch=1,
                 aliases={1 + 4 * nk + i: i for i in range(nk)})(where, *args)
    return dict(zip(BIG, res))


def _rs_share(l, full):
    nk = len(BIG)

    def body(*refs):
        f_refs = refs[nk:2 * nk]
        send, recv = refs[2 * nk:]
        x, y, c = _me()
        started = []
        for i in range(nk):
            cp = _remote(f_refs[i].at[l, c], f_refs[i].at[l, c], send.at[i], recv.at[i], (x, y, 1 - c))
            cp.start()
            started.append(cp)
        for i, cp in enumerate(started):
            _remote(f_refs[i].at[l, 1 - c], f_refs[i].at[l, 1 - c], send.at[i], recv.at[i], (x, y, 1 - c)).wait_recv()
            cp.wait_send()

    res = _pcall(body, name="rs_share", out_shape=tuple(_sds(full[k].shape, F32) for k in BIG),
                 in_specs=[ANY] * nk, out_specs=tuple([ANY] * nk), aliases={i: i for i in range(nk)},
                 scratch_shapes=[pltpu.SemaphoreType.DMA((nk,)) for _ in range(2)])(*[full[k] for k in BIG])
    return dict(zip(BIG, res))


def _gather_partials(slab):
    def body(s_ref, o_ref, send, recv):
        x, y, c = _me()
        me = 4 * x + 2 * y + c
        started = []
        peers = []
        for mask in range(1, N_DEV):
            peer = (x ^ (mask >> 2), y ^ ((mask >> 1) & 1), c ^ (mask & 1))
            peers.append(peer)
            cp = _remote(s_ref, o_ref.at[me], send.at[mask - 1], recv.at[mask - 1], peer)
            cp.start()
            started.append(cp)
        for i, peer in enumerate(peers):
            _remote(s_ref, o_ref.at[4 * peer[0] + 2 * peer[1] + peer[2]], send.at[i], recv.at[i], peer).wait_recv()
        for cp in started:
            cp.wait_send()

    return _pcall(body, name="gather_partials", out_shape=_sds((N_DEV,) + slab.shape, F32), in_specs=[ANY], out_specs=ANY,
                  scratch_shapes=[pltpu.SemaphoreType.DMA((N_DEV - 1,)), pltpu.SemaphoreType.DMA((N_DEV - 1,))])(slab)


def _sum_slabs(slabs, own, me):
    n, r, lanes = slabs.shape
    tr = r // 2

    def body(me_ref, s_ref, own_ref, o_ref):
        o_ref[...] = jnp.zeros((tr, lanes), F32)
        for i in range(n):
            @pl.when(me_ref[0] == i)
            def _():
                o_ref[...] += own_ref[...]

            @pl.when(me_ref[0] != i)
            def _():
                o_ref[...] += s_ref[i]

    return _pcall(body, name="sum_partials", out_shape=_sds((r, lanes), F32), grid=(2,),
                  in_specs=[pl.BlockSpec((n, tr, lanes), lambda i, me_ref: (0, i, 0)),
                            pl.BlockSpec((tr, lanes), lambda i, me_ref: (i, 0))],
                  out_specs=pl.BlockSpec((tr, lanes), lambda i, me_ref: (i, 0)), prefetch=1)(me, slabs, own)


def _cast_into_gathered(w, chip, by_cols, name):
    l, r, c = w.shape

    def body(chip_ref, w_ref, o_ref):
        o_ref[...] = w_ref[...].astype(BF16)

    if by_cols:
        shape, out = (l, r, N_CHIPS * c), pl.BlockSpec((None, r, c), lambda i, chip_ref: (i, 0, chip_ref[0]))
    else:
        shape, out = (l, N_CHIPS * r, c), pl.BlockSpec((None, r, c), lambda i, chip_ref: (i, chip_ref[0], 0))
    return _pcall(body, name=name, out_shape=_sds(shape, BF16), grid=(l,),
                  in_specs=[pl.BlockSpec((None, r, c), lambda i, chip_ref: (i, 0, 0))], out_specs=out, prefetch=1)(chip, w)


def _adamw(w, g, m, v, name, tr):
    r, c = w.shape

    def body(w_ref, g_ref, m_ref, v_ref, d_ref, mo_ref, vo_ref):
        gv = g_ref[...]
        mn = ADAM_B1 * m_ref[...] + (1.0 - ADAM_B1) * gv
        vn = ADAM_B2 * v_ref[...] + (1.0 - ADAM_B2) * (gv * gv)
        m_hat = mn / (1.0 - ADAM_B1 ** ADAM_STEP)
        v_hat = vn / (1.0 - ADAM_B2 ** ADAM_STEP)
        d_ref[...] = -ADAM_LR * (m_hat / (jnp.sqrt(v_hat) + ADAM_EPS) + ADAM_WD * w_ref[...])
        mo_ref[...] = mn
        vo_ref[...] = vn

    blk = pl.BlockSpec((tr, c), lambda i: (i, 0))
    return _pcall(body, name=name, out_shape=tuple(_sds((r, c), F32) for _ in range(3)), grid=(r // tr,),
                  in_specs=[blk] * 4, out_specs=(blk, blk, blk))(w, g, m, v)


AG_RIDES = {"attn_fwd": (("in_t", "out", "up"), ()), "up_proj": (("down",), ("in_t", "out", "up")), "down_proj": ((), ("down",))}
RS_RIDES = {"ffn_act_bwd": ("up",), "attn_bwd": ("in_t", "out", "down")}


class _Rides:
    def __init__(self, table, build):
        self.table, self.build = table, build

    def get(self, name):
        return self.build(self.table[name]) if name in self.table else None


class _MeshSchedule:
    def __init__(self, wts, depth, c_arr, where):
        self.wts, self.depth, self.c_arr, self.where = wts, depth, c_arr, where
        self.pairs, self.recv, self.full = None, {}, None

    def fwd_comms(self, l):
        if l + 1 >= self.depth:
            return {}
        return _Rides(AG_RIDES, lambda keys: _ag_comm(self.wts, l + 1, *keys))

    def bwd_comms(self, l):
        if self.pairs is None:
            return {}
        return _Rides(RS_RIDES, lambda keys: _rs_comm(self.pairs, keys, self.recv))

    def _finish(self, l):
        self.full = _rs_share(l, _quad_sum(self.pairs, self.recv, self.where, l, self.full))
        self.pairs, self.recv = None, {}

    def after_bwd(self, l, grads):
        if self.pairs is not None:
            self._finish(l + 1)
        if self.full is None:
            geo = _piece_geo(grads)
            self.full = {k: jnp.zeros((self.depth, 2) + geo[k], F32) for k in BIG}
        self.pairs = _pair_sum(grads, _rs_swap_halves(grads), self.c_arr)
        if l == 0:
            _run_comm(_rs_comm(self.pairs, BIG, self.recv), "rs_to_owners")
            self._finish(0)


SHARDED_SMALL = ("conv_a_w", "conv_c_w", "conv_f_w")
SMALL = ("norm_mix_g", "conv_a_w", "conv_c_w", "conv_c_b", "ln_c_g", "ln_c_b", "out_norm_g", "norm_ffn_g",
         "conv_f_w", "rel_bias", "final_g")
SLAB_ROWS = 16


def _pack(arrays):
    flat = jnp.concatenate([a.reshape(-1) for a in arrays])
    unit = SLAB_ROWS * LANES
    total = -(-flat.shape[0] // unit) * unit
    return jnp.pad(flat, (0, total - flat.shape[0])).reshape(-1, LANES)


def _unpack(slab, shapes):
    flat = slab.reshape(-1)
    out, off = [], 0
    for shp in shapes:
        size = math.prod(shp)
        out.append(flat[off:off + size].reshape(shp))
        off += size
    return out


def kernel(x, norm_mix_g, w_in, conv_a_w, conv_c_w, conv_c_b, ln_c_g, ln_c_b, out_norm_g, w_out, norm_ffn_g, w_up, conv_f_w, w_down, rel_bias, final_g, loss_target, m_norm_mix_g, m_w_in, m_conv_a_w, m_conv_c_w, m_conv_c_b, m_ln_c_g, m_ln_c_b, m_out_norm_g, m_w_out, m_norm_ffn_g, m_w_up, m_conv_f_w, m_w_down, m_rel_bias, m_final_g, v_norm_mix_g, v_w_in, v_conv_a_w, v_conv_c_w, v_conv_c_b, v_ln_c_g, v_ln_c_b, v_out_norm_g, v_w_out, v_norm_ffn_g, v_w_up, v_conv_f_w, v_w_down, v_rel_bias, v_final_g):
    weights = dict(norm_mix_g=norm_mix_g, w_in=w_in, conv_a_w=conv_a_w, conv_c_w=conv_c_w, conv_c_b=conv_c_b,
                   ln_c_g=ln_c_g, ln_c_b=ln_c_b, out_norm_g=out_norm_g, w_out=w_out, norm_ffn_g=norm_ffn_g, w_up=w_up,
                   conv_f_w=conv_f_w, w_down=w_down, rel_bias=rel_bias, final_g=final_g)
    mom_m = dict(norm_mix_g=m_norm_mix_g, w_in=m_w_in, conv_a_w=m_conv_a_w, conv_c_w=m_conv_c_w, conv_c_b=m_conv_c_b,
                 ln_c_g=m_ln_c_g, ln_c_b=m_ln_c_b, out_norm_g=m_out_norm_g, w_out=m_w_out, norm_ffn_g=m_norm_ffn_g,
                 w_up=m_w_up, conv_f_w=m_conv_f_w, w_down=m_w_down, rel_bias=m_rel_bias, final_g=m_final_g)
    mom_v = dict(norm_mix_g=v_norm_mix_g, w_in=v_w_in, conv_a_w=v_conv_a_w, conv_c_w=v_conv_c_w, conv_c_b=v_conv_c_b,
                 ln_c_g=v_ln_c_g, ln_c_b=v_ln_c_b, out_norm_g=v_out_norm_g, w_out=v_w_out, norm_ffn_g=v_norm_ffn_g,
                 w_up=v_w_up, conv_f_w=v_conv_f_w, w_down=v_w_down, rel_bias=v_rel_bias, final_g=v_final_g)
    xi, yi, ci = _me()
    chip = _chip_of(xi, yi)
    c_arr = jnp.reshape(ci, (1,)).astype(I32)
    chip_arr = jnp.reshape(chip, (1,)).astype(I32)
    me_arr = jnp.reshape(4 * xi + 2 * yi + ci, (1,)).astype(I32)
    where = jnp.stack([chip, ci]).astype(I32)
    depth = w_out.shape[0]

    wts = {"in_t": _cast_into_gathered(jnp.swapaxes(w_in, 1, 2), chip_arr, False, "cast_in"),
           "out": _cast_into_gathered(w_out, chip_arr, False, "cast_out"),
           "up": _cast_into_gathered(w_up, chip_arr, True, "cast_up"),
           "down": _cast_into_gathered(w_down, chip_arr, False, "cast_down")}
    store = {}
    _run_comm(_small_gather_comm(_pack([weights[n] for n in SHARDED_SMALL]), store), "ag_small")
    _run_comm(_ag_comm(wts, 0, BIG, BIG), "ag_weights")
    prm = {n: weights[n] for n in SMALL if n not in SHARDED_SMALL}
    per_chip = [_unpack(store["small"][j], [weights[n].shape for n in SHARDED_SMALL]) for j in range(N_CHIPS)]
    for i, n in enumerate(SHARDED_SMALL):
        prm[n] = jnp.concatenate([per_chip[j][i] for j in range(N_CHIPS)], axis=-1)

    sched = _MeshSchedule(wts, depth, c_arr, where)
    loss_row, dx, small, d_rel, d_final = _local_step(x[0], loss_target[0], wts, prm, sched)
    reduced = sched.full
    loss = lax.psum(loss_row[0, 0], ("x", "y", "c"))

    stacked = {n: jnp.stack([small[l][n] for l in range(depth)]) for n in small[0]}
    stacked["rel_bias"] = d_rel
    stacked["final_g"] = d_final
    full_shapes = [stacked[n].shape for n in SMALL]
    partial = _pack([stacked[n] for n in SMALL])
    summed = _unpack(_sum_slabs(_gather_partials(partial), partial, me_arr), full_shapes)
    grads = {}
    for n, g in zip(SMALL, summed):
        if n in SHARDED_SMALL:
            width = weights[n].shape[-1]
            g = lax.dynamic_slice_in_dim(g, chip * width, width, axis=g.ndim - 1)
        grads[n] = g

    shard_shapes = {"in_t": jnp.swapaxes(w_in, 1, 2).shape, "out": w_out.shape, "up": w_up.shape, "down": w_down.shape}
    red = {k: reduced[k].reshape(shard_shapes[k]) for k in BIG}
    grads["w_in"] = jnp.swapaxes(red["in_t"], 1, 2)
    grads["w_out"], grads["w_up"], grads["w_down"] = red["out"], red["up"], red["down"]

    delta, new_m, new_v = {}, {}, {}
    for n in ("w_in", "w_out", "w_up", "w_down"):
        shp = weights[n].shape
        flat = lambda a, shp=shp: a.reshape(shp[0] * shp[1], shp[2])
        tile = max(t for t in range(8, 257, 8) if shp[1] % t == 0)
        d, mn, vn = _adamw(flat(weights[n]), flat(grads[n]), flat(mom_m[n]), flat(mom_v[n]), "adamw_" + n, tile)
        delta[n], new_m[n], new_v[n] = d.reshape(shp), mn.reshape(shp), vn.reshape(shp)
    shapes = [weights[n].shape for n in SMALL]
    packed = [_pack([src[n] for n in SMALL]) for src in (weights, grads, mom_m, mom_v)]
    d, mn, vn = _adamw(*packed, "adamw_small", packed[0].shape[0] // 2)
    for n, a, b, c in zip(SMALL, _unpack(d, shapes), _unpack(mn, shapes), _unpack(vn, shapes)):
        delta[n], new_m[n], new_v[n] = a, b, c

    order = ("norm_mix_g", "w_in", "conv_a_w", "conv_c_w", "conv_c_b", "ln_c_g", "ln_c_b", "out_norm_g", "w_out",
             "norm_ffn_g", "w_up", "conv_f_w", "w_down", "rel_bias", "final_g")
    return (loss, dx[None], *[grads[n] for n in order], *[delta[n] for n in order], *[new_m[n] for n in order],
            *[new_v[n] for n in order])
```

```python
import functools
import math

import numpy as np
import jax
import jax.numpy as jnp
from jax import lax
from jax.experimental import pallas as pl
from jax.experimental.pallas import tpu as pltpu

F32 = jnp.float32
BF16 = jnp.bfloat16
I32 = jnp.int32

EPS = 1e-6
NEG = -1e30
D_HEAD = 64
LANES = 128
BLK = 128
ATTN_GROUP_FWD = 4
ATTN_GROUP_BWD = 4
DILATED_BRANCHES = ((128, 1), (512, 4), (2048, 16))
NUM_BUCKETS = 32
MAX_DISTANCE = 2048
SHORT_CONV = 3
CONFORMER_CONV = 31
FFN_CONV = 3
PAD_SHORT = 8
PAD_LONG = 32
ROW_CHUNK = 256
V7X_VMEM_BYTES = 64 * 1024 * 1024
VMEM_REQUEST = V7X_VMEM_BYTES * 7 // 8

ADAM_LR = 0.001
ADAM_B1 = 0.9
ADAM_B2 = 0.999
ADAM_EPS = 1e-08
ADAM_WD = 0.01
ADAM_STEP = 10

MESH = pl.DeviceIdType.MESH
ANY = pl.BlockSpec(memory_space=pl.ANY)


def _sds(shape, dtype):
    return jax.ShapeDtypeStruct(tuple(shape), dtype)


class _Comm:
    def __init__(self, ins, out_shapes, aliases, sems, start, finish, done):
        self.ins, self.out_shapes, self.aliases, self.sems = list(ins), list(out_shapes), dict(aliases), list(sems)
        self.start, self.finish, self.done = start, finish, done


def _pcall(body, *, name, out_shape, grid=(), in_specs=None, out_specs=None, scratch_shapes=(), vmem=VMEM_REQUEST,
           aliases=None, prefetch=0, comm=None):
    params = pltpu.CompilerParams(dimension_semantics=("arbitrary",) * len(grid), vmem_limit_bytes=vmem)
    single = not isinstance(out_shape, (tuple, list))
    outs = [out_shape] if single else list(out_shape)
    ospecs = [out_specs] if single else list(out_specs)
    ispecs, scratch, aliases = list(in_specs), list(scratch_shapes), dict(aliases or {})
    n_in, n_out, n_scr = len(ispecs), len(outs), len(scratch)
    kernel_body = body
    if comm is not None:
        n_ci, n_co = len(comm.ins), len(comm.out_shapes)

        def kernel_body(*refs):
            pre, rest = refs[:prefetch], refs[prefetch:]
            core_in, c_in = rest[:n_in], rest[n_in:n_in + n_ci]
            o0 = n_in + n_ci
            core_out, c_out = rest[o0:o0 + n_out], rest[o0 + n_out:o0 + n_out + n_co]
            s0 = o0 + n_out + n_co
            core_scr, c_sem = rest[s0:s0 + n_scr], rest[s0 + n_scr:]
            first = functools.reduce(jnp.logical_and, [pl.program_id(a) == 0 for a in range(len(grid))])
            last = functools.reduce(jnp.logical_and, [pl.program_id(a) == grid[a] - 1 for a in range(len(grid))])
            pl.when(first)(lambda: comm.start(c_in, c_out, c_sem))
            body(*pre, *core_in, *core_out, *core_scr)
            pl.when(last)(lambda: comm.finish(c_in, c_out, c_sem))

        for i, o in comm.aliases.items():
            aliases[prefetch + n_in + i] = n_out + o
        ispecs += [ANY] * n_ci
        ospecs += [ANY] * n_co
        outs += comm.out_shapes
        scratch += comm.sems
    if prefetch:
        spec = pltpu.PrefetchScalarGridSpec(num_scalar_prefetch=prefetch, grid=grid, in_specs=ispecs,
                                            out_specs=tuple(ospecs), scratch_shapes=scratch)
        call = pl.pallas_call(kernel_body, name=name, out_shape=tuple(outs), grid_spec=spec,
                              input_output_aliases=aliases, compiler_params=params)
    else:
        call = pl.pallas_call(kernel_body, name=name, out_shape=tuple(outs), grid=grid, in_specs=ispecs,
                              out_specs=tuple(ospecs), scratch_shapes=scratch, input_output_aliases=aliases,
                              compiler_params=params)

    def run(*args):
        res = call(*args, *(comm.ins if comm is not None else ()))
        if comm is not None:
            comm.done(res[n_out:])
        return res[0] if single else tuple(res[:n_out])

    return run


def _run_comm(comm, name):
    def body(*refs):
        n_ci, n_co = len(comm.ins), len(comm.out_shapes)
        c_in, c_out, c_sem = refs[:n_ci], refs[n_ci:n_ci + n_co], refs[n_ci + n_co:]
        comm.start(c_in, c_out, c_sem)
        comm.finish(c_in, c_out, c_sem)

    res = pl.pallas_call(body, name=name, out_shape=tuple(comm.out_shapes), in_specs=[ANY] * len(comm.ins),
                         out_specs=tuple([ANY] * len(comm.out_shapes)), scratch_shapes=comm.sems,
                         input_output_aliases=comm.aliases)(*comm.ins)
    comm.done(res)


def _dot(a, b):
    return lax.dot_general(a, b, (((1,), (0,)), ((), ())), preferred_element_type=F32)


def _dot_nt(a, b):
    return lax.dot_general(a, b, (((1,), (1,)), ((), ())), preferred_element_type=F32)


def _dot_tn(a, b):
    return lax.dot_general(a, b, (((0,), (0,)), ((), ())), preferred_element_type=F32)


def _sigmoid(x):
    return 1.0 / (1.0 + jnp.exp(-x))


def _rstd(x):
    return lax.rsqrt(jnp.mean(x * x, axis=-1, keepdims=True) + EPS)


def _rms_fwd(x, g, name, comm=None):
    s, d = x.shape
    tm = ROW_CHUNK

    def body(x_ref, g_ref, o_ref):
        xv = x_ref[...]
        o_ref[...] = (xv * _rstd(xv) * g_ref[...]).astype(BF16)

    return _pcall(body, name=name, out_shape=_sds((s, d), BF16), grid=(s // tm,),
                  in_specs=[pl.BlockSpec((tm, d), lambda i: (i, 0)), pl.BlockSpec((1, d), lambda i: (0, 0))],
                  out_specs=pl.BlockSpec((tm, d), lambda i: (i, 0)), comm=comm)(x, g)


def _rms_bwd(x, g, dh, dres, name):
    s, d = x.shape
    tm = ROW_CHUNK

    def body(x_ref, g_ref, dh_ref, dres_ref, dx_ref, dxb_ref, dg_ref):
        i = pl.program_id(0)
        xv = x_ref[...]
        r = _rstd(xv)
        xh = xv * r
        dhv = dh_ref[...]
        gd = dhv * g_ref[...]
        dx = dres_ref[...] + r * (gd - xh * jnp.mean(gd * xh, axis=-1, keepdims=True))
        dx_ref[...] = dx
        dxb_ref[...] = dx.astype(BF16)
        part = jnp.sum(dhv * xh, axis=0, keepdims=True)

        @pl.when(i == 0)
        def _():
            dg_ref[...] = part

        @pl.when(i > 0)
        def _():
            dg_ref[...] += part

    row = pl.BlockSpec((tm, d), lambda i: (i, 0))
    vec = pl.BlockSpec((1, d), lambda i: (0, 0))
    return _pcall(body, name=name, out_shape=(_sds((s, d), F32), _sds((s, d), BF16), _sds((1, d), F32)),
                  grid=(s // tm,), in_specs=[row, vec, row, row], out_specs=(row, row, vec))(x, g, dh, dres)


def _final_loss(x, g, tgt, name):
    s, d = x.shape
    tm = ROW_CHUNK

    def body(x_ref, g_ref, t_ref, loss_ref, dx_ref, dxb_ref, dg_ref):
        i = pl.program_id(0)
        xv = x_ref[...]
        r = _rstd(xv)
        xh = xv * r
        e = xh * g_ref[...] - t_ref[...]
        lpart = 0.5 * jnp.sum(jnp.mean(e * e, axis=-1, keepdims=True), axis=0, keepdims=True)
        dy = e * (1.0 / d)
        gd = dy * g_ref[...]
        dx = r * (gd - xh * jnp.mean(gd * xh, axis=-1, keepdims=True))
        dx_ref[...] = dx
        dxb_ref[...] = dx.astype(BF16)
        part = jnp.sum(dy * xh, axis=0, keepdims=True)
        lrow = jnp.broadcast_to(lpart, (1, LANES))

        @pl.when(i == 0)
        def _():
            dg_ref[...] = part
            loss_ref[...] = lrow

        @pl.when(i > 0)
        def _():
            dg_ref[...] += part
            loss_ref[...] += lrow

    row = pl.BlockSpec((tm, d), lambda i: (i, 0))
    vec = pl.BlockSpec((1, d), lambda i: (0, 0))
    return _pcall(body, name=name,
                  out_shape=(_sds((1, LANES), F32), _sds((s, d), F32), _sds((s, d), BF16), _sds((1, d), F32)),
                  grid=(s // tm,), in_specs=[row, vec, row],
                  out_specs=(pl.BlockSpec((1, LANES), lambda i: (0, 0)), row, row, vec))(x, g, tgt)


def _mm_n(a, b, layer, *, nt, tn, out_dtype, name, comm=None):
    s, k = a.shape
    n = b.shape[1] if nt else b.shape[2]
    rows = 512

    def body(a_ref, b_ref, o_ref):
        bv = b_ref[...]
        for r0 in range(0, s, rows):
            av = a_ref[r0:r0 + rows, :]
            o_ref[r0:r0 + rows, :] = (_dot_nt(av, bv) if nt else _dot(av, bv)).astype(out_dtype)

    b_spec = (pl.BlockSpec((None, tn, k), lambda j: (layer, j, 0)) if nt
              else pl.BlockSpec((None, k, tn), lambda j: (layer, 0, j)))
    return _pcall(body, name=name, out_shape=_sds((s, n), out_dtype), grid=(n // tn,),
                  in_specs=[pl.BlockSpec((s, k), lambda j: (0, 0)), b_spec],
                  out_specs=pl.BlockSpec((s, tn), lambda j: (0, j)), comm=comm)(a, b)


def _mm_k(a, b, layer, resid, *, nt, tk, b_off, name, comm=None):
    s, ka = a.shape
    n = b.shape[1] if nt else b.shape[2]
    rows = 512

    def body(a_ref, b_ref, *refs):
        o_ref = refs[-1]
        kk = pl.program_id(0)
        bv = b_ref[...]

        @pl.when(kk == 0)
        def _():
            o_ref[...] = jnp.zeros((s, n), F32) if resid is None else refs[0][...]

        for r0 in range(0, s, rows):
            av = a_ref[r0:r0 + rows, :]
            o_ref[r0:r0 + rows, :] += _dot_nt(av, bv) if nt else _dot(av, bv)

    b_spec = (pl.BlockSpec((None, n, tk), lambda kk: (layer, 0, kk + b_off)) if nt
              else pl.BlockSpec((None, tk, n), lambda kk: (layer, kk + b_off, 0)))
    full = pl.BlockSpec((s, n), lambda kk: (0, 0))
    extra = () if resid is None else (resid,)
    return _pcall(body, name=name, out_shape=_sds((s, n), F32), grid=(ka // tk,),
                  in_specs=[pl.BlockSpec((s, tk), lambda kk: (0, kk)), b_spec] + [full] * len(extra),
                  out_specs=full, comm=comm)(a, b, *extra)


def _mm_tn(a, b, *, t, name):
    s, ka = a.shape
    n = b.shape[1]

    def body(a_ref, b_ref, o_ref):
        o_ref[...] = _dot_tn(a_ref[...], b_ref[...]).astype(BF16)

    return _pcall(body, name=name, out_shape=_sds((ka, n), BF16), grid=(ka // t,),
                  in_specs=[pl.BlockSpec((s, t), lambda i: (0, i)), pl.BlockSpec((s, n), lambda i: (0, 0))],
                  out_specs=pl.BlockSpec((t, n), lambda i: (i, 0)))(a, b)


def _mm_tn2(a, b_lo, b_hi, *, t, name):
    s, ka = a.shape
    half = b_lo.shape[1]
    nb = half // t

    def body(a_ref, lo_ref, hi_ref, o_ref):
        j = pl.program_id(0)

        @pl.when(j < nb)
        def _():
            o_ref[...] = _dot_tn(a_ref[...], lo_ref[...]).astype(BF16)

        @pl.when(j >= nb)
        def _():
            o_ref[...] = _dot_tn(a_ref[...], hi_ref[...]).astype(BF16)

    return _pcall(body, name=name, out_shape=_sds((ka, 2 * half), BF16), grid=(2 * nb,),
                  in_specs=[pl.BlockSpec((s, ka), lambda j: (0, 0)),
                            pl.BlockSpec((s, t), lambda j: (0, jnp.minimum(j, nb - 1))),
                            pl.BlockSpec((s, t), lambda j: (0, jnp.maximum(j - nb, 0)))],
                  out_specs=pl.BlockSpec((ka, t), lambda j: (0, j)))(a, b_lo, b_hi)


def _conv_taps(win, w_ref, width, pad, rows):
    acc = None
    for k in range(width):
        off = pad - (width - 1) + k
        term = w_ref[pl.ds(k, 1), :] * win[off:off + rows, :]
        acc = term if acc is None else acc + term
    return acc


def _conv_taps_t(win, w_ref, width, rows):
    acc = None
    for k in range(width):
        off = (width - 1) - k
        term = w_ref[pl.ds(k, 1), :] * win[off:off + rows, :]
        acc = term if acc is None else acc + term
    return acc


def _conv_wgrad(dw_ref, g, win, width, pad, rows):
    for k in range(width):
        off = pad - (width - 1) + k
        dw_ref[pl.ds(k, 1), :] += jnp.sum(g * win[off:off + rows, :], axis=0, keepdims=True)


def _mixer_a_fwd(ah, ab, ac, win_t, wa_ref, rows):
    ct = _conv_taps(win_t, wa_ref, SHORT_CONV, PAD_SHORT, rows)
    return ab * ct, ct


def _mixer_c_fwd(win_u, wc_ref, cb_ref, lg_ref, lb_ref, rows):
    u = _conv_taps(win_u, wc_ref, CONFORMER_CONV, PAD_LONG, rows) + cb_ref[...]
    mu = jnp.mean(u, axis=-1, keepdims=True)
    uc = u - mu
    rs = lax.rsqrt(jnp.mean(uc * uc, axis=-1, keepdims=True) + EPS)
    uh = uc * rs
    ln = uh * lg_ref[...] + lb_ref[...]
    sg = _sigmoid(ln)
    return ln * sg, ln, sg, uh, rs


def _mix_fwd(z, wa, wc, cb, lg, lb, ga, gc, name):
    s = z.shape[0]
    w = wa.shape[1]
    nblk = z.shape[1] // w
    rc = ROW_CHUNK

    def body(ah_ref, ab_ref, ac_ref, cv_ref, cg_ref, wa_ref, wc_ref, cb_ref, lg_ref, lb_ref, ga_ref, gc_ref,
             ya_ref, yc_ref, tpad, upad):
        tpad[pl.ds(0, PAD_SHORT), :] = jnp.zeros((PAD_SHORT, w), F32)
        upad[pl.ds(0, PAD_LONG), :] = jnp.zeros((PAD_LONG, w), F32)

        def chunk(i, carry):
            base = pl.multiple_of(i * rc, rc)
            rows = pl.ds(base, rc)
            ah, ab, ac = ah_ref[rows, :], ab_ref[rows, :], ac_ref[rows, :]
            tpad[pl.ds(base + PAD_SHORT, rc), :] = ac * ah
            ya, _ = _mixer_a_fwd(ah, ab, ac, tpad[pl.ds(base, rc + PAD_SHORT), :], wa_ref, rc)
            ya_ref[rows, :] = (ya * _rstd(ya) * ga_ref[...]).astype(BF16)
            upad[pl.ds(base + PAD_LONG, rc), :] = cv_ref[rows, :] * _sigmoid(cg_ref[rows, :])
            yc = _mixer_c_fwd(upad[pl.ds(base, rc + PAD_LONG), :], wc_ref, cb_ref, lg_ref, lb_ref, rc)[0]
            yc_ref[rows, :] = (yc * _rstd(yc) * gc_ref[...]).astype(BF16)
            return carry

        lax.fori_loop(0, s // rc, chunk, 0)

    def zblk(j):
        return pl.BlockSpec((s, w), lambda i: (0, j))

    def whole(a):
        return pl.BlockSpec(a.shape, lambda i: (0, 0))

    return _pcall(
        body, name=name, out_shape=(_sds((s, w), BF16), _sds((s, w), BF16)), grid=(1,),
        in_specs=[zblk(0), zblk(1), zblk(2), zblk(nblk - 2), zblk(nblk - 1)] + [whole(a) for a in (wa, wc, cb, lg, lb, ga, gc)],
        out_specs=(pl.BlockSpec((s, w), lambda i: (0, 0)), pl.BlockSpec((s, w), lambda i: (0, 0))),
        scratch_shapes=[pltpu.VMEM((s + PAD_SHORT, w), F32), pltpu.VMEM((s + PAD_LONG, w), F32)],
    )(z, z, z, z, z, wa, wc, cb, lg, lb, ga, gc)


def _mix_bwd(z, dy, wa, wc, cb, lg, lb, ga, gc, name):
    s = z.shape[0]
    w = wa.shape[1]
    nblk = z.shape[1] // w
    nyb = dy.shape[1] // w
    rc = ROW_CHUNK

    def body(ah_ref, ab_ref, ac_ref, cv_ref, cg_ref, dya_ref, dyc_ref,
             wa_ref, wc_ref, cb_ref, lg_ref, lb_ref, ga_ref, gc_ref,
             dza_ref, dzc_ref, dwa_ref, dwc_ref, dcb_ref, dlg_ref, dlb_ref, dga_ref, dgc_ref,
             tpad, upad, dctp, dup):
        tpad[pl.ds(0, PAD_SHORT), :] = jnp.zeros((PAD_SHORT, w), F32)
        upad[pl.ds(0, PAD_LONG), :] = jnp.zeros((PAD_LONG, w), F32)
        dctp[pl.ds(s, PAD_SHORT), :] = jnp.zeros((PAD_SHORT, w), F32)
        dup[pl.ds(s, PAD_LONG), :] = jnp.zeros((PAD_LONG, w), F32)
        for ref in (dwa_ref, dwc_ref, dcb_ref, dlg_ref, dlb_ref, dga_ref, dgc_ref):
            ref[...] = jnp.zeros(ref.shape, F32)

        def rms_bwd(y, g_ref, dyn, dg_ref):
            r = _rstd(y)
            yh = y * r
            gd = dyn * g_ref[...]
            dg_ref[...] += jnp.sum(dyn * yh, axis=0, keepdims=True)
            return r * (gd - yh * jnp.mean(gd * yh, axis=-1, keepdims=True))

        def first(i, carry):
            base = pl.multiple_of(i * rc, rc)
            rows = pl.ds(base, rc)
            ah, ab, ac = ah_ref[rows, :], ab_ref[rows, :], ac_ref[rows, :]
            tpad[pl.ds(base + PAD_SHORT, rc), :] = ac * ah
            win_t = tpad[pl.ds(base, rc + PAD_SHORT), :]
            ya, ct = _mixer_a_fwd(ah, ab, ac, win_t, wa_ref, rc)
            dya = rms_bwd(ya, ga_ref, dya_ref[rows, :], dga_ref)
            dza_ref[rows, w:2 * w] = (dya * ct).astype(BF16)
            dct = dya * ab
            dctp[rows, :] = dct
            _conv_wgrad(dwa_ref, dct, win_t, SHORT_CONV, PAD_SHORT, rc)

            upad[pl.ds(base + PAD_LONG, rc), :] = cv_ref[rows, :] * _sigmoid(cg_ref[rows, :])
            win_u = upad[pl.ds(base, rc + PAD_LONG), :]
            yc, ln, sg, uh, rs = _mixer_c_fwd(win_u, wc_ref, cb_ref, lg_ref, lb_ref, rc)
            dyc = rms_bwd(yc, gc_ref, dyc_ref[rows, :], dgc_ref)
            dln = dyc * (sg * (1.0 + ln * (1.0 - sg)))
            dlg_ref[...] += jnp.sum(dln * uh, axis=0, keepdims=True)
            dlb_ref[...] += jnp.sum(dln, axis=0, keepdims=True)
            duh = dln * lg_ref[...]
            du = rs * (duh - jnp.mean(duh, axis=-1, keepdims=True) - uh * jnp.mean(duh * uh, axis=-1, keepdims=True))
            dcb_ref[...] += jnp.sum(du, axis=0, keepdims=True)
            dup[rows, :] = du
            _conv_wgrad(dwc_ref, du, win_u, CONFORMER_CONV, PAD_LONG, rc)
            return carry

        lax.fori_loop(0, s // rc, first, 0)

        def second(i, carry):
            base = pl.multiple_of(i * rc, rc)
            rows = pl.ds(base, rc)
            dt = _conv_taps_t(dctp[pl.ds(base, rc + PAD_SHORT), :], wa_ref, SHORT_CONV, rc)
            dza_ref[rows, 0:w] = (dt * ac_ref[rows, :]).astype(BF16)
            dza_ref[rows, 2 * w:3 * w] = (dt * ah_ref[rows, :]).astype(BF16)
            du0 = _conv_taps_t(dup[pl.ds(base, rc + PAD_LONG), :], wc_ref, CONFORMER_CONV, rc)
            sg = _sigmoid(cg_ref[rows, :])
            dzc_ref[rows, 0:w] = (du0 * sg).astype(BF16)
            dzc_ref[rows, w:2 * w] = (du0 * cv_ref[rows, :] * sg * (1.0 - sg)).astype(BF16)
            return carry

        lax.fori_loop(0, s // rc, second, 0)

    def blk(j):
        return pl.BlockSpec((s, w), lambda i: (0, j))

    def whole(a):
        return pl.BlockSpec(tuple(a.shape), lambda i: (0, 0))

    params = (wa, wc, cb, lg, lb, ga, gc)
    outs = (_sds((s, 3 * w), BF16), _sds((s, 2 * w), BF16)) + tuple(_sds(p.shape, F32) for p in params)
    return _pcall(
        body, name=name, out_shape=outs, grid=(1,),
        in_specs=[blk(0), blk(1), blk(2), blk(nblk - 2), blk(nblk - 1), blk(0), blk(nyb - 1)] + [whole(p) for p in params],
        out_specs=tuple(whole(o) for o in outs),
        scratch_shapes=[pltpu.VMEM((s + PAD_SHORT, w), F32), pltpu.VMEM((s + PAD_LONG, w), F32),
                        pltpu.VMEM((s + PAD_SHORT, w), F32), pltpu.VMEM((s + PAD_LONG, w), F32)],
    )(z, z, z, z, z, dy, dy, *params)


def _ffn_act_fwd(up, wf, name, comm=None):
    s, f2 = up.shape
    f = f2 // 2
    tc = 256
    nb = f // tc
    rc = ROW_CHUNK

    def body(g_ref, v_ref, wg_ref, wv_ref, o_ref, gpad, vpad):
        gpad[pl.ds(0, PAD_SHORT), :] = jnp.zeros((PAD_SHORT, tc), F32)
        vpad[pl.ds(0, PAD_SHORT), :] = jnp.zeros((PAD_SHORT, tc), F32)

        def chunk(i, carry):
            base = pl.multiple_of(i * rc, rc)
            rows = pl.ds(base, rc)
            gpad[pl.ds(base + PAD_SHORT, rc), :] = g_ref[rows, :].astype(F32)
            vpad[pl.ds(base + PAD_SHORT, rc), :] = v_ref[rows, :].astype(F32)
            gc = _conv_taps(gpad[pl.ds(base, rc + PAD_SHORT), :], wg_ref, FFN_CONV, PAD_SHORT, rc)
            vc = _conv_taps(vpad[pl.ds(base, rc + PAD_SHORT), :], wv_ref, FFN_CONV, PAD_SHORT, rc)
            o_ref[rows, :] = (gc * _sigmoid(gc) * vc).astype(BF16)
            return carry

        lax.fori_loop(0, s // rc, chunk, 0)

    return _pcall(
        body, name=name, out_shape=_sds((s, f), BF16), grid=(nb,),
        in_specs=[pl.BlockSpec((s, tc), lambda j: (0, j)), pl.BlockSpec((s, tc), lambda j: (0, j + nb)),
                  pl.BlockSpec((FFN_CONV, tc), lambda j: (0, j)), pl.BlockSpec((FFN_CONV, tc), lambda j: (0, j + nb))],
        out_specs=pl.BlockSpec((s, tc), lambda j: (0, j)),
        scratch_shapes=[pltpu.VMEM((s + PAD_SHORT, tc), F32), pltpu.VMEM((s + PAD_SHORT, tc), F32)], comm=comm,
    )(up, up, wf, wf)


def _ffn_act_bwd(up, dact, wf, name, comm=None):
    s, f2 = up.shape
    f = f2 // 2
    tc = 256
    nb = f // tc
    rc = ROW_CHUNK

    def body(g_ref, v_ref, da_ref, wg_ref, wv_ref, act_ref, dg_ref, dv_ref, dwg_ref, dwv_ref, gpad, vpad, dgp, dvp):
        gpad[pl.ds(0, PAD_SHORT), :] = jnp.zeros((PAD_SHORT, tc), F32)
        vpad[pl.ds(0, PAD_SHORT), :] = jnp.zeros((PAD_SHORT, tc), F32)
        dgp[pl.ds(s, PAD_SHORT), :] = jnp.zeros((PAD_SHORT, tc), F32)
        dvp[pl.ds(s, PAD_SHORT), :] = jnp.zeros((PAD_SHORT, tc), F32)
        dwg_ref[...] = jnp.zeros((FFN_CONV, tc), F32)
        dwv_ref[...] = jnp.zeros((FFN_CONV, tc), F32)

        def first(i, carry):
            base = pl.multiple_of(i * rc, rc)
            rows = pl.ds(base, rc)
            gpad[pl.ds(base + PAD_SHORT, rc), :] = g_ref[rows, :].astype(F32)
            vpad[pl.ds(base + PAD_SHORT, rc), :] = v_ref[rows, :].astype(F32)
            win_g = gpad[pl.ds(base, rc + PAD_SHORT), :]
            win_v = vpad[pl.ds(base, rc + PAD_SHORT), :]
            gc = _conv_taps(win_g, wg_ref, FFN_CONV, PAD_SHORT, rc)
            vc = _conv_taps(win_v, wv_ref, FFN_CONV, PAD_SHORT, rc)
            sg = _sigmoid(gc)
            silu = gc * sg
            act_ref[rows, :] = (silu * vc).astype(BF16)
            da = da_ref[rows, :].astype(F32)
            dgc = da * vc * (sg * (1.0 + gc * (1.0 - sg)))
            dvc = da * silu
            dgp[rows, :] = dgc
            dvp[rows, :] = dvc
            _conv_wgrad(dwg_ref, dgc, win_g, FFN_CONV, PAD_SHORT, rc)
            _conv_wgrad(dwv_ref, dvc, win_v, FFN_CONV, PAD_SHORT, rc)
            return carry

        lax.fori_loop(0, s // rc, first, 0)

        def second(i, carry):
            base = pl.multiple_of(i * rc, rc)
            rows = pl.ds(base, rc)
            dg_ref[rows, :] = _conv_taps_t(dgp[pl.ds(base, rc + PAD_SHORT), :], wg_ref, FFN_CONV, rc).astype(BF16)
            dv_ref[rows, :] = _conv_taps_t(dvp[pl.ds(base, rc + PAD_SHORT), :], wv_ref, FFN_CONV, rc).astype(BF16)
            return carry

        lax.fori_loop(0, s // rc, second, 0)

    lo = pl.BlockSpec((s, tc), lambda j: (0, j))
    hi = pl.BlockSpec((s, tc), lambda j: (0, j + nb))
    wlo = pl.BlockSpec((FFN_CONV, tc), lambda j: (0, j))
    whi = pl.BlockSpec((FFN_CONV, tc), lambda j: (0, j + nb))
    act, dgate, dval, dwg, dwv = _pcall(
        body, name=name,
        out_shape=(_sds((s, f), BF16), _sds((s, f), BF16), _sds((s, f), BF16), _sds((FFN_CONV, f), F32), _sds((FFN_CONV, f), F32)),
        grid=(nb,), in_specs=[lo, hi, lo, wlo, whi], out_specs=(lo, lo, lo, wlo, wlo),
        scratch_shapes=[pltpu.VMEM((s + PAD_SHORT, tc), F32) for _ in range(4)], comm=comm,
    )(up, up, dact, wf, wf)
    return act, dgate, dval, jnp.concatenate([dwg, dwv], axis=1)


def _y_assemble(yan, yb, ycn, gb, name):
    s, w = yan.shape
    wb = yb.shape[1]
    tm = ROW_CHUNK

    def body(ya_ref, yb_ref, yc_ref, g_ref, o_ref):
        ybv = yb_ref[...]
        o_ref[:, 0:w] = ya_ref[...]
        o_ref[:, w:w + wb] = (ybv * _rstd(ybv) * g_ref[...]).astype(BF16)
        o_ref[:, w + wb:] = yc_ref[...]

    return _pcall(body, name=name, out_shape=_sds((s, 2 * w + wb), BF16), grid=(s // tm,),
                  in_specs=[pl.BlockSpec((tm, w), lambda i: (i, 0)), pl.BlockSpec((tm, wb), lambda i: (i, 0)),
                            pl.BlockSpec((tm, w), lambda i: (i, 0)), pl.BlockSpec((1, wb), lambda i: (0, 0))],
                  out_specs=pl.BlockSpec((tm, 2 * w + wb), lambda i: (i, 0)))(yan, yb, ycn, gb)


def _yb_norm_bwd(yb, dy, gb, name):
    s, wb = yb.shape
    w = wb // 2
    heads = wb // D_HEAD
    tm = ROW_CHUNK

    def body(yb_ref, d1_ref, d2_ref, g_ref, dyb_ref, dl_ref, dg_ref):
        i = pl.program_id(0)
        y = yb_ref[...]
        dyn = jnp.concatenate([d1_ref[...], d2_ref[...]], axis=1)
        r = _rstd(y)
        yh = y * r
        gd = dyn * g_ref[...]
        dyb = r * (gd - yh * jnp.mean(gd * yh, axis=-1, keepdims=True))
        dyb_ref[...] = dyb
        part = jnp.sum(dyn * yh, axis=0, keepdims=True)
        prod = dyb * y
        even = lax.broadcasted_iota(I32, (tm, LANES), 1) < D_HEAD
        for p in range(heads // 2):
            blk = prod[:, p * LANES:(p + 1) * LANES]
            ev = jnp.sum(jnp.where(even, blk, 0.0), axis=1, keepdims=True)
            od = jnp.sum(jnp.where(even, 0.0, blk), axis=1, keepdims=True)
            dl_ref[2 * p] = jnp.broadcast_to(ev, (tm, LANES))
            dl_ref[2 * p + 1] = jnp.broadcast_to(od, (tm, LANES))

        @pl.when(i == 0)
        def _():
            dg_ref[...] = part

        @pl.when(i > 0)
        def _():
            dg_ref[...] += part

    return _pcall(
        body, name=name, out_shape=(_sds((s, wb), F32), _sds((heads, s, LANES), F32), _sds((1, wb), F32)),
        grid=(s // tm,),
        in_specs=[pl.BlockSpec((tm, wb), lambda i: (i, 0)), pl.BlockSpec((tm, w), lambda i: (i, 1)),
                  pl.BlockSpec((tm, w), lambda i: (i, 2)), pl.BlockSpec((1, wb), lambda i: (0, 0))],
        out_specs=(pl.BlockSpec((tm, wb), lambda i: (i, 0)), pl.BlockSpec((heads, tm, LANES), lambda i: (0, i, 0)),
                   pl.BlockSpec((1, wb), lambda i: (0, 0))),
    )(yb, dy, dy, gb)


def _t5_bucket_table():
    max_exact = NUM_BUCKETS // 2
    out = np.full((len(DILATED_BRANCHES), BLK, 2 * BLK), -1, np.int32)
    rel = np.arange(BLK)[:, None] - np.arange(2 * BLK)[None, :] + BLK
    for b, (window, dilation) in enumerate(DILATED_BRANCHES):
        n_keys = window // dilation
        dist = np.maximum(rel, 0) * dilation
        d_f = np.maximum(dist, 1).astype(np.float32)
        large = max_exact + (np.log(d_f / np.float32(max_exact)) / np.float32(math.log(MAX_DISTANCE / max_exact))
                             * np.float32(NUM_BUCKETS - max_exact)).astype(np.int32)
        large = np.minimum(large, NUM_BUCKETS - 1)
        bucket = np.where(dist < max_exact, dist, large)
        out[b] = np.where((rel >= 0) & (rel <= n_keys), bucket, -1)
    return out


def _bias_tiles(rel_bias, buckets, name):
    nbk, heads = rel_bias.shape
    nbr = buckets.shape[0]

    def body(rb_ref, bk_ref, o_ref):
        for br in range(nbr):
            bk = bk_ref[br]
            tiles = [jnp.full((BLK, 2 * BLK), NEG, F32) for _ in range(heads)]
            for b in range(nbk):
                hit = bk == b
                tiles = [jnp.where(hit, rb_ref[b, h], tiles[h]) for h in range(heads)]
            for h in range(heads):
                o_ref[br, h] = tiles[h]

    return _pcall(body, name=name, out_shape=_sds((nbr, heads, BLK, 2 * BLK), F32), grid=(1,),
                  in_specs=[pl.BlockSpec(memory_space=pltpu.SMEM), pl.BlockSpec(buckets.shape, lambda i: (0, 0, 0))],
                  out_specs=pl.BlockSpec((nbr, heads, BLK, 2 * BLK), lambda i: (0, 0, 0, 0)))(rel_bias, buckets)


def _bias_grad(dtiles, buckets, nbk, name):
    nbr, heads = dtiles.shape[:2]

    def body(dt_ref, bk_ref, o_ref):
        row = lax.broadcasted_iota(I32, (nbk, LANES), 0)
        col = lax.broadcasted_iota(I32, (nbk, LANES), 1)
        out = jnp.zeros((nbk, LANES), F32)
        for h in range(heads):
            for b in range(nbk):
                tot = jnp.zeros((), F32)
                for br in range(nbr):
                    tot = tot + jnp.sum(jnp.where(bk_ref[br] == b, dt_ref[br, h], 0.0))
                out = jnp.where((row == b) & (col == h), tot, out)
        o_ref[...] = out

    return _pcall(body, name=name, out_shape=_sds((nbk, LANES), F32), grid=(1,),
                  in_specs=[pl.BlockSpec(dtiles.shape, lambda i: (0, 0, 0, 0)), pl.BlockSpec(buckets.shape, lambda i: (0, 0, 0))],
                  out_specs=pl.BlockSpec((nbk, LANES), lambda i: (0, 0)))(dtiles, buckets)


def _largest_divisor(n, cap):
    return max(g for g in range(1, cap + 1) if n % g == 0)


def _attn_blocks(s, visit, group):
    for br, (window, d) in enumerate(DILATED_BRANCHES):
        n_blk = (s // d) // BLK
        span = BLK * d
        g1 = _largest_divisor(d, group)

        def firsts(t, carry, br=br, d=d, g1=g1):
            for j in range(g1):
                visit(br, d, t * g1 + j, False)
            return carry

        lax.fori_loop(0, d // g1, firsts, 0)
        if n_blk > 1:
            total = d * (n_blk - 1)
            g2 = _largest_divisor(total, group)

            def rest(t, carry, br=br, d=d, n_blk=n_blk, span=span, g2=g2):
                for j in range(g2):
                    idx = t * g2 + j
                    visit(br, d, idx // (n_blk - 1) + (1 + idx % (n_blk - 1)) * span, True)
                return carry

            lax.fori_loop(0, total // g2, rest, 0)


def _rows(start, size, d):
    return pl.ds(pl.multiple_of(start, BLK), size) if d == 1 else pl.ds(start, size, stride=d)


def _attn_fwd(z, btiles, col0, name, comm=None):
    s = z.shape[0]
    nbr, heads = btiles.shape[:2]
    pairs = heads // 2
    scale = D_HEAD ** -0.5
    rc = ROW_CHUNK

    def body(q_ref, k_ref, v_ref, bt_ref, yb_ref, lse_ref, acc_ref, m_ref, l_ref):
        even = lax.broadcasted_iota(I32, (BLK, LANES), 1) < D_HEAD
        even2 = lax.broadcasted_iota(I32, (2 * BLK, LANES), 1) < D_HEAD

        def visit(br, d, start, prev):
            kw = 2 * BLK if prev else BLK
            rows_q = _rows(start, BLK, d)
            rows_k = _rows(start - BLK * d, kw, d) if prev else rows_q
            qb = q_ref[rows_q, :]
            kb = k_ref[rows_k, :].astype(BF16)
            vw = v_ref[rows_k, :]
            ev_k = even2 if prev else even
            qm = jnp.concatenate([jnp.where(even, qb, 0.0), jnp.where(even, 0.0, qb)], axis=0).astype(BF16)
            bias = [bt_ref[br, e] if prev else bt_ref[br, e, :, BLK:] for e in range(2)]
            sc = _dot_nt(qm, kb) * scale + jnp.concatenate(bias, axis=0)
            m = jnp.max(sc, axis=1, keepdims=True)
            p = jnp.exp(sc - m)
            l = jnp.sum(p, axis=1, keepdims=True)
            pb = p.astype(BF16)
            vm = jnp.concatenate([jnp.where(ev_k, vw, 0.0), jnp.where(ev_k, 0.0, vw)], axis=0).astype(BF16)
            acc_ref.at[br][rows_q, :] = _dot(jnp.concatenate([pb[:BLK], pb[BLK:]], axis=1), vm)
            for e in range(2):
                m_ref.at[br, e][rows_q, :] = jnp.broadcast_to(m[e * BLK:(e + 1) * BLK], (BLK, LANES))
                l_ref.at[br, e][rows_q, :] = jnp.broadcast_to(l[e * BLK:(e + 1) * BLK], (BLK, LANES))

        _attn_blocks(s, visit, ATTN_GROUP_FWD)

        ev_c = lax.broadcasted_iota(I32, (rc, LANES), 1) < D_HEAD

        def merge(i, carry):
            rows = pl.ds(pl.multiple_of(i * rc, rc), rc)
            wts, dens = [], []
            for e in range(2):
                ms = [m_ref[br, e, rows, :] for br in range(nbr)]
                top = functools.reduce(jnp.maximum, ms)
                w = [jnp.exp(mb - top) for mb in ms]
                den = functools.reduce(lambda a, b: a + b, [w[br] * l_ref[br, e, rows, :] for br in range(nbr)])
                lse_ref[e, rows, :] = top + jnp.log(den)
                wts.append(w)
                dens.append(den)
            num = functools.reduce(lambda a, b: a + b,
                                   [jnp.where(ev_c, wts[0][br], wts[1][br]) * acc_ref[br, rows, :] for br in range(nbr)])
            yb_ref[rows, :] = num / jnp.where(ev_c, dens[0], dens[1])
            return carry

        lax.fori_loop(0, s // rc, merge, 0)

    def zcol(j):
        return pl.BlockSpec((s, LANES), lambda p, j=j: (0, col0 + j + p))

    return _pcall(
        body, name=name, out_shape=(_sds((s, pairs * LANES), F32), _sds((heads, s, LANES), F32)), grid=(pairs,),
        in_specs=[zcol(0), zcol(pairs), zcol(2 * pairs), pl.BlockSpec((nbr, 2, BLK, 2 * BLK), lambda p: (0, p, 0, 0))],
        out_specs=(pl.BlockSpec((s, LANES), lambda p: (0, p)), pl.BlockSpec((2, s, LANES), lambda p: (p, 0, 0))),
        scratch_shapes=[pltpu.VMEM((nbr, s, LANES), F32), pltpu.VMEM((nbr, 2, s, LANES), F32), pltpu.VMEM((nbr, 2, s, LANES), F32)],
        comm=comm,
    )(z, z, z, btiles)


def _attn_bwd(z, btiles, dyb, lse, delta, dbias_in, col0, name, comm=None):
    s = z.shape[0]
    nbr, heads = btiles.shape[:2]
    pairs = heads // 2
    scale = D_HEAD ** -0.5

    def body(q_ref, k_ref, v_ref, bt_ref, dy_ref, lse_ref, dl_ref, dbi_ref,
             dq_ref, dk_ref, dv_ref, db_ref, dqa, dka, dva):
        even = lax.broadcasted_iota(I32, (BLK, LANES), 1) < D_HEAD
        even2 = lax.broadcasted_iota(I32, (2 * BLK, LANES), 1) < D_HEAD
        for ref in (dqa, dka, dva):
            ref[...] = jnp.zeros((s, LANES), F32)
        db_ref[...] = dbi_ref[...]

        def visit(br, d, start, prev):
            kw = 2 * BLK if prev else BLK
            rows_q = _rows(start, BLK, d)
            rows_k = _rows(start - BLK * d, kw, d) if prev else rows_q
            qb = q_ref[rows_q, :]
            dyv = dy_ref[rows_q, :]
            kwin = k_ref[rows_k, :]
            kb = kwin.astype(BF16)
            vb = v_ref[rows_k, :].astype(BF16)
            ev_k = even2 if prev else even
            qm = jnp.concatenate([jnp.where(even, qb, 0.0), jnp.where(even, 0.0, qb)], axis=0).astype(BF16)
            dym = jnp.concatenate([jnp.where(even, dyv, 0.0), jnp.where(even, 0.0, dyv)], axis=0).astype(BF16)
            bias = [bt_ref[br, e] if prev else bt_ref[br, e, :, BLK:] for e in range(2)]
            sc = _dot_nt(qm, kb) * scale + jnp.concatenate(bias, axis=0)
            lt = jnp.concatenate([lse_ref.at[e][rows_q, :] for e in range(2)], axis=0)
            dt = jnp.concatenate([dl_ref.at[e][rows_q, :] for e in range(2)], axis=0)
            if prev:
                lt = jnp.concatenate([lt, lt], axis=1)
                dt = jnp.concatenate([dt, dt], axis=1)
            p = jnp.exp(sc - lt)
            ds = p * (_dot_nt(dym, vb) - dt)
            for e in range(2):
                if prev:
                    db_ref[br, e] += ds[e * BLK:(e + 1) * BLK]
                else:
                    db_ref[br, e, :, BLK:] += ds[e * BLK:(e + 1) * BLK]
            dsb = ds.astype(BF16)
            km = jnp.concatenate([jnp.where(ev_k, kwin, 0.0), jnp.where(ev_k, 0.0, kwin)], axis=0).astype(BF16)
            dqa[rows_q, :] += _dot(jnp.concatenate([dsb[:BLK], dsb[BLK:]], axis=1), km) * scale
            dka[rows_k, :] += _dot_tn(dsb, qm) * scale
            dva[rows_k, :] += _dot_tn(p.astype(BF16), dym)

        _attn_blocks(s, visit, ATTN_GROUP_BWD)
        dq_ref[...] = dqa[...].astype(BF16)
        dk_ref[...] = dka[...].astype(BF16)
        dv_ref[...] = dva[...].astype(BF16)

    def zcol(j):
        return pl.BlockSpec((s, LANES), lambda p, j=j: (0, col0 + j + p))

    col = pl.BlockSpec((s, LANES), lambda p: (0, p))
    stat = pl.BlockSpec((2, s, LANES), lambda p: (p, 0, 0))
    tile = pl.BlockSpec((nbr, 2, BLK, 2 * BLK), lambda p: (0, p, 0, 0))
    wide = _sds((s, pairs * LANES), BF16)
    return _pcall(
        body, name=name, out_shape=(wide, wide, wide, _sds(btiles.shape, F32)), grid=(pairs,),
        in_specs=[zcol(0), zcol(pairs), zcol(2 * pairs), tile, col, stat, stat, tile],
        out_specs=(col, col, col, tile),
        scratch_shapes=[pltpu.VMEM((s, LANES), F32) for _ in range(3)], comm=comm,
    )(z, z, z, btiles, dyb, lse, delta, dbias_in)


def _row(v):
    return v.reshape(1, -1)


class _LocalSchedule:
    def __init__(self):
        self.big = {}

    def fwd_comms(self, l):
        return {}

    def bwd_comms(self, l):
        return {}

    def after_bwd(self, l, grads):
        self.big[l] = grads


def _layer_fwd(l, x, wts, prm, btiles, comms):
    d = x.shape[1]
    wq = d // 4
    gout = prm["out_norm_g"][l]
    h = _rms_fwd(x, _row(prm["norm_mix_g"][l]), "rms_mix_fwd", comm=comms.get("rms_mix_fwd"))
    z = _mm_n(h, wts["in_t"], l, nt=True, tn=256, out_dtype=F32, name="in_proj")
    yan, ycn = _mix_fwd(z, prm["conv_a_w"][l], prm["conv_c_w"][l], _row(prm["conv_c_b"][l]), _row(prm["ln_c_g"][l]),
                        _row(prm["ln_c_b"][l]), _row(gout[:wq]), _row(gout[3 * wq:]), "mix_fwd")
    yb, lse = _attn_fwd(z, btiles, 3 * wq // LANES, "attn_fwd", comm=comms.get("attn_fwd"))
    y = _y_assemble(yan, yb, ycn, _row(gout[wq:3 * wq]), "y_assemble")
    x_mid = _mm_k(y, wts["out"], l, x, nt=False, tk=512, b_off=0, name="out_proj")
    h2 = _rms_fwd(x_mid, _row(prm["norm_ffn_g"][l]), "rms_ffn_fwd")
    up = _mm_n(h2, wts["up"], l, nt=False, tn=512, out_dtype=BF16, name="up_proj")
    act = _ffn_act_fwd(up, prm["conv_f_w"][l], "ffn_act_fwd", comm=comms.get("ffn_act_fwd"))
    x_out = _mm_k(act, wts["down"], l, x_mid, nt=False, tk=256, b_off=0, name="down_proj")
    return x_out, (x, h, z, yb, lse, y, x_mid, h2, up)


def _layer_bwd(l, dxo, dxo_b, saved, wts, prm, btiles, dbias, comms):
    x, h, z, yb, lse, y, x_mid, h2, up = saved
    d = x.shape[1]
    wq = d // 4
    f = up.shape[1] // 2
    gout = prm["out_norm_g"][l]
    dact = _mm_n(dxo_b, wts["down"], l, nt=True, tn=256, out_dtype=BF16, name="down_proj_dx")
    act, dgate, dval, dwf = _ffn_act_bwd(up, dact, prm["conv_f_w"][l], "ffn_act_bwd", comm=comms.get("ffn_act_bwd"))
    g_down = _mm_tn(act, dxo_b, t=256, name="down_proj_dw")
    dh2 = _mm_k(dgate, wts["up"], l, None, nt=True, tk=256, b_off=0, name="up_proj_dx_gate")
    dh2 = _mm_k(dval, wts["up"], l, dh2, nt=True, tk=256, b_off=f // 256, name="up_proj_dx_val")
    dxm, dxm_b, dg_ffn = _rms_bwd(x_mid, _row(prm["norm_ffn_g"][l]), dh2, dxo, "rms_ffn_bwd")
    g_up = _mm_tn2(h2, dgate, dval, t=256, name="up_proj_dw")
    dy = _mm_k(dxm_b, wts["out"], l, None, nt=True, tk=512, b_off=0, name="out_proj_dx")
    g_out = _mm_tn(y, dxm_b, t=256, name="out_proj_dw")
    dza, dzc, dwa, dwc, dcb, dlg, dlb, dga, dgc = _mix_bwd(
        z, dy, prm["conv_a_w"][l], prm["conv_c_w"][l], _row(prm["conv_c_b"][l]), _row(prm["ln_c_g"][l]),
        _row(prm["ln_c_b"][l]), _row(gout[:wq]), _row(gout[3 * wq:]), "mix_bwd")
    dyb, delta, dgb = _yb_norm_bwd(yb, dy, _row(gout[wq:3 * wq]), "yb_norm_bwd")
    dq, dk, dv, dbias = _attn_bwd(z, btiles, dyb, lse, delta, dbias, 3 * wq // LANES, "attn_bwd",
                                  comm=comms.get("attn_bwd"))
    dz = jnp.concatenate([dza, dq, dk, dv, dzc], axis=1)
    dh = _mm_k(dz, wts["in_t"], l, None, nt=False, tk=256, b_off=0, name="in_proj_dx")
    dx, dx_b, dg_mix = _rms_bwd(x, _row(prm["norm_mix_g"][l]), dh, dxm, "rms_mix_bwd")
    g_in_t = _mm_tn(dz, h, t=256, name="in_proj_dw")
    big = {"in_t": g_in_t, "out": g_out, "up": g_up, "down": g_down}
    small = {"norm_mix_g": dg_mix[0], "conv_a_w": dwa, "conv_c_w": dwc, "conv_c_b": dcb[0], "ln_c_g": dlg[0],
             "ln_c_b": dlb[0], "out_norm_g": jnp.concatenate([dga[0], dgb[0], dgc[0]]), "norm_ffn_g": dg_ffn[0],
             "conv_f_w": dwf}
    return dx, dx_b, big, small, dbias


def _local_step(x, tgt, wts, prm, sched):
    depth = prm["norm_mix_g"].shape[0]
    buckets = jnp.asarray(_t5_bucket_table())
    btiles = _bias_tiles(prm["rel_bias"], buckets, "bias_tiles")
    saved = []
    for l in range(depth):
        x, sv = _layer_fwd(l, x, wts, prm, btiles, sched.fwd_comms(l))
        saved.append(sv)
    loss, dx, dx_b, dg_final = _final_loss(x, _row(prm["final_g"]), tgt, "final_loss")
    dbias = jnp.zeros(btiles.shape, F32)
    small = [None] * depth
    for l in reversed(range(depth)):
        dx, dx_b, grads, small[l], dbias = _layer_bwd(l, dx, dx_b, saved[l], wts, prm, btiles, dbias, sched.bwd_comms(l))
        sched.after_bwd(l, grads)
    nbk, heads = prm["rel_bias"].shape
    d_rel = _bias_grad(dbias, buckets, nbk, "bias_grad")[:, :heads]
    return loss, dx, small, d_rel, dg_final[0]


BIG = ("in_t", "out", "up", "down")
COL_SHARDED = ("up",)
N_CHIPS = 4
N_DEV = 8
BF16_ROWS = 16


def _me():
    return lax.axis_index("x"), lax.axis_index("y"), lax.axis_index("c")


def _chip_of(x, y):
    return 2 * x + y


def _other_chips(x, y):
    return ((1 - x, y), (x, 1 - y), (1 - x, 1 - y))


def _remote(src, dst, send_sem, recv_sem, device):
    return pltpu.make_async_remote_copy(src_ref=src, dst_ref=dst, send_sem=send_sem, recv_sem=recv_sem,
                                        device_id=device, device_id_type=MESH)


def _ag_comm(wts, layer, ici_keys, fwd_keys):
    keys = tuple(k for k in BIG if k in ici_keys or k in fwd_keys)

    def geo(k):
        _, rows, cols = wts[k].shape
        return (rows, cols // N_CHIPS) if k in COL_SHARDED else (rows // N_CHIPS, cols)

    def copies(refs, sems):
        g = dict(zip(keys, refs))
        isend, irecv, dsend, drecv = sems
        x, y, c = _me()
        mine = _chip_of(x, y)

        def region(k, chip, half):
            r, cc = geo(k)
            h = r // 2
            if k in COL_SHARDED:
                return g[k].at[layer, pl.ds(pl.multiple_of(half * h, BF16_ROWS), h), pl.ds(pl.multiple_of(chip * cc, LANES), cc)]
            return g[k].at[layer, pl.ds(pl.multiple_of(chip * r + half * h, BF16_ROWS), h), :]

        def ici(k, f, landing):
            chip = _other_chips(x, y)[f]
            where = region(k, _chip_of(*chip) if landing else mine, c)
            i = keys.index(k)
            return _remote(where, where, isend.at[i, f], irecv.at[i, f], (*chip, c))

        def fwd(k, f, landing):
            chip = _other_chips(x, y)[f]
            where = region(k, _chip_of(*chip), 1 - c if landing else c)
            i = keys.index(k)
            return _remote(where, where, dsend.at[i, f], drecv.at[i, f], (x, y, 1 - c))

        return ici, fwd

    def start(ins, outs, sems):
        ici, fwd = copies(outs, sems)
        for k in keys:
            for f in range(3):
                if k in ici_keys:
                    ici(k, f, False).start()
                else:
                    fwd(k, f, False).start()

    def finish(ins, outs, sems):
        ici, fwd = copies(outs, sems)
        for k in keys:
            for f in range(3):
                if k in ici_keys:
                    ici(k, f, True).wait_recv()
                    if k in fwd_keys:
                        fwd(k, f, False).start()
        for k in keys:
            for f in range(3):
                if k in fwd_keys:
                    fwd(k, f, True).wait_recv()
                    fwd(k, f, False).wait_send()
                if k in ici_keys:
                    ici(k, f, False).wait_send()

    def done(res):
        wts.update(zip(keys, res))

    n = len(keys)
    return _Comm([wts[k] for k in keys], [_sds(wts[k].shape, BF16) for k in keys], {i: i for i in range(n)},
                 [pltpu.SemaphoreType.DMA((n, 3)) for _ in range(4)], start, finish, done)


def _small_gather_comm(slab, store):
    def copies(ins, outs, sems):
        send, recv, lsem = sems
        x, y, c = _me()
        mine = _chip_of(x, y)
        own = pltpu.make_async_copy(ins[0], outs[0].at[mine], lsem)
        pairs = []
        for f, chip in enumerate(_other_chips(x, y)):
            out = _remote(ins[0], outs[0].at[mine], send.at[f], recv.at[f], (*chip, c))
            land = _remote(ins[0], outs[0].at[_chip_of(*chip)], send.at[f], recv.at[f], (*chip, c))
            pairs.append((out, land))
        return own, pairs

    def start(ins, outs, sems):
        own, pairs = copies(ins, outs, sems)
        own.start()
        for out, _ in pairs:
            out.start()

    def finish(ins, outs, sems):
        own, pairs = copies(ins, outs, sems)
        for out, land in pairs:
            land.wait_recv()
            out.wait_send()
        own.wait()

    def done(res):
        store["small"] = res[0]

    return _Comm([slab], [_sds((N_CHIPS,) + slab.shape, F32)], {},
                 [pltpu.SemaphoreType.DMA((3,)), pltpu.SemaphoreType.DMA((3,)), pltpu.SemaphoreType.DMA], start, finish, done)


def _piece_geo(g):
    geo = {}
    for k in BIG:
        rows, cols = g[k].shape
        geo[k] = (rows // 2, cols // N_CHIPS) if k in COL_SHARDED else (rows // (2 * N_CHIPS), cols)
    return geo


def _rs_swap_halves(g):
    geo = _piece_geo(g)
    n_copies = sum(N_CHIPS if k in COL_SHARDED else 1 for k in BIG)

    def body(*refs):
        g_refs = dict(zip(BIG, refs[:len(BIG)]))
        t_refs = dict(zip(BIG, refs[len(BIG):2 * len(BIG)]))
        send, recv = refs[2 * len(BIG):]
        x, y, c = _me()
        copies = []
        for k in BIG:
            h, cc = geo[k]
            if k in COL_SHARDED:
                rows = pl.ds(pl.multiple_of((1 - c) * h, BF16_ROWS), h)
                for j in range(N_CHIPS):
                    copies.append((g_refs[k].at[rows, pl.ds(j * cc, cc)], t_refs[k].at[j]))
            else:
                copies.append((g_refs[k].at[:, 1 - c], t_refs[k]))
        started = []
        for i, (src, dst) in enumerate(copies):
            cp = _remote(src, dst, send.at[i], recv.at[i], (x, y, 1 - c))
            cp.start()
            started.append(cp)
        for cp in started:
            cp.wait()

    ins = [g[k] if k in COL_SHARDED else g[k].reshape(N_CHIPS, 2, geo[k][0], geo[k][1]) for k in BIG]
    outs = tuple(_sds((N_CHIPS,) + geo[k], BF16) for k in BIG)
    res = _pcall(body, name="rs_swap_halves", out_shape=outs, in_specs=[ANY] * len(ins), out_specs=tuple([ANY] * len(outs)),
                 scratch_shapes=[pltpu.SemaphoreType.DMA((n_copies,)) for _ in range(2)])(*ins)
    return dict(zip(BIG, res))


def _pair_sum(g, theirs, c_arr):
    geo = _piece_geo(g)

    def body(c_ref, *refs):
        nk = len(BIG)
        for i in range(nk):
            refs[2 * nk + i][...] = (refs[i][...].astype(F32) + refs[nk + i][...].astype(F32)).astype(BF16)

    in_specs, ins = [], []
    for k in BIG:
        h, cc = geo[k]
        if k in COL_SHARDED:
            in_specs.append(pl.BlockSpec((h, cc), lambda j, c_ref: (c_ref[0], j)))
            ins.append(g[k])
        else:
            in_specs.append(pl.BlockSpec((None, h, cc), lambda j, c_ref: (2 * j + c_ref[0], 0, 0)))
            ins.append(g[k].reshape(2 * N_CHIPS, h, cc))
    slab = [pl.BlockSpec((None,) + geo[k], lambda j, c_ref: (j, 0, 0)) for k in BIG]
    res = _pcall(body, name="rs_pair_sum", out_shape=tuple(_sds((N_CHIPS,) + geo[k], BF16) for k in BIG), grid=(N_CHIPS,),
                 in_specs=in_specs + slab, out_specs=tuple(slab), prefetch=1)(c_arr, *ins, *[theirs[k] for k in BIG])
    return dict(zip(BIG, res))


def _rs_comm(p, keys, store):
    def copies(ins, outs, sems):
        send, recv = sems
        x, y, c = _me()
        return [_remote(ins[i].at[_chip_of(*chip)], outs[i].at[f], send.at[i, f], recv.at[i, f], (*chip, c))
                for i in range(len(keys)) for f, chip in enumerate(_other_chips(x, y))]

    def start(ins, outs, sems):
        for cp in copies(ins, outs, sems):
            cp.start()

    def finish(ins, outs, sems):
        for cp in copies(ins, outs, sems):
            cp.wait()

    def done(res):
        store.update(zip(keys, res))

    return _Comm([p[k] for k in keys], [_sds((3,) + p[k].shape[1:], BF16) for k in keys], {},
                 [pltpu.SemaphoreType.DMA((len(keys), 3)) for _ in range(2)], start, finish, done)


def _quad_sum(p, b, where, l, full):
    parts = 2
    nk = len(BIG)

    def body(where_ref, *refs):
        for i in range(nk):
            acc = refs[i][...].astype(F32)
            for f in range(3):
                acc = acc + refs[nk + 3 * i + f][...].astype(F32)
            refs[5 * nk + i][...] = acc

    own, recv, outs = [], [], []
    for k in BIG:
        h, cc = p[k].shape[1:]
        th = h // parts
        own.append(pl.BlockSpec((None, th, cc), lambda i, w_ref: (w_ref[0], i, 0)))
        recv += [pl.BlockSpec((None, th, cc), lambda i, w_ref, f=f: (f, i, 0)) for f in range(3)]
        outs.append(pl.BlockSpec((None, None, th, cc), lambda i, w_ref: (l, w_ref[1], i, 0)))
    args = [p[k] for k in BIG] + [b[k] for k in BIG for _ in range(3)] + [full[k] for k in BIG]
    res = _pcall(body, name="rs_quad_sum", out_shape=tuple(_sds(full[k].shape, F32) for k in BIG), grid=(parts,),
                 in_specs=own + recv + [ANY] * nk, out_specs=tuple(outs), prefetch=1,
                 aliases={1 + 4 * nk + i: i for i in range(nk)})(where, *args)
    return dict(zip(BIG, res))


def _rs_share(l, full):
    nk = len(BIG)

    def body(*refs):
        f_refs = refs[nk:2 * nk]
        send, recv = refs[2 * nk:]
        x, y, c = _me()
        started = []
        for i in range(nk):
            cp = _remote(f_refs[i].at[l, c], f_refs[i].at[l, c], send.at[i], recv.at[i], (x, y, 1 - c))
            cp.start()
            started.append(cp)
        for i, cp in enumerate(started):
            _remote(f_refs[i].at[l, 1 - c], f_refs[i].at[l, 1 - c], send.at[i], recv.at[i], (x, y, 1 - c)).wait_recv()
            cp.wait_send()

    res = _pcall(body, name="rs_share", out_shape=tuple(_sds(full[k].shape, F32) for k in BIG),
                 in_specs=[ANY] * nk, out_specs=tuple([ANY] * nk), aliases={i: i for i in range(nk)},
                 scratch_shapes=[pltpu.SemaphoreType.DMA((nk,)) for _ in range(2)])(*[full[k] for k in BIG])
    return dict(zip(BIG, res))


def _gather_partials(slab):
    def body(s_ref, o_ref, send, recv):
        x, y, c = _me()
        me = 4 * x + 2 * y + c
        started = []
        peers = []
        for mask in range(1, N_DEV):
            peer = (x ^ (mask >> 2), y ^ ((mask >> 1) & 1), c ^ (mask & 1))
            peers.append(peer)
            cp = _remote(s_ref, o_ref.at[me], send.at[mask - 1], recv.at[mask - 1], peer)
            cp.start()
            started.append(cp)
        for i, peer in enumerate(peers):
            _remote(s_ref, o_ref.at[4 * peer[0] + 2 * peer[1] + peer[2]], send.at[i], recv.at[i], peer).wait_recv()
        for cp in started:
            cp.wait_send()

    return _pcall(body, name="gather_partials", out_shape=_sds((N_DEV,) + slab.shape, F32), in_specs=[ANY], out_specs=ANY,
                  scratch_shapes=[pltpu.SemaphoreType.DMA((N_DEV - 1,)), pltpu.SemaphoreType.DMA((N_DEV - 1,))])(slab)


def _sum_slabs(slabs, own, me):
    n, r, lanes = slabs.shape
    tr = r // 2

    def body(me_ref, s_ref, own_ref, o_ref):
        o_ref[...] = jnp.zeros((tr, lanes), F32)
        for i in range(n):
            @pl.when(me_ref[0] == i)
            def _():
                o_ref[...] += own_ref[...]

            @pl.when(me_ref[0] != i)
            def _():
                o_ref[...] += s_ref[i]

    return _pcall(body, name="sum_partials", out_shape=_sds((r, lanes), F32), grid=(2,),
                  in_specs=[pl.BlockSpec((n, tr, lanes), lambda i, me_ref: (0, i, 0)),
                            pl.BlockSpec((tr, lanes), lambda i, me_ref: (i, 0))],
                  out_specs=pl.BlockSpec((tr, lanes), lambda i, me_ref: (i, 0)), prefetch=1)(me, slabs, own)


def _cast_into_gathered(w, chip, by_cols, name):
    l, r, c = w.shape

    def body(chip_ref, w_ref, o_ref):
        o_ref[...] = w_ref[...].astype(BF16)

    if by_cols:
        shape, out = (l, r, N_CHIPS * c), pl.BlockSpec((None, r, c), lambda i, chip_ref: (i, 0, chip_ref[0]))
    else:
        shape, out = (l, N_CHIPS * r, c), pl.BlockSpec((None, r, c), lambda i, chip_ref: (i, chip_ref[0], 0))
    return _pcall(body, name=name, out_shape=_sds(shape, BF16), grid=(l,),
                  in_specs=[pl.BlockSpec((None, r, c), lambda i, chip_ref: (i, 0, 0))], out_specs=out, prefetch=1)(chip, w)


def _adamw(w, g, m, v, name, tr):
    r, c = w.shape

    def body(w_ref, g_ref, m_ref, v_ref, d_ref, mo_ref, vo_ref):
        gv = g_ref[...]
        mn = ADAM_B1 * m_ref[...] + (1.0 - ADAM_B1) * gv
        vn = ADAM_B2 * v_ref[...] + (1.0 - ADAM_B2) * (gv * gv)
        m_hat = mn / (1.0 - ADAM_B1 ** ADAM_STEP)
        v_hat = vn / (1.0 - ADAM_B2 ** ADAM_STEP)
        d_ref[...] = -ADAM_LR * (m_hat / (jnp.sqrt(v_hat) + ADAM_EPS) + ADAM_WD * w_ref[...])
        mo_ref[...] = mn
        vo_ref[...] = vn

    blk = pl.BlockSpec((tr, c), lambda i: (i, 0))
    return _pcall(body, name=name, out_shape=tuple(_sds((r, c), F32) for _ in range(3)), grid=(r // tr,),
                  in_specs=[blk] * 4, out_specs=(blk, blk, blk))(w, g, m, v)


AG_RIDES = {"attn_fwd": (1, ("in_t", "out", "up"), ()), "ffn_act_fwd": (1, ("down",), ("in_t", "out", "up")),
            "rms_mix_fwd": (0, (), ("down",))}
RS_RIDES = {"ffn_act_bwd": ("up",), "attn_bwd": ("in_t", "out", "down")}


class _Rides:
    def __init__(self, table, build):
        self.table, self.build = table, build

    def get(self, name):
        return self.build(self.table[name]) if name in self.table else None


class _MeshSchedule:
    def __init__(self, wts, depth, c_arr, where):
        self.wts, self.depth, self.c_arr, self.where = wts, depth, c_arr, where
        self.pairs, self.recv, self.full = None, {}, None

    def fwd_comms(self, l):
        table = {name: (l + off, ici, fwd) for name, (off, ici, fwd) in AG_RIDES.items() if 1 <= l + off < self.depth}
        return _Rides(table, lambda ride: _ag_comm(self.wts, *ride))

    def bwd_comms(self, l):
        if self.pairs is None:
            return {}
        return _Rides(RS_RIDES, lambda keys: _rs_comm(self.pairs, keys, self.recv))

    def _finish(self, l):
        self.full = _rs_share(l, _quad_sum(self.pairs, self.recv, self.where, l, self.full))
        self.pairs, self.recv = None, {}

    def after_bwd(self, l, grads):
        if self.pairs is not None:
            self._finish(l + 1)
        if self.full is None:
            geo = _piece_geo(grads)
            self.full = {k: jnp.zeros((self.depth, 2) + geo[k], F32) for k in BIG}
        self.pairs = _pair_sum(grads, _rs_swap_halves(grads), self.c_arr)
        if l == 0:
            _run_comm(_rs_comm(self.pairs, BIG, self.recv), "rs_to_owners")
            self._finish(0)


SHARDED_SMALL = ("conv_a_w", "conv_c_w", "conv_f_w")
SMALL = ("norm_mix_g", "conv_a_w", "conv_c_w", "conv_c_b", "ln_c_g", "ln_c_b", "out_norm_g", "norm_ffn_g",
         "conv_f_w", "rel_bias", "final_g")
SLAB_ROWS = 16


def _pack(arrays):
    flat = jnp.concatenate([a.reshape(-1) for a in arrays])
    unit = SLAB_ROWS * LANES
    total = -(-flat.shape[0] // unit) * unit
    return jnp.pad(flat, (0, total - flat.shape[0])).reshape(-1, LANES)


def _unpack(slab, shapes):
    flat = slab.reshape(-1)
    out, off = [], 0
    for shp in shapes:
        size = math.prod(shp)
        out.append(flat[off:off + size].reshape(shp))
        off += size
    return out


def kernel(x, norm_mix_g, w_in, conv_a_w, conv_c_w, conv_c_b, ln_c_g, ln_c_b, out_norm_g, w_out, norm_ffn_g, w_up, conv_f_w, w_down, rel_bias, final_g, loss_target, m_norm_mix_g, m_w_in, m_conv_a_w, m_conv_c_w, m_conv_c_b, m_ln_c_g, m_ln_c_b, m_out_norm_g, m_w_out, m_norm_ffn_g, m_w_up, m_conv_f_w, m_w_down, m_rel_bias, m_final_g, v_norm_mix_g, v_w_in, v_conv_a_w, v_conv_c_w, v_conv_c_b, v_ln_c_g, v_ln_c_b, v_out_norm_g, v_w_out, v_norm_ffn_g, v_w_up, v_conv_f_w, v_w_down, v_rel_bias, v_final_g):
    weights = dict(norm_mix_g=norm_mix_g, w_in=w_in, conv_a_w=conv_a_w, conv_c_w=conv_c_w, conv_c_b=conv_c_b,
                   ln_c_g=ln_c_g, ln_c_b=ln_c_b, out_norm_g=out_norm_g, w_out=w_out, norm_ffn_g=norm_ffn_g, w_up=w_up,
                   conv_f_w=conv_f_w, w_down=w_down, rel_bias=rel_bias, final_g=final_g)
    mom_m = dict(norm_mix_g=m_norm_mix_g, w_in=m_w_in, conv_a_w=m_conv_a_w, conv_c_w=m_conv_c_w, conv_c_b=m_conv_c_b,
                 ln_c_g=m_ln_c_g, ln_c_b=m_ln_c_b, out_norm_g=m_out_norm_g, w_out=m_w_out, norm_ffn_g=m_norm_ffn_g,
                 w_up=m_w_up, conv_f_w=m_conv_f_w, w_down=m_w_down, rel_bias=m_rel_bias, final_g=m_final_g)
    mom_v = dict(norm_mix_g=v_norm_mix_g, w_in=v_w_in, conv_a_w=v_conv_a_w, conv_c_w=v_conv_c_w, conv_c_b=v_conv_c_b,
                 ln_c_g=v_ln_c_g, ln_c_b=v_ln_c_b, out_norm_g=v_out_norm_g, w_out=v_w_out, norm_ffn_g=v_norm_ffn_g,
                 w_up=v_w_up, conv_f_w=v_conv_f_w, w_down=v_w_down, rel_bias=v_rel_bias, final_g=v_final_g)
    xi, yi, ci = _me()
    chip = _chip_of(xi, yi)
    c_arr = jnp.reshape(ci, (1,)).astype(I32)
    chip_arr = jnp.reshape(chip, (1,)).astype(I32)
    me_arr = jnp.reshape(4 * xi + 2 * yi + ci, (1,)).astype(I32)
    where = jnp.stack([chip, ci]).astype(I32)
    depth = w_out.shape[0]

    wts = {"in_t": _cast_into_gathered(jnp.swapaxes(w_in, 1, 2), chip_arr, False, "cast_in"),
           "out": _cast_into_gathered(w_out, chip_arr, False, "cast_out"),
           "up": _cast_into_gathered(w_up, chip_arr, True, "cast_up"),
           "down": _cast_into_gathered(w_down, chip_arr, False, "cast_down")}
    store = {}
    _run_comm(_small_gather_comm(_pack([weights[n] for n in SHARDED_SMALL]), store), "ag_small")
    _run_comm(_ag_comm(wts, 0, BIG, BIG), "ag_weights")
    prm = {n: weights[n] for n in SMALL if n not in SHARDED_SMALL}
    per_chip = [_unpack(store["small"][j], [weights[n].shape for n in SHARDED_SMALL]) for j in range(N_CHIPS)]
    for i, n in enumerate(SHARDED_SMALL):
        prm[n] = jnp.concatenate([per_chip[j][i] for j in range(N_CHIPS)], axis=-1)

    sched = _MeshSchedule(wts, depth, c_arr, where)
    loss_row, dx, small, d_rel, d_final = _local_step(x[0], loss_target[0], wts, prm, sched)
    reduced = sched.full
    loss = lax.psum(loss_row[0, 0], ("x", "y", "c"))

    stacked = {n: jnp.stack([small[l][n] for l in range(depth)]) for n in small[0]}
    stacked["rel_bias"] = d_rel
    stacked["final_g"] = d_final
    full_shapes = [stacked[n].shape for n in SMALL]
    partial = _pack([stacked[n] for n in SMALL])
    summed = _unpack(_sum_slabs(_gather_partials(partial), partial, me_arr), full_shapes)
    grads = {}
    for n, g in zip(SMALL, summed):
        if n in SHARDED_SMALL:
            width = weights[n].shape[-1]
            g = lax.dynamic_slice_in_dim(g, chip * width, width, axis=g.ndim - 1)
        grads[n] = g

    shard_shapes = {"in_t": jnp.swapaxes(w_in, 1, 2).shape, "out": w_out.shape, "up": w_up.shape, "down": w_down.shape}
    red = {k: reduced[k].reshape(shard_shapes[k]) for k in BIG}
    grads["w_in"] = jnp.swapaxes(red["in_t"], 1, 2)
    grads["w_out"], grads["w_up"], grads["w_down"] = red["out"], red["up"], red["down"]

    delta, new_m, new_v = {}, {}, {}
    for n in ("w_in", "w_out", "w_up", "w_down"):
        shp = weights[n].shape
        flat = lambda a, shp=shp: a.reshape(shp[0] * shp[1], shp[2])
        tile = max(t for t in range(8, 257, 8) if shp[1] % t == 0)
        d, mn, vn = _adamw(flat(weights[n]), flat(grads[n]), flat(mom_m[n]), flat(mom_v[n]), "adamw_" + n, tile)
        delta[n], new_m[n], new_v[n] = d.reshape(shp), mn.reshape(shp), vn.reshape(shp)
    shapes = [weights[n].shape for n in SMALL]
    packed = [_pack([src[n] for n in SMALL]) for src in (weights, grads, mom_m, mom_v)]
    d, mn, vn = _adamw(*packed, "adamw_small", packed[0].shape[0] // 2)
    for n, a, b, c in zip(SMALL, _unpack(d, shapes), _unpack(mn, shapes), _unpack(vn, shapes)):
        delta[n], new_m[n], new_v[n] = a, b, c

    order = ("norm_mix_g", "w_in", "conv_a_w", "conv_c_w", "conv_c_b", "ln_c_g", "ln_c_b", "out_norm_g", "w_out",
             "norm_ffn_g", "w_up", "conv_f_w", "w_down", "rel_bias", "final_g")
    return (loss, dx[None], *[grads[n] for n in order], *[delta[n] for n in order], *[new_m[n] for n in order],
            *[new_v[n] for n in order])
```

```python
import functools
import math

import numpy as np
import jax
import jax.numpy as jnp
from jax import lax
from jax.experimental import pallas as pl
from jax.experimental.pallas import tpu as pltpu

F32 = jnp.float32
BF16 = jnp.bfloat16
I32 = jnp.int32

EPS = 1e-6
NEG = -1e30
D_HEAD = 64
LANES = 128
BLK = 128
ATTN_GROUP_FWD = 4
ATTN_GROUP_BWD = 4
DILATED_BRANCHES = ((128, 1), (512, 4), (2048, 16))
NUM_BUCKETS = 32
MAX_DISTANCE = 2048
SHORT_CONV = 3
CONFORMER_CONV = 31
FFN_CONV = 3
PAD_SHORT = 8
PAD_LONG = 32
ROW_CHUNK = 256
V7X_VMEM_BYTES = 64 * 1024 * 1024
VMEM_REQUEST = V7X_VMEM_BYTES * 7 // 8

ADAM_LR = 0.001
ADAM_B1 = 0.9
ADAM_B2 = 0.999
ADAM_EPS = 1e-08
ADAM_WD = 0.01
ADAM_STEP = 10

MESH = pl.DeviceIdType.MESH
ANY = pl.BlockSpec(memory_space=pl.ANY)


def _sds(shape, dtype):
    return jax.ShapeDtypeStruct(tuple(shape), dtype)


class _Comm:
    def __init__(self, ins, out_shapes, aliases, sems, start, finish, done):
        self.ins, self.out_shapes, self.aliases, self.sems = list(ins), list(out_shapes), dict(aliases), list(sems)
        self.start, self.finish, self.done = start, finish, done


def _pcall(body, *, name, out_shape, grid=(), in_specs=None, out_specs=None, scratch_shapes=(), vmem=VMEM_REQUEST,
           aliases=None, prefetch=0, comm=None):
    params = pltpu.CompilerParams(dimension_semantics=("arbitrary",) * len(grid), vmem_limit_bytes=vmem)
    single = not isinstance(out_shape, (tuple, list))
    outs = [out_shape] if single else list(out_shape)
    ospecs = [out_specs] if single else list(out_specs)
    ispecs, scratch, aliases = list(in_specs), list(scratch_shapes), dict(aliases or {})
    n_in, n_out, n_scr = len(ispecs), len(outs), len(scratch)
    kernel_body = body
    if comm is not None:
        n_ci, n_co = len(comm.ins), len(comm.out_shapes)

        def kernel_body(*refs):
            pre, rest = refs[:prefetch], refs[prefetch:]
            core_in, c_in = rest[:n_in], rest[n_in:n_in + n_ci]
            o0 = n_in + n_ci
            core_out, c_out = rest[o0:o0 + n_out], rest[o0 + n_out:o0 + n_out + n_co]
            s0 = o0 + n_out + n_co
            core_scr, c_sem = rest[s0:s0 + n_scr], rest[s0 + n_scr:]
            first = functools.reduce(jnp.logical_and, [pl.program_id(a) == 0 for a in range(len(grid))])
            last = functools.reduce(jnp.logical_and, [pl.program_id(a) == grid[a] - 1 for a in range(len(grid))])
            pl.when(first)(lambda: comm.start(c_in, c_out, c_sem))
            body(*pre, *core_in, *core_out, *core_scr)
            pl.when(last)(lambda: comm.finish(c_in, c_out, c_sem))

        for i, o in comm.aliases.items():
            aliases[prefetch + n_in + i] = n_out + o
        ispecs += [ANY] * n_ci
        ospecs += [ANY] * n_co
        outs += comm.out_shapes
        scratch += comm.sems
    if prefetch:
        spec = pltpu.PrefetchScalarGridSpec(num_scalar_prefetch=prefetch, grid=grid, in_specs=ispecs,
                                            out_specs=tuple(ospecs), scratch_shapes=scratch)
        call = pl.pallas_call(kernel_body, name=name, out_shape=tuple(outs), grid_spec=spec,
                              input_output_aliases=aliases, compiler_params=params)
    else:
        call = pl.pallas_call(kernel_body, name=name, out_shape=tuple(outs), grid=grid, in_specs=ispecs,
                              out_specs=tuple(ospecs), scratch_shapes=scratch, input_output_aliases=aliases,
                              compiler_params=params)

    def run(*args):
        res = call(*args, *(comm.ins if comm is not None else ()))
        if comm is not None:
            comm.done(res[n_out:])
        return res[0] if single else tuple(res[:n_out])

    return run


def _run_comm(comm, name):
    def body(*refs):
        n_ci, n_co = len(comm.ins), len(comm.out_shapes)
        c_in, c_out, c_sem = refs[:n_ci], refs[n_ci:n_ci + n_co], refs[n_ci + n_co:]
        comm.start(c_in, c_out, c_sem)
        comm.finish(c_in, c_out, c_sem)

    res = pl.pallas_call(body, name=name, out_shape=tuple(comm.out_shapes), in_specs=[ANY] * len(comm.ins),
                         out_specs=tuple([ANY] * len(comm.out_shapes)), scratch_shapes=comm.sems,
                         input_output_aliases=comm.aliases)(*comm.ins)
    comm.done(res)


def _dot(a, b):
    return lax.dot_general(a, b, (((1,), (0,)), ((), ())), preferred_element_type=F32)


def _dot_nt(a, b):
    return lax.dot_general(a, b, (((1,), (1,)), ((), ())), preferred_element_type=F32)


def _dot_tn(a, b):
    return lax.dot_general(a, b, (((0,), (0,)), ((), ())), preferred_element_type=F32)


def _sigmoid(x):
    return 1.0 / (1.0 + jnp.exp(-x))


def _rstd(x):
    return lax.rsqrt(jnp.mean(x * x, axis=-1, keepdims=True) + EPS)


def _rms_fwd(x, g, name, comm=None):
    s, d = x.shape
    tm = ROW_CHUNK

    def body(x_ref, g_ref, o_ref):
        xv = x_ref[...]
        o_ref[...] = (xv * _rstd(xv) * g_ref[...]).astype(BF16)

    return _pcall(body, name=name, out_shape=_sds((s, d), BF16), grid=(s // tm,),
                  in_specs=[pl.BlockSpec((tm, d), lambda i: (i, 0)), pl.BlockSpec((1, d), lambda i: (0, 0))],
                  out_specs=pl.BlockSpec((tm, d), lambda i: (i, 0)), comm=comm)(x, g)


def _rms_bwd(x, g, dh, dres, name):
    s, d = x.shape
    tm = ROW_CHUNK

    def body(x_ref, g_ref, dh_ref, dres_ref, dx_ref, dxb_ref, dg_ref):
        i = pl.program_id(0)
        xv = x_ref[...]
        r = _rstd(xv)
        xh = xv * r
        dhv = dh_ref[...]
        gd = dhv * g_ref[...]
        dx = dres_ref[...] + r * (gd - xh * jnp.mean(gd * xh, axis=-1, keepdims=True))
        dx_ref[...] = dx
        dxb_ref[...] = dx.astype(BF16)
        part = jnp.sum(dhv * xh, axis=0, keepdims=True)

        @pl.when(i == 0)
        def _():
            dg_ref[...] = part

        @pl.when(i > 0)
        def _():
            dg_ref[...] += part

    row = pl.BlockSpec((tm, d), lambda i: (i, 0))
    vec = pl.BlockSpec((1, d), lambda i: (0, 0))
    return _pcall(body, name=name, out_shape=(_sds((s, d), F32), _sds((s, d), BF16), _sds((1, d), F32)),
                  grid=(s // tm,), in_specs=[row, vec, row, row], out_specs=(row, row, vec))(x, g, dh, dres)


def _final_loss(x, g, tgt, name):
    s, d = x.shape
    tm = ROW_CHUNK

    def body(x_ref, g_ref, t_ref, loss_ref, dx_ref, dxb_ref, dg_ref):
        i = pl.program_id(0)
        xv = x_ref[...]
        r = _rstd(xv)
        xh = xv * r
        e = xh * g_ref[...] - t_ref[...]
        lpart = 0.5 * jnp.sum(jnp.mean(e * e, axis=-1, keepdims=True), axis=0, keepdims=True)
        dy = e * (1.0 / d)
        gd = dy * g_ref[...]
        dx = r * (gd - xh * jnp.mean(gd * xh, axis=-1, keepdims=True))
        dx_ref[...] = dx
        dxb_ref[...] = dx.astype(BF16)
        part = jnp.sum(dy * xh, axis=0, keepdims=True)
        lrow = jnp.broadcast_to(lpart, (1, LANES))

        @pl.when(i == 0)
        def _():
            dg_ref[...] = part
            loss_ref[...] = lrow

        @pl.when(i > 0)
        def _():
            dg_ref[...] += part
            loss_ref[...] += lrow

    row = pl.BlockSpec((tm, d), lambda i: (i, 0))
    vec = pl.BlockSpec((1, d), lambda i: (0, 0))
    return _pcall(body, name=name,
                  out_shape=(_sds((1, LANES), F32), _sds((s, d), F32), _sds((s, d), BF16), _sds((1, d), F32)),
                  grid=(s // tm,), in_specs=[row, vec, row],
                  out_specs=(pl.BlockSpec((1, LANES), lambda i: (0, 0)), row, row, vec))(x, g, tgt)


def _mm_n(a, b, layer, *, nt, tn, out_dtype, name, comm=None):
    s, k = a.shape
    n = b.shape[1] if nt else b.shape[2]
    rows = 512

    def body(a_ref, b_ref, o_ref):
        bv = b_ref[...]
        for r0 in range(0, s, rows):
            av = a_ref[r0:r0 + rows, :]
            o_ref[r0:r0 + rows, :] = (_dot_nt(av, bv) if nt else _dot(av, bv)).astype(out_dtype)

    b_spec = (pl.BlockSpec((None, tn, k), lambda j: (layer, j, 0)) if nt
              else pl.BlockSpec((None, k, tn), lambda j: (layer, 0, j)))
    return _pcall(body, name=name, out_shape=_sds((s, n), out_dtype), grid=(n // tn,),
                  in_specs=[pl.BlockSpec((s, k), lambda j: (0, 0)), b_spec],
                  out_specs=pl.BlockSpec((s, tn), lambda j: (0, j)), comm=comm)(a, b)


def _mm_k(a, b, layer, resid, *, nt, tk, b_off, name, comm=None):
    s, ka = a.shape
    n = b.shape[1] if nt else b.shape[2]
    rows = 512

    def body(a_ref, b_ref, *refs):
        o_ref = refs[-1]
        kk = pl.program_id(0)
        bv = b_ref[...]

        @pl.when(kk == 0)
        def _():
            o_ref[...] = jnp.zeros((s, n), F32) if resid is None else refs[0][...]

        for r0 in range(0, s, rows):
            av = a_ref[r0:r0 + rows, :]
            o_ref[r0:r0 + rows, :] += _dot_nt(av, bv) if nt else _dot(av, bv)

    b_spec = (pl.BlockSpec((None, n, tk), lambda kk: (layer, 0, kk + b_off)) if nt
              else pl.BlockSpec((None, tk, n), lambda kk: (layer, kk + b_off, 0)))
    full = pl.BlockSpec((s, n), lambda kk: (0, 0))
    extra = () if resid is None else (resid,)
    return _pcall(body, name=name, out_shape=_sds((s, n), F32), grid=(ka // tk,),
                  in_specs=[pl.BlockSpec((s, tk), lambda kk: (0, kk)), b_spec] + [full] * len(extra),
                  out_specs=full, comm=comm)(a, b, *extra)


def _mm_tn(a, b, *, t, name):
    s, ka = a.shape
    n = b.shape[1]

    def body(a_ref, b_ref, o_ref):
        o_ref[...] = _dot_tn(a_ref[...], b_ref[...]).astype(BF16)

    return _pcall(body, name=name, out_shape=_sds((ka, n), BF16), grid=(ka // t,),
                  in_specs=[pl.BlockSpec((s, t), lambda i: (0, i)), pl.BlockSpec((s, n), lambda i: (0, 0))],
                  out_specs=pl.BlockSpec((t, n), lambda i: (i, 0)))(a, b)


def _mm_tn2(a, b_lo, b_hi, *, t, name):
    s, ka = a.shape
    half = b_lo.shape[1]
    nb = half // t

    def body(a_ref, lo_ref, hi_ref, o_ref):
        j = pl.program_id(0)

        @pl.when(j < nb)
        def _():
            o_ref[...] = _dot_tn(a_ref[...], lo_ref[...]).astype(BF16)

        @pl.when(j >= nb)
        def _():
            o_ref[...] = _dot_tn(a_ref[...], hi_ref[...]).astype(BF16)

    return _pcall(body, name=name, out_shape=_sds((ka, 2 * half), BF16), grid=(2 * nb,),
                  in_specs=[pl.BlockSpec((s, ka), lambda j: (0, 0)),
                            pl.BlockSpec((s, t), lambda j: (0, jnp.minimum(j, nb - 1))),
                            pl.BlockSpec((s, t), lambda j: (0, jnp.maximum(j - nb, 0)))],
                  out_specs=pl.BlockSpec((ka, t), lambda j: (0, j)))(a, b_lo, b_hi)


def _conv_taps(win, w_ref, width, pad, rows):
    acc = None
    for k in range(width):
        off = pad - (width - 1) + k
        term = w_ref[pl.ds(k, 1), :] * win[off:off + rows, :]
        acc = term if acc is None else acc + term
    return acc


def _conv_taps_t(win, w_ref, width, rows):
    acc = None
    for k in range(width):
        off = (width - 1) - k
        term = w_ref[pl.ds(k, 1), :] * win[off:off + rows, :]
        acc = term if acc is None else acc + term
    return acc


def _conv_wgrad(dw_ref, g, win, width, pad, rows):
    for k in range(width):
        off = pad - (width - 1) + k
        dw_ref[pl.ds(k, 1), :] += jnp.sum(g * win[off:off + rows, :], axis=0, keepdims=True)


def _mixer_a_fwd(ah, ab, ac, win_t, wa_ref, rows):
    ct = _conv_taps(win_t, wa_ref, SHORT_CONV, PAD_SHORT, rows)
    return ab * ct, ct


def _mixer_c_fwd(win_u, wc_ref, cb_ref, lg_ref, lb_ref, rows):
    u = _conv_taps(win_u, wc_ref, CONFORMER_CONV, PAD_LONG, rows) + cb_ref[...]
    mu = jnp.mean(u, axis=-1, keepdims=True)
    uc = u - mu
    rs = lax.rsqrt(jnp.mean(uc * uc, axis=-1, keepdims=True) + EPS)
    uh = uc * rs
    ln = uh * lg_ref[...] + lb_ref[...]
    sg = _sigmoid(ln)
    return ln * sg, ln, sg, uh, rs


def _mix_fwd(z, wa, wc, cb, lg, lb, ga, gc, name):
    s = z.shape[0]
    w = wa.shape[1]
    nblk = z.shape[1] // w
    rc = ROW_CHUNK

    def body(ah_ref, ab_ref, ac_ref, cv_ref, cg_ref, wa_ref, wc_ref, cb_ref, lg_ref, lb_ref, ga_ref, gc_ref,
             ya_ref, yc_ref, tpad, upad):
        tpad[pl.ds(0, PAD_SHORT), :] = jnp.zeros((PAD_SHORT, w), F32)
        upad[pl.ds(0, PAD_LONG), :] = jnp.zeros((PAD_LONG, w), F32)

        def chunk(i, carry):
            base = pl.multiple_of(i * rc, rc)
            rows = pl.ds(base, rc)
            ah, ab, ac = ah_ref[rows, :], ab_ref[rows, :], ac_ref[rows, :]
            tpad[pl.ds(base + PAD_SHORT, rc), :] = ac * ah
            ya, _ = _mixer_a_fwd(ah, ab, ac, tpad[pl.ds(base, rc + PAD_SHORT), :], wa_ref, rc)
            ya_ref[rows, :] = (ya * _rstd(ya) * ga_ref[...]).astype(BF16)
            upad[pl.ds(base + PAD_LONG, rc), :] = cv_ref[rows, :] * _sigmoid(cg_ref[rows, :])
            yc = _mixer_c_fwd(upad[pl.ds(base, rc + PAD_LONG), :], wc_ref, cb_ref, lg_ref, lb_ref, rc)[0]
            yc_ref[rows, :] = (yc * _rstd(yc) * gc_ref[...]).astype(BF16)
            return carry

        lax.fori_loop(0, s // rc, chunk, 0)

    def zblk(j):
        return pl.BlockSpec((s, w), lambda i: (0, j))

    def whole(a):
        return pl.BlockSpec(a.shape, lambda i: (0, 0))

    return _pcall(
        body, name=name, out_shape=(_sds((s, w), BF16), _sds((s, w), BF16)), grid=(1,),
        in_specs=[zblk(0), zblk(1), zblk(2), zblk(nblk - 2), zblk(nblk - 1)] + [whole(a) for a in (wa, wc, cb, lg, lb, ga, gc)],
        out_specs=(pl.BlockSpec((s, w), lambda i: (0, 0)), pl.BlockSpec((s, w), lambda i: (0, 0))),
        scratch_shapes=[pltpu.VMEM((s + PAD_SHORT, w), F32), pltpu.VMEM((s + PAD_LONG, w), F32)],
    )(z, z, z, z, z, wa, wc, cb, lg, lb, ga, gc)


def _mix_bwd(z, dy, wa, wc, cb, lg, lb, ga, gc, name):
    s = z.shape[0]
    w = wa.shape[1]
    nblk = z.shape[1] // w
    nyb = dy.shape[1] // w
    rc = ROW_CHUNK

    def body(ah_ref, ab_ref, ac_ref, cv_ref, cg_ref, dya_ref, dyc_ref,
             wa_ref, wc_ref, cb_ref, lg_ref, lb_ref, ga_ref, gc_ref,
             dza_ref, dzc_ref, dwa_ref, dwc_ref, dcb_ref, dlg_ref, dlb_ref, dga_ref, dgc_ref,
             tpad, upad, dctp, dup):
        tpad[pl.ds(0, PAD_SHORT), :] = jnp.zeros((PAD_SHORT, w), F32)
        upad[pl.ds(0, PAD_LONG), :] = jnp.zeros((PAD_LONG, w), F32)
        dctp[pl.ds(s, PAD_SHORT), :] = jnp.zeros((PAD_SHORT, w), F32)
        dup[pl.ds(s, PAD_LONG), :] = jnp.zeros((PAD_LONG, w), F32)
        for ref in (dwa_ref, dwc_ref, dcb_ref, dlg_ref, dlb_ref, dga_ref, dgc_ref):
            ref[...] = jnp.zeros(ref.shape, F32)

        def rms_bwd(y, g_ref, dyn, dg_ref):
            r = _rstd(y)
            yh = y * r
            gd = dyn * g_ref[...]
            dg_ref[...] += jnp.sum(dyn * yh, axis=0, keepdims=True)
            return r * (gd - yh * jnp.mean(gd * yh, axis=-1, keepdims=True))

        def first(i, carry):
            base = pl.multiple_of(i * rc, rc)
            rows = pl.ds(base, rc)
            ah, ab, ac = ah_ref[rows, :], ab_ref[rows, :], ac_ref[rows, :]
            tpad[pl.ds(base + PAD_SHORT, rc), :] = ac * ah
            win_t = tpad[pl.ds(base, rc + PAD_SHORT), :]
            ya, ct = _mixer_a_fwd(ah, ab, ac, win_t, wa_ref, rc)
            dya = rms_bwd(ya, ga_ref, dya_ref[rows, :], dga_ref)
            dza_ref[rows, w:2 * w] = (dya * ct).astype(BF16)
            dct = dya * ab
            dctp[rows, :] = dct
            _conv_wgrad(dwa_ref, dct, win_t, SHORT_CONV, PAD_SHORT, rc)

            upad[pl.ds(base + PAD_LONG, rc), :] = cv_ref[rows, :] * _sigmoid(cg_ref[rows, :])
            win_u = upad[pl.ds(base, rc + PAD_LONG), :]
            yc, ln, sg, uh, rs = _mixer_c_fwd(win_u, wc_ref, cb_ref, lg_ref, lb_ref, rc)
            dyc = rms_bwd(yc, gc_ref, dyc_ref[rows, :], dgc_ref)
            dln = dyc * (sg * (1.0 + ln * (1.0 - sg)))
            dlg_ref[...] += jnp.sum(dln * uh, axis=0, keepdims=True)
            dlb_ref[...] += jnp.sum(dln, axis=0, keepdims=True)
            duh = dln * lg_ref[...]
            du = rs * (duh - jnp.mean(duh, axis=-1, keepdims=True) - uh * jnp.mean(duh * uh, axis=-1, keepdims=True))
            dcb_ref[...] += jnp.sum(du, axis=0, keepdims=True)
            dup[rows, :] = du
            _conv_wgrad(dwc_ref, du, win_u, CONFORMER_CONV, PAD_LONG, rc)
            return carry

        lax.fori_loop(0, s // rc, first, 0)

        def second(i, carry):
            base = pl.multiple_of(i * rc, rc)
            rows = pl.ds(base, rc)
            dt = _conv_taps_t(dctp[pl.ds(base, rc + PAD_SHORT), :], wa_ref, SHORT_CONV, rc)
            dza_ref[rows, 0:w] = (dt * ac_ref[rows, :]).astype(BF16)
            dza_ref[rows, 2 * w:3 * w] = (dt * ah_ref[rows, :]).astype(BF16)
            du0 = _conv_taps_t(dup[pl.ds(base, rc + PAD_LONG), :], wc_ref, CONFORMER_CONV, rc)
            sg = _sigmoid(cg_ref[rows, :])
            dzc_ref[rows, 0:w] = (du0 * sg).astype(BF16)
            dzc_ref[rows, w:2 * w] = (du0 * cv_ref[rows, :] * sg * (1.0 - sg)).astype(BF16)
            return carry

        lax.fori_loop(0, s // rc, second, 0)

    def blk(j):
        return pl.BlockSpec((s, w), lambda i: (0, j))

    def whole(a):
        return pl.BlockSpec(tuple(a.shape), lambda i: (0, 0))

    params = (wa, wc, cb, lg, lb, ga, gc)
    outs = (_sds((s, 3 * w), BF16), _sds((s, 2 * w), BF16)) + tuple(_sds(p.shape, F32) for p in params)
    return _pcall(
        body, name=name, out_shape=outs, grid=(1,),
        in_specs=[blk(0), blk(1), blk(2), blk(nblk - 2), blk(nblk - 1), blk(0), blk(nyb - 1)] + [whole(p) for p in params],
        out_specs=tuple(whole(o) for o in outs),
        scratch_shapes=[pltpu.VMEM((s + PAD_SHORT, w), F32), pltpu.VMEM((s + PAD_LONG, w), F32),
                        pltpu.VMEM((s + PAD_SHORT, w), F32), pltpu.VMEM((s + PAD_LONG, w), F32)],
    )(z, z, z, z, z, dy, dy, *params)


def _ffn_act_fwd(up, wf, name, comm=None):
    s, f2 = up.shape
    f = f2 // 2
    tc = 256
    nb = f // tc
    rc = ROW_CHUNK

    def body(g_ref, v_ref, wg_ref, wv_ref, o_ref, gpad, vpad):
        gpad[pl.ds(0, PAD_SHORT), :] = jnp.zeros((PAD_SHORT, tc), F32)
        vpad[pl.ds(0, PAD_SHORT), :] = jnp.zeros((PAD_SHORT, tc), F32)

        def chunk(i, carry):
            base = pl.multiple_of(i * rc, rc)
            rows = pl.ds(base, rc)
            gpad[pl.ds(base + PAD_SHORT, rc), :] = g_ref[rows, :].astype(F32)
            vpad[pl.ds(base + PAD_SHORT, rc), :] = v_ref[rows, :].astype(F32)
            gc = _conv_taps(gpad[pl.ds(base, rc + PAD_SHORT), :], wg_ref, FFN_CONV, PAD_SHORT, rc)
            vc = _conv_taps(vpad[pl.ds(base, rc + PAD_SHORT), :], wv_ref, FFN_CONV, PAD_SHORT, rc)
            o_ref[rows, :] = (gc * _sigmoid(gc) * vc).astype(BF16)
            return carry

        lax.fori_loop(0, s // rc, chunk, 0)

    return _pcall(
        body, name=name, out_shape=_sds((s, f), BF16), grid=(nb,),
        in_specs=[pl.BlockSpec((s, tc), lambda j: (0, j)), pl.BlockSpec((s, tc), lambda j: (0, j + nb)),
                  pl.BlockSpec((FFN_CONV, tc), lambda j: (0, j)), pl.BlockSpec((FFN_CONV, tc), lambda j: (0, j + nb))],
        out_specs=pl.BlockSpec((s, tc), lambda j: (0, j)),
        scratch_shapes=[pltpu.VMEM((s + PAD_SHORT, tc), F32), pltpu.VMEM((s + PAD_SHORT, tc), F32)], comm=comm,
    )(up, up, wf, wf)


def _ffn_act_bwd(up, dact, wf, name, comm=None):
    s, f2 = up.shape
    f = f2 // 2
    tc = 256
    nb = f // tc
    rc = ROW_CHUNK

    def body(g_ref, v_ref, da_ref, wg_ref, wv_ref, act_ref, dg_ref, dv_ref, dwg_ref, dwv_ref, gpad, vpad, dgp, dvp):
        gpad[pl.ds(0, PAD_SHORT), :] = jnp.zeros((PAD_SHORT, tc), F32)
        vpad[pl.ds(0, PAD_SHORT), :] = jnp.zeros((PAD_SHORT, tc), F32)
        dgp[pl.ds(s, PAD_SHORT), :] = jnp.zeros((PAD_SHORT, tc), F32)
        dvp[pl.ds(s, PAD_SHORT), :] = jnp.zeros((PAD_SHORT, tc), F32)
        dwg_ref[...] = jnp.zeros((FFN_CONV, tc), F32)
        dwv_ref[...] = jnp.zeros((FFN_CONV, tc), F32)

        def first(i, carry):
            base = pl.multiple_of(i * rc, rc)
            rows = pl.ds(base, rc)
            gpad[pl.ds(base + PAD_SHORT, rc), :] = g_ref[rows, :].astype(F32)
            vpad[pl.ds(base + PAD_SHORT, rc), :] = v_ref[rows, :].astype(F32)
            win_g = gpad[pl.ds(base, rc + PAD_SHORT), :]
            win_v = vpad[pl.ds(base, rc + PAD_SHORT), :]
            gc = _conv_taps(win_g, wg_ref, FFN_CONV, PAD_SHORT, rc)
            vc = _conv_taps(win_v, wv_ref, FFN_CONV, PAD_SHORT, rc)
            sg = _sigmoid(gc)
            silu = gc * sg
            act_ref[rows, :] = (silu * vc).astype(BF16)
            da = da_ref[rows, :].astype(F32)
            dgc = da * vc * (sg * (1.0 + gc * (1.0 - sg)))
            dvc = da * silu
            dgp[rows, :] = dgc
            dvp[rows, :] = dvc
            _conv_wgrad(dwg_ref, dgc, win_g, FFN_CONV, PAD_SHORT, rc)
            _conv_wgrad(dwv_ref, dvc, win_v, FFN_CONV, PAD_SHORT, rc)
            return carry

        lax.fori_loop(0, s // rc, first, 0)

        def second(i, carry):
            base = pl.multiple_of(i * rc, rc)
            rows = pl.ds(base, rc)
            dg_ref[rows, :] = _conv_taps_t(dgp[pl.ds(base, rc + PAD_SHORT), :], wg_ref, FFN_CONV, rc).astype(BF16)
            dv_ref[rows, :] = _conv_taps_t(dvp[pl.ds(base, rc + PAD_SHORT), :], wv_ref, FFN_CONV, rc).astype(BF16)
            return carry

        lax.fori_loop(0, s // rc, second, 0)

    lo = pl.BlockSpec((s, tc), lambda j: (0, j))
    hi = pl.BlockSpec((s, tc), lambda j: (0, j + nb))
    wlo = pl.BlockSpec((FFN_CONV, tc), lambda j: (0, j))
    whi = pl.BlockSpec((FFN_CONV, tc), lambda j: (0, j + nb))
    act, dgate, dval, dwg, dwv = _pcall(
        body, name=name,
        out_shape=(_sds((s, f), BF16), _sds((s, f), BF16), _sds((s, f), BF16), _sds((FFN_CONV, f), F32), _sds((FFN_CONV, f), F32)),
        grid=(nb,), in_specs=[lo, hi, lo, wlo, whi], out_specs=(lo, lo, lo, wlo, wlo),
        scratch_shapes=[pltpu.VMEM((s + PAD_SHORT, tc), F32) for _ in range(4)], comm=comm,
    )(up, up, dact, wf, wf)
    return act, dgate, dval, jnp.concatenate([dwg, dwv], axis=1)


def _y_assemble(yan, yb, ycn, gb, name):
    s, w = yan.shape
    wb = yb.shape[1]
    tm = ROW_CHUNK

    def body(ya_ref, yb_ref, yc_ref, g_ref, o_ref):
        ybv = yb_ref[...]
        o_ref[:, 0:w] = ya_ref[...]
        o_ref[:, w:w + wb] = (ybv * _rstd(ybv) * g_ref[...]).astype(BF16)
        o_ref[:, w + wb:] = yc_ref[...]

    return _pcall(body, name=name, out_shape=_sds((s, 2 * w + wb), BF16), grid=(s // tm,),
                  in_specs=[pl.BlockSpec((tm, w), lambda i: (i, 0)), pl.BlockSpec((tm, wb), lambda i: (i, 0)),
                            pl.BlockSpec((tm, w), lambda i: (i, 0)), pl.BlockSpec((1, wb), lambda i: (0, 0))],
                  out_specs=pl.BlockSpec((tm, 2 * w + wb), lambda i: (i, 0)))(yan, yb, ycn, gb)


def _yb_norm_bwd(yb, dy, gb, name):
    s, wb = yb.shape
    w = wb // 2
    heads = wb // D_HEAD
    tm = ROW_CHUNK

    def body(yb_ref, d1_ref, d2_ref, g_ref, dyb_ref, dl_ref, dg_ref):
        i = pl.program_id(0)
        y = yb_ref[...]
        dyn = jnp.concatenate([d1_ref[...], d2_ref[...]], axis=1)
        r = _rstd(y)
        yh = y * r
        gd = dyn * g_ref[...]
        dyb = r * (gd - yh * jnp.mean(gd * yh, axis=-1, keepdims=True))
        dyb_ref[...] = dyb
        part = jnp.sum(dyn * yh, axis=0, keepdims=True)
        prod = dyb * y
        even = lax.broadcasted_iota(I32, (tm, LANES), 1) < D_HEAD
        for p in range(heads // 2):
            blk = prod[:, p * LANES:(p + 1) * LANES]
            ev = jnp.sum(jnp.where(even, blk, 0.0), axis=1, keepdims=True)
            od = jnp.sum(jnp.where(even, 0.0, blk), axis=1, keepdims=True)
            dl_ref[2 * p] = jnp.broadcast_to(ev, (tm, LANES))
            dl_ref[2 * p + 1] = jnp.broadcast_to(od, (tm, LANES))

        @pl.when(i == 0)
        def _():
            dg_ref[...] = part

        @pl.when(i > 0)
        def _():
            dg_ref[...] += part

    return _pcall(
        body, name=name, out_shape=(_sds((s, wb), F32), _sds((heads, s, LANES), F32), _sds((1, wb), F32)),
        grid=(s // tm,),
        in_specs=[pl.BlockSpec((tm, wb), lambda i: (i, 0)), pl.BlockSpec((tm, w), lambda i: (i, 1)),
                  pl.BlockSpec((tm, w), lambda i: (i, 2)), pl.BlockSpec((1, wb), lambda i: (0, 0))],
        out_specs=(pl.BlockSpec((tm, wb), lambda i: (i, 0)), pl.BlockSpec((heads, tm, LANES), lambda i: (0, i, 0)),
                   pl.BlockSpec((1, wb), lambda i: (0, 0))),
    )(yb, dy, dy, gb)


def _t5_bucket_table():
    max_exact = NUM_BUCKETS // 2
    out = np.full((len(DILATED_BRANCHES), BLK, 2 * BLK), -1, np.int32)
    rel = np.arange(BLK)[:, None] - np.arange(2 * BLK)[None, :] + BLK
    for b, (window, dilation) in enumerate(DILATED_BRANCHES):
        n_keys = window // dilation
        dist = np.maximum(rel, 0) * dilation
        d_f = np.maximum(dist, 1).astype(np.float32)
        large = max_exact + (np.log(d_f / np.float32(max_exact)) / np.float32(math.log(MAX_DISTANCE / max_exact))
                             * np.float32(NUM_BUCKETS - max_exact)).astype(np.int32)
        large = np.minimum(large, NUM_BUCKETS - 1)
        bucket = np.where(dist < max_exact, dist, large)
        out[b] = np.where((rel >= 0) & (rel <= n_keys), bucket, -1)
    return out


def _bias_tiles(rel_bias, buckets, name):
    nbk, heads = rel_bias.shape
    nbr = buckets.shape[0]

    def body(rb_ref, bk_ref, o_ref):
        for br in range(nbr):
            bk = bk_ref[br]
            tiles = [jnp.full((BLK, 2 * BLK), NEG, F32) for _ in range(heads)]
            for b in range(nbk):
                hit = bk == b
                tiles = [jnp.where(hit, rb_ref[b, h], tiles[h]) for h in range(heads)]
            for h in range(heads):
                o_ref[br, h] = tiles[h]

    return _pcall(body, name=name, out_shape=_sds((nbr, heads, BLK, 2 * BLK), F32), grid=(1,),
                  in_specs=[pl.BlockSpec(memory_space=pltpu.SMEM), pl.BlockSpec(buckets.shape, lambda i: (0, 0, 0))],
                  out_specs=pl.BlockSpec((nbr, heads, BLK, 2 * BLK), lambda i: (0, 0, 0, 0)))(rel_bias, buckets)


def _bias_grad(dtiles, buckets, nbk, name):
    nbr, heads = dtiles.shape[:2]

    def body(dt_ref, bk_ref, o_ref):
        row = lax.broadcasted_iota(I32, (nbk, LANES), 0)
        col = lax.broadcasted_iota(I32, (nbk, LANES), 1)
        out = jnp.zeros((nbk, LANES), F32)
        for h in range(heads):
            for b in range(nbk):
                tot = jnp.zeros((), F32)
                for br in range(nbr):
                    tot = tot + jnp.sum(jnp.where(bk_ref[br] == b, dt_ref[br, h], 0.0))
                out = jnp.where((row == b) & (col == h), tot, out)
        o_ref[...] = out

    return _pcall(body, name=name, out_shape=_sds((nbk, LANES), F32), grid=(1,),
                  in_specs=[pl.BlockSpec(dtiles.shape, lambda i: (0, 0, 0, 0)), pl.BlockSpec(buckets.shape, lambda i: (0, 0, 0))],
                  out_specs=pl.BlockSpec((nbk, LANES), lambda i: (0, 0)))(dtiles, buckets)


def _largest_divisor(n, cap):
    return max(g for g in range(1, cap + 1) if n % g == 0)


def _attn_blocks(s, visit, group):
    for br, (window, d) in enumerate(DILATED_BRANCHES):
        n_blk = (s // d) // BLK
        span = BLK * d
        g1 = _largest_divisor(d, group)

        def firsts(t, carry, br=br, d=d, g1=g1):
            for j in range(g1):
                visit(br, d, t * g1 + j, False)
            return carry

        lax.fori_loop(0, d // g1, firsts, 0)
        if n_blk > 1:
            total = d * (n_blk - 1)
            g2 = _largest_divisor(total, group)

            def rest(t, carry, br=br, d=d, n_blk=n_blk, span=span, g2=g2):
                for j in range(g2):
                    idx = t * g2 + j
                    visit(br, d, idx // (n_blk - 1) + (1 + idx % (n_blk - 1)) * span, True)
                return carry

            lax.fori_loop(0, total // g2, rest, 0)


def _rows(start, size, d):
    return pl.ds(pl.multiple_of(start, BLK), size) if d == 1 else pl.ds(start, size, stride=d)


def _attn_fwd(z, btiles, col0, name, comm=None):
    s = z.shape[0]
    nbr, heads = btiles.shape[:2]
    pairs = heads // 2
    scale = D_HEAD ** -0.5
    rc = ROW_CHUNK

    def body(q_ref, k_ref, v_ref, bt_ref, yb_ref, lse_ref, acc_ref, m_ref, l_ref):
        even = lax.broadcasted_iota(I32, (BLK, LANES), 1) < D_HEAD
        even2 = lax.broadcasted_iota(I32, (2 * BLK, LANES), 1) < D_HEAD

        def visit(br, d, start, prev):
            kw = 2 * BLK if prev else BLK
            rows_q = _rows(start, BLK, d)
            rows_k = _rows(start - BLK * d, kw, d) if prev else rows_q
            qb = q_ref[rows_q, :]
            kb = k_ref[rows_k, :].astype(BF16)
            vw = v_ref[rows_k, :]
            ev_k = even2 if prev else even
            qm = jnp.concatenate([jnp.where(even, qb, 0.0), jnp.where(even, 0.0, qb)], axis=0).astype(BF16)
            bias = [bt_ref[br, e] if prev else bt_ref[br, e, :, BLK:] for e in range(2)]
            sc = _dot_nt(qm, kb) * scale + jnp.concatenate(bias, axis=0)
            m = jnp.max(sc, axis=1, keepdims=True)
            p = jnp.exp(sc - m)
            l = jnp.sum(p, axis=1, keepdims=True)
            pb = p.astype(BF16)
            vm = jnp.concatenate([jnp.where(ev_k, vw, 0.0), jnp.where(ev_k, 0.0, vw)], axis=0).astype(BF16)
            acc_ref.at[br][rows_q, :] = _dot(jnp.concatenate([pb[:BLK], pb[BLK:]], axis=1), vm)
            for e in range(2):
                m_ref.at[br, e][rows_q, :] = jnp.broadcast_to(m[e * BLK:(e + 1) * BLK], (BLK, LANES))
                l_ref.at[br, e][rows_q, :] = jnp.broadcast_to(l[e * BLK:(e + 1) * BLK], (BLK, LANES))

        _attn_blocks(s, visit, ATTN_GROUP_FWD)

        ev_c = lax.broadcasted_iota(I32, (rc, LANES), 1) < D_HEAD

        def merge(i, carry):
            rows = pl.ds(pl.multiple_of(i * rc, rc), rc)
            wts, dens = [], []
            for e in range(2):
                ms = [m_ref[br, e, rows, :] for br in range(nbr)]
                top = functools.reduce(jnp.maximum, ms)
                w = [jnp.exp(mb - top) for mb in ms]
                den = functools.reduce(lambda a, b: a + b, [w[br] * l_ref[br, e, rows, :] for br in range(nbr)])
                lse_ref[e, rows, :] = top + jnp.log(den)
                wts.append(w)
                dens.append(den)
            num = functools.reduce(lambda a, b: a + b,
                                   [jnp.where(ev_c, wts[0][br], wts[1][br]) * acc_ref[br, rows, :] for br in range(nbr)])
            yb_ref[rows, :] = num / jnp.where(ev_c, dens[0], dens[1])
            return carry

        lax.fori_loop(0, s // rc, merge, 0)

    def zcol(j):
        return pl.BlockSpec((s, LANES), lambda p, j=j: (0, col0 + j + p))

    return _pcall(
        body, name=name, out_shape=(_sds((s, pairs * LANES), F32), _sds((heads, s, LANES), F32)), grid=(pairs,),
        in_specs=[zcol(0), zcol(pairs), zcol(2 * pairs), pl.BlockSpec((nbr, 2, BLK, 2 * BLK), lambda p: (0, p, 0, 0))],
        out_specs=(pl.BlockSpec((s, LANES), lambda p: (0, p)), pl.BlockSpec((2, s, LANES), lambda p: (p, 0, 0))),
        scratch_shapes=[pltpu.VMEM((nbr, s, LANES), F32), pltpu.VMEM((nbr, 2, s, LANES), F32), pltpu.VMEM((nbr, 2, s, LANES), F32)],
        comm=comm,
    )(z, z, z, btiles)


def _attn_bwd(z, btiles, dyb, lse, delta, dbias_in, col0, name, comm=None):
    s = z.shape[0]
    nbr, heads = btiles.shape[:2]
    pairs = heads // 2
    scale = D_HEAD ** -0.5

    def body(q_ref, k_ref, v_ref, bt_ref, dy_ref, lse_ref, dl_ref, dbi_ref,
             dq_ref, dk_ref, dv_ref, db_ref, dqa, dka, dva):
        even = lax.broadcasted_iota(I32, (BLK, LANES), 1) < D_HEAD
        even2 = lax.broadcasted_iota(I32, (2 * BLK, LANES), 1) < D_HEAD
        for ref in (dqa, dka, dva):
            ref[...] = jnp.zeros((s, LANES), F32)
        db_ref[...] = dbi_ref[...]

        def visit(br, d, start, prev):
            kw = 2 * BLK if prev else BLK
            rows_q = _rows(start, BLK, d)
            rows_k = _rows(start - BLK * d, kw, d) if prev else rows_q
            qb = q_ref[rows_q, :]
            dyv = dy_ref[rows_q, :]
            kwin = k_ref[rows_k, :]
            kb = kwin.astype(BF16)
            vb = v_ref[rows_k, :].astype(BF16)
            ev_k = even2 if prev else even
            qm = jnp.concatenate([jnp.where(even, qb, 0.0), jnp.where(even, 0.0, qb)], axis=0).astype(BF16)
            dym = jnp.concatenate([jnp.where(even, dyv, 0.0), jnp.where(even, 0.0, dyv)], axis=0).astype(BF16)
            bias = [bt_ref[br, e] if prev else bt_ref[br, e, :, BLK:] for e in range(2)]
            sc = _dot_nt(qm, kb) * scale + jnp.concatenate(bias, axis=0)
            lt = jnp.concatenate([lse_ref.at[e][rows_q, :] for e in range(2)], axis=0)
            dt = jnp.concatenate([dl_ref.at[e][rows_q, :] for e in range(2)], axis=0)
            if prev:
                lt = jnp.concatenate([lt, lt], axis=1)
                dt = jnp.concatenate([dt, dt], axis=1)
            p = jnp.exp(sc - lt)
            ds = p * (_dot_nt(dym, vb) - dt)
            for e in range(2):
                if prev:
                    db_ref[br, e] += ds[e * BLK:(e + 1) * BLK]
                else:
                    db_ref[br, e, :, BLK:] += ds[e * BLK:(e + 1) * BLK]
            dsb = ds.astype(BF16)
            km = jnp.concatenate([jnp.where(ev_k, kwin, 0.0), jnp.where(ev_k, 0.0, kwin)], axis=0).astype(BF16)
            dqa[rows_q, :] += _dot(jnp.concatenate([dsb[:BLK], dsb[BLK:]], axis=1), km) * scale
            dka[rows_k, :] += _dot_tn(dsb, qm) * scale
            dva[rows_k, :] += _dot_tn(p.astype(BF16), dym)

        _attn_blocks(s, visit, ATTN_GROUP_BWD)
        dq_ref[...] = dqa[...].astype(BF16)
        dk_ref[...] = dka[...].astype(BF16)
        dv_ref[...] = dva[...].astype(BF16)

    def zcol(j):
        return pl.BlockSpec((s, LANES), lambda p, j=j: (0, col0 + j + p))

    col = pl.BlockSpec((s, LANES), lambda p: (0, p))
    stat = pl.BlockSpec((2, s, LANES), lambda p: (p, 0, 0))
    tile = pl.BlockSpec((nbr, 2, BLK, 2 * BLK), lambda p: (0, p, 0, 0))
    wide = _sds((s, pairs * LANES), BF16)
    return _pcall(
        body, name=name, out_shape=(wide, wide, wide, _sds(btiles.shape, F32)), grid=(pairs,),
        in_specs=[zcol(0), zcol(pairs), zcol(2 * pairs), tile, col, stat, stat, tile],
        out_specs=(col, col, col, tile),
        scratch_shapes=[pltpu.VMEM((s, LANES), F32) for _ in range(3)], comm=comm,
    )(z, z, z, btiles, dyb, lse, delta, dbias_in)


def _row(v):
    return v.reshape(1, -1)


class _LocalSchedule:
    def __init__(self):
        self.big = {}

    def fwd_comms(self, l):
        return {}

    def bwd_comms(self, l):
        return {}

    def after_bwd(self, l, grads):
        self.big[l] = grads


def _layer_fwd(l, x, wts, prm, btiles, comms):
    d = x.shape[1]
    wq = d // 4
    gout = prm["out_norm_g"][l]
    h = _rms_fwd(x, _row(prm["norm_mix_g"][l]), "rms_mix_fwd", comm=comms.get("rms_mix_fwd"))
    z = _mm_n(h, wts["in_t"], l, nt=True, tn=256, out_dtype=F32, name="in_proj")
    yan, ycn = _mix_fwd(z, prm["conv_a_w"][l], prm["conv_c_w"][l], _row(prm["conv_c_b"][l]), _row(prm["ln_c_g"][l]),
                        _row(prm["ln_c_b"][l]), _row(gout[:wq]), _row(gout[3 * wq:]), "mix_fwd")
    yb, lse = _attn_fwd(z, btiles, 3 * wq // LANES, "attn_fwd", comm=comms.get("attn_fwd"))
    y = _y_assemble(yan, yb, ycn, _row(gout[wq:3 * wq]), "y_assemble")
    x_mid = _mm_k(y, wts["out"], l, x, nt=False, tk=512, b_off=0, name="out_proj")
    h2 = _rms_fwd(x_mid, _row(prm["norm_ffn_g"][l]), "rms_ffn_fwd")
    up = _mm_n(h2, wts["up"], l, nt=False, tn=512, out_dtype=BF16, name="up_proj", comm=comms.get("up_proj"))
    act = _ffn_act_fwd(up, prm["conv_f_w"][l], "ffn_act_fwd", comm=comms.get("ffn_act_fwd"))
    x_out = _mm_k(act, wts["down"], l, x_mid, nt=False, tk=256, b_off=0, name="down_proj", comm=comms.get("down_proj"))
    return x_out, (x, h, z, yb, lse, y, x_mid, h2, up)


def _layer_bwd(l, dxo, dxo_b, saved, wts, prm, btiles, dbias, comms):
    x, h, z, yb, lse, y, x_mid, h2, up = saved
    d = x.shape[1]
    wq = d // 4
    f = up.shape[1] // 2
    gout = prm["out_norm_g"][l]
    dact = _mm_n(dxo_b, wts["down"], l, nt=True, tn=256, out_dtype=BF16, name="down_proj_dx", comm=comms.get(SWAP_RIDE))
    act, dgate, dval, dwf = _ffn_act_bwd(up, dact, prm["conv_f_w"][l], "ffn_act_bwd", comm=comms.get("ffn_act_bwd"))
    g_down = _mm_tn(act, dxo_b, t=256, name="down_proj_dw")
    dh2 = _mm_k(dgate, wts["up"], l, None, nt=True, tk=256, b_off=0, name="up_proj_dx_gate")
    dh2 = _mm_k(dval, wts["up"], l, dh2, nt=True, tk=256, b_off=f // 256, name="up_proj_dx_val")
    dxm, dxm_b, dg_ffn = _rms_bwd(x_mid, _row(prm["norm_ffn_g"][l]), dh2, dxo, "rms_ffn_bwd")
    g_up = _mm_tn2(h2, dgate, dval, t=256, name="up_proj_dw")
    dy = _mm_k(dxm_b, wts["out"], l, None, nt=True, tk=512, b_off=0, name="out_proj_dx")
    g_out = _mm_tn(y, dxm_b, t=256, name="out_proj_dw")
    dza, dzc, dwa, dwc, dcb, dlg, dlb, dga, dgc = _mix_bwd(
        z, dy, prm["conv_a_w"][l], prm["conv_c_w"][l], _row(prm["conv_c_b"][l]), _row(prm["ln_c_g"][l]),
        _row(prm["ln_c_b"][l]), _row(gout[:wq]), _row(gout[3 * wq:]), "mix_bwd")
    dyb, delta, dgb = _yb_norm_bwd(yb, dy, _row(gout[wq:3 * wq]), "yb_norm_bwd")
    dq, dk, dv, dbias = _attn_bwd(z, btiles, dyb, lse, delta, dbias, 3 * wq // LANES, "attn_bwd",
                                  comm=comms.get("attn_bwd"))
    dz = jnp.concatenate([dza, dq, dk, dv, dzc], axis=1)
    dh = _mm_k(dz, wts["in_t"], l, None, nt=False, tk=256, b_off=0, name="in_proj_dx")
    dx, dx_b, dg_mix = _rms_bwd(x, _row(prm["norm_mix_g"][l]), dh, dxm, "rms_mix_bwd")
    g_in_t = _mm_tn(dz, h, t=256, name="in_proj_dw")
    big = {"in_t": g_in_t, "out": g_out, "up": g_up, "down": g_down}
    small = {"norm_mix_g": dg_mix[0], "conv_a_w": dwa, "conv_c_w": dwc, "conv_c_b": dcb[0], "ln_c_g": dlg[0],
             "ln_c_b": dlb[0], "out_norm_g": jnp.concatenate([dga[0], dgb[0], dgc[0]]), "norm_ffn_g": dg_ffn[0],
             "conv_f_w": dwf}
    return dx, dx_b, big, small, dbias


def _local_step(x, tgt, wts, prm, sched):
    depth = prm["norm_mix_g"].shape[0]
    buckets = jnp.asarray(_t5_bucket_table())
    btiles = _bias_tiles(prm["rel_bias"], buckets, "bias_tiles")
    saved = []
    for l in range(depth):
        x, sv = _layer_fwd(l, x, wts, prm, btiles, sched.fwd_comms(l))
        saved.append(sv)
    loss, dx, dx_b, dg_final = _final_loss(x, _row(prm["final_g"]), tgt, "final_loss")
    dbias = jnp.zeros(btiles.shape, F32)
    small = [None] * depth
    for l in reversed(range(depth)):
        dx, dx_b, grads, small[l], dbias = _layer_bwd(l, dx, dx_b, saved[l], wts, prm, btiles, dbias, sched.bwd_comms(l))
        sched.after_bwd(l, grads)
    nbk, heads = prm["rel_bias"].shape
    d_rel = _bias_grad(dbias, buckets, nbk, "bias_grad")[:, :heads]
    return loss, dx, small, d_rel, dg_final[0]


BIG = ("in_t", "out", "up", "down")
COL_SHARDED = ("up",)
N_CHIPS = 4
N_DEV = 8
BF16_ROWS = 16


def _me():
    return lax.axis_index("x"), lax.axis_index("y"), lax.axis_index("c")


def _chip_of(x, y):
    return 2 * x + y


def _other_chips(x, y):
    return ((1 - x, y), (x, 1 - y), (1 - x, 1 - y))


def _remote(src, dst, send_sem, recv_sem, device):
    return pltpu.make_async_remote_copy(src_ref=src, dst_ref=dst, send_sem=send_sem, recv_sem=recv_sem,
                                        device_id=device, device_id_type=MESH)


def _ag_comm(wts, layer, ici_keys, fwd_keys):
    keys = tuple(k for k in BIG if k in ici_keys or k in fwd_keys)

    def geo(k):
        _, rows, cols = wts[k].shape
        return (rows, cols // N_CHIPS) if k in COL_SHARDED else (rows // N_CHIPS, cols)

    def copies(refs, sems):
        g = dict(zip(keys, refs))
        isend, irecv, dsend, drecv = sems
        x, y, c = _me()
        mine = _chip_of(x, y)

        def region(k, chip, half):
            r, cc = geo(k)
            h = r // 2
            if k in COL_SHARDED:
                return g[k].at[layer, pl.ds(pl.multiple_of(half * h, BF16_ROWS), h), pl.ds(pl.multiple_of(chip * cc, LANES), cc)]
            return g[k].at[layer, pl.ds(pl.multiple_of(chip * r + half * h, BF16_ROWS), h), :]

        def ici(k, f, landing):
            chip = _other_chips(x, y)[f]
            where = region(k, _chip_of(*chip) if landing else mine, c)
            i = keys.index(k)
            return _remote(where, where, isend.at[i, f], irecv.at[i, f], (*chip, c))

        def fwd(k, f, landing):
            chip = _other_chips(x, y)[f]
            where = region(k, _chip_of(*chip), 1 - c if landing else c)
            i = keys.index(k)
            return _remote(where, where, dsend.at[i, f], drecv.at[i, f], (x, y, 1 - c))

        return ici, fwd

    def start(ins, outs, sems):
        ici, fwd = copies(outs, sems)
        for k in keys:
            for f in range(3):
                if k in ici_keys:
                    ici(k, f, False).start()
                else:
                    fwd(k, f, False).start()

    def finish(ins, outs, sems):
        ici, fwd = copies(outs, sems)
        for k in keys:
            for f in range(3):
                if k in ici_keys:
                    ici(k, f, True).wait_recv()
                    if k in fwd_keys:
                        fwd(k, f, False).start()
        for k in keys:
            for f in range(3):
                if k in fwd_keys:
                    fwd(k, f, True).wait_recv()
                    fwd(k, f, False).wait_send()
                if k in ici_keys:
                    ici(k, f, False).wait_send()

    def done(res):
        wts.update(zip(keys, res))

    n = len(keys)
    return _Comm([wts[k] for k in keys], [_sds(wts[k].shape, BF16) for k in keys], {i: i for i in range(n)},
                 [pltpu.SemaphoreType.DMA((n, 3)) for _ in range(4)], start, finish, done)


def _small_gather_comm(slab, store):
    def copies(ins, outs, sems):
        send, recv, lsem = sems
        x, y, c = _me()
        mine = _chip_of(x, y)
        own = pltpu.make_async_copy(ins[0], outs[0].at[mine], lsem)
        pairs = []
        for f, chip in enumerate(_other_chips(x, y)):
            out = _remote(ins[0], outs[0].at[mine], send.at[f], recv.at[f], (*chip, c))
            land = _remote(ins[0], outs[0].at[_chip_of(*chip)], send.at[f], recv.at[f], (*chip, c))
            pairs.append((out, land))
        return own, pairs

    def start(ins, outs, sems):
        own, pairs = copies(ins, outs, sems)
        own.start()
        for out, _ in pairs:
            out.start()

    def finish(ins, outs, sems):
        own, pairs = copies(ins, outs, sems)
        for out, land in pairs:
            land.wait_recv()
            out.wait_send()
        own.wait()

    def done(res):
        store["small"] = res[0]

    return _Comm([slab], [_sds((N_CHIPS,) + slab.shape, F32)], {},
                 [pltpu.SemaphoreType.DMA((3,)), pltpu.SemaphoreType.DMA((3,)), pltpu.SemaphoreType.DMA], start, finish, done)


def _piece_geo(g):
    geo = {}
    for k in BIG:
        rows, cols = g[k].shape
        geo[k] = (rows // 2, cols // N_CHIPS) if k in COL_SHARDED else (rows // (2 * N_CHIPS), cols)
    return geo


def _swap_comm(g, done):
    geo = _piece_geo(g)
    n_copies = sum(N_CHIPS if k in COL_SHARDED else 1 for k in BIG)

    def copies(ins, outs, sems):
        g_refs, t_refs = dict(zip(BIG, ins)), dict(zip(BIG, outs))
        send, recv = sems
        x, y, c = _me()
        pairs = []
        for k in BIG:
            h, cc = geo[k]
            if k in COL_SHARDED:
                rows = pl.ds(pl.multiple_of((1 - c) * h, BF16_ROWS), h)
                pairs += [(g_refs[k].at[rows, pl.ds(j * cc, cc)], t_refs[k].at[j]) for j in range(N_CHIPS)]
            else:
                pairs.append((g_refs[k].at[:, 1 - c], t_refs[k]))
        return [_remote(src, dst, send.at[i], recv.at[i], (x, y, 1 - c)) for i, (src, dst) in enumerate(pairs)]

    def start(ins, outs, sems):
        for cp in copies(ins, outs, sems):
            cp.start()

    def finish(ins, outs, sems):
        for cp in copies(ins, outs, sems):
            cp.wait()

    ins = [g[k] if k in COL_SHARDED else g[k].reshape(N_CHIPS, 2, geo[k][0], geo[k][1]) for k in BIG]
    return _Comm(ins, [_sds((N_CHIPS,) + geo[k], BF16) for k in BIG], {},
                 [pltpu.SemaphoreType.DMA((n_copies,)) for _ in range(2)], start, finish,
                 lambda res: done(dict(zip(BIG, res))))


def _pair_sum(g, theirs, c_arr):
    geo = _piece_geo(g)

    def body(c_ref, *refs):
        nk = len(BIG)
        for i in range(nk):
            refs[2 * nk + i][...] = (refs[i][...].astype(F32) + refs[nk + i][...].astype(F32)).astype(BF16)

    in_specs, ins = [], []
    for k in BIG:
        h, cc = geo[k]
        if k in COL_SHARDED:
            in_specs.append(pl.BlockSpec((h, cc), lambda j, c_ref: (c_ref[0], j)))
            ins.append(g[k])
        else:
            in_specs.append(pl.BlockSpec((None, h, cc), lambda j, c_ref: (2 * j + c_ref[0], 0, 0)))
            ins.append(g[k].reshape(2 * N_CHIPS, h, cc))
    slab = [pl.BlockSpec((None,) + geo[k], lambda j, c_ref: (j, 0, 0)) for k in BIG]
    res = _pcall(body, name="rs_pair_sum", out_shape=tuple(_sds((N_CHIPS,) + geo[k], BF16) for k in BIG), grid=(N_CHIPS,),
                 in_specs=in_specs + slab, out_specs=tuple(slab), prefetch=1)(c_arr, *ins, *[theirs[k] for k in BIG])
    return dict(zip(BIG, res))


def _rs_comm(p, keys, store):
    def copies(ins, outs, sems):
        send, recv = sems
        x, y, c = _me()
        return [_remote(ins[i].at[_chip_of(*chip)], outs[i].at[f], send.at[i, f], recv.at[i, f], (*chip, c))
                for i in range(len(keys)) for f, chip in enumerate(_other_chips(x, y))]

    def start(ins, outs, sems):
        for cp in copies(ins, outs, sems):
            cp.start()

    def finish(ins, outs, sems):
        for cp in copies(ins, outs, sems):
            cp.wait()

    def done(res):
        store.update(zip(keys, res))

    return _Comm([p[k] for k in keys], [_sds((3,) + p[k].shape[1:], BF16) for k in keys], {},
                 [pltpu.SemaphoreType.DMA((len(keys), 3)) for _ in range(2)], start, finish, done)


def _quad_sum(p, b, where, l, full):
    parts = 2
    nk = len(BIG)

    def body(where_ref, *refs):
        for i in range(nk):
            acc = refs[i][...].astype(F32)
            for f in range(3):
                acc = acc + refs[nk + 3 * i + f][...].astype(F32)
            refs[5 * nk + i][...] = acc

    own, recv, outs = [], [], []
    for k in BIG:
        h, cc = p[k].shape[1:]
        th = h // parts
        own.append(pl.BlockSpec((None, th, cc), lambda i, w_ref: (w_ref[0], i, 0)))
        recv += [pl.BlockSpec((None, th, cc), lambda i, w_ref, f=f: (f, i, 0)) for f in range(3)]
        outs.append(pl.BlockSpec((None, None, th, cc), lambda i, w_ref: (l, w_ref[1], i, 0)))
    args = [p[k] for k in BIG] + [b[k] for k in BIG for _ in range(3)] + [full[k] for k in BIG]
    res = _pcall(body, name="rs_quad_sum", out_shape=tuple(_sds(full[k].shape, F32) for k in BIG), grid=(parts,),
                 in_specs=own + recv + [ANY] * nk, out_specs=tuple(outs), prefetch=1,
                 aliases={1 + 4 * nk + i: i for i in range(nk)})(where, *args)
    return dict(zip(BIG, res))


def _rs_share(l, full):
    nk = len(BIG)

    def body(*refs):
        f_refs = refs[nk:2 * nk]
        send, recv = refs[2 * nk:]
        x, y, c = _me()
        started = []
        for i in range(nk):
            cp = _remote(f_refs[i].at[l, c], f_refs[i].at[l, c], send.at[i], recv.at[i], (x, y, 1 - c))
            cp.start()
            started.append(cp)
        for i, cp in enumerate(started):
            _remote(f_refs[i].at[l, 1 - c], f_refs[i].at[l, 1 - c], send.at[i], recv.at[i], (x, y, 1 - c)).wait_recv()
            cp.wait_send()

    res = _pcall(body, name="rs_share", out_shape=tuple(_sds(full[k].shape, F32) for k in BIG),
                 in_specs=[ANY] * nk, out_specs=tuple([ANY] * nk), aliases={i: i for i in range(nk)},
                 scratch_shapes=[pltpu.SemaphoreType.DMA((nk,)) for _ in range(2)])(*[full[k] for k in BIG])
    return dict(zip(BIG, res))


def _gather_partials(slab):
    def body(s_ref, o_ref, send, recv):
        x, y, c = _me()
        me = 4 * x + 2 * y + c
        started = []
        peers = []
        for mask in range(1, N_DEV):
            peer = (x ^ (mask >> 2), y ^ ((mask >> 1) & 1), c ^ (mask & 1))
            peers.append(peer)
            cp = _remote(s_ref, o_ref.at[me], send.at[mask - 1], recv.at[mask - 1], peer)
            cp.start()
            started.append(cp)
        for i, peer in enumerate(peers):
            _remote(s_ref, o_ref.at[4 * peer[0] + 2 * peer[1] + peer[2]], send.at[i], recv.at[i], peer).wait_recv()
        for cp in started:
            cp.wait_send()

    return _pcall(body, name="gather_partials", out_shape=_sds((N_DEV,) + slab.shape, F32), in_specs=[ANY], out_specs=ANY,
                  scratch_shapes=[pltpu.SemaphoreType.DMA((N_DEV - 1,)), pltpu.SemaphoreType.DMA((N_DEV - 1,))])(slab)


def _sum_slabs(slabs, own, me):
    n, r, lanes = slabs.shape
    tr = r // 2

    def body(me_ref, s_ref, own_ref, o_ref):
        o_ref[...] = jnp.zeros((tr, lanes), F32)
        for i in range(n):
            @pl.when(me_ref[0] == i)
            def _():
                o_ref[...] += own_ref[...]

            @pl.when(me_ref[0] != i)
            def _():
                o_ref[...] += s_ref[i]

    return _pcall(body, name="sum_partials", out_shape=_sds((r, lanes), F32), grid=(2,),
                  in_specs=[pl.BlockSpec((n, tr, lanes), lambda i, me_ref: (0, i, 0)),
                            pl.BlockSpec((tr, lanes), lambda i, me_ref: (i, 0))],
                  out_specs=pl.BlockSpec((tr, lanes), lambda i, me_ref: (i, 0)), prefetch=1)(me, slabs, own)


def _cast_into_gathered(w, chip, by_cols, name):
    l, r, c = w.shape

    def body(chip_ref, w_ref, o_ref):
        o_ref[...] = w_ref[...].astype(BF16)

    if by_cols:
        shape, out = (l, r, N_CHIPS * c), pl.BlockSpec((None, r, c), lambda i, chip_ref: (i, 0, chip_ref[0]))
    else:
        shape, out = (l, N_CHIPS * r, c), pl.BlockSpec((None, r, c), lambda i, chip_ref: (i, chip_ref[0], 0))
    return _pcall(body, name=name, out_shape=_sds(shape, BF16), grid=(l,),
                  in_specs=[pl.BlockSpec((None, r, c), lambda i, chip_ref: (i, 0, 0))], out_specs=out, prefetch=1)(chip, w)


def _adamw(w, g, m, v, name, tr):
    r, c = w.shape

    def body(w_ref, g_ref, m_ref, v_ref, d_ref, mo_ref, vo_ref):
        gv = g_ref[...]
        mn = ADAM_B1 * m_ref[...] + (1.0 - ADAM_B1) * gv
        vn = ADAM_B2 * v_ref[...] + (1.0 - ADAM_B2) * (gv * gv)
        m_hat = mn / (1.0 - ADAM_B1 ** ADAM_STEP)
        v_hat = vn / (1.0 - ADAM_B2 ** ADAM_STEP)
        d_ref[...] = -ADAM_LR * (m_hat / (jnp.sqrt(v_hat) + ADAM_EPS) + ADAM_WD * w_ref[...])
        mo_ref[...] = mn
        vo_ref[...] = vn

    blk = pl.BlockSpec((tr, c), lambda i: (i, 0))
    return _pcall(body, name=name, out_shape=tuple(_sds((r, c), F32) for _ in range(3)), grid=(r // tr,),
                  in_specs=[blk] * 4, out_specs=(blk, blk, blk))(w, g, m, v)


AG_RIDES = {"attn_fwd": (1, ("up",), ()), "up_proj": (1, ("in_t",), ()), "ffn_act_fwd": (1, ("down",), ("up",)),
            "down_proj": (1, ("out",), ("in_t",)), "rms_mix_fwd": (0, (), ("out", "down"))}
RS_RIDES = {"ffn_act_bwd": ("up",), "attn_bwd": ("in_t", "out", "down")}
SWAP_RIDE = "down_proj_dx"


class _Rides:
    def __init__(self, table, build):
        self.table, self.build = table, build

    def get(self, name):
        return self.build(self.table[name]) if name in self.table else None


class _MeshSchedule:
    def __init__(self, wts, depth, c_arr, where):
        self.wts, self.depth, self.c_arr, self.where = wts, depth, c_arr, where
        self.grads, self.pairs, self.recv, self.full = None, None, {}, None

    def fwd_comms(self, l):
        table = {name: (l + off, ici, fwd) for name, (off, ici, fwd) in AG_RIDES.items() if 1 <= l + off < self.depth}
        return _Rides(table, lambda ride: _ag_comm(self.wts, *ride))

    def _swapped(self, theirs):
        self.pairs = _pair_sum(self.grads, theirs, self.c_arr)

    def bwd_comms(self, l):
        if self.grads is None:
            return {}
        table = dict(RS_RIDES)
        table[SWAP_RIDE] = None
        return _Rides(table, lambda keys: _swap_comm(self.grads, self._swapped) if keys is None
                      else _rs_comm(self.pairs, keys, self.recv))

    def _finish(self, l):
        self.full = _rs_share(l, _quad_sum(self.pairs, self.recv, self.where, l, self.full))
        self.grads, self.pairs, self.recv = None, None, {}

    def after_bwd(self, l, grads):
        if self.grads is not None:
            self._finish(l + 1)
        if self.full is None:
            geo = _piece_geo(grads)
            self.full = {k: jnp.zeros((self.depth, 2) + geo[k], F32) for k in BIG}
        self.grads = grads
        if l == 0:
            _run_comm(_swap_comm(grads, self._swapped), "rs_swap_halves")
            _run_comm(_rs_comm(self.pairs, BIG, self.recv), "rs_to_owners")
            self._finish(0)


SHARDED_SMALL = ("conv_a_w", "conv_c_w", "conv_f_w")
SMALL = ("norm_mix_g", "conv_a_w", "conv_c_w", "conv_c_b", "ln_c_g", "ln_c_b", "out_norm_g", "norm_ffn_g",
         "conv_f_w", "rel_bias", "final_g")
SLAB_ROWS = 16


def _pack(arrays):
    flat = jnp.concatenate([a.reshape(-1) for a in arrays])
    unit = SLAB_ROWS * LANES
    total = -(-flat.shape[0] // unit) * unit
    return jnp.pad(flat, (0, total - flat.shape[0])).reshape(-1, LANES)


def _unpack(slab, shapes):
    flat = slab.reshape(-1)
    out, off = [], 0
    for shp in shapes:
        size = math.prod(shp)
        out.append(flat[off:off + size].reshape(shp))
        off += size
    return out


def kernel(x, norm_mix_g, w_in, conv_a_w, conv_c_w, conv_c_b, ln_c_g, ln_c_b, out_norm_g, w_out, norm_ffn_g, w_up, conv_f_w, w_down, rel_bias, final_g, loss_target, m_norm_mix_g, m_w_in, m_conv_a_w, m_conv_c_w, m_conv_c_b, m_ln_c_g, m_ln_c_b, m_out_norm_g, m_w_out, m_norm_ffn_g, m_w_up, m_conv_f_w, m_w_down, m_rel_bias, m_final_g, v_norm_mix_g, v_w_in, v_conv_a_w, v_conv_c_w, v_conv_c_b, v_ln_c_g, v_ln_c_b, v_out_norm_g, v_w_out, v_norm_ffn_g, v_w_up, v_conv_f_w, v_w_down, v_rel_bias, v_final_g):
    weights = dict(norm_mix_g=norm_mix_g, w_in=w_in, conv_a_w=conv_a_w, conv_c_w=conv_c_w, conv_c_b=conv_c_b,
                   ln_c_g=ln_c_g, ln_c_b=ln_c_b, out_norm_g=out_norm_g, w_out=w_out, norm_ffn_g=norm_ffn_g, w_up=w_up,
                   conv_f_w=conv_f_w, w_down=w_down, rel_bias=rel_bias, final_g=final_g)
    mom_m = dict(norm_mix_g=m_norm_mix_g, w_in=m_w_in, conv_a_w=m_conv_a_w, conv_c_w=m_conv_c_w, conv_c_b=m_conv_c_b,
                 ln_c_g=m_ln_c_g, ln_c_b=m_ln_c_b, out_norm_g=m_out_norm_g, w_out=m_w_out, norm_ffn_g=m_norm_ffn_g,
                 w_up=m_w_up, conv_f_w=m_conv_f_w, w_down=m_w_down, rel_bias=m_rel_bias, final_g=m_final_g)
    mom_v = dict(norm_mix_g=v_norm_mix_g, w_in=v_w_in, conv_a_w=v_conv_a_w, conv_c_w=v_conv_c_w, conv_c_b=v_conv_c_b,
                 ln_c_g=v_ln_c_g, ln_c_b=v_ln_c_b, out_norm_g=v_out_norm_g, w_out=v_w_out, norm_ffn_g=v_norm_ffn_g,
                 w_up=v_w_up, conv_f_w=v_conv_f_w, w_down=v_w_down, rel_bias=v_rel_bias, final_g=v_final_g)
    xi, yi, ci = _me()
    chip = _chip_of(xi, yi)
    c_arr = jnp.reshape(ci, (1,)).astype(I32)
    chip_arr = jnp.reshape(chip, (1,)).astype(I32)
    me_arr = jnp.reshape(4 * xi + 2 * yi + ci, (1,)).astype(I32)
    where = jnp.stack([chip, ci]).astype(I32)
    depth = w_out.shape[0]

    wts = {"in_t": _cast_into_gathered(jnp.swapaxes(w_in, 1, 2), chip_arr, False, "cast_in"),
           "out": _cast_into_gathered(w_out, chip_arr, False, "cast_out"),
           "up": _cast_into_gathered(w_up, chip_arr, True, "cast_up"),
           "down": _cast_into_gathered(w_down, chip_arr, False, "cast_down")}
    store = {}
    _run_comm(_small_gather_comm(_pack([weights[n] for n in SHARDED_SMALL]), store), "ag_small")
    _run_comm(_ag_comm(wts, 0, BIG, BIG), "ag_weights")
    prm = {n: weights[n] for n in SMALL if n not in SHARDED_SMALL}
    per_chip = [_unpack(store["small"][j], [weights[n].shape for n in SHARDED_SMALL]) for j in range(N_CHIPS)]
    for i, n in enumerate(SHARDED_SMALL):
        prm[n] = jnp.concatenate([per_chip[j][i] for j in range(N_CHIPS)], axis=-1)

    sched = _MeshSchedule(wts, depth, c_arr, where)
    loss_row, dx, small, d_rel, d_final = _local_step(x[0], loss_target[0], wts, prm, sched)
    reduced = sched.full
    loss = lax.psum(loss_row[0, 0], ("x", "y", "c"))

    stacked = {n: jnp.stack([small[l][n] for l in range(depth)]) for n in small[0]}
    stacked["rel_bias"] = d_rel
    stacked["final_g"] = d_final
    full_shapes = [stacked[n].shape for n in SMALL]
    partial = _pack([stacked[n] for n in SMALL])
    summed = _unpack(_sum_slabs(_gather_partials(partial), partial, me_arr), full_shapes)
    grads = {}
    for n, g in zip(SMALL, summed):
        if n in SHARDED_SMALL:
            width = weights[n].shape[-1]
            g = lax.dynamic_slice_in_dim(g, chip * width, width, axis=g.ndim - 1)
        grads[n] = g

    shard_shapes = {"in_t": jnp.swapaxes(w_in, 1, 2).shape, "out": w_out.shape, "up": w_up.shape, "down": w_down.shape}
    red = {k: reduced[k].reshape(shard_shapes[k]) for k in BIG}
    grads["w_in"] = jnp.swapaxes(red["in_t"], 1, 2)
    grads["w_out"], grads["w_up"], grads["w_down"] = red["out"], red["up"], red["down"]

    delta, new_m, new_v = {}, {}, {}
    for n in ("w_in", "w_out", "w_up", "w_down"):
        shp = weights[n].shape
        flat = lambda a, shp=shp: a.reshape(shp[0] * shp[1], shp[2])
        tile = max(t for t in range(8, 257, 8) if shp[1] % t == 0)
        d, mn, vn = _adamw(flat(weights[n]), flat(grads[n]), flat(mom_m[n]), flat(mom_v[n]), "adamw_" + n, tile)
        delta[n], new_m[n], new_v[n] = d.reshape(shp), mn.reshape(shp), vn.reshape(shp)
    shapes = [weights[n].shape for n in SMALL]
    packed = [_pack([src[n] for n in SMALL]) for src in (weights, grads, mom_m, mom_v)]
    d, mn, vn = _adamw(*packed, "adamw_small", packed[0].shape[0] // 2)
    for n, a, b, c in zip(SMALL, _unpack(d, shapes), _unpack(mn, shapes), _unpack(vn, shapes)):
        delta[n], new_m[n], new_v[n] = a, b, c

    order = ("norm_mix_g", "w_in", "conv_a_w", "conv_c_w", "conv_c_b", "ln_c_g", "ln_c_b", "out_norm_g", "w_out",
             "norm_ffn_g", "w_up", "conv_f_w", "w_down", "rel_bias", "final_g")
    return (loss, dx[None], *[grads[n] for n in order], *[delta[n] for n in order], *[new_m[n] for n in order],
            *[new_v[n] for n in order])
```

```python
import functools
import math

import numpy as np
import jax
import jax.numpy as jnp
from jax import lax
from jax.experimental import pallas as pl
from jax.experimental.pallas import tpu as pltpu

F32 = jnp.float32
BF16 = jnp.bfloat16
I32 = jnp.int32

EPS = 1e-6
NEG = -1e30
D_HEAD = 64
LANES = 128
BLK = 128
ATTN_GROUP_FWD = 4
ATTN_GROUP_BWD = 4
DILATED_BRANCHES = ((128, 1), (512, 4), (2048, 16))
NUM_BUCKETS = 32
MAX_DISTANCE = 2048
SHORT_CONV = 3
CONFORMER_CONV = 31
FFN_CONV = 3
PAD_SHORT = 8
PAD_LONG = 32
ROW_CHUNK = 256
V7X_VMEM_BYTES = 64 * 1024 * 1024
VMEM_REQUEST = V7X_VMEM_BYTES * 7 // 8

ADAM_LR = 0.001
ADAM_B1 = 0.9
ADAM_B2 = 0.999
ADAM_EPS = 1e-08
ADAM_WD = 0.01
ADAM_STEP = 10

MESH = pl.DeviceIdType.MESH
ANY = pl.BlockSpec(memory_space=pl.ANY)


def _sds(shape, dtype):
    return jax.ShapeDtypeStruct(tuple(shape), dtype)


class _Comm:
    def __init__(self, ins, out_shapes, aliases, sems, start, finish, done):
        self.ins, self.out_shapes, self.aliases, self.sems = list(ins), list(out_shapes), dict(aliases), list(sems)
        self.start, self.finish, self.done = start, finish, done


def _pcall(body, *, name, out_shape, grid=(), in_specs=None, out_specs=None, scratch_shapes=(), vmem=VMEM_REQUEST,
           aliases=None, prefetch=0, comm=None):
    params = pltpu.CompilerParams(dimension_semantics=("arbitrary",) * len(grid), vmem_limit_bytes=vmem)
    single = not isinstance(out_shape, (tuple, list))
    outs = [out_shape] if single else list(out_shape)
    ospecs = [out_specs] if single else list(out_specs)
    ispecs, scratch, aliases = list(in_specs), list(scratch_shapes), dict(aliases or {})
    n_in, n_out, n_scr = len(ispecs), len(outs), len(scratch)
    kernel_body = body
    if comm is not None:
        n_ci, n_co = len(comm.ins), len(comm.out_shapes)

        def kernel_body(*refs):
            pre, rest = refs[:prefetch], refs[prefetch:]
            core_in, c_in = rest[:n_in], rest[n_in:n_in + n_ci]
            o0 = n_in + n_ci
            core_out, c_out = rest[o0:o0 + n_out], rest[o0 + n_out:o0 + n_out + n_co]
            s0 = o0 + n_out + n_co
            core_scr, c_sem = rest[s0:s0 + n_scr], rest[s0 + n_scr:]
            first = functools.reduce(jnp.logical_and, [pl.program_id(a) == 0 for a in range(len(grid))])
            last = functools.reduce(jnp.logical_and, [pl.program_id(a) == grid[a] - 1 for a in range(len(grid))])
            pl.when(first)(lambda: comm.start(c_in, c_out, c_sem))
            body(*pre, *core_in, *core_out, *core_scr)
            pl.when(last)(lambda: comm.finish(c_in, c_out, c_sem))

        for i, o in comm.aliases.items():
            aliases[prefetch + n_in + i] = n_out + o
        ispecs += [ANY] * n_ci
        ospecs += [ANY] * n_co
        outs += comm.out_shapes
        scratch += comm.sems
    if prefetch:
        spec = pltpu.PrefetchScalarGridSpec(num_scalar_prefetch=prefetch, grid=grid, in_specs=ispecs,
                                            out_specs=tuple(ospecs), scratch_shapes=scratch)
        call = pl.pallas_call(kernel_body, name=name, out_shape=tuple(outs), grid_spec=spec,
                              input_output_aliases=aliases, compiler_params=params)
    else:
        call = pl.pallas_call(kernel_body, name=name, out_shape=tuple(outs), grid=grid, in_specs=ispecs,
                              out_specs=tuple(ospecs), scratch_shapes=scratch, input_output_aliases=aliases,
                              compiler_params=params)

    def run(*args):
        res = call(*args, *(comm.ins if comm is not None else ()))
        if comm is not None:
            comm.done(res[n_out:])
        return res[0] if single else tuple(res[:n_out])

    return run


def _run_comm(comm, name):
    def body(*refs):
        n_ci, n_co = len(comm.ins), len(comm.out_shapes)
        c_in, c_out, c_sem = refs[:n_ci], refs[n_ci:n_ci + n_co], refs[n_ci + n_co:]
        comm.start(c_in, c_out, c_sem)
        comm.finish(c_in, c_out, c_sem)

    res = pl.pallas_call(body, name=name, out_shape=tuple(comm.out_shapes), in_specs=[ANY] * len(comm.ins),
                         out_specs=tuple([ANY] * len(comm.out_shapes)), scratch_shapes=comm.sems,
                         input_output_aliases=comm.aliases)(*comm.ins)
    comm.done(res)


def _dot(a, b):
    return lax.dot_general(a, b, (((1,), (0,)), ((), ())), preferred_element_type=F32)


def _dot_nt(a, b):
    return lax.dot_general(a, b, (((1,), (1,)), ((), ())), preferred_element_type=F32)


def _dot_tn(a, b):
    return lax.dot_general(a, b, (((0,), (0,)), ((), ())), preferred_element_type=F32)


def _sigmoid(x):
    return 1.0 / (1.0 + jnp.exp(-x))


def _rstd(x):
    return lax.rsqrt(jnp.mean(x * x, axis=-1, keepdims=True) + EPS)


def _rms_fwd(x, g, name, comm=None):
    s, d = x.shape
    tm = ROW_CHUNK

    def body(x_ref, g_ref, o_ref):
        xv = x_ref[...]
        o_ref[...] = (xv * _rstd(xv) * g_ref[...]).astype(BF16)

    return _pcall(body, name=name, out_shape=_sds((s, d), BF16), grid=(s // tm,),
                  in_specs=[pl.BlockSpec((tm, d), lambda i: (i, 0)), pl.BlockSpec((1, d), lambda i: (0, 0))],
                  out_specs=pl.BlockSpec((tm, d), lambda i: (i, 0)), comm=comm)(x, g)


def _rms_bwd(x, g, dh, dres, name):
    s, d = x.shape
    tm = ROW_CHUNK

    def body(x_ref, g_ref, dh_ref, dres_ref, dx_ref, dxb_ref, dg_ref):
        i = pl.program_id(0)
        xv = x_ref[...]
        r = _rstd(xv)
        xh = xv * r
        dhv = dh_ref[...]
        gd = dhv * g_ref[...]
        dx = dres_ref[...] + r * (gd - xh * jnp.mean(gd * xh, axis=-1, keepdims=True))
        dx_ref[...] = dx
        dxb_ref[...] = dx.astype(BF16)
        part = jnp.sum(dhv * xh, axis=0, keepdims=True)

        @pl.when(i == 0)
        def _():
            dg_ref[...] = part

        @pl.when(i > 0)
        def _():
            dg_ref[...] += part

    row = pl.BlockSpec((tm, d), lambda i: (i, 0))
    vec = pl.BlockSpec((1, d), lambda i: (0, 0))
    return _pcall(body, name=name, out_shape=(_sds((s, d), F32), _sds((s, d), BF16), _sds((1, d), F32)),
                  grid=(s // tm,), in_specs=[row, vec, row, row], out_specs=(row, row, vec))(x, g, dh, dres)


def _final_loss(x, g, tgt, name):
    s, d = x.shape
    tm = ROW_CHUNK

    def body(x_ref, g_ref, t_ref, loss_ref, dx_ref, dxb_ref, dg_ref):
        i = pl.program_id(0)
        xv = x_ref[...]
        r = _rstd(xv)
        xh = xv * r
        e = xh * g_ref[...] - t_ref[...]
        lpart = 0.5 * jnp.sum(jnp.mean(e * e, axis=-1, keepdims=True), axis=0, keepdims=True)
        dy = e * (1.0 / d)
        gd = dy * g_ref[...]
        dx = r * (gd - xh * jnp.mean(gd * xh, axis=-1, keepdims=True))
        dx_ref[...] = dx
        dxb_ref[...] = dx.astype(BF16)
        part = jnp.sum(dy * xh, axis=0, keepdims=True)
        lrow = jnp.broadcast_to(lpart, (1, LANES))

        @pl.when(i == 0)
        def _():
            dg_ref[...] = part
            loss_ref[...] = lrow

        @pl.when(i > 0)
        def _():
            dg_ref[...] += part
            loss_ref[...] += lrow

    row = pl.BlockSpec((tm, d), lambda i: (i, 0))
    vec = pl.BlockSpec((1, d), lambda i: (0, 0))
    return _pcall(body, name=name,
                  out_shape=(_sds((1, LANES), F32), _sds((s, d), F32), _sds((s, d), BF16), _sds((1, d), F32)),
                  grid=(s // tm,), in_specs=[row, vec, row],
                  out_specs=(pl.BlockSpec((1, LANES), lambda i: (0, 0)), row, row, vec))(x, g, tgt)


def _mm_n(a, b, layer, *, nt, tn, out_dtype, name, resid=None, b_part=0, comm=None):
    s, k = a.shape
    n = b.shape[1] if nt else b.shape[2]
    rows = 512

    def body(a_ref, b_ref, *refs):
        o_ref = refs[-1]
        bv = b_ref[...]
        for r0 in range(0, s, rows):
            av = a_ref[r0:r0 + rows, :]
            prod = _dot_nt(av, bv) if nt else _dot(av, bv)
            if resid is not None:
                prod = refs[0][r0:r0 + rows, :] + prod
            o_ref[r0:r0 + rows, :] = prod.astype(out_dtype)

    b_spec = (pl.BlockSpec((None, tn, k), lambda j: (layer, j, b_part)) if nt
              else pl.BlockSpec((None, k, tn), lambda j: (layer, b_part, j)))
    col = pl.BlockSpec((s, tn), lambda j: (0, j))
    extra = () if resid is None else (resid,)
    return _pcall(body, name=name, out_shape=_sds((s, n), out_dtype), grid=(n // tn,),
                  in_specs=[pl.BlockSpec((s, k), lambda j: (0, 0)), b_spec] + [col] * len(extra),
                  out_specs=col, comm=comm)(a, b, *extra)


def _mm_tn(a, b, *, t, name):
    s, ka = a.shape
    n = b.shape[1]

    def body(a_ref, b_ref, o_ref):
        o_ref[...] = _dot_tn(a_ref[...], b_ref[...]).astype(BF16)

    return _pcall(body, name=name, out_shape=_sds((ka, n), BF16), grid=(ka // t,),
                  in_specs=[pl.BlockSpec((s, t), lambda i: (0, i)), pl.BlockSpec((s, n), lambda i: (0, 0))],
                  out_specs=pl.BlockSpec((t, n), lambda i: (i, 0)))(a, b)


def _mm_tn2(a, b_lo, b_hi, *, t, name):
    s, ka = a.shape
    half = b_lo.shape[1]
    nb = half // t

    def body(a_ref, lo_ref, hi_ref, o_ref):
        j = pl.program_id(0)

        @pl.when(j < nb)
        def _():
            o_ref[...] = _dot_tn(a_ref[...], lo_ref[...]).astype(BF16)

        @pl.when(j >= nb)
        def _():
            o_ref[...] = _dot_tn(a_ref[...], hi_ref[...]).astype(BF16)

    return _pcall(body, name=name, out_shape=_sds((ka, 2 * half), BF16), grid=(2 * nb,),
                  in_specs=[pl.BlockSpec((s, ka), lambda j: (0, 0)),
                            pl.BlockSpec((s, t), lambda j: (0, jnp.minimum(j, nb - 1))),
                            pl.BlockSpec((s, t), lambda j: (0, jnp.maximum(j - nb, 0)))],
                  out_specs=pl.BlockSpec((ka, t), lambda j: (0, j)))(a, b_lo, b_hi)


SUBLANES = 8


def _tap_windows(win, width, lead, rows):
    offs = [lead + k for k in range(width)]
    if width <= SUBLANES:
        return [win[o:o + rows, :] for o in offs]
    n = win.shape[0]
    out = {}
    for r in sorted({o % SUBLANES for o in offs}):
        base = win if r == 0 else pltpu.roll(win, n - r, axis=0)
        for o in offs:
            if o % SUBLANES == r:
                out[o - lead] = base[o - r:o - r + rows, :]
    return [out[k] for k in range(width)]


def _conv_taps(taps, w_ref):
    acc = None
    for k, tap in enumerate(taps):
        term = w_ref[pl.ds(k, 1), :] * tap
        acc = term if acc is None else acc + term
    return acc


def _causal_taps(win, width, pad, rows):
    return _tap_windows(win, width, pad - (width - 1), rows)


def _anticausal_taps(win, width, rows):
    return _tap_windows(win, width, 0, rows)[::-1]


def _conv_wgrad(dw_ref, g, taps):
    for k, tap in enumerate(taps):
        dw_ref[pl.ds(k, 1), :] += jnp.sum(g * tap, axis=0, keepdims=True)


def _mixer_a_fwd(ab, taps_t, wa_ref):
    ct = _conv_taps(taps_t, wa_ref)
    return ab * ct, ct


def _mixer_c_fwd(taps_u, wc_ref, cb_ref, lg_ref, lb_ref):
    u = _conv_taps(taps_u, wc_ref) + cb_ref[...]
    mu = jnp.mean(u, axis=-1, keepdims=True)
    uc = u - mu
    rs = lax.rsqrt(jnp.mean(uc * uc, axis=-1, keepdims=True) + EPS)
    uh = uc * rs
    ln = uh * lg_ref[...] + lb_ref[...]
    sg = _sigmoid(ln)
    return ln * sg, ln, sg, uh, rs


def _mix_fwd(z, wa, wc, cb, lg, lb, ga, gc, name):
    s = z.shape[0]
    w = wa.shape[1]
    nblk = z.shape[1] // w
    rc = ROW_CHUNK

    def body(ah_ref, ab_ref, ac_ref, cv_ref, cg_ref, wa_ref, wc_ref, cb_ref, lg_ref, lb_ref, ga_ref, gc_ref,
             ya_ref, yc_ref, tpad, upad):
        tpad[pl.ds(0, PAD_SHORT), :] = jnp.zeros((PAD_SHORT, w), F32)
        upad[pl.ds(0, PAD_LONG), :] = jnp.zeros((PAD_LONG, w), F32)

        def chunk(i, carry):
            base = pl.multiple_of(i * rc, rc)
            rows = pl.ds(base, rc)
            ah, ab, ac = ah_ref[rows, :], ab_ref[rows, :], ac_ref[rows, :]
            tpad[pl.ds(base + PAD_SHORT, rc), :] = ac * ah
            ya, _ = _mixer_a_fwd(ab, _causal_taps(tpad[pl.ds(base, rc + PAD_SHORT), :], SHORT_CONV, PAD_SHORT, rc), wa_ref)
            ya_ref[rows, :] = (ya * _rstd(ya) * ga_ref[...]).astype(BF16)
            upad[pl.ds(base + PAD_LONG, rc), :] = cv_ref[rows, :] * _sigmoid(cg_ref[rows, :])
            taps_u = _causal_taps(upad[pl.ds(base, rc + PAD_LONG), :], CONFORMER_CONV, PAD_LONG, rc)
            yc = _mixer_c_fwd(taps_u, wc_ref, cb_ref, lg_ref, lb_ref)[0]
            yc_ref[rows, :] = (yc * _rstd(yc) * gc_ref[...]).astype(BF16)
            return carry

        lax.fori_loop(0, s // rc, chunk, 0)

    def zblk(j):
        return pl.BlockSpec((s, w), lambda i: (0, j))

    def whole(a):
        return pl.BlockSpec(a.shape, lambda i: (0, 0))

    return _pcall(
        body, name=name, out_shape=(_sds((s, w), BF16), _sds((s, w), BF16)), grid=(1,),
        in_specs=[zblk(0), zblk(1), zblk(2), zblk(nblk - 2), zblk(nblk - 1)] + [whole(a) for a in (wa, wc, cb, lg, lb, ga, gc)],
        out_specs=(pl.BlockSpec((s, w), lambda i: (0, 0)), pl.BlockSpec((s, w), lambda i: (0, 0))),
        scratch_shapes=[pltpu.VMEM((s + PAD_SHORT, w), F32), pltpu.VMEM((s + PAD_LONG, w), F32)],
    )(z, z, z, z, z, wa, wc, cb, lg, lb, ga, gc)


def _mix_bwd(z, dy, wa, wc, cb, lg, lb, ga, gc, name):
    s = z.shape[0]
    w = wa.shape[1]
    nblk = z.shape[1] // w
    nyb = dy.shape[1] // w
    rc = ROW_CHUNK

    def body(ah_ref, ab_ref, ac_ref, cv_ref, cg_ref, dya_ref, dyc_ref,
             wa_ref, wc_ref, cb_ref, lg_ref, lb_ref, ga_ref, gc_ref,
             dza_ref, dzc_ref, dwa_ref, dwc_ref, dcb_ref, dlg_ref, dlb_ref, dga_ref, dgc_ref,
             tpad, upad, dctp, dup):
        tpad[pl.ds(0, PAD_SHORT), :] = jnp.zeros((PAD_SHORT, w), F32)
        upad[pl.ds(0, PAD_LONG), :] = jnp.zeros((PAD_LONG, w), F32)
        dctp[pl.ds(s, PAD_SHORT), :] = jnp.zeros((PAD_SHORT, w), F32)
        dup[pl.ds(s, PAD_LONG), :] = jnp.zeros((PAD_LONG, w), F32)
        for ref in (dwa_ref, dwc_ref, dcb_ref, dlg_ref, dlb_ref, dga_ref, dgc_ref):
            ref[...] = jnp.zeros(ref.shape, F32)

        def rms_bwd(y, g_ref, dyn, dg_ref):
            r = _rstd(y)
            yh = y * r
            gd = dyn * g_ref[...]
            dg_ref[...] += jnp.sum(dyn * yh, axis=0, keepdims=True)
            return r * (gd - yh * jnp.mean(gd * yh, axis=-1, keepdims=True))

        def first(i, carry):
            base = pl.multiple_of(i * rc, rc)
            rows = pl.ds(base, rc)
            ah, ab, ac = ah_ref[rows, :], ab_ref[rows, :], ac_ref[rows, :]
            tpad[pl.ds(base + PAD_SHORT, rc), :] = ac * ah
            taps_t = _causal_taps(tpad[pl.ds(base, rc + PAD_SHORT), :], SHORT_CONV, PAD_SHORT, rc)
            ya, ct = _mixer_a_fwd(ab, taps_t, wa_ref)
            dya = rms_bwd(ya, ga_ref, dya_ref[rows, :], dga_ref)
            dza_ref[rows, w:2 * w] = (dya * ct).astype(BF16)
            dct = dya * ab
            dctp[rows, :] = dct
            _conv_wgrad(dwa_ref, dct, taps_t)

            upad[pl.ds(base + PAD_LONG, rc), :] = cv_ref[rows, :] * _sigmoid(cg_ref[rows, :])
            taps_u = _causal_taps(upad[pl.ds(base, rc + PAD_LONG), :], CONFORMER_CONV, PAD_LONG, rc)
            yc, ln, sg, uh, rs = _mixer_c_fwd(taps_u, wc_ref, cb_ref, lg_ref, lb_ref)
            dyc = rms_bwd(yc, gc_ref, dyc_ref[rows, :], dgc_ref)
            dln = dyc * (sg * (1.0 + ln * (1.0 - sg)))
            dlg_ref[...] += jnp.sum(dln * uh, axis=0, keepdims=True)
            dlb_ref[...] += jnp.sum(dln, axis=0, keepdims=True)
            duh = dln * lg_ref[...]
            du = rs * (duh - jnp.mean(duh, axis=-1, keepdims=True) - uh * jnp.mean(duh * uh, axis=-1, keepdims=True))
            dcb_ref[...] += jnp.sum(du, axis=0, keepdims=True)
            dup[rows, :] = du
            _conv_wgrad(dwc_ref, du, taps_u)
            return carry

        lax.fori_loop(0, s // rc, first, 0)

        def second(i, carry):
            base = pl.multiple_of(i * rc, rc)
            rows = pl.ds(base, rc)
            dt = _conv_taps(_anticausal_taps(dctp[pl.ds(base, rc + PAD_SHORT), :], SHORT_CONV, rc), wa_ref)
            dza_ref[rows, 0:w] = (dt * ac_ref[rows, :]).astype(BF16)
            dza_ref[rows, 2 * w:3 * w] = (dt * ah_ref[rows, :]).astype(BF16)
            du0 = _conv_taps(_anticausal_taps(dup[pl.ds(base, rc + PAD_LONG), :], CONFORMER_CONV, rc), wc_ref)
            sg = _sigmoid(cg_ref[rows, :])
            dzc_ref[rows, 0:w] = (du0 * sg).astype(BF16)
            dzc_ref[rows, w:2 * w] = (du0 * cv_ref[rows, :] * sg * (1.0 - sg)).astype(BF16)
            return carry

        lax.fori_loop(0, s // rc, second, 0)

    def blk(j):
        return pl.BlockSpec((s, w), lambda i: (0, j))

    def whole(a):
        return pl.BlockSpec(tuple(a.shape), lambda i: (0, 0))

    params = (wa, wc, cb, lg, lb, ga, gc)
    outs = (_sds((s, 3 * w), BF16), _sds((s, 2 * w), BF16)) + tuple(_sds(p.shape, F32) for p in params)
    return _pcall(
        body, name=name, out_shape=outs, grid=(1,),
        in_specs=[blk(0), blk(1), blk(2), blk(nblk - 2), blk(nblk - 1), blk(0), blk(nyb - 1)] + [whole(p) for p in params],
        out_specs=tuple(whole(o) for o in outs),
        scratch_shapes=[pltpu.VMEM((s + PAD_SHORT, w), F32), pltpu.VMEM((s + PAD_LONG, w), F32),
                        pltpu.VMEM((s + PAD_SHORT, w), F32), pltpu.VMEM((s + PAD_LONG, w), F32)],
    )(z, z, z, z, z, dy, dy, *params)


def _ffn_act_fwd(up, wf, name, comm=None):
    s, f2 = up.shape
    f = f2 // 2
    tc = 256
    nb = f // tc
    rc = ROW_CHUNK

    def body(g_ref, v_ref, wg_ref, wv_ref, o_ref, gpad, vpad):
        gpad[pl.ds(0, PAD_SHORT), :] = jnp.zeros((PAD_SHORT, tc), F32)
        vpad[pl.ds(0, PAD_SHORT), :] = jnp.zeros((PAD_SHORT, tc), F32)

        def chunk(i, carry):
            base = pl.multiple_of(i * rc, rc)
            rows = pl.ds(base, rc)
            gpad[pl.ds(base + PAD_SHORT, rc), :] = g_ref[rows, :].astype(F32)
            vpad[pl.ds(base + PAD_SHORT, rc), :] = v_ref[rows, :].astype(F32)
            gc = _conv_taps(_causal_taps(gpad[pl.ds(base, rc + PAD_SHORT), :], FFN_CONV, PAD_SHORT, rc), wg_ref)
            vc = _conv_taps(_causal_taps(vpad[pl.ds(base, rc + PAD_SHORT), :], FFN_CONV, PAD_SHORT, rc), wv_ref)
            o_ref[rows, :] = (gc * _sigmoid(gc) * vc).astype(BF16)
            return carry

        lax.fori_loop(0, s // rc, chunk, 0)

    return _pcall(
        body, name=name, out_shape=_sds((s, f), BF16), grid=(nb,),
        in_specs=[pl.BlockSpec((s, tc), lambda j: (0, j)), pl.BlockSpec((s, tc), lambda j: (0, j + nb)),
                  pl.BlockSpec((FFN_CONV, tc), lambda j: (0, j)), pl.BlockSpec((FFN_CONV, tc), lambda j: (0, j + nb))],
        out_specs=pl.BlockSpec((s, tc), lambda j: (0, j)),
        scratch_shapes=[pltpu.VMEM((s + PAD_SHORT, tc), F32), pltpu.VMEM((s + PAD_SHORT, tc), F32)], comm=comm,
    )(up, up, wf, wf)


def _ffn_act_bwd(up, dact, wf, name, comm=None):
    s, f2 = up.shape
    f = f2 // 2
    tc = 256
    nb = f // tc
    rc = ROW_CHUNK

    def body(g_ref, v_ref, da_ref, wg_ref, wv_ref, act_ref, dg_ref, dv_ref, dwg_ref, dwv_ref, gpad, vpad, dgp, dvp):
        gpad[pl.ds(0, PAD_SHORT), :] = jnp.zeros((PAD_SHORT, tc), F32)
        vpad[pl.ds(0, PAD_SHORT), :] = jnp.zeros((PAD_SHORT, tc), F32)
        dgp[pl.ds(s, PAD_SHORT), :] = jnp.zeros((PAD_SHORT, tc), F32)
        dvp[pl.ds(s, PAD_SHORT), :] = jnp.zeros((PAD_SHORT, tc), F32)
        dwg_ref[...] = jnp.zeros((FFN_CONV, tc), F32)
        dwv_ref[...] = jnp.zeros((FFN_CONV, tc), F32)

        def first(i, carry):
            base = pl.multiple_of(i * rc, rc)
            rows = pl.ds(base, rc)
            gpad[pl.ds(base + PAD_SHORT, rc), :] = g_ref[rows, :].astype(F32)
            vpad[pl.ds(base + PAD_SHORT, rc), :] = v_ref[rows, :].astype(F32)
            taps_g = _causal_taps(gpad[pl.ds(base, rc + PAD_SHORT), :], FFN_CONV, PAD_SHORT, rc)
            taps_v = _causal_taps(vpad[pl.ds(base, rc + PAD_SHORT), :], FFN_CONV, PAD_SHORT, rc)
            gc = _conv_taps(taps_g, wg_ref)
            vc = _conv_taps(taps_v, wv_ref)
            sg = _sigmoid(gc)
            silu = gc * sg
            act_ref[rows, :] = (silu * vc).astype(BF16)
            da = da_ref[rows, :].astype(F32)
            dgc = da * vc * (sg * (1.0 + gc * (1.0 - sg)))
            dvc = da * silu
            dgp[rows, :] = dgc
            dvp[rows, :] = dvc
            _conv_wgrad(dwg_ref, dgc, taps_g)
            _conv_wgrad(dwv_ref, dvc, taps_v)
            return carry

        lax.fori_loop(0, s // rc, first, 0)

        def second(i, carry):
            base = pl.multiple_of(i * rc, rc)
            rows = pl.ds(base, rc)
            dg_ref[rows, :] = _conv_taps(_anticausal_taps(dgp[pl.ds(base, rc + PAD_SHORT), :], FFN_CONV, rc), wg_ref).astype(BF16)
            dv_ref[rows, :] = _conv_taps(_anticausal_taps(dvp[pl.ds(base, rc + PAD_SHORT), :], FFN_CONV, rc), wv_ref).astype(BF16)
            return carry

        lax.fori_loop(0, s // rc, second, 0)

    lo = pl.BlockSpec((s, tc), lambda j: (0, j))
    hi = pl.BlockSpec((s, tc), lambda j: (0, j + nb))
    wlo = pl.BlockSpec((FFN_CONV, tc), lambda j: (0, j))
    whi = pl.BlockSpec((FFN_CONV, tc), lambda j: (0, j + nb))
    act, dgate, dval, dwg, dwv = _pcall(
        body, name=name,
        out_shape=(_sds((s, f), BF16), _sds((s, f), BF16), _sds((s, f), BF16), _sds((FFN_CONV, f), F32), _sds((FFN_CONV, f), F32)),
        grid=(nb,), in_specs=[lo, hi, lo, wlo, whi], out_specs=(lo, lo, lo, wlo, wlo),
        scratch_shapes=[pltpu.VMEM((s + PAD_SHORT, tc), F32) for _ in range(4)], comm=comm,
    )(up, up, dact, wf, wf)
    return act, dgate, dval, jnp.concatenate([dwg, dwv], axis=1)


def _y_assemble(yan, yb, ycn, gb, name):
    s, w = yan.shape
    wb = yb.shape[1]
    tm = ROW_CHUNK

    def body(ya_ref, yb_ref, yc_ref, g_ref, o_ref):
        ybv = yb_ref[...]
        o_ref[:, 0:w] = ya_ref[...]
        o_ref[:, w:w + wb] = (ybv * _rstd(ybv) * g_ref[...]).astype(BF16)
        o_ref[:, w + wb:] = yc_ref[...]

    return _pcall(body, name=name, out_shape=_sds((s, 2 * w + wb), BF16), grid=(s // tm,),
                  in_specs=[pl.BlockSpec((tm, w), lambda i: (i, 0)), pl.BlockSpec((tm, wb), lambda i: (i, 0)),
                            pl.BlockSpec((tm, w), lambda i: (i, 0)), pl.BlockSpec((1, wb), lambda i: (0, 0))],
                  out_specs=pl.BlockSpec((tm, 2 * w + wb), lambda i: (i, 0)))(yan, yb, ycn, gb)


def _yb_norm_bwd(yb, dy, gb, name):
    s, wb = yb.shape
    w = wb // 2
    heads = wb // D_HEAD
    tm = ROW_CHUNK

    def body(yb_ref, d1_ref, d2_ref, g_ref, dyb_ref, dl_ref, dg_ref):
        i = pl.program_id(0)
        y = yb_ref[...]
        dyn = jnp.concatenate([d1_ref[...], d2_ref[...]], axis=1)
        r = _rstd(y)
        yh = y * r
        gd = dyn * g_ref[...]
        dyb = r * (gd - yh * jnp.mean(gd * yh, axis=-1, keepdims=True))
        dyb_ref[...] = dyb
        part = jnp.sum(dyn * yh, axis=0, keepdims=True)
        prod = dyb * y
        even = lax.broadcasted_iota(I32, (tm, LANES), 1) < D_HEAD
        for p in range(heads // 2):
            blk = prod[:, p * LANES:(p + 1) * LANES]
            ev = jnp.sum(jnp.where(even, blk, 0.0), axis=1, keepdims=True)
            od = jnp.sum(jnp.where(even, 0.0, blk), axis=1, keepdims=True)
            dl_ref[2 * p] = jnp.broadcast_to(ev, (tm, LANES))
            dl_ref[2 * p + 1] = jnp.broadcast_to(od, (tm, LANES))

        @pl.when(i == 0)
        def _():
            dg_ref[...] = part

        @pl.when(i > 0)
        def _():
            dg_ref[...] += part

    return _pcall(
        body, name=name, out_shape=(_sds((s, wb), F32), _sds((heads, s, LANES), F32), _sds((1, wb), F32)),
        grid=(s // tm,),
        in_specs=[pl.BlockSpec((tm, wb), lambda i: (i, 0)), pl.BlockSpec((tm, w), lambda i: (i, 1)),
                  pl.BlockSpec((tm, w), lambda i: (i, 2)), pl.BlockSpec((1, wb), lambda i: (0, 0))],
        out_specs=(pl.BlockSpec((tm, wb), lambda i: (i, 0)), pl.BlockSpec((heads, tm, LANES), lambda i: (0, i, 0)),
                   pl.BlockSpec((1, wb), lambda i: (0, 0))),
    )(yb, dy, dy, gb)


def _t5_bucket_table():
    max_exact = NUM_BUCKETS // 2
    out = np.full((len(DILATED_BRANCHES), BLK, 2 * BLK), -1, np.int32)
    rel = np.arange(BLK)[:, None] - np.arange(2 * BLK)[None, :] + BLK
    for b, (window, dilation) in enumerate(DILATED_BRANCHES):
        n_keys = window // dilation
        dist = np.maximum(rel, 0) * dilation
        d_f = np.maximum(dist, 1).astype(np.float32)
        large = max_exact + (np.log(d_f / np.float32(max_exact)) / np.float32(math.log(MAX_DISTANCE / max_exact))
                             * np.float32(NUM_BUCKETS - max_exact)).astype(np.int32)
        large = np.minimum(large, NUM_BUCKETS - 1)
        bucket = np.where(dist < max_exact, dist, large)
        out[b] = np.where((rel >= 0) & (rel <= n_keys), bucket, -1)
    return out


def _bias_tiles(rel_bias, buckets, name):
    nbk, heads = rel_bias.shape
    nbr = buckets.shape[0]

    def body(rb_ref, bk_ref, o_ref):
        for br in range(nbr):
            bk = bk_ref[br]
            tiles = [jnp.full((BLK, 2 * BLK), NEG, F32) for _ in range(heads)]
            for b in range(nbk):
                hit = bk == b
                tiles = [jnp.where(hit, rb_ref[b, h], tiles[h]) for h in range(heads)]
            for h in range(heads):
                o_ref[br, h] = tiles[h]

    return _pcall(body, name=name, out_shape=_sds((nbr, heads, BLK, 2 * BLK), F32), grid=(1,),
                  in_specs=[pl.BlockSpec(memory_space=pltpu.SMEM), pl.BlockSpec(buckets.shape, lambda i: (0, 0, 0))],
                  out_specs=pl.BlockSpec((nbr, heads, BLK, 2 * BLK), lambda i: (0, 0, 0, 0)))(rel_bias, buckets)


def _bias_grad(dtiles, buckets, nbk, name):
    nbr, heads = dtiles.shape[:2]

    def body(dt_ref, bk_ref, o_ref):
        row = lax.broadcasted_iota(I32, (nbk, LANES), 0)
        col = lax.broadcasted_iota(I32, (nbk, LANES), 1)
        out = jnp.zeros((nbk, LANES), F32)
        for h in range(heads):
            for b in range(nbk):
                tot = jnp.zeros((), F32)
                for br in range(nbr):
                    tot = tot + jnp.sum(jnp.where(bk_ref[br] == b, dt_ref[br, h], 0.0))
                out = jnp.where((row == b) & (col == h), tot, out)
        o_ref[...] = out

    return _pcall(body, name=name, out_shape=_sds((nbk, LANES), F32), grid=(1,),
                  in_specs=[pl.BlockSpec(dtiles.shape, lambda i: (0, 0, 0, 0)), pl.BlockSpec(buckets.shape, lambda i: (0, 0, 0))],
                  out_specs=pl.BlockSpec((nbk, LANES), lambda i: (0, 0)))(dtiles, buckets)


def _largest_divisor(n, cap):
    return max(g for g in range(1, cap + 1) if n % g == 0)


def _attn_blocks(s, visit, group):
    for br, (window, d) in enumerate(DILATED_BRANCHES):
        n_blk = (s // d) // BLK
        span = BLK * d
        g1 = _largest_divisor(d, group)

        def firsts(t, carry, br=br, d=d, g1=g1):
            for j in range(g1):
                visit(br, d, t * g1 + j, False)
            return carry

        lax.fori_loop(0, d // g1, firsts, 0)
        if n_blk > 1:
            total = d * (n_blk - 1)
            g2 = _largest_divisor(total, group)

            def rest(t, carry, br=br, d=d, n_blk=n_blk, span=span, g2=g2):
                for j in range(g2):
                    idx = t * g2 + j
                    visit(br, d, idx // (n_blk - 1) + (1 + idx % (n_blk - 1)) * span, True)
                return carry

            lax.fori_loop(0, total // g2, rest, 0)


def _rows(start, size, d):
    return pl.ds(pl.multiple_of(start, BLK), size) if d == 1 else pl.ds(start, size, stride=d)


def _attn_fwd(z, btiles, col0, name, comm=None):
    s = z.shape[0]
    nbr, heads = btiles.shape[:2]
    pairs = heads // 2
    scale = D_HEAD ** -0.5
    rc = ROW_CHUNK

    def body(q_ref, k_ref, v_ref, bt_ref, yb_ref, lse_ref, acc_ref, m_ref, l_ref):
        even = lax.broadcasted_iota(I32, (BLK, LANES), 1) < D_HEAD
        even2 = lax.broadcasted_iota(I32, (2 * BLK, LANES), 1) < D_HEAD

        def visit(br, d, start, prev):
            kw = 2 * BLK if prev else BLK
            rows_q = _rows(start, BLK, d)
            rows_k = _rows(start - BLK * d, kw, d) if prev else rows_q
            qb = q_ref[rows_q, :]
            kb = k_ref[rows_k, :].astype(BF16)
            vw = v_ref[rows_k, :]
            ev_k = even2 if prev else even
            qm = jnp.concatenate([jnp.where(even, qb, 0.0), jnp.where(even, 0.0, qb)], axis=0).astype(BF16)
            bias = [bt_ref[br, e] if prev else bt_ref[br, e, :, BLK:] for e in range(2)]
            sc = _dot_nt(qm, kb) * scale + jnp.concatenate(bias, axis=0)
            m = jnp.max(sc, axis=1, keepdims=True)
            p = jnp.exp(sc - m)
            l = jnp.sum(p, axis=1, keepdims=True)
            pb = p.astype(BF16)
            vm = jnp.concatenate([jnp.where(ev_k, vw, 0.0), jnp.where(ev_k, 0.0, vw)], axis=0).astype(BF16)
            acc_ref.at[br][rows_q, :] = _dot(jnp.concatenate([pb[:BLK], pb[BLK:]], axis=1), vm)
            for e in range(2):
                m_ref.at[br, e][rows_q, :] = jnp.broadcast_to(m[e * BLK:(e + 1) * BLK], (BLK, LANES))
                l_ref.at[br, e][rows_q, :] = jnp.broadcast_to(l[e * BLK:(e + 1) * BLK], (BLK, LANES))

        _attn_blocks(s, visit, ATTN_GROUP_FWD)

        ev_c = lax.broadcasted_iota(I32, (rc, LANES), 1) < D_HEAD

        def merge(i, carry):
            rows = pl.ds(pl.multiple_of(i * rc, rc), rc)
            wts, dens = [], []
            for e in range(2):
                ms = [m_ref[br, e, rows, :] for br in range(nbr)]
                top = functools.reduce(jnp.maximum, ms)
                w = [jnp.exp(mb - top) for mb in ms]
                den = functools.reduce(lambda a, b: a + b, [w[br] * l_ref[br, e, rows, :] for br in range(nbr)])
                lse_ref[e, rows, :] = top + jnp.log(den)
                wts.append(w)
                dens.append(den)
            num = functools.reduce(lambda a, b: a + b,
                                   [jnp.where(ev_c, wts[0][br], wts[1][br]) * acc_ref[br, rows, :] for br in range(nbr)])
            yb_ref[rows, :] = num / jnp.where(ev_c, dens[0], dens[1])
            return carry

        lax.fori_loop(0, s // rc, merge, 0)

    def zcol(j):
        return pl.BlockSpec((s, LANES), lambda p, j=j: (0, col0 + j + p))

    return _pcall(
        body, name=name, out_shape=(_sds((s, pairs * LANES), F32), _sds((heads, s, LANES), F32)), grid=(pairs,),
        in_specs=[zcol(0), zcol(pairs), zcol(2 * pairs), pl.BlockSpec((nbr, 2, BLK, 2 * BLK), lambda p: (0, p, 0, 0))],
        out_specs=(pl.BlockSpec((s, LANES), lambda p: (0, p)), pl.BlockSpec((2, s, LANES), lambda p: (p, 0, 0))),
        scratch_shapes=[pltpu.VMEM((nbr, s, LANES), F32), pltpu.VMEM((nbr, 2, s, LANES), F32), pltpu.VMEM((nbr, 2, s, LANES), F32)],
        comm=comm,
    )(z, z, z, btiles)


def _attn_bwd(z, btiles, dyb, lse, delta, dbias_in, col0, name, comm=None):
    s = z.shape[0]
    nbr, heads = btiles.shape[:2]
    pairs = heads // 2
    scale = D_HEAD ** -0.5

    def body(q_ref, k_ref, v_ref, bt_ref, dy_ref, lse_ref, dl_ref, dbi_ref,
             dq_ref, dk_ref, dv_ref, db_ref, dqa, dka, dva):
        even = lax.broadcasted_iota(I32, (BLK, LANES), 1) < D_HEAD
        even2 = lax.broadcasted_iota(I32, (2 * BLK, LANES), 1) < D_HEAD
        for ref in (dqa, dka, dva):
            ref[...] = jnp.zeros((s, LANES), F32)
        db_ref[...] = dbi_ref[...]

        def visit(br, d, start, prev):
            kw = 2 * BLK if prev else BLK
            rows_q = _rows(start, BLK, d)
            rows_k = _rows(start - BLK * d, kw, d) if prev else rows_q
            qb = q_ref[rows_q, :]
            dyv = dy_ref[rows_q, :]
            kwin = k_ref[rows_k, :]
            kb = kwin.astype(BF16)
            vb = v_ref[rows_k, :].astype(BF16)
            ev_k = even2 if prev else even
            qm = jnp.concatenate([jnp.where(even, qb, 0.0), jnp.where(even, 0.0, qb)], axis=0).astype(BF16)
            dym = jnp.concatenate([jnp.where(even, dyv, 0.0), jnp.where(even, 0.0, dyv)], axis=0).astype(BF16)
            bias = [bt_ref[br, e] if prev else bt_ref[br, e, :, BLK:] for e in range(2)]
            sc = _dot_nt(qm, kb) * scale + jnp.concatenate(bias, axis=0)
            lt = jnp.concatenate([lse_ref.at[e][rows_q, :] for e in range(2)], axis=0)
            dt = jnp.concatenate([dl_ref.at[e][rows_q, :] for e in range(2)], axis=0)
            if prev:
                lt = jnp.concatenate([lt, lt], axis=1)
                dt = jnp.concatenate([dt, dt], axis=1)
            p = jnp.exp(sc - lt)
            ds = p * (_dot_nt(dym, vb) - dt)
            for e in range(2):
                if prev:
                    db_ref[br, e] += ds[e * BLK:(e + 1) * BLK]
                else:
                    db_ref[br, e, :, BLK:] += ds[e * BLK:(e + 1) * BLK]
            dsb = ds.astype(BF16)
            km = jnp.concatenate([jnp.where(ev_k, kwin, 0.0), jnp.where(ev_k, 0.0, kwin)], axis=0).astype(BF16)
            dqa[rows_q, :] += _dot(jnp.concatenate([dsb[:BLK], dsb[BLK:]], axis=1), km) * scale
            dka[rows_k, :] += _dot_tn(dsb, qm) * scale
            dva[rows_k, :] += _dot_tn(p.astype(BF16), dym)

        _attn_blocks(s, visit, ATTN_GROUP_BWD)
        dq_ref[...] = dqa[...].astype(BF16)
        dk_ref[...] = dka[...].astype(BF16)
        dv_ref[...] = dva[...].astype(BF16)

    def zcol(j):
        return pl.BlockSpec((s, LANES), lambda p, j=j: (0, col0 + j + p))

    col = pl.BlockSpec((s, LANES), lambda p: (0, p))
    stat = pl.BlockSpec((2, s, LANES), lambda p: (p, 0, 0))
    tile = pl.BlockSpec((nbr, 2, BLK, 2 * BLK), lambda p: (0, p, 0, 0))
    wide = _sds((s, pairs * LANES), BF16)
    return _pcall(
        body, name=name, out_shape=(wide, wide, wide, _sds(btiles.shape, F32)), grid=(pairs,),
        in_specs=[zcol(0), zcol(pairs), zcol(2 * pairs), tile, col, stat, stat, tile],
        out_specs=(col, col, col, tile),
        scratch_shapes=[pltpu.VMEM((s, LANES), F32) for _ in range(3)], comm=comm,
    )(z, z, z, btiles, dyb, lse, delta, dbias_in)


def _row(v):
    return v.reshape(1, -1)


class _LocalSchedule:
    def __init__(self):
        self.big = {}

    def fwd_comms(self, l):
        return {}

    def bwd_comms(self, l):
        return {}

    def after_bwd(self, l, grads):
        self.big[l] = grads


def _layer_fwd(l, x, wts, prm, btiles, comms):
    d = x.shape[1]
    wq = d // 4
    gout = prm["out_norm_g"][l]
    h = _rms_fwd(x, _row(prm["norm_mix_g"][l]), "rms_mix_fwd", comm=comms.get("rms_mix_fwd"))
    z = _mm_n(h, wts["in_t"], l, nt=True, tn=256, out_dtype=F32, name="in_proj")
    yan, ycn = _mix_fwd(z, prm["conv_a_w"][l], prm["conv_c_w"][l], _row(prm["conv_c_b"][l]), _row(prm["ln_c_g"][l]),
                        _row(prm["ln_c_b"][l]), _row(gout[:wq]), _row(gout[3 * wq:]), "mix_fwd")
    yb, lse = _attn_fwd(z, btiles, 3 * wq // LANES, "attn_fwd", comm=comms.get("attn_fwd"))
    y = _y_assemble(yan, yb, ycn, _row(gout[wq:3 * wq]), "y_assemble")
    x_mid = _mm_n(y, wts["out"], l, nt=False, tn=256, out_dtype=F32, name="out_proj", resid=x)
    h2 = _rms_fwd(x_mid, _row(prm["norm_ffn_g"][l]), "rms_ffn_fwd")
    up = _mm_n(h2, wts["up"], l, nt=False, tn=512, out_dtype=BF16, name="up_proj", comm=comms.get("up_proj"))
    act = _ffn_act_fwd(up, prm["conv_f_w"][l], "ffn_act_fwd", comm=comms.get("ffn_act_fwd"))
    x_out = _mm_n(act, wts["down"], l, nt=False, tn=256, out_dtype=F32, name="down_proj", resid=x_mid,
                  comm=comms.get("down_proj"))
    return x_out, (x, h, z, yb, lse, y, x_mid, h2, up)


def _layer_bwd(l, dxo, dxo_b, saved, wts, prm, btiles, dbias, comms):
    x, h, z, yb, lse, y, x_mid, h2, up = saved
    d = x.shape[1]
    wq = d // 4
    f = up.shape[1] // 2
    gout = prm["out_norm_g"][l]
    dact = _mm_n(dxo_b, wts["down"], l, nt=True, tn=256, out_dtype=BF16, name="down_proj_dx", comm=comms.get(SWAP_RIDE))
    act, dgate, dval, dwf = _ffn_act_bwd(up, dact, prm["conv_f_w"][l], "ffn_act_bwd", comm=comms.get("ffn_act_bwd"))
    g_down = _mm_tn(act, dxo_b, t=256, name="down_proj_dw")
    dh2 = _mm_n(dgate, wts["up"], l, nt=True, tn=256, out_dtype=F32, name="up_proj_dx_gate")
    dh2 = _mm_n(dval, wts["up"], l, nt=True, tn=256, out_dtype=F32, name="up_proj_dx_val", resid=dh2, b_part=1)
    dxm, dxm_b, dg_ffn = _rms_bwd(x_mid, _row(prm["norm_ffn_g"][l]), dh2, dxo, "rms_ffn_bwd")
    g_up = _mm_tn2(h2, dgate, dval, t=256, name="up_proj_dw")
    dy = _mm_n(dxm_b, wts["out"], l, nt=True, tn=256, out_dtype=F32, name="out_proj_dx")
    g_out = _mm_tn(y, dxm_b, t=256, name="out_proj_dw")
    dza, dzc, dwa, dwc, dcb, dlg, dlb, dga, dgc = _mix_bwd(
        z, dy, prm["conv_a_w"][l], prm["conv_c_w"][l], _row(prm["conv_c_b"][l]), _row(prm["ln_c_g"][l]),
        _row(prm["ln_c_b"][l]), _row(gout[:wq]), _row(gout[3 * wq:]), "mix_bwd")
    dyb, delta, dgb = _yb_norm_bwd(yb, dy, _row(gout[wq:3 * wq]), "yb_norm_bwd")
    dq, dk, dv, dbias = _attn_bwd(z, btiles, dyb, lse, delta, dbias, 3 * wq // LANES, "attn_bwd",
                                  comm=comms.get("attn_bwd"))
    dz = jnp.concatenate([dza, dq, dk, dv, dzc], axis=1)
    dh = _mm_n(dz, wts["in_t"], l, nt=False, tn=256, out_dtype=F32, name="in_proj_dx")
    dx, dx_b, dg_mix = _rms_bwd(x, _row(prm["norm_mix_g"][l]), dh, dxm, "rms_mix_bwd")
    g_in_t = _mm_tn(dz, h, t=256, name="in_proj_dw")
    big = {"in_t": g_in_t, "out": g_out, "up": g_up, "down": g_down}
    small = {"norm_mix_g": dg_mix[0], "conv_a_w": dwa, "conv_c_w": dwc, "conv_c_b": dcb[0], "ln_c_g": dlg[0],
             "ln_c_b": dlb[0], "out_norm_g": jnp.concatenate([dga[0], dgb[0], dgc[0]]), "norm_ffn_g": dg_ffn[0],
             "conv_f_w": dwf}
    return dx, dx_b, big, small, dbias


def _local_step(x, tgt, wts, prm, sched):
    depth = prm["norm_mix_g"].shape[0]
    buckets = jnp.asarray(_t5_bucket_table())
    btiles = _bias_tiles(prm["rel_bias"], buckets, "bias_tiles")
    saved = []
    for l in range(depth):
        x, sv = _layer_fwd(l, x, wts, prm, btiles, sched.fwd_comms(l))
        saved.append(sv)
    loss, dx, dx_b, dg_final = _final_loss(x, _row(prm["final_g"]), tgt, "final_loss")
    dbias = jnp.zeros(btiles.shape, F32)
    small = [None] * depth
    for l in reversed(range(depth)):
        dx, dx_b, grads, small[l], dbias = _layer_bwd(l, dx, dx_b, saved[l], wts, prm, btiles, dbias, sched.bwd_comms(l))
        sched.after_bwd(l, grads)
    nbk, heads = prm["rel_bias"].shape
    d_rel = _bias_grad(dbias, buckets, nbk, "bias_grad")[:, :heads]
    return loss, dx, small, d_rel, dg_final[0]


BIG = ("in_t", "out", "up", "down")
COL_SHARDED = ("up",)
N_CHIPS = 4
N_DEV = 8
BF16_ROWS = 16


def _me():
    return lax.axis_index("x"), lax.axis_index("y"), lax.axis_index("c")


def _chip_of(x, y):
    return 2 * x + y


def _other_chips(x, y):
    return ((1 - x, y), (x, 1 - y), (1 - x, 1 - y))


def _remote(src, dst, send_sem, recv_sem, device):
    return pltpu.make_async_remote_copy(src_ref=src, dst_ref=dst, send_sem=send_sem, recv_sem=recv_sem,
                                        device_id=device, device_id_type=MESH)


def _ag_comm(wts, layer, ici_keys, fwd_keys):
    keys = tuple(k for k in BIG if k in ici_keys or k in fwd_keys)

    def geo(k):
        _, rows, cols = wts[k].shape
        return (rows, cols // N_CHIPS) if k in COL_SHARDED else (rows // N_CHIPS, cols)

    def copies(refs, sems):
        g = dict(zip(keys, refs))
        isend, irecv, dsend, drecv = sems
        x, y, c = _me()
        mine = _chip_of(x, y)

        def region(k, chip, half):
            r, cc = geo(k)
            h = r // 2
            if k in COL_SHARDED:
                return g[k].at[layer, pl.ds(pl.multiple_of(half * h, BF16_ROWS), h), pl.ds(pl.multiple_of(chip * cc, LANES), cc)]
            return g[k].at[layer, pl.ds(pl.multiple_of(chip * r + half * h, BF16_ROWS), h), :]

        def ici(k, f, landing):
            chip = _other_chips(x, y)[f]
            where = region(k, _chip_of(*chip) if landing else mine, c)
            i = keys.index(k)
            return _remote(where, where, isend.at[i, f], irecv.at[i, f], (*chip, c))

        def fwd(k, f, landing):
            chip = _other_chips(x, y)[f]
            where = region(k, _chip_of(*chip), 1 - c if landing else c)
            i = keys.index(k)
            return _remote(where, where, dsend.at[i, f], drecv.at[i, f], (x, y, 1 - c))

        return ici, fwd

    def start(ins, outs, sems):
        ici, fwd = copies(outs, sems)
        for k in keys:
            for f in range(3):
                if k in ici_keys:
                    ici(k, f, False).start()
                else:
                    fwd(k, f, False).start()

    def finish(ins, outs, sems):
        ici, fwd = copies(outs, sems)
        for k in keys:
            for f in range(3):
                if k in ici_keys:
                    ici(k, f, True).wait_recv()
                    if k in fwd_keys:
                        fwd(k, f, False).start()
        for k in keys:
            for f in range(3):
                if k in fwd_keys:
                    fwd(k, f, True).wait_recv()
                    fwd(k, f, False).wait_send()
                if k in ici_keys:
                    ici(k, f, False).wait_send()

    def done(res):
        wts.update(zip(keys, res))

    n = len(keys)
    return _Comm([wts[k] for k in keys], [_sds(wts[k].shape, BF16) for k in keys], {i: i for i in range(n)},
                 [pltpu.SemaphoreType.DMA((n, 3)) for _ in range(4)], start, finish, done)


def _small_gather_comm(slab, store):
    def copies(ins, outs, sems):
        send, recv, lsem = sems
        x, y, c = _me()
        mine = _chip_of(x, y)
        own = pltpu.make_async_copy(ins[0], outs[0].at[mine], lsem)
        pairs = []
        for f, chip in enumerate(_other_chips(x, y)):
            out = _remote(ins[0], outs[0].at[mine], send.at[f], recv.at[f], (*chip, c))
            land = _remote(ins[0], outs[0].at[_chip_of(*chip)], send.at[f], recv.at[f], (*chip, c))
            pairs.append((out, land))
        return own, pairs

    def start(ins, outs, sems):
        own, pairs = copies(ins, outs, sems)
        own.start()
        for out, _ in pairs:
            out.start()

    def finish(ins, outs, sems):
        own, pairs = copies(ins, outs, sems)
        for out, land in pairs:
            land.wait_recv()
            out.wait_send()
        own.wait()

    def done(res):
        store["small"] = res[0]

    return _Comm([slab], [_sds((N_CHIPS,) + slab.shape, F32)], {},
                 [pltpu.SemaphoreType.DMA((3,)), pltpu.SemaphoreType.DMA((3,)), pltpu.SemaphoreType.DMA], start, finish, done)


def _piece_geo(g):
    geo = {}
    for k in BIG:
        rows, cols = g[k].shape
        geo[k] = (rows // 2, cols // N_CHIPS) if k in COL_SHARDED else (rows // (2 * N_CHIPS), cols)
    return geo


def _swap_comm(g, done):
    geo = _piece_geo(g)
    n_copies = sum(N_CHIPS if k in COL_SHARDED else 1 for k in BIG)

    def copies(ins, outs, sems):
        g_refs, t_refs = dict(zip(BIG, ins)), dict(zip(BIG, outs))
        send, recv = sems
        x, y, c = _me()
        pairs = []
        for k in BIG:
            h, cc = geo[k]
            if k in COL_SHARDED:
                rows = pl.ds(pl.multiple_of((1 - c) * h, BF16_ROWS), h)
                pairs += [(g_refs[k].at[rows, pl.ds(j * cc, cc)], t_refs[k].at[j]) for j in range(N_CHIPS)]
            else:
                pairs.append((g_refs[k].at[:, 1 - c], t_refs[k]))
        return [_remote(src, dst, send.at[i], recv.at[i], (x, y, 1 - c)) for i, (src, dst) in enumerate(pairs)]

    def start(ins, outs, sems):
        for cp in copies(ins, outs, sems):
            cp.start()

    def finish(ins, outs, sems):
        for cp in copies(ins, outs, sems):
            cp.wait()

    ins = [g[k] if k in COL_SHARDED else g[k].reshape(N_CHIPS, 2, geo[k][0], geo[k][1]) for k in BIG]
    return _Comm(ins, [_sds((N_CHIPS,) + geo[k], BF16) for k in BIG], {},
                 [pltpu.SemaphoreType.DMA((n_copies,)) for _ in range(2)], start, finish,
                 lambda res: done(dict(zip(BIG, res))))


def _pair_sum(g, theirs, c_arr):
    geo = _piece_geo(g)

    def body(c_ref, *refs):
        nk = len(BIG)
        for i in range(nk):
            refs[2 * nk + i][...] = (refs[i][...].astype(F32) + refs[nk + i][...].astype(F32)).astype(BF16)

    in_specs, ins = [], []
    for k in BIG:
        h, cc = geo[k]
        if k in COL_SHARDED:
            in_specs.append(pl.BlockSpec((h, cc), lambda j, c_ref: (c_ref[0], j)))
            ins.append(g[k])
        else:
            in_specs.append(pl.BlockSpec((None, h, cc), lambda j, c_ref: (2 * j + c_ref[0], 0, 0)))
            ins.append(g[k].reshape(2 * N_CHIPS, h, cc))
    slab = [pl.BlockSpec((None,) + geo[k], lambda j, c_ref: (j, 0, 0)) for k in BIG]
    res = _pcall(body, name="rs_pair_sum", out_shape=tuple(_sds((N_CHIPS,) + geo[k], BF16) for k in BIG), grid=(N_CHIPS,),
                 in_specs=in_specs + slab, out_specs=tuple(slab), prefetch=1)(c_arr, *ins, *[theirs[k] for k in BIG])
    return dict(zip(BIG, res))


def _rs_comm(p, keys, store):
    def copies(ins, outs, sems):
        send, recv = sems
        x, y, c = _me()
        return [_remote(ins[i].at[_chip_of(*chip)], outs[i].at[f], send.at[i, f], recv.at[i, f], (*chip, c))
                for i in range(len(keys)) for f, chip in enumerate(_other_chips(x, y))]

    def start(ins, outs, sems):
        for cp in copies(ins, outs, sems):
            cp.start()

    def finish(ins, outs, sems):
        for cp in copies(ins, outs, sems):
            cp.wait()

    def done(res):
        store.update(zip(keys, res))

    return _Comm([p[k] for k in keys], [_sds((3,) + p[k].shape[1:], BF16) for k in keys], {},
                 [pltpu.SemaphoreType.DMA((len(keys), 3)) for _ in range(2)], start, finish, done)


def _quad_sum(p, b, where, l, full):
    parts = 2
    nk = len(BIG)

    def body(where_ref, *refs):
        for i in range(nk):
            acc = refs[i][...].astype(F32)
            for f in range(3):
                acc = acc + refs[nk + 3 * i + f][...].astype(F32)
            refs[5 * nk + i][...] = acc

    own, recv, outs = [], [], []
    for k in BIG:
        h, cc = p[k].shape[1:]
        th = h // parts
        own.append(pl.BlockSpec((None, th, cc), lambda i, w_ref: (w_ref[0], i, 0)))
        recv += [pl.BlockSpec((None, th, cc), lambda i, w_ref, f=f: (f, i, 0)) for f in range(3)]
        outs.append(pl.BlockSpec((None, None, th, cc), lambda i, w_ref: (l, w_ref[1], i, 0)))
    args = [p[k] for k in BIG] + [b[k] for k in BIG for _ in range(3)] + [full[k] for k in BIG]
    res = _pcall(body, name="rs_quad_sum", out_shape=tuple(_sds(full[k].shape, F32) for k in BIG), grid=(parts,),
                 in_specs=own + recv + [ANY] * nk, out_specs=tuple(outs), prefetch=1,
                 aliases={1 + 4 * nk + i: i for i in range(nk)})(where, *args)
    return dict(zip(BIG, res))


def _rs_share(l, full):
    nk = len(BIG)

    def body(*refs):
        f_refs = refs[nk:2 * nk]
        send, recv = refs[2 * nk:]
        x, y, c = _me()
        started = []
        for i in range(nk):
            cp = _remote(f_refs[i].at[l, c], f_refs[i].at[l, c], send.at[i], recv.at[i], (x, y, 1 - c))
            cp.start()
            started.append(cp)
        for i, cp in enumerate(started):
            _remote(f_refs[i].at[l, 1 - c], f_refs[i].at[l, 1 - c], send.at[i], recv.at[i], (x, y, 1 - c)).wait_recv()
            cp.wait_send()

    res = _pcall(body, name="rs_share", out_shape=tuple(_sds(full[k].shape, F32) for k in BIG),
                 in_specs=[ANY] * nk, out_specs=tuple([ANY] * nk), aliases={i: i for i in range(nk)},
                 scratch_shapes=[pltpu.SemaphoreType.DMA((nk,)) for _ in range(2)])(*[full[k] for k in BIG])
    return dict(zip(BIG, res))


def _gather_partials(slab):
    def body(s_ref, o_ref, send, recv):
        x, y, c = _me()
        me = 4 * x + 2 * y + c
        started = []
        peers = []
        for mask in range(1, N_DEV):
            peer = (x ^ (mask >> 2), y ^ ((mask >> 1) & 1), c ^ (mask & 1))
            peers.append(peer)
            cp = _remote(s_ref, o_ref.at[me], send.at[mask - 1], recv.at[mask - 1], peer)
            cp.start()
            started.append(cp)
        for i, peer in enumerate(peers):
            _remote(s_ref, o_ref.at[4 * peer[0] + 2 * peer[1] + peer[2]], send.at[i], recv.at[i], peer).wait_recv()
        for cp in started:
            cp.wait_send()

    return _pcall(body, name="gather_partials", out_shape=_sds((N_DEV,) + slab.shape, F32), in_specs=[ANY], out_specs=ANY,
                  scratch_shapes=[pltpu.SemaphoreType.DMA((N_DEV - 1,)), pltpu.SemaphoreType.DMA((N_DEV - 1,))])(slab)


def _sum_slabs(slabs, own, me):
    n, r, lanes = slabs.shape
    tr = r // 2

    def body(me_ref, s_ref, own_ref, o_ref):
        o_ref[...] = jnp.zeros((tr, lanes), F32)
        for i in range(n):
            @pl.when(me_ref[0] == i)
            def _():
                o_ref[...] += own_ref[...]

            @pl.when(me_ref[0] != i)
            def _():
                o_ref[...] += s_ref[i]

    return _pcall(body, name="sum_partials", out_shape=_sds((r, lanes), F32), grid=(2,),
                  in_specs=[pl.BlockSpec((n, tr, lanes), lambda i, me_ref: (0, i, 0)),
                            pl.BlockSpec((tr, lanes), lambda i, me_ref: (i, 0))],
                  out_specs=pl.BlockSpec((tr, lanes), lambda i, me_ref: (i, 0)), prefetch=1)(me, slabs, own)


def _cast_into_gathered(w, chip, by_cols, name):
    l, r, c = w.shape

    def body(chip_ref, w_ref, o_ref):
        o_ref[...] = w_ref[...].astype(BF16)

    if by_cols:
        shape, out = (l, r, N_CHIPS * c), pl.BlockSpec((None, r, c), lambda i, chip_ref: (i, 0, chip_ref[0]))
    else:
        shape, out = (l, N_CHIPS * r, c), pl.BlockSpec((None, r, c), lambda i, chip_ref: (i, chip_ref[0], 0))
    return _pcall(body, name=name, out_shape=_sds(shape, BF16), grid=(l,),
                  in_specs=[pl.BlockSpec((None, r, c), lambda i, chip_ref: (i, 0, 0))], out_specs=out, prefetch=1)(chip, w)


def _adamw(w, g, m, v, name, tr):
    r, c = w.shape

    def body(w_ref, g_ref, m_ref, v_ref, d_ref, mo_ref, vo_ref):
        gv = g_ref[...]
        mn = ADAM_B1 * m_ref[...] + (1.0 - ADAM_B1) * gv
        vn = ADAM_B2 * v_ref[...] + (1.0 - ADAM_B2) * (gv * gv)
        m_hat = mn / (1.0 - ADAM_B1 ** ADAM_STEP)
        v_hat = vn / (1.0 - ADAM_B2 ** ADAM_STEP)
        d_ref[...] = -ADAM_LR * (m_hat / (jnp.sqrt(v_hat) + ADAM_EPS) + ADAM_WD * w_ref[...])
        mo_ref[...] = mn
        vo_ref[...] = vn

    blk = pl.BlockSpec((tr, c), lambda i: (i, 0))
    return _pcall(body, name=name, out_shape=tuple(_sds((r, c), F32) for _ in range(3)), grid=(r // tr,),
                  in_specs=[blk] * 4, out_specs=(blk, blk, blk))(w, g, m, v)


AG_RIDES = {"attn_fwd": (1, ("up",), ()), "up_proj": (1, ("in_t",), ()), "ffn_act_fwd": (1, ("down",), ("up",)),
            "down_proj": (1, ("out",), ("in_t",)), "rms_mix_fwd": (0, (), ("out", "down"))}
RS_RIDES = {"ffn_act_bwd": ("up",), "attn_bwd": ("in_t", "out", "down")}
SWAP_RIDE = "down_proj_dx"


class _Rides:
    def __init__(self, table, build):
        self.table, self.build = table, build

    def get(self, name):
        return self.build(self.table[name]) if name in self.table else None


class _MeshSchedule:
    def __init__(self, wts, depth, c_arr, where):
        self.wts, self.depth, self.c_arr, self.where = wts, depth, c_arr, where
        self.grads, self.pairs, self.recv, self.full = None, None, {}, None

    def fwd_comms(self, l):
        table = {name: (l + off, ici, fwd) for name, (off, ici, fwd) in AG_RIDES.items() if 1 <= l + off < self.depth}
        return _Rides(table, lambda ride: _ag_comm(self.wts, *ride))

    def _swapped(self, theirs):
        self.pairs = _pair_sum(self.grads, theirs, self.c_arr)

    def bwd_comms(self, l):
        if self.grads is None:
            return {}
        table = dict(RS_RIDES)
        table[SWAP_RIDE] = None
        return _Rides(table, lambda keys: _swap_comm(self.grads, self._swapped) if keys is None
                      else _rs_comm(self.pairs, keys, self.recv))

    def _finish(self, l):
        self.full = _rs_share(l, _quad_sum(self.pairs, self.recv, self.where, l, self.full))
        self.grads, self.pairs, self.recv = None, None, {}

    def after_bwd(self, l, grads):
        if self.grads is not None:
            self._finish(l + 1)
        if self.full is None:
            geo = _piece_geo(grads)
            self.full = {k: jnp.zeros((self.depth, 2) + geo[k], F32) for k in BIG}
        self.grads = grads
        if l == 0:
            _run_comm(_swap_comm(grads, self._swapped), "rs_swap_halves")
            _run_comm(_rs_comm(self.pairs, BIG, self.recv), "rs_to_owners")
            self._finish(0)


SHARDED_SMALL = ("conv_a_w", "conv_c_w", "conv_f_w")
SMALL = ("norm_mix_g", "conv_a_w", "conv_c_w", "conv_c_b", "ln_c_g", "ln_c_b", "out_norm_g", "norm_ffn_g",
         "conv_f_w", "rel_bias", "final_g")
SLAB_ROWS = 16


def _pack(arrays):
    flat = jnp.concatenate([a.reshape(-1) for a in arrays])
    unit = SLAB_ROWS * LANES
    total = -(-flat.shape[0] // unit) * unit
    return jnp.pad(flat, (0, total - flat.shape[0])).reshape(-1, LANES)


def _unpack(slab, shapes):
    flat = slab.reshape(-1)
    out, off = [], 0
    for shp in shapes:
        size = math.prod(shp)
        out.append(flat[off:off + size].reshape(shp))
        off += size
    return out


def kernel(x, norm_mix_g, w_in, conv_a_w, conv_c_w, conv_c_b, ln_c_g, ln_c_b, out_norm_g, w_out, norm_ffn_g, w_up, conv_f_w, w_down, rel_bias, final_g, loss_target, m_norm_mix_g, m_w_in, m_conv_a_w, m_conv_c_w, m_conv_c_b, m_ln_c_g, m_ln_c_b, m_out_norm_g, m_w_out, m_norm_ffn_g, m_w_up, m_conv_f_w, m_w_down, m_rel_bias, m_final_g, v_norm_mix_g, v_w_in, v_conv_a_w, v_conv_c_w, v_conv_c_b, v_ln_c_g, v_ln_c_b, v_out_norm_g, v_w_out, v_norm_ffn_g, v_w_up, v_conv_f_w, v_w_down, v_rel_bias, v_final_g):
    weights = dict(norm_mix_g=norm_mix_g, w_in=w_in, conv_a_w=conv_a_w, conv_c_w=conv_c_w, conv_c_b=conv_c_b,
                   ln_c_g=ln_c_g, ln_c_b=ln_c_b, out_norm_g=out_norm_g, w_out=w_out, norm_ffn_g=norm_ffn_g, w_up=w_up,
                   conv_f_w=conv_f_w, w_down=w_down, rel_bias=rel_bias, final_g=final_g)
    mom_m = dict(norm_mix_g=m_norm_mix_g, w_in=m_w_in, conv_a_w=m_conv_a_w, conv_c_w=m_conv_c_w, conv_c_b=m_conv_c_b,
                 ln_c_g=m_ln_c_g, ln_c_b=m_ln_c_b, out_norm_g=m_out_norm_g, w_out=m_w_out, norm_ffn_g=m_norm_ffn_g,
                 w_up=m_w_up, conv_f_w=m_conv_f_w, w_down=m_w_down, rel_bias=m_rel_bias, final_g=m_final_g)
    mom_v = dict(norm_mix_g=v_norm_mix_g, w_in=v_w_in, conv_a_w=v_conv_a_w, conv_c_w=v_conv_c_w, conv_c_b=v_conv_c_b,
                 ln_c_g=v_ln_c_g, ln_c_b=v_ln_c_b, out_norm_g=v_out_norm_g, w_out=v_w_out, norm_ffn_g=v_norm_ffn_g,
                 w_up=v_w_up, conv_f_w=v_conv_f_w, w_down=v_w_down, rel_bias=v_rel_bias, final_g=v_final_g)
    xi, yi, ci = _me()
    chip = _chip_of(xi, yi)
    c_arr = jnp.reshape(ci, (1,)).astype(I32)
    chip_arr = jnp.reshape(chip, (1,)).astype(I32)
    me_arr = jnp.reshape(4 * xi + 2 * yi + ci, (1,)).astype(I32)
    where = jnp.stack([chip, ci]).astype(I32)
    depth = w_out.shape[0]

    wts = {"in_t": _cast_into_gathered(jnp.swapaxes(w_in, 1, 2), chip_arr, False, "cast_in"),
           "out": _cast_into_gathered(w_out, chip_arr, False, "cast_out"),
           "up": _cast_into_gathered(w_up, chip_arr, True, "cast_up"),
           "down": _cast_into_gathered(w_down, chip_arr, False, "cast_down")}
    store = {}
    _run_comm(_small_gather_comm(_pack([weights[n] for n in SHARDED_SMALL]), store), "ag_small")
    _run_comm(_ag_comm(wts, 0, BIG, BIG), "ag_weights")
    prm = {n: weights[n] for n in SMALL if n not in SHARDED_SMALL}
    per_chip = [_unpack(store["small"][j], [weights[n].shape for n in SHARDED_SMALL]) for j in range(N_CHIPS)]
    for i, n in enumerate(SHARDED_SMALL):
        prm[n] = jnp.concatenate([per_chip[j][i] for j in range(N_CHIPS)], axis=-1)

    sched = _MeshSchedule(wts, depth, c_arr, where)
    loss_row, dx, small, d_rel, d_final = _local_step(x[0], loss_target[0], wts, prm, sched)
    reduced = sched.full
    loss = lax.psum(loss_row[0, 0], ("x", "y", "c"))

    stacked = {n: jnp.stack([small[l][n] for l in range(depth)]) for n in small[0]}
    stacked["rel_bias"] = d_rel
    stacked["final_g"] = d_final
    full_shapes = [stacked[n].shape for n in SMALL]
    partial = _pack([stacked[n] for n in SMALL])
    summed = _unpack(_sum_slabs(_gather_partials(partial), partial, me_arr), full_shapes)
    grads = {}
    for n, g in zip(SMALL, summed):
        if n in SHARDED_SMALL:
            width = weights[n].shape[-1]
            g = lax.dynamic_slice_in_dim(g, chip * width, width, axis=g.ndim - 1)
        grads[n] = g

    shard_shapes = {"in_t": jnp.swapaxes(w_in, 1, 2).shape, "out": w_out.shape, "up": w_up.shape, "down": w_down.shape}
    red = {k: reduced[k].reshape(shard_shapes[k]) for k in BIG}
    grads["w_in"] = jnp.swapaxes(red["in_t"], 1, 2)
    grads["w_out"], grads["w_up"], grads["w_down"] = red["out"], red["up"], red["down"]

    delta, new_m, new_v = {}, {}, {}
    for n in ("w_in", "w_out", "w_up", "w_down"):
        shp = weights[n].shape
        flat = lambda a, shp=shp: a.reshape(shp[0] * shp[1], shp[2])
        tile = max(t for t in range(8, 257, 8) if shp[1] % t == 0)
        d, mn, vn = _adamw(flat(weights[n]), flat(grads[n]), flat(mom_m[n]), flat(mom_v[n]), "adamw_" + n, tile)
        delta[n], new_m[n], new_v[n] = d.reshape(shp), mn.reshape(shp), vn.reshape(shp)
    shapes = [weights[n].shape for n in SMALL]
    packed = [_pack([src[n] for n in SMALL]) for src in (weights, grads, mom_m, mom_v)]
    d, mn, vn = _adamw(*packed, "adamw_small", packed[0].shape[0] // 2)
    for n, a, b, c in zip(SMALL, _unpack(d, shapes), _unpack(mn, shapes), _unpack(vn, shapes)):
        delta[n], new_m[n], new_v[n] = a, b, c

    order = ("norm_mix_g", "w_in", "conv_a_w", "conv_c_w", "conv_c_b", "ln_c_g", "ln_c_b", "out_norm_g", "w_out",
             "norm_ffn_g", "w_up", "conv_f_w", "w_down", "rel_bias", "final_g")
    return (loss, dx[None], *[grads[n] for n in order], *[delta[n] for n in order], *[new_m[n] for n in order],
            *[new_v[n] for n in order])
```

```python
import functools
import math

import numpy as np
import jax
import jax.numpy as jnp
from jax import lax
from jax.experimental import pallas as pl
from jax.experimental.pallas import tpu as pltpu

F32 = jnp.float32
BF16 = jnp.bfloat16
I32 = jnp.int32

EPS = 1e-6
NEG = -1e30
D_HEAD = 64
LANES = 128
BLK = 128
ATTN_GROUP_FWD = 4
ATTN_GROUP_BWD = 4
DILATED_BRANCHES = ((128, 1), (512, 4), (2048, 16))
NUM_BUCKETS = 32
MAX_DISTANCE = 2048
SHORT_CONV = 3
CONFORMER_CONV = 31
FFN_CONV = 3
PAD_SHORT = 8
PAD_LONG = 32
ROW_CHUNK = 256
V7X_VMEM_BYTES = 64 * 1024 * 1024
VMEM_REQUEST = V7X_VMEM_BYTES * 7 // 8

ADAM_LR = 0.001
ADAM_B1 = 0.9
ADAM_B2 = 0.999
ADAM_EPS = 1e-08
ADAM_WD = 0.01
ADAM_STEP = 10

MESH = pl.DeviceIdType.MESH
ANY = pl.BlockSpec(memory_space=pl.ANY)


def _sds(shape, dtype):
    return jax.ShapeDtypeStruct(tuple(shape), dtype)


class _Comm:
    def __init__(self, ins, out_shapes, aliases, sems, start, finish, done):
        self.ins, self.out_shapes, self.aliases, self.sems = list(ins), list(out_shapes), dict(aliases), list(sems)
        self.start, self.finish, self.done = start, finish, done


def _pcall(body, *, name, out_shape, grid=(), in_specs=None, out_specs=None, scratch_shapes=(), vmem=VMEM_REQUEST,
           aliases=None, prefetch=0, comm=None):
    params = pltpu.CompilerParams(dimension_semantics=("arbitrary",) * len(grid), vmem_limit_bytes=vmem)
    single = not isinstance(out_shape, (tuple, list))
    outs = [out_shape] if single else list(out_shape)
    ospecs = [out_specs] if single else list(out_specs)
    ispecs, scratch, aliases = list(in_specs), list(scratch_shapes), dict(aliases or {})
    n_in, n_out, n_scr = len(ispecs), len(outs), len(scratch)
    kernel_body = body
    if comm is not None:
        n_ci, n_co = len(comm.ins), len(comm.out_shapes)

        def kernel_body(*refs):
            pre, rest = refs[:prefetch], refs[prefetch:]
            core_in, c_in = rest[:n_in], rest[n_in:n_in + n_ci]
            o0 = n_in + n_ci
            core_out, c_out = rest[o0:o0 + n_out], rest[o0 + n_out:o0 + n_out + n_co]
            s0 = o0 + n_out + n_co
            core_scr, c_sem = rest[s0:s0 + n_scr], rest[s0 + n_scr:]
            first = functools.reduce(jnp.logical_and, [pl.program_id(a) == 0 for a in range(len(grid))])
            last = functools.reduce(jnp.logical_and, [pl.program_id(a) == grid[a] - 1 for a in range(len(grid))])
            pl.when(first)(lambda: comm.start(c_in, c_out, c_sem))
            body(*pre, *core_in, *core_out, *core_scr)
            pl.when(last)(lambda: comm.finish(c_in, c_out, c_sem))

        for i, o in comm.aliases.items():
            aliases[prefetch + n_in + i] = n_out + o
        ispecs += [ANY] * n_ci
        ospecs += [ANY] * n_co
        outs += comm.out_shapes
        scratch += comm.sems
    if prefetch:
        spec = pltpu.PrefetchScalarGridSpec(num_scalar_prefetch=prefetch, grid=grid, in_specs=ispecs,
                                            out_specs=tuple(ospecs), scratch_shapes=scratch)
        call = pl.pallas_call(kernel_body, name=name, out_shape=tuple(outs), grid_spec=spec,
                              input_output_aliases=aliases, compiler_params=params)
    else:
        call = pl.pallas_call(kernel_body, name=name, out_shape=tuple(outs), grid=grid, in_specs=ispecs,
                              out_specs=tuple(ospecs), scratch_shapes=scratch, input_output_aliases=aliases,
                              compiler_params=params)

    def run(*args):
        res = call(*args, *(comm.ins if comm is not None else ()))
        if comm is not None:
            comm.done(res[n_out:])
        return res[0] if single else tuple(res[:n_out])

    return run


def _run_comm(comm, name):
    def body(*refs):
        n_ci, n_co = len(comm.ins), len(comm.out_shapes)
        c_in, c_out, c_sem = refs[:n_ci], refs[n_ci:n_ci + n_co], refs[n_ci + n_co:]
        comm.start(c_in, c_out, c_sem)
        comm.finish(c_in, c_out, c_sem)

    res = pl.pallas_call(body, name=name, out_shape=tuple(comm.out_shapes), in_specs=[ANY] * len(comm.ins),
                         out_specs=tuple([ANY] * len(comm.out_shapes)), scratch_shapes=comm.sems,
                         input_output_aliases=comm.aliases)(*comm.ins)
    comm.done(res)


def _dot(a, b):
    return lax.dot_general(a, b, (((1,), (0,)), ((), ())), preferred_element_type=F32)


def _dot_nt(a, b):
    return lax.dot_general(a, b, (((1,), (1,)), ((), ())), preferred_element_type=F32)


def _dot_tn(a, b):
    return lax.dot_general(a, b, (((0,), (0,)), ((), ())), preferred_element_type=F32)


def _sigmoid(x):
    return 1.0 / (1.0 + jnp.exp(-x))


def _rstd(x):
    return lax.rsqrt(jnp.mean(x * x, axis=-1, keepdims=True) + EPS)


def _rms_fwd(x, g, name, comm=None):
    s, d = x.shape
    tm = ROW_CHUNK

    def body(x_ref, g_ref, o_ref):
        xv = x_ref[...]
        o_ref[...] = (xv * _rstd(xv) * g_ref[...]).astype(BF16)

    return _pcall(body, name=name, out_shape=_sds((s, d), BF16), grid=(s // tm,),
                  in_specs=[pl.BlockSpec((tm, d), lambda i: (i, 0)), pl.BlockSpec((1, d), lambda i: (0, 0))],
                  out_specs=pl.BlockSpec((tm, d), lambda i: (i, 0)), comm=comm)(x, g)


def _rms_bwd(x, g, dh, dres, name):
    s, d = x.shape
    tm = ROW_CHUNK

    def body(x_ref, g_ref, dh_ref, dres_ref, dx_ref, dxb_ref, dg_ref):
        i = pl.program_id(0)
        xv = x_ref[...]
        r = _rstd(xv)
        xh = xv * r
        dhv = dh_ref[...]
        gd = dhv * g_ref[...]
        dx = dres_ref[...] + r * (gd - xh * jnp.mean(gd * xh, axis=-1, keepdims=True))
        dx_ref[...] = dx
        dxb_ref[...] = dx.astype(BF16)
        part = jnp.sum(dhv * xh, axis=0, keepdims=True)

        @pl.when(i == 0)
        def _():
            dg_ref[...] = part

        @pl.when(i > 0)
        def _():
            dg_ref[...] += part

    row = pl.BlockSpec((tm, d), lambda i: (i, 0))
    vec = pl.BlockSpec((1, d), lambda i: (0, 0))
    return _pcall(body, name=name, out_shape=(_sds((s, d), F32), _sds((s, d), BF16), _sds((1, d), F32)),
                  grid=(s // tm,), in_specs=[row, vec, row, row], out_specs=(row, row, vec))(x, g, dh, dres)


def _final_loss(x, g, tgt, name):
    s, d = x.shape
    tm = ROW_CHUNK

    def body(x_ref, g_ref, t_ref, loss_ref, dx_ref, dxb_ref, dg_ref):
        i = pl.program_id(0)
        xv = x_ref[...]
        r = _rstd(xv)
        xh = xv * r
        e = xh * g_ref[...] - t_ref[...]
        lpart = 0.5 * jnp.sum(jnp.mean(e * e, axis=-1, keepdims=True), axis=0, keepdims=True)
        dy = e * (1.0 / d)
        gd = dy * g_ref[...]
        dx = r * (gd - xh * jnp.mean(gd * xh, axis=-1, keepdims=True))
        dx_ref[...] = dx
        dxb_ref[...] = dx.astype(BF16)
        part = jnp.sum(dy * xh, axis=0, keepdims=True)
        lrow = jnp.broadcast_to(lpart, (1, LANES))

        @pl.when(i == 0)
        def _():
            dg_ref[...] = part
            loss_ref[...] = lrow

        @pl.when(i > 0)
        def _():
            dg_ref[...] += part
            loss_ref[...] += lrow

    row = pl.BlockSpec((tm, d), lambda i: (i, 0))
    vec = pl.BlockSpec((1, d), lambda i: (0, 0))
    return _pcall(body, name=name,
                  out_shape=(_sds((1, LANES), F32), _sds((s, d), F32), _sds((s, d), BF16), _sds((1, d), F32)),
                  grid=(s // tm,), in_specs=[row, vec, row],
                  out_specs=(pl.BlockSpec((1, LANES), lambda i: (0, 0)), row, row, vec))(x, g, tgt)


def _mm_n(a, b, layer, *, nt, tn, out_dtype, name, resid=None, b_part=0, comm=None):
    s, k = a.shape
    n = b.shape[1] if nt else b.shape[2]
    rows = 512

    def body(a_ref, b_ref, *refs):
        o_ref = refs[-1]
        bv = b_ref[...]
        for r0 in range(0, s, rows):
            av = a_ref[r0:r0 + rows, :]
            prod = _dot_nt(av, bv) if nt else _dot(av, bv)
            if resid is not None:
                prod = refs[0][r0:r0 + rows, :] + prod
            o_ref[r0:r0 + rows, :] = prod.astype(out_dtype)

    b_spec = (pl.BlockSpec((None, tn, k), lambda j: (layer, j, b_part)) if nt
              else pl.BlockSpec((None, k, tn), lambda j: (layer, b_part, j)))
    col = pl.BlockSpec((s, tn), lambda j: (0, j))
    extra = () if resid is None else (resid,)
    return _pcall(body, name=name, out_shape=_sds((s, n), out_dtype), grid=(n // tn,),
                  in_specs=[pl.BlockSpec((s, k), lambda j: (0, 0)), b_spec] + [col] * len(extra),
                  out_specs=col, comm=comm)(a, b, *extra)


def _mm_tn(a, b, *, t, name):
    s, ka = a.shape
    n = b.shape[1]

    def body(a_ref, b_ref, o_ref):
        o_ref[...] = _dot_tn(a_ref[...], b_ref[...]).astype(BF16)

    return _pcall(body, name=name, out_shape=_sds((ka, n), BF16), grid=(ka // t,),
                  in_specs=[pl.BlockSpec((s, t), lambda i: (0, i)), pl.BlockSpec((s, n), lambda i: (0, 0))],
                  out_specs=pl.BlockSpec((t, n), lambda i: (i, 0)))(a, b)


def _mm_tn2(a, b_lo, b_hi, *, t, name):
    s, ka = a.shape
    half = b_lo.shape[1]
    nb = half // t

    def body(a_ref, lo_ref, hi_ref, o_ref):
        j = pl.program_id(0)

        @pl.when(j < nb)
        def _():
            o_ref[...] = _dot_tn(a_ref[...], lo_ref[...]).astype(BF16)

        @pl.when(j >= nb)
        def _():
            o_ref[...] = _dot_tn(a_ref[...], hi_ref[...]).astype(BF16)

    return _pcall(body, name=name, out_shape=_sds((ka, 2 * half), BF16), grid=(2 * nb,),
                  in_specs=[pl.BlockSpec((s, ka), lambda j: (0, 0)),
                            pl.BlockSpec((s, t), lambda j: (0, jnp.minimum(j, nb - 1))),
                            pl.BlockSpec((s, t), lambda j: (0, jnp.maximum(j - nb, 0)))],
                  out_specs=pl.BlockSpec((ka, t), lambda j: (0, j)))(a, b_lo, b_hi)


SUBLANES = 8


def _tap_windows(win, width, lead, rows):
    offs = [lead + k for k in range(width)]
    if width <= SUBLANES:
        return [win[o:o + rows, :] for o in offs]
    n = win.shape[0]
    out = {}
    for r in sorted({o % SUBLANES for o in offs}):
        base = win if r == 0 else pltpu.roll(win, n - r, axis=0)
        for o in offs:
            if o % SUBLANES == r:
                out[o - lead] = base[o - r:o - r + rows, :]
    return [out[k] for k in range(width)]


def _conv_taps(taps, w_ref):
    acc = None
    for k, tap in enumerate(taps):
        term = w_ref[pl.ds(k, 1), :] * tap
        acc = term if acc is None else acc + term
    return acc


def _causal_taps(win, width, pad, rows):
    return _tap_windows(win, width, pad - (width - 1), rows)


def _anticausal_taps(win, width, rows):
    return _tap_windows(win, width, 0, rows)[::-1]


def _conv_wgrad(dw_ref, g, taps):
    for k, tap in enumerate(taps):
        dw_ref[pl.ds(k, 1), :] += jnp.sum(g * tap, axis=0, keepdims=True)


def _mixer_a_fwd(ab, taps_t, wa_ref):
    ct = _conv_taps(taps_t, wa_ref)
    return ab * ct, ct


def _mixer_c_fwd(taps_u, wc_ref, cb_ref, lg_ref, lb_ref):
    u = _conv_taps(taps_u, wc_ref) + cb_ref[...]
    mu = jnp.mean(u, axis=-1, keepdims=True)
    uc = u - mu
    rs = lax.rsqrt(jnp.mean(uc * uc, axis=-1, keepdims=True) + EPS)
    uh = uc * rs
    ln = uh * lg_ref[...] + lb_ref[...]
    sg = _sigmoid(ln)
    return ln * sg, ln, sg, uh, rs


def _mix_fwd(z, wa, wc, cb, lg, lb, ga, gc, name):
    s = z.shape[0]
    w = wa.shape[1]
    nblk = z.shape[1] // w
    rc = ROW_CHUNK

    def body(ah_ref, ab_ref, ac_ref, cv_ref, cg_ref, wa_ref, wc_ref, cb_ref, lg_ref, lb_ref, ga_ref, gc_ref,
             ya_ref, yc_ref, tpad, upad):
        tpad[pl.ds(0, PAD_SHORT), :] = jnp.zeros((PAD_SHORT, w), F32)
        upad[pl.ds(0, PAD_LONG), :] = jnp.zeros((PAD_LONG, w), F32)

        def chunk(i, carry):
            base = pl.multiple_of(i * rc, rc)
            rows = pl.ds(base, rc)
            ah, ab, ac = ah_ref[rows, :], ab_ref[rows, :], ac_ref[rows, :]
            tpad[pl.ds(base + PAD_SHORT, rc), :] = ac * ah
            ya, _ = _mixer_a_fwd(ab, _causal_taps(tpad[pl.ds(base, rc + PAD_SHORT), :], SHORT_CONV, PAD_SHORT, rc), wa_ref)
            ya_ref[rows, :] = (ya * _rstd(ya) * ga_ref[...]).astype(BF16)
            upad[pl.ds(base + PAD_LONG, rc), :] = cv_ref[rows, :] * _sigmoid(cg_ref[rows, :])
            taps_u = _causal_taps(upad[pl.ds(base, rc + PAD_LONG), :], CONFORMER_CONV, PAD_LONG, rc)
            yc = _mixer_c_fwd(taps_u, wc_ref, cb_ref, lg_ref, lb_ref)[0]
            yc_ref[rows, :] = (yc * _rstd(yc) * gc_ref[...]).astype(BF16)
            return carry

        lax.fori_loop(0, s // rc, chunk, 0)

    def zblk(j):
        return pl.BlockSpec((s, w), lambda i: (0, j))

    def whole(a):
        return pl.BlockSpec(a.shape, lambda i: (0, 0))

    return _pcall(
        body, name=name, out_shape=(_sds((s, w), BF16), _sds((s, w), BF16)), grid=(1,),
        in_specs=[zblk(0), zblk(1), zblk(2), zblk(nblk - 2), zblk(nblk - 1)] + [whole(a) for a in (wa, wc, cb, lg, lb, ga, gc)],
        out_specs=(pl.BlockSpec((s, w), lambda i: (0, 0)), pl.BlockSpec((s, w), lambda i: (0, 0))),
        scratch_shapes=[pltpu.VMEM((s + PAD_SHORT, w), F32), pltpu.VMEM((s + PAD_LONG, w), F32)],
    )(z, z, z, z, z, wa, wc, cb, lg, lb, ga, gc)


def _mix_bwd(z, dy, wa, wc, cb, lg, lb, ga, gc, name):
    s = z.shape[0]
    w = wa.shape[1]
    nblk = z.shape[1] // w
    nyb = dy.shape[1] // w
    rc = ROW_CHUNK

    def body(ah_ref, ab_ref, ac_ref, cv_ref, cg_ref, dya_ref, dyc_ref,
             wa_ref, wc_ref, cb_ref, lg_ref, lb_ref, ga_ref, gc_ref,
             dza_ref, dzc_ref, dwa_ref, dwc_ref, dcb_ref, dlg_ref, dlb_ref, dga_ref, dgc_ref,
             tpad, upad, dctp, dup):
        tpad[pl.ds(0, PAD_SHORT), :] = jnp.zeros((PAD_SHORT, w), F32)
        upad[pl.ds(0, PAD_LONG), :] = jnp.zeros((PAD_LONG, w), F32)
        dctp[pl.ds(s, PAD_SHORT), :] = jnp.zeros((PAD_SHORT, w), F32)
        dup[pl.ds(s, PAD_LONG), :] = jnp.zeros((PAD_LONG, w), F32)
        for ref in (dwa_ref, dwc_ref, dcb_ref, dlg_ref, dlb_ref, dga_ref, dgc_ref):
            ref[...] = jnp.zeros(ref.shape, F32)

        def rms_bwd(y, g_ref, dyn, dg_ref):
            r = _rstd(y)
            yh = y * r
            gd = dyn * g_ref[...]
            dg_ref[...] += jnp.sum(dyn * yh, axis=0, keepdims=True)
            return r * (gd - yh * jnp.mean(gd * yh, axis=-1, keepdims=True))

        def first(i, carry):
            base = pl.multiple_of(i * rc, rc)
            rows = pl.ds(base, rc)
            ah, ab, ac = ah_ref[rows, :], ab_ref[rows, :], ac_ref[rows, :]
            tpad[pl.ds(base + PAD_SHORT, rc), :] = ac * ah
            taps_t = _causal_taps(tpad[pl.ds(base, rc + PAD_SHORT), :], SHORT_CONV, PAD_SHORT, rc)
            ya, ct = _mixer_a_fwd(ab, taps_t, wa_ref)
            dya = rms_bwd(ya, ga_ref, dya_ref[rows, :], dga_ref)
            dza_ref[rows, w:2 * w] = (dya * ct).astype(BF16)
            dct = dya * ab
            dctp[rows, :] = dct
            _conv_wgrad(dwa_ref, dct, taps_t)

            upad[pl.ds(base + PAD_LONG, rc), :] = cv_ref[rows, :] * _sigmoid(cg_ref[rows, :])
            taps_u = _causal_taps(upad[pl.ds(base, rc + PAD_LONG), :], CONFORMER_CONV, PAD_LONG, rc)
            yc, ln, sg, uh, rs = _mixer_c_fwd(taps_u, wc_ref, cb_ref, lg_ref, lb_ref)
            dyc = rms_bwd(yc, gc_ref, dyc_ref[rows, :], dgc_ref)
            dln = dyc * (sg * (1.0 + ln * (1.0 - sg)))
            dlg_ref[...] += jnp.sum(dln * uh, axis=0, keepdims=True)
            dlb_ref[...] += jnp.sum(dln, axis=0, keepdims=True)
            duh = dln * lg_ref[...]
            du = rs * (duh - jnp.mean(duh, axis=-1, keepdims=True) - uh * jnp.mean(duh * uh, axis=-1, keepdims=True))
            dcb_ref[...] += jnp.sum(du, axis=0, keepdims=True)
            dup[rows, :] = du
            _conv_wgrad(dwc_ref, du, taps_u)
            return carry

        lax.fori_loop(0, s // rc, first, 0)

        def second(i, carry):
            base = pl.multiple_of(i * rc, rc)
            rows = pl.ds(base, rc)
            dt = _conv_taps(_anticausal_taps(dctp[pl.ds(base, rc + PAD_SHORT), :], SHORT_CONV, rc), wa_ref)
            dza_ref[rows, 0:w] = (dt * ac_ref[rows, :]).astype(BF16)
            dza_ref[rows, 2 * w:3 * w] = (dt * ah_ref[rows, :]).astype(BF16)
            du0 = _conv_taps(_anticausal_taps(dup[pl.ds(base, rc + PAD_LONG), :], CONFORMER_CONV, rc), wc_ref)
            sg = _sigmoid(cg_ref[rows, :])
            dzc_ref[rows, 0:w] = (du0 * sg).astype(BF16)
            dzc_ref[rows, w:2 * w] = (du0 * cv_ref[rows, :] * sg * (1.0 - sg)).astype(BF16)
            return carry

        lax.fori_loop(0, s // rc, second, 0)

    def blk(j):
        return pl.BlockSpec((s, w), lambda i: (0, j))

    def whole(a):
        return pl.BlockSpec(tuple(a.shape), lambda i: (0, 0))

    params = (wa, wc, cb, lg, lb, ga, gc)
    outs = (_sds((s, 3 * w), BF16), _sds((s, 2 * w), BF16)) + tuple(_sds(p.shape, F32) for p in params)
    return _pcall(
        body, name=name, out_shape=outs, grid=(1,),
        in_specs=[blk(0), blk(1), blk(2), blk(nblk - 2), blk(nblk - 1), blk(0), blk(nyb - 1)] + [whole(p) for p in params],
        out_specs=tuple(whole(o) for o in outs),
        scratch_shapes=[pltpu.VMEM((s + PAD_SHORT, w), F32), pltpu.VMEM((s + PAD_LONG, w), F32),
                        pltpu.VMEM((s + PAD_SHORT, w), F32), pltpu.VMEM((s + PAD_LONG, w), F32)],
    )(z, z, z, z, z, dy, dy, *params)


def _ffn_act_fwd(up, wf, name, comm=None):
    s, f2 = up.shape
    f = f2 // 2
    tc = 256
    nb = f // tc
    rc = ROW_CHUNK

    def body(g_ref, v_ref, wg_ref, wv_ref, o_ref, gpad, vpad):
        gpad[pl.ds(0, PAD_SHORT), :] = jnp.zeros((PAD_SHORT, tc), F32)
        vpad[pl.ds(0, PAD_SHORT), :] = jnp.zeros((PAD_SHORT, tc), F32)

        def chunk(i, carry):
            base = pl.multiple_of(i * rc, rc)
            rows = pl.ds(base, rc)
            gpad[pl.ds(base + PAD_SHORT, rc), :] = g_ref[rows, :].astype(F32)
            vpad[pl.ds(base + PAD_SHORT, rc), :] = v_ref[rows, :].astype(F32)
            gc = _conv_taps(_causal_taps(gpad[pl.ds(base, rc + PAD_SHORT), :], FFN_CONV, PAD_SHORT, rc), wg_ref)
            vc = _conv_taps(_causal_taps(vpad[pl.ds(base, rc + PAD_SHORT), :], FFN_CONV, PAD_SHORT, rc), wv_ref)
            o_ref[rows, :] = (gc * _sigmoid(gc) * vc).astype(BF16)
            return carry

        lax.fori_loop(0, s // rc, chunk, 0)

    return _pcall(
        body, name=name, out_shape=_sds((s, f), BF16), grid=(nb,),
        in_specs=[pl.BlockSpec((s, tc), lambda j: (0, j)), pl.BlockSpec((s, tc), lambda j: (0, j + nb)),
                  pl.BlockSpec((FFN_CONV, tc), lambda j: (0, j)), pl.BlockSpec((FFN_CONV, tc), lambda j: (0, j + nb))],
        out_specs=pl.BlockSpec((s, tc), lambda j: (0, j)),
        scratch_shapes=[pltpu.VMEM((s + PAD_SHORT, tc), F32), pltpu.VMEM((s + PAD_SHORT, tc), F32)], comm=comm,
    )(up, up, wf, wf)


def _ffn_act_bwd(up, dact, wf, name, comm=None):
    s, f2 = up.shape
    f = f2 // 2
    tc = 256
    nb = f // tc
    rc = ROW_CHUNK

    def body(g_ref, v_ref, da_ref, wg_ref, wv_ref, act_ref, dg_ref, dv_ref, dwg_ref, dwv_ref, gpad, vpad, dgp, dvp):
        gpad[pl.ds(0, PAD_SHORT), :] = jnp.zeros((PAD_SHORT, tc), F32)
        vpad[pl.ds(0, PAD_SHORT), :] = jnp.zeros((PAD_SHORT, tc), F32)
        dgp[pl.ds(s, PAD_SHORT), :] = jnp.zeros((PAD_SHORT, tc), F32)
        dvp[pl.ds(s, PAD_SHORT), :] = jnp.zeros((PAD_SHORT, tc), F32)
        dwg_ref[...] = jnp.zeros((FFN_CONV, tc), F32)
        dwv_ref[...] = jnp.zeros((FFN_CONV, tc), F32)

        def first(i, carry):
            base = pl.multiple_of(i * rc, rc)
            rows = pl.ds(base, rc)
            gpad[pl.ds(base + PAD_SHORT, rc), :] = g_ref[rows, :].astype(F32)
            vpad[pl.ds(base + PAD_SHORT, rc), :] = v_ref[rows, :].astype(F32)
            taps_g = _causal_taps(gpad[pl.ds(base, rc + PAD_SHORT), :], FFN_CONV, PAD_SHORT, rc)
            taps_v = _causal_taps(vpad[pl.ds(base, rc + PAD_SHORT), :], FFN_CONV, PAD_SHORT, rc)
            gc = _conv_taps(taps_g, wg_ref)
            vc = _conv_taps(taps_v, wv_ref)
            sg = _sigmoid(gc)
            silu = gc * sg
            act_ref[rows, :] = (silu * vc).astype(BF16)
            da = da_ref[rows, :].astype(F32)
            dgc = da * vc * (sg * (1.0 + gc * (1.0 - sg)))
            dvc = da * silu
            dgp[rows, :] = dgc
            dvp[rows, :] = dvc
            _conv_wgrad(dwg_ref, dgc, taps_g)
            _conv_wgrad(dwv_ref, dvc, taps_v)
            return carry

        lax.fori_loop(0, s // rc, first, 0)

        def second(i, carry):
            base = pl.multiple_of(i * rc, rc)
            rows = pl.ds(base, rc)
            dg_ref[rows, :] = _conv_taps(_anticausal_taps(dgp[pl.ds(base, rc + PAD_SHORT), :], FFN_CONV, rc), wg_ref).astype(BF16)
            dv_ref[rows, :] = _conv_taps(_anticausal_taps(dvp[pl.ds(base, rc + PAD_SHORT), :], FFN_CONV, rc), wv_ref).astype(BF16)
            return carry

        lax.fori_loop(0, s // rc, second, 0)

    lo = pl.BlockSpec((s, tc), lambda j: (0, j))
    hi = pl.BlockSpec((s, tc), lambda j: (0, j + nb))
    wlo = pl.BlockSpec((FFN_CONV, tc), lambda j: (0, j))
    whi = pl.BlockSpec((FFN_CONV, tc), lambda j: (0, j + nb))
    act, dgate, dval, dwg, dwv = _pcall(
        body, name=name,
        out_shape=(_sds((s, f), BF16), _sds((s, f), BF16), _sds((s, f), BF16), _sds((FFN_CONV, f), F32), _sds((FFN_CONV, f), F32)),
        grid=(nb,), in_specs=[lo, hi, lo, wlo, whi], out_specs=(lo, lo, lo, wlo, wlo),
        scratch_shapes=[pltpu.VMEM((s + PAD_SHORT, tc), F32) for _ in range(4)], comm=comm,
    )(up, up, dact, wf, wf)
    return act, dgate, dval, jnp.concatenate([dwg, dwv], axis=1)


def _y_assemble(yan, yb, ycn, gb, name):
    s, w = yan.shape
    wb = yb.shape[1]
    tm = ROW_CHUNK

    def body(ya_ref, yb_ref, yc_ref, g_ref, o_ref):
        ybv = yb_ref[...]
        o_ref[:, 0:w] = ya_ref[...]
        o_ref[:, w:w + wb] = (ybv * _rstd(ybv) * g_ref[...]).astype(BF16)
        o_ref[:, w + wb:] = yc_ref[...]

    return _pcall(body, name=name, out_shape=_sds((s, 2 * w + wb), BF16), grid=(s // tm,),
                  in_specs=[pl.BlockSpec((tm, w), lambda i: (i, 0)), pl.BlockSpec((tm, wb), lambda i: (i, 0)),
                            pl.BlockSpec((tm, w), lambda i: (i, 0)), pl.BlockSpec((1, wb), lambda i: (0, 0))],
                  out_specs=pl.BlockSpec((tm, 2 * w + wb), lambda i: (i, 0)))(yan, yb, ycn, gb)


def _yb_norm_bwd(yb, dy, gb, name):
    s, wb = yb.shape
    w = wb // 2
    heads = wb // D_HEAD
    tm = ROW_CHUNK

    def body(yb_ref, d1_ref, d2_ref, g_ref, dyb_ref, dl_ref, dg_ref):
        i = pl.program_id(0)
        y = yb_ref[...]
        dyn = jnp.concatenate([d1_ref[...], d2_ref[...]], axis=1)
        r = _rstd(y)
        yh = y * r
        gd = dyn * g_ref[...]
        dyb = r * (gd - yh * jnp.mean(gd * yh, axis=-1, keepdims=True))
        dyb_ref[...] = dyb
        part = jnp.sum(dyn * yh, axis=0, keepdims=True)
        prod = dyb * y
        even = lax.broadcasted_iota(I32, (tm, LANES), 1) < D_HEAD
        for p in range(heads // 2):
            blk = prod[:, p * LANES:(p + 1) * LANES]
            ev = jnp.sum(jnp.where(even, blk, 0.0), axis=1, keepdims=True)
            od = jnp.sum(jnp.where(even, 0.0, blk), axis=1, keepdims=True)
            dl_ref[2 * p] = jnp.broadcast_to(ev, (tm, LANES))
            dl_ref[2 * p + 1] = jnp.broadcast_to(od, (tm, LANES))

        @pl.when(i == 0)
        def _():
            dg_ref[...] = part

        @pl.when(i > 0)
        def _():
            dg_ref[...] += part

    return _pcall(
        body, name=name, out_shape=(_sds((s, wb), F32), _sds((heads, s, LANES), F32), _sds((1, wb), F32)),
        grid=(s // tm,),
        in_specs=[pl.BlockSpec((tm, wb), lambda i: (i, 0)), pl.BlockSpec((tm, w), lambda i: (i, 1)),
                  pl.BlockSpec((tm, w), lambda i: (i, 2)), pl.BlockSpec((1, wb), lambda i: (0, 0))],
        out_specs=(pl.BlockSpec((tm, wb), lambda i: (i, 0)), pl.BlockSpec((heads, tm, LANES), lambda i: (0, i, 0)),
                   pl.BlockSpec((1, wb), lambda i: (0, 0))),
    )(yb, dy, dy, gb)


def _t5_bucket_table():
    max_exact = NUM_BUCKETS // 2
    out = np.full((len(DILATED_BRANCHES), BLK, 2 * BLK), -1, np.int32)
    rel = np.arange(BLK)[:, None] - np.arange(2 * BLK)[None, :] + BLK
    for b, (window, dilation) in enumerate(DILATED_BRANCHES):
        n_keys = window // dilation
        dist = np.maximum(rel, 0) * dilation
        d_f = np.maximum(dist, 1).astype(np.float32)
        large = max_exact + (np.log(d_f / np.float32(max_exact)) / np.float32(math.log(MAX_DISTANCE / max_exact))
                             * np.float32(NUM_BUCKETS - max_exact)).astype(np.int32)
        large = np.minimum(large, NUM_BUCKETS - 1)
        bucket = np.where(dist < max_exact, dist, large)
        out[b] = np.where((rel >= 0) & (rel <= n_keys), bucket, -1)
    return out


def _bias_tiles(rel_bias, buckets, name):
    nbk, heads = rel_bias.shape
    nbr = buckets.shape[0]

    def body(rb_ref, bk_ref, o_ref):
        for br in range(nbr):
            bk = bk_ref[br]
            tiles = [jnp.full((BLK, 2 * BLK), NEG, F32) for _ in range(heads)]
            for b in range(nbk):
                hit = bk == b
                tiles = [jnp.where(hit, rb_ref[b, h], tiles[h]) for h in range(heads)]
            for h in range(heads):
                o_ref[br, h] = tiles[h]

    return _pcall(body, name=name, out_shape=_sds((nbr, heads, BLK, 2 * BLK), F32), grid=(1,),
                  in_specs=[pl.BlockSpec(memory_space=pltpu.SMEM), pl.BlockSpec(buckets.shape, lambda i: (0, 0, 0))],
                  out_specs=pl.BlockSpec((nbr, heads, BLK, 2 * BLK), lambda i: (0, 0, 0, 0)))(rel_bias, buckets)


def _bias_grad(dtiles, buckets, nbk, name):
    nbr, heads = dtiles.shape[:2]

    def body(dt_ref, bk_ref, o_ref):
        row = lax.broadcasted_iota(I32, (nbk, LANES), 0)
        col = lax.broadcasted_iota(I32, (nbk, LANES), 1)
        out = jnp.zeros((nbk, LANES), F32)
        for h in range(heads):
            for b in range(nbk):
                tot = jnp.zeros((), F32)
                for br in range(nbr):
                    tot = tot + jnp.sum(jnp.where(bk_ref[br] == b, dt_ref[br, h], 0.0))
                out = jnp.where((row == b) & (col == h), tot, out)
        o_ref[...] = out

    return _pcall(body, name=name, out_shape=_sds((nbk, LANES), F32), grid=(1,),
                  in_specs=[pl.BlockSpec(dtiles.shape, lambda i: (0, 0, 0, 0)), pl.BlockSpec(buckets.shape, lambda i: (0, 0, 0))],
                  out_specs=pl.BlockSpec((nbk, LANES), lambda i: (0, 0)))(dtiles, buckets)


def _largest_divisor(n, cap):
    return max(g for g in range(1, cap + 1) if n % g == 0)


def _attn_blocks(s, visit, group):
    for br, (window, d) in enumerate(DILATED_BRANCHES):
        n_blk = (s // d) // BLK
        span = BLK * d
        g1 = _largest_divisor(d, group)

        def firsts(t, carry, br=br, d=d, g1=g1):
            for j in range(g1):
                visit(br, d, t * g1 + j, False)
            return carry

        lax.fori_loop(0, d // g1, firsts, 0)
        if n_blk > 1:
            total = d * (n_blk - 1)
            g2 = _largest_divisor(total, group)

            def rest(t, carry, br=br, d=d, n_blk=n_blk, span=span, g2=g2):
                for j in range(g2):
                    idx = t * g2 + j
                    visit(br, d, idx // (n_blk - 1) + (1 + idx % (n_blk - 1)) * span, True)
                return carry

            lax.fori_loop(0, total // g2, rest, 0)


def _rows(start, size, d):
    return pl.ds(pl.multiple_of(start, BLK), size) if d == 1 else pl.ds(start, size, stride=d)


def _attn_fwd(z, btiles, col0, name, comm=None):
    s = z.shape[0]
    nbr, heads = btiles.shape[:2]
    pairs = heads // 2
    scale = D_HEAD ** -0.5
    rc = ROW_CHUNK

    def body(q_ref, k_ref, v_ref, bt_ref, yb_ref, lse_ref, acc_ref, m_ref, l_ref):
        even = lax.broadcasted_iota(I32, (BLK, LANES), 1) < D_HEAD
        even2 = lax.broadcasted_iota(I32, (2 * BLK, LANES), 1) < D_HEAD

        def visit(br, d, start, prev):
            kw = 2 * BLK if prev else BLK
            rows_q = _rows(start, BLK, d)
            rows_k = _rows(start - BLK * d, kw, d) if prev else rows_q
            qb = q_ref[rows_q, :]
            kb = k_ref[rows_k, :].astype(BF16)
            vw = v_ref[rows_k, :]
            ev_k = even2 if prev else even
            qm = jnp.concatenate([jnp.where(even, qb, 0.0), jnp.where(even, 0.0, qb)], axis=0).astype(BF16)
            bias = [bt_ref[br, e] if prev else bt_ref[br, e, :, BLK:] for e in range(2)]
            sc = _dot_nt(qm, kb) * scale + jnp.concatenate(bias, axis=0)
            m = jnp.max(sc, axis=1, keepdims=True)
            p = jnp.exp(sc - m)
            l = jnp.sum(p, axis=1, keepdims=True)
            pb = p.astype(BF16)
            vm = jnp.concatenate([jnp.where(ev_k, vw, 0.0), jnp.where(ev_k, 0.0, vw)], axis=0).astype(BF16)
            acc_ref.at[br][rows_q, :] = _dot(jnp.concatenate([pb[:BLK], pb[BLK:]], axis=1), vm)
            for e in range(2):
                m_ref.at[br, e][rows_q, :] = jnp.broadcast_to(m[e * BLK:(e + 1) * BLK], (BLK, LANES))
                l_ref.at[br, e][rows_q, :] = jnp.broadcast_to(l[e * BLK:(e + 1) * BLK], (BLK, LANES))

        _attn_blocks(s, visit, ATTN_GROUP_FWD)

        ev_c = lax.broadcasted_iota(I32, (rc, LANES), 1) < D_HEAD

        def merge(i, carry):
            rows = pl.ds(pl.multiple_of(i * rc, rc), rc)
            wts, dens = [], []
            for e in range(2):
                ms = [m_ref[br, e, rows, :] for br in range(nbr)]
                top = functools.reduce(jnp.maximum, ms)
                w = [jnp.exp(mb - top) for mb in ms]
                den = functools.reduce(lambda a, b: a + b, [w[br] * l_ref[br, e, rows, :] for br in range(nbr)])
                lse_ref[e, rows, :] = top + jnp.log(den)
                wts.append(w)
                dens.append(den)
            num = functools.reduce(lambda a, b: a + b,
                                   [jnp.where(ev_c, wts[0][br], wts[1][br]) * acc_ref[br, rows, :] for br in range(nbr)])
            yb_ref[rows, :] = num / jnp.where(ev_c, dens[0], dens[1])
            return carry

        lax.fori_loop(0, s // rc, merge, 0)

    def zcol(j):
        return pl.BlockSpec((s, LANES), lambda p, j=j: (0, col0 + j + p))

    return _pcall(
        body, name=name, out_shape=(_sds((s, pairs * LANES), F32), _sds((heads, s, LANES), F32)), grid=(pairs,),
        in_specs=[zcol(0), zcol(pairs), zcol(2 * pairs), pl.BlockSpec((nbr, 2, BLK, 2 * BLK), lambda p: (0, p, 0, 0))],
        out_specs=(pl.BlockSpec((s, LANES), lambda p: (0, p)), pl.BlockSpec((2, s, LANES), lambda p: (p, 0, 0))),
        scratch_shapes=[pltpu.VMEM((nbr, s, LANES), F32), pltpu.VMEM((nbr, 2, s, LANES), F32), pltpu.VMEM((nbr, 2, s, LANES), F32)],
        comm=comm,
    )(z, z, z, btiles)


def _attn_bwd(z, btiles, dyb, lse, delta, dbias_in, col0, name, comm=None):
    s = z.shape[0]
    nbr, heads = btiles.shape[:2]
    pairs = heads // 2
    scale = D_HEAD ** -0.5

    def body(q_ref, k_ref, v_ref, bt_ref, dy_ref, lse_ref, dl_ref, dbi_ref,
             dq_ref, dk_ref, dv_ref, db_ref, dqa, dka, dva):
        even = lax.broadcasted_iota(I32, (BLK, LANES), 1) < D_HEAD
        even2 = lax.broadcasted_iota(I32, (2 * BLK, LANES), 1) < D_HEAD
        for ref in (dqa, dka, dva):
            ref[...] = jnp.zeros((s, LANES), F32)
        db_ref[...] = dbi_ref[...]

        def visit(br, d, start, prev):
            kw = 2 * BLK if prev else BLK
            rows_q = _rows(start, BLK, d)
            rows_k = _rows(start - BLK * d, kw, d) if prev else rows_q
            qb = q_ref[rows_q, :]
            dyv = dy_ref[rows_q, :]
            kwin = k_ref[rows_k, :]
            kb = kwin.astype(BF16)
            vb = v_ref[rows_k, :].astype(BF16)
            ev_k = even2 if prev else even
            qm = jnp.concatenate([jnp.where(even, qb, 0.0), jnp.where(even, 0.0, qb)], axis=0).astype(BF16)
            dym = jnp.concatenate([jnp.where(even, dyv, 0.0), jnp.where(even, 0.0, dyv)], axis=0).astype(BF16)
            bias = [bt_ref[br, e] if prev else bt_ref[br, e, :, BLK:] for e in range(2)]
            sc = _dot_nt(qm, kb) * scale + jnp.concatenate(bias, axis=0)
            lt = jnp.concatenate([lse_ref.at[e][rows_q, :] for e in range(2)], axis=0)
            dt = jnp.concatenate([dl_ref.at[e][rows_q, :] for e in range(2)], axis=0)
            if prev:
                lt = jnp.concatenate([lt, lt], axis=1)
                dt = jnp.concatenate([dt, dt], axis=1)
            p = jnp.exp(sc - lt)
            ds = p * (_dot_nt(dym, vb) - dt)
            for e in range(2):
                if prev:
                    db_ref[br, e] += ds[e * BLK:(e + 1) * BLK]
                else:
                    db_ref[br, e, :, BLK:] += ds[e * BLK:(e + 1) * BLK]
            dsb = ds.astype(BF16)
            km = jnp.concatenate([jnp.where(ev_k, kwin, 0.0), jnp.where(ev_k, 0.0, kwin)], axis=0).astype(BF16)
            dqa[rows_q, :] += _dot(jnp.concatenate([dsb[:BLK], dsb[BLK:]], axis=1), km) * scale
            dka[rows_k, :] += _dot_tn(dsb, qm) * scale
            dva[rows_k, :] += _dot_tn(p.astype(BF16), dym)

        _attn_blocks(s, visit, ATTN_GROUP_BWD)
        dq_ref[...] = dqa[...].astype(BF16)
        dk_ref[...] = dka[...].astype(BF16)
        dv_ref[...] = dva[...].astype(BF16)

    def zcol(j):
        return pl.BlockSpec((s, LANES), lambda p, j=j: (0, col0 + j + p))

    col = pl.BlockSpec((s, LANES), lambda p: (0, p))
    stat = pl.BlockSpec((2, s, LANES), lambda p: (p, 0, 0))
    tile = pl.BlockSpec((nbr, 2, BLK, 2 * BLK), lambda p: (0, p, 0, 0))
    wide = _sds((s, pairs * LANES), BF16)
    return _pcall(
        body, name=name, out_shape=(wide, wide, wide, _sds(btiles.shape, F32)), grid=(pairs,),
        in_specs=[zcol(0), zcol(pairs), zcol(2 * pairs), tile, col, stat, stat, tile],
        out_specs=(col, col, col, tile),
        scratch_shapes=[pltpu.VMEM((s, LANES), F32) for _ in range(3)], comm=comm,
    )(z, z, z, btiles, dyb, lse, delta, dbias_in)


def _row(v):
    return v.reshape(1, -1)


class _LocalSchedule:
    def __init__(self):
        self.big = {}

    def fwd_comms(self, l):
        return {}

    def bwd_comms(self, l):
        return {}

    def after_bwd(self, l, grads):
        self.big[l] = grads


def _layer_fwd(l, x, wts, prm, btiles, comms):
    d = x.shape[1]
    wq = d // 4
    gout = prm["out_norm_g"][l]
    h = _rms_fwd(x, _row(prm["norm_mix_g"][l]), "rms_mix_fwd", comm=comms.get("rms_mix_fwd"))
    z = _mm_n(h, wts["in_t"], l, nt=True, tn=256, out_dtype=F32, name="in_proj", comm=comms.get("in_proj"))
    yan, ycn = _mix_fwd(z, prm["conv_a_w"][l], prm["conv_c_w"][l], _row(prm["conv_c_b"][l]), _row(prm["ln_c_g"][l]),
                        _row(prm["ln_c_b"][l]), _row(gout[:wq]), _row(gout[3 * wq:]), "mix_fwd")
    yb, lse = _attn_fwd(z, btiles, 3 * wq // LANES, "attn_fwd", comm=comms.get("attn_fwd"))
    y = _y_assemble(yan, yb, ycn, _row(gout[wq:3 * wq]), "y_assemble")
    x_mid = _mm_n(y, wts["out"], l, nt=False, tn=256, out_dtype=F32, name="out_proj", resid=x, comm=comms.get("out_proj"))
    h2 = _rms_fwd(x_mid, _row(prm["norm_ffn_g"][l]), "rms_ffn_fwd")
    up = _mm_n(h2, wts["up"], l, nt=False, tn=512, out_dtype=BF16, name="up_proj", comm=comms.get("up_proj"))
    act = _ffn_act_fwd(up, prm["conv_f_w"][l], "ffn_act_fwd", comm=comms.get("ffn_act_fwd"))
    x_out = _mm_n(act, wts["down"], l, nt=False, tn=256, out_dtype=F32, name="down_proj", resid=x_mid,
                  comm=comms.get("down_proj"))
    return x_out, (x, h, z, yb, lse, y, x_mid, h2, up)


def _layer_bwd(l, dxo, dxo_b, saved, wts, prm, btiles, dbias, comms):
    x, h, z, yb, lse, y, x_mid, h2, up = saved
    d = x.shape[1]
    wq = d // 4
    f = up.shape[1] // 2
    gout = prm["out_norm_g"][l]
    dact = _mm_n(dxo_b, wts["down"], l, nt=True, tn=256, out_dtype=BF16, name="down_proj_dx", comm=comms.get(SWAP_RIDE))
    act, dgate, dval, dwf = _ffn_act_bwd(up, dact, prm["conv_f_w"][l], "ffn_act_bwd", comm=comms.get("ffn_act_bwd"))
    g_down = _mm_tn(act, dxo_b, t=256, name="down_proj_dw")
    dh2 = _mm_n(dgate, wts["up"], l, nt=True, tn=256, out_dtype=F32, name="up_proj_dx_gate", comm=comms.get(SHARE_RIDE))
    dh2 = _mm_n(dval, wts["up"], l, nt=True, tn=256, out_dtype=F32, name="up_proj_dx_val", resid=dh2, b_part=1)
    dxm, dxm_b, dg_ffn = _rms_bwd(x_mid, _row(prm["norm_ffn_g"][l]), dh2, dxo, "rms_ffn_bwd")
    g_up = _mm_tn2(h2, dgate, dval, t=256, name="up_proj_dw")
    dy = _mm_n(dxm_b, wts["out"], l, nt=True, tn=256, out_dtype=F32, name="out_proj_dx")
    g_out = _mm_tn(y, dxm_b, t=256, name="out_proj_dw")
    dza, dzc, dwa, dwc, dcb, dlg, dlb, dga, dgc = _mix_bwd(
        z, dy, prm["conv_a_w"][l], prm["conv_c_w"][l], _row(prm["conv_c_b"][l]), _row(prm["ln_c_g"][l]),
        _row(prm["ln_c_b"][l]), _row(gout[:wq]), _row(gout[3 * wq:]), "mix_bwd")
    dyb, delta, dgb = _yb_norm_bwd(yb, dy, _row(gout[wq:3 * wq]), "yb_norm_bwd")
    dq, dk, dv, dbias = _attn_bwd(z, btiles, dyb, lse, delta, dbias, 3 * wq // LANES, "attn_bwd",
                                  comm=comms.get("attn_bwd"))
    dz = jnp.concatenate([dza, dq, dk, dv, dzc], axis=1)
    dh = _mm_n(dz, wts["in_t"], l, nt=False, tn=256, out_dtype=F32, name="in_proj_dx")
    dx, dx_b, dg_mix = _rms_bwd(x, _row(prm["norm_mix_g"][l]), dh, dxm, "rms_mix_bwd")
    g_in_t = _mm_tn(dz, h, t=256, name="in_proj_dw")
    big = {"in_t": g_in_t, "out": g_out, "up": g_up, "down": g_down}
    small = {"norm_mix_g": dg_mix[0], "conv_a_w": dwa, "conv_c_w": dwc, "conv_c_b": dcb[0], "ln_c_g": dlg[0],
             "ln_c_b": dlb[0], "out_norm_g": jnp.concatenate([dga[0], dgb[0], dgc[0]]), "norm_ffn_g": dg_ffn[0],
             "conv_f_w": dwf}
    return dx, dx_b, big, small, dbias


def _local_step(x, tgt, wts, prm, sched):
    depth = prm["norm_mix_g"].shape[0]
    buckets = jnp.asarray(_t5_bucket_table())
    btiles = _bias_tiles(prm["rel_bias"], buckets, "bias_tiles")
    saved = []
    for l in range(depth):
        x, sv = _layer_fwd(l, x, wts, prm, btiles, sched.fwd_comms(l))
        saved.append(sv)
    loss, dx, dx_b, dg_final = _final_loss(x, _row(prm["final_g"]), tgt, "final_loss")
    dbias = jnp.zeros(btiles.shape, F32)
    small = [None] * depth
    for l in reversed(range(depth)):
        dx, dx_b, grads, small[l], dbias = _layer_bwd(l, dx, dx_b, saved[l], wts, prm, btiles, dbias, sched.bwd_comms(l))
        sched.after_bwd(l, grads)
    nbk, heads = prm["rel_bias"].shape
    d_rel = _bias_grad(dbias, buckets, nbk, "bias_grad")[:, :heads]
    return loss, dx, small, d_rel, dg_final[0]


BIG = ("in_t", "out", "up", "down")
COL_SHARDED = ("up",)
N_CHIPS = 4
N_DEV = 8
BF16_ROWS = 16


def _me():
    return lax.axis_index("x"), lax.axis_index("y"), lax.axis_index("c")


def _chip_of(x, y):
    return 2 * x + y


def _other_chips(x, y):
    return ((1 - x, y), (x, 1 - y), (1 - x, 1 - y))


def _remote(src, dst, send_sem, recv_sem, device):
    return pltpu.make_async_remote_copy(src_ref=src, dst_ref=dst, send_sem=send_sem, recv_sem=recv_sem,
                                        device_id=device, device_id_type=MESH)


def _ag_comm(wts, layer, ici_keys, fwd_keys):
    keys = tuple(k for k in BIG if k in ici_keys or k in fwd_keys)

    def geo(k):
        _, rows, cols = wts[k].shape
        return (rows, cols // N_CHIPS) if k in COL_SHARDED else (rows // N_CHIPS, cols)

    def copies(refs, sems):
        g = dict(zip(keys, refs))
        isend, irecv, dsend, drecv = sems
        x, y, c = _me()
        mine = _chip_of(x, y)

        def region(k, chip, half):
            r, cc = geo(k)
            h = r // 2
            if k in COL_SHARDED:
                return g[k].at[layer, pl.ds(pl.multiple_of(half * h, BF16_ROWS), h), pl.ds(pl.multiple_of(chip * cc, LANES), cc)]
            return g[k].at[layer, pl.ds(pl.multiple_of(chip * r + half * h, BF16_ROWS), h), :]

        def ici(k, f, landing):
            chip = _other_chips(x, y)[f]
            where = region(k, _chip_of(*chip) if landing else mine, c)
            i = keys.index(k)
            return _remote(where, where, isend.at[i, f], irecv.at[i, f], (*chip, c))

        def fwd(k, f, landing):
            chip = _other_chips(x, y)[f]
            where = region(k, _chip_of(*chip), 1 - c if landing else c)
            i = keys.index(k)
            return _remote(where, where, dsend.at[i, f], drecv.at[i, f], (x, y, 1 - c))

        return ici, fwd

    def start(ins, outs, sems):
        ici, fwd = copies(outs, sems)
        for k in keys:
            for f in range(3):
                if k in ici_keys:
                    ici(k, f, False).start()
                else:
                    fwd(k, f, False).start()

    def finish(ins, outs, sems):
        ici, fwd = copies(outs, sems)
        for k in keys:
            for f in range(3):
                if k in ici_keys:
                    ici(k, f, True).wait_recv()
                    if k in fwd_keys:
                        fwd(k, f, False).start()
        for k in keys:
            for f in range(3):
                if k in fwd_keys:
                    fwd(k, f, True).wait_recv()
                    fwd(k, f, False).wait_send()
                if k in ici_keys:
                    ici(k, f, False).wait_send()

    def done(res):
        wts.update(zip(keys, res))

    n = len(keys)
    return _Comm([wts[k] for k in keys], [_sds(wts[k].shape, BF16) for k in keys], {i: i for i in range(n)},
                 [pltpu.SemaphoreType.DMA((n, 3)) for _ in range(4)], start, finish, done)


def _small_gather_comm(slab, store):
    def copies(ins, outs, sems):
        send, recv, lsem = sems
        x, y, c = _me()
        mine = _chip_of(x, y)
        own = pltpu.make_async_copy(ins[0], outs[0].at[mine], lsem)
        pairs = []
        for f, chip in enumerate(_other_chips(x, y)):
            out = _remote(ins[0], outs[0].at[mine], send.at[f], recv.at[f], (*chip, c))
            land = _remote(ins[0], outs[0].at[_chip_of(*chip)], send.at[f], recv.at[f], (*chip, c))
            pairs.append((out, land))
        return own, pairs

    def start(ins, outs, sems):
        own, pairs = copies(ins, outs, sems)
        own.start()
        for out, _ in pairs:
            out.start()

    def finish(ins, outs, sems):
        own, pairs = copies(ins, outs, sems)
        for out, land in pairs:
            land.wait_recv()
            out.wait_send()
        own.wait()

    def done(res):
        store["small"] = res[0]

    return _Comm([slab], [_sds((N_CHIPS,) + slab.shape, F32)], {},
                 [pltpu.SemaphoreType.DMA((3,)), pltpu.SemaphoreType.DMA((3,)), pltpu.SemaphoreType.DMA], start, finish, done)


def _piece_geo(g):
    geo = {}
    for k in BIG:
        rows, cols = g[k].shape
        geo[k] = (rows // 2, cols // N_CHIPS) if k in COL_SHARDED else (rows // (2 * N_CHIPS), cols)
    return geo


def _swap_comm(g, done):
    geo = _piece_geo(g)
    n_copies = sum(N_CHIPS if k in COL_SHARDED else 1 for k in BIG)

    def copies(ins, outs, sems):
        g_refs, t_refs = dict(zip(BIG, ins)), dict(zip(BIG, outs))
        send, recv = sems
        x, y, c = _me()
        pairs = []
        for k in BIG:
            h, cc = geo[k]
            if k in COL_SHARDED:
                rows = pl.ds(pl.multiple_of((1 - c) * h, BF16_ROWS), h)
                pairs += [(g_refs[k].at[rows, pl.ds(j * cc, cc)], t_refs[k].at[j]) for j in range(N_CHIPS)]
            else:
                pairs.append((g_refs[k].at[:, 1 - c], t_refs[k]))
        return [_remote(src, dst, send.at[i], recv.at[i], (x, y, 1 - c)) for i, (src, dst) in enumerate(pairs)]

    def start(ins, outs, sems):
        for cp in copies(ins, outs, sems):
            cp.start()

    def finish(ins, outs, sems):
        for cp in copies(ins, outs, sems):
            cp.wait()

    ins = [g[k] if k in COL_SHARDED else g[k].reshape(N_CHIPS, 2, geo[k][0], geo[k][1]) for k in BIG]
    return _Comm(ins, [_sds((N_CHIPS,) + geo[k], BF16) for k in BIG], {},
                 [pltpu.SemaphoreType.DMA((n_copies,)) for _ in range(2)], start, finish,
                 lambda res: done(dict(zip(BIG, res))))


def _pair_sum(g, theirs, c_arr):
    geo = _piece_geo(g)

    def body(c_ref, *refs):
        nk = len(BIG)
        for i in range(nk):
            refs[2 * nk + i][...] = (refs[i][...].astype(F32) + refs[nk + i][...].astype(F32)).astype(BF16)

    in_specs, ins = [], []
    for k in BIG:
        h, cc = geo[k]
        if k in COL_SHARDED:
            in_specs.append(pl.BlockSpec((h, cc), lambda j, c_ref: (c_ref[0], j)))
            ins.append(g[k])
        else:
            in_specs.append(pl.BlockSpec((None, h, cc), lambda j, c_ref: (2 * j + c_ref[0], 0, 0)))
            ins.append(g[k].reshape(2 * N_CHIPS, h, cc))
    slab = [pl.BlockSpec((None,) + geo[k], lambda j, c_ref: (j, 0, 0)) for k in BIG]
    res = _pcall(body, name="rs_pair_sum", out_shape=tuple(_sds((N_CHIPS,) + geo[k], BF16) for k in BIG), grid=(N_CHIPS,),
                 in_specs=in_specs + slab, out_specs=tuple(slab), prefetch=1)(c_arr, *ins, *[theirs[k] for k in BIG])
    return dict(zip(BIG, res))


def _rs_comm(p, keys, store):
    def copies(ins, outs, sems):
        send, recv = sems
        x, y, c = _me()
        return [_remote(ins[i].at[_chip_of(*chip)], outs[i].at[f], send.at[i, f], recv.at[i, f], (*chip, c))
                for i in range(len(keys)) for f, chip in enumerate(_other_chips(x, y))]

    def start(ins, outs, sems):
        for cp in copies(ins, outs, sems):
            cp.start()

    def finish(ins, outs, sems):
        for cp in copies(ins, outs, sems):
            cp.wait()

    def done(res):
        store.update(zip(keys, res))

    return _Comm([p[k] for k in keys], [_sds((3,) + p[k].shape[1:], BF16) for k in keys], {},
                 [pltpu.SemaphoreType.DMA((len(keys), 3)) for _ in range(2)], start, finish, done)


def _quad_sum(p, b, where, l, full):
    parts = 2
    nk = len(BIG)

    def body(where_ref, *refs):
        for i in range(nk):
            acc = refs[i][...].astype(F32)
            for f in range(3):
                acc = acc + refs[nk + 3 * i + f][...].astype(F32)
            refs[5 * nk + i][...] = acc

    own, recv, outs = [], [], []
    for k in BIG:
        h, cc = p[k].shape[1:]
        th = h // parts
        own.append(pl.BlockSpec((None, th, cc), lambda i, w_ref: (w_ref[0], i, 0)))
        recv += [pl.BlockSpec((None, th, cc), lambda i, w_ref, f=f: (f, i, 0)) for f in range(3)]
        outs.append(pl.BlockSpec((None, None, th, cc), lambda i, w_ref: (l, w_ref[1], i, 0)))
    args = [p[k] for k in BIG] + [b[k] for k in BIG for _ in range(3)] + [full[k] for k in BIG]
    res = _pcall(body, name="rs_quad_sum", out_shape=tuple(_sds(full[k].shape, F32) for k in BIG), grid=(parts,),
                 in_specs=own + recv + [ANY] * nk, out_specs=tuple(outs), prefetch=1,
                 aliases={1 + 4 * nk + i: i for i in range(nk)})(where, *args)
    return dict(zip(BIG, res))


def _share_comm(layers, full, done):
    nk = len(BIG)

    def copies(outs, sems, landing):
        send, recv = sems
        x, y, c = _me()
        half = 1 - c if landing else c
        return [_remote(outs[i].at[l, half], outs[i].at[l, half], send.at[i, j], recv.at[i, j], (x, y, 1 - c))
                for i in range(nk) for j, l in enumerate(layers)]

    def start(ins, outs, sems):
        for cp in copies(outs, sems, False):
            cp.start()

    def finish(ins, outs, sems):
        for cp in copies(outs, sems, True):
            cp.wait_recv()
        for cp in copies(outs, sems, False):
            cp.wait_send()

    return _Comm([full[k] for k in BIG], [_sds(full[k].shape, F32) for k in BIG], {i: i for i in range(nk)},
                 [pltpu.SemaphoreType.DMA((nk, len(layers))) for _ in range(2)], start, finish,
                 lambda res: done(dict(zip(BIG, res))))


def _gather_comm(slab, done):
    def copies(ins, outs, sems, landing):
        send, recv = sems
        x, y, c = _me()
        me = 4 * x + 2 * y + c
        out = []
        for mask in range(1, N_DEV):
            peer = (x ^ (mask >> 2), y ^ ((mask >> 1) & 1), c ^ (mask & 1))
            slot = 4 * peer[0] + 2 * peer[1] + peer[2] if landing else me
            out.append(_remote(ins[0], outs[0].at[slot], send.at[mask - 1], recv.at[mask - 1], peer))
        return out

    def start(ins, outs, sems):
        for cp in copies(ins, outs, sems, False):
            cp.start()

    def finish(ins, outs, sems):
        for cp in copies(ins, outs, sems, True):
            cp.wait_recv()
        for cp in copies(ins, outs, sems, False):
            cp.wait_send()

    return _Comm([slab], [_sds((N_DEV,) + slab.shape, F32)], {},
                 [pltpu.SemaphoreType.DMA((N_DEV - 1,)), pltpu.SemaphoreType.DMA((N_DEV - 1,))], start, finish,
                 lambda res: done(res[0]))


def _sum_slabs(slabs, own, me):
    n, r, lanes = slabs.shape
    tr = r // 2

    def body(me_ref, s_ref, own_ref, o_ref):
        o_ref[...] = jnp.zeros((tr, lanes), F32)
        for i in range(n):
            @pl.when(me_ref[0] == i)
            def _():
                o_ref[...] += own_ref[...]

            @pl.when(me_ref[0] != i)
            def _():
                o_ref[...] += s_ref[i]

    return _pcall(body, name="sum_partials", out_shape=_sds((r, lanes), F32), grid=(2,),
                  in_specs=[pl.BlockSpec((n, tr, lanes), lambda i, me_ref: (0, i, 0)),
                            pl.BlockSpec((tr, lanes), lambda i, me_ref: (i, 0))],
                  out_specs=pl.BlockSpec((tr, lanes), lambda i, me_ref: (i, 0)), prefetch=1)(me, slabs, own)


def _cast_into_gathered(w, chip, by_cols, name):
    l, r, c = w.shape

    def body(chip_ref, w_ref, o_ref):
        o_ref[...] = w_ref[...].astype(BF16)

    if by_cols:
        shape, out = (l, r, N_CHIPS * c), pl.BlockSpec((None, r, c), lambda i, chip_ref: (i, 0, chip_ref[0]))
    else:
        shape, out = (l, N_CHIPS * r, c), pl.BlockSpec((None, r, c), lambda i, chip_ref: (i, chip_ref[0], 0))
    return _pcall(body, name=name, out_shape=_sds(shape, BF16), grid=(l,),
                  in_specs=[pl.BlockSpec((None, r, c), lambda i, chip_ref: (i, 0, 0))], out_specs=out, prefetch=1)(chip, w)


def _adamw(w, g, m, v, name, tr, comm=None):
    r, c = w.shape

    def body(w_ref, g_ref, m_ref, v_ref, d_ref, mo_ref, vo_ref):
        gv = g_ref[...]
        mn = ADAM_B1 * m_ref[...] + (1.0 - ADAM_B1) * gv
        vn = ADAM_B2 * v_ref[...] + (1.0 - ADAM_B2) * (gv * gv)
        m_hat = mn / (1.0 - ADAM_B1 ** ADAM_STEP)
        v_hat = vn / (1.0 - ADAM_B2 ** ADAM_STEP)
        d_ref[...] = -ADAM_LR * (m_hat / (jnp.sqrt(v_hat) + ADAM_EPS) + ADAM_WD * w_ref[...])
        mo_ref[...] = mn
        vo_ref[...] = vn

    blk = pl.BlockSpec((tr, c), lambda i: (i, 0))
    return _pcall(body, name=name, out_shape=tuple(_sds((r, c), F32) for _ in range(3)), grid=(r // tr,),
                  in_specs=[blk] * 4, out_specs=(blk, blk, blk), comm=comm)(w, g, m, v)


AG_RIDES = {"in_proj": (0, ("out",), ()), "attn_fwd": (0, ("up",), ("out",)), "out_proj": (0, (), ("up",)),
            "up_proj": (1, ("in_t",), ()), "ffn_act_fwd": (1, ("down",), ()), "down_proj": (1, (), ("in_t",)),
            "rms_mix_fwd": (0, (), ("down",))}
AG_FIRST = ("in_t", "down")
RS_RIDES = {"ffn_act_bwd": ("up",), "attn_bwd": ("in_t", "out", "down")}
SWAP_RIDE = "down_proj_dx"
SHARE_RIDE = "up_proj_dx_gate"
GATHER_RIDE = "w_up"


class _Rides:
    def __init__(self, table, build):
        self.table, self.build = table, build

    def get(self, name):
        return self.build(self.table[name]) if name in self.table else None


class _MeshSchedule:
    def __init__(self, wts, depth, c_arr, where):
        self.wts, self.depth, self.c_arr, self.where = wts, depth, c_arr, where
        self.grads, self.pairs, self.recv, self.full, self.unshared = None, None, {}, None, []

    def fwd_comms(self, l):
        table = {}
        for name, (off, ici, fwd) in AG_RIDES.items():
            if l + off == 0:
                ici, fwd = (tuple(k for k in keys if k not in AG_FIRST) for keys in (ici, fwd))
            if l + off < self.depth and (ici or fwd):
                table[name] = (l + off, ici, fwd)
        return _Rides(table, lambda ride: _ag_comm(self.wts, *ride))

    def _swapped(self, theirs):
        self.pairs = _pair_sum(self.grads, theirs, self.c_arr)

    def _shared(self, full):
        self.full, self.unshared = full, []

    def bwd_comms(self, l):
        if self.grads is None:
            return {}
        table = dict(RS_RIDES)
        table[SWAP_RIDE] = "swap"
        if self.unshared:
            table[SHARE_RIDE] = "share"
        return _Rides(table, lambda what: _swap_comm(self.grads, self._swapped) if what == "swap"
                      else _share_comm(self.unshared, self.full, self._shared) if what == "share"
                      else _rs_comm(self.pairs, what, self.recv))

    def _reduce(self, l):
        self.full = _quad_sum(self.pairs, self.recv, self.where, l, self.full)
        self.unshared = self.unshared + [l]
        self.grads, self.pairs, self.recv = None, None, {}

    def after_bwd(self, l, grads):
        if self.grads is not None:
            self._reduce(l + 1)
        if self.full is None:
            geo = _piece_geo(grads)
            self.full = {k: jnp.zeros((self.depth, 2) + geo[k], F32) for k in BIG}
        self.grads = grads
        if l == 0:
            _run_comm(_swap_comm(grads, self._swapped), "rs_swap_halves")
            _run_comm(_rs_comm(self.pairs, BIG, self.recv), "rs_to_owners")
            self._reduce(0)
            _run_comm(_share_comm(self.unshared, self.full, self._shared), "rs_share")


SHARDED_SMALL = ("conv_a_w", "conv_c_w", "conv_f_w")
SMALL = ("norm_mix_g", "conv_a_w", "conv_c_w", "conv_c_b", "ln_c_g", "ln_c_b", "out_norm_g", "norm_ffn_g",
         "conv_f_w", "rel_bias", "final_g")
SLAB_ROWS = 16


def _pack(arrays):
    flat = jnp.concatenate([a.reshape(-1) for a in arrays])
    unit = SLAB_ROWS * LANES
    total = -(-flat.shape[0] // unit) * unit
    return jnp.pad(flat, (0, total - flat.shape[0])).reshape(-1, LANES)


def _unpack(slab, shapes):
    flat = slab.reshape(-1)
    out, off = [], 0
    for shp in shapes:
        size = math.prod(shp)
        out.append(flat[off:off + size].reshape(shp))
        off += size
    return out


def kernel(x, norm_mix_g, w_in, conv_a_w, conv_c_w, conv_c_b, ln_c_g, ln_c_b, out_norm_g, w_out, norm_ffn_g, w_up, conv_f_w, w_down, rel_bias, final_g, loss_target, m_norm_mix_g, m_w_in, m_conv_a_w, m_conv_c_w, m_conv_c_b, m_ln_c_g, m_ln_c_b, m_out_norm_g, m_w_out, m_norm_ffn_g, m_w_up, m_conv_f_w, m_w_down, m_rel_bias, m_final_g, v_norm_mix_g, v_w_in, v_conv_a_w, v_conv_c_w, v_conv_c_b, v_ln_c_g, v_ln_c_b, v_out_norm_g, v_w_out, v_norm_ffn_g, v_w_up, v_conv_f_w, v_w_down, v_rel_bias, v_final_g):
    weights = dict(norm_mix_g=norm_mix_g, w_in=w_in, conv_a_w=conv_a_w, conv_c_w=conv_c_w, conv_c_b=conv_c_b,
                   ln_c_g=ln_c_g, ln_c_b=ln_c_b, out_norm_g=out_norm_g, w_out=w_out, norm_ffn_g=norm_ffn_g, w_up=w_up,
                   conv_f_w=conv_f_w, w_down=w_down, rel_bias=rel_bias, final_g=final_g)
    mom_m = dict(norm_mix_g=m_norm_mix_g, w_in=m_w_in, conv_a_w=m_conv_a_w, conv_c_w=m_conv_c_w, conv_c_b=m_conv_c_b,
                 ln_c_g=m_ln_c_g, ln_c_b=m_ln_c_b, out_norm_g=m_out_norm_g, w_out=m_w_out, norm_ffn_g=m_norm_ffn_g,
                 w_up=m_w_up, conv_f_w=m_conv_f_w, w_down=m_w_down, rel_bias=m_rel_bias, final_g=m_final_g)
    mom_v = dict(norm_mix_g=v_norm_mix_g, w_in=v_w_in, conv_a_w=v_conv_a_w, conv_c_w=v_conv_c_w, conv_c_b=v_conv_c_b,
                 ln_c_g=v_ln_c_g, ln_c_b=v_ln_c_b, out_norm_g=v_out_norm_g, w_out=v_w_out, norm_ffn_g=v_norm_ffn_g,
                 w_up=v_w_up, conv_f_w=v_conv_f_w, w_down=v_w_down, rel_bias=v_rel_bias, final_g=v_final_g)
    xi, yi, ci = _me()
    chip = _chip_of(xi, yi)
    c_arr = jnp.reshape(ci, (1,)).astype(I32)
    chip_arr = jnp.reshape(chip, (1,)).astype(I32)
    me_arr = jnp.reshape(4 * xi + 2 * yi + ci, (1,)).astype(I32)
    where = jnp.stack([chip, ci]).astype(I32)
    depth = w_out.shape[0]

    wts = {"in_t": _cast_into_gathered(jnp.swapaxes(w_in, 1, 2), chip_arr, False, "cast_in"),
           "out": _cast_into_gathered(w_out, chip_arr, False, "cast_out"),
           "up": _cast_into_gathered(w_up, chip_arr, True, "cast_up"),
           "down": _cast_into_gathered(w_down, chip_arr, False, "cast_down")}
    store = {}
    _run_comm(_small_gather_comm(_pack([weights[n] for n in SHARDED_SMALL]), store), "ag_small")
    _run_comm(_ag_comm(wts, 0, AG_FIRST, AG_FIRST), "ag_weights")
    prm = {n: weights[n] for n in SMALL if n not in SHARDED_SMALL}
    per_chip = [_unpack(store["small"][j], [weights[n].shape for n in SHARDED_SMALL]) for j in range(N_CHIPS)]
    for i, n in enumerate(SHARDED_SMALL):
        prm[n] = jnp.concatenate([per_chip[j][i] for j in range(N_CHIPS)], axis=-1)

    sched = _MeshSchedule(wts, depth, c_arr, where)
    loss_row, dx, small, d_rel, d_final = _local_step(x[0], loss_target[0], wts, prm, sched)
    reduced = sched.full
    loss = lax.psum(loss_row[0, 0], ("x", "y", "c"))

    grads = {}
    shard_shapes = {"in_t": jnp.swapaxes(w_in, 1, 2).shape, "out": w_out.shape, "up": w_up.shape, "down": w_down.shape}
    red = {k: reduced[k].reshape(shard_shapes[k]) for k in BIG}
    grads["w_in"] = jnp.swapaxes(red["in_t"], 1, 2)
    grads["w_out"], grads["w_up"], grads["w_down"] = red["out"], red["up"], red["down"]

    stacked = {n: jnp.stack([small[l][n] for l in range(depth)]) for n in small[0]}
    stacked["rel_bias"] = d_rel
    stacked["final_g"] = d_final
    full_shapes = [stacked[n].shape for n in SMALL]
    partial = _pack([stacked[n] for n in SMALL])
    delta, new_m, new_v = {}, {}, {}
    for n in ("w_in", "w_out", "w_up", "w_down"):
        shp = weights[n].shape
        flat = lambda a, shp=shp: a.reshape(shp[0] * shp[1], shp[2])
        tile = max(t for t in range(8, 257, 8) if shp[1] % t == 0)
        comm = _gather_comm(partial, lambda res: store.update(partials=res)) if n == GATHER_RIDE else None
        d, mn, vn = _adamw(flat(weights[n]), flat(grads[n]), flat(mom_m[n]), flat(mom_v[n]), "adamw_" + n, tile, comm)
        delta[n], new_m[n], new_v[n] = d.reshape(shp), mn.reshape(shp), vn.reshape(shp)

    summed = _unpack(_sum_slabs(store["partials"], partial, me_arr), full_shapes)
    for n, g in zip(SMALL, summed):
        if n in SHARDED_SMALL:
            width = weights[n].shape[-1]
            g = lax.dynamic_slice_in_dim(g, chip * width, width, axis=g.ndim - 1)
        grads[n] = g
    shapes = [weights[n].shape for n in SMALL]
    packed = [_pack([src[n] for n in SMALL]) for src in (weights, grads, mom_m, mom_v)]
    d, mn, vn = _adamw(*packed, "adamw_small", packed[0].shape[0] // 2)
    for n, a, b, c in zip(SMALL, _unpack(d, shapes), _unpack(mn, shapes), _unpack(vn, shapes)):
        delta[n], new_m[n], new_v[n] = a, b, c

    order = ("norm_mix_g", "w_in", "conv_a_w", "conv_c_w", "conv_c_b", "ln_c_g", "ln_c_b", "out_norm_g", "w_out",
             "norm_ffn_g", "w_up", "conv_f_w", "w_down", "rel_bias", "final_g")
    return (loss, dx[None], *[grads[n] for n in order], *[delta[n] for n in order], *[new_m[n] for n in order],
            *[new_v[n] for n in order])
```

```python
import functools
import math

import numpy as np
import jax
import jax.numpy as jnp
from jax import lax
from jax.experimental import pallas as pl
from jax.experimental.pallas import tpu as pltpu

F32 = jnp.float32
BF16 = jnp.bfloat16
I32 = jnp.int32

EPS = 1e-6
NEG = -1e30
D_HEAD = 64
LANES = 128
BLK = 128
ATTN_GROUP_FWD = 4
ATTN_GROUP_BWD = 4
DILATED_BRANCHES = ((128, 1), (512, 4), (2048, 16))
NUM_BUCKETS = 32
MAX_DISTANCE = 2048
SHORT_CONV = 3
CONFORMER_CONV = 31
FFN_CONV = 3
PAD_SHORT = 8
PAD_LONG = 32
ROW_CHUNK = 256
V7X_VMEM_BYTES = 64 * 1024 * 1024
VMEM_REQUEST = V7X_VMEM_BYTES * 7 // 8

ADAM_LR = 0.001
ADAM_B1 = 0.9
ADAM_B2 = 0.999
ADAM_EPS = 1e-08
ADAM_WD = 0.01
ADAM_STEP = 10

MESH = pl.DeviceIdType.MESH
ANY = pl.BlockSpec(memory_space=pl.ANY)


def _sds(shape, dtype):
    return jax.ShapeDtypeStruct(tuple(shape), dtype)


class _Comm:
    def __init__(self, ins, out_shapes, aliases, sems, start, finish, done):
        self.ins, self.out_shapes, self.aliases, self.sems = list(ins), list(out_shapes), dict(aliases), list(sems)
        self.start, self.finish, self.done = start, finish, done


def _pcall(body, *, name, out_shape, grid=(), in_specs=None, out_specs=None, scratch_shapes=(), vmem=VMEM_REQUEST,
           aliases=None, prefetch=0, comm=None):
    params = pltpu.CompilerParams(dimension_semantics=("arbitrary",) * len(grid), vmem_limit_bytes=vmem)
    single = not isinstance(out_shape, (tuple, list))
    outs = [out_shape] if single else list(out_shape)
    ospecs = [out_specs] if single else list(out_specs)
    ispecs, scratch, aliases = list(in_specs), list(scratch_shapes), dict(aliases or {})
    n_in, n_out, n_scr = len(ispecs), len(outs), len(scratch)
    kernel_body = body
    if comm is not None:
        n_ci, n_co = len(comm.ins), len(comm.out_shapes)

        def kernel_body(*refs):
            pre, rest = refs[:prefetch], refs[prefetch:]
            core_in, c_in = rest[:n_in], rest[n_in:n_in + n_ci]
            o0 = n_in + n_ci
            core_out, c_out = rest[o0:o0 + n_out], rest[o0 + n_out:o0 + n_out + n_co]
            s0 = o0 + n_out + n_co
            core_scr, c_sem = rest[s0:s0 + n_scr], rest[s0 + n_scr:]
            first = functools.reduce(jnp.logical_and, [pl.program_id(a) == 0 for a in range(len(grid))])
            last = functools.reduce(jnp.logical_and, [pl.program_id(a) == grid[a] - 1 for a in range(len(grid))])
            pl.when(first)(lambda: comm.start(c_in, c_out, c_sem))
            body(*pre, *core_in, *core_out, *core_scr)
            pl.when(last)(lambda: comm.finish(c_in, c_out, c_sem))

        for i, o in comm.aliases.items():
            aliases[prefetch + n_in + i] = n_out + o
        ispecs += [ANY] * n_ci
        ospecs += [ANY] * n_co
        outs += comm.out_shapes
        scratch += comm.sems
    if prefetch:
        spec = pltpu.PrefetchScalarGridSpec(num_scalar_prefetch=prefetch, grid=grid, in_specs=ispecs,
                                            out_specs=tuple(ospecs), scratch_shapes=scratch)
        call = pl.pallas_call(kernel_body, name=name, out_shape=tuple(outs), grid_spec=spec,
                              input_output_aliases=aliases, compiler_params=params)
    else:
        call = pl.pallas_call(kernel_body, name=name, out_shape=tuple(outs), grid=grid, in_specs=ispecs,
                              out_specs=tuple(ospecs), scratch_shapes=scratch, input_output_aliases=aliases,
                              compiler_params=params)

    def run(*args):
        res = call(*args, *(comm.ins if comm is not None else ()))
        if comm is not None:
            comm.done(res[n_out:])
        return res[0] if single else tuple(res[:n_out])

    return run


def _both(a, b):
    def split(refs, na):
        return refs[:na], refs[na:]

    def run(which):
        def go(ins, outs, sems):
            for comm, i, o, s in zip((a, b), split(ins, len(a.ins)), split(outs, len(a.out_shapes)), split(sems, len(a.sems))):
                getattr(comm, which)(i, o, s)
        return go

    def done(res):
        a.done(res[:len(a.out_shapes)])
        b.done(res[len(a.out_shapes):])

    aliases = dict(a.aliases)
    aliases.update({len(a.ins) + i: len(a.out_shapes) + o for i, o in b.aliases.items()})
    return _Comm(a.ins + b.ins, a.out_shapes + b.out_shapes, aliases, a.sems + b.sems, run("start"), run("finish"), done)


def _run_comm(comm, name):
    def body(*refs):
        n_ci, n_co = len(comm.ins), len(comm.out_shapes)
        c_in, c_out, c_sem = refs[:n_ci], refs[n_ci:n_ci + n_co], refs[n_ci + n_co:]
        comm.start(c_in, c_out, c_sem)
        comm.finish(c_in, c_out, c_sem)

    res = pl.pallas_call(body, name=name, out_shape=tuple(comm.out_shapes), in_specs=[ANY] * len(comm.ins),
                         out_specs=tuple([ANY] * len(comm.out_shapes)), scratch_shapes=comm.sems,
                         input_output_aliases=comm.aliases)(*comm.ins)
    comm.done(res)


def _dot(a, b):
    return lax.dot_general(a, b, (((1,), (0,)), ((), ())), preferred_element_type=F32)


def _dot_nt(a, b):
    return lax.dot_general(a, b, (((1,), (1,)), ((), ())), preferred_element_type=F32)


def _dot_tn(a, b):
    return lax.dot_general(a, b, (((0,), (0,)), ((), ())), preferred_element_type=F32)


def _sigmoid(x):
    return 1.0 / (1.0 + jnp.exp(-x))


def _rstd(x):
    return lax.rsqrt(jnp.mean(x * x, axis=-1, keepdims=True) + EPS)


def _rms_fwd(x, g, name, comm=None):
    s, d = x.shape
    tm = ROW_CHUNK

    def body(x_ref, g_ref, o_ref):
        xv = x_ref[...]
        o_ref[...] = (xv * _rstd(xv) * g_ref[...]).astype(BF16)

    return _pcall(body, name=name, out_shape=_sds((s, d), BF16), grid=(s // tm,),
                  in_specs=[pl.BlockSpec((tm, d), lambda i: (i, 0)), pl.BlockSpec((1, d), lambda i: (0, 0))],
                  out_specs=pl.BlockSpec((tm, d), lambda i: (i, 0)), comm=comm)(x, g)


def _rms_bwd(x, g, dh, dres, name):
    s, d = x.shape
    tm = ROW_CHUNK

    def body(x_ref, g_ref, dh_ref, dres_ref, dx_ref, dxb_ref, dg_ref):
        i = pl.program_id(0)
        xv = x_ref[...]
        r = _rstd(xv)
        xh = xv * r
        dhv = dh_ref[...]
        gd = dhv * g_ref[...]
        dx = dres_ref[...] + r * (gd - xh * jnp.mean(gd * xh, axis=-1, keepdims=True))
        dx_ref[...] = dx
        dxb_ref[...] = dx.astype(BF16)
        part = jnp.sum(dhv * xh, axis=0, keepdims=True)

        @pl.when(i == 0)
        def _():
            dg_ref[...] = part

        @pl.when(i > 0)
        def _():
            dg_ref[...] += part

    row = pl.BlockSpec((tm, d), lambda i: (i, 0))
    vec = pl.BlockSpec((1, d), lambda i: (0, 0))
    return _pcall(body, name=name, out_shape=(_sds((s, d), F32), _sds((s, d), BF16), _sds((1, d), F32)),
                  grid=(s // tm,), in_specs=[row, vec, row, row], out_specs=(row, row, vec))(x, g, dh, dres)


def _final_loss(x, g, tgt, name):
    s, d = x.shape
    tm = ROW_CHUNK

    def body(x_ref, g_ref, t_ref, loss_ref, dx_ref, dxb_ref, dg_ref):
        i = pl.program_id(0)
        xv = x_ref[...]
        r = _rstd(xv)
        xh = xv * r
        e = xh * g_ref[...] - t_ref[...]
        lpart = 0.5 * jnp.sum(jnp.mean(e * e, axis=-1, keepdims=True), axis=0, keepdims=True)
        dy = e * (1.0 / d)
        gd = dy * g_ref[...]
        dx = r * (gd - xh * jnp.mean(gd * xh, axis=-1, keepdims=True))
        dx_ref[...] = dx
        dxb_ref[...] = dx.astype(BF16)
        part = jnp.sum(dy * xh, axis=0, keepdims=True)
        lrow = jnp.broadcast_to(lpart, (1, LANES))

        @pl.when(i == 0)
        def _():
            dg_ref[...] = part
            loss_ref[...] = lrow

        @pl.when(i > 0)
        def _():
            dg_ref[...] += part
            loss_ref[...] += lrow

    row = pl.BlockSpec((tm, d), lambda i: (i, 0))
    vec = pl.BlockSpec((1, d), lambda i: (0, 0))
    return _pcall(body, name=name,
                  out_shape=(_sds((1, LANES), F32), _sds((s, d), F32), _sds((s, d), BF16), _sds((1, d), F32)),
                  grid=(s // tm,), in_specs=[row, vec, row],
                  out_specs=(pl.BlockSpec((1, LANES), lambda i: (0, 0)), row, row, vec))(x, g, tgt)


def _mm_n(a, b, layer, *, nt, tn, out_dtype, name, resid=None, b_part=0, comm=None):
    s, k = a.shape
    n = b.shape[1] if nt else b.shape[2]
    rows = 512

    def body(a_ref, b_ref, *refs):
        o_ref = refs[-1]
        bv = b_ref[...]
        for r0 in range(0, s, rows):
            av = a_ref[r0:r0 + rows, :]
            prod = _dot_nt(av, bv) if nt else _dot(av, bv)
            if resid is not None:
                prod = refs[0][r0:r0 + rows, :] + prod
            o_ref[r0:r0 + rows, :] = prod.astype(out_dtype)

    b_spec = (pl.BlockSpec((None, tn, k), lambda j: (layer, j, b_part)) if nt
              else pl.BlockSpec((None, k, tn), lambda j: (layer, b_part, j)))
    col = pl.BlockSpec((s, tn), lambda j: (0, j))
    extra = () if resid is None else (resid,)
    return _pcall(body, name=name, out_shape=_sds((s, n), out_dtype), grid=(n // tn,),
                  in_specs=[pl.BlockSpec((s, k), lambda j: (0, 0)), b_spec] + [col] * len(extra),
                  out_specs=col, comm=comm)(a, b, *extra)


def _mm_tn(a, b, *, t, name):
    s, ka = a.shape
    n = b.shape[1]

    def body(a_ref, b_ref, o_ref):
        o_ref[...] = _dot_tn(a_ref[...], b_ref[...]).astype(BF16)

    return _pcall(body, name=name, out_shape=_sds((ka, n), BF16), grid=(ka // t,),
                  in_specs=[pl.BlockSpec((s, t), lambda i: (0, i)), pl.BlockSpec((s, n), lambda i: (0, 0))],
                  out_specs=pl.BlockSpec((t, n), lambda i: (i, 0)))(a, b)


def _mm_tn2(a, b_lo, b_hi, *, t, name):
    s, ka = a.shape
    half = b_lo.shape[1]
    nb = half // t

    def body(a_ref, lo_ref, hi_ref, o_ref):
        j = pl.program_id(0)

        @pl.when(j < nb)
        def _():
            o_ref[...] = _dot_tn(a_ref[...], lo_ref[...]).astype(BF16)

        @pl.when(j >= nb)
        def _():
            o_ref[...] = _dot_tn(a_ref[...], hi_ref[...]).astype(BF16)

    return _pcall(body, name=name, out_shape=_sds((ka, 2 * half), BF16), grid=(2 * nb,),
                  in_specs=[pl.BlockSpec((s, ka), lambda j: (0, 0)),
                            pl.BlockSpec((s, t), lambda j: (0, jnp.minimum(j, nb - 1))),
                            pl.BlockSpec((s, t), lambda j: (0, jnp.maximum(j - nb, 0)))],
                  out_specs=pl.BlockSpec((ka, t), lambda j: (0, j)))(a, b_lo, b_hi)


SUBLANES = 8


def _tap_windows(win, width, lead, rows):
    offs = [lead + k for k in range(width)]
    if width <= SUBLANES:
        return [win[o:o + rows, :] for o in offs]
    n = win.shape[0]
    out = {}
    for r in sorted({o % SUBLANES for o in offs}):
        base = win if r == 0 else pltpu.roll(win, n - r, axis=0)
        for o in offs:
            if o % SUBLANES == r:
                out[o - lead] = base[o - r:o - r + rows, :]
    return [out[k] for k in range(width)]


def _conv_taps(taps, w_ref):
    acc = None
    for k, tap in enumerate(taps):
        term = w_ref[pl.ds(k, 1), :] * tap
        acc = term if acc is None else acc + term
    return acc


def _causal_taps(win, width, pad, rows):
    return _tap_windows(win, width, pad - (width - 1), rows)


def _anticausal_taps(win, width, rows):
    return _tap_windows(win, width, 0, rows)[::-1]


def _conv_wgrad(dw_ref, g, taps):
    for k, tap in enumerate(taps):
        dw_ref[pl.ds(k, 1), :] += jnp.sum(g * tap, axis=0, keepdims=True)


def _mixer_a_fwd(ab, taps_t, wa_ref):
    ct = _conv_taps(taps_t, wa_ref)
    return ab * ct, ct


def _mixer_c_fwd(taps_u, wc_ref, cb_ref, lg_ref, lb_ref):
    u = _conv_taps(taps_u, wc_ref) + cb_ref[...]
    mu = jnp.mean(u, axis=-1, keepdims=True)
    uc = u - mu
    rs = lax.rsqrt(jnp.mean(uc * uc, axis=-1, keepdims=True) + EPS)
    uh = uc * rs
    ln = uh * lg_ref[...] + lb_ref[...]
    sg = _sigmoid(ln)
    return ln * sg, ln, sg, uh, rs


def _mix_fwd(z, wa, wc, cb, lg, lb, ga, gc, name):
    s = z.shape[0]
    w = wa.shape[1]
    nblk = z.shape[1] // w
    rc = ROW_CHUNK

    def body(ah_ref, ab_ref, ac_ref, cv_ref, cg_ref, wa_ref, wc_ref, cb_ref, lg_ref, lb_ref, ga_ref, gc_ref,
             ya_ref, yc_ref, tpad, upad):
        tpad[pl.ds(0, PAD_SHORT), :] = jnp.zeros((PAD_SHORT, w), F32)
        upad[pl.ds(0, PAD_LONG), :] = jnp.zeros((PAD_LONG, w), F32)

        def chunk(i, carry):
            base = pl.multiple_of(i * rc, rc)
            rows = pl.ds(base, rc)
            ah, ab, ac = ah_ref[rows, :], ab_ref[rows, :], ac_ref[rows, :]
            tpad[pl.ds(base + PAD_SHORT, rc), :] = ac * ah
            ya, _ = _mixer_a_fwd(ab, _causal_taps(tpad[pl.ds(base, rc + PAD_SHORT), :], SHORT_CONV, PAD_SHORT, rc), wa_ref)
            ya_ref[rows, :] = (ya * _rstd(ya) * ga_ref[...]).astype(BF16)
            upad[pl.ds(base + PAD_LONG, rc), :] = cv_ref[rows, :] * _sigmoid(cg_ref[rows, :])
            taps_u = _causal_taps(upad[pl.ds(base, rc + PAD_LONG), :], CONFORMER_CONV, PAD_LONG, rc)
            yc = _mixer_c_fwd(taps_u, wc_ref, cb_ref, lg_ref, lb_ref)[0]
            yc_ref[rows, :] = (yc * _rstd(yc) * gc_ref[...]).astype(BF16)
            return carry

        lax.fori_loop(0, s // rc, chunk, 0)

    def zblk(j):
        return pl.BlockSpec((s, w), lambda i: (0, j))

    def whole(a):
        return pl.BlockSpec(a.shape, lambda i: (0, 0))

    return _pcall(
        body, name=name, out_shape=(_sds((s, w), BF16), _sds((s, w), BF16)), grid=(1,),
        in_specs=[zblk(0), zblk(1), zblk(2), zblk(nblk - 2), zblk(nblk - 1)] + [whole(a) for a in (wa, wc, cb, lg, lb, ga, gc)],
        out_specs=(pl.BlockSpec((s, w), lambda i: (0, 0)), pl.BlockSpec((s, w), lambda i: (0, 0))),
        scratch_shapes=[pltpu.VMEM((s + PAD_SHORT, w), F32), pltpu.VMEM((s + PAD_LONG, w), F32)],
    )(z, z, z, z, z, wa, wc, cb, lg, lb, ga, gc)


def _mix_bwd(z, dy, wa, wc, cb, lg, lb, ga, gc, name):
    s = z.shape[0]
    w = wa.shape[1]
    nblk = z.shape[1] // w
    nyb = dy.shape[1] // w
    rc = ROW_CHUNK

    def body(ah_ref, ab_ref, ac_ref, cv_ref, cg_ref, dya_ref, dyc_ref,
             wa_ref, wc_ref, cb_ref, lg_ref, lb_ref, ga_ref, gc_ref,
             dza_ref, dzc_ref, dwa_ref, dwc_ref, dcb_ref, dlg_ref, dlb_ref, dga_ref, dgc_ref,
             tpad, upad, dctp, dup):
        tpad[pl.ds(0, PAD_SHORT), :] = jnp.zeros((PAD_SHORT, w), F32)
        upad[pl.ds(0, PAD_LONG), :] = jnp.zeros((PAD_LONG, w), F32)
        dctp[pl.ds(s, PAD_SHORT), :] = jnp.zeros((PAD_SHORT, w), F32)
        dup[pl.ds(s, PAD_LONG), :] = jnp.zeros((PAD_LONG, w), F32)
        for ref in (dwa_ref, dwc_ref, dcb_ref, dlg_ref, dlb_ref, dga_ref, dgc_ref):
            ref[...] = jnp.zeros(ref.shape, F32)

        def rms_bwd(y, g_ref, dyn, dg_ref):
            r = _rstd(y)
            yh = y * r
            gd = dyn * g_ref[...]
            dg_ref[...] += jnp.sum(dyn * yh, axis=0, keepdims=True)
            return r * (gd - yh * jnp.mean(gd * yh, axis=-1, keepdims=True))

        def first(i, carry):
            base = pl.multiple_of(i * rc, rc)
            rows = pl.ds(base, rc)
            ah, ab, ac = ah_ref[rows, :], ab_ref[rows, :], ac_ref[rows, :]
            tpad[pl.ds(base + PAD_SHORT, rc), :] = ac * ah
            taps_t = _causal_taps(tpad[pl.ds(base, rc + PAD_SHORT), :], SHORT_CONV, PAD_SHORT, rc)
            ya, ct = _mixer_a_fwd(ab, taps_t, wa_ref)
            dya = rms_bwd(ya, ga_ref, dya_ref[rows, :], dga_ref)
            dza_ref[rows, w:2 * w] = (dya * ct).astype(BF16)
            dct = dya * ab
            dctp[rows, :] = dct
            _conv_wgrad(dwa_ref, dct, taps_t)

            upad[pl.ds(base + PAD_LONG, rc), :] = cv_ref[rows, :] * _sigmoid(cg_ref[rows, :])
            taps_u = _causal_taps(upad[pl.ds(base, rc + PAD_LONG), :], CONFORMER_CONV, PAD_LONG, rc)
            yc, ln, sg, uh, rs = _mixer_c_fwd(taps_u, wc_ref, cb_ref, lg_ref, lb_ref)
            dyc = rms_bwd(yc, gc_ref, dyc_ref[rows, :], dgc_ref)
            dln = dyc * (sg * (1.0 + ln * (1.0 - sg)))
            dlg_ref[...] += jnp.sum(dln * uh, axis=0, keepdims=True)
            dlb_ref[...] += jnp.sum(dln, axis=0, keepdims=True)
            duh = dln * lg_ref[...]
            du = rs * (duh - jnp.mean(duh, axis=-1, keepdims=True) - uh * jnp.mean(duh * uh, axis=-1, keepdims=True))
            dcb_ref[...] += jnp.sum(du, axis=0, keepdims=True)
            dup[rows, :] = du
            _conv_wgrad(dwc_ref, du, taps_u)
            return carry

        lax.fori_loop(0, s // rc, first, 0)

        def second(i, carry):
            base = pl.multiple_of(i * rc, rc)
            rows = pl.ds(base, rc)
            dt = _conv_taps(_anticausal_taps(dctp[pl.ds(base, rc + PAD_SHORT), :], SHORT_CONV, rc), wa_ref)
            dza_ref[rows, 0:w] = (dt * ac_ref[rows, :]).astype(BF16)
            dza_ref[rows, 2 * w:3 * w] = (dt * ah_ref[rows, :]).astype(BF16)
            du0 = _conv_taps(_anticausal_taps(dup[pl.ds(base, rc + PAD_LONG), :], CONFORMER_CONV, rc), wc_ref)
            sg = _sigmoid(cg_ref[rows, :])
            dzc_ref[rows, 0:w] = (du0 * sg).astype(BF16)
            dzc_ref[rows, w:2 * w] = (du0 * cv_ref[rows, :] * sg * (1.0 - sg)).astype(BF16)
            return carry

        lax.fori_loop(0, s // rc, second, 0)

    def blk(j):
        return pl.BlockSpec((s, w), lambda i: (0, j))

    def whole(a):
        return pl.BlockSpec(tuple(a.shape), lambda i: (0, 0))

    params = (wa, wc, cb, lg, lb, ga, gc)
    outs = (_sds((s, 3 * w), BF16), _sds((s, 2 * w), BF16)) + tuple(_sds(p.shape, F32) for p in params)
    return _pcall(
        body, name=name, out_shape=outs, grid=(1,),
        in_specs=[blk(0), blk(1), blk(2), blk(nblk - 2), blk(nblk - 1), blk(0), blk(nyb - 1)] + [whole(p) for p in params],
        out_specs=tuple(whole(o) for o in outs),
        scratch_shapes=[pltpu.VMEM((s + PAD_SHORT, w), F32), pltpu.VMEM((s + PAD_LONG, w), F32),
                        pltpu.VMEM((s + PAD_SHORT, w), F32), pltpu.VMEM((s + PAD_LONG, w), F32)],
    )(z, z, z, z, z, dy, dy, *params)


def _ffn_act_fwd(up, wf, name, comm=None):
    s, f2 = up.shape
    f = f2 // 2
    tc = 256
    nb = f // tc
    rc = ROW_CHUNK

    def body(g_ref, v_ref, wg_ref, wv_ref, o_ref, gpad, vpad):
        gpad[pl.ds(0, PAD_SHORT), :] = jnp.zeros((PAD_SHORT, tc), F32)
        vpad[pl.ds(0, PAD_SHORT), :] = jnp.zeros((PAD_SHORT, tc), F32)

        def chunk(i, carry):
            base = pl.multiple_of(i * rc, rc)
            rows = pl.ds(base, rc)
            gpad[pl.ds(base + PAD_SHORT, rc), :] = g_ref[rows, :].astype(F32)
            vpad[pl.ds(base + PAD_SHORT, rc), :] = v_ref[rows, :].astype(F32)
            gc = _conv_taps(_causal_taps(gpad[pl.ds(base, rc + PAD_SHORT), :], FFN_CONV, PAD_SHORT, rc), wg_ref)
            vc = _conv_taps(_causal_taps(vpad[pl.ds(base, rc + PAD_SHORT), :], FFN_CONV, PAD_SHORT, rc), wv_ref)
            o_ref[rows, :] = (gc * _sigmoid(gc) * vc).astype(BF16)
            return carry

        lax.fori_loop(0, s // rc, chunk, 0)

    return _pcall(
        body, name=name, out_shape=_sds((s, f), BF16), grid=(nb,),
        in_specs=[pl.BlockSpec((s, tc), lambda j: (0, j)), pl.BlockSpec((s, tc), lambda j: (0, j + nb)),
                  pl.BlockSpec((FFN_CONV, tc), lambda j: (0, j)), pl.BlockSpec((FFN_CONV, tc), lambda j: (0, j + nb))],
        out_specs=pl.BlockSpec((s, tc), lambda j: (0, j)),
        scratch_shapes=[pltpu.VMEM((s + PAD_SHORT, tc), F32), pltpu.VMEM((s + PAD_SHORT, tc), F32)], comm=comm,
    )(up, up, wf, wf)


def _ffn_act_bwd(up, dact, wf, name, comm=None):
    s, f2 = up.shape
    f = f2 // 2
    tc = 256
    nb = f // tc
    rc = ROW_CHUNK

    def body(g_ref, v_ref, da_ref, wg_ref, wv_ref, act_ref, dg_ref, dv_ref, dwg_ref, dwv_ref, gpad, vpad, dgp, dvp):
        gpad[pl.ds(0, PAD_SHORT), :] = jnp.zeros((PAD_SHORT, tc), F32)
        vpad[pl.ds(0, PAD_SHORT), :] = jnp.zeros((PAD_SHORT, tc), F32)
        dgp[pl.ds(s, PAD_SHORT), :] = jnp.zeros((PAD_SHORT, tc), F32)
        dvp[pl.ds(s, PAD_SHORT), :] = jnp.zeros((PAD_SHORT, tc), F32)
        dwg_ref[...] = jnp.zeros((FFN_CONV, tc), F32)
        dwv_ref[...] = jnp.zeros((FFN_CONV, tc), F32)

        def first(i, carry):
            base = pl.multiple_of(i * rc, rc)
            rows = pl.ds(base, rc)
            gpad[pl.ds(base + PAD_SHORT, rc), :] = g_ref[rows, :].astype(F32)
            vpad[pl.ds(base + PAD_SHORT, rc), :] = v_ref[rows, :].astype(F32)
            taps_g = _causal_taps(gpad[pl.ds(base, rc + PAD_SHORT), :], FFN_CONV, PAD_SHORT, rc)
            taps_v = _causal_taps(vpad[pl.ds(base, rc + PAD_SHORT), :], FFN_CONV, PAD_SHORT, rc)
            gc = _conv_taps(taps_g, wg_ref)
            vc = _conv_taps(taps_v, wv_ref)
            sg = _sigmoid(gc)
            silu = gc * sg
            act_ref[rows, :] = (silu * vc).astype(BF16)
            da = da_ref[rows, :].astype(F32)
            dgc = da * vc * (sg * (1.0 + gc * (1.0 - sg)))
            dvc = da * silu
            dgp[rows, :] = dgc
            dvp[rows, :] = dvc
            _conv_wgrad(dwg_ref, dgc, taps_g)
            _conv_wgrad(dwv_ref, dvc, taps_v)
            return carry

        lax.fori_loop(0, s // rc, first, 0)

        def second(i, carry):
            base = pl.multiple_of(i * rc, rc)
            rows = pl.ds(base, rc)
            dg_ref[rows, :] = _conv_taps(_anticausal_taps(dgp[pl.ds(base, rc + PAD_SHORT), :], FFN_CONV, rc), wg_ref).astype(BF16)
            dv_ref[rows, :] = _conv_taps(_anticausal_taps(dvp[pl.ds(base, rc + PAD_SHORT), :], FFN_CONV, rc), wv_ref).astype(BF16)
            return carry

        lax.fori_loop(0, s // rc, second, 0)

    lo = pl.BlockSpec((s, tc), lambda j: (0, j))
    hi = pl.BlockSpec((s, tc), lambda j: (0, j + nb))
    wlo = pl.BlockSpec((FFN_CONV, tc), lambda j: (0, j))
    whi = pl.BlockSpec((FFN_CONV, tc), lambda j: (0, j + nb))
    act, dgate, dval, dwg, dwv = _pcall(
        body, name=name,
        out_shape=(_sds((s, f), BF16), _sds((s, f), BF16), _sds((s, f), BF16), _sds((FFN_CONV, f), F32), _sds((FFN_CONV, f), F32)),
        grid=(nb,), in_specs=[lo, hi, lo, wlo, whi], out_specs=(lo, lo, lo, wlo, wlo),
        scratch_shapes=[pltpu.VMEM((s + PAD_SHORT, tc), F32) for _ in range(4)], comm=comm,
    )(up, up, dact, wf, wf)
    return act, dgate, dval, jnp.concatenate([dwg, dwv], axis=1)


def _y_assemble(yan, yb, ycn, gb, name):
    s, w = yan.shape
    wb = yb.shape[1]
    tm = ROW_CHUNK

    def body(ya_ref, yb_ref, yc_ref, g_ref, o_ref):
        ybv = yb_ref[...]
        o_ref[:, 0:w] = ya_ref[...]
        o_ref[:, w:w + wb] = (ybv * _rstd(ybv) * g_ref[...]).astype(BF16)
        o_ref[:, w + wb:] = yc_ref[...]

    return _pcall(body, name=name, out_shape=_sds((s, 2 * w + wb), BF16), grid=(s // tm,),
                  in_specs=[pl.BlockSpec((tm, w), lambda i: (i, 0)), pl.BlockSpec((tm, wb), lambda i: (i, 0)),
                            pl.BlockSpec((tm, w), lambda i: (i, 0)), pl.BlockSpec((1, wb), lambda i: (0, 0))],
                  out_specs=pl.BlockSpec((tm, 2 * w + wb), lambda i: (i, 0)))(yan, yb, ycn, gb)


def _yb_norm_bwd(yb, dy, gb, name):
    s, wb = yb.shape
    w = wb // 2
    heads = wb // D_HEAD
    tm = ROW_CHUNK

    def body(yb_ref, d1_ref, d2_ref, g_ref, dyb_ref, dl_ref, dg_ref):
        i = pl.program_id(0)
        y = yb_ref[...]
        dyn = jnp.concatenate([d1_ref[...], d2_ref[...]], axis=1)
        r = _rstd(y)
        yh = y * r
        gd = dyn * g_ref[...]
        dyb = r * (gd - yh * jnp.mean(gd * yh, axis=-1, keepdims=True))
        dyb_ref[...] = dyb
        part = jnp.sum(dyn * yh, axis=0, keepdims=True)
        prod = dyb * y
        even = lax.broadcasted_iota(I32, (tm, LANES), 1) < D_HEAD
        for p in range(heads // 2):
            blk = prod[:, p * LANES:(p + 1) * LANES]
            ev = jnp.sum(jnp.where(even, blk, 0.0), axis=1, keepdims=True)
            od = jnp.sum(jnp.where(even, 0.0, blk), axis=1, keepdims=True)
            dl_ref[2 * p] = jnp.broadcast_to(ev, (tm, LANES))
            dl_ref[2 * p + 1] = jnp.broadcast_to(od, (tm, LANES))

        @pl.when(i == 0)
        def _():
            dg_ref[...] = part

        @pl.when(i > 0)
        def _():
            dg_ref[...] += part

    return _pcall(
        body, name=name, out_shape=(_sds((s, wb), F32), _sds((heads, s, LANES), F32), _sds((1, wb), F32)),
        grid=(s // tm,),
        in_specs=[pl.BlockSpec((tm, wb), lambda i: (i, 0)), pl.BlockSpec((tm, w), lambda i: (i, 1)),
                  pl.BlockSpec((tm, w), lambda i: (i, 2)), pl.BlockSpec((1, wb), lambda i: (0, 0))],
        out_specs=(pl.BlockSpec((tm, wb), lambda i: (i, 0)), pl.BlockSpec((heads, tm, LANES), lambda i: (0, i, 0)),
                   pl.BlockSpec((1, wb), lambda i: (0, 0))),
    )(yb, dy, dy, gb)


def _t5_bucket_table():
    max_exact = NUM_BUCKETS // 2
    out = np.full((len(DILATED_BRANCHES), BLK, 2 * BLK), -1, np.int32)
    rel = np.arange(BLK)[:, None] - np.arange(2 * BLK)[None, :] + BLK
    for b, (window, dilation) in enumerate(DILATED_BRANCHES):
        n_keys = window // dilation
        dist = np.maximum(rel, 0) * dilation
        d_f = np.maximum(dist, 1).astype(np.float32)
        large = max_exact + (np.log(d_f / np.float32(max_exact)) / np.float32(math.log(MAX_DISTANCE / max_exact))
                             * np.float32(NUM_BUCKETS - max_exact)).astype(np.int32)
        large = np.minimum(large, NUM_BUCKETS - 1)
        bucket = np.where(dist < max_exact, dist, large)
        out[b] = np.where((rel >= 0) & (rel <= n_keys), bucket, -1)
    return out


def _bias_tiles(rel_bias, buckets, name):
    nbk, heads = rel_bias.shape
    nbr = buckets.shape[0]

    def body(rb_ref, bk_ref, o_ref):
        for br in range(nbr):
            bk = bk_ref[br]
            tiles = [jnp.full((BLK, 2 * BLK), NEG, F32) for _ in range(heads)]
            for b in range(nbk):
                hit = bk == b
                tiles = [jnp.where(hit, rb_ref[b, h], tiles[h]) for h in range(heads)]
            for h in range(heads):
                o_ref[br, h] = tiles[h]

    return _pcall(body, name=name, out_shape=_sds((nbr, heads, BLK, 2 * BLK), F32), grid=(1,),
                  in_specs=[pl.BlockSpec(memory_space=pltpu.SMEM), pl.BlockSpec(buckets.shape, lambda i: (0, 0, 0))],
                  out_specs=pl.BlockSpec((nbr, heads, BLK, 2 * BLK), lambda i: (0, 0, 0, 0)))(rel_bias, buckets)


def _bias_grad(dtiles, buckets, nbk, name):
    nbr, heads = dtiles.shape[:2]

    def body(dt_ref, bk_ref, o_ref):
        row = lax.broadcasted_iota(I32, (nbk, LANES), 0)
        col = lax.broadcasted_iota(I32, (nbk, LANES), 1)
        out = jnp.zeros((nbk, LANES), F32)
        for h in range(heads):
            for b in range(nbk):
                tot = jnp.zeros((), F32)
                for br in range(nbr):
                    tot = tot + jnp.sum(jnp.where(bk_ref[br] == b, dt_ref[br, h], 0.0))
                out = jnp.where((row == b) & (col == h), tot, out)
        o_ref[...] = out

    return _pcall(body, name=name, out_shape=_sds((nbk, LANES), F32), grid=(1,),
                  in_specs=[pl.BlockSpec(dtiles.shape, lambda i: (0, 0, 0, 0)), pl.BlockSpec(buckets.shape, lambda i: (0, 0, 0))],
                  out_specs=pl.BlockSpec((nbk, LANES), lambda i: (0, 0)))(dtiles, buckets)


def _largest_divisor(n, cap):
    return max(g for g in range(1, cap + 1) if n % g == 0)


def _attn_blocks(s, visit, group):
    for br, (window, d) in enumerate(DILATED_BRANCHES):
        n_blk = (s // d) // BLK
        span = BLK * d
        g1 = _largest_divisor(d, group)

        def firsts(t, carry, br=br, d=d, g1=g1):
            for j in range(g1):
                visit(br, d, t * g1 + j, False)
            return carry

        lax.fori_loop(0, d // g1, firsts, 0)
        if n_blk > 1:
            total = d * (n_blk - 1)
            g2 = _largest_divisor(total, group)

            def rest(t, carry, br=br, d=d, n_blk=n_blk, span=span, g2=g2):
                for j in range(g2):
                    idx = t * g2 + j
                    visit(br, d, idx // (n_blk - 1) + (1 + idx % (n_blk - 1)) * span, True)
                return carry

            lax.fori_loop(0, total // g2, rest, 0)


def _rows(start, size, d):
    return pl.ds(pl.multiple_of(start, BLK), size) if d == 1 else pl.ds(start, size, stride=d)


def _attn_fwd(z, btiles, col0, name, comm=None):
    s = z.shape[0]
    nbr, heads = btiles.shape[:2]
    pairs = heads // 2
    scale = D_HEAD ** -0.5
    rc = ROW_CHUNK

    def body(q_ref, k_ref, v_ref, bt_ref, yb_ref, lse_ref, acc_ref, m_ref, l_ref):
        even = lax.broadcasted_iota(I32, (BLK, LANES), 1) < D_HEAD
        even2 = lax.broadcasted_iota(I32, (2 * BLK, LANES), 1) < D_HEAD

        def visit(br, d, start, prev):
            kw = 2 * BLK if prev else BLK
            rows_q = _rows(start, BLK, d)
            rows_k = _rows(start - BLK * d, kw, d) if prev else rows_q
            qb = q_ref[rows_q, :]
            kb = k_ref[rows_k, :].astype(BF16)
            vw = v_ref[rows_k, :]
            ev_k = even2 if prev else even
            qm = jnp.concatenate([jnp.where(even, qb, 0.0), jnp.where(even, 0.0, qb)], axis=0).astype(BF16)
            bias = [bt_ref[br, e] if prev else bt_ref[br, e, :, BLK:] for e in range(2)]
            sc = _dot_nt(qm, kb) * scale + jnp.concatenate(bias, axis=0)
            m = jnp.max(sc, axis=1, keepdims=True)
            p = jnp.exp(sc - m)
            l = jnp.sum(p, axis=1, keepdims=True)
            pb = p.astype(BF16)
            vm = jnp.concatenate([jnp.where(ev_k, vw, 0.0), jnp.where(ev_k, 0.0, vw)], axis=0).astype(BF16)
            acc_ref.at[br][rows_q, :] = _dot(jnp.concatenate([pb[:BLK], pb[BLK:]], axis=1), vm)
            for e in range(2):
                m_ref.at[br, e][rows_q, :] = jnp.broadcast_to(m[e * BLK:(e + 1) * BLK], (BLK, LANES))
                l_ref.at[br, e][rows_q, :] = jnp.broadcast_to(l[e * BLK:(e + 1) * BLK], (BLK, LANES))

        _attn_blocks(s, visit, ATTN_GROUP_FWD)

        ev_c = lax.broadcasted_iota(I32, (rc, LANES), 1) < D_HEAD

        def merge(i, carry):
            rows = pl.ds(pl.multiple_of(i * rc, rc), rc)
            wts, dens = [], []
            for e in range(2):
                ms = [m_ref[br, e, rows, :] for br in range(nbr)]
                top = functools.reduce(jnp.maximum, ms)
                w = [jnp.exp(mb - top) for mb in ms]
                den = functools.reduce(lambda a, b: a + b, [w[br] * l_ref[br, e, rows, :] for br in range(nbr)])
                lse_ref[e, rows, :] = top + jnp.log(den)
                wts.append(w)
                dens.append(den)
            num = functools.reduce(lambda a, b: a + b,
                                   [jnp.where(ev_c, wts[0][br], wts[1][br]) * acc_ref[br, rows, :] for br in range(nbr)])
            yb_ref[rows, :] = num / jnp.where(ev_c, dens[0], dens[1])
            return carry

        lax.fori_loop(0, s // rc, merge, 0)

    def zcol(j):
        return pl.BlockSpec((s, LANES), lambda p, j=j: (0, col0 + j + p))

    return _pcall(
        body, name=name, out_shape=(_sds((s, pairs * LANES), F32), _sds((heads, s, LANES), F32)), grid=(pairs,),
        in_specs=[zcol(0), zcol(pairs), zcol(2 * pairs), pl.BlockSpec((nbr, 2, BLK, 2 * BLK), lambda p: (0, p, 0, 0))],
        out_specs=(pl.BlockSpec((s, LANES), lambda p: (0, p)), pl.BlockSpec((2, s, LANES), lambda p: (p, 0, 0))),
        scratch_shapes=[pltpu.VMEM((nbr, s, LANES), F32), pltpu.VMEM((nbr, 2, s, LANES), F32), pltpu.VMEM((nbr, 2, s, LANES), F32)],
        comm=comm,
    )(z, z, z, btiles)


def _attn_bwd(z, btiles, dyb, lse, delta, dbias_in, col0, name, comm=None):
    s = z.shape[0]
    nbr, heads = btiles.shape[:2]
    pairs = heads // 2
    scale = D_HEAD ** -0.5

    def body(q_ref, k_ref, v_ref, bt_ref, dy_ref, lse_ref, dl_ref, dbi_ref,
             dq_ref, dk_ref, dv_ref, db_ref, dqa, dka, dva):
        even = lax.broadcasted_iota(I32, (BLK, LANES), 1) < D_HEAD
        even2 = lax.broadcasted_iota(I32, (2 * BLK, LANES), 1) < D_HEAD
        for ref in (dqa, dka, dva):
            ref[...] = jnp.zeros((s, LANES), F32)
        db_ref[...] = dbi_ref[...]

        def visit(br, d, start, prev):
            kw = 2 * BLK if prev else BLK
            rows_q = _rows(start, BLK, d)
            rows_k = _rows(start - BLK * d, kw, d) if prev else rows_q
            qb = q_ref[rows_q, :]
            dyv = dy_ref[rows_q, :]
            kwin = k_ref[rows_k, :]
            kb = kwin.astype(BF16)
            vb = v_ref[rows_k, :].astype(BF16)
            ev_k = even2 if prev else even
            qm = jnp.concatenate([jnp.where(even, qb, 0.0), jnp.where(even, 0.0, qb)], axis=0).astype(BF16)
            dym = jnp.concatenate([jnp.where(even, dyv, 0.0), jnp.where(even, 0.0, dyv)], axis=0).astype(BF16)
            bias = [bt_ref[br, e] if prev else bt_ref[br, e, :, BLK:] for e in range(2)]
            sc = _dot_nt(qm, kb) * scale + jnp.concatenate(bias, axis=0)
            lt = jnp.concatenate([lse_ref.at[e][rows_q, :] for e in range(2)], axis=0)
            dt = jnp.concatenate([dl_ref.at[e][rows_q, :] for e in range(2)], axis=0)
            if prev:
                lt = jnp.concatenate([lt, lt], axis=1)
                dt = jnp.concatenate([dt, dt], axis=1)
            p = jnp.exp(sc - lt)
            ds = p * (_dot_nt(dym, vb) - dt)
            for e in range(2):
                if prev:
                    db_ref[br, e] += ds[e * BLK:(e + 1) * BLK]
                else:
                    db_ref[br, e, :, BLK:] += ds[e * BLK:(e + 1) * BLK]
            dsb = ds.astype(BF16)
            km = jnp.concatenate([jnp.where(ev_k, kwin, 0.0), jnp.where(ev_k, 0.0, kwin)], axis=0).astype(BF16)
            dqa[rows_q, :] += _dot(jnp.concatenate([dsb[:BLK], dsb[BLK:]], axis=1), km) * scale
            dka[rows_k, :] += _dot_tn(dsb, qm) * scale
            dva[rows_k, :] += _dot_tn(p.astype(BF16), dym)

        _attn_blocks(s, visit, ATTN_GROUP_BWD)
        dq_ref[...] = dqa[...].astype(BF16)
        dk_ref[...] = dka[...].astype(BF16)
        dv_ref[...] = dva[...].astype(BF16)

    def zcol(j):
        return pl.BlockSpec((s, LANES), lambda p, j=j: (0, col0 + j + p))

    col = pl.BlockSpec((s, LANES), lambda p: (0, p))
    stat = pl.BlockSpec((2, s, LANES), lambda p: (p, 0, 0))
    tile = pl.BlockSpec((nbr, 2, BLK, 2 * BLK), lambda p: (0, p, 0, 0))
    wide = _sds((s, pairs * LANES), BF16)
    return _pcall(
        body, name=name, out_shape=(wide, wide, wide, _sds(btiles.shape, F32)), grid=(pairs,),
        in_specs=[zcol(0), zcol(pairs), zcol(2 * pairs), tile, col, stat, stat, tile],
        out_specs=(col, col, col, tile),
        scratch_shapes=[pltpu.VMEM((s, LANES), F32) for _ in range(3)], comm=comm,
    )(z, z, z, btiles, dyb, lse, delta, dbias_in)


def _row(v):
    return v.reshape(1, -1)


class _LocalSchedule:
    def __init__(self):
        self.big = {}

    def fwd_comms(self, l):
        return {}

    def bwd_comms(self, l):
        return {}

    def after_bwd(self, l, grads):
        self.big[l] = grads


def _layer_fwd(l, x, wts, prm, btiles, comms):
    d = x.shape[1]
    wq = d // 4
    gout = prm["out_norm_g"][l]
    h = _rms_fwd(x, _row(prm["norm_mix_g"][l]), "rms_mix_fwd", comm=comms.get("rms_mix_fwd"))
    z = _mm_n(h, wts["in_t"], l, nt=True, tn=256, out_dtype=F32, name="in_proj", comm=comms.get("in_proj"))
    yan, ycn = _mix_fwd(z, prm["conv_a_w"][l], prm["conv_c_w"][l], _row(prm["conv_c_b"][l]), _row(prm["ln_c_g"][l]),
                        _row(prm["ln_c_b"][l]), _row(gout[:wq]), _row(gout[3 * wq:]), "mix_fwd")
    yb, lse = _attn_fwd(z, btiles, 3 * wq // LANES, "attn_fwd", comm=comms.get("attn_fwd"))
    y = _y_assemble(yan, yb, ycn, _row(gout[wq:3 * wq]), "y_assemble")
    x_mid = _mm_n(y, wts["out"], l, nt=False, tn=256, out_dtype=F32, name="out_proj", resid=x, comm=comms.get("out_proj"))
    h2 = _rms_fwd(x_mid, _row(prm["norm_ffn_g"][l]), "rms_ffn_fwd")
    up = _mm_n(h2, wts["up"], l, nt=False, tn=512, out_dtype=BF16, name="up_proj", comm=comms.get("up_proj"))
    act = _ffn_act_fwd(up, prm["conv_f_w"][l], "ffn_act_fwd", comm=comms.get("ffn_act_fwd"))
    x_out = _mm_n(act, wts["down"], l, nt=False, tn=256, out_dtype=F32, name="down_proj", resid=x_mid,
                  comm=comms.get("down_proj"))
    return x_out, (x, h, z, yb, lse, y, x_mid, h2, up)


def _layer_bwd(l, dxo, dxo_b, saved, wts, prm, btiles, dbias, comms):
    x, h, z, yb, lse, y, x_mid, h2, up = saved
    d = x.shape[1]
    wq = d // 4
    f = up.shape[1] // 2
    gout = prm["out_norm_g"][l]
    dact = _mm_n(dxo_b, wts["down"], l, nt=True, tn=256, out_dtype=BF16, name="down_proj_dx", comm=comms.get(SWAP_RIDE))
    act, dgate, dval, dwf = _ffn_act_bwd(up, dact, prm["conv_f_w"][l], "ffn_act_bwd", comm=comms.get("ffn_act_bwd"))
    g_down = _mm_tn(act, dxo_b, t=256, name="down_proj_dw")
    dh2 = _mm_n(dgate, wts["up"], l, nt=True, tn=256, out_dtype=F32, name="up_proj_dx_gate", comm=comms.get(SHARE_RIDE))
    dh2 = _mm_n(dval, wts["up"], l, nt=True, tn=256, out_dtype=F32, name="up_proj_dx_val", resid=dh2, b_part=1)
    dxm, dxm_b, dg_ffn = _rms_bwd(x_mid, _row(prm["norm_ffn_g"][l]), dh2, dxo, "rms_ffn_bwd")
    g_up = _mm_tn2(h2, dgate, dval, t=256, name="up_proj_dw")
    dy = _mm_n(dxm_b, wts["out"], l, nt=True, tn=256, out_dtype=F32, name="out_proj_dx")
    g_out = _mm_tn(y, dxm_b, t=256, name="out_proj_dw")
    dza, dzc, dwa, dwc, dcb, dlg, dlb, dga, dgc = _mix_bwd(
        z, dy, prm["conv_a_w"][l], prm["conv_c_w"][l], _row(prm["conv_c_b"][l]), _row(prm["ln_c_g"][l]),
        _row(prm["ln_c_b"][l]), _row(gout[:wq]), _row(gout[3 * wq:]), "mix_bwd")
    dyb, delta, dgb = _yb_norm_bwd(yb, dy, _row(gout[wq:3 * wq]), "yb_norm_bwd")
    dq, dk, dv, dbias = _attn_bwd(z, btiles, dyb, lse, delta, dbias, 3 * wq // LANES, "attn_bwd",
                                  comm=comms.get("attn_bwd"))
    dz = jnp.concatenate([dza, dq, dk, dv, dzc], axis=1)
    dh = _mm_n(dz, wts["in_t"], l, nt=False, tn=256, out_dtype=F32, name="in_proj_dx")
    dx, dx_b, dg_mix = _rms_bwd(x, _row(prm["norm_mix_g"][l]), dh, dxm, "rms_mix_bwd")
    g_in_t = _mm_tn(dz, h, t=256, name="in_proj_dw")
    big = {"in_t": g_in_t, "out": g_out, "up": g_up, "down": g_down}
    small = {"norm_mix_g": dg_mix[0], "conv_a_w": dwa, "conv_c_w": dwc, "conv_c_b": dcb[0], "ln_c_g": dlg[0],
             "ln_c_b": dlb[0], "out_norm_g": jnp.concatenate([dga[0], dgb[0], dgc[0]]), "norm_ffn_g": dg_ffn[0],
             "conv_f_w": dwf}
    return dx, dx_b, big, small, dbias


def _local_step(x, tgt, wts, prm, sched):
    depth = prm["norm_mix_g"].shape[0]
    buckets = jnp.asarray(_t5_bucket_table())
    btiles = _bias_tiles(prm["rel_bias"], buckets, "bias_tiles")
    saved = []
    for l in range(depth):
        x, sv = _layer_fwd(l, x, wts, prm, btiles, sched.fwd_comms(l))
        saved.append(sv)
    loss, dx, dx_b, dg_final = _final_loss(x, _row(prm["final_g"]), tgt, "final_loss")
    dbias = jnp.zeros(btiles.shape, F32)
    small = [None] * depth
    for l in reversed(range(depth)):
        dx, dx_b, grads, small[l], dbias = _layer_bwd(l, dx, dx_b, saved[l], wts, prm, btiles, dbias, sched.bwd_comms(l))
        sched.after_bwd(l, grads)
    nbk, heads = prm["rel_bias"].shape
    d_rel = _bias_grad(dbias, buckets, nbk, "bias_grad")[:, :heads]
    return loss, dx, small, d_rel, dg_final[0]


BIG = ("in_t", "out", "up", "down")
COL_SHARDED = ("up",)
N_CHIPS = 4
N_DEV = 8
BF16_ROWS = 16


def _me():
    return lax.axis_index("x"), lax.axis_index("y"), lax.axis_index("c")


def _chip_of(x, y):
    return 2 * x + y


def _other_chips(x, y):
    return ((1 - x, y), (x, 1 - y), (1 - x, 1 - y))


def _remote(src, dst, send_sem, recv_sem, device):
    return pltpu.make_async_remote_copy(src_ref=src, dst_ref=dst, send_sem=send_sem, recv_sem=recv_sem,
                                        device_id=device, device_id_type=MESH)


def _ag_comm(wts, layer, ici_keys, fwd_keys):
    keys = tuple(k for k in BIG if k in ici_keys or k in fwd_keys)

    def geo(k):
        _, rows, cols = wts[k].shape
        return (rows, cols // N_CHIPS) if k in COL_SHARDED else (rows // N_CHIPS, cols)

    def copies(refs, sems):
        g = dict(zip(keys, refs))
        isend, irecv, dsend, drecv = sems
        x, y, c = _me()
        mine = _chip_of(x, y)

        def region(k, chip, half):
            r, cc = geo(k)
            h = r // 2
            if k in COL_SHARDED:
                return g[k].at[layer, pl.ds(pl.multiple_of(half * h, BF16_ROWS), h), pl.ds(pl.multiple_of(chip * cc, LANES), cc)]
            return g[k].at[layer, pl.ds(pl.multiple_of(chip * r + half * h, BF16_ROWS), h), :]

        def ici(k, f, landing):
            chip = _other_chips(x, y)[f]
            where = region(k, _chip_of(*chip) if landing else mine, c)
            i = keys.index(k)
            return _remote(where, where, isend.at[i, f], irecv.at[i, f], (*chip, c))

        def fwd(k, f, landing):
            chip = _other_chips(x, y)[f]
            where = region(k, _chip_of(*chip), 1 - c if landing else c)
            i = keys.index(k)
            return _remote(where, where, dsend.at[i, f], drecv.at[i, f], (x, y, 1 - c))

        return ici, fwd

    def start(ins, outs, sems):
        ici, fwd = copies(outs, sems)
        for k in keys:
            for f in range(3):
                if k in ici_keys:
                    ici(k, f, False).start()
                else:
                    fwd(k, f, False).start()

    def finish(ins, outs, sems):
        ici, fwd = copies(outs, sems)
        for k in keys:
            for f in range(3):
                if k in ici_keys:
                    ici(k, f, True).wait_recv()
                    if k in fwd_keys:
                        fwd(k, f, False).start()
        for k in keys:
            for f in range(3):
                if k in fwd_keys:
                    fwd(k, f, True).wait_recv()
                    fwd(k, f, False).wait_send()
                if k in ici_keys:
                    ici(k, f, False).wait_send()

    def done(res):
        wts.update(zip(keys, res))

    n = len(keys)
    return _Comm([wts[k] for k in keys], [_sds(wts[k].shape, BF16) for k in keys], {i: i for i in range(n)},
                 [pltpu.SemaphoreType.DMA((n, 3)) for _ in range(4)], start, finish, done)


def _small_gather_comm(slab, store):
    def copies(ins, outs, sems):
        send, recv, lsem = sems
        x, y, c = _me()
        mine = _chip_of(x, y)
        own = pltpu.make_async_copy(ins[0], outs[0].at[mine], lsem)
        pairs = []
        for f, chip in enumerate(_other_chips(x, y)):
            out = _remote(ins[0], outs[0].at[mine], send.at[f], recv.at[f], (*chip, c))
            land = _remote(ins[0], outs[0].at[_chip_of(*chip)], send.at[f], recv.at[f], (*chip, c))
            pairs.append((out, land))
        return own, pairs

    def start(ins, outs, sems):
        own, pairs = copies(ins, outs, sems)
        own.start()
        for out, _ in pairs:
            out.start()

    def finish(ins, outs, sems):
        own, pairs = copies(ins, outs, sems)
        for out, land in pairs:
            land.wait_recv()
            out.wait_send()
        own.wait()

    def done(res):
        store["small"] = res[0]

    return _Comm([slab], [_sds((N_CHIPS,) + slab.shape, F32)], {},
                 [pltpu.SemaphoreType.DMA((3,)), pltpu.SemaphoreType.DMA((3,)), pltpu.SemaphoreType.DMA], start, finish, done)


def _piece_geo(g):
    geo = {}
    for k in BIG:
        rows, cols = g[k].shape
        geo[k] = (rows // 2, cols // N_CHIPS) if k in COL_SHARDED else (rows // (2 * N_CHIPS), cols)
    return geo


def _swap_comm(g, done):
    geo = _piece_geo(g)
    n_copies = sum(N_CHIPS if k in COL_SHARDED else 1 for k in BIG)

    def copies(ins, outs, sems):
        g_refs, t_refs = dict(zip(BIG, ins)), dict(zip(BIG, outs))
        send, recv = sems
        x, y, c = _me()
        pairs = []
        for k in BIG:
            h, cc = geo[k]
            if k in COL_SHARDED:
                rows = pl.ds(pl.multiple_of((1 - c) * h, BF16_ROWS), h)
                pairs += [(g_refs[k].at[rows, pl.ds(j * cc, cc)], t_refs[k].at[j]) for j in range(N_CHIPS)]
            else:
                pairs.append((g_refs[k].at[:, 1 - c], t_refs[k]))
        return [_remote(src, dst, send.at[i], recv.at[i], (x, y, 1 - c)) for i, (src, dst) in enumerate(pairs)]

    def start(ins, outs, sems):
        for cp in copies(ins, outs, sems):
            cp.start()

    def finish(ins, outs, sems):
        for cp in copies(ins, outs, sems):
            cp.wait()

    ins = [g[k] if k in COL_SHARDED else g[k].reshape(N_CHIPS, 2, geo[k][0], geo[k][1]) for k in BIG]
    return _Comm(ins, [_sds((N_CHIPS,) + geo[k], BF16) for k in BIG], {},
                 [pltpu.SemaphoreType.DMA((n_copies,)) for _ in range(2)], start, finish,
                 lambda res: done(dict(zip(BIG, res))))


def _pair_sum(g, theirs, c_arr):
    geo = _piece_geo(g)

    def body(c_ref, *refs):
        nk = len(BIG)
        for i in range(nk):
            refs[2 * nk + i][...] = (refs[i][...].astype(F32) + refs[nk + i][...].astype(F32)).astype(BF16)

    in_specs, ins = [], []
    for k in BIG:
        h, cc = geo[k]
        if k in COL_SHARDED:
            in_specs.append(pl.BlockSpec((h, cc), lambda j, c_ref: (c_ref[0], j)))
            ins.append(g[k])
        else:
            in_specs.append(pl.BlockSpec((None, h, cc), lambda j, c_ref: (2 * j + c_ref[0], 0, 0)))
            ins.append(g[k].reshape(2 * N_CHIPS, h, cc))
    slab = [pl.BlockSpec((None,) + geo[k], lambda j, c_ref: (j, 0, 0)) for k in BIG]
    res = _pcall(body, name="rs_pair_sum", out_shape=tuple(_sds((N_CHIPS,) + geo[k], BF16) for k in BIG), grid=(N_CHIPS,),
                 in_specs=in_specs + slab, out_specs=tuple(slab), prefetch=1)(c_arr, *ins, *[theirs[k] for k in BIG])
    return dict(zip(BIG, res))


def _rs_comm(p, keys, store):
    def copies(ins, outs, sems):
        send, recv = sems
        x, y, c = _me()
        return [_remote(ins[i].at[_chip_of(*chip)], outs[i].at[f], send.at[i, f], recv.at[i, f], (*chip, c))
                for i in range(len(keys)) for f, chip in enumerate(_other_chips(x, y))]

    def start(ins, outs, sems):
        for cp in copies(ins, outs, sems):
            cp.start()

    def finish(ins, outs, sems):
        for cp in copies(ins, outs, sems):
            cp.wait()

    def done(res):
        store.update(zip(keys, res))

    return _Comm([p[k] for k in keys], [_sds((3,) + p[k].shape[1:], BF16) for k in keys], {},
                 [pltpu.SemaphoreType.DMA((len(keys), 3)) for _ in range(2)], start, finish, done)


def _quad_sum(p, b, where, l, full):
    parts = 2
    nk = len(BIG)

    def body(where_ref, *refs):
        for i in range(nk):
            acc = refs[i][...].astype(F32)
            for f in range(3):
                acc = acc + refs[nk + 3 * i + f][...].astype(F32)
            refs[5 * nk + i][...] = acc

    own, recv, outs = [], [], []
    for k in BIG:
        h, cc = p[k].shape[1:]
        th = h // parts
        own.append(pl.BlockSpec((None, th, cc), lambda i, w_ref: (w_ref[0], i, 0)))
        recv += [pl.BlockSpec((None, th, cc), lambda i, w_ref, f=f: (f, i, 0)) for f in range(3)]
        outs.append(pl.BlockSpec((None, None, th, cc), lambda i, w_ref: (l, w_ref[1], i, 0)))
    args = [p[k] for k in BIG] + [b[k] for k in BIG for _ in range(3)] + [full[k] for k in BIG]
    res = _pcall(body, name="rs_quad_sum", out_shape=tuple(_sds(full[k].shape, F32) for k in BIG), grid=(parts,),
                 in_specs=own + recv + [ANY] * nk, out_specs=tuple(outs), prefetch=1,
                 aliases={1 + 4 * nk + i: i for i in range(nk)})(where, *args)
    return dict(zip(BIG, res))


def _share_comm(layers, full, done):
    nk = len(BIG)

    def copies(outs, sems, landing):
        send, recv = sems
        x, y, c = _me()
        half = 1 - c if landing else c
        return [_remote(outs[i].at[l, half], outs[i].at[l, half], send.at[i, j], recv.at[i, j], (x, y, 1 - c))
                for i in range(nk) for j, l in enumerate(layers)]

    def start(ins, outs, sems):
        for cp in copies(outs, sems, False):
            cp.start()

    def finish(ins, outs, sems):
        for cp in copies(outs, sems, True):
            cp.wait_recv()
        for cp in copies(outs, sems, False):
            cp.wait_send()

    return _Comm([full[k] for k in BIG], [_sds(full[k].shape, F32) for k in BIG], {i: i for i in range(nk)},
                 [pltpu.SemaphoreType.DMA((nk, len(layers))) for _ in range(2)], start, finish,
                 lambda res: done(dict(zip(BIG, res))))


def _gather_comm(slab, done):
    def copies(ins, outs, sems, landing):
        send, recv = sems
        x, y, c = _me()
        me = 4 * x + 2 * y + c
        out = []
        for mask in range(1, N_DEV):
            peer = (x ^ (mask >> 2), y ^ ((mask >> 1) & 1), c ^ (mask & 1))
            slot = 4 * peer[0] + 2 * peer[1] + peer[2] if landing else me
            out.append(_remote(ins[0], outs[0].at[slot], send.at[mask - 1], recv.at[mask - 1], peer))
        return out

    def start(ins, outs, sems):
        for cp in copies(ins, outs, sems, False):
            cp.start()

    def finish(ins, outs, sems):
        for cp in copies(ins, outs, sems, True):
            cp.wait_recv()
        for cp in copies(ins, outs, sems, False):
            cp.wait_send()

    return _Comm([slab], [_sds((N_DEV,) + slab.shape, F32)], {},
                 [pltpu.SemaphoreType.DMA((N_DEV - 1,)), pltpu.SemaphoreType.DMA((N_DEV - 1,))], start, finish,
                 lambda res: done(res[0]))


def _sum_slabs(slabs, own, me):
    n, r, lanes = slabs.shape
    tr = r // 2

    def body(me_ref, s_ref, own_ref, o_ref):
        o_ref[...] = jnp.zeros((tr, lanes), F32)
        for i in range(n):
            @pl.when(me_ref[0] == i)
            def _():
                o_ref[...] += own_ref[...]

            @pl.when(me_ref[0] != i)
            def _():
                o_ref[...] += s_ref[i]

    return _pcall(body, name="sum_partials", out_shape=_sds((r, lanes), F32), grid=(2,),
                  in_specs=[pl.BlockSpec((n, tr, lanes), lambda i, me_ref: (0, i, 0)),
                            pl.BlockSpec((tr, lanes), lambda i, me_ref: (i, 0))],
                  out_specs=pl.BlockSpec((tr, lanes), lambda i, me_ref: (i, 0)), prefetch=1)(me, slabs, own)


def _cast_into_gathered(w, chip, by_cols, name):
    l, r, c = w.shape

    def body(chip_ref, w_ref, o_ref):
        o_ref[...] = w_ref[...].astype(BF16)

    if by_cols:
        shape, out = (l, r, N_CHIPS * c), pl.BlockSpec((None, r, c), lambda i, chip_ref: (i, 0, chip_ref[0]))
    else:
        shape, out = (l, N_CHIPS * r, c), pl.BlockSpec((None, r, c), lambda i, chip_ref: (i, chip_ref[0], 0))
    return _pcall(body, name=name, out_shape=_sds(shape, BF16), grid=(l,),
                  in_specs=[pl.BlockSpec((None, r, c), lambda i, chip_ref: (i, 0, 0))], out_specs=out, prefetch=1)(chip, w)


def _adamw_math(w, g, m, v):
    mn = ADAM_B1 * m + (1.0 - ADAM_B1) * g
    vn = ADAM_B2 * v + (1.0 - ADAM_B2) * (g * g)
    m_hat = mn / (1.0 - ADAM_B1 ** ADAM_STEP)
    v_hat = vn / (1.0 - ADAM_B2 ** ADAM_STEP)
    return -ADAM_LR * (m_hat / (jnp.sqrt(v_hat) + ADAM_EPS) + ADAM_WD * w), mn, vn


def _adamw(w, g, m, v, name, tr):
    r, c = w.shape

    def body(w_ref, g_ref, m_ref, v_ref, d_ref, mo_ref, vo_ref):
        d_ref[...], mo_ref[...], vo_ref[...] = _adamw_math(w_ref[...], g_ref[...], m_ref[...], v_ref[...])

    blk = pl.BlockSpec((tr, c), lambda i: (i, 0))
    return _pcall(body, name=name, out_shape=tuple(_sds((r, c), F32) for _ in range(3)), grid=(r // tr,),
                  in_specs=[blk] * 4, out_specs=(blk, blk, blk))(w, g, m, v)


def _adamw_small(groups, name):
    count = len(groups[0])
    shapes = [a.shape for a in groups[0]]
    as2d = [(math.prod(s[:-1]), s[-1]) for s in shapes]

    def body(*refs):
        for i in range(count):
            out = _adamw_math(*[refs[j * count + i][...] for j in range(4)])
            for j in range(3):
                refs[(4 + j) * count + i][...] = out[j]

    specs = [pl.BlockSpec(s, lambda i: (0, 0)) for s in as2d]
    res = _pcall(body, name=name, out_shape=tuple(_sds(s, F32) for _ in range(3) for s in as2d), grid=(1,),
                 in_specs=specs * 4, out_specs=tuple(specs * 3))(*[a.reshape(s) for grp in groups for a, s in zip(grp, as2d)])
    return [[res[j * count + i].reshape(shapes[i]) for i in range(count)] for j in range(3)]


AG_RIDES = {"in_proj": (0, ("out",), ()), "attn_fwd": (0, ("up",), ("out",)), "out_proj": (0, (), ("up",)),
            "up_proj": (1, ("in_t",), ()), "ffn_act_fwd": (1, ("down",), ()), "down_proj": (1, (), ("in_t",)),
            "rms_mix_fwd": (0, (), ("down",))}
AG_FIRST = ("in_t", "down")
RS_RIDES = {"ffn_act_bwd": ("up",), "attn_bwd": ("in_t", "out", "down")}
SWAP_RIDE = "down_proj_dx"
SHARE_RIDE = "up_proj_dx_gate"


class _Rides:
    def __init__(self, table, build):
        self.table, self.build = table, build

    def get(self, name):
        return self.build(self.table[name]) if name in self.table else None


class _MeshSchedule:
    def __init__(self, wts, depth, c_arr, where):
        self.wts, self.depth, self.c_arr, self.where = wts, depth, c_arr, where
        self.grads, self.pairs, self.recv, self.full, self.unshared = None, None, {}, None, []

    def fwd_comms(self, l):
        table = {}
        for name, (off, ici, fwd) in AG_RIDES.items():
            if l + off == 0:
                ici, fwd = (tuple(k for k in keys if k not in AG_FIRST) for keys in (ici, fwd))
            if l + off < self.depth and (ici or fwd):
                table[name] = (l + off, ici, fwd)
        return _Rides(table, lambda ride: _ag_comm(self.wts, *ride))

    def _swapped(self, theirs):
        self.pairs = _pair_sum(self.grads, theirs, self.c_arr)

    def _shared(self, full):
        self.full, self.unshared = full, []

    def bwd_comms(self, l):
        if self.grads is None:
            return {}
        table = dict(RS_RIDES)
        table[SWAP_RIDE] = "swap"
        if self.unshared:
            table[SHARE_RIDE] = "share"
        return _Rides(table, lambda what: _swap_comm(self.grads, self._swapped) if what == "swap"
                      else _share_comm(self.unshared, self.full, self._shared) if what == "share"
                      else _rs_comm(self.pairs, what, self.recv))

    def _reduce(self, l):
        self.full = _quad_sum(self.pairs, self.recv, self.where, l, self.full)
        self.unshared = self.unshared + [l]
        self.grads, self.pairs, self.recv = None, None, {}

    def after_bwd(self, l, grads):
        if self.grads is not None:
            self._reduce(l + 1)
        if self.full is None:
            geo = _piece_geo(grads)
            self.full = {k: jnp.zeros((self.depth, 2) + geo[k], F32) for k in BIG}
        self.grads = grads

    def finish(self, extra):
        _run_comm(_swap_comm(self.grads, self._swapped), "rs_swap_halves")
        _run_comm(_both(_rs_comm(self.pairs, BIG, self.recv), extra), "rs_to_owners")
        self._reduce(0)
        _run_comm(_share_comm(self.unshared, self.full, self._shared), "rs_share")
        return self.full


SHARDED_SMALL = ("conv_a_w", "conv_c_w", "conv_f_w")
SMALL = ("norm_mix_g", "conv_a_w", "conv_c_w", "conv_c_b", "ln_c_g", "ln_c_b", "out_norm_g", "norm_ffn_g",
         "conv_f_w", "rel_bias", "final_g")
SLAB_ROWS = 16


def _pack(arrays):
    flat = jnp.concatenate([a.reshape(-1) for a in arrays])
    unit = SLAB_ROWS * LANES
    total = -(-flat.shape[0] // unit) * unit
    return jnp.pad(flat, (0, total - flat.shape[0])).reshape(-1, LANES)


def _unpack(slab, shapes):
    flat = slab.reshape(-1)
    out, off = [], 0
    for shp in shapes:
        size = math.prod(shp)
        out.append(flat[off:off + size].reshape(shp))
        off += size
    return out


def kernel(x, norm_mix_g, w_in, conv_a_w, conv_c_w, conv_c_b, ln_c_g, ln_c_b, out_norm_g, w_out, norm_ffn_g, w_up, conv_f_w, w_down, rel_bias, final_g, loss_target, m_norm_mix_g, m_w_in, m_conv_a_w, m_conv_c_w, m_conv_c_b, m_ln_c_g, m_ln_c_b, m_out_norm_g, m_w_out, m_norm_ffn_g, m_w_up, m_conv_f_w, m_w_down, m_rel_bias, m_final_g, v_norm_mix_g, v_w_in, v_conv_a_w, v_conv_c_w, v_conv_c_b, v_ln_c_g, v_ln_c_b, v_out_norm_g, v_w_out, v_norm_ffn_g, v_w_up, v_conv_f_w, v_w_down, v_rel_bias, v_final_g):
    weights = dict(norm_mix_g=norm_mix_g, w_in=w_in, conv_a_w=conv_a_w, conv_c_w=conv_c_w, conv_c_b=conv_c_b,
                   ln_c_g=ln_c_g, ln_c_b=ln_c_b, out_norm_g=out_norm_g, w_out=w_out, norm_ffn_g=norm_ffn_g, w_up=w_up,
                   conv_f_w=conv_f_w, w_down=w_down, rel_bias=rel_bias, final_g=final_g)
    mom_m = dict(norm_mix_g=m_norm_mix_g, w_in=m_w_in, conv_a_w=m_conv_a_w, conv_c_w=m_conv_c_w, conv_c_b=m_conv_c_b,
                 ln_c_g=m_ln_c_g, ln_c_b=m_ln_c_b, out_norm_g=m_out_norm_g, w_out=m_w_out, norm_ffn_g=m_norm_ffn_g,
                 w_up=m_w_up, conv_f_w=m_conv_f_w, w_down=m_w_down, rel_bias=m_rel_bias, final_g=m_final_g)
    mom_v = dict(norm_mix_g=v_norm_mix_g, w_in=v_w_in, conv_a_w=v_conv_a_w, conv_c_w=v_conv_c_w, conv_c_b=v_conv_c_b,
                 ln_c_g=v_ln_c_g, ln_c_b=v_ln_c_b, out_norm_g=v_out_norm_g, w_out=v_w_out, norm_ffn_g=v_norm_ffn_g,
                 w_up=v_w_up, conv_f_w=v_conv_f_w, w_down=v_w_down, rel_bias=v_rel_bias, final_g=v_final_g)
    xi, yi, ci = _me()
    chip = _chip_of(xi, yi)
    c_arr = jnp.reshape(ci, (1,)).astype(I32)
    chip_arr = jnp.reshape(chip, (1,)).astype(I32)
    me_arr = jnp.reshape(4 * xi + 2 * yi + ci, (1,)).astype(I32)
    where = jnp.stack([chip, ci]).astype(I32)
    depth = w_out.shape[0]

    wts = {"in_t": _cast_into_gathered(jnp.swapaxes(w_in, 1, 2), chip_arr, False, "cast_in"),
           "out": _cast_into_gathered(w_out, chip_arr, False, "cast_out"),
           "up": _cast_into_gathered(w_up, chip_arr, True, "cast_up"),
           "down": _cast_into_gathered(w_down, chip_arr, False, "cast_down")}
    store = {}
    _run_comm(_small_gather_comm(_pack([weights[n] for n in SHARDED_SMALL]), store), "ag_small")
    _run_comm(_ag_comm(wts, 0, AG_FIRST, AG_FIRST), "ag_weights")
    prm = {n: weights[n] for n in SMALL if n not in SHARDED_SMALL}
    per_chip = [_unpack(store["small"][j], [weights[n].shape for n in SHARDED_SMALL]) for j in range(N_CHIPS)]
    for i, n in enumerate(SHARDED_SMALL):
        prm[n] = jnp.concatenate([per_chip[j][i] for j in range(N_CHIPS)], axis=-1)

    sched = _MeshSchedule(wts, depth, c_arr, where)
    loss_row, dx, small, d_rel, d_final = _local_step(x[0], loss_target[0], wts, prm, sched)
    loss = lax.psum(loss_row[0, 0], ("x", "y", "c"))

    stacked = {n: jnp.stack([small[l][n] for l in range(depth)]) for n in small[0]}
    stacked["rel_bias"] = d_rel
    stacked["final_g"] = d_final
    full_shapes = [stacked[n].shape for n in SMALL]
    partial = _pack([stacked[n] for n in SMALL])
    reduced = sched.finish(_gather_comm(partial, lambda res: store.update(partials=res)))

    grads = {}
    shard_shapes = {"in_t": jnp.swapaxes(w_in, 1, 2).shape, "out": w_out.shape, "up": w_up.shape, "down": w_down.shape}
    red = {k: reduced[k].reshape(shard_shapes[k]) for k in BIG}
    grads["w_in"] = jnp.swapaxes(red["in_t"], 1, 2)
    grads["w_out"], grads["w_up"], grads["w_down"] = red["out"], red["up"], red["down"]
    delta, new_m, new_v = {}, {}, {}
    for n in ("w_in", "w_out", "w_up", "w_down"):
        shp = weights[n].shape
        flat = lambda a, shp=shp: a.reshape(shp[0] * shp[1], shp[2])
        tile = max(t for t in range(8, 257, 8) if shp[1] % t == 0)
        d, mn, vn = _adamw(flat(weights[n]), flat(grads[n]), flat(mom_m[n]), flat(mom_v[n]), "adamw_" + n, tile)
        delta[n], new_m[n], new_v[n] = d.reshape(shp), mn.reshape(shp), vn.reshape(shp)

    summed = _unpack(_sum_slabs(store["partials"], partial, me_arr), full_shapes)
    for n, g in zip(SMALL, summed):
        if n in SHARDED_SMALL:
            width = weights[n].shape[-1]
            g = lax.dynamic_slice_in_dim(g, chip * width, width, axis=g.ndim - 1)
        grads[n] = g
    res = _adamw_small([[src[n] for n in SMALL] for src in (weights, grads, mom_m, mom_v)], "adamw_small")
    for i, n in enumerate(SMALL):
        delta[n], new_m[n], new_v[n] = res[0][i], res[1][i], res[2][i]

    order = ("norm_mix_g", "w_in", "conv_a_w", "conv_c_w", "conv_c_b", "ln_c_g", "ln_c_b", "out_norm_g", "w_out",
             "norm_ffn_g", "w_up", "conv_f_w", "w_down", "rel_bias", "final_g")
    return (loss, dx[None], *[grads[n] for n in order], *[delta[n] for n in order], *[new_m[n] for n in order],
            *[new_v[n] for n in order])
```

```python
import functools
import math

import numpy as np
import jax
import jax.numpy as jnp
from jax import lax
from jax.experimental import pallas as pl
from jax.experimental.pallas import tpu as pltpu

F32 = jnp.float32
BF16 = jnp.bfloat16
I32 = jnp.int32

EPS = 1e-6
NEG = -1e30
D_HEAD = 64
LANES = 128
BLK = 128
ATTN_GROUP_FWD = 4
ATTN_GROUP_BWD = 4
DILATED_BRANCHES = ((128, 1), (512, 4), (2048, 16))
NUM_BUCKETS = 32
MAX_DISTANCE = 2048
SHORT_CONV = 3
CONFORMER_CONV = 31
FFN_CONV = 3
PAD_SHORT = 8
PAD_LONG = 32
ROW_CHUNK = 256
V7X_VMEM_BYTES = 64 * 1024 * 1024
VMEM_REQUEST = V7X_VMEM_BYTES * 7 // 8

ADAM_LR = 0.001
ADAM_B1 = 0.9
ADAM_B2 = 0.999
ADAM_EPS = 1e-08
ADAM_WD = 0.01
ADAM_STEP = 10

MESH = pl.DeviceIdType.MESH
ANY = pl.BlockSpec(memory_space=pl.ANY)


def _sds(shape, dtype):
    return jax.ShapeDtypeStruct(tuple(shape), dtype)


class _Comm:
    def __init__(self, ins, out_shapes, aliases, sems, start, finish, done):
        self.ins, self.out_shapes, self.aliases, self.sems = list(ins), list(out_shapes), dict(aliases), list(sems)
        self.start, self.finish, self.done = start, finish, done


def _pcall(body, *, name, out_shape, grid=(), in_specs=None, out_specs=None, scratch_shapes=(), vmem=VMEM_REQUEST,
           aliases=None, prefetch=0, comm=None):
    params = pltpu.CompilerParams(dimension_semantics=("arbitrary",) * len(grid), vmem_limit_bytes=vmem)
    single = not isinstance(out_shape, (tuple, list))
    outs = [out_shape] if single else list(out_shape)
    ospecs = [out_specs] if single else list(out_specs)
    ispecs, scratch, aliases = list(in_specs), list(scratch_shapes), dict(aliases or {})
    n_in, n_out, n_scr = len(ispecs), len(outs), len(scratch)
    kernel_body = body
    if comm is not None:
        n_ci, n_co = len(comm.ins), len(comm.out_shapes)

        def kernel_body(*refs):
            pre, rest = refs[:prefetch], refs[prefetch:]
            core_in, c_in = rest[:n_in], rest[n_in:n_in + n_ci]
            o0 = n_in + n_ci
            core_out, c_out = rest[o0:o0 + n_out], rest[o0 + n_out:o0 + n_out + n_co]
            s0 = o0 + n_out + n_co
            core_scr, c_sem = rest[s0:s0 + n_scr], rest[s0 + n_scr:]
            first = functools.reduce(jnp.logical_and, [pl.program_id(a) == 0 for a in range(len(grid))])
            last = functools.reduce(jnp.logical_and, [pl.program_id(a) == grid[a] - 1 for a in range(len(grid))])
            pl.when(first)(lambda: comm.start(c_in, c_out, c_sem))
            body(*pre, *core_in, *core_out, *core_scr)
            pl.when(last)(lambda: comm.finish(c_in, c_out, c_sem))

        for i, o in comm.aliases.items():
            aliases[prefetch + n_in + i] = n_out + o
        ispecs += [ANY] * n_ci
        ospecs += [ANY] * n_co
        outs += comm.out_shapes
        scratch += comm.sems
    if prefetch:
        spec = pltpu.PrefetchScalarGridSpec(num_scalar_prefetch=prefetch, grid=grid, in_specs=ispecs,
                                            out_specs=tuple(ospecs), scratch_shapes=scratch)
        call = pl.pallas_call(kernel_body, name=name, out_shape=tuple(outs), grid_spec=spec,
                              input_output_aliases=aliases, compiler_params=params)
    else:
        call = pl.pallas_call(kernel_body, name=name, out_shape=tuple(outs), grid=grid, in_specs=ispecs,
                              out_specs=tuple(ospecs), scratch_shapes=scratch, input_output_aliases=aliases,
                              compiler_params=params)

    def run(*args):
        res = call(*args, *(comm.ins if comm is not None else ()))
        if comm is not None:
            comm.done(res[n_out:])
        return res[0] if single else tuple(res[:n_out])

    return run


def _both(a, b):
    def split(refs, na):
        return refs[:na], refs[na:]

    def run(which):
        def go(ins, outs, sems):
            for comm, i, o, s in zip((a, b), split(ins, len(a.ins)), split(outs, len(a.out_shapes)), split(sems, len(a.sems))):
                getattr(comm, which)(i, o, s)
        return go

    def done(res):
        a.done(res[:len(a.out_shapes)])
        b.done(res[len(a.out_shapes):])

    aliases = dict(a.aliases)
    aliases.update({len(a.ins) + i: len(a.out_shapes) + o for i, o in b.aliases.items()})
    return _Comm(a.ins + b.ins, a.out_shapes + b.out_shapes, aliases, a.sems + b.sems, run("start"), run("finish"), done)


def _run_comm(comm, name):
    def body(*refs):
        n_ci, n_co = len(comm.ins), len(comm.out_shapes)
        c_in, c_out, c_sem = refs[:n_ci], refs[n_ci:n_ci + n_co], refs[n_ci + n_co:]
        comm.start(c_in, c_out, c_sem)
        comm.finish(c_in, c_out, c_sem)

    res = pl.pallas_call(body, name=name, out_shape=tuple(comm.out_shapes), in_specs=[ANY] * len(comm.ins),
                         out_specs=tuple([ANY] * len(comm.out_shapes)), scratch_shapes=comm.sems,
                         input_output_aliases=comm.aliases)(*comm.ins)
    comm.done(res)


def _dot(a, b):
    return lax.dot_general(a, b, (((1,), (0,)), ((), ())), preferred_element_type=F32)


def _dot_nt(a, b):
    return lax.dot_general(a, b, (((1,), (1,)), ((), ())), preferred_element_type=F32)


def _dot_tn(a, b):
    return lax.dot_general(a, b, (((0,), (0,)), ((), ())), preferred_element_type=F32)


def _sigmoid(x):
    return 1.0 / (1.0 + jnp.exp(-x))


def _rstd(x):
    return lax.rsqrt(jnp.mean(x * x, axis=-1, keepdims=True) + EPS)


def _rms_fwd(x, g, name, comm=None):
    s, d = x.shape
    tm = ROW_CHUNK

    def body(x_ref, g_ref, o_ref):
        xv = x_ref[...]
        o_ref[...] = (xv * _rstd(xv) * g_ref[...]).astype(BF16)

    return _pcall(body, name=name, out_shape=_sds((s, d), BF16), grid=(s // tm,),
                  in_specs=[pl.BlockSpec((tm, d), lambda i: (i, 0)), pl.BlockSpec((1, d), lambda i: (0, 0))],
                  out_specs=pl.BlockSpec((tm, d), lambda i: (i, 0)), comm=comm)(x, g)


def _rms_bwd(x, g, dh, dres, name):
    s, d = x.shape
    tm = ROW_CHUNK

    def body(x_ref, g_ref, dh_ref, dres_ref, dx_ref, dxb_ref, dg_ref):
        i = pl.program_id(0)
        xv = x_ref[...]
        r = _rstd(xv)
        xh = xv * r
        dhv = dh_ref[...]
        gd = dhv * g_ref[...]
        dx = dres_ref[...] + r * (gd - xh * jnp.mean(gd * xh, axis=-1, keepdims=True))
        dx_ref[...] = dx
        dxb_ref[...] = dx.astype(BF16)
        part = jnp.sum(dhv * xh, axis=0, keepdims=True)

        @pl.when(i == 0)
        def _():
            dg_ref[...] = part

        @pl.when(i > 0)
        def _():
            dg_ref[...] += part

    row = pl.BlockSpec((tm, d), lambda i: (i, 0))
    vec = pl.BlockSpec((1, d), lambda i: (0, 0))
    return _pcall(body, name=name, out_shape=(_sds((s, d), F32), _sds((s, d), BF16), _sds((1, d), F32)),
                  grid=(s // tm,), in_specs=[row, vec, row, row], out_specs=(row, row, vec))(x, g, dh, dres)


def _final_loss(x, g, tgt, name):
    s, d = x.shape
    tm = ROW_CHUNK

    def body(x_ref, g_ref, t_ref, loss_ref, dx_ref, dxb_ref, dg_ref):
        i = pl.program_id(0)
        xv = x_ref[...]
        r = _rstd(xv)
        xh = xv * r
        e = xh * g_ref[...] - t_ref[...]
        lpart = 0.5 * jnp.sum(jnp.mean(e * e, axis=-1, keepdims=True), axis=0, keepdims=True)
        dy = e * (1.0 / d)
        gd = dy * g_ref[...]
        dx = r * (gd - xh * jnp.mean(gd * xh, axis=-1, keepdims=True))
        dx_ref[...] = dx
        dxb_ref[...] = dx.astype(BF16)
        part = jnp.sum(dy * xh, axis=0, keepdims=True)
        lrow = jnp.broadcast_to(lpart, (1, LANES))

        @pl.when(i == 0)
        def _():
            dg_ref[...] = part
            loss_ref[...] = lrow

        @pl.when(i > 0)
        def _():
            dg_ref[...] += part
            loss_ref[...] += lrow

    row = pl.BlockSpec((tm, d), lambda i: (i, 0))
    vec = pl.BlockSpec((1, d), lambda i: (0, 0))
    return _pcall(body, name=name,
                  out_shape=(_sds((1, LANES), F32), _sds((s, d), F32), _sds((s, d), BF16), _sds((1, d), F32)),
                  grid=(s // tm,), in_specs=[row, vec, row],
                  out_specs=(pl.BlockSpec((1, LANES), lambda i: (0, 0)), row, row, vec))(x, g, tgt)


def _mm_n(a, b, layer, *, nt, tn, out_dtype, name, resid=None, b_part=0, comm=None):
    s, k = a.shape
    n = b.shape[1] if nt else b.shape[2]
    rows = 512

    def body(a_ref, b_ref, *refs):
        o_ref = refs[-1]
        bv = b_ref[...]
        for r0 in range(0, s, rows):
            av = a_ref[r0:r0 + rows, :]
            prod = _dot_nt(av, bv) if nt else _dot(av, bv)
            if resid is not None:
                prod = refs[0][r0:r0 + rows, :] + prod
            o_ref[r0:r0 + rows, :] = prod.astype(out_dtype)

    b_spec = (pl.BlockSpec((None, tn, k), lambda j: (layer, j, b_part)) if nt
              else pl.BlockSpec((None, k, tn), lambda j: (layer, b_part, j)))
    col = pl.BlockSpec((s, tn), lambda j: (0, j))
    extra = () if resid is None else (resid,)
    return _pcall(body, name=name, out_shape=_sds((s, n), out_dtype), grid=(n // tn,),
                  in_specs=[pl.BlockSpec((s, k), lambda j: (0, 0)), b_spec] + [col] * len(extra),
                  out_specs=col, comm=comm)(a, b, *extra)


def _mm_tn(a, b, *, t, name):
    s, ka = a.shape
    n = b.shape[1]

    def body(a_ref, b_ref, o_ref):
        o_ref[...] = _dot_tn(a_ref[...], b_ref[...]).astype(BF16)

    return _pcall(body, name=name, out_shape=_sds((ka, n), BF16), grid=(ka // t,),
                  in_specs=[pl.BlockSpec((s, t), lambda i: (0, i)), pl.BlockSpec((s, n), lambda i: (0, 0))],
                  out_specs=pl.BlockSpec((t, n), lambda i: (i, 0)))(a, b)


def _mm_tn_pieces(pieces, b, *, t, name):
    s, n = b.shape
    blocks = [p.shape[1] // t for p in pieces]
    starts = [sum(blocks[:i]) for i in range(len(pieces))]

    def body(*refs):
        p_refs, b_ref, o_ref = refs[:len(pieces)], refs[len(pieces)], refs[len(pieces) + 1]
        j = pl.program_id(0)
        for p_ref, start, count in zip(p_refs, starts, blocks):
            @pl.when((j >= start) & (j < start + count))
            def _(p_ref=p_ref):
                o_ref[...] = _dot_tn(p_ref[...], b_ref[...]).astype(BF16)

    specs = [pl.BlockSpec((s, t), lambda j, start=start, count=count: (0, jnp.clip(j - start, 0, count - 1)))
             for start, count in zip(starts, blocks)]
    return _pcall(body, name=name, out_shape=_sds((sum(blocks) * t, n), BF16), grid=(sum(blocks),),
                  in_specs=specs + [pl.BlockSpec((s, n), lambda j: (0, 0))],
                  out_specs=pl.BlockSpec((t, n), lambda j: (j, 0)))(*pieces, b)


def _mm_tn2(a, b_lo, b_hi, *, t, name):
    s, ka = a.shape
    half = b_lo.shape[1]
    nb = half // t

    def body(a_ref, lo_ref, hi_ref, o_ref):
        j = pl.program_id(0)

        @pl.when(j < nb)
        def _():
            o_ref[...] = _dot_tn(a_ref[...], lo_ref[...]).astype(BF16)

        @pl.when(j >= nb)
        def _():
            o_ref[...] = _dot_tn(a_ref[...], hi_ref[...]).astype(BF16)

    return _pcall(body, name=name, out_shape=_sds((ka, 2 * half), BF16), grid=(2 * nb,),
                  in_specs=[pl.BlockSpec((s, ka), lambda j: (0, 0)),
                            pl.BlockSpec((s, t), lambda j: (0, jnp.minimum(j, nb - 1))),
                            pl.BlockSpec((s, t), lambda j: (0, jnp.maximum(j - nb, 0)))],
                  out_specs=pl.BlockSpec((ka, t), lambda j: (0, j)))(a, b_lo, b_hi)


SUBLANES = 8


def _tap_windows(win, width, lead, rows):
    offs = [lead + k for k in range(width)]
    if width <= SUBLANES:
        return [win[o:o + rows, :] for o in offs]
    n = win.shape[0]
    out = {}
    for r in sorted({o % SUBLANES for o in offs}):
        base = win if r == 0 else pltpu.roll(win, n - r, axis=0)
        for o in offs:
            if o % SUBLANES == r:
                out[o - lead] = base[o - r:o - r + rows, :]
    return [out[k] for k in range(width)]


def _conv_taps(taps, w_ref):
    acc = None
    for k, tap in enumerate(taps):
        term = w_ref[pl.ds(k, 1), :] * tap
        acc = term if acc is None else acc + term
    return acc


def _causal_taps(win, width, pad, rows):
    return _tap_windows(win, width, pad - (width - 1), rows)


def _anticausal_taps(win, width, rows):
    return _tap_windows(win, width, 0, rows)[::-1]


def _conv_wgrad(dw_ref, g, taps):
    for k, tap in enumerate(taps):
        dw_ref[pl.ds(k, 1), :] += jnp.sum(g * tap, axis=0, keepdims=True)


def _mixer_a_fwd(ab, taps_t, wa_ref):
    ct = _conv_taps(taps_t, wa_ref)
    return ab * ct, ct


def _mixer_c_fwd(taps_u, wc_ref, cb_ref, lg_ref, lb_ref):
    u = _conv_taps(taps_u, wc_ref) + cb_ref[...]
    mu = jnp.mean(u, axis=-1, keepdims=True)
    uc = u - mu
    rs = lax.rsqrt(jnp.mean(uc * uc, axis=-1, keepdims=True) + EPS)
    uh = uc * rs
    ln = uh * lg_ref[...] + lb_ref[...]
    sg = _sigmoid(ln)
    return ln * sg, ln, sg, uh, rs


def _mix_fwd(z, wa, wc, cb, lg, lb, ga, gc, name):
    s = z.shape[0]
    w = wa.shape[1]
    nblk = z.shape[1] // w
    rc = ROW_CHUNK

    def body(ah_ref, ab_ref, ac_ref, cv_ref, cg_ref, wa_ref, wc_ref, cb_ref, lg_ref, lb_ref, ga_ref, gc_ref,
             ya_ref, yc_ref, tpad, upad):
        tpad[pl.ds(0, PAD_SHORT), :] = jnp.zeros((PAD_SHORT, w), F32)
        upad[pl.ds(0, PAD_LONG), :] = jnp.zeros((PAD_LONG, w), F32)

        def chunk(i, carry):
            base = pl.multiple_of(i * rc, rc)
            rows = pl.ds(base, rc)
            ah, ab, ac = ah_ref[rows, :], ab_ref[rows, :], ac_ref[rows, :]
            tpad[pl.ds(base + PAD_SHORT, rc), :] = ac * ah
            ya, _ = _mixer_a_fwd(ab, _causal_taps(tpad[pl.ds(base, rc + PAD_SHORT), :], SHORT_CONV, PAD_SHORT, rc), wa_ref)
            ya_ref[rows, :] = (ya * _rstd(ya) * ga_ref[...]).astype(BF16)
            upad[pl.ds(base + PAD_LONG, rc), :] = cv_ref[rows, :] * _sigmoid(cg_ref[rows, :])
            taps_u = _causal_taps(upad[pl.ds(base, rc + PAD_LONG), :], CONFORMER_CONV, PAD_LONG, rc)
            yc = _mixer_c_fwd(taps_u, wc_ref, cb_ref, lg_ref, lb_ref)[0]
            yc_ref[rows, :] = (yc * _rstd(yc) * gc_ref[...]).astype(BF16)
            return carry

        lax.fori_loop(0, s // rc, chunk, 0)

    def zblk(j):
        return pl.BlockSpec((s, w), lambda i: (0, j))

    def whole(a):
        return pl.BlockSpec(a.shape, lambda i: (0, 0))

    return _pcall(
        body, name=name, out_shape=(_sds((s, w), BF16), _sds((s, w), BF16)), grid=(1,),
        in_specs=[zblk(0), zblk(1), zblk(2), zblk(nblk - 2), zblk(nblk - 1)] + [whole(a) for a in (wa, wc, cb, lg, lb, ga, gc)],
        out_specs=(pl.BlockSpec((s, w), lambda i: (0, 0)), pl.BlockSpec((s, w), lambda i: (0, 0))),
        scratch_shapes=[pltpu.VMEM((s + PAD_SHORT, w), F32), pltpu.VMEM((s + PAD_LONG, w), F32)],
    )(z, z, z, z, z, wa, wc, cb, lg, lb, ga, gc)


def _mix_bwd(z, dy, wa, wc, cb, lg, lb, ga, gc, name):
    s = z.shape[0]
    w = wa.shape[1]
    nblk = z.shape[1] // w
    nyb = dy.shape[1] // w
    rc = ROW_CHUNK

    def body(ah_ref, ab_ref, ac_ref, cv_ref, cg_ref, dya_ref, dyc_ref,
             wa_ref, wc_ref, cb_ref, lg_ref, lb_ref, ga_ref, gc_ref,
             dza_ref, dzc_ref, dwa_ref, dwc_ref, dcb_ref, dlg_ref, dlb_ref, dga_ref, dgc_ref,
             tpad, upad, dctp, dup):
        tpad[pl.ds(0, PAD_SHORT), :] = jnp.zeros((PAD_SHORT, w), F32)
        upad[pl.ds(0, PAD_LONG), :] = jnp.zeros((PAD_LONG, w), F32)
        dctp[pl.ds(s, PAD_SHORT), :] = jnp.zeros((PAD_SHORT, w), F32)
        dup[pl.ds(s, PAD_LONG), :] = jnp.zeros((PAD_LONG, w), F32)
        for ref in (dwa_ref, dwc_ref, dcb_ref, dlg_ref, dlb_ref, dga_ref, dgc_ref):
            ref[...] = jnp.zeros(ref.shape, F32)

        def rms_bwd(y, g_ref, dyn, dg_ref):
            r = _rstd(y)
            yh = y * r
            gd = dyn * g_ref[...]
            dg_ref[...] += jnp.sum(dyn * yh, axis=0, keepdims=True)
            return r * (gd - yh * jnp.mean(gd * yh, axis=-1, keepdims=True))

        def first(i, carry):
            base = pl.multiple_of(i * rc, rc)
            rows = pl.ds(base, rc)
            ah, ab, ac = ah_ref[rows, :], ab_ref[rows, :], ac_ref[rows, :]
            tpad[pl.ds(base + PAD_SHORT, rc), :] = ac * ah
            taps_t = _causal_taps(tpad[pl.ds(base, rc + PAD_SHORT), :], SHORT_CONV, PAD_SHORT, rc)
            ya, ct = _mixer_a_fwd(ab, taps_t, wa_ref)
            dya = rms_bwd(ya, ga_ref, dya_ref[rows, :], dga_ref)
            dza_ref[rows, w:2 * w] = (dya * ct).astype(BF16)
            dct = dya * ab
            dctp[rows, :] = dct
            _conv_wgrad(dwa_ref, dct, taps_t)

            upad[pl.ds(base + PAD_LONG, rc), :] = cv_ref[rows, :] * _sigmoid(cg_ref[rows, :])
            taps_u = _causal_taps(upad[pl.ds(base, rc + PAD_LONG), :], CONFORMER_CONV, PAD_LONG, rc)
            yc, ln, sg, uh, rs = _mixer_c_fwd(taps_u, wc_ref, cb_ref, lg_ref, lb_ref)
            dyc = rms_bwd(yc, gc_ref, dyc_ref[rows, :], dgc_ref)
            dln = dyc * (sg * (1.0 + ln * (1.0 - sg)))
            dlg_ref[...] += jnp.sum(dln * uh, axis=0, keepdims=True)
            dlb_ref[...] += jnp.sum(dln, axis=0, keepdims=True)
            duh = dln * lg_ref[...]
            du = rs * (duh - jnp.mean(duh, axis=-1, keepdims=True) - uh * jnp.mean(duh * uh, axis=-1, keepdims=True))
            dcb_ref[...] += jnp.sum(du, axis=0, keepdims=True)
            dup[rows, :] = du
            _conv_wgrad(dwc_ref, du, taps_u)
            return carry

        lax.fori_loop(0, s // rc, first, 0)

        def second(i, carry):
            base = pl.multiple_of(i * rc, rc)
            rows = pl.ds(base, rc)
            dt = _conv_taps(_anticausal_taps(dctp[pl.ds(base, rc + PAD_SHORT), :], SHORT_CONV, rc), wa_ref)
            dza_ref[rows, 0:w] = (dt * ac_ref[rows, :]).astype(BF16)
            dza_ref[rows, 2 * w:3 * w] = (dt * ah_ref[rows, :]).astype(BF16)
            du0 = _conv_taps(_anticausal_taps(dup[pl.ds(base, rc + PAD_LONG), :], CONFORMER_CONV, rc), wc_ref)
            sg = _sigmoid(cg_ref[rows, :])
            dzc_ref[rows, 0:w] = (du0 * sg).astype(BF16)
            dzc_ref[rows, w:2 * w] = (du0 * cv_ref[rows, :] * sg * (1.0 - sg)).astype(BF16)
            return carry

        lax.fori_loop(0, s // rc, second, 0)

    def blk(j):
        return pl.BlockSpec((s, w), lambda i: (0, j))

    def whole(a):
        return pl.BlockSpec(tuple(a.shape), lambda i: (0, 0))

    params = (wa, wc, cb, lg, lb, ga, gc)
    outs = (_sds((s, 3 * w), BF16), _sds((s, 2 * w), BF16)) + tuple(_sds(p.shape, F32) for p in params)
    return _pcall(
        body, name=name, out_shape=outs, grid=(1,),
        in_specs=[blk(0), blk(1), blk(2), blk(nblk - 2), blk(nblk - 1), blk(0), blk(nyb - 1)] + [whole(p) for p in params],
        out_specs=tuple(whole(o) for o in outs),
        scratch_shapes=[pltpu.VMEM((s + PAD_SHORT, w), F32), pltpu.VMEM((s + PAD_LONG, w), F32),
                        pltpu.VMEM((s + PAD_SHORT, w), F32), pltpu.VMEM((s + PAD_LONG, w), F32)],
    )(z, z, z, z, z, dy, dy, *params)


def _ffn_act_fwd(up, wf, name, comm=None):
    s, f2 = up.shape
    f = f2 // 2
    tc = 256
    nb = f // tc
    rc = ROW_CHUNK

    def body(g_ref, v_ref, wg_ref, wv_ref, o_ref, gpad, vpad):
        gpad[pl.ds(0, PAD_SHORT), :] = jnp.zeros((PAD_SHORT, tc), F32)
        vpad[pl.ds(0, PAD_SHORT), :] = jnp.zeros((PAD_SHORT, tc), F32)

        def chunk(i, carry):
            base = pl.multiple_of(i * rc, rc)
            rows = pl.ds(base, rc)
            gpad[pl.ds(base + PAD_SHORT, rc), :] = g_ref[rows, :].astype(F32)
            vpad[pl.ds(base + PAD_SHORT, rc), :] = v_ref[rows, :].astype(F32)
            gc = _conv_taps(_causal_taps(gpad[pl.ds(base, rc + PAD_SHORT), :], FFN_CONV, PAD_SHORT, rc), wg_ref)
            vc = _conv_taps(_causal_taps(vpad[pl.ds(base, rc + PAD_SHORT), :], FFN_CONV, PAD_SHORT, rc), wv_ref)
            o_ref[rows, :] = (gc * _sigmoid(gc) * vc).astype(BF16)
            return carry

        lax.fori_loop(0, s // rc, chunk, 0)

    return _pcall(
        body, name=name, out_shape=_sds((s, f), BF16), grid=(nb,),
        in_specs=[pl.BlockSpec((s, tc), lambda j: (0, j)), pl.BlockSpec((s, tc), lambda j: (0, j + nb)),
                  pl.BlockSpec((FFN_CONV, tc), lambda j: (0, j)), pl.BlockSpec((FFN_CONV, tc), lambda j: (0, j + nb))],
        out_specs=pl.BlockSpec((s, tc), lambda j: (0, j)),
        scratch_shapes=[pltpu.VMEM((s + PAD_SHORT, tc), F32), pltpu.VMEM((s + PAD_SHORT, tc), F32)], comm=comm,
    )(up, up, wf, wf)


def _ffn_act_bwd(up, dact, wf, name, comm=None):
    s, f2 = up.shape
    f = f2 // 2
    tc = 256
    nb = f // tc
    rc = ROW_CHUNK

    def body(g_ref, v_ref, da_ref, wg_ref, wv_ref, act_ref, dg_ref, dv_ref, dwg_ref, dwv_ref, gpad, vpad, dgp, dvp):
        gpad[pl.ds(0, PAD_SHORT), :] = jnp.zeros((PAD_SHORT, tc), F32)
        vpad[pl.ds(0, PAD_SHORT), :] = jnp.zeros((PAD_SHORT, tc), F32)
        dgp[pl.ds(s, PAD_SHORT), :] = jnp.zeros((PAD_SHORT, tc), F32)
        dvp[pl.ds(s, PAD_SHORT), :] = jnp.zeros((PAD_SHORT, tc), F32)
        dwg_ref[...] = jnp.zeros((FFN_CONV, tc), F32)
        dwv_ref[...] = jnp.zeros((FFN_CONV, tc), F32)

        def first(i, carry):
            base = pl.multiple_of(i * rc, rc)
            rows = pl.ds(base, rc)
            gpad[pl.ds(base + PAD_SHORT, rc), :] = g_ref[rows, :].astype(F32)
            vpad[pl.ds(base + PAD_SHORT, rc), :] = v_ref[rows, :].astype(F32)
            taps_g = _causal_taps(gpad[pl.ds(base, rc + PAD_SHORT), :], FFN_CONV, PAD_SHORT, rc)
            taps_v = _causal_taps(vpad[pl.ds(base, rc + PAD_SHORT), :], FFN_CONV, PAD_SHORT, rc)
            gc = _conv_taps(taps_g, wg_ref)
            vc = _conv_taps(taps_v, wv_ref)
            sg = _sigmoid(gc)
            silu = gc * sg
            act_ref[rows, :] = (silu * vc).astype(BF16)
            da = da_ref[rows, :].astype(F32)
            dgc = da * vc * (sg * (1.0 + gc * (1.0 - sg)))
            dvc = da * silu
            dgp[rows, :] = dgc
            dvp[rows, :] = dvc
            _conv_wgrad(dwg_ref, dgc, taps_g)
            _conv_wgrad(dwv_ref, dvc, taps_v)
            return carry

        lax.fori_loop(0, s // rc, first, 0)

        def second(i, carry):
            base = pl.multiple_of(i * rc, rc)
            rows = pl.ds(base, rc)
            dg_ref[rows, :] = _conv_taps(_anticausal_taps(dgp[pl.ds(base, rc + PAD_SHORT), :], FFN_CONV, rc), wg_ref).astype(BF16)
            dv_ref[rows, :] = _conv_taps(_anticausal_taps(dvp[pl.ds(base, rc + PAD_SHORT), :], FFN_CONV, rc), wv_ref).astype(BF16)
            return carry

        lax.fori_loop(0, s // rc, second, 0)

    lo = pl.BlockSpec((s, tc), lambda j: (0, j))
    hi = pl.BlockSpec((s, tc), lambda j: (0, j + nb))
    wlo = pl.BlockSpec((FFN_CONV, tc), lambda j: (0, j))
    whi = pl.BlockSpec((FFN_CONV, tc), lambda j: (0, j + nb))
    act, dgate, dval, dwg, dwv = _pcall(
        body, name=name,
        out_shape=(_sds((s, f), BF16), _sds((s, f), BF16), _sds((s, f), BF16), _sds((FFN_CONV, f), F32), _sds((FFN_CONV, f), F32)),
        grid=(nb,), in_specs=[lo, hi, lo, wlo, whi], out_specs=(lo, lo, lo, wlo, wlo),
        scratch_shapes=[pltpu.VMEM((s + PAD_SHORT, tc), F32) for _ in range(4)], comm=comm,
    )(up, up, dact, wf, wf)
    return act, dgate, dval, jnp.concatenate([dwg, dwv], axis=1)


def _out_proj(yan, yb, ycn, gb, x, w_out, layer, g_next, name, comm=None):
    s, w = yan.shape
    wb = yb.shape[1]
    d = x.shape[1]
    tm = ROW_CHUNK

    def body(ya_ref, yb_ref, yc_ref, gb_ref, x_ref, w_ref, g_ref, y_ref, xm_ref, h_ref):
        ybv = yb_ref[...]
        y = jnp.concatenate([ya_ref[...], (ybv * _rstd(ybv) * gb_ref[...]).astype(BF16), yc_ref[...]], axis=1)
        y_ref[...] = y
        xm = x_ref[...] + _dot(y, w_ref[...])
        xm_ref[...] = xm
        h_ref[...] = (xm * _rstd(xm) * g_ref[...]).astype(BF16)

    def rows(width):
        return pl.BlockSpec((tm, width), lambda i: (i, 0))

    def vec(width):
        return pl.BlockSpec((1, width), lambda i: (0, 0))

    return _pcall(body, name=name, out_shape=(_sds((s, d), BF16), _sds((s, d), F32), _sds((s, d), BF16)), grid=(s // tm,),
                  in_specs=[rows(w), rows(wb), rows(w), vec(wb), rows(d), pl.BlockSpec((None, d, d), lambda i: (layer, 0, 0)), vec(d)],
                  out_specs=(rows(d), rows(d), rows(d)), comm=comm)(yan, yb, ycn, gb, x, w_out, g_next)


def _down_proj(act, w_down, layer, x_mid, g_next, name, comm=None):
    s, f = act.shape
    d = x_mid.shape[1]
    tm = ROW_CHUNK

    def body(a_ref, w_ref, x_ref, *refs):
        xo = x_ref[...] + _dot(a_ref[...], w_ref[...])
        refs[-2 if g_next is not None else -1][...] = xo
        if g_next is not None:
            refs[-1][...] = (xo * _rstd(xo) * refs[0][...]).astype(BF16)

    row = pl.BlockSpec((tm, d), lambda i: (i, 0))
    ins = [act, w_down, x_mid] + ([g_next] if g_next is not None else [])
    in_specs = [pl.BlockSpec((tm, f), lambda i: (i, 0)), pl.BlockSpec((None, f, d), lambda i: (layer, 0, 0)), row]
    in_specs += [pl.BlockSpec((1, d), lambda i: (0, 0))] if g_next is not None else []
    outs = (_sds((s, d), F32), _sds((s, d), BF16)) if g_next is not None else (_sds((s, d), F32),)
    res = _pcall(body, name=name, out_shape=outs, grid=(s // tm,), in_specs=in_specs, out_specs=tuple([row] * len(outs)),
                 comm=comm)(*ins)
    return (res[0], res[1]) if g_next is not None else (res[0], None)


def _proj_dx(pieces, w, layer, nt, x, g, dres, name, comm=None):
    s, d = x.shape
    tm = ROW_CHUNK
    widths = [p.shape[1] for p in pieces]

    def body(*refs):
        p_refs, (w_ref, x_ref, g_ref, dres_ref, dx_ref, dxb_ref, dg_ref) = refs[:len(pieces)], refs[len(pieces):]
        i = pl.program_id(0)
        dh, off = None, 0
        for p_ref, width in zip(p_refs, widths):
            part = _dot_nt(p_ref[...], w_ref[:, off:off + width]) if nt else _dot(p_ref[...], w_ref[off:off + width, :])
            dh = part if dh is None else dh + part
            off += width
        xv = x_ref[...]
        r = _rstd(xv)
        xh = xv * r
        gd = dh * g_ref[...]
        dx = dres_ref[...] + r * (gd - xh * jnp.mean(gd * xh, axis=-1, keepdims=True))
        dx_ref[...] = dx
        dxb_ref[...] = dx.astype(BF16)
        part = jnp.sum(dh * xh, axis=0, keepdims=True)

        @pl.when(i == 0)
        def _():
            dg_ref[...] = part

        @pl.when(i > 0)
        def _():
            dg_ref[...] += part

    row = pl.BlockSpec((tm, d), lambda i: (i, 0))
    vec = pl.BlockSpec((1, d), lambda i: (0, 0))
    w_spec = pl.BlockSpec((None,) + w.shape[1:], lambda i: (layer, 0, 0))
    return _pcall(body, name=name, out_shape=(_sds((s, d), F32), _sds((s, d), BF16), _sds((1, d), F32)), grid=(s // tm,),
                  in_specs=[pl.BlockSpec((tm, width), lambda i: (i, 0)) for width in widths] + [w_spec, row, vec, row],
                  out_specs=(row, row, vec), comm=comm)(*pieces, w, x, g, dres)


def _yb_norm_bwd(yb, dy, gb, name):
    s, wb = yb.shape
    w = wb // 2
    heads = wb // D_HEAD
    tm = ROW_CHUNK

    def body(yb_ref, d1_ref, d2_ref, g_ref, dyb_ref, dl_ref, dg_ref):
        i = pl.program_id(0)
        y = yb_ref[...]
        dyn = jnp.concatenate([d1_ref[...], d2_ref[...]], axis=1)
        r = _rstd(y)
        yh = y * r
        gd = dyn * g_ref[...]
        dyb = r * (gd - yh * jnp.mean(gd * yh, axis=-1, keepdims=True))
        dyb_ref[...] = dyb
        part = jnp.sum(dyn * yh, axis=0, keepdims=True)
        prod = dyb * y
        even = lax.broadcasted_iota(I32, (tm, LANES), 1) < D_HEAD
        for p in range(heads // 2):
            blk = prod[:, p * LANES:(p + 1) * LANES]
            ev = jnp.sum(jnp.where(even, blk, 0.0), axis=1, keepdims=True)
            od = jnp.sum(jnp.where(even, 0.0, blk), axis=1, keepdims=True)
            dl_ref[2 * p] = jnp.broadcast_to(ev, (tm, LANES))
            dl_ref[2 * p + 1] = jnp.broadcast_to(od, (tm, LANES))

        @pl.when(i == 0)
        def _():
            dg_ref[...] = part

        @pl.when(i > 0)
        def _():
            dg_ref[...] += part

    return _pcall(
        body, name=name, out_shape=(_sds((s, wb), F32), _sds((heads, s, LANES), F32), _sds((1, wb), F32)),
        grid=(s // tm,),
        in_specs=[pl.BlockSpec((tm, wb), lambda i: (i, 0)), pl.BlockSpec((tm, w), lambda i: (i, 1)),
                  pl.BlockSpec((tm, w), lambda i: (i, 2)), pl.BlockSpec((1, wb), lambda i: (0, 0))],
        out_specs=(pl.BlockSpec((tm, wb), lambda i: (i, 0)), pl.BlockSpec((heads, tm, LANES), lambda i: (0, i, 0)),
                   pl.BlockSpec((1, wb), lambda i: (0, 0))),
    )(yb, dy, dy, gb)


def _t5_bucket_table():
    max_exact = NUM_BUCKETS // 2
    out = np.full((len(DILATED_BRANCHES), BLK, 2 * BLK), -1, np.int32)
    rel = np.arange(BLK)[:, None] - np.arange(2 * BLK)[None, :] + BLK
    for b, (window, dilation) in enumerate(DILATED_BRANCHES):
        n_keys = window // dilation
        dist = np.maximum(rel, 0) * dilation
        d_f = np.maximum(dist, 1).astype(np.float32)
        large = max_exact + (np.log(d_f / np.float32(max_exact)) / np.float32(math.log(MAX_DISTANCE / max_exact))
                             * np.float32(NUM_BUCKETS - max_exact)).astype(np.int32)
        large = np.minimum(large, NUM_BUCKETS - 1)
        bucket = np.where(dist < max_exact, dist, large)
        out[b] = np.where((rel >= 0) & (rel <= n_keys), bucket, -1)
    return out


def _bias_tiles(rel_bias, buckets, name):
    nbk, heads = rel_bias.shape
    nbr = buckets.shape[0]

    def body(rb_ref, bk_ref, o_ref):
        for br in range(nbr):
            bk = bk_ref[br]
            tiles = [jnp.full((BLK, 2 * BLK), NEG, F32) for _ in range(heads)]
            for b in range(nbk):
                hit = bk == b
                tiles = [jnp.where(hit, rb_ref[b, h], tiles[h]) for h in range(heads)]
            for h in range(heads):
                o_ref[br, h] = tiles[h]

    return _pcall(body, name=name, out_shape=_sds((nbr, heads, BLK, 2 * BLK), F32), grid=(1,),
                  in_specs=[pl.BlockSpec(memory_space=pltpu.SMEM), pl.BlockSpec(buckets.shape, lambda i: (0, 0, 0))],
                  out_specs=pl.BlockSpec((nbr, heads, BLK, 2 * BLK), lambda i: (0, 0, 0, 0)))(rel_bias, buckets)


def _bias_grad(dtiles, buckets, nbk, name):
    nbr, heads = dtiles.shape[:2]

    def body(dt_ref, bk_ref, o_ref):
        row = lax.broadcasted_iota(I32, (nbk, LANES), 0)
        col = lax.broadcasted_iota(I32, (nbk, LANES), 1)
        out = jnp.zeros((nbk, LANES), F32)
        for h in range(heads):
            for b in range(nbk):
                tot = jnp.zeros((), F32)
                for br in range(nbr):
                    tot = tot + jnp.sum(jnp.where(bk_ref[br] == b, dt_ref[br, h], 0.0))
                out = jnp.where((row == b) & (col == h), tot, out)
        o_ref[...] = out

    return _pcall(body, name=name, out_shape=_sds((nbk, LANES), F32), grid=(1,),
                  in_specs=[pl.BlockSpec(dtiles.shape, lambda i: (0, 0, 0, 0)), pl.BlockSpec(buckets.shape, lambda i: (0, 0, 0))],
                  out_specs=pl.BlockSpec((nbk, LANES), lambda i: (0, 0)))(dtiles, buckets)


def _largest_divisor(n, cap):
    return max(g for g in range(1, cap + 1) if n % g == 0)


def _attn_blocks(s, visit, group):
    for br, (window, d) in enumerate(DILATED_BRANCHES):
        n_blk = (s // d) // BLK
        span = BLK * d
        g1 = _largest_divisor(d, group)

        def firsts(t, carry, br=br, d=d, g1=g1):
            for j in range(g1):
                visit(br, d, t * g1 + j, False)
            return carry

        lax.fori_loop(0, d // g1, firsts, 0)
        if n_blk > 1:
            total = d * (n_blk - 1)
            g2 = _largest_divisor(total, group)

            def rest(t, carry, br=br, d=d, n_blk=n_blk, span=span, g2=g2):
                for j in range(g2):
                    idx = t * g2 + j
                    visit(br, d, idx // (n_blk - 1) + (1 + idx % (n_blk - 1)) * span, True)
                return carry

            lax.fori_loop(0, total // g2, rest, 0)


def _rows(start, size, d):
    return pl.ds(pl.multiple_of(start, BLK), size) if d == 1 else pl.ds(start, size, stride=d)


def _attn_fwd(z, btiles, col0, name, comm=None):
    s = z.shape[0]
    nbr, heads = btiles.shape[:2]
    pairs = heads // 2
    scale = D_HEAD ** -0.5
    rc = ROW_CHUNK

    def body(q_ref, k_ref, v_ref, bt_ref, yb_ref, lse_ref, acc_ref, m_ref, l_ref):
        even = lax.broadcasted_iota(I32, (BLK, LANES), 1) < D_HEAD
        even2 = lax.broadcasted_iota(I32, (2 * BLK, LANES), 1) < D_HEAD

        def visit(br, d, start, prev):
            kw = 2 * BLK if prev else BLK
            rows_q = _rows(start, BLK, d)
            rows_k = _rows(start - BLK * d, kw, d) if prev else rows_q
            qb = q_ref[rows_q, :]
            kb = k_ref[rows_k, :].astype(BF16)
            vw = v_ref[rows_k, :]
            ev_k = even2 if prev else even
            qm = jnp.concatenate([jnp.where(even, qb, 0.0), jnp.where(even, 0.0, qb)], axis=0).astype(BF16)
            bias = [bt_ref[br, e] if prev else bt_ref[br, e, :, BLK:] for e in range(2)]
            sc = _dot_nt(qm, kb) * scale + jnp.concatenate(bias, axis=0)
            m = jnp.max(sc, axis=1, keepdims=True)
            p = jnp.exp(sc - m)
            l = jnp.sum(p, axis=1, keepdims=True)
            pb = p.astype(BF16)
            vm = jnp.concatenate([jnp.where(ev_k, vw, 0.0), jnp.where(ev_k, 0.0, vw)], axis=0).astype(BF16)
            acc_ref.at[br][rows_q, :] = _dot(jnp.concatenate([pb[:BLK], pb[BLK:]], axis=1), vm)
            for e in range(2):
                m_ref.at[br, e][rows_q, :] = jnp.broadcast_to(m[e * BLK:(e + 1) * BLK], (BLK, LANES))
                l_ref.at[br, e][rows_q, :] = jnp.broadcast_to(l[e * BLK:(e + 1) * BLK], (BLK, LANES))

        _attn_blocks(s, visit, ATTN_GROUP_FWD)

        ev_c = lax.broadcasted_iota(I32, (rc, LANES), 1) < D_HEAD

        def merge(i, carry):
            rows = pl.ds(pl.multiple_of(i * rc, rc), rc)
            wts, dens = [], []
            for e in range(2):
                ms = [m_ref[br, e, rows, :] for br in range(nbr)]
                top = functools.reduce(jnp.maximum, ms)
                w = [jnp.exp(mb - top) for mb in ms]
                den = functools.reduce(lambda a, b: a + b, [w[br] * l_ref[br, e, rows, :] for br in range(nbr)])
                lse_ref[e, rows, :] = top + jnp.log(den)
                wts.append(w)
                dens.append(den)
            num = functools.reduce(lambda a, b: a + b,
                                   [jnp.where(ev_c, wts[0][br], wts[1][br]) * acc_ref[br, rows, :] for br in range(nbr)])
            yb_ref[rows, :] = num / jnp.where(ev_c, dens[0], dens[1])
            return carry

        lax.fori_loop(0, s // rc, merge, 0)

    def zcol(j):
        return pl.BlockSpec((s, LANES), lambda p, j=j: (0, col0 + j + p))

    return _pcall(
        body, name=name, out_shape=(_sds((s, pairs * LANES), F32), _sds((heads, s, LANES), F32)), grid=(pairs,),
        in_specs=[zcol(0), zcol(pairs), zcol(2 * pairs), pl.BlockSpec((nbr, 2, BLK, 2 * BLK), lambda p: (0, p, 0, 0))],
        out_specs=(pl.BlockSpec((s, LANES), lambda p: (0, p)), pl.BlockSpec((2, s, LANES), lambda p: (p, 0, 0))),
        scratch_shapes=[pltpu.VMEM((nbr, s, LANES), F32), pltpu.VMEM((nbr, 2, s, LANES), F32), pltpu.VMEM((nbr, 2, s, LANES), F32)],
        comm=comm,
    )(z, z, z, btiles)


def _attn_bwd(z, btiles, dyb, lse, delta, dbias_in, col0, name, comm=None):
    s = z.shape[0]
    nbr, heads = btiles.shape[:2]
    pairs = heads // 2
    scale = D_HEAD ** -0.5

    def body(q_ref, k_ref, v_ref, bt_ref, dy_ref, lse_ref, dl_ref, dbi_ref,
             dq_ref, dk_ref, dv_ref, db_ref, dqa, dka, dva):
        even = lax.broadcasted_iota(I32, (BLK, LANES), 1) < D_HEAD
        even2 = lax.broadcasted_iota(I32, (2 * BLK, LANES), 1) < D_HEAD
        for ref in (dqa, dka, dva):
            ref[...] = jnp.zeros((s, LANES), F32)
        db_ref[...] = dbi_ref[...]

        def visit(br, d, start, prev):
            kw = 2 * BLK if prev else BLK
            rows_q = _rows(start, BLK, d)
            rows_k = _rows(start - BLK * d, kw, d) if prev else rows_q
            qb = q_ref[rows_q, :]
            dyv = dy_ref[rows_q, :]
            kwin = k_ref[rows_k, :]
            kb = kwin.astype(BF16)
            vb = v_ref[rows_k, :].astype(BF16)
            ev_k = even2 if prev else even
            qm = jnp.concatenate([jnp.where(even, qb, 0.0), jnp.where(even, 0.0, qb)], axis=0).astype(BF16)
            dym = jnp.concatenate([jnp.where(even, dyv, 0.0), jnp.where(even, 0.0, dyv)], axis=0).astype(BF16)
            bias = [bt_ref[br, e] if prev else bt_ref[br, e, :, BLK:] for e in range(2)]
            sc = _dot_nt(qm, kb) * scale + jnp.concatenate(bias, axis=0)
            lt = jnp.concatenate([lse_ref.at[e][rows_q, :] for e in range(2)], axis=0)
            dt = jnp.concatenate([dl_ref.at[e][rows_q, :] for e in range(2)], axis=0)
            if prev:
                lt = jnp.concatenate([lt, lt], axis=1)
                dt = jnp.concatenate([dt, dt], axis=1)
            p = jnp.exp(sc - lt)
            ds = p * (_dot_nt(dym, vb) - dt)
            for e in range(2):
                if prev:
                    db_ref[br, e] += ds[e * BLK:(e + 1) * BLK]
                else:
                    db_ref[br, e, :, BLK:] += ds[e * BLK:(e + 1) * BLK]
            dsb = ds.astype(BF16)
            km = jnp.concatenate([jnp.where(ev_k, kwin, 0.0), jnp.where(ev_k, 0.0, kwin)], axis=0).astype(BF16)
            dqa[rows_q, :] += _dot(jnp.concatenate([dsb[:BLK], dsb[BLK:]], axis=1), km) * scale
            dka[rows_k, :] += _dot_tn(dsb, qm) * scale
            dva[rows_k, :] += _dot_tn(p.astype(BF16), dym)

        _attn_blocks(s, visit, ATTN_GROUP_BWD)
        dq_ref[...] = dqa[...].astype(BF16)
        dk_ref[...] = dka[...].astype(BF16)
        dv_ref[...] = dva[...].astype(BF16)

    def zcol(j):
        return pl.BlockSpec((s, LANES), lambda p, j=j: (0, col0 + j + p))

    col = pl.BlockSpec((s, LANES), lambda p: (0, p))
    stat = pl.BlockSpec((2, s, LANES), lambda p: (p, 0, 0))
    tile = pl.BlockSpec((nbr, 2, BLK, 2 * BLK), lambda p: (0, p, 0, 0))
    wide = _sds((s, pairs * LANES), BF16)
    return _pcall(
        body, name=name, out_shape=(wide, wide, wide, _sds(btiles.shape, F32)), grid=(pairs,),
        in_specs=[zcol(0), zcol(pairs), zcol(2 * pairs), tile, col, stat, stat, tile],
        out_specs=(col, col, col, tile),
        scratch_shapes=[pltpu.VMEM((s, LANES), F32) for _ in range(3)], comm=comm,
    )(z, z, z, btiles, dyb, lse, delta, dbias_in)


def _row(v):
    return v.reshape(1, -1)


class _LocalSchedule:
    def __init__(self):
        self.big = {}

    def fwd_comms(self, l):
        return {}

    def bwd_comms(self, l):
        return {}

    def after_bwd(self, l, grads):
        self.big[l] = grads


def _layer_fwd(l, x, h, wts, prm, btiles, comms):
    d = x.shape[1]
    wq = d // 4
    depth = prm["norm_mix_g"].shape[0]
    gout = prm["out_norm_g"][l]
    z = _mm_n(h, wts["in_t"], l, nt=True, tn=256, out_dtype=F32, name="in_proj", comm=comms.get("in_proj"))
    yan, ycn = _mix_fwd(z, prm["conv_a_w"][l], prm["conv_c_w"][l], _row(prm["conv_c_b"][l]), _row(prm["ln_c_g"][l]),
                        _row(prm["ln_c_b"][l]), _row(gout[:wq]), _row(gout[3 * wq:]), "mix_fwd")
    yb, lse = _attn_fwd(z, btiles, 3 * wq // LANES, "attn_fwd", comm=comms.get("attn_fwd"))
    y, x_mid, h2 = _out_proj(yan, yb, ycn, _row(gout[wq:3 * wq]), x, wts["out"], l, _row(prm["norm_ffn_g"][l]),
                             "out_proj", comm=comms.get("out_proj"))
    up = _mm_n(h2, wts["up"], l, nt=False, tn=512, out_dtype=BF16, name="up_proj", comm=comms.get("up_proj"))
    act = _ffn_act_fwd(up, prm["conv_f_w"][l], "ffn_act_fwd", comm=comms.get("ffn_act_fwd"))
    g_next = _row(prm["norm_mix_g"][l + 1]) if l + 1 < depth else None
    x_out, h_next = _down_proj(act, wts["down"], l, x_mid, g_next, "down_proj", comm=comms.get("down_proj"))
    return x_out, h_next, (x, h, z, yb, lse, y, x_mid, h2, up)


def _layer_bwd(l, dxo, dxo_b, saved, wts, prm, btiles, dbias, comms):
    x, h, z, yb, lse, y, x_mid, h2, up = saved
    d = x.shape[1]
    wq = d // 4
    gout = prm["out_norm_g"][l]
    dact = _mm_n(dxo_b, wts["down"], l, nt=True, tn=256, out_dtype=BF16, name="down_proj_dx", comm=comms.get(SWAP_RIDE))
    act, dgate, dval, dwf = _ffn_act_bwd(up, dact, prm["conv_f_w"][l], "ffn_act_bwd", comm=comms.get("ffn_act_bwd"))
    g_down = _mm_tn(act, dxo_b, t=256, name="down_proj_dw")
    dxm, dxm_b, dg_ffn = _proj_dx([dgate, dval], wts["up"], l, True, x_mid, _row(prm["norm_ffn_g"][l]), dxo, "up_proj_dx",
                                  comm=comms.get(SHARE_RIDE))
    g_up = _mm_tn2(h2, dgate, dval, t=256, name="up_proj_dw")
    dy = _mm_n(dxm_b, wts["out"], l, nt=True, tn=256, out_dtype=F32, name="out_proj_dx")
    g_out = _mm_tn(y, dxm_b, t=256, name="out_proj_dw")
    dza, dzc, dwa, dwc, dcb, dlg, dlb, dga, dgc = _mix_bwd(
        z, dy, prm["conv_a_w"][l], prm["conv_c_w"][l], _row(prm["conv_c_b"][l]), _row(prm["ln_c_g"][l]),
        _row(prm["ln_c_b"][l]), _row(gout[:wq]), _row(gout[3 * wq:]), "mix_bwd")
    dyb, delta, dgb = _yb_norm_bwd(yb, dy, _row(gout[wq:3 * wq]), "yb_norm_bwd")
    dq, dk, dv, dbias = _attn_bwd(z, btiles, dyb, lse, delta, dbias, 3 * wq // LANES, "attn_bwd",
                                  comm=comms.get("attn_bwd"))
    dz = [dza, dq, dk, dv, dzc]
    dx, dx_b, dg_mix = _proj_dx(dz, wts["in_t"], l, False, x, _row(prm["norm_mix_g"][l]), dxm, "in_proj_dx")
    g_in_t = _mm_tn_pieces(dz, h, t=256, name="in_proj_dw")
    big = {"in_t": g_in_t, "out": g_out, "up": g_up, "down": g_down}
    small = {"norm_mix_g": dg_mix[0], "conv_a_w": dwa, "conv_c_w": dwc, "conv_c_b": dcb[0], "ln_c_g": dlg[0],
             "ln_c_b": dlb[0], "out_norm_g": jnp.concatenate([dga[0], dgb[0], dgc[0]]), "norm_ffn_g": dg_ffn[0],
             "conv_f_w": dwf}
    return dx, dx_b, big, small, dbias


def _local_step(x, tgt, wts, prm, sched):
    depth = prm["norm_mix_g"].shape[0]
    buckets = jnp.asarray(_t5_bucket_table())
    btiles = _bias_tiles(prm["rel_bias"], buckets, "bias_tiles")
    saved = []
    h = _rms_fwd(x, _row(prm["norm_mix_g"][0]), "rms_mix_fwd")
    for l in range(depth):
        x, h, sv = _layer_fwd(l, x, h, wts, prm, btiles, sched.fwd_comms(l))
        saved.append(sv)
    loss, dx, dx_b, dg_final = _final_loss(x, _row(prm["final_g"]), tgt, "final_loss")
    dbias = jnp.zeros(btiles.shape, F32)
    small = [None] * depth
    for l in reversed(range(depth)):
        dx, dx_b, grads, small[l], dbias = _layer_bwd(l, dx, dx_b, saved[l], wts, prm, btiles, dbias, sched.bwd_comms(l))
        sched.after_bwd(l, grads)
    nbk, heads = prm["rel_bias"].shape
    d_rel = _bias_grad(dbias, buckets, nbk, "bias_grad")[:, :heads]
    return loss, dx, small, d_rel, dg_final[0]


BIG = ("in_t", "out", "up", "down")
COL_SHARDED = ("up",)
N_CHIPS = 4
N_DEV = 8
BF16_ROWS = 16


def _me():
    return lax.axis_index("x"), lax.axis_index("y"), lax.axis_index("c")


def _chip_of(x, y):
    return 2 * x + y


def _other_chips(x, y):
    return ((1 - x, y), (x, 1 - y), (1 - x, 1 - y))


def _remote(src, dst, send_sem, recv_sem, device):
    return pltpu.make_async_remote_copy(src_ref=src, dst_ref=dst, send_sem=send_sem, recv_sem=recv_sem,
                                        device_id=device, device_id_type=MESH)


def _ag_comm(wts, layer, ici_keys, fwd_keys):
    keys = tuple(k for k in BIG if k in ici_keys or k in fwd_keys)

    def geo(k):
        _, rows, cols = wts[k].shape
        return (rows, cols // N_CHIPS) if k in COL_SHARDED else (rows // N_CHIPS, cols)

    def copies(refs, sems):
        g = dict(zip(keys, refs))
        isend, irecv, dsend, drecv = sems
        x, y, c = _me()
        mine = _chip_of(x, y)

        def region(k, chip, half):
            r, cc = geo(k)
            h = r // 2
            if k in COL_SHARDED:
                return g[k].at[layer, pl.ds(pl.multiple_of(half * h, BF16_ROWS), h), pl.ds(pl.multiple_of(chip * cc, LANES), cc)]
            return g[k].at[layer, pl.ds(pl.multiple_of(chip * r + half * h, BF16_ROWS), h), :]

        def ici(k, f, landing):
            chip = _other_chips(x, y)[f]
            where = region(k, _chip_of(*chip) if landing else mine, c)
            i = keys.index(k)
            return _remote(where, where, isend.at[i, f], irecv.at[i, f], (*chip, c))

        def fwd(k, f, landing):
            chip = _other_chips(x, y)[f]
            where = region(k, _chip_of(*chip), 1 - c if landing else c)
            i = keys.index(k)
            return _remote(where, where, dsend.at[i, f], drecv.at[i, f], (x, y, 1 - c))

        return ici, fwd

    def start(ins, outs, sems):
        ici, fwd = copies(outs, sems)
        for k in keys:
            for f in range(3):
                if k in ici_keys:
                    ici(k, f, False).start()
                else:
                    fwd(k, f, False).start()

    def finish(ins, outs, sems):
        ici, fwd = copies(outs, sems)
        for k in keys:
            for f in range(3):
                if k in ici_keys:
                    ici(k, f, True).wait_recv()
                    if k in fwd_keys:
                        fwd(k, f, False).start()
        for k in keys:
            for f in range(3):
                if k in fwd_keys:
                    fwd(k, f, True).wait_recv()
                    fwd(k, f, False).wait_send()
                if k in ici_keys:
                    ici(k, f, False).wait_send()

    def done(res):
        wts.update(zip(keys, res))

    n = len(keys)
    return _Comm([wts[k] for k in keys], [_sds(wts[k].shape, BF16) for k in keys], {i: i for i in range(n)},
                 [pltpu.SemaphoreType.DMA((n, 3)) for _ in range(4)], start, finish, done)


def _small_gather_comm(slab, store):
    def copies(ins, outs, sems):
        send, recv, lsem = sems
        x, y, c = _me()
        mine = _chip_of(x, y)
        own = pltpu.make_async_copy(ins[0], outs[0].at[mine], lsem)
        pairs = []
        for f, chip in enumerate(_other_chips(x, y)):
            out = _remote(ins[0], outs[0].at[mine], send.at[f], recv.at[f], (*chip, c))
            land = _remote(ins[0], outs[0].at[_chip_of(*chip)], send.at[f], recv.at[f], (*chip, c))
            pairs.append((out, land))
        return own, pairs

    def start(ins, outs, sems):
        own, pairs = copies(ins, outs, sems)
        own.start()
        for out, _ in pairs:
            out.start()

    def finish(ins, outs, sems):
        own, pairs = copies(ins, outs, sems)
        for out, land in pairs:
            land.wait_recv()
            out.wait_send()
        own.wait()

    def done(res):
        store["small"] = res[0]

    return _Comm([slab], [_sds((N_CHIPS,) + slab.shape, F32)], {},
                 [pltpu.SemaphoreType.DMA((3,)), pltpu.SemaphoreType.DMA((3,)), pltpu.SemaphoreType.DMA], start, finish, done)


def _piece_geo(g):
    geo = {}
    for k in BIG:
        rows, cols = g[k].shape
        geo[k] = (rows // 2, cols // N_CHIPS) if k in COL_SHARDED else (rows // (2 * N_CHIPS), cols)
    return geo


def _swap_comm(g, done):
    geo = _piece_geo(g)
    n_copies = sum(N_CHIPS if k in COL_SHARDED else 1 for k in BIG)

    def copies(ins, outs, sems):
        g_refs, t_refs = dict(zip(BIG, ins)), dict(zip(BIG, outs))
        send, recv = sems
        x, y, c = _me()
        pairs = []
        for k in BIG:
            h, cc = geo[k]
            if k in COL_SHARDED:
                rows = pl.ds(pl.multiple_of((1 - c) * h, BF16_ROWS), h)
                pairs += [(g_refs[k].at[rows, pl.ds(j * cc, cc)], t_refs[k].at[j]) for j in range(N_CHIPS)]
            else:
                pairs.append((g_refs[k].at[:, 1 - c], t_refs[k]))
        return [_remote(src, dst, send.at[i], recv.at[i], (x, y, 1 - c)) for i, (src, dst) in enumerate(pairs)]

    def start(ins, outs, sems):
        for cp in copies(ins, outs, sems):
            cp.start()

    def finish(ins, outs, sems):
        for cp in copies(ins, outs, sems):
            cp.wait()

    ins = [g[k] if k in COL_SHARDED else g[k].reshape(N_CHIPS, 2, geo[k][0], geo[k][1]) for k in BIG]
    return _Comm(ins, [_sds((N_CHIPS,) + geo[k], BF16) for k in BIG], {},
                 [pltpu.SemaphoreType.DMA((n_copies,)) for _ in range(2)], start, finish,
                 lambda res: done(dict(zip(BIG, res))))


def _pair_sum(g, theirs, c_arr):
    geo = _piece_geo(g)

    def body(c_ref, *refs):
        nk = len(BIG)
        for i in range(nk):
            refs[2 * nk + i][...] = (refs[i][...].astype(F32) + refs[nk + i][...].astype(F32)).astype(BF16)

    in_specs, ins = [], []
    for k in BIG:
        h, cc = geo[k]
        if k in COL_SHARDED:
            in_specs.append(pl.BlockSpec((h, cc), lambda j, c_ref: (c_ref[0], j)))
            ins.append(g[k])
        else:
            in_specs.append(pl.BlockSpec((None, h, cc), lambda j, c_ref: (2 * j + c_ref[0], 0, 0)))
            ins.append(g[k].reshape(2 * N_CHIPS, h, cc))
    slab = [pl.BlockSpec((None,) + geo[k], lambda j, c_ref: (j, 0, 0)) for k in BIG]
    res = _pcall(body, name="rs_pair_sum", out_shape=tuple(_sds((N_CHIPS,) + geo[k], BF16) for k in BIG), grid=(N_CHIPS,),
                 in_specs=in_specs + slab, out_specs=tuple(slab), prefetch=1)(c_arr, *ins, *[theirs[k] for k in BIG])
    return dict(zip(BIG, res))


def _rs_comm(p, keys, store):
    def copies(ins, outs, sems):
        send, recv = sems
        x, y, c = _me()
        return [_remote(ins[i].at[_chip_of(*chip)], outs[i].at[f], send.at[i, f], recv.at[i, f], (*chip, c))
                for i in range(len(keys)) for f, chip in enumerate(_other_chips(x, y))]

    def start(ins, outs, sems):
        for cp in copies(ins, outs, sems):
            cp.start()

    def finish(ins, outs, sems):
        for cp in copies(ins, outs, sems):
            cp.wait()

    def done(res):
        store.update(zip(keys, res))

    return _Comm([p[k] for k in keys], [_sds((3,) + p[k].shape[1:], BF16) for k in keys], {},
                 [pltpu.SemaphoreType.DMA((len(keys), 3)) for _ in range(2)], start, finish, done)


def _quad_sum(p, b, where, l, full):
    parts = 2
    nk = len(BIG)

    def body(where_ref, *refs):
        for i in range(nk):
            acc = refs[i][...].astype(F32)
            for f in range(3):
                acc = acc + refs[nk + 3 * i + f][...].astype(F32)
            refs[5 * nk + i][...] = acc

    own, recv, outs = [], [], []
    for k in BIG:
        h, cc = p[k].shape[1:]
        th = h // parts
        own.append(pl.BlockSpec((None, th, cc), lambda i, w_ref: (w_ref[0], i, 0)))
        recv += [pl.BlockSpec((None, th, cc), lambda i, w_ref, f=f: (f, i, 0)) for f in range(3)]
        outs.append(pl.BlockSpec((None, None, th, cc), lambda i, w_ref: (l, w_ref[1], i, 0)))
    args = [p[k] for k in BIG] + [b[k] for k in BIG for _ in range(3)] + [full[k] for k in BIG]
    res = _pcall(body, name="rs_quad_sum", out_shape=tuple(_sds(full[k].shape, F32) for k in BIG), grid=(parts,),
                 in_specs=own + recv + [ANY] * nk, out_specs=tuple(outs), prefetch=1,
                 aliases={1 + 4 * nk + i: i for i in range(nk)})(where, *args)
    return dict(zip(BIG, res))


def _share_comm(layers, full, done):
    nk = len(BIG)

    def copies(outs, sems, landing):
        send, recv = sems
        x, y, c = _me()
        half = 1 - c if landing else c
        return [_remote(outs[i].at[l, half], outs[i].at[l, half], send.at[i, j], recv.at[i, j], (x, y, 1 - c))
                for i in range(nk) for j, l in enumerate(layers)]

    def start(ins, outs, sems):
        for cp in copies(outs, sems, False):
            cp.start()

    def finish(ins, outs, sems):
        for cp in copies(outs, sems, True):
            cp.wait_recv()
        for cp in copies(outs, sems, False):
            cp.wait_send()

    return _Comm([full[k] for k in BIG], [_sds(full[k].shape, F32) for k in BIG], {i: i for i in range(nk)},
                 [pltpu.SemaphoreType.DMA((nk, len(layers))) for _ in range(2)], start, finish,
                 lambda res: done(dict(zip(BIG, res))))


def _gather_comm(slab, done):
    def copies(ins, outs, sems, landing):
        send, recv = sems
        x, y, c = _me()
        me = 4 * x + 2 * y + c
        out = []
        for mask in range(1, N_DEV):
            peer = (x ^ (mask >> 2), y ^ ((mask >> 1) & 1), c ^ (mask & 1))
            slot = 4 * peer[0] + 2 * peer[1] + peer[2] if landing else me
            out.append(_remote(ins[0], outs[0].at[slot], send.at[mask - 1], recv.at[mask - 1], peer))
        return out

    def start(ins, outs, sems):
        for cp in copies(ins, outs, sems, False):
            cp.start()

    def finish(ins, outs, sems):
        for cp in copies(ins, outs, sems, True):
            cp.wait_recv()
        for cp in copies(ins, outs, sems, False):
            cp.wait_send()

    return _Comm([slab], [_sds((N_DEV,) + slab.shape, F32)], {},
                 [pltpu.SemaphoreType.DMA((N_DEV - 1,)), pltpu.SemaphoreType.DMA((N_DEV - 1,))], start, finish,
                 lambda res: done(res[0]))


def _sum_slabs(slabs, own, me):
    n, r, lanes = slabs.shape
    tr = r // 2

    def body(me_ref, s_ref, own_ref, o_ref):
        o_ref[...] = jnp.zeros((tr, lanes), F32)
        for i in range(n):
            @pl.when(me_ref[0] == i)
            def _():
                o_ref[...] += own_ref[...]

            @pl.when(me_ref[0] != i)
            def _():
                o_ref[...] += s_ref[i]

    return _pcall(body, name="sum_partials", out_shape=_sds((r, lanes), F32), grid=(2,),
                  in_specs=[pl.BlockSpec((n, tr, lanes), lambda i, me_ref: (0, i, 0)),
                            pl.BlockSpec((tr, lanes), lambda i, me_ref: (i, 0))],
                  out_specs=pl.BlockSpec((tr, lanes), lambda i, me_ref: (i, 0)), prefetch=1)(me, slabs, own)


def _cast_into_gathered(w, chip, by_cols, name):
    l, r, c = w.shape

    def body(chip_ref, w_ref, o_ref):
        o_ref[...] = w_ref[...].astype(BF16)

    if by_cols:
        shape, out = (l, r, N_CHIPS * c), pl.BlockSpec((None, r, c), lambda i, chip_ref: (i, 0, chip_ref[0]))
    else:
        shape, out = (l, N_CHIPS * r, c), pl.BlockSpec((None, r, c), lambda i, chip_ref: (i, chip_ref[0], 0))
    return _pcall(body, name=name, out_shape=_sds(shape, BF16), grid=(l,),
                  in_specs=[pl.BlockSpec((None, r, c), lambda i, chip_ref: (i, 0, 0))], out_specs=out, prefetch=1)(chip, w)


def _adamw_math(w, g, m, v):
    mn = ADAM_B1 * m + (1.0 - ADAM_B1) * g
    vn = ADAM_B2 * v + (1.0 - ADAM_B2) * (g * g)
    m_hat = mn / (1.0 - ADAM_B1 ** ADAM_STEP)
    v_hat = vn / (1.0 - ADAM_B2 ** ADAM_STEP)
    return -ADAM_LR * (m_hat / (jnp.sqrt(v_hat) + ADAM_EPS) + ADAM_WD * w), mn, vn


def _adamw(w, g, m, v, name, tr):
    r, c = w.shape

    def body(w_ref, g_ref, m_ref, v_ref, d_ref, mo_ref, vo_ref):
        d_ref[...], mo_ref[...], vo_ref[...] = _adamw_math(w_ref[...], g_ref[...], m_ref[...], v_ref[...])

    blk = pl.BlockSpec((tr, c), lambda i: (i, 0))
    return _pcall(body, name=name, out_shape=tuple(_sds((r, c), F32) for _ in range(3)), grid=(r // tr,),
                  in_specs=[blk] * 4, out_specs=(blk, blk, blk))(w, g, m, v)


def _adamw_small(groups, name):
    count = len(groups[0])
    shapes = [a.shape for a in groups[0]]
    as2d = [(math.prod(s[:-1]), s[-1]) for s in shapes]

    def body(*refs):
        for i in range(count):
            out = _adamw_math(*[refs[j * count + i][...] for j in range(4)])
            for j in range(3):
                refs[(4 + j) * count + i][...] = out[j]

    specs = [pl.BlockSpec(s, lambda i: (0, 0)) for s in as2d]
    res = _pcall(body, name=name, out_shape=tuple(_sds(s, F32) for _ in range(3) for s in as2d), grid=(1,),
                 in_specs=specs * 4, out_specs=tuple(specs * 3))(*[a.reshape(s) for grp in groups for a, s in zip(grp, as2d)])
    return [[res[j * count + i].reshape(shapes[i]) for i in range(count)] for j in range(3)]


AG_RIDES = {"in_proj": (0, ("out",), ("down",)), "attn_fwd": (0, ("up",), ("out",)), "out_proj": (0, (), ("up",)),
            "up_proj": (1, ("in_t",), ()), "ffn_act_fwd": (1, ("down",), ()), "down_proj": (1, (), ("in_t",))}
AG_FIRST = ("in_t", "down")
RS_RIDES = {"ffn_act_bwd": ("up",), "attn_bwd": ("in_t", "out", "down")}
SWAP_RIDE = "down_proj_dx"
SHARE_RIDE = "up_proj_dx"


class _Rides:
    def __init__(self, table, build):
        self.table, self.build = table, build

    def get(self, name):
        return self.build(self.table[name]) if name in self.table else None


class _MeshSchedule:
    def __init__(self, wts, depth, c_arr, where):
        self.wts, self.depth, self.c_arr, self.where = wts, depth, c_arr, where
        self.grads, self.pairs, self.recv, self.full, self.unshared = None, None, {}, None, []

    def fwd_comms(self, l):
        table = {}
        for name, (off, ici, fwd) in AG_RIDES.items():
            if l + off == 0:
                ici, fwd = (tuple(k for k in keys if k not in AG_FIRST) for keys in (ici, fwd))
            if l + off < self.depth and (ici or fwd):
                table[name] = (l + off, ici, fwd)
        return _Rides(table, lambda ride: _ag_comm(self.wts, *ride))

    def _swapped(self, theirs):
        self.pairs = _pair_sum(self.grads, theirs, self.c_arr)

    def _shared(self, full):
        self.full, self.unshared = full, []

    def bwd_comms(self, l):
        if self.grads is None:
            return {}
        table = dict(RS_RIDES)
        table[SWAP_RIDE] = "swap"
        if self.unshared:
            table[SHARE_RIDE] = "share"
        return _Rides(table, lambda what: _swap_comm(self.grads, self._swapped) if what == "swap"
                      else _share_comm(self.unshared, self.full, self._shared) if what == "share"
                      else _rs_comm(self.pairs, what, self.recv))

    def _reduce(self, l):
        self.full = _quad_sum(self.pairs, self.recv, self.where, l, self.full)
        self.unshared = self.unshared + [l]
        self.grads, self.pairs, self.recv = None, None, {}

    def after_bwd(self, l, grads):
        if self.grads is not None:
            self._reduce(l + 1)
        if self.full is None:
            geo = _piece_geo(grads)
            self.full = {k: jnp.zeros((self.depth, 2) + geo[k], F32) for k in BIG}
        self.grads = grads

    def finish(self, extra):
        _run_comm(_swap_comm(self.grads, self._swapped), "rs_swap_halves")
        _run_comm(_both(_rs_comm(self.pairs, BIG, self.recv), extra), "rs_to_owners")
        self._reduce(0)
        _run_comm(_share_comm(self.unshared, self.full, self._shared), "rs_share")
        return self.full


SHARDED_SMALL = ("conv_a_w", "conv_c_w", "conv_f_w")
SMALL = ("norm_mix_g", "conv_a_w", "conv_c_w", "conv_c_b", "ln_c_g", "ln_c_b", "out_norm_g", "norm_ffn_g",
         "conv_f_w", "rel_bias", "final_g")
SLAB_ROWS = 16


def _pack(arrays):
    flat = jnp.concatenate([a.reshape(-1) for a in arrays])
    unit = SLAB_ROWS * LANES
    total = -(-flat.shape[0] // unit) * unit
    return jnp.pad(flat, (0, total - flat.shape[0])).reshape(-1, LANES)


def _unpack(slab, shapes):
    flat = slab.reshape(-1)
    out, off = [], 0
    for shp in shapes:
        size = math.prod(shp)
        out.append(flat[off:off + size].reshape(shp))
        off += size
    return out


def kernel(x, norm_mix_g, w_in, conv_a_w, conv_c_w, conv_c_b, ln_c_g, ln_c_b, out_norm_g, w_out, norm_ffn_g, w_up, conv_f_w, w_down, rel_bias, final_g, loss_target, m_norm_mix_g, m_w_in, m_conv_a_w, m_conv_c_w, m_conv_c_b, m_ln_c_g, m_ln_c_b, m_out_norm_g, m_w_out, m_norm_ffn_g, m_w_up, m_conv_f_w, m_w_down, m_rel_bias, m_final_g, v_norm_mix_g, v_w_in, v_conv_a_w, v_conv_c_w, v_conv_c_b, v_ln_c_g, v_ln_c_b, v_out_norm_g, v_w_out, v_norm_ffn_g, v_w_up, v_conv_f_w, v_w_down, v_rel_bias, v_final_g):
    weights = dict(norm_mix_g=norm_mix_g, w_in=w_in, conv_a_w=conv_a_w, conv_c_w=conv_c_w, conv_c_b=conv_c_b,
                   ln_c_g=ln_c_g, ln_c_b=ln_c_b, out_norm_g=out_norm_g, w_out=w_out, norm_ffn_g=norm_ffn_g, w_up=w_up,
                   conv_f_w=conv_f_w, w_down=w_down, rel_bias=rel_bias, final_g=final_g)
    mom_m = dict(norm_mix_g=m_norm_mix_g, w_in=m_w_in, conv_a_w=m_conv_a_w, conv_c_w=m_conv_c_w, conv_c_b=m_conv_c_b,
                 ln_c_g=m_ln_c_g, ln_c_b=m_ln_c_b, out_norm_g=m_out_norm_g, w_out=m_w_out, norm_ffn_g=m_norm_ffn_g,
                 w_up=m_w_up, conv_f_w=m_conv_f_w, w_down=m_w_down, rel_bias=m_rel_bias, final_g=m_final_g)
    mom_v = dict(norm_mix_g=v_norm_mix_g, w_in=v_w_in, conv_a_w=v_conv_a_w, conv_c_w=v_conv_c_w, conv_c_b=v_conv_c_b,
                 ln_c_g=v_ln_c_g, ln_c_b=v_ln_c_b, out_norm_g=v_out_norm_g, w_out=v_w_out, norm_ffn_g=v_norm_ffn_g,
                 w_up=v_w_up, conv_f_w=v_conv_f_w, w_down=v_w_down, rel_bias=v_rel_bias, final_g=v_final_g)
    xi, yi, ci = _me()
    chip = _chip_of(xi, yi)
    c_arr = jnp.reshape(ci, (1,)).astype(I32)
    chip_arr = jnp.reshape(chip, (1,)).astype(I32)
    me_arr = jnp.reshape(4 * xi + 2 * yi + ci, (1,)).astype(I32)
    where = jnp.stack([chip, ci]).astype(I32)
    depth = w_out.shape[0]

    wts = {"in_t": _cast_into_gathered(jnp.swapaxes(w_in, 1, 2), chip_arr, False, "cast_in"),
           "out": _cast_into_gathered(w_out, chip_arr, False, "cast_out"),
           "up": _cast_into_gathered(w_up, chip_arr, True, "cast_up"),
           "down": _cast_into_gathered(w_down, chip_arr, False, "cast_down")}
    store = {}
    _run_comm(_small_gather_comm(_pack([weights[n] for n in SHARDED_SMALL]), store), "ag_small")
    _run_comm(_ag_comm(wts, 0, AG_FIRST, AG_FIRST), "ag_weights")
    prm = {n: weights[n] for n in SMALL if n not in SHARDED_SMALL}
    per_chip = [_unpack(store["small"][j], [weights[n].shape for n in SHARDED_SMALL]) for j in range(N_CHIPS)]
    for i, n in enumerate(SHARDED_SMALL):
        prm[n] = jnp.concatenate([per_chip[j][i] for j in range(N_CHIPS)], axis=-1)

    sched = _MeshSchedule(wts, depth, c_arr, where)
    loss_row, dx, small, d_rel, d_final = _local_step(x[0], loss_target[0], wts, prm, sched)
    loss = lax.psum(loss_row[0, 0], ("x", "y", "c"))

    stacked = {n: jnp.stack([small[l][n] for l in range(depth)]) for n in small[0]}
    stacked["rel_bias"] = d_rel
    stacked["final_g"] = d_final
    full_shapes = [stacked[n].shape for n in SMALL]
    partial = _pack([stacked[n] for n in SMALL])
    reduced = sched.finish(_gather_comm(partial, lambda res: store.update(partials=res)))

    grads = {}
    shard_shapes = {"in_t": jnp.swapaxes(w_in, 1, 2).shape, "out": w_out.shape, "up": w_up.shape, "down": w_down.shape}
    red = {k: reduced[k].reshape(shard_shapes[k]) for k in BIG}
    grads["w_in"] = jnp.swapaxes(red["in_t"], 1, 2)
    grads["w_out"], grads["w_up"], grads["w_down"] = red["out"], red["up"], red["down"]
    delta, new_m, new_v = {}, {}, {}
    for n in ("w_in", "w_out", "w_up", "w_down"):
        shp = weights[n].shape
        flat = lambda a, shp=shp: a.reshape(shp[0] * shp[1], shp[2])
        tile = max(t for t in range(8, 257, 8) if shp[1] % t == 0)
        d, mn, vn = _adamw(flat(weights[n]), flat(grads[n]), flat(mom_m[n]), flat(mom_v[n]), "adamw_" + n, tile)
        delta[n], new_m[n], new_v[n] = d.reshape(shp), mn.reshape(shp), vn.reshape(shp)

    summed = _unpack(_sum_slabs(store["partials"], partial, me_arr), full_shapes)
    for n, g in zip(SMALL, summed):
        if n in SHARDED_SMALL:
            width = weights[n].shape[-1]
            g = lax.dynamic_slice_in_dim(g, chip * width, width, axis=g.ndim - 1)
        grads[n] = g
    res = _adamw_small([[src[n] for n in SMALL] for src in (weights, grads, mom_m, mom_v)], "adamw_small")
    for i, n in enumerate(SMALL):
        delta[n], new_m[n], new_v[n] = res[0][i], res[1][i], res[2][i]

    order = ("norm_mix_g", "w_in", "conv_a_w", "conv_c_w", "conv_c_b", "ln_c_g", "ln_c_b", "out_norm_g", "w_out",
             "norm_ffn_g", "w_up", "conv_f_w", "w_down", "rel_bias", "final_g")
    return (loss, dx[None], *[grads[n] for n in order], *[delta[n] for n in order], *[new_m[n] for n in order],
            *[new_v[n] for n in order])
```

```python
import functools
import math

import numpy as np
import jax
import jax.numpy as jnp
from jax import lax
from jax.experimental import pallas as pl
from jax.experimental.pallas import tpu as pltpu

F32 = jnp.float32
BF16 = jnp.bfloat16
I32 = jnp.int32

EPS = 1e-6
NEG = -1e30
D_HEAD = 64
LANES = 128
BLK = 128
ATTN_GROUP_FWD = 4
ATTN_GROUP_BWD = 4
DILATED_BRANCHES = ((128, 1), (512, 4), (2048, 16))
NUM_BUCKETS = 32
MAX_DISTANCE = 2048
SHORT_CONV = 3
CONFORMER_CONV = 31
FFN_CONV = 3
PAD_SHORT = 8
PAD_LONG = 32
ROW_CHUNK = 256
V7X_VMEM_BYTES = 64 * 1024 * 1024
VMEM_REQUEST = V7X_VMEM_BYTES * 7 // 8

ADAM_LR = 0.001
ADAM_B1 = 0.9
ADAM_B2 = 0.999
ADAM_EPS = 1e-08
ADAM_WD = 0.01
ADAM_STEP = 10

MESH = pl.DeviceIdType.MESH
ANY = pl.BlockSpec(memory_space=pl.ANY)


def _sds(shape, dtype):
    return jax.ShapeDtypeStruct(tuple(shape), dtype)


class _Comm:
    def __init__(self, ins, out_shapes, aliases, sems, start, finish, done):
        self.ins, self.out_shapes, self.aliases, self.sems = list(ins), list(out_shapes), dict(aliases), list(sems)
        self.start, self.finish, self.done = start, finish, done


def _pcall(body, *, name, out_shape, grid=(), in_specs=None, out_specs=None, scratch_shapes=(), vmem=VMEM_REQUEST,
           aliases=None, prefetch=0, comm=None):
    params = pltpu.CompilerParams(dimension_semantics=("arbitrary",) * len(grid), vmem_limit_bytes=vmem)
    single = not isinstance(out_shape, (tuple, list))
    outs = [out_shape] if single else list(out_shape)
    ospecs = [out_specs] if single else list(out_specs)
    ispecs, scratch, aliases = list(in_specs), list(scratch_shapes), dict(aliases or {})
    n_in, n_out, n_scr = len(ispecs), len(outs), len(scratch)
    kernel_body = body
    if comm is not None:
        n_ci, n_co = len(comm.ins), len(comm.out_shapes)

        def kernel_body(*refs):
            pre, rest = refs[:prefetch], refs[prefetch:]
            core_in, c_in = rest[:n_in], rest[n_in:n_in + n_ci]
            o0 = n_in + n_ci
            core_out, c_out = rest[o0:o0 + n_out], rest[o0 + n_out:o0 + n_out + n_co]
            s0 = o0 + n_out + n_co
            core_scr, c_sem = rest[s0:s0 + n_scr], rest[s0 + n_scr:]
            first = functools.reduce(jnp.logical_and, [pl.program_id(a) == 0 for a in range(len(grid))])
            last = functools.reduce(jnp.logical_and, [pl.program_id(a) == grid[a] - 1 for a in range(len(grid))])
            pl.when(first)(lambda: comm.start(c_in, c_out, c_sem))
            body(*pre, *core_in, *core_out, *core_scr)
            pl.when(last)(lambda: comm.finish(c_in, c_out, c_sem))

        for i, o in comm.aliases.items():
            aliases[prefetch + n_in + i] = n_out + o
        ispecs += [ANY] * n_ci
        ospecs += [ANY] * n_co
        outs += comm.out_shapes
        scratch += comm.sems
    if prefetch:
        spec = pltpu.PrefetchScalarGridSpec(num_scalar_prefetch=prefetch, grid=grid, in_specs=ispecs,
                                            out_specs=tuple(ospecs), scratch_shapes=scratch)
        call = pl.pallas_call(kernel_body, name=name, out_shape=tuple(outs), grid_spec=spec,
                              input_output_aliases=aliases, compiler_params=params)
    else:
        call = pl.pallas_call(kernel_body, name=name, out_shape=tuple(outs), grid=grid, in_specs=ispecs,
                              out_specs=tuple(ospecs), scratch_shapes=scratch, input_output_aliases=aliases,
                              compiler_params=params)

    def run(*args):
        res = call(*args, *(comm.ins if comm is not None else ()))
        if comm is not None:
            comm.done(res[n_out:])
        return res[0] if single else tuple(res[:n_out])

    return run


def _both(a, b):
    def split(refs, na):
        return refs[:na], refs[na:]

    def run(which):
        def go(ins, outs, sems):
            for comm, i, o, s in zip((a, b), split(ins, len(a.ins)), split(outs, len(a.out_shapes)), split(sems, len(a.sems))):
                getattr(comm, which)(i, o, s)
        return go

    def done(res):
        a.done(res[:len(a.out_shapes)])
        b.done(res[len(a.out_shapes):])

    aliases = dict(a.aliases)
    aliases.update({len(a.ins) + i: len(a.out_shapes) + o for i, o in b.aliases.items()})
    return _Comm(a.ins + b.ins, a.out_shapes + b.out_shapes, aliases, a.sems + b.sems, run("start"), run("finish"), done)


def _run_comm(comm, name):
    def body(*refs):
        n_ci, n_co = len(comm.ins), len(comm.out_shapes)
        c_in, c_out, c_sem = refs[:n_ci], refs[n_ci:n_ci + n_co], refs[n_ci + n_co:]
        comm.start(c_in, c_out, c_sem)
        comm.finish(c_in, c_out, c_sem)

    res = pl.pallas_call(body, name=name, out_shape=tuple(comm.out_shapes), in_specs=[ANY] * len(comm.ins),
                         out_specs=tuple([ANY] * len(comm.out_shapes)), scratch_shapes=comm.sems,
                         input_output_aliases=comm.aliases)(*comm.ins)
    comm.done(res)


def _dot(a, b):
    return lax.dot_general(a, b, (((1,), (0,)), ((), ())), preferred_element_type=F32)


def _dot_nt(a, b):
    return lax.dot_general(a, b, (((1,), (1,)), ((), ())), preferred_element_type=F32)


def _dot_tn(a, b):
    return lax.dot_general(a, b, (((0,), (0,)), ((), ())), preferred_element_type=F32)


def _sigmoid(x):
    return 1.0 / (1.0 + jnp.exp(-x))


def _rstd(x):
    return lax.rsqrt(jnp.mean(x * x, axis=-1, keepdims=True) + EPS)


def _rms_fwd(x, g, name, comm=None):
    s, d = x.shape
    tm = ROW_CHUNK

    def body(x_ref, g_ref, o_ref):
        xv = x_ref[...]
        o_ref[...] = (xv * _rstd(xv) * g_ref[...]).astype(BF16)

    return _pcall(body, name=name, out_shape=_sds((s, d), BF16), grid=(s // tm,),
                  in_specs=[pl.BlockSpec((tm, d), lambda i: (i, 0)), pl.BlockSpec((1, d), lambda i: (0, 0))],
                  out_specs=pl.BlockSpec((tm, d), lambda i: (i, 0)), comm=comm)(x, g)


def _rms_bwd(x, g, dh, dres, name):
    s, d = x.shape
    tm = ROW_CHUNK

    def body(x_ref, g_ref, dh_ref, dres_ref, dx_ref, dxb_ref, dg_ref):
        i = pl.program_id(0)
        xv = x_ref[...]
        r = _rstd(xv)
        xh = xv * r
        dhv = dh_ref[...]
        gd = dhv * g_ref[...]
        dx = dres_ref[...] + r * (gd - xh * jnp.mean(gd * xh, axis=-1, keepdims=True))
        dx_ref[...] = dx
        dxb_ref[...] = dx.astype(BF16)
        part = jnp.sum(dhv * xh, axis=0, keepdims=True)

        @pl.when(i == 0)
        def _():
            dg_ref[...] = part

        @pl.when(i > 0)
        def _():
            dg_ref[...] += part

    row = pl.BlockSpec((tm, d), lambda i: (i, 0))
    vec = pl.BlockSpec((1, d), lambda i: (0, 0))
    return _pcall(body, name=name, out_shape=(_sds((s, d), F32), _sds((s, d), BF16), _sds((1, d), F32)),
                  grid=(s // tm,), in_specs=[row, vec, row, row], out_specs=(row, row, vec))(x, g, dh, dres)


def _final_loss(x, g, tgt, name):
    s, d = x.shape
    tm = ROW_CHUNK

    def body(x_ref, g_ref, t_ref, loss_ref, dx_ref, dxb_ref, dg_ref):
        i = pl.program_id(0)
        xv = x_ref[...]
        r = _rstd(xv)
        xh = xv * r
        e = xh * g_ref[...] - t_ref[...]
        lpart = 0.5 * jnp.sum(jnp.mean(e * e, axis=-1, keepdims=True), axis=0, keepdims=True)
        dy = e * (1.0 / d)
        gd = dy * g_ref[...]
        dx = r * (gd - xh * jnp.mean(gd * xh, axis=-1, keepdims=True))
        dx_ref[...] = dx
        dxb_ref[...] = dx.astype(BF16)
        part = jnp.sum(dy * xh, axis=0, keepdims=True)
        lrow = jnp.broadcast_to(lpart, (1, LANES))

        @pl.when(i == 0)
        def _():
            dg_ref[...] = part
            loss_ref[...] = lrow

        @pl.when(i > 0)
        def _():
            dg_ref[...] += part
            loss_ref[...] += lrow

    row = pl.BlockSpec((tm, d), lambda i: (i, 0))
    vec = pl.BlockSpec((1, d), lambda i: (0, 0))
    return _pcall(body, name=name,
                  out_shape=(_sds((1, LANES), F32), _sds((s, d), F32), _sds((s, d), BF16), _sds((1, d), F32)),
                  grid=(s // tm,), in_specs=[row, vec, row],
                  out_specs=(pl.BlockSpec((1, LANES), lambda i: (0, 0)), row, row, vec))(x, g, tgt)


def _mm_n(a, b, layer, *, nt, tn, out_dtype, name, resid=None, b_part=0, comm=None):
    s, k = a.shape
    n = b.shape[1] if nt else b.shape[2]
    rows = 512

    def body(a_ref, b_ref, *refs):
        o_ref = refs[-1]
        bv = b_ref[...]
        for r0 in range(0, s, rows):
            av = a_ref[r0:r0 + rows, :]
            prod = _dot_nt(av, bv) if nt else _dot(av, bv)
            if resid is not None:
                prod = refs[0][r0:r0 + rows, :] + prod
            o_ref[r0:r0 + rows, :] = prod.astype(out_dtype)

    b_spec = (pl.BlockSpec((None, tn, k), lambda j: (layer, j, b_part)) if nt
              else pl.BlockSpec((None, k, tn), lambda j: (layer, b_part, j)))
    col = pl.BlockSpec((s, tn), lambda j: (0, j))
    extra = () if resid is None else (resid,)
    return _pcall(body, name=name, out_shape=_sds((s, n), out_dtype), grid=(n // tn,),
                  in_specs=[pl.BlockSpec((s, k), lambda j: (0, 0)), b_spec] + [col] * len(extra),
                  out_specs=col, comm=comm)(a, b, *extra)


def _mm_tn(a, b, *, t, name):
    s, ka = a.shape
    n = b.shape[1]

    def body(a_ref, b_ref, o_ref):
        o_ref[...] = _dot_tn(a_ref[...], b_ref[...]).astype(BF16)

    return _pcall(body, name=name, out_shape=_sds((ka, n), BF16), grid=(ka // t,),
                  in_specs=[pl.BlockSpec((s, t), lambda i: (0, i)), pl.BlockSpec((s, n), lambda i: (0, 0))],
                  out_specs=pl.BlockSpec((t, n), lambda i: (i, 0)))(a, b)


def _mm_tn_pieces(pieces, b, *, t, name):
    s, n = b.shape
    blocks = [p.shape[1] // t for p in pieces]
    starts = [sum(blocks[:i]) for i in range(len(pieces))]

    def body(*refs):
        p_refs, b_ref, o_ref = refs[:len(pieces)], refs[len(pieces)], refs[len(pieces) + 1]
        j = pl.program_id(0)
        for p_ref, start, count in zip(p_refs, starts, blocks):
            @pl.when((j >= start) & (j < start + count))
            def _(p_ref=p_ref):
                o_ref[...] = _dot_tn(p_ref[...], b_ref[...]).astype(BF16)

    specs = [pl.BlockSpec((s, t), lambda j, start=start, count=count: (0, jnp.clip(j - start, 0, count - 1)))
             for start, count in zip(starts, blocks)]
    return _pcall(body, name=name, out_shape=_sds((sum(blocks) * t, n), BF16), grid=(sum(blocks),),
                  in_specs=specs + [pl.BlockSpec((s, n), lambda j: (0, 0))],
                  out_specs=pl.BlockSpec((t, n), lambda j: (j, 0)))(*pieces, b)


def _mm_tn2(a, b_lo, b_hi, *, t, name, comm=None):
    s, ka = a.shape
    half = b_lo.shape[1]
    nb = half // t

    def body(a_ref, lo_ref, hi_ref, o_ref):
        j = pl.program_id(0)

        @pl.when(j < nb)
        def _():
            o_ref[...] = _dot_tn(a_ref[...], lo_ref[...]).astype(BF16)

        @pl.when(j >= nb)
        def _():
            o_ref[...] = _dot_tn(a_ref[...], hi_ref[...]).astype(BF16)

    return _pcall(body, name=name, out_shape=_sds((ka, 2 * half), BF16), grid=(2 * nb,),
                  in_specs=[pl.BlockSpec((s, ka), lambda j: (0, 0)),
                            pl.BlockSpec((s, t), lambda j: (0, jnp.minimum(j, nb - 1))),
                            pl.BlockSpec((s, t), lambda j: (0, jnp.maximum(j - nb, 0)))],
                  out_specs=pl.BlockSpec((ka, t), lambda j: (0, j)), comm=comm)(a, b_lo, b_hi)


SUBLANES = 8


def _tap_windows(win, width, lead, rows):
    offs = [lead + k for k in range(width)]
    if width <= SUBLANES:
        return [win[o:o + rows, :] for o in offs]
    n = win.shape[0]
    out = {}
    for r in sorted({o % SUBLANES for o in offs}):
        base = win if r == 0 else pltpu.roll(win, n - r, axis=0)
        for o in offs:
            if o % SUBLANES == r:
                out[o - lead] = base[o - r:o - r + rows, :]
    return [out[k] for k in range(width)]


def _conv_taps(taps, w_ref):
    acc = None
    for k, tap in enumerate(taps):
        term = w_ref[pl.ds(k, 1), :] * tap
        acc = term if acc is None else acc + term
    return acc


def _causal_taps(win, width, pad, rows):
    return _tap_windows(win, width, pad - (width - 1), rows)


def _anticausal_taps(win, width, rows):
    return _tap_windows(win, width, 0, rows)[::-1]


def _conv_wgrad(dw_ref, g, taps):
    for k, tap in enumerate(taps):
        dw_ref[pl.ds(k, 1), :] += jnp.sum(g * tap, axis=0, keepdims=True)


def _mixer_a_fwd(ab, taps_t, wa_ref):
    ct = _conv_taps(taps_t, wa_ref)
    return ab * ct, ct


def _mixer_c_fwd(taps_u, wc_ref, cb_ref, lg_ref, lb_ref):
    u = _conv_taps(taps_u, wc_ref) + cb_ref[...]
    mu = jnp.mean(u, axis=-1, keepdims=True)
    uc = u - mu
    rs = lax.rsqrt(jnp.mean(uc * uc, axis=-1, keepdims=True) + EPS)
    uh = uc * rs
    ln = uh * lg_ref[...] + lb_ref[...]
    sg = _sigmoid(ln)
    return ln * sg, ln, sg, uh, rs


def _mix_fwd(z, wa, wc, cb, lg, lb, ga, gc, name):
    s = z.shape[0]
    w = wa.shape[1]
    nblk = z.shape[1] // w
    rc = ROW_CHUNK

    def body(ah_ref, ab_ref, ac_ref, cv_ref, cg_ref, wa_ref, wc_ref, cb_ref, lg_ref, lb_ref, ga_ref, gc_ref,
             ya_ref, yc_ref, tpad, upad):
        tpad[pl.ds(0, PAD_SHORT), :] = jnp.zeros((PAD_SHORT, w), F32)
        upad[pl.ds(0, PAD_LONG), :] = jnp.zeros((PAD_LONG, w), F32)

        def chunk(i, carry):
            base = pl.multiple_of(i * rc, rc)
            rows = pl.ds(base, rc)
            ah, ab, ac = ah_ref[rows, :], ab_ref[rows, :], ac_ref[rows, :]
            tpad[pl.ds(base + PAD_SHORT, rc), :] = ac * ah
            ya, _ = _mixer_a_fwd(ab, _causal_taps(tpad[pl.ds(base, rc + PAD_SHORT), :], SHORT_CONV, PAD_SHORT, rc), wa_ref)
            ya_ref[rows, :] = (ya * _rstd(ya) * ga_ref[...]).astype(BF16)
            upad[pl.ds(base + PAD_LONG, rc), :] = cv_ref[rows, :] * _sigmoid(cg_ref[rows, :])
            taps_u = _causal_taps(upad[pl.ds(base, rc + PAD_LONG), :], CONFORMER_CONV, PAD_LONG, rc)
            yc = _mixer_c_fwd(taps_u, wc_ref, cb_ref, lg_ref, lb_ref)[0]
            yc_ref[rows, :] = (yc * _rstd(yc) * gc_ref[...]).astype(BF16)
            return carry

        lax.fori_loop(0, s // rc, chunk, 0)

    def zblk(j):
        return pl.BlockSpec((s, w), lambda i: (0, j))

    def whole(a):
        return pl.BlockSpec(a.shape, lambda i: (0, 0))

    return _pcall(
        body, name=name, out_shape=(_sds((s, w), BF16), _sds((s, w), BF16)), grid=(1,),
        in_specs=[zblk(0), zblk(1), zblk(2), zblk(nblk - 2), zblk(nblk - 1)] + [whole(a) for a in (wa, wc, cb, lg, lb, ga, gc)],
        out_specs=(pl.BlockSpec((s, w), lambda i: (0, 0)), pl.BlockSpec((s, w), lambda i: (0, 0))),
        scratch_shapes=[pltpu.VMEM((s + PAD_SHORT, w), F32), pltpu.VMEM((s + PAD_LONG, w), F32)],
    )(z, z, z, z, z, wa, wc, cb, lg, lb, ga, gc)


def _mix_bwd(z, dy, wa, wc, cb, lg, lb, ga, gc, name, comm=None):
    s = z.shape[0]
    w = wa.shape[1]
    nblk = z.shape[1] // w
    nyb = dy.shape[1] // w
    rc = ROW_CHUNK

    def body(ah_ref, ab_ref, ac_ref, cv_ref, cg_ref, dya_ref, dyc_ref,
             wa_ref, wc_ref, cb_ref, lg_ref, lb_ref, ga_ref, gc_ref,
             dza_ref, dzc_ref, dwa_ref, dwc_ref, dcb_ref, dlg_ref, dlb_ref, dga_ref, dgc_ref,
             tpad, upad, dctp, dup):
        tpad[pl.ds(0, PAD_SHORT), :] = jnp.zeros((PAD_SHORT, w), F32)
        upad[pl.ds(0, PAD_LONG), :] = jnp.zeros((PAD_LONG, w), F32)
        dctp[pl.ds(s, PAD_SHORT), :] = jnp.zeros((PAD_SHORT, w), F32)
        dup[pl.ds(s, PAD_LONG), :] = jnp.zeros((PAD_LONG, w), F32)
        for ref in (dwa_ref, dwc_ref, dcb_ref, dlg_ref, dlb_ref, dga_ref, dgc_ref):
            ref[...] = jnp.zeros(ref.shape, F32)

        def rms_bwd(y, g_ref, dyn, dg_ref):
            r = _rstd(y)
            yh = y * r
            gd = dyn * g_ref[...]
            dg_ref[...] += jnp.sum(dyn * yh, axis=0, keepdims=True)
            return r * (gd - yh * jnp.mean(gd * yh, axis=-1, keepdims=True))

        def first(i, carry):
            base = pl.multiple_of(i * rc, rc)
            rows = pl.ds(base, rc)
            ah, ab, ac = ah_ref[rows, :], ab_ref[rows, :], ac_ref[rows, :]
            tpad[pl.ds(base + PAD_SHORT, rc), :] = ac * ah
            taps_t = _causal_taps(tpad[pl.ds(base, rc + PAD_SHORT), :], SHORT_CONV, PAD_SHORT, rc)
            ya, ct = _mixer_a_fwd(ab, taps_t, wa_ref)
            dya = rms_bwd(ya, ga_ref, dya_ref[rows, :], dga_ref)
            dza_ref[rows, w:2 * w] = (dya * ct).astype(BF16)
            dct = dya * ab
            dctp[rows, :] = dct
            _conv_wgrad(dwa_ref, dct, taps_t)

            upad[pl.ds(base + PAD_LONG, rc), :] = cv_ref[rows, :] * _sigmoid(cg_ref[rows, :])
            taps_u = _causal_taps(upad[pl.ds(base, rc + PAD_LONG), :], CONFORMER_CONV, PAD_LONG, rc)
            yc, ln, sg, uh, rs = _mixer_c_fwd(taps_u, wc_ref, cb_ref, lg_ref, lb_ref)
            dyc = rms_bwd(yc, gc_ref, dyc_ref[rows, :], dgc_ref)
            dln = dyc * (sg * (1.0 + ln * (1.0 - sg)))
            dlg_ref[...] += jnp.sum(dln * uh, axis=0, keepdims=True)
            dlb_ref[...] += jnp.sum(dln, axis=0, keepdims=True)
            duh = dln * lg_ref[...]
            du = rs * (duh - jnp.mean(duh, axis=-1, keepdims=True) - uh * jnp.mean(duh * uh, axis=-1, keepdims=True))
            dcb_ref[...] += jnp.sum(du, axis=0, keepdims=True)
            dup[rows, :] = du
            _conv_wgrad(dwc_ref, du, taps_u)
            return carry

        lax.fori_loop(0, s // rc, first, 0)

        def second(i, carry):
            base = pl.multiple_of(i * rc, rc)
            rows = pl.ds(base, rc)
            dt = _conv_taps(_anticausal_taps(dctp[pl.ds(base, rc + PAD_SHORT), :], SHORT_CONV, rc), wa_ref)
            dza_ref[rows, 0:w] = (dt * ac_ref[rows, :]).astype(BF16)
            dza_ref[rows, 2 * w:3 * w] = (dt * ah_ref[rows, :]).astype(BF16)
            du0 = _conv_taps(_anticausal_taps(dup[pl.ds(base, rc + PAD_LONG), :], CONFORMER_CONV, rc), wc_ref)
            sg = _sigmoid(cg_ref[rows, :])
            dzc_ref[rows, 0:w] = (du0 * sg).astype(BF16)
            dzc_ref[rows, w:2 * w] = (du0 * cv_ref[rows, :] * sg * (1.0 - sg)).astype(BF16)
            return carry

        lax.fori_loop(0, s // rc, second, 0)

    def blk(j):
        return pl.BlockSpec((s, w), lambda i: (0, j))

    def whole(a):
        return pl.BlockSpec(tuple(a.shape), lambda i: (0, 0))

    params = (wa, wc, cb, lg, lb, ga, gc)
    outs = (_sds((s, 3 * w), BF16), _sds((s, 2 * w), BF16)) + tuple(_sds(p.shape, F32) for p in params)
    return _pcall(
        body, name=name, out_shape=outs, grid=(1,),
        in_specs=[blk(0), blk(1), blk(2), blk(nblk - 2), blk(nblk - 1), blk(0), blk(nyb - 1)] + [whole(p) for p in params],
        out_specs=tuple(whole(o) for o in outs),
        scratch_shapes=[pltpu.VMEM((s + PAD_SHORT, w), F32), pltpu.VMEM((s + PAD_LONG, w), F32),
                        pltpu.VMEM((s + PAD_SHORT, w), F32), pltpu.VMEM((s + PAD_LONG, w), F32)], comm=comm,
    )(z, z, z, z, z, dy, dy, *params)


def _ffn_act_fwd(up, wf, name, comm=None):
    s, f2 = up.shape
    f = f2 // 2
    tc = 256
    nb = f // tc
    rc = ROW_CHUNK

    def body(g_ref, v_ref, wg_ref, wv_ref, o_ref, gpad, vpad):
        gpad[pl.ds(0, PAD_SHORT), :] = jnp.zeros((PAD_SHORT, tc), F32)
        vpad[pl.ds(0, PAD_SHORT), :] = jnp.zeros((PAD_SHORT, tc), F32)

        def chunk(i, carry):
            base = pl.multiple_of(i * rc, rc)
            rows = pl.ds(base, rc)
            gpad[pl.ds(base + PAD_SHORT, rc), :] = g_ref[rows, :].astype(F32)
            vpad[pl.ds(base + PAD_SHORT, rc), :] = v_ref[rows, :].astype(F32)
            gc = _conv_taps(_causal_taps(gpad[pl.ds(base, rc + PAD_SHORT), :], FFN_CONV, PAD_SHORT, rc), wg_ref)
            vc = _conv_taps(_causal_taps(vpad[pl.ds(base, rc + PAD_SHORT), :], FFN_CONV, PAD_SHORT, rc), wv_ref)
            o_ref[rows, :] = (gc * _sigmoid(gc) * vc).astype(BF16)
            return carry

        lax.fori_loop(0, s // rc, chunk, 0)

    return _pcall(
        body, name=name, out_shape=_sds((s, f), BF16), grid=(nb,),
        in_specs=[pl.BlockSpec((s, tc), lambda j: (0, j)), pl.BlockSpec((s, tc), lambda j: (0, j + nb)),
                  pl.BlockSpec((FFN_CONV, tc), lambda j: (0, j)), pl.BlockSpec((FFN_CONV, tc), lambda j: (0, j + nb))],
        out_specs=pl.BlockSpec((s, tc), lambda j: (0, j)),
        scratch_shapes=[pltpu.VMEM((s + PAD_SHORT, tc), F32), pltpu.VMEM((s + PAD_SHORT, tc), F32)], comm=comm,
    )(up, up, wf, wf)


def _ffn_act_bwd(up, dact, wf, name, comm=None):
    s, f2 = up.shape
    f = f2 // 2
    tc = 256
    nb = f // tc
    rc = ROW_CHUNK

    def body(g_ref, v_ref, da_ref, wg_ref, wv_ref, act_ref, dg_ref, dv_ref, dwg_ref, dwv_ref, gpad, vpad, dgp, dvp):
        gpad[pl.ds(0, PAD_SHORT), :] = jnp.zeros((PAD_SHORT, tc), F32)
        vpad[pl.ds(0, PAD_SHORT), :] = jnp.zeros((PAD_SHORT, tc), F32)
        dgp[pl.ds(s, PAD_SHORT), :] = jnp.zeros((PAD_SHORT, tc), F32)
        dvp[pl.ds(s, PAD_SHORT), :] = jnp.zeros((PAD_SHORT, tc), F32)
        dwg_ref[...] = jnp.zeros((FFN_CONV, tc), F32)
        dwv_ref[...] = jnp.zeros((FFN_CONV, tc), F32)

        def first(i, carry):
            base = pl.multiple_of(i * rc, rc)
            rows = pl.ds(base, rc)
            gpad[pl.ds(base + PAD_SHORT, rc), :] = g_ref[rows, :].astype(F32)
            vpad[pl.ds(base + PAD_SHORT, rc), :] = v_ref[rows, :].astype(F32)
            taps_g = _causal_taps(gpad[pl.ds(base, rc + PAD_SHORT), :], FFN_CONV, PAD_SHORT, rc)
            taps_v = _causal_taps(vpad[pl.ds(base, rc + PAD_SHORT), :], FFN_CONV, PAD_SHORT, rc)
            gc = _conv_taps(taps_g, wg_ref)
            vc = _conv_taps(taps_v, wv_ref)
            sg = _sigmoid(gc)
            silu = gc * sg
            act_ref[rows, :] = (silu * vc).astype(BF16)
            da = da_ref[rows, :].astype(F32)
            dgc = da * vc * (sg * (1.0 + gc * (1.0 - sg)))
            dvc = da * silu
            dgp[rows, :] = dgc
            dvp[rows, :] = dvc
            _conv_wgrad(dwg_ref, dgc, taps_g)
            _conv_wgrad(dwv_ref, dvc, taps_v)
            return carry

        lax.fori_loop(0, s // rc, first, 0)

        def second(i, carry):
            base = pl.multiple_of(i * rc, rc)
            rows = pl.ds(base, rc)
            dg_ref[rows, :] = _conv_taps(_anticausal_taps(dgp[pl.ds(base, rc + PAD_SHORT), :], FFN_CONV, rc), wg_ref).astype(BF16)
            dv_ref[rows, :] = _conv_taps(_anticausal_taps(dvp[pl.ds(base, rc + PAD_SHORT), :], FFN_CONV, rc), wv_ref).astype(BF16)
            return carry

        lax.fori_loop(0, s // rc, second, 0)

    lo = pl.BlockSpec((s, tc), lambda j: (0, j))
    hi = pl.BlockSpec((s, tc), lambda j: (0, j + nb))
    wlo = pl.BlockSpec((FFN_CONV, tc), lambda j: (0, j))
    whi = pl.BlockSpec((FFN_CONV, tc), lambda j: (0, j + nb))
    act, dgate, dval, dwg, dwv = _pcall(
        body, name=name,
        out_shape=(_sds((s, f), BF16), _sds((s, f), BF16), _sds((s, f), BF16), _sds((FFN_CONV, f), F32), _sds((FFN_CONV, f), F32)),
        grid=(nb,), in_specs=[lo, hi, lo, wlo, whi], out_specs=(lo, lo, lo, wlo, wlo),
        scratch_shapes=[pltpu.VMEM((s + PAD_SHORT, tc), F32) for _ in range(4)], comm=comm,
    )(up, up, dact, wf, wf)
    return act, dgate, dval, jnp.concatenate([dwg, dwv], axis=1)


def _out_proj(yan, yb, ycn, gb, x, w_out, layer, g_next, name, comm=None):
    s, w = yan.shape
    wb = yb.shape[1]
    d = x.shape[1]
    tm = ROW_CHUNK

    def body(ya_ref, yb_ref, yc_ref, gb_ref, x_ref, w_ref, g_ref, y_ref, xm_ref, h_ref):
        ybv = yb_ref[...]
        y = jnp.concatenate([ya_ref[...], (ybv * _rstd(ybv) * gb_ref[...]).astype(BF16), yc_ref[...]], axis=1)
        y_ref[...] = y
        xm = x_ref[...] + _dot(y, w_ref[...])
        xm_ref[...] = xm
        h_ref[...] = (xm * _rstd(xm) * g_ref[...]).astype(BF16)

    def rows(width):
        return pl.BlockSpec((tm, width), lambda i: (i, 0))

    def vec(width):
        return pl.BlockSpec((1, width), lambda i: (0, 0))

    return _pcall(body, name=name, out_shape=(_sds((s, d), BF16), _sds((s, d), F32), _sds((s, d), BF16)), grid=(s // tm,),
                  in_specs=[rows(w), rows(wb), rows(w), vec(wb), rows(d), pl.BlockSpec((None, d, d), lambda i: (layer, 0, 0)), vec(d)],
                  out_specs=(rows(d), rows(d), rows(d)), comm=comm)(yan, yb, ycn, gb, x, w_out, g_next)


def _down_proj(act, w_down, layer, x_mid, g_next, name, comm=None):
    s, f = act.shape
    d = x_mid.shape[1]
    tm = ROW_CHUNK

    def body(a_ref, w_ref, x_ref, *refs):
        xo = x_ref[...] + _dot(a_ref[...], w_ref[...])
        refs[-2 if g_next is not None else -1][...] = xo
        if g_next is not None:
            refs[-1][...] = (xo * _rstd(xo) * refs[0][...]).astype(BF16)

    row = pl.BlockSpec((tm, d), lambda i: (i, 0))
    ins = [act, w_down, x_mid] + ([g_next] if g_next is not None else [])
    in_specs = [pl.BlockSpec((tm, f), lambda i: (i, 0)), pl.BlockSpec((None, f, d), lambda i: (layer, 0, 0)), row]
    in_specs += [pl.BlockSpec((1, d), lambda i: (0, 0))] if g_next is not None else []
    outs = (_sds((s, d), F32), _sds((s, d), BF16)) if g_next is not None else (_sds((s, d), F32),)
    res = _pcall(body, name=name, out_shape=outs, grid=(s // tm,), in_specs=in_specs, out_specs=tuple([row] * len(outs)),
                 comm=comm)(*ins)
    return (res[0], res[1]) if g_next is not None else (res[0], None)


def _proj_dx(pieces, w, layer, nt, x, g, dres, name, comm=None):
    s, d = x.shape
    tm = ROW_CHUNK
    widths = [p.shape[1] for p in pieces]

    def body(*refs):
        p_refs, (w_ref, x_ref, g_ref, dres_ref, dx_ref, dxb_ref, dg_ref) = refs[:len(pieces)], refs[len(pieces):]
        i = pl.program_id(0)
        dh, off = None, 0
        for p_ref, width in zip(p_refs, widths):
            part = _dot_nt(p_ref[...], w_ref[:, off:off + width]) if nt else _dot(p_ref[...], w_ref[off:off + width, :])
            dh = part if dh is None else dh + part
            off += width
        xv = x_ref[...]
        r = _rstd(xv)
        xh = xv * r
        gd = dh * g_ref[...]
        dx = dres_ref[...] + r * (gd - xh * jnp.mean(gd * xh, axis=-1, keepdims=True))
        dx_ref[...] = dx
        dxb_ref[...] = dx.astype(BF16)
        part = jnp.sum(dh * xh, axis=0, keepdims=True)

        @pl.when(i == 0)
        def _():
            dg_ref[...] = part

        @pl.when(i > 0)
        def _():
            dg_ref[...] += part

    row = pl.BlockSpec((tm, d), lambda i: (i, 0))
    vec = pl.BlockSpec((1, d), lambda i: (0, 0))
    w_spec = pl.BlockSpec((None,) + w.shape[1:], lambda i: (layer, 0, 0))
    return _pcall(body, name=name, out_shape=(_sds((s, d), F32), _sds((s, d), BF16), _sds((1, d), F32)), grid=(s // tm,),
                  in_specs=[pl.BlockSpec((tm, width), lambda i: (i, 0)) for width in widths] + [w_spec, row, vec, row],
                  out_specs=(row, row, vec), comm=comm)(*pieces, w, x, g, dres)


def _yb_norm_bwd(yb, dy, gb, name, comm=None):
    s, wb = yb.shape
    w = wb // 2
    heads = wb // D_HEAD
    tm = ROW_CHUNK

    def body(yb_ref, d1_ref, d2_ref, g_ref, dyb_ref, dl_ref, dg_ref):
        i = pl.program_id(0)
        y = yb_ref[...]
        dyn = jnp.concatenate([d1_ref[...], d2_ref[...]], axis=1)
        r = _rstd(y)
        yh = y * r
        gd = dyn * g_ref[...]
        dyb = r * (gd - yh * jnp.mean(gd * yh, axis=-1, keepdims=True))
        dyb_ref[...] = dyb
        part = jnp.sum(dyn * yh, axis=0, keepdims=True)
        prod = dyb * y
        even = lax.broadcasted_iota(I32, (tm, LANES), 1) < D_HEAD
        for p in range(heads // 2):
            blk = prod[:, p * LANES:(p + 1) * LANES]
            ev = jnp.sum(jnp.where(even, blk, 0.0), axis=1, keepdims=True)
            od = jnp.sum(jnp.where(even, 0.0, blk), axis=1, keepdims=True)
            dl_ref[2 * p] = jnp.broadcast_to(ev, (tm, LANES))
            dl_ref[2 * p + 1] = jnp.broadcast_to(od, (tm, LANES))

        @pl.when(i == 0)
        def _():
            dg_ref[...] = part

        @pl.when(i > 0)
        def _():
            dg_ref[...] += part

    return _pcall(
        body, name=name, out_shape=(_sds((s, wb), F32), _sds((heads, s, LANES), F32), _sds((1, wb), F32)),
        grid=(s // tm,),
        in_specs=[pl.BlockSpec((tm, wb), lambda i: (i, 0)), pl.BlockSpec((tm, w), lambda i: (i, 1)),
                  pl.BlockSpec((tm, w), lambda i: (i, 2)), pl.BlockSpec((1, wb), lambda i: (0, 0))],
        out_specs=(pl.BlockSpec((tm, wb), lambda i: (i, 0)), pl.BlockSpec((heads, tm, LANES), lambda i: (0, i, 0)),
                   pl.BlockSpec((1, wb), lambda i: (0, 0))), comm=comm,
    )(yb, dy, dy, gb)


def _t5_bucket_table():
    max_exact = NUM_BUCKETS // 2
    out = np.full((len(DILATED_BRANCHES), BLK, 2 * BLK), -1, np.int32)
    rel = np.arange(BLK)[:, None] - np.arange(2 * BLK)[None, :] + BLK
    for b, (window, dilation) in enumerate(DILATED_BRANCHES):
        n_keys = window // dilation
        dist = np.maximum(rel, 0) * dilation
        d_f = np.maximum(dist, 1).astype(np.float32)
        large = max_exact + (np.log(d_f / np.float32(max_exact)) / np.float32(math.log(MAX_DISTANCE / max_exact))
                             * np.float32(NUM_BUCKETS - max_exact)).astype(np.int32)
        large = np.minimum(large, NUM_BUCKETS - 1)
        bucket = np.where(dist < max_exact, dist, large)
        out[b] = np.where((rel >= 0) & (rel <= n_keys), bucket, -1)
    return out


def _bias_tiles(rel_bias, buckets, name):
    nbk, heads = rel_bias.shape
    nbr = buckets.shape[0]

    def body(rb_ref, bk_ref, o_ref):
        for br in range(nbr):
            bk = bk_ref[br]
            tiles = [jnp.full((BLK, 2 * BLK), NEG, F32) for _ in range(heads)]
            for b in range(nbk):
                hit = bk == b
                tiles = [jnp.where(hit, rb_ref[b, h], tiles[h]) for h in range(heads)]
            for h in range(heads):
                o_ref[br, h] = tiles[h]

    return _pcall(body, name=name, out_shape=_sds((nbr, heads, BLK, 2 * BLK), F32), grid=(1,),
                  in_specs=[pl.BlockSpec(memory_space=pltpu.SMEM), pl.BlockSpec(buckets.shape, lambda i: (0, 0, 0))],
                  out_specs=pl.BlockSpec((nbr, heads, BLK, 2 * BLK), lambda i: (0, 0, 0, 0)))(rel_bias, buckets)


def _bias_grad(dtiles, buckets, nbk, name):
    nbr, heads = dtiles.shape[:2]

    def body(dt_ref, bk_ref, o_ref):
        row = lax.broadcasted_iota(I32, (nbk, LANES), 0)
        col = lax.broadcasted_iota(I32, (nbk, LANES), 1)
        out = jnp.zeros((nbk, LANES), F32)
        for h in range(heads):
            for b in range(nbk):
                tot = jnp.zeros((), F32)
                for br in range(nbr):
                    tot = tot + jnp.sum(jnp.where(bk_ref[br] == b, dt_ref[br, h], 0.0))
                out = jnp.where((row == b) & (col == h), tot, out)
        o_ref[...] = out

    return _pcall(body, name=name, out_shape=_sds((nbk, LANES), F32), grid=(1,),
                  in_specs=[pl.BlockSpec(dtiles.shape, lambda i: (0, 0, 0, 0)), pl.BlockSpec(buckets.shape, lambda i: (0, 0, 0))],
                  out_specs=pl.BlockSpec((nbk, LANES), lambda i: (0, 0)))(dtiles, buckets)


def _largest_divisor(n, cap):
    return max(g for g in range(1, cap + 1) if n % g == 0)


def _attn_blocks(s, visit, group):
    for br, (window, d) in enumerate(DILATED_BRANCHES):
        n_blk = (s // d) // BLK
        span = BLK * d
        g1 = _largest_divisor(d, group)

        def firsts(t, carry, br=br, d=d, g1=g1):
            for j in range(g1):
                visit(br, d, t * g1 + j, False)
            return carry

        lax.fori_loop(0, d // g1, firsts, 0)
        if n_blk > 1:
            total = d * (n_blk - 1)
            g2 = _largest_divisor(total, group)

            def rest(t, carry, br=br, d=d, n_blk=n_blk, span=span, g2=g2):
                for j in range(g2):
                    idx = t * g2 + j
                    visit(br, d, idx // (n_blk - 1) + (1 + idx % (n_blk - 1)) * span, True)
                return carry

            lax.fori_loop(0, total // g2, rest, 0)


def _rows(start, size, d):
    return pl.ds(pl.multiple_of(start, BLK), size) if d == 1 else pl.ds(start, size, stride=d)


def _attn_fwd(z, btiles, col0, name, comm=None):
    s = z.shape[0]
    nbr, heads = btiles.shape[:2]
    pairs = heads // 2
    scale = D_HEAD ** -0.5
    rc = ROW_CHUNK

    def body(q_ref, k_ref, v_ref, bt_ref, yb_ref, lse_ref, acc_ref, m_ref, l_ref):
        even = lax.broadcasted_iota(I32, (BLK, LANES), 1) < D_HEAD
        even2 = lax.broadcasted_iota(I32, (2 * BLK, LANES), 1) < D_HEAD

        def visit(br, d, start, prev):
            kw = 2 * BLK if prev else BLK
            rows_q = _rows(start, BLK, d)
            rows_k = _rows(start - BLK * d, kw, d) if prev else rows_q
            qb = q_ref[rows_q, :]
            kb = k_ref[rows_k, :].astype(BF16)
            vw = v_ref[rows_k, :]
            ev_k = even2 if prev else even
            qm = jnp.concatenate([jnp.where(even, qb, 0.0), jnp.where(even, 0.0, qb)], axis=0).astype(BF16)
            bias = [bt_ref[br, e] if prev else bt_ref[br, e, :, BLK:] for e in range(2)]
            sc = _dot_nt(qm, kb) * scale + jnp.concatenate(bias, axis=0)
            m = jnp.max(sc, axis=1, keepdims=True)
            p = jnp.exp(sc - m)
            l = jnp.sum(p, axis=1, keepdims=True)
            pb = p.astype(BF16)
            vm = jnp.concatenate([jnp.where(ev_k, vw, 0.0), jnp.where(ev_k, 0.0, vw)], axis=0).astype(BF16)
            acc_ref.at[br][rows_q, :] = _dot(jnp.concatenate([pb[:BLK], pb[BLK:]], axis=1), vm)
            for e in range(2):
                m_ref.at[br, e][rows_q, :] = jnp.broadcast_to(m[e * BLK:(e + 1) * BLK], (BLK, LANES))
                l_ref.at[br, e][rows_q, :] = jnp.broadcast_to(l[e * BLK:(e + 1) * BLK], (BLK, LANES))

        _attn_blocks(s, visit, ATTN_GROUP_FWD)

        ev_c = lax.broadcasted_iota(I32, (rc, LANES), 1) < D_HEAD

        def merge(i, carry):
            rows = pl.ds(pl.multiple_of(i * rc, rc), rc)
            wts, dens = [], []
            for e in range(2):
                ms = [m_ref[br, e, rows, :] for br in range(nbr)]
                top = functools.reduce(jnp.maximum, ms)
                w = [jnp.exp(mb - top) for mb in ms]
                den = functools.reduce(lambda a, b: a + b, [w[br] * l_ref[br, e, rows, :] for br in range(nbr)])
                lse_ref[e, rows, :] = top + jnp.log(den)
                wts.append(w)
                dens.append(den)
            num = functools.reduce(lambda a, b: a + b,
                                   [jnp.where(ev_c, wts[0][br], wts[1][br]) * acc_ref[br, rows, :] for br in range(nbr)])
            yb_ref[rows, :] = num / jnp.where(ev_c, dens[0], dens[1])
            return carry

        lax.fori_loop(0, s // rc, merge, 0)

    def zcol(j):
        return pl.BlockSpec((s, LANES), lambda p, j=j: (0, col0 + j + p))

    return _pcall(
        body, name=name, out_shape=(_sds((s, pairs * LANES), F32), _sds((heads, s, LANES), F32)), grid=(pairs,),
        in_specs=[zcol(0), zcol(pairs), zcol(2 * pairs), pl.BlockSpec((nbr, 2, BLK, 2 * BLK), lambda p: (0, p, 0, 0))],
        out_specs=(pl.BlockSpec((s, LANES), lambda p: (0, p)), pl.BlockSpec((2, s, LANES), lambda p: (p, 0, 0))),
        scratch_shapes=[pltpu.VMEM((nbr, s, LANES), F32), pltpu.VMEM((nbr, 2, s, LANES), F32), pltpu.VMEM((nbr, 2, s, LANES), F32)],
        comm=comm,
    )(z, z, z, btiles)


def _attn_bwd(z, btiles, dyb, lse, delta, dbias_in, col0, name, comm=None):
    s = z.shape[0]
    nbr, heads = btiles.shape[:2]
    pairs = heads // 2
    scale = D_HEAD ** -0.5

    def body(q_ref, k_ref, v_ref, bt_ref, dy_ref, lse_ref, dl_ref, dbi_ref,
             dq_ref, dk_ref, dv_ref, db_ref, dqa, dka, dva):
        even = lax.broadcasted_iota(I32, (BLK, LANES), 1) < D_HEAD
        even2 = lax.broadcasted_iota(I32, (2 * BLK, LANES), 1) < D_HEAD
        for ref in (dqa, dka, dva):
            ref[...] = jnp.zeros((s, LANES), F32)
        db_ref[...] = dbi_ref[...]

        def visit(br, d, start, prev):
            kw = 2 * BLK if prev else BLK
            rows_q = _rows(start, BLK, d)
            rows_k = _rows(start - BLK * d, kw, d) if prev else rows_q
            qb = q_ref[rows_q, :]
            dyv = dy_ref[rows_q, :]
            kwin = k_ref[rows_k, :]
            kb = kwin.astype(BF16)
            vb = v_ref[rows_k, :].astype(BF16)
            ev_k = even2 if prev else even
            qm = jnp.concatenate([jnp.where(even, qb, 0.0), jnp.where(even, 0.0, qb)], axis=0).astype(BF16)
            dym = jnp.concatenate([jnp.where(even, dyv, 0.0), jnp.where(even, 0.0, dyv)], axis=0).astype(BF16)
            bias = [bt_ref[br, e] if prev else bt_ref[br, e, :, BLK:] for e in range(2)]
            sc = _dot_nt(qm, kb) * scale + jnp.concatenate(bias, axis=0)
            lt = jnp.concatenate([lse_ref.at[e][rows_q, :] for e in range(2)], axis=0)
            dt = jnp.concatenate([dl_ref.at[e][rows_q, :] for e in range(2)], axis=0)
            if prev:
                lt = jnp.concatenate([lt, lt], axis=1)
                dt = jnp.concatenate([dt, dt], axis=1)
            p = jnp.exp(sc - lt)
            ds = p * (_dot_nt(dym, vb) - dt)
            for e in range(2):
                if prev:
                    db_ref[br, e] += ds[e * BLK:(e + 1) * BLK]
                else:
                    db_ref[br, e, :, BLK:] += ds[e * BLK:(e + 1) * BLK]
            dsb = ds.astype(BF16)
            km = jnp.concatenate([jnp.where(ev_k, kwin, 0.0), jnp.where(ev_k, 0.0, kwin)], axis=0).astype(BF16)
            dqa[rows_q, :] += _dot(jnp.concatenate([dsb[:BLK], dsb[BLK:]], axis=1), km) * scale
            dka[rows_k, :] += _dot_tn(dsb, qm) * scale
            dva[rows_k, :] += _dot_tn(p.astype(BF16), dym)

        _attn_blocks(s, visit, ATTN_GROUP_BWD)
        dq_ref[...] = dqa[...].astype(BF16)
        dk_ref[...] = dka[...].astype(BF16)
        dv_ref[...] = dva[...].astype(BF16)

    def zcol(j):
        return pl.BlockSpec((s, LANES), lambda p, j=j: (0, col0 + j + p))

    col = pl.BlockSpec((s, LANES), lambda p: (0, p))
    stat = pl.BlockSpec((2, s, LANES), lambda p: (p, 0, 0))
    tile = pl.BlockSpec((nbr, 2, BLK, 2 * BLK), lambda p: (0, p, 0, 0))
    wide = _sds((s, pairs * LANES), BF16)
    return _pcall(
        body, name=name, out_shape=(wide, wide, wide, _sds(btiles.shape, F32)), grid=(pairs,),
        in_specs=[zcol(0), zcol(pairs), zcol(2 * pairs), tile, col, stat, stat, tile],
        out_specs=(col, col, col, tile),
        scratch_shapes=[pltpu.VMEM((s, LANES), F32) for _ in range(3)], comm=comm,
    )(z, z, z, btiles, dyb, lse, delta, dbias_in)


def _row(v):
    return v.reshape(1, -1)


class _Rides:
    def __init__(self):
        self.table, self.grads = {}, {}

    def add(self, name, build):
        self.table.setdefault(name, []).append(build)

    def get(self, name):
        comm = None
        for build in self.table.get(name, ()):
            comm = build() if comm is None else _both(comm, build())
        return comm

    def ready(self, key, g):
        self.grads[key] = g


class _LocalSchedule:
    def __init__(self):
        self.big = {}

    def fwd_comms(self, l):
        return _Rides()

    def bwd_comms(self, l):
        return _Rides()

    def after_bwd(self, l, grads):
        self.big[l] = grads


def _layer_fwd(l, x, h, wts, prm, btiles, comms):
    d = x.shape[1]
    wq = d // 4
    depth = prm["norm_mix_g"].shape[0]
    gout = prm["out_norm_g"][l]
    z = _mm_n(h, wts["in_t"], l, nt=True, tn=256, out_dtype=F32, name="in_proj", comm=comms.get("in_proj"))
    yan, ycn = _mix_fwd(z, prm["conv_a_w"][l], prm["conv_c_w"][l], _row(prm["conv_c_b"][l]), _row(prm["ln_c_g"][l]),
                        _row(prm["ln_c_b"][l]), _row(gout[:wq]), _row(gout[3 * wq:]), "mix_fwd")
    yb, lse = _attn_fwd(z, btiles, 3 * wq // LANES, "attn_fwd", comm=comms.get("attn_fwd"))
    y, x_mid, h2 = _out_proj(yan, yb, ycn, _row(gout[wq:3 * wq]), x, wts["out"], l, _row(prm["norm_ffn_g"][l]),
                             "out_proj", comm=comms.get("out_proj"))
    up = _mm_n(h2, wts["up"], l, nt=False, tn=512, out_dtype=BF16, name="up_proj", comm=comms.get("up_proj"))
    act = _ffn_act_fwd(up, prm["conv_f_w"][l], "ffn_act_fwd", comm=comms.get("ffn_act_fwd"))
    g_next = _row(prm["norm_mix_g"][l + 1]) if l + 1 < depth else None
    x_out, h_next = _down_proj(act, wts["down"], l, x_mid, g_next, "down_proj", comm=comms.get("down_proj"))
    return x_out, h_next, (x, h, z, yb, lse, y, x_mid, h2, up)


def _layer_bwd(l, dxo, dxo_b, saved, wts, prm, btiles, dbias, comms):
    x, h, z, yb, lse, y, x_mid, h2, up = saved
    d = x.shape[1]
    wq = d // 4
    gout = prm["out_norm_g"][l]
    dact = _mm_n(dxo_b, wts["down"], l, nt=True, tn=256, out_dtype=BF16, name="down_proj_dx", comm=comms.get("down_proj_dx"))
    act, dgate, dval, dwf = _ffn_act_bwd(up, dact, prm["conv_f_w"][l], "ffn_act_bwd", comm=comms.get("ffn_act_bwd"))
    g_down = _mm_tn(act, dxo_b, t=256, name="down_proj_dw")
    comms.ready("down", g_down)
    dxm, dxm_b, dg_ffn = _proj_dx([dgate, dval], wts["up"], l, True, x_mid, _row(prm["norm_ffn_g"][l]), dxo, "up_proj_dx",
                                  comm=comms.get("up_proj_dx"))
    g_up = _mm_tn2(h2, dgate, dval, t=256, name="up_proj_dw", comm=comms.get("up_proj_dw"))
    comms.ready("up", g_up)
    dy = _mm_n(dxm_b, wts["out"], l, nt=True, tn=256, out_dtype=F32, name="out_proj_dx")
    g_out = _mm_tn(y, dxm_b, t=256, name="out_proj_dw")
    comms.ready("out", g_out)
    dza, dzc, dwa, dwc, dcb, dlg, dlb, dga, dgc = _mix_bwd(
        z, dy, prm["conv_a_w"][l], prm["conv_c_w"][l], _row(prm["conv_c_b"][l]), _row(prm["ln_c_g"][l]),
        _row(prm["ln_c_b"][l]), _row(gout[:wq]), _row(gout[3 * wq:]), "mix_bwd", comm=comms.get("mix_bwd"))
    dyb, delta, dgb = _yb_norm_bwd(yb, dy, _row(gout[wq:3 * wq]), "yb_norm_bwd", comm=comms.get("yb_norm_bwd"))
    dq, dk, dv, dbias = _attn_bwd(z, btiles, dyb, lse, delta, dbias, 3 * wq // LANES, "attn_bwd",
                                  comm=comms.get("attn_bwd"))
    dz = [dza, dq, dk, dv, dzc]
    dx, dx_b, dg_mix = _proj_dx(dz, wts["in_t"], l, False, x, _row(prm["norm_mix_g"][l]), dxm, "in_proj_dx",
                                comm=comms.get("in_proj_dx"))
    g_in_t = _mm_tn_pieces(dz, h, t=256, name="in_proj_dw")
    big = {"in_t": g_in_t, "out": g_out, "up": g_up, "down": g_down}
    small = {"norm_mix_g": dg_mix[0], "conv_a_w": dwa, "conv_c_w": dwc, "conv_c_b": dcb[0], "ln_c_g": dlg[0],
             "ln_c_b": dlb[0], "out_norm_g": jnp.concatenate([dga[0], dgb[0], dgc[0]]), "norm_ffn_g": dg_ffn[0],
             "conv_f_w": dwf}
    return dx, dx_b, big, small, dbias


def _local_step(x, tgt, wts, prm, sched):
    depth = prm["norm_mix_g"].shape[0]
    buckets = jnp.asarray(_t5_bucket_table())
    btiles = _bias_tiles(prm["rel_bias"], buckets, "bias_tiles")
    saved = []
    h = _rms_fwd(x, _row(prm["norm_mix_g"][0]), "rms_mix_fwd")
    for l in range(depth):
        x, h, sv = _layer_fwd(l, x, h, wts, prm, btiles, sched.fwd_comms(l))
        saved.append(sv)
    loss, dx, dx_b, dg_final = _final_loss(x, _row(prm["final_g"]), tgt, "final_loss")
    dbias = jnp.zeros(btiles.shape, F32)
    small = [None] * depth
    for l in reversed(range(depth)):
        dx, dx_b, grads, small[l], dbias = _layer_bwd(l, dx, dx_b, saved[l], wts, prm, btiles, dbias, sched.bwd_comms(l))
        sched.after_bwd(l, grads)
    nbk, heads = prm["rel_bias"].shape
    d_rel = _bias_grad(dbias, buckets, nbk, "bias_grad")[:, :heads]
    return loss, dx, small, d_rel, dg_final[0]


BIG = ("in_t", "out", "up", "down")
COL_SHARDED = ("up",)
N_CHIPS = 4
N_DEV = 8
BF16_ROWS = 16


def _me():
    return lax.axis_index("x"), lax.axis_index("y"), lax.axis_index("c")


def _chip_of(x, y):
    return 2 * x + y


def _other_chips(x, y):
    return ((1 - x, y), (x, 1 - y), (1 - x, 1 - y))


def _remote(src, dst, send_sem, recv_sem, device):
    return pltpu.make_async_remote_copy(src_ref=src, dst_ref=dst, send_sem=send_sem, recv_sem=recv_sem,
                                        device_id=device, device_id_type=MESH)


def _ag_comm(wts, layer, ici_keys, fwd_keys):
    keys = tuple(k for k in BIG if k in ici_keys or k in fwd_keys)

    def geo(k):
        _, rows, cols = wts[k].shape
        return (rows, cols // N_CHIPS) if k in COL_SHARDED else (rows // N_CHIPS, cols)

    def copies(refs, sems):
        g = dict(zip(keys, refs))
        isend, irecv, dsend, drecv = sems
        x, y, c = _me()
        mine = _chip_of(x, y)

        def region(k, chip, half):
            r, cc = geo(k)
            h = r // 2
            if k in COL_SHARDED:
                return g[k].at[layer, pl.ds(pl.multiple_of(half * h, BF16_ROWS), h), pl.ds(pl.multiple_of(chip * cc, LANES), cc)]
            return g[k].at[layer, pl.ds(pl.multiple_of(chip * r + half * h, BF16_ROWS), h), :]

        def ici(k, f, landing):
            chip = _other_chips(x, y)[f]
            where = region(k, _chip_of(*chip) if landing else mine, c)
            i = keys.index(k)
            return _remote(where, where, isend.at[i, f], irecv.at[i, f], (*chip, c))

        def fwd(k, f, landing):
            chip = _other_chips(x, y)[f]
            where = region(k, _chip_of(*chip), 1 - c if landing else c)
            i = keys.index(k)
            return _remote(where, where, dsend.at[i, f], drecv.at[i, f], (x, y, 1 - c))

        return ici, fwd

    def start(ins, outs, sems):
        ici, fwd = copies(outs, sems)
        for k in keys:
            for f in range(3):
                if k in ici_keys:
                    ici(k, f, False).start()
                else:
                    fwd(k, f, False).start()

    def finish(ins, outs, sems):
        ici, fwd = copies(outs, sems)
        for k in keys:
            for f in range(3):
                if k in ici_keys:
                    ici(k, f, True).wait_recv()
                    if k in fwd_keys:
                        fwd(k, f, False).start()
        for k in keys:
            for f in range(3):
                if k in fwd_keys:
                    fwd(k, f, True).wait_recv()
                    fwd(k, f, False).wait_send()
                if k in ici_keys:
                    ici(k, f, False).wait_send()

    def done(res):
        wts.update(zip(keys, res))

    n = len(keys)
    return _Comm([wts[k] for k in keys], [_sds(wts[k].shape, BF16) for k in keys], {i: i for i in range(n)},
                 [pltpu.SemaphoreType.DMA((n, 3)) for _ in range(4)], start, finish, done)


def _small_gather_comm(slab, store):
    def copies(ins, outs, sems):
        send, recv, lsem = sems
        x, y, c = _me()
        mine = _chip_of(x, y)
        own = pltpu.make_async_copy(ins[0], outs[0].at[mine], lsem)
        pairs = []
        for f, chip in enumerate(_other_chips(x, y)):
            out = _remote(ins[0], outs[0].at[mine], send.at[f], recv.at[f], (*chip, c))
            land = _remote(ins[0], outs[0].at[_chip_of(*chip)], send.at[f], recv.at[f], (*chip, c))
            pairs.append((out, land))
        return own, pairs

    def start(ins, outs, sems):
        own, pairs = copies(ins, outs, sems)
        own.start()
        for out, _ in pairs:
            out.start()

    def finish(ins, outs, sems):
        own, pairs = copies(ins, outs, sems)
        for out, land in pairs:
            land.wait_recv()
            out.wait_send()
        own.wait()

    def done(res):
        store["small"] = res[0]

    return _Comm([slab], [_sds((N_CHIPS,) + slab.shape, F32)], {},
                 [pltpu.SemaphoreType.DMA((3,)), pltpu.SemaphoreType.DMA((3,)), pltpu.SemaphoreType.DMA], start, finish, done)


def _piece_geo(g):
    geo = {}
    for k in g:
        rows, cols = g[k].shape
        geo[k] = (rows // 2, cols // N_CHIPS) if k in COL_SHARDED else (rows // (2 * N_CHIPS), cols)
    return geo


def _swap_comm(g, keys, done):
    geo = _piece_geo(g)
    n_copies = sum(N_CHIPS if k in COL_SHARDED else 1 for k in keys)

    def copies(ins, outs, sems):
        g_refs, t_refs = dict(zip(keys, ins)), dict(zip(keys, outs))
        send, recv = sems
        x, y, c = _me()
        pairs = []
        for k in keys:
            h, cc = geo[k]
            if k in COL_SHARDED:
                rows = pl.ds(pl.multiple_of((1 - c) * h, BF16_ROWS), h)
                pairs += [(g_refs[k].at[rows, pl.ds(j * cc, cc)], t_refs[k].at[j]) for j in range(N_CHIPS)]
            else:
                pairs.append((g_refs[k].at[:, 1 - c], t_refs[k]))
        return [_remote(src, dst, send.at[i], recv.at[i], (x, y, 1 - c)) for i, (src, dst) in enumerate(pairs)]

    def start(ins, outs, sems):
        for cp in copies(ins, outs, sems):
            cp.start()

    def finish(ins, outs, sems):
        for cp in copies(ins, outs, sems):
            cp.wait()

    ins = [g[k] if k in COL_SHARDED else g[k].reshape(N_CHIPS, 2, geo[k][0], geo[k][1]) for k in keys]
    return _Comm(ins, [_sds((N_CHIPS,) + geo[k], BF16) for k in keys], {},
                 [pltpu.SemaphoreType.DMA((n_copies,)) for _ in range(2)], start, finish,
                 lambda res: done(dict(zip(keys, res))))


def _pair_sum(g, theirs, c_arr, keys):
    geo = _piece_geo(g)

    def body(c_ref, *refs):
        nk = len(keys)
        for i in range(nk):
            refs[2 * nk + i][...] = (refs[i][...].astype(F32) + refs[nk + i][...].astype(F32)).astype(BF16)

    in_specs, ins = [], []
    for k in keys:
        h, cc = geo[k]
        if k in COL_SHARDED:
            in_specs.append(pl.BlockSpec((h, cc), lambda j, c_ref: (c_ref[0], j)))
            ins.append(g[k])
        else:
            in_specs.append(pl.BlockSpec((None, h, cc), lambda j, c_ref: (2 * j + c_ref[0], 0, 0)))
            ins.append(g[k].reshape(2 * N_CHIPS, h, cc))
    slab = [pl.BlockSpec((None,) + geo[k], lambda j, c_ref: (j, 0, 0)) for k in keys]
    res = _pcall(body, name="rs_pair_sum", out_shape=tuple(_sds((N_CHIPS,) + geo[k], BF16) for k in keys), grid=(N_CHIPS,),
                 in_specs=in_specs + slab, out_specs=tuple(slab), prefetch=1)(c_arr, *ins, *[theirs[k] for k in keys])
    return dict(zip(keys, res))


def _rs_comm(p, keys, store):
    def copies(ins, outs, sems):
        send, recv = sems
        x, y, c = _me()
        return [_remote(ins[i].at[_chip_of(*chip)], outs[i].at[f], send.at[i, f], recv.at[i, f], (*chip, c))
                for i in range(len(keys)) for f, chip in enumerate(_other_chips(x, y))]

    def start(ins, outs, sems):
        for cp in copies(ins, outs, sems):
            cp.start()

    def finish(ins, outs, sems):
        for cp in copies(ins, outs, sems):
            cp.wait()

    def done(res):
        store.update(zip(keys, res))

    return _Comm([p[k] for k in keys], [_sds((3,) + p[k].shape[1:], BF16) for k in keys], {},
                 [pltpu.SemaphoreType.DMA((len(keys), 3)) for _ in range(2)], start, finish, done)


def _quad_sum(p, b, where, l, full):
    parts = 2
    nk = len(BIG)

    def body(where_ref, *refs):
        for i in range(nk):
            acc = refs[i][...].astype(F32)
            for f in range(3):
                acc = acc + refs[nk + 3 * i + f][...].astype(F32)
            refs[5 * nk + i][...] = acc

    own, recv, outs = [], [], []
    for k in BIG:
        h, cc = p[k].shape[1:]
        th = h // parts
        own.append(pl.BlockSpec((None, th, cc), lambda i, w_ref: (w_ref[0], i, 0)))
        recv += [pl.BlockSpec((None, th, cc), lambda i, w_ref, f=f: (f, i, 0)) for f in range(3)]
        outs.append(pl.BlockSpec((None, None, th, cc), lambda i, w_ref: (l, w_ref[1], i, 0)))
    args = [p[k] for k in BIG] + [b[k] for k in BIG for _ in range(3)] + [full[k] for k in BIG]
    res = _pcall(body, name="rs_quad_sum", out_shape=tuple(_sds(full[k].shape, F32) for k in BIG), grid=(parts,),
                 in_specs=own + recv + [ANY] * nk, out_specs=tuple(outs), prefetch=1,
                 aliases={1 + 4 * nk + i: i for i in range(nk)})(where, *args)
    return dict(zip(BIG, res))


def _share_comm(layers, full, done):
    nk = len(BIG)

    def copies(outs, sems, landing):
        send, recv = sems
        x, y, c = _me()
        half = 1 - c if landing else c
        return [_remote(outs[i].at[l, half], outs[i].at[l, half], send.at[i, j], recv.at[i, j], (x, y, 1 - c))
                for i in range(nk) for j, l in enumerate(layers)]

    def start(ins, outs, sems):
        for cp in copies(outs, sems, False):
            cp.start()

    def finish(ins, outs, sems):
        for cp in copies(outs, sems, True):
            cp.wait_recv()
        for cp in copies(outs, sems, False):
            cp.wait_send()

    return _Comm([full[k] for k in BIG], [_sds(full[k].shape, F32) for k in BIG], {i: i for i in range(nk)},
                 [pltpu.SemaphoreType.DMA((nk, len(layers))) for _ in range(2)], start, finish,
                 lambda res: done(dict(zip(BIG, res))))


def _gather_comm(slab, done):
    def copies(ins, outs, sems, landing):
        send, recv = sems
        x, y, c = _me()
        me = 4 * x + 2 * y + c
        out = []
        for mask in range(1, N_DEV):
            peer = (x ^ (mask >> 2), y ^ ((mask >> 1) & 1), c ^ (mask & 1))
            slot = 4 * peer[0] + 2 * peer[1] + peer[2] if landing else me
            out.append(_remote(ins[0], outs[0].at[slot], send.at[mask - 1], recv.at[mask - 1], peer))
        return out

    def start(ins, outs, sems):
        for cp in copies(ins, outs, sems, False):
            cp.start()

    def finish(ins, outs, sems):
        for cp in copies(ins, outs, sems, True):
            cp.wait_recv()
        for cp in copies(ins, outs, sems, False):
            cp.wait_send()

    return _Comm([slab], [_sds((N_DEV,) + slab.shape, F32)], {},
                 [pltpu.SemaphoreType.DMA((N_DEV - 1,)), pltpu.SemaphoreType.DMA((N_DEV - 1,))], start, finish,
                 lambda res: done(res[0]))


def _sum_slabs(slabs, own, me):
    n, r, lanes = slabs.shape
    tr = r // 2

    def body(me_ref, s_ref, own_ref, o_ref):
        o_ref[...] = jnp.zeros((tr, lanes), F32)
        for i in range(n):
            @pl.when(me_ref[0] == i)
            def _():
                o_ref[...] += own_ref[...]

            @pl.when(me_ref[0] != i)
            def _():
                o_ref[...] += s_ref[i]

    return _pcall(body, name="sum_partials", out_shape=_sds((r, lanes), F32), grid=(2,),
                  in_specs=[pl.BlockSpec((n, tr, lanes), lambda i, me_ref: (0, i, 0)),
                            pl.BlockSpec((tr, lanes), lambda i, me_ref: (i, 0))],
                  out_specs=pl.BlockSpec((tr, lanes), lambda i, me_ref: (i, 0)), prefetch=1)(me, slabs, own)


def _cast_into_gathered(w, chip, by_cols, name):
    l, r, c = w.shape

    def body(chip_ref, w_ref, o_ref):
        o_ref[...] = w_ref[...].astype(BF16)

    if by_cols:
        shape, out = (l, r, N_CHIPS * c), pl.BlockSpec((None, r, c), lambda i, chip_ref: (i, 0, chip_ref[0]))
    else:
        shape, out = (l, N_CHIPS * r, c), pl.BlockSpec((None, r, c), lambda i, chip_ref: (i, chip_ref[0], 0))
    return _pcall(body, name=name, out_shape=_sds(shape, BF16), grid=(l,),
                  in_specs=[pl.BlockSpec((None, r, c), lambda i, chip_ref: (i, 0, 0))], out_specs=out, prefetch=1)(chip, w)


def _adamw_math(w, g, m, v):
    mn = ADAM_B1 * m + (1.0 - ADAM_B1) * g
    vn = ADAM_B2 * v + (1.0 - ADAM_B2) * (g * g)
    m_hat = mn / (1.0 - ADAM_B1 ** ADAM_STEP)
    v_hat = vn / (1.0 - ADAM_B2 ** ADAM_STEP)
    return -ADAM_LR * (m_hat / (jnp.sqrt(v_hat) + ADAM_EPS) + ADAM_WD * w), mn, vn


def _adamw(w, g, m, v, name, tr):
    r, c = w.shape

    def body(w_ref, g_ref, m_ref, v_ref, d_ref, mo_ref, vo_ref):
        d_ref[...], mo_ref[...], vo_ref[...] = _adamw_math(w_ref[...], g_ref[...], m_ref[...], v_ref[...])

    blk = pl.BlockSpec((tr, c), lambda i: (i, 0))
    return _pcall(body, name=name, out_shape=tuple(_sds((r, c), F32) for _ in range(3)), grid=(r // tr,),
                  in_specs=[blk] * 4, out_specs=(blk, blk, blk))(w, g, m, v)


def _adamw_small(groups, name):
    count = len(groups[0])
    shapes = [a.shape for a in groups[0]]
    as2d = [(math.prod(s[:-1]), s[-1]) for s in shapes]

    def body(*refs):
        for i in range(count):
            out = _adamw_math(*[refs[j * count + i][...] for j in range(4)])
            for j in range(3):
                refs[(4 + j) * count + i][...] = out[j]

    specs = [pl.BlockSpec(s, lambda i: (0, 0)) for s in as2d]
    res = _pcall(body, name=name, out_shape=tuple(_sds(s, F32) for _ in range(3) for s in as2d), grid=(1,),
                 in_specs=specs * 4, out_specs=tuple(specs * 3))(*[a.reshape(s) for grp in groups for a, s in zip(grp, as2d)])
    return [[res[j * count + i].reshape(shapes[i]) for i in range(count)] for j in range(3)]


AG_RIDES = {"in_proj": (0, ("out",), ("down",)), "attn_fwd": (0, ("up",), ("out",)), "out_proj": (0, (), ("up",)),
            "up_proj": (1, ("in_t",), ()), "ffn_act_fwd": (1, ("down",), ()), "down_proj": (1, (), ("in_t",))}
AG_FIRST = ("in_t", "down")
RS_RIDES = {"down_proj_dx": "swap", "ffn_act_bwd": ("up",), "up_proj_dx": "share", "attn_bwd": ("in_t", "out", "down")}
RS_RIDES_LAST = {"down_proj_dx": "swap", "ffn_act_bwd": ("up",), "up_proj_dx": "share", "up_proj_dw": ("in_t",),
                 "mix_bwd": ("down",), "yb_norm_bwd": ("out",)}
EARLY = ("out", "up", "down")
EARLY_RIDES = {"mix_bwd": "swap", "attn_bwd": ("up", "down"), "in_proj_dx": ("out",)}


class _Reduction:
    def __init__(self, grads):
        self.grads, self.pairs, self.recv = grads, {}, {}


class _MeshSchedule:
    def __init__(self, wts, depth, c_arr, where):
        self.wts, self.depth, self.c_arr, self.where = wts, depth, c_arr, where
        self.pending, self.last, self.full, self.unshared = None, None, None, []

    def fwd_comms(self, l):
        rides = _Rides()
        for name, (off, ici, fwd) in AG_RIDES.items():
            if l + off == 0:
                ici, fwd = (tuple(k for k in keys if k not in AG_FIRST) for keys in (ici, fwd))
            if l + off < self.depth and (ici or fwd):
                rides.add(name, lambda ride=(l + off, ici, fwd): _ag_comm(self.wts, *ride))
        return rides

    def _shared(self, full):
        self.full, self.unshared = full, []

    def _ride(self, red, what, swap_keys):
        if what == "swap":
            return _swap_comm(red.grads, swap_keys, lambda theirs: red.pairs.update(
                _pair_sum(red.grads, theirs, self.c_arr, swap_keys)))
        if what == "share":
            return _share_comm(self.unshared, self.full, self._shared)
        return _rs_comm(red.pairs, what, red.recv)

    def bwd_comms(self, l):
        rides = _Rides()
        if self.pending is not None:
            for name, what in (RS_RIDES_LAST if l == 0 else RS_RIDES).items():
                if what != "share" or self.unshared:
                    rides.add(name, lambda what=what, red=self.pending: self._ride(red, what, BIG))
        if l == 0:
            self.last = _Reduction(rides.grads)
            for name, what in EARLY_RIDES.items():
                rides.add(name, lambda what=what: self._ride(self.last, what, EARLY))
        return rides

    def _reduce(self, l, red):
        self.full = _quad_sum(red.pairs, red.recv, self.where, l, self.full)
        self.unshared = self.unshared + [l]

    def after_bwd(self, l, grads):
        if self.pending is not None:
            self._reduce(l + 1, self.pending)
        if self.full is None:
            geo = _piece_geo(grads)
            self.full = {k: jnp.zeros((self.depth, 2) + geo[k], F32) for k in BIG}
        self.pending = _Reduction(grads) if l > 0 else None
        if l == 0:
            self.last.grads = grads

    def finish(self, extra):
        red = self.last
        late = tuple(k for k in BIG if k not in red.pairs)
        _run_comm(self._ride(red, "swap", late), "rs_swap_halves")
        _run_comm(_both(_rs_comm(red.pairs, late, red.recv), extra), "rs_to_owners")
        self._reduce(0, red)
        _run_comm(_share_comm(self.unshared, self.full, self._shared), "rs_share")
        return self.full


SHARDED_SMALL = ("conv_a_w", "conv_c_w", "conv_f_w")
SMALL = ("norm_mix_g", "conv_a_w", "conv_c_w", "conv_c_b", "ln_c_g", "ln_c_b", "out_norm_g", "norm_ffn_g",
         "conv_f_w", "rel_bias", "final_g")
SLAB_ROWS = 16


def _pack(arrays):
    flat = jnp.concatenate([a.reshape(-1) for a in arrays])
    unit = SLAB_ROWS * LANES
    total = -(-flat.shape[0] // unit) * unit
    return jnp.pad(flat, (0, total - flat.shape[0])).reshape(-1, LANES)


def _unpack(slab, shapes):
    flat = slab.reshape(-1)
    out, off = [], 0
    for shp in shapes:
        size = math.prod(shp)
        out.append(flat[off:off + size].reshape(shp))
        off += size
    return out


def kernel(x, norm_mix_g, w_in, conv_a_w, conv_c_w, conv_c_b, ln_c_g, ln_c_b, out_norm_g, w_out, norm_ffn_g, w_up, conv_f_w, w_down, rel_bias, final_g, loss_target, m_norm_mix_g, m_w_in, m_conv_a_w, m_conv_c_w, m_conv_c_b, m_ln_c_g, m_ln_c_b, m_out_norm_g, m_w_out, m_norm_ffn_g, m_w_up, m_conv_f_w, m_w_down, m_rel_bias, m_final_g, v_norm_mix_g, v_w_in, v_conv_a_w, v_conv_c_w, v_conv_c_b, v_ln_c_g, v_ln_c_b, v_out_norm_g, v_w_out, v_norm_ffn_g, v_w_up, v_conv_f_w, v_w_down, v_rel_bias, v_final_g):
    weights = dict(norm_mix_g=norm_mix_g, w_in=w_in, conv_a_w=conv_a_w, conv_c_w=conv_c_w, conv_c_b=conv_c_b,
                   ln_c_g=ln_c_g, ln_c_b=ln_c_b, out_norm_g=out_norm_g, w_out=w_out, norm_ffn_g=norm_ffn_g, w_up=w_up,
                   conv_f_w=conv_f_w, w_down=w_down, rel_bias=rel_bias, final_g=final_g)
    mom_m = dict(norm_mix_g=m_norm_mix_g, w_in=m_w_in, conv_a_w=m_conv_a_w, conv_c_w=m_conv_c_w, conv_c_b=m_conv_c_b,
                 ln_c_g=m_ln_c_g, ln_c_b=m_ln_c_b, out_norm_g=m_out_norm_g, w_out=m_w_out, norm_ffn_g=m_norm_ffn_g,
                 w_up=m_w_up, conv_f_w=m_conv_f_w, w_down=m_w_down, rel_bias=m_rel_bias, final_g=m_final_g)
    mom_v = dict(norm_mix_g=v_norm_mix_g, w_in=v_w_in, conv_a_w=v_conv_a_w, conv_c_w=v_conv_c_w, conv_c_b=v_conv_c_b,
                 ln_c_g=v_ln_c_g, ln_c_b=v_ln_c_b, out_norm_g=v_out_norm_g, w_out=v_w_out, norm_ffn_g=v_norm_ffn_g,
                 w_up=v_w_up, conv_f_w=v_conv_f_w, w_down=v_w_down, rel_bias=v_rel_bias, final_g=v_final_g)
    xi, yi, ci = _me()
    chip = _chip_of(xi, yi)
    c_arr = jnp.reshape(ci, (1,)).astype(I32)
    chip_arr = jnp.reshape(chip, (1,)).astype(I32)
    me_arr = jnp.reshape(4 * xi + 2 * yi + ci, (1,)).astype(I32)
    where = jnp.stack([chip, ci]).astype(I32)
    depth = w_out.shape[0]

    wts = {"in_t": _cast_into_gathered(jnp.swapaxes(w_in, 1, 2), chip_arr, False, "cast_in"),
           "out": _cast_into_gathered(w_out, chip_arr, False, "cast_out"),
           "up": _cast_into_gathered(w_up, chip_arr, True, "cast_up"),
           "down": _cast_into_gathered(w_down, chip_arr, False, "cast_down")}
    store = {}
    _run_comm(_small_gather_comm(_pack([weights[n] for n in SHARDED_SMALL]), store), "ag_small")
    _run_comm(_ag_comm(wts, 0, AG_FIRST, AG_FIRST), "ag_weights")
    prm = {n: weights[n] for n in SMALL if n not in SHARDED_SMALL}
    per_chip = [_unpack(store["small"][j], [weights[n].shape for n in SHARDED_SMALL]) for j in range(N_CHIPS)]
    for i, n in enumerate(SHARDED_SMALL):
        prm[n] = jnp.concatenate([per_chip[j][i] for j in range(N_CHIPS)], axis=-1)

    sched = _MeshSchedule(wts, depth, c_arr, where)
    loss_row, dx, small, d_rel, d_final = _local_step(x[0], loss_target[0], wts, prm, sched)
    loss = lax.psum(loss_row[0, 0], ("x", "y", "c"))

    stacked = {n: jnp.stack([small[l][n] for l in range(depth)]) for n in small[0]}
    stacked["rel_bias"] = d_rel
    stacked["final_g"] = d_final
    full_shapes = [stacked[n].shape for n in SMALL]
    partial = _pack([stacked[n] for n in SMALL])
    reduced = sched.finish(_gather_comm(partial, lambda res: store.update(partials=res)))

    grads = {}
    shard_shapes = {"in_t": jnp.swapaxes(w_in, 1, 2).shape, "out": w_out.shape, "up": w_up.shape, "down": w_down.shape}
    red = {k: reduced[k].reshape(shard_shapes[k]) for k in BIG}
    grads["w_in"] = jnp.swapaxes(red["in_t"], 1, 2)
    grads["w_out"], grads["w_up"], grads["w_down"] = red["out"], red["up"], red["down"]
    delta, new_m, new_v = {}, {}, {}
    for n in ("w_in", "w_out", "w_up", "w_down"):
        shp = weights[n].shape
        flat = lambda a, shp=shp: a.reshape(shp[0] * shp[1], shp[2])
        tile = max(t for t in range(8, 257, 8) if shp[1] % t == 0)
        d, mn, vn = _adamw(flat(weights[n]), flat(grads[n]), flat(mom_m[n]), flat(mom_v[n]), "adamw_" + n, tile)
        delta[n], new_m[n], new_v[n] = d.reshape(shp), mn.reshape(shp), vn.reshape(shp)

    summed = _unpack(_sum_slabs(store["partials"], partial, me_arr), full_shapes)
    for n, g in zip(SMALL, summed):
        if n in SHARDED_SMALL:
            width = weights[n].shape[-1]
            g = lax.dynamic_slice_in_dim(g, chip * width, width, axis=g.ndim - 1)
        grads[n] = g
    res = _adamw_small([[src[n] for n in SMALL] for src in (weights, grads, mom_m, mom_v)], "adamw_small")
    for i, n in enumerate(SMALL):
        delta[n], new_m[n], new_v[n] = res[0][i], res[1][i], res[2][i]

    order = ("norm_mix_g", "w_in", "conv_a_w", "conv_c_w", "conv_c_b", "ln_c_g", "ln_c_b", "out_norm_g", "w_out",
             "norm_ffn_g", "w_up", "conv_f_w", "w_down", "rel_bias", "final_g")
    return (loss, dx[None], *[grads[n] for n in order], *[delta[n] for n in order], *[new_m[n] for n in order],
            *[new_v[n] for n in order])
```

```python
import functools
import math

import numpy as np
import jax
import jax.numpy as jnp
from jax import lax
from jax.experimental import pallas as pl
from jax.experimental.pallas import tpu as pltpu

F32 = jnp.float32
BF16 = jnp.bfloat16
I32 = jnp.int32

EPS = 1e-6
NEG = -1e30
D_HEAD = 64
LANES = 128
BLK = 128
ATTN_GROUP_FWD = 16
ATTN_GROUP_BWD = 16
DILATED_BRANCHES = ((128, 1), (512, 4), (2048, 16))
NUM_BUCKETS = 32
MAX_DISTANCE = 2048
SHORT_CONV = 3
CONFORMER_CONV = 31
FFN_CONV = 3
PAD_SHORT = 8
PAD_LONG = 32
ROW_CHUNK = 256
V7X_VMEM_BYTES = 64 * 1024 * 1024
VMEM_REQUEST = V7X_VMEM_BYTES * 7 // 8

ADAM_LR = 0.001
ADAM_B1 = 0.9
ADAM_B2 = 0.999
ADAM_EPS = 1e-08
ADAM_WD = 0.01
ADAM_STEP = 10

MESH = pl.DeviceIdType.MESH
ANY = pl.BlockSpec(memory_space=pl.ANY)


def _sds(shape, dtype):
    return jax.ShapeDtypeStruct(tuple(shape), dtype)


class _Comm:
    def __init__(self, ins, out_shapes, aliases, sems, start, finish, done):
        self.ins, self.out_shapes, self.aliases, self.sems = list(ins), list(out_shapes), dict(aliases), list(sems)
        self.start, self.finish, self.done = start, finish, done


def _pcall(body, *, name, out_shape, grid=(), in_specs=None, out_specs=None, scratch_shapes=(), vmem=VMEM_REQUEST,
           aliases=None, prefetch=0, comm=None):
    params = pltpu.CompilerParams(dimension_semantics=("arbitrary",) * len(grid), vmem_limit_bytes=vmem)
    single = not isinstance(out_shape, (tuple, list))
    outs = [out_shape] if single else list(out_shape)
    ospecs = [out_specs] if single else list(out_specs)
    ispecs, scratch, aliases = list(in_specs), list(scratch_shapes), dict(aliases or {})
    n_in, n_out, n_scr = len(ispecs), len(outs), len(scratch)
    kernel_body = body
    if comm is not None:
        n_ci, n_co = len(comm.ins), len(comm.out_shapes)

        def kernel_body(*refs):
            pre, rest = refs[:prefetch], refs[prefetch:]
            core_in, c_in = rest[:n_in], rest[n_in:n_in + n_ci]
            o0 = n_in + n_ci
            core_out, c_out = rest[o0:o0 + n_out], rest[o0 + n_out:o0 + n_out + n_co]
            s0 = o0 + n_out + n_co
            core_scr, c_sem = rest[s0:s0 + n_scr], rest[s0 + n_scr:]
            first = functools.reduce(jnp.logical_and, [pl.program_id(a) == 0 for a in range(len(grid))])
            last = functools.reduce(jnp.logical_and, [pl.program_id(a) == grid[a] - 1 for a in range(len(grid))])
            pl.when(first)(lambda: comm.start(c_in, c_out, c_sem))
            body(*pre, *core_in, *core_out, *core_scr)
            pl.when(last)(lambda: comm.finish(c_in, c_out, c_sem))

        for i, o in comm.aliases.items():
            aliases[prefetch + n_in + i] = n_out + o
        ispecs += [ANY] * n_ci
        ospecs += [ANY] * n_co
        outs += comm.out_shapes
        scratch += comm.sems
    if prefetch:
        spec = pltpu.PrefetchScalarGridSpec(num_scalar_prefetch=prefetch, grid=grid, in_specs=ispecs,
                                            out_specs=tuple(ospecs), scratch_shapes=scratch)
        call = pl.pallas_call(kernel_body, name=name, out_shape=tuple(outs), grid_spec=spec,
                              input_output_aliases=aliases, compiler_params=params)
    else:
        call = pl.pallas_call(kernel_body, name=name, out_shape=tuple(outs), grid=grid, in_specs=ispecs,
                              out_specs=tuple(ospecs), scratch_shapes=scratch, input_output_aliases=aliases,
                              compiler_params=params)

    def run(*args):
        res = call(*args, *(comm.ins if comm is not None else ()))
        if comm is not None:
            comm.done(res[n_out:])
        return res[0] if single else tuple(res[:n_out])

    return run


def _both(a, b):
    def split(refs, na):
        return refs[:na], refs[na:]

    def run(which):
        def go(ins, outs, sems):
            for comm, i, o, s in zip((a, b), split(ins, len(a.ins)), split(outs, len(a.out_shapes)), split(sems, len(a.sems))):
                getattr(comm, which)(i, o, s)
        return go

    def done(res):
        a.done(res[:len(a.out_shapes)])
        b.done(res[len(a.out_shapes):])

    aliases = dict(a.aliases)
    aliases.update({len(a.ins) + i: len(a.out_shapes) + o for i, o in b.aliases.items()})
    return _Comm(a.ins + b.ins, a.out_shapes + b.out_shapes, aliases, a.sems + b.sems, run("start"), run("finish"), done)


def _run_comm(comm, name):
    def body(*refs):
        n_ci, n_co = len(comm.ins), len(comm.out_shapes)
        c_in, c_out, c_sem = refs[:n_ci], refs[n_ci:n_ci + n_co], refs[n_ci + n_co:]
        comm.start(c_in, c_out, c_sem)
        comm.finish(c_in, c_out, c_sem)

    res = pl.pallas_call(body, name=name, out_shape=tuple(comm.out_shapes), in_specs=[ANY] * len(comm.ins),
                         out_specs=tuple([ANY] * len(comm.out_shapes)), scratch_shapes=comm.sems,
                         input_output_aliases=comm.aliases)(*comm.ins)
    comm.done(res)


def _dot(a, b):
    return lax.dot_general(a, b, (((1,), (0,)), ((), ())), preferred_element_type=F32)


def _dot_nt(a, b):
    return lax.dot_general(a, b, (((1,), (1,)), ((), ())), preferred_element_type=F32)


def _dot_tn(a, b):
    return lax.dot_general(a, b, (((0,), (0,)), ((), ())), preferred_element_type=F32)


def _sigmoid(x):
    return 1.0 / (1.0 + jnp.exp(-x))


def _rstd(x):
    return lax.rsqrt(jnp.mean(x * x, axis=-1, keepdims=True) + EPS)


def _rms_fwd(x, g, name):
    s, d = x.shape
    tm = ROW_CHUNK

    def body(x_ref, g_ref, o_ref):
        xv = x_ref[...]
        o_ref[...] = (xv * _rstd(xv) * g_ref[...]).astype(BF16)

    return _pcall(body, name=name, out_shape=_sds((s, d), BF16), grid=(s // tm,),
                  in_specs=[pl.BlockSpec((tm, d), lambda i: (i, 0)), pl.BlockSpec((1, d), lambda i: (0, 0))],
                  out_specs=pl.BlockSpec((tm, d), lambda i: (i, 0)))(x, g)


def _final_loss(x, g, tgt, name):
    s, d = x.shape
    tm = ROW_CHUNK

    def body(x_ref, g_ref, t_ref, loss_ref, dx_ref, dxb_ref, dg_ref):
        i = pl.program_id(0)
        xv = x_ref[...]
        r = _rstd(xv)
        xh = xv * r
        e = xh * g_ref[...] - t_ref[...]
        lpart = 0.5 * jnp.sum(jnp.mean(e * e, axis=-1, keepdims=True), axis=0, keepdims=True)
        dy = e * (1.0 / d)
        gd = dy * g_ref[...]
        dx = r * (gd - xh * jnp.mean(gd * xh, axis=-1, keepdims=True))
        dx_ref[...] = dx
        dxb_ref[...] = dx.astype(BF16)
        part = jnp.sum(dy * xh, axis=0, keepdims=True)
        lrow = jnp.broadcast_to(lpart, (1, LANES))

        @pl.when(i == 0)
        def _():
            dg_ref[...] = part
            loss_ref[...] = lrow

        @pl.when(i > 0)
        def _():
            dg_ref[...] += part
            loss_ref[...] += lrow

    row = pl.BlockSpec((tm, d), lambda i: (i, 0))
    vec = pl.BlockSpec((1, d), lambda i: (0, 0))
    return _pcall(body, name=name,
                  out_shape=(_sds((1, LANES), F32), _sds((s, d), F32), _sds((s, d), BF16), _sds((1, d), F32)),
                  grid=(s // tm,), in_specs=[row, vec, row],
                  out_specs=(pl.BlockSpec((1, LANES), lambda i: (0, 0)), row, row, vec))(x, g, tgt)


def _mm_n(a, b, layer, *, nt, tn, out_dtype, name, resid=None, b_part=0, comm=None):
    s, k = a.shape
    n = b.shape[1] if nt else b.shape[2]
    rows = 512

    def body(a_ref, b_ref, *refs):
        o_ref = refs[-1]
        bv = b_ref[...]
        for r0 in range(0, s, rows):
            av = a_ref[r0:r0 + rows, :]
            prod = _dot_nt(av, bv) if nt else _dot(av, bv)
            if resid is not None:
                prod = refs[0][r0:r0 + rows, :] + prod
            o_ref[r0:r0 + rows, :] = prod.astype(out_dtype)

    b_spec = (pl.BlockSpec((None, tn, k), lambda j: (layer, j, b_part)) if nt
              else pl.BlockSpec((None, k, tn), lambda j: (layer, b_part, j)))
    col = pl.BlockSpec((s, tn), lambda j: (0, j))
    extra = () if resid is None else (resid,)
    return _pcall(body, name=name, out_shape=_sds((s, n), out_dtype), grid=(n // tn,),
                  in_specs=[pl.BlockSpec((s, k), lambda j: (0, 0)), b_spec] + [col] * len(extra),
                  out_specs=col, comm=comm)(a, b, *extra)


def _mm_tn(a, b, *, t, name):
    s, ka = a.shape
    n = b.shape[1]

    def body(a_ref, b_ref, o_ref):
        o_ref[...] = _dot_tn(a_ref[...], b_ref[...]).astype(BF16)

    return _pcall(body, name=name, out_shape=_sds((ka, n), BF16), grid=(ka // t,),
                  in_specs=[pl.BlockSpec((s, t), lambda i: (0, i)), pl.BlockSpec((s, n), lambda i: (0, 0))],
                  out_specs=pl.BlockSpec((t, n), lambda i: (i, 0)))(a, b)


def _mm_tn_pieces(pieces, b, *, t, name):
    s, n = b.shape
    blocks = [p.shape[1] // t for p in pieces]
    starts = [sum(blocks[:i]) for i in range(len(pieces))]

    def body(*refs):
        p_refs, b_ref, o_ref = refs[:len(pieces)], refs[len(pieces)], refs[len(pieces) + 1]
        j = pl.program_id(0)
        for p_ref, start, count in zip(p_refs, starts, blocks):
            @pl.when((j >= start) & (j < start + count))
            def _(p_ref=p_ref):
                o_ref[...] = _dot_tn(p_ref[...], b_ref[...]).astype(BF16)

    specs = [pl.BlockSpec((s, t), lambda j, start=start, count=count: (0, jnp.clip(j - start, 0, count - 1)))
             for start, count in zip(starts, blocks)]
    return _pcall(body, name=name, out_shape=_sds((sum(blocks) * t, n), BF16), grid=(sum(blocks),),
                  in_specs=specs + [pl.BlockSpec((s, n), lambda j: (0, 0))],
                  out_specs=pl.BlockSpec((t, n), lambda j: (j, 0)))(*pieces, b)


def _mm_tn2(a, b_lo, b_hi, *, t, name, comm=None):
    s, ka = a.shape
    half = b_lo.shape[1]
    nb = half // t

    def body(a_ref, lo_ref, hi_ref, o_ref):
        j = pl.program_id(0)

        @pl.when(j < nb)
        def _():
            o_ref[...] = _dot_tn(a_ref[...], lo_ref[...]).astype(BF16)

        @pl.when(j >= nb)
        def _():
            o_ref[...] = _dot_tn(a_ref[...], hi_ref[...]).astype(BF16)

    return _pcall(body, name=name, out_shape=_sds((ka, 2 * half), BF16), grid=(2 * nb,),
                  in_specs=[pl.BlockSpec((s, ka), lambda j: (0, 0)),
                            pl.BlockSpec((s, t), lambda j: (0, jnp.minimum(j, nb - 1))),
                            pl.BlockSpec((s, t), lambda j: (0, jnp.maximum(j - nb, 0)))],
                  out_specs=pl.BlockSpec((ka, t), lambda j: (0, j)), comm=comm)(a, b_lo, b_hi)


SUBLANES = 8


def _tap_windows(win, width, lead, rows):
    offs = [lead + k for k in range(width)]
    if width <= SUBLANES:
        return [win[o:o + rows, :] for o in offs]
    n = win.shape[0]
    out = {}
    for r in sorted({o % SUBLANES for o in offs}):
        base = win if r == 0 else pltpu.roll(win, n - r, axis=0)
        for o in offs:
            if o % SUBLANES == r:
                out[o - lead] = base[o - r:o - r + rows, :]
    return [out[k] for k in range(width)]


def _conv_taps(taps, w_ref):
    acc = None
    for k, tap in enumerate(taps):
        term = w_ref[pl.ds(k, 1), :] * tap
        acc = term if acc is None else acc + term
    return acc


def _causal_taps(win, width, pad, rows):
    return _tap_windows(win, width, pad - (width - 1), rows)


def _anticausal_taps(win, width, rows):
    return _tap_windows(win, width, 0, rows)[::-1]


def _conv_wgrad(dw_ref, g, taps):
    for k, tap in enumerate(taps):
        dw_ref[pl.ds(k, 1), :] += jnp.sum(g * tap, axis=0, keepdims=True)


def _mixer_a_fwd(ab, taps_t, wa_ref):
    ct = _conv_taps(taps_t, wa_ref)
    return ab * ct, ct


def _mixer_c_fwd(taps_u, wc_ref, cb_ref, lg_ref, lb_ref):
    u = _conv_taps(taps_u, wc_ref) + cb_ref[...]
    mu = jnp.mean(u, axis=-1, keepdims=True)
    uc = u - mu
    rs = lax.rsqrt(jnp.mean(uc * uc, axis=-1, keepdims=True) + EPS)
    uh = uc * rs
    ln = uh * lg_ref[...] + lb_ref[...]
    sg = _sigmoid(ln)
    return ln * sg, ln, sg, uh, rs


def _mix_fwd(z, wa, wc, cb, lg, lb, ga, gc, name):
    s = z.shape[0]
    w = wa.shape[1]
    nblk = z.shape[1] // w
    rc = ROW_CHUNK

    def body(ah_ref, ab_ref, ac_ref, cv_ref, cg_ref, wa_ref, wc_ref, cb_ref, lg_ref, lb_ref, ga_ref, gc_ref,
             ya_ref, yc_ref, tpad, upad):
        tpad[pl.ds(0, PAD_SHORT), :] = jnp.zeros((PAD_SHORT, w), F32)
        upad[pl.ds(0, PAD_LONG), :] = jnp.zeros((PAD_LONG, w), F32)

        def chunk(i, carry):
            base = pl.multiple_of(i * rc, rc)
            rows = pl.ds(base, rc)
            ah, ab, ac = ah_ref[rows, :], ab_ref[rows, :], ac_ref[rows, :]
            tpad[pl.ds(base + PAD_SHORT, rc), :] = ac * ah
            ya, _ = _mixer_a_fwd(ab, _causal_taps(tpad[pl.ds(base, rc + PAD_SHORT), :], SHORT_CONV, PAD_SHORT, rc), wa_ref)
            ya_ref[rows, :] = (ya * _rstd(ya) * ga_ref[...]).astype(BF16)
            upad[pl.ds(base + PAD_LONG, rc), :] = cv_ref[rows, :] * _sigmoid(cg_ref[rows, :])
            taps_u = _causal_taps(upad[pl.ds(base, rc + PAD_LONG), :], CONFORMER_CONV, PAD_LONG, rc)
            yc = _mixer_c_fwd(taps_u, wc_ref, cb_ref, lg_ref, lb_ref)[0]
            yc_ref[rows, :] = (yc * _rstd(yc) * gc_ref[...]).astype(BF16)
            return carry

        lax.fori_loop(0, s // rc, chunk, 0)

    def zblk(j):
        return pl.BlockSpec((s, w), lambda i: (0, j))

    def whole(a):
        return pl.BlockSpec(a.shape, lambda i: (0, 0))

    return _pcall(
        body, name=name, out_shape=(_sds((s, w), BF16), _sds((s, w), BF16)), grid=(1,),
        in_specs=[zblk(0), zblk(1), zblk(2), zblk(nblk - 2), zblk(nblk - 1)] + [whole(a) for a in (wa, wc, cb, lg, lb, ga, gc)],
        out_specs=(pl.BlockSpec((s, w), lambda i: (0, 0)), pl.BlockSpec((s, w), lambda i: (0, 0))),
        scratch_shapes=[pltpu.VMEM((s + PAD_SHORT, w), F32), pltpu.VMEM((s + PAD_LONG, w), F32)],
    )(z, z, z, z, z, wa, wc, cb, lg, lb, ga, gc)


def _mix_bwd(z, dy, wa, wc, cb, lg, lb, ga, gc, name, comm=None):
    s = z.shape[0]
    w = wa.shape[1]
    nblk = z.shape[1] // w
    nyb = dy.shape[1] // w
    rc = ROW_CHUNK

    def body(ah_ref, ab_ref, ac_ref, cv_ref, cg_ref, dya_ref, dyc_ref,
             wa_ref, wc_ref, cb_ref, lg_ref, lb_ref, ga_ref, gc_ref,
             dza_ref, dzc_ref, dwa_ref, dwc_ref, dcb_ref, dlg_ref, dlb_ref, dga_ref, dgc_ref,
             tpad, upad, dctp, dup):
        tpad[pl.ds(0, PAD_SHORT), :] = jnp.zeros((PAD_SHORT, w), F32)
        upad[pl.ds(0, PAD_LONG), :] = jnp.zeros((PAD_LONG, w), F32)
        dctp[pl.ds(s, PAD_SHORT), :] = jnp.zeros((PAD_SHORT, w), F32)
        dup[pl.ds(s, PAD_LONG), :] = jnp.zeros((PAD_LONG, w), F32)
        for ref in (dwa_ref, dwc_ref, dcb_ref, dlg_ref, dlb_ref, dga_ref, dgc_ref):
            ref[...] = jnp.zeros(ref.shape, F32)

        def rms_bwd(y, g_ref, dyn, dg_ref):
            r = _rstd(y)
            yh = y * r
            gd = dyn * g_ref[...]
            dg_ref[...] += jnp.sum(dyn * yh, axis=0, keepdims=True)
            return r * (gd - yh * jnp.mean(gd * yh, axis=-1, keepdims=True))

        def first(i, carry):
            base = pl.multiple_of(i * rc, rc)
            rows = pl.ds(base, rc)
            ah, ab, ac = ah_ref[rows, :], ab_ref[rows, :], ac_ref[rows, :]
            tpad[pl.ds(base + PAD_SHORT, rc), :] = ac * ah
            taps_t = _causal_taps(tpad[pl.ds(base, rc + PAD_SHORT), :], SHORT_CONV, PAD_SHORT, rc)
            ya, ct = _mixer_a_fwd(ab, taps_t, wa_ref)
            dya = rms_bwd(ya, ga_ref, dya_ref[rows, :], dga_ref)
            dza_ref[rows, w:2 * w] = (dya * ct).astype(BF16)
            dct = dya * ab
            dctp[rows, :] = dct
            _conv_wgrad(dwa_ref, dct, taps_t)

            upad[pl.ds(base + PAD_LONG, rc), :] = cv_ref[rows, :] * _sigmoid(cg_ref[rows, :])
            taps_u = _causal_taps(upad[pl.ds(base, rc + PAD_LONG), :], CONFORMER_CONV, PAD_LONG, rc)
            yc, ln, sg, uh, rs = _mixer_c_fwd(taps_u, wc_ref, cb_ref, lg_ref, lb_ref)
            dyc = rms_bwd(yc, gc_ref, dyc_ref[rows, :], dgc_ref)
            dln = dyc * (sg * (1.0 + ln * (1.0 - sg)))
            dlg_ref[...] += jnp.sum(dln * uh, axis=0, keepdims=True)
            dlb_ref[...] += jnp.sum(dln, axis=0, keepdims=True)
            duh = dln * lg_ref[...]
            du = rs * (duh - jnp.mean(duh, axis=-1, keepdims=True) - uh * jnp.mean(duh * uh, axis=-1, keepdims=True))
            dcb_ref[...] += jnp.sum(du, axis=0, keepdims=True)
            dup[rows, :] = du
            _conv_wgrad(dwc_ref, du, taps_u)
            return carry

        lax.fori_loop(0, s // rc, first, 0)

        def second(i, carry):
            base = pl.multiple_of(i * rc, rc)
            rows = pl.ds(base, rc)
            dt = _conv_taps(_anticausal_taps(dctp[pl.ds(base, rc + PAD_SHORT), :], SHORT_CONV, rc), wa_ref)
            dza_ref[rows, 0:w] = (dt * ac_ref[rows, :]).astype(BF16)
            dza_ref[rows, 2 * w:3 * w] = (dt * ah_ref[rows, :]).astype(BF16)
            du0 = _conv_taps(_anticausal_taps(dup[pl.ds(base, rc + PAD_LONG), :], CONFORMER_CONV, rc), wc_ref)
            sg = _sigmoid(cg_ref[rows, :])
            dzc_ref[rows, 0:w] = (du0 * sg).astype(BF16)
            dzc_ref[rows, w:2 * w] = (du0 * cv_ref[rows, :] * sg * (1.0 - sg)).astype(BF16)
            return carry

        lax.fori_loop(0, s // rc, second, 0)

    def blk(j):
        return pl.BlockSpec((s, w), lambda i: (0, j))

    def whole(a):
        return pl.BlockSpec(tuple(a.shape), lambda i: (0, 0))

    params = (wa, wc, cb, lg, lb, ga, gc)
    outs = (_sds((s, 3 * w), BF16), _sds((s, 2 * w), BF16)) + tuple(_sds(p.shape, F32) for p in params)
    return _pcall(
        body, name=name, out_shape=outs, grid=(1,),
        in_specs=[blk(0), blk(1), blk(2), blk(nblk - 2), blk(nblk - 1), blk(0), blk(nyb - 1)] + [whole(p) for p in params],
        out_specs=tuple(whole(o) for o in outs),
        scratch_shapes=[pltpu.VMEM((s + PAD_SHORT, w), F32), pltpu.VMEM((s + PAD_LONG, w), F32),
                        pltpu.VMEM((s + PAD_SHORT, w), F32), pltpu.VMEM((s + PAD_LONG, w), F32)], comm=comm,
    )(z, z, z, z, z, dy, dy, *params)


def _ffn_act_fwd(up, wf, name, comm=None):
    s, f2 = up.shape
    f = f2 // 2
    tc = 256
    nb = f // tc
    rc = ROW_CHUNK

    def body(g_ref, v_ref, wg_ref, wv_ref, o_ref, gpad, vpad):
        gpad[pl.ds(0, PAD_SHORT), :] = jnp.zeros((PAD_SHORT, tc), F32)
        vpad[pl.ds(0, PAD_SHORT), :] = jnp.zeros((PAD_SHORT, tc), F32)

        def chunk(i, carry):
            base = pl.multiple_of(i * rc, rc)
            rows = pl.ds(base, rc)
            gpad[pl.ds(base + PAD_SHORT, rc), :] = g_ref[rows, :].astype(F32)
            vpad[pl.ds(base + PAD_SHORT, rc), :] = v_ref[rows, :].astype(F32)
            gc = _conv_taps(_causal_taps(gpad[pl.ds(base, rc + PAD_SHORT), :], FFN_CONV, PAD_SHORT, rc), wg_ref)
            vc = _conv_taps(_causal_taps(vpad[pl.ds(base, rc + PAD_SHORT), :], FFN_CONV, PAD_SHORT, rc), wv_ref)
            o_ref[rows, :] = (gc * _sigmoid(gc) * vc).astype(BF16)
            return carry

        lax.fori_loop(0, s // rc, chunk, 0)

    return _pcall(
        body, name=name, out_shape=_sds((s, f), BF16), grid=(nb,),
        in_specs=[pl.BlockSpec((s, tc), lambda j: (0, j)), pl.BlockSpec((s, tc), lambda j: (0, j + nb)),
                  pl.BlockSpec((FFN_CONV, tc), lambda j: (0, j)), pl.BlockSpec((FFN_CONV, tc), lambda j: (0, j + nb))],
        out_specs=pl.BlockSpec((s, tc), lambda j: (0, j)),
        scratch_shapes=[pltpu.VMEM((s + PAD_SHORT, tc), F32), pltpu.VMEM((s + PAD_SHORT, tc), F32)], comm=comm,
    )(up, up, wf, wf)


def _ffn_act_bwd(up, dact, wf, name, comm=None):
    s, f2 = up.shape
    f = f2 // 2
    tc = 256
    nb = f // tc
    rc = ROW_CHUNK

    def body(g_ref, v_ref, da_ref, wg_ref, wv_ref, act_ref, dg_ref, dv_ref, dwg_ref, dwv_ref, gpad, vpad, dgp, dvp):
        gpad[pl.ds(0, PAD_SHORT), :] = jnp.zeros((PAD_SHORT, tc), F32)
        vpad[pl.ds(0, PAD_SHORT), :] = jnp.zeros((PAD_SHORT, tc), F32)
        dgp[pl.ds(s, PAD_SHORT), :] = jnp.zeros((PAD_SHORT, tc), F32)
        dvp[pl.ds(s, PAD_SHORT), :] = jnp.zeros((PAD_SHORT, tc), F32)
        dwg_ref[...] = jnp.zeros((FFN_CONV, tc), F32)
        dwv_ref[...] = jnp.zeros((FFN_CONV, tc), F32)

        def first(i, carry):
            base = pl.multiple_of(i * rc, rc)
            rows = pl.ds(base, rc)
            gpad[pl.ds(base + PAD_SHORT, rc), :] = g_ref[rows, :].astype(F32)
            vpad[pl.ds(base + PAD_SHORT, rc), :] = v_ref[rows, :].astype(F32)
            taps_g = _causal_taps(gpad[pl.ds(base, rc + PAD_SHORT), :], FFN_CONV, PAD_SHORT, rc)
            taps_v = _causal_taps(vpad[pl.ds(base, rc + PAD_SHORT), :], FFN_CONV, PAD_SHORT, rc)
            gc = _conv_taps(taps_g, wg_ref)
            vc = _conv_taps(taps_v, wv_ref)
            sg = _sigmoid(gc)
            silu = gc * sg
            act_ref[rows, :] = (silu * vc).astype(BF16)
            da = da_ref[rows, :].astype(F32)
            dgc = da * vc * (sg * (1.0 + gc * (1.0 - sg)))
            dvc = da * silu
            dgp[rows, :] = dgc
            dvp[rows, :] = dvc
            _conv_wgrad(dwg_ref, dgc, taps_g)
            _conv_wgrad(dwv_ref, dvc, taps_v)
            return carry

        lax.fori_loop(0, s // rc, first, 0)

        def second(i, carry):
            base = pl.multiple_of(i * rc, rc)
            rows = pl.ds(base, rc)
            dg_ref[rows, :] = _conv_taps(_anticausal_taps(dgp[pl.ds(base, rc + PAD_SHORT), :], FFN_CONV, rc), wg_ref).astype(BF16)
            dv_ref[rows, :] = _conv_taps(_anticausal_taps(dvp[pl.ds(base, rc + PAD_SHORT), :], FFN_CONV, rc), wv_ref).astype(BF16)
            return carry

        lax.fori_loop(0, s // rc, second, 0)

    lo = pl.BlockSpec((s, tc), lambda j: (0, j))
    hi = pl.BlockSpec((s, tc), lambda j: (0, j + nb))
    wlo = pl.BlockSpec((FFN_CONV, tc), lambda j: (0, j))
    whi = pl.BlockSpec((FFN_CONV, tc), lambda j: (0, j + nb))
    act, dgate, dval, dwg, dwv = _pcall(
        body, name=name,
        out_shape=(_sds((s, f), BF16), _sds((s, f), BF16), _sds((s, f), BF16), _sds((FFN_CONV, f), F32), _sds((FFN_CONV, f), F32)),
        grid=(nb,), in_specs=[lo, hi, lo, wlo, whi], out_specs=(lo, lo, lo, wlo, wlo),
        scratch_shapes=[pltpu.VMEM((s + PAD_SHORT, tc), F32) for _ in range(4)], comm=comm,
    )(up, up, dact, wf, wf)
    return act, dgate, dval, jnp.concatenate([dwg, dwv], axis=1)


def _out_proj(yan, yb, ycn, gb, x, w_out, layer, g_next, name, comm=None):
    s, w = yan.shape
    wb = yb.shape[1]
    d = x.shape[1]
    tm = ROW_CHUNK

    def body(ya_ref, yb_ref, yc_ref, gb_ref, x_ref, w_ref, g_ref, y_ref, xm_ref, h_ref):
        ybv = yb_ref[...]
        y = jnp.concatenate([ya_ref[...], (ybv * _rstd(ybv) * gb_ref[...]).astype(BF16), yc_ref[...]], axis=1)
        y_ref[...] = y
        xm = x_ref[...] + _dot(y, w_ref[...])
        xm_ref[...] = xm
        h_ref[...] = (xm * _rstd(xm) * g_ref[...]).astype(BF16)

    def rows(width):
        return pl.BlockSpec((tm, width), lambda i: (i, 0))

    def vec(width):
        return pl.BlockSpec((1, width), lambda i: (0, 0))

    return _pcall(body, name=name, out_shape=(_sds((s, d), BF16), _sds((s, d), F32), _sds((s, d), BF16)), grid=(s // tm,),
                  in_specs=[rows(w), rows(wb), rows(w), vec(wb), rows(d), pl.BlockSpec((None, d, d), lambda i: (layer, 0, 0)), vec(d)],
                  out_specs=(rows(d), rows(d), rows(d)), comm=comm)(yan, yb, ycn, gb, x, w_out, g_next)


def _down_proj(act, w_down, layer, x_mid, g_next, name, comm=None):
    s, f = act.shape
    d = x_mid.shape[1]
    tm = ROW_CHUNK

    def body(a_ref, w_ref, x_ref, *refs):
        xo = x_ref[...] + _dot(a_ref[...], w_ref[...])
        refs[-2 if g_next is not None else -1][...] = xo
        if g_next is not None:
            refs[-1][...] = (xo * _rstd(xo) * refs[0][...]).astype(BF16)

    row = pl.BlockSpec((tm, d), lambda i: (i, 0))
    ins = [act, w_down, x_mid] + ([g_next] if g_next is not None else [])
    in_specs = [pl.BlockSpec((tm, f), lambda i: (i, 0)), pl.BlockSpec((None, f, d), lambda i: (layer, 0, 0)), row]
    in_specs += [pl.BlockSpec((1, d), lambda i: (0, 0))] if g_next is not None else []
    outs = (_sds((s, d), F32), _sds((s, d), BF16)) if g_next is not None else (_sds((s, d), F32),)
    res = _pcall(body, name=name, out_shape=outs, grid=(s // tm,), in_specs=in_specs, out_specs=tuple([row] * len(outs)),
                 comm=comm)(*ins)
    return (res[0], res[1]) if g_next is not None else (res[0], None)


def _proj_dx(pieces, w, layer, nt, x, g, dres, name, comm=None):
    s, d = x.shape
    tm = ROW_CHUNK
    widths = [p.shape[1] for p in pieces]

    def body(*refs):
        p_refs, (w_ref, x_ref, g_ref, dres_ref, dx_ref, dxb_ref, dg_ref) = refs[:len(pieces)], refs[len(pieces):]
        i = pl.program_id(0)
        dh, off = None, 0
        for p_ref, width in zip(p_refs, widths):
            part = _dot_nt(p_ref[...], w_ref[:, off:off + width]) if nt else _dot(p_ref[...], w_ref[off:off + width, :])
            dh = part if dh is None else dh + part
            off += width
        xv = x_ref[...]
        r = _rstd(xv)
        xh = xv * r
        gd = dh * g_ref[...]
        dx = dres_ref[...] + r * (gd - xh * jnp.mean(gd * xh, axis=-1, keepdims=True))
        dx_ref[...] = dx
        dxb_ref[...] = dx.astype(BF16)
        part = jnp.sum(dh * xh, axis=0, keepdims=True)

        @pl.when(i == 0)
        def _():
            dg_ref[...] = part

        @pl.when(i > 0)
        def _():
            dg_ref[...] += part

    row = pl.BlockSpec((tm, d), lambda i: (i, 0))
    vec = pl.BlockSpec((1, d), lambda i: (0, 0))
    w_spec = pl.BlockSpec((None,) + w.shape[1:], lambda i: (layer, 0, 0))
    return _pcall(body, name=name, out_shape=(_sds((s, d), F32), _sds((s, d), BF16), _sds((1, d), F32)), grid=(s // tm,),
                  in_specs=[pl.BlockSpec((tm, width), lambda i: (i, 0)) for width in widths] + [w_spec, row, vec, row],
                  out_specs=(row, row, vec), comm=comm)(*pieces, w, x, g, dres)


def _yb_norm_bwd(yb, dy, gb, name, comm=None):
    s, wb = yb.shape
    w = wb // 2
    heads = wb // D_HEAD
    tm = ROW_CHUNK

    def body(yb_ref, d1_ref, d2_ref, g_ref, dyb_ref, dl_ref, dg_ref):
        i = pl.program_id(0)
        y = yb_ref[...]
        dyn = jnp.concatenate([d1_ref[...], d2_ref[...]], axis=1)
        r = _rstd(y)
        yh = y * r
        gd = dyn * g_ref[...]
        dyb = r * (gd - yh * jnp.mean(gd * yh, axis=-1, keepdims=True))
        dyb_ref[...] = dyb
        part = jnp.sum(dyn * yh, axis=0, keepdims=True)
        prod = dyb * y
        even = lax.broadcasted_iota(I32, (tm, LANES), 1) < D_HEAD
        for p in range(heads // 2):
            blk = prod[:, p * LANES:(p + 1) * LANES]
            ev = jnp.sum(jnp.where(even, blk, 0.0), axis=1, keepdims=True)
            od = jnp.sum(jnp.where(even, 0.0, blk), axis=1, keepdims=True)
            dl_ref[2 * p] = jnp.broadcast_to(ev, (tm, LANES))
            dl_ref[2 * p + 1] = jnp.broadcast_to(od, (tm, LANES))

        @pl.when(i == 0)
        def _():
            dg_ref[...] = part

        @pl.when(i > 0)
        def _():
            dg_ref[...] += part

    return _pcall(
        body, name=name, out_shape=(_sds((s, wb), F32), _sds((heads, s, LANES), F32), _sds((1, wb), F32)),
        grid=(s // tm,),
        in_specs=[pl.BlockSpec((tm, wb), lambda i: (i, 0)), pl.BlockSpec((tm, w), lambda i: (i, 1)),
                  pl.BlockSpec((tm, w), lambda i: (i, 2)), pl.BlockSpec((1, wb), lambda i: (0, 0))],
        out_specs=(pl.BlockSpec((tm, wb), lambda i: (i, 0)), pl.BlockSpec((heads, tm, LANES), lambda i: (0, i, 0)),
                   pl.BlockSpec((1, wb), lambda i: (0, 0))), comm=comm,
    )(yb, dy, dy, gb)


def _t5_bucket_table():
    max_exact = NUM_BUCKETS // 2
    out = np.full((len(DILATED_BRANCHES), BLK, 2 * BLK), -1, np.int32)
    rel = np.arange(BLK)[:, None] - np.arange(2 * BLK)[None, :] + BLK
    for b, (window, dilation) in enumerate(DILATED_BRANCHES):
        n_keys = window // dilation
        dist = np.maximum(rel, 0) * dilation
        d_f = np.maximum(dist, 1).astype(np.float32)
        large = max_exact + (np.log(d_f / np.float32(max_exact)) / np.float32(math.log(MAX_DISTANCE / max_exact))
                             * np.float32(NUM_BUCKETS - max_exact)).astype(np.int32)
        large = np.minimum(large, NUM_BUCKETS - 1)
        bucket = np.where(dist < max_exact, dist, large)
        out[b] = np.where((rel >= 0) & (rel <= n_keys), bucket, -1)
    return out


def _bias_tiles(rel_bias, buckets, name):
    nbk, heads = rel_bias.shape
    nbr = buckets.shape[0]

    def body(rb_ref, bk_ref, o_ref):
        for br in range(nbr):
            bk = bk_ref[br]
            tiles = [jnp.full((BLK, 2 * BLK), NEG, F32) for _ in range(heads)]
            for b in range(nbk):
                hit = bk == b
                tiles = [jnp.where(hit, rb_ref[b, h], tiles[h]) for h in range(heads)]
            for h in range(heads):
                o_ref[br, h] = tiles[h]

    return _pcall(body, name=name, out_shape=_sds((nbr, heads, BLK, 2 * BLK), F32), grid=(1,),
                  in_specs=[pl.BlockSpec(memory_space=pltpu.SMEM), pl.BlockSpec(buckets.shape, lambda i: (0, 0, 0))],
                  out_specs=pl.BlockSpec((nbr, heads, BLK, 2 * BLK), lambda i: (0, 0, 0, 0)))(rel_bias, buckets)


def _bias_grad(dtiles, buckets, nbk, name):
    nbr, heads = dtiles.shape[:2]

    def body(dt_ref, bk_ref, o_ref):
        row = lax.broadcasted_iota(I32, (nbk, LANES), 0)
        col = lax.broadcasted_iota(I32, (nbk, LANES), 1)
        out = jnp.zeros((nbk, LANES), F32)
        for h in range(heads):
            for b in range(nbk):
                tot = jnp.zeros((), F32)
                for br in range(nbr):
                    tot = tot + jnp.sum(jnp.where(bk_ref[br] == b, dt_ref[br, h], 0.0))
                out = jnp.where((row == b) & (col == h), tot, out)
        o_ref[...] = out

    return _pcall(body, name=name, out_shape=_sds((nbk, LANES), F32), grid=(1,),
                  in_specs=[pl.BlockSpec(dtiles.shape, lambda i: (0, 0, 0, 0)), pl.BlockSpec(buckets.shape, lambda i: (0, 0, 0))],
                  out_specs=pl.BlockSpec((nbk, LANES), lambda i: (0, 0)))(dtiles, buckets)


def _largest_divisor(n, cap):
    return max(g for g in range(1, cap + 1) if n % g == 0)


def _attn_blocks(s, visit, group):
    for br, (window, d) in enumerate(DILATED_BRANCHES):
        n_blk = (s // d) // BLK
        span = BLK * d
        g1 = _largest_divisor(d, group)

        def firsts(t, carry, br=br, d=d, g1=g1):
            for j in range(g1):
                visit(br, d, t * g1 + j, False)
            return carry

        lax.fori_loop(0, d // g1, firsts, 0)
        if n_blk > 1:
            total = d * (n_blk - 1)
            g2 = _largest_divisor(total, group)

            def rest(t, carry, br=br, d=d, n_blk=n_blk, span=span, g2=g2):
                for j in range(g2):
                    idx = t * g2 + j
                    visit(br, d, idx // (n_blk - 1) + (1 + idx % (n_blk - 1)) * span, True)
                return carry

            lax.fori_loop(0, total // g2, rest, 0)


def _rows(start, size, d):
    return pl.ds(pl.multiple_of(start, BLK), size) if d == 1 else pl.ds(start, size, stride=d)


def _attn_fwd(z, btiles, col0, name, comm=None):
    s = z.shape[0]
    nbr, heads = btiles.shape[:2]
    pairs = heads // 2
    scale = D_HEAD ** -0.5
    rc = ROW_CHUNK

    def body(q_ref, k_ref, v_ref, bt_ref, yb_ref, lse_ref, acc_ref, m_ref, l_ref):
        even = lax.broadcasted_iota(I32, (BLK, LANES), 1) < D_HEAD
        even2 = lax.broadcasted_iota(I32, (2 * BLK, LANES), 1) < D_HEAD

        def visit(br, d, start, prev):
            kw = 2 * BLK if prev else BLK
            rows_q = _rows(start, BLK, d)
            rows_k = _rows(start - BLK * d, kw, d) if prev else rows_q
            qb = q_ref[rows_q, :]
            kb = k_ref[rows_k, :].astype(BF16)
            vw = v_ref[rows_k, :]
            ev_k = even2 if prev else even
            qm = jnp.concatenate([jnp.where(even, qb, 0.0), jnp.where(even, 0.0, qb)], axis=0).astype(BF16)
            bias = [bt_ref[br, e] if prev else bt_ref[br, e, :, BLK:] for e in range(2)]
            sc = _dot_nt(qm, kb) * scale + jnp.concatenate(bias, axis=0)
            m = jnp.max(sc, axis=1, keepdims=True)
            p = jnp.exp(sc - m)
            l = jnp.sum(p, axis=1, keepdims=True)
            pb = p.astype(BF16)
            vm = jnp.concatenate([jnp.where(ev_k, vw, 0.0), jnp.where(ev_k, 0.0, vw)], axis=0).astype(BF16)
            acc_ref.at[br][rows_q, :] = _dot(jnp.concatenate([pb[:BLK], pb[BLK:]], axis=1), vm)
            for e in range(2):
                m_ref.at[br, e][rows_q, :] = jnp.broadcast_to(m[e * BLK:(e + 1) * BLK], (BLK, LANES))
                l_ref.at[br, e][rows_q, :] = jnp.broadcast_to(l[e * BLK:(e + 1) * BLK], (BLK, LANES))

        _attn_blocks(s, visit, ATTN_GROUP_FWD)

        ev_c = lax.broadcasted_iota(I32, (rc, LANES), 1) < D_HEAD

        def merge(i, carry):
            rows = pl.ds(pl.multiple_of(i * rc, rc), rc)
            wts, dens = [], []
            for e in range(2):
                ms = [m_ref[br, e, rows, :] for br in range(nbr)]
                top = functools.reduce(jnp.maximum, ms)
                w = [jnp.exp(mb - top) for mb in ms]
                den = functools.reduce(lambda a, b: a + b, [w[br] * l_ref[br, e, rows, :] for br in range(nbr)])
                lse_ref[e, rows, :] = top + jnp.log(den)
                wts.append(w)
                dens.append(den)
            num = functools.reduce(lambda a, b: a + b,
                                   [jnp.where(ev_c, wts[0][br], wts[1][br]) * acc_ref[br, rows, :] for br in range(nbr)])
            yb_ref[rows, :] = num / jnp.where(ev_c, dens[0], dens[1])
            return carry

        lax.fori_loop(0, s // rc, merge, 0)

    def zcol(j):
        return pl.BlockSpec((s, LANES), lambda p, j=j: (0, col0 + j + p))

    return _pcall(
        body, name=name, out_shape=(_sds((s, pairs * LANES), F32), _sds((heads, s, LANES), F32)), grid=(pairs,),
        in_specs=[zcol(0), zcol(pairs), zcol(2 * pairs), pl.BlockSpec((nbr, 2, BLK, 2 * BLK), lambda p: (0, p, 0, 0))],
        out_specs=(pl.BlockSpec((s, LANES), lambda p: (0, p)), pl.BlockSpec((2, s, LANES), lambda p: (p, 0, 0))),
        scratch_shapes=[pltpu.VMEM((nbr, s, LANES), F32), pltpu.VMEM((nbr, 2, s, LANES), F32), pltpu.VMEM((nbr, 2, s, LANES), F32)],
        comm=comm,
    )(z, z, z, btiles)


def _attn_bwd(z, btiles, dyb, lse, delta, dbias_in, col0, name, comm=None):
    s = z.shape[0]
    nbr, heads = btiles.shape[:2]
    pairs = heads // 2
    scale = D_HEAD ** -0.5

    def body(q_ref, k_ref, v_ref, bt_ref, dy_ref, lse_ref, dl_ref, dbi_ref,
             dq_ref, dk_ref, dv_ref, db_ref, dqa, dka, dva):
        even = lax.broadcasted_iota(I32, (BLK, LANES), 1) < D_HEAD
        even2 = lax.broadcasted_iota(I32, (2 * BLK, LANES), 1) < D_HEAD
        for ref in (dqa, dka, dva):
            ref[...] = jnp.zeros((s, LANES), F32)
        db_ref[...] = dbi_ref[...]

        def visit(br, d, start, prev):
            kw = 2 * BLK if prev else BLK
            rows_q = _rows(start, BLK, d)
            rows_k = _rows(start - BLK * d, kw, d) if prev else rows_q
            qb = q_ref[rows_q, :]
            dyv = dy_ref[rows_q, :]
            kwin = k_ref[rows_k, :]
            kb = kwin.astype(BF16)
            vb = v_ref[rows_k, :].astype(BF16)
            ev_k = even2 if prev else even
            qm = jnp.concatenate([jnp.where(even, qb, 0.0), jnp.where(even, 0.0, qb)], axis=0).astype(BF16)
            dym = jnp.concatenate([jnp.where(even, dyv, 0.0), jnp.where(even, 0.0, dyv)], axis=0).astype(BF16)
            bias = [bt_ref[br, e] if prev else bt_ref[br, e, :, BLK:] for e in range(2)]
            sc = _dot_nt(qm, kb) * scale + jnp.concatenate(bias, axis=0)
            lt = jnp.concatenate([lse_ref.at[e][rows_q, :] for e in range(2)], axis=0)
            dt = jnp.concatenate([dl_ref.at[e][rows_q, :] for e in range(2)], axis=0)
            if prev:
                lt = jnp.concatenate([lt, lt], axis=1)
                dt = jnp.concatenate([dt, dt], axis=1)
            p = jnp.exp(sc - lt)
            ds = p * (_dot_nt(dym, vb) - dt)
            for e in range(2):
                if prev:
                    db_ref[br, e] += ds[e * BLK:(e + 1) * BLK]
                else:
                    db_ref[br, e, :, BLK:] += ds[e * BLK:(e + 1) * BLK]
            dsb = ds.astype(BF16)
            km = jnp.concatenate([jnp.where(ev_k, kwin, 0.0), jnp.where(ev_k, 0.0, kwin)], axis=0).astype(BF16)
            dqa[rows_q, :] += _dot(jnp.concatenate([dsb[:BLK], dsb[BLK:]], axis=1), km) * scale
            dka[rows_k, :] += _dot_tn(dsb, qm) * scale
            dva[rows_k, :] += _dot_tn(p.astype(BF16), dym)

        _attn_blocks(s, visit, ATTN_GROUP_BWD)
        dq_ref[...] = dqa[...].astype(BF16)
        dk_ref[...] = dka[...].astype(BF16)
        dv_ref[...] = dva[...].astype(BF16)

    def zcol(j):
        return pl.BlockSpec((s, LANES), lambda p, j=j: (0, col0 + j + p))

    col = pl.BlockSpec((s, LANES), lambda p: (0, p))
    stat = pl.BlockSpec((2, s, LANES), lambda p: (p, 0, 0))
    tile = pl.BlockSpec((nbr, 2, BLK, 2 * BLK), lambda p: (0, p, 0, 0))
    wide = _sds((s, pairs * LANES), BF16)
    return _pcall(
        body, name=name, out_shape=(wide, wide, wide, _sds(btiles.shape, F32)), grid=(pairs,),
        in_specs=[zcol(0), zcol(pairs), zcol(2 * pairs), tile, col, stat, stat, tile],
        out_specs=(col, col, col, tile),
        scratch_shapes=[pltpu.VMEM((s, LANES), F32) for _ in range(3)], comm=comm,
    )(z, z, z, btiles, dyb, lse, delta, dbias_in)


def _row(v):
    return v.reshape(1, -1)


class _Rides:
    def __init__(self):
        self.table, self.grads = {}, {}

    def add(self, name, build):
        self.table.setdefault(name, []).append(build)

    def get(self, name):
        comm = None
        for build in self.table.get(name, ()):
            comm = build() if comm is None else _both(comm, build())
        return comm

    def ready(self, key, g):
        self.grads[key] = g


class _LocalSchedule:
    def __init__(self):
        self.big = {}

    def fwd_comms(self, l):
        return _Rides()

    def bwd_comms(self, l):
        return _Rides()

    def after_bwd(self, l, grads):
        self.big[l] = grads


def _layer_fwd(l, x, h, wts, prm, btiles, comms):
    d = x.shape[1]
    wq = d // 4
    depth = prm["norm_mix_g"].shape[0]
    gout = prm["out_norm_g"][l]
    z = _mm_n(h, wts["in_t"], l, nt=True, tn=256, out_dtype=F32, name="in_proj", comm=comms.get("in_proj"))
    yan, ycn = _mix_fwd(z, prm["conv_a_w"][l], prm["conv_c_w"][l], _row(prm["conv_c_b"][l]), _row(prm["ln_c_g"][l]),
                        _row(prm["ln_c_b"][l]), _row(gout[:wq]), _row(gout[3 * wq:]), "mix_fwd")
    yb, lse = _attn_fwd(z, btiles, 3 * wq // LANES, "attn_fwd", comm=comms.get("attn_fwd"))
    y, x_mid, h2 = _out_proj(yan, yb, ycn, _row(gout[wq:3 * wq]), x, wts["out"], l, _row(prm["norm_ffn_g"][l]),
                             "out_proj", comm=comms.get("out_proj"))
    up = _mm_n(h2, wts["up"], l, nt=False, tn=512, out_dtype=BF16, name="up_proj", comm=comms.get("up_proj"))
    act = _ffn_act_fwd(up, prm["conv_f_w"][l], "ffn_act_fwd", comm=comms.get("ffn_act_fwd"))
    g_next = _row(prm["norm_mix_g"][l + 1]) if l + 1 < depth else None
    x_out, h_next = _down_proj(act, wts["down"], l, x_mid, g_next, "down_proj", comm=comms.get("down_proj"))
    return x_out, h_next, (x, h, z, yb, lse, y, x_mid, h2, up)


def _layer_bwd(l, dxo, dxo_b, saved, wts, prm, btiles, dbias, comms):
    x, h, z, yb, lse, y, x_mid, h2, up = saved
    d = x.shape[1]
    wq = d // 4
    gout = prm["out_norm_g"][l]
    dact = _mm_n(dxo_b, wts["down"], l, nt=True, tn=256, out_dtype=BF16, name="down_proj_dx", comm=comms.get("down_proj_dx"))
    act, dgate, dval, dwf = _ffn_act_bwd(up, dact, prm["conv_f_w"][l], "ffn_act_bwd", comm=comms.get("ffn_act_bwd"))
    g_down = _mm_tn(act, dxo_b, t=256, name="down_proj_dw")
    comms.ready("down", g_down)
    dxm, dxm_b, dg_ffn = _proj_dx([dgate, dval], wts["up"], l, True, x_mid, _row(prm["norm_ffn_g"][l]), dxo, "up_proj_dx",
                                  comm=comms.get("up_proj_dx"))
    g_up = _mm_tn2(h2, dgate, dval, t=256, name="up_proj_dw", comm=comms.get("up_proj_dw"))
    comms.ready("up", g_up)
    dy = _mm_n(dxm_b, wts["out"], l, nt=True, tn=256, out_dtype=F32, name="out_proj_dx")
    g_out = _mm_tn(y, dxm_b, t=256, name="out_proj_dw")
    comms.ready("out", g_out)
    dza, dzc, dwa, dwc, dcb, dlg, dlb, dga, dgc = _mix_bwd(
        z, dy, prm["conv_a_w"][l], prm["conv_c_w"][l], _row(prm["conv_c_b"][l]), _row(prm["ln_c_g"][l]),
        _row(prm["ln_c_b"][l]), _row(gout[:wq]), _row(gout[3 * wq:]), "mix_bwd", comm=comms.get("mix_bwd"))
    dyb, delta, dgb = _yb_norm_bwd(yb, dy, _row(gout[wq:3 * wq]), "yb_norm_bwd", comm=comms.get("yb_norm_bwd"))
    dq, dk, dv, dbias = _attn_bwd(z, btiles, dyb, lse, delta, dbias, 3 * wq // LANES, "attn_bwd",
                                  comm=comms.get("attn_bwd"))
    dz = [dza, dq, dk, dv, dzc]
    dx, dx_b, dg_mix = _proj_dx(dz, wts["in_t"], l, False, x, _row(prm["norm_mix_g"][l]), dxm, "in_proj_dx",
                                comm=comms.get("in_proj_dx"))
    g_in_t = _mm_tn_pieces(dz, h, t=256, name="in_proj_dw")
    big = {"in_t": g_in_t, "out": g_out, "up": g_up, "down": g_down}
    small = {"norm_mix_g": dg_mix[0], "conv_a_w": dwa, "conv_c_w": dwc, "conv_c_b": dcb[0], "ln_c_g": dlg[0],
             "ln_c_b": dlb[0], "out_norm_g": jnp.concatenate([dga[0], dgb[0], dgc[0]]), "norm_ffn_g": dg_ffn[0],
             "conv_f_w": dwf}
    return dx, dx_b, big, small, dbias


def _local_step(x, tgt, wts, prm, sched):
    depth = prm["norm_mix_g"].shape[0]
    buckets = jnp.asarray(_t5_bucket_table())
    btiles = _bias_tiles(prm["rel_bias"], buckets, "bias_tiles")
    saved = []
    h = _rms_fwd(x, _row(prm["norm_mix_g"][0]), "rms_mix_fwd")
    for l in range(depth):
        x, h, sv = _layer_fwd(l, x, h, wts, prm, btiles, sched.fwd_comms(l))
        saved.append(sv)
    loss, dx, dx_b, dg_final = _final_loss(x, _row(prm["final_g"]), tgt, "final_loss")
    dbias = jnp.zeros(btiles.shape, F32)
    small = [None] * depth
    for l in reversed(range(depth)):
        dx, dx_b, grads, small[l], dbias = _layer_bwd(l, dx, dx_b, saved[l], wts, prm, btiles, dbias, sched.bwd_comms(l))
        sched.after_bwd(l, grads)
    nbk, heads = prm["rel_bias"].shape
    d_rel = _bias_grad(dbias, buckets, nbk, "bias_grad")[:, :heads]
    return loss, dx, small, d_rel, dg_final[0]


BIG = ("in_t", "out", "up", "down")
COL_SHARDED = ("up",)
N_CHIPS = 4
N_DEV = 8
BF16_ROWS = 16


def _me():
    return lax.axis_index("x"), lax.axis_index("y"), lax.axis_index("c")


def _chip_of(x, y):
    return 2 * x + y


def _other_chips(x, y):
    return ((1 - x, y), (x, 1 - y), (1 - x, 1 - y))


def _remote(src, dst, send_sem, recv_sem, device):
    return pltpu.make_async_remote_copy(src_ref=src, dst_ref=dst, send_sem=send_sem, recv_sem=recv_sem,
                                        device_id=device, device_id_type=MESH)


def _ag_comm(wts, layer, ici_keys, fwd_keys):
    keys = tuple(k for k in BIG if k in ici_keys or k in fwd_keys)

    def geo(k):
        _, rows, cols = wts[k].shape
        return (rows, cols // N_CHIPS) if k in COL_SHARDED else (rows // N_CHIPS, cols)

    def copies(refs, sems):
        g = dict(zip(keys, refs))
        isend, irecv, dsend, drecv = sems
        x, y, c = _me()
        mine = _chip_of(x, y)

        def region(k, chip, half):
            r, cc = geo(k)
            h = r // 2
            if k in COL_SHARDED:
                return g[k].at[layer, pl.ds(pl.multiple_of(half * h, BF16_ROWS), h), pl.ds(pl.multiple_of(chip * cc, LANES), cc)]
            return g[k].at[layer, pl.ds(pl.multiple_of(chip * r + half * h, BF16_ROWS), h), :]

        def ici(k, f, landing):
            chip = _other_chips(x, y)[f]
            where = region(k, _chip_of(*chip) if landing else mine, c)
            i = keys.index(k)
            return _remote(where, where, isend.at[i, f], irecv.at[i, f], (*chip, c))

        def fwd(k, f, landing):
            chip = _other_chips(x, y)[f]
            where = region(k, _chip_of(*chip), 1 - c if landing else c)
            i = keys.index(k)
            return _remote(where, where, dsend.at[i, f], drecv.at[i, f], (x, y, 1 - c))

        return ici, fwd

    def start(ins, outs, sems):
        ici, fwd = copies(outs, sems)
        for k in keys:
            for f in range(3):
                if k in ici_keys:
                    ici(k, f, False).start()
                else:
                    fwd(k, f, False).start()

    def finish(ins, outs, sems):
        ici, fwd = copies(outs, sems)
        for k in keys:
            for f in range(3):
                if k in ici_keys:
                    ici(k, f, True).wait_recv()
                    if k in fwd_keys:
                        fwd(k, f, False).start()
        for k in keys:
            for f in range(3):
                if k in fwd_keys:
                    fwd(k, f, True).wait_recv()
                    fwd(k, f, False).wait_send()
                if k in ici_keys:
                    ici(k, f, False).wait_send()

    def done(res):
        wts.update(zip(keys, res))

    n = len(keys)
    return _Comm([wts[k] for k in keys], [_sds(wts[k].shape, BF16) for k in keys], {i: i for i in range(n)},
                 [pltpu.SemaphoreType.DMA((n, 3)) for _ in range(4)], start, finish, done)


def _small_gather_comm(slab, store):
    def copies(ins, outs, sems):
        send, recv, lsem = sems
        x, y, c = _me()
        mine = _chip_of(x, y)
        own = pltpu.make_async_copy(ins[0], outs[0].at[mine], lsem)
        pairs = []
        for f, chip in enumerate(_other_chips(x, y)):
            out = _remote(ins[0], outs[0].at[mine], send.at[f], recv.at[f], (*chip, c))
            land = _remote(ins[0], outs[0].at[_chip_of(*chip)], send.at[f], recv.at[f], (*chip, c))
            pairs.append((out, land))
        return own, pairs

    def start(ins, outs, sems):
        own, pairs = copies(ins, outs, sems)
        own.start()
        for out, _ in pairs:
            out.start()

    def finish(ins, outs, sems):
        own, pairs = copies(ins, outs, sems)
        for out, land in pairs:
            land.wait_recv()
            out.wait_send()
        own.wait()

    def done(res):
        store["small"] = res[0]

    return _Comm([slab], [_sds((N_CHIPS,) + slab.shape, F32)], {},
                 [pltpu.SemaphoreType.DMA((3,)), pltpu.SemaphoreType.DMA((3,)), pltpu.SemaphoreType.DMA], start, finish, done)


def _piece_geo(g):
    geo = {}
    for k in g:
        rows, cols = g[k].shape
        geo[k] = (rows // 2, cols // N_CHIPS) if k in COL_SHARDED else (rows // (2 * N_CHIPS), cols)
    return geo


def _swap_comm(g, keys, done):
    geo = _piece_geo(g)
    n_copies = sum(N_CHIPS if k in COL_SHARDED else 1 for k in keys)

    def copies(ins, outs, sems):
        g_refs, t_refs = dict(zip(keys, ins)), dict(zip(keys, outs))
        send, recv = sems
        x, y, c = _me()
        pairs = []
        for k in keys:
            h, cc = geo[k]
            if k in COL_SHARDED:
                rows = pl.ds(pl.multiple_of((1 - c) * h, BF16_ROWS), h)
                pairs += [(g_refs[k].at[rows, pl.ds(j * cc, cc)], t_refs[k].at[j]) for j in range(N_CHIPS)]
            else:
                pairs.append((g_refs[k].at[:, 1 - c], t_refs[k]))
        return [_remote(src, dst, send.at[i], recv.at[i], (x, y, 1 - c)) for i, (src, dst) in enumerate(pairs)]

    def start(ins, outs, sems):
        for cp in copies(ins, outs, sems):
            cp.start()

    def finish(ins, outs, sems):
        for cp in copies(ins, outs, sems):
            cp.wait()

    ins = [g[k] if k in COL_SHARDED else g[k].reshape(N_CHIPS, 2, geo[k][0], geo[k][1]) for k in keys]
    return _Comm(ins, [_sds((N_CHIPS,) + geo[k], BF16) for k in keys], {},
                 [pltpu.SemaphoreType.DMA((n_copies,)) for _ in range(2)], start, finish,
                 lambda res: done(dict(zip(keys, res))))


def _pair_sum(g, theirs, c_arr, keys):
    geo = _piece_geo(g)

    def body(c_ref, *refs):
        nk = len(keys)
        for i in range(nk):
            refs[2 * nk + i][...] = (refs[i][...].astype(F32) + refs[nk + i][...].astype(F32)).astype(BF16)

    in_specs, ins = [], []
    for k in keys:
        h, cc = geo[k]
        if k in COL_SHARDED:
            in_specs.append(pl.BlockSpec((h, cc), lambda j, c_ref: (c_ref[0], j)))
            ins.append(g[k])
        else:
            in_specs.append(pl.BlockSpec((None, h, cc), lambda j, c_ref: (2 * j + c_ref[0], 0, 0)))
            ins.append(g[k].reshape(2 * N_CHIPS, h, cc))
    slab = [pl.BlockSpec((None,) + geo[k], lambda j, c_ref: (j, 0, 0)) for k in keys]
    res = _pcall(body, name="rs_pair_sum", out_shape=tuple(_sds((N_CHIPS,) + geo[k], BF16) for k in keys), grid=(N_CHIPS,),
                 in_specs=in_specs + slab, out_specs=tuple(slab), prefetch=1)(c_arr, *ins, *[theirs[k] for k in keys])
    return dict(zip(keys, res))


def _rs_comm(p, keys, store):
    def copies(ins, outs, sems):
        send, recv = sems
        x, y, c = _me()
        return [_remote(ins[i].at[_chip_of(*chip)], outs[i].at[f], send.at[i, f], recv.at[i, f], (*chip, c))
                for i in range(len(keys)) for f, chip in enumerate(_other_chips(x, y))]

    def start(ins, outs, sems):
        for cp in copies(ins, outs, sems):
            cp.start()

    def finish(ins, outs, sems):
        for cp in copies(ins, outs, sems):
            cp.wait()

    def done(res):
        store.update(zip(keys, res))

    return _Comm([p[k] for k in keys], [_sds((3,) + p[k].shape[1:], BF16) for k in keys], {},
                 [pltpu.SemaphoreType.DMA((len(keys), 3)) for _ in range(2)], start, finish, done)


def _quad_sum(p, b, where, l, full):
    parts = 2
    nk = len(BIG)

    def body(where_ref, *refs):
        for i in range(nk):
            acc = refs[i][...].astype(F32)
            for f in range(3):
                acc = acc + refs[nk + 3 * i + f][...].astype(F32)
            refs[5 * nk + i][...] = acc

    own, recv, outs = [], [], []
    for k in BIG:
        h, cc = p[k].shape[1:]
        th = h // parts
        own.append(pl.BlockSpec((None, th, cc), lambda i, w_ref: (w_ref[0], i, 0)))
        recv += [pl.BlockSpec((None, th, cc), lambda i, w_ref, f=f: (f, i, 0)) for f in range(3)]
        outs.append(pl.BlockSpec((None, None, th, cc), lambda i, w_ref: (l, w_ref[1], i, 0)))
    args = [p[k] for k in BIG] + [b[k] for k in BIG for _ in range(3)] + [full[k] for k in BIG]
    res = _pcall(body, name="rs_quad_sum", out_shape=tuple(_sds(full[k].shape, F32) for k in BIG), grid=(parts,),
                 in_specs=own + recv + [ANY] * nk, out_specs=tuple(outs), prefetch=1,
                 aliases={1 + 4 * nk + i: i for i in range(nk)})(where, *args)
    return dict(zip(BIG, res))


def _share_comm(layers, full, done):
    nk = len(BIG)

    def copies(outs, sems, landing):
        send, recv = sems
        x, y, c = _me()
        half = 1 - c if landing else c
        return [_remote(outs[i].at[l, half], outs[i].at[l, half], send.at[i, j], recv.at[i, j], (x, y, 1 - c))
                for i in range(nk) for j, l in enumerate(layers)]

    def start(ins, outs, sems):
        for cp in copies(outs, sems, False):
            cp.start()

    def finish(ins, outs, sems):
        for cp in copies(outs, sems, True):
            cp.wait_recv()
        for cp in copies(outs, sems, False):
            cp.wait_send()

    return _Comm([full[k] for k in BIG], [_sds(full[k].shape, F32) for k in BIG], {i: i for i in range(nk)},
                 [pltpu.SemaphoreType.DMA((nk, len(layers))) for _ in range(2)], start, finish,
                 lambda res: done(dict(zip(BIG, res))))


def _gather_comm(slab, done):
    def copies(ins, outs, sems, landing):
        send, recv = sems
        x, y, c = _me()
        me = 4 * x + 2 * y + c
        out = []
        for mask in range(1, N_DEV):
            peer = (x ^ (mask >> 2), y ^ ((mask >> 1) & 1), c ^ (mask & 1))
            slot = 4 * peer[0] + 2 * peer[1] + peer[2] if landing else me
            out.append(_remote(ins[0], outs[0].at[slot], send.at[mask - 1], recv.at[mask - 1], peer))
        return out

    def start(ins, outs, sems):
        for cp in copies(ins, outs, sems, False):
            cp.start()

    def finish(ins, outs, sems):
        for cp in copies(ins, outs, sems, True):
            cp.wait_recv()
        for cp in copies(ins, outs, sems, False):
            cp.wait_send()

    return _Comm([slab], [_sds((N_DEV,) + slab.shape, F32)], {},
                 [pltpu.SemaphoreType.DMA((N_DEV - 1,)), pltpu.SemaphoreType.DMA((N_DEV - 1,))], start, finish,
                 lambda res: done(res[0]))


def _sum_slabs(slabs, own, me):
    n, r, lanes = slabs.shape
    tr = r // 2

    def body(me_ref, s_ref, own_ref, o_ref):
        o_ref[...] = jnp.zeros((tr, lanes), F32)
        for i in range(n):
            @pl.when(me_ref[0] == i)
            def _():
                o_ref[...] += own_ref[...]

            @pl.when(me_ref[0] != i)
            def _():
                o_ref[...] += s_ref[i]

    return _pcall(body, name="sum_partials", out_shape=_sds((r, lanes), F32), grid=(2,),
                  in_specs=[pl.BlockSpec((n, tr, lanes), lambda i, me_ref: (0, i, 0)),
                            pl.BlockSpec((tr, lanes), lambda i, me_ref: (i, 0))],
                  out_specs=pl.BlockSpec((tr, lanes), lambda i, me_ref: (i, 0)), prefetch=1)(me, slabs, own)


def _cast_into_gathered(w, chip, by_cols, name):
    l, r, c = w.shape

    def body(chip_ref, w_ref, o_ref):
        o_ref[...] = w_ref[...].astype(BF16)

    if by_cols:
        shape, out = (l, r, N_CHIPS * c), pl.BlockSpec((None, r, c), lambda i, chip_ref: (i, 0, chip_ref[0]))
    else:
        shape, out = (l, N_CHIPS * r, c), pl.BlockSpec((None, r, c), lambda i, chip_ref: (i, chip_ref[0], 0))
    return _pcall(body, name=name, out_shape=_sds(shape, BF16), grid=(l,),
                  in_specs=[pl.BlockSpec((None, r, c), lambda i, chip_ref: (i, 0, 0))], out_specs=out, prefetch=1)(chip, w)


def _adamw_math(w, g, m, v):
    mn = ADAM_B1 * m + (1.0 - ADAM_B1) * g
    vn = ADAM_B2 * v + (1.0 - ADAM_B2) * (g * g)
    m_hat = mn / (1.0 - ADAM_B1 ** ADAM_STEP)
    v_hat = vn / (1.0 - ADAM_B2 ** ADAM_STEP)
    return -ADAM_LR * (m_hat / (jnp.sqrt(v_hat) + ADAM_EPS) + ADAM_WD * w), mn, vn


def _adamw(w, g, m, v, name, tr):
    r, c = w.shape

    def body(w_ref, g_ref, m_ref, v_ref, go_ref, d_ref, mo_ref, vo_ref):
        gv = g_ref[...]
        go_ref[...] = gv
        d_ref[...], mo_ref[...], vo_ref[...] = _adamw_math(w_ref[...], gv, m_ref[...], v_ref[...])

    blk = pl.BlockSpec((tr, c), lambda i: (i, 0))
    return _pcall(body, name=name, out_shape=tuple(_sds((r, c), F32) for _ in range(4)), grid=(r // tr,),
                  in_specs=[blk] * 4, out_specs=(blk, blk, blk, blk))(w, g, m, v)


def _adamw_small(groups, name):
    count = len(groups[0])
    shapes = [a.shape for a in groups[0]]
    as2d = [(math.prod(s[:-1]), s[-1]) for s in shapes]

    def body(*refs):
        for i in range(count):
            out = _adamw_math(*[refs[j * count + i][...] for j in range(4)])
            for j in range(3):
                refs[(4 + j) * count + i][...] = out[j]

    specs = [pl.BlockSpec(s, lambda i: (0, 0)) for s in as2d]
    res = _pcall(body, name=name, out_shape=tuple(_sds(s, F32) for _ in range(3) for s in as2d), grid=(1,),
                 in_specs=specs * 4, out_specs=tuple(specs * 3))(*[a.reshape(s) for grp in groups for a, s in zip(grp, as2d)])
    return [[res[j * count + i].reshape(shapes[i]) for i in range(count)] for j in range(3)]


AG_RIDES = {"in_proj": (0, ("out",), ("down",)), "attn_fwd": (0, ("up",), ("out",)), "out_proj": (0, (), ("up",)),
            "up_proj": (1, ("in_t",), ()), "ffn_act_fwd": (1, ("down",), ()), "down_proj": (1, (), ("in_t",))}
AG_FIRST = ("in_t", "down")
RS_RIDES = {"down_proj_dx": "swap", "ffn_act_bwd": ("up",), "up_proj_dx": "share", "attn_bwd": ("in_t", "out", "down")}
RS_RIDES_LAST = {"down_proj_dx": "swap", "ffn_act_bwd": ("up",), "up_proj_dx": "share", "up_proj_dw": ("in_t",),
                 "mix_bwd": ("down",), "yb_norm_bwd": ("out",)}
EARLY = ("out", "up", "down")
EARLY_RIDES = {"mix_bwd": "swap", "attn_bwd": ("up", "down"), "in_proj_dx": ("out",)}


class _Reduction:
    def __init__(self, grads):
        self.grads, self.pairs, self.recv = grads, {}, {}


class _MeshSchedule:
    def __init__(self, wts, depth, c_arr, where):
        self.wts, self.depth, self.c_arr, self.where = wts, depth, c_arr, where
        self.pending, self.last, self.full, self.unshared = None, None, None, []

    def fwd_comms(self, l):
        rides = _Rides()
        for name, (off, ici, fwd) in AG_RIDES.items():
            if l + off == 0:
                ici, fwd = (tuple(k for k in keys if k not in AG_FIRST) for keys in (ici, fwd))
            if l + off < self.depth and (ici or fwd):
                rides.add(name, lambda ride=(l + off, ici, fwd): _ag_comm(self.wts, *ride))
        return rides

    def _shared(self, full):
        self.full, self.unshared = full, []

    def _ride(self, red, what, swap_keys):
        if what == "swap":
            return _swap_comm(red.grads, swap_keys, lambda theirs: red.pairs.update(
                _pair_sum(red.grads, theirs, self.c_arr, swap_keys)))
        if what == "share":
            return _share_comm(self.unshared, self.full, self._shared)
        return _rs_comm(red.pairs, what, red.recv)

    def bwd_comms(self, l):
        rides = _Rides()
        if self.pending is not None:
            for name, what in (RS_RIDES_LAST if l == 0 else RS_RIDES).items():
                if what != "share" or self.unshared:
                    rides.add(name, lambda what=what, red=self.pending: self._ride(red, what, BIG))
        if l == 0:
            self.last = _Reduction(rides.grads)
            for name, what in EARLY_RIDES.items():
                rides.add(name, lambda what=what: self._ride(self.last, what, EARLY))
        return rides

    def _reduce(self, l, red):
        self.full = _quad_sum(red.pairs, red.recv, self.where, l, self.full)
        self.unshared = self.unshared + [l]

    def after_bwd(self, l, grads):
        if self.pending is not None:
            self._reduce(l + 1, self.pending)
        if self.full is None:
            geo = _piece_geo(grads)
            self.full = {k: jnp.zeros((self.depth, 2) + geo[k], F32) for k in BIG}
        self.pending = _Reduction(grads) if l > 0 else None
        if l == 0:
            self.last.grads = grads

    def finish(self, extra):
        red = self.last
        late = tuple(k for k in BIG if k not in red.pairs)
        _run_comm(self._ride(red, "swap", late), "rs_swap_halves")
        _run_comm(_both(_rs_comm(red.pairs, late, red.recv), extra), "rs_to_owners")
        self._reduce(0, red)
        _run_comm(_share_comm(self.unshared, self.full, self._shared), "rs_share")
        return self.full


SHARDED_SMALL = ("conv_a_w", "conv_c_w", "conv_f_w")
SMALL = ("norm_mix_g", "conv_a_w", "conv_c_w", "conv_c_b", "ln_c_g", "ln_c_b", "out_norm_g", "norm_ffn_g",
         "conv_f_w", "rel_bias", "final_g")
SLAB_ROWS = 16


def _pack(arrays):
    flat = jnp.concatenate([a.reshape(-1) for a in arrays])
    unit = SLAB_ROWS * LANES
    total = -(-flat.shape[0] // unit) * unit
    return jnp.pad(flat, (0, total - flat.shape[0])).reshape(-1, LANES)


def _unpack(slab, shapes):
    flat = slab.reshape(-1)
    out, off = [], 0
    for shp in shapes:
        size = math.prod(shp)
        out.append(flat[off:off + size].reshape(shp))
        off += size
    return out


def kernel(x, norm_mix_g, w_in, conv_a_w, conv_c_w, conv_c_b, ln_c_g, ln_c_b, out_norm_g, w_out, norm_ffn_g, w_up, conv_f_w, w_down, rel_bias, final_g, loss_target, m_norm_mix_g, m_w_in, m_conv_a_w, m_conv_c_w, m_conv_c_b, m_ln_c_g, m_ln_c_b, m_out_norm_g, m_w_out, m_norm_ffn_g, m_w_up, m_conv_f_w, m_w_down, m_rel_bias, m_final_g, v_norm_mix_g, v_w_in, v_conv_a_w, v_conv_c_w, v_conv_c_b, v_ln_c_g, v_ln_c_b, v_out_norm_g, v_w_out, v_norm_ffn_g, v_w_up, v_conv_f_w, v_w_down, v_rel_bias, v_final_g):
    weights = dict(norm_mix_g=norm_mix_g, w_in=w_in, conv_a_w=conv_a_w, conv_c_w=conv_c_w, conv_c_b=conv_c_b,
                   ln_c_g=ln_c_g, ln_c_b=ln_c_b, out_norm_g=out_norm_g, w_out=w_out, norm_ffn_g=norm_ffn_g, w_up=w_up,
                   conv_f_w=conv_f_w, w_down=w_down, rel_bias=rel_bias, final_g=final_g)
    mom_m = dict(norm_mix_g=m_norm_mix_g, w_in=m_w_in, conv_a_w=m_conv_a_w, conv_c_w=m_conv_c_w, conv_c_b=m_conv_c_b,
                 ln_c_g=m_ln_c_g, ln_c_b=m_ln_c_b, out_norm_g=m_out_norm_g, w_out=m_w_out, norm_ffn_g=m_norm_ffn_g,
                 w_up=m_w_up, conv_f_w=m_conv_f_w, w_down=m_w_down, rel_bias=m_rel_bias, final_g=m_final_g)
    mom_v = dict(norm_mix_g=v_norm_mix_g, w_in=v_w_in, conv_a_w=v_conv_a_w, conv_c_w=v_conv_c_w, conv_c_b=v_conv_c_b,
                 ln_c_g=v_ln_c_g, ln_c_b=v_ln_c_b, out_norm_g=v_out_norm_g, w_out=v_w_out, norm_ffn_g=v_norm_ffn_g,
                 w_up=v_w_up, conv_f_w=v_conv_f_w, w_down=v_w_down, rel_bias=v_rel_bias, final_g=v_final_g)
    xi, yi, ci = _me()
    chip = _chip_of(xi, yi)
    c_arr = jnp.reshape(ci, (1,)).astype(I32)
    chip_arr = jnp.reshape(chip, (1,)).astype(I32)
    me_arr = jnp.reshape(4 * xi + 2 * yi + ci, (1,)).astype(I32)
    where = jnp.stack([chip, ci]).astype(I32)
    depth = w_out.shape[0]

    wts = {"in_t": _cast_into_gathered(jnp.swapaxes(w_in, 1, 2), chip_arr, False, "cast_in"),
           "out": _cast_into_gathered(w_out, chip_arr, False, "cast_out"),
           "up": _cast_into_gathered(w_up, chip_arr, True, "cast_up"),
           "down": _cast_into_gathered(w_down, chip_arr, False, "cast_down")}
    store = {}
    _run_comm(_both(_ag_comm(wts, 0, AG_FIRST, AG_FIRST),
                    _small_gather_comm(_pack([weights[n] for n in SHARDED_SMALL]), store)), "ag_weights")
    prm = {n: weights[n] for n in SMALL if n not in SHARDED_SMALL}
    per_chip = [_unpack(store["small"][j], [weights[n].shape for n in SHARDED_SMALL]) for j in range(N_CHIPS)]
    for i, n in enumerate(SHARDED_SMALL):
        prm[n] = jnp.concatenate([per_chip[j][i] for j in range(N_CHIPS)], axis=-1)

    sched = _MeshSchedule(wts, depth, c_arr, where)
    loss_row, dx, small, d_rel, d_final = _local_step(x[0], loss_target[0], wts, prm, sched)
    loss = lax.psum(loss_row[0, 0], ("x", "y", "c"))

    stacked = {n: jnp.stack([small[l][n] for l in range(depth)]) for n in small[0]}
    stacked["rel_bias"] = d_rel
    stacked["final_g"] = d_final
    full_shapes = [stacked[n].shape for n in SMALL]
    partial = _pack([stacked[n] for n in SMALL])
    reduced = sched.finish(_gather_comm(partial, lambda res: store.update(partials=res)))

    grads = {}
    shard_shapes = {"in_t": jnp.swapaxes(w_in, 1, 2).shape, "out": w_out.shape, "up": w_up.shape, "down": w_down.shape}
    red = {k: reduced[k].reshape(shard_shapes[k]) for k in BIG}
    grads["w_in"] = jnp.swapaxes(red["in_t"], 1, 2)
    grads["w_out"], grads["w_up"], grads["w_down"] = red["out"], red["up"], red["down"]
    delta, new_m, new_v = {}, {}, {}
    for n in ("w_in", "w_out", "w_up", "w_down"):
        shp = weights[n].shape
        flat = lambda a, shp=shp: a.reshape(shp[0] * shp[1], shp[2])
        tile = max(t for t in range(8, 257, 8) if shp[1] % t == 0)
        g, d, mn, vn = _adamw(flat(weights[n]), flat(grads[n]), flat(mom_m[n]), flat(mom_v[n]), "adamw_" + n, tile)
        grads[n], delta[n], new_m[n], new_v[n] = g.reshape(shp), d.reshape(shp), mn.reshape(shp), vn.reshape(shp)

    summed = _unpack(_sum_slabs(store["partials"], partial, me_arr), full_shapes)
    for n, g in zip(SMALL, summed):
        if n in SHARDED_SMALL:
            width = weights[n].shape[-1]
            g = lax.dynamic_slice_in_dim(g, chip * width, width, axis=g.ndim - 1)
        grads[n] = g
    res = _adamw_small([[src[n] for n in SMALL] for src in (weights, grads, mom_m, mom_v)], "adamw_small")
    for i, n in enumerate(SMALL):
        delta[n], new_m[n], new_v[n] = res[0][i], res[1][i], res[2][i]

    order = ("norm_mix_g", "w_in", "conv_a_w", "conv_c_w", "conv_c_b", "ln_c_g", "ln_c_b", "out_norm_g", "w_out",
             "norm_ffn_g", "w_up", "conv_f_w", "w_down", "rel_bias", "final_g")
    return (loss, dx[None], *[grads[n] for n in order], *[delta[n] for n in order], *[new_m[n] for n in order],
            *[new_v[n] for n in order])
```

```python
import functools
import math

import numpy as np
import jax
import jax.numpy as jnp
from jax import lax
from jax.experimental import pallas as pl
from jax.experimental.pallas import tpu as pltpu

F32 = jnp.float32
BF16 = jnp.bfloat16
I32 = jnp.int32

EPS = 1e-6
NEG = -1e30
D_HEAD = 64
LANES = 128
BLK = 128
ATTN_GROUP_FWD = 16
ATTN_GROUP_BWD = 16
DILATED_BRANCHES = ((128, 1), (512, 4), (2048, 16))
NUM_BUCKETS = 32
MAX_DISTANCE = 2048
SHORT_CONV = 3
CONFORMER_CONV = 31
FFN_CONV = 3
PAD_SHORT = 8
PAD_LONG = 32
ROW_CHUNK = 256
V7X_VMEM_BYTES = 64 * 1024 * 1024
VMEM_REQUEST = V7X_VMEM_BYTES * 7 // 8

ADAM_LR = 0.001
ADAM_B1 = 0.9
ADAM_B2 = 0.999
ADAM_EPS = 1e-08
ADAM_WD = 0.01
ADAM_STEP = 10

MESH = pl.DeviceIdType.MESH
ANY = pl.BlockSpec(memory_space=pl.ANY)


def _sds(shape, dtype):
    return jax.ShapeDtypeStruct(tuple(shape), dtype)


class _Comm:
    def __init__(self, ins, out_shapes, aliases, sems, start, finish, done):
        self.ins, self.out_shapes, self.aliases, self.sems = list(ins), list(out_shapes), dict(aliases), list(sems)
        self.start, self.finish, self.done = start, finish, done


def _pcall(body, *, name, out_shape, grid=(), in_specs=None, out_specs=None, scratch_shapes=(), vmem=VMEM_REQUEST,
           aliases=None, prefetch=0, comm=None):
    params = pltpu.CompilerParams(dimension_semantics=("arbitrary",) * len(grid), vmem_limit_bytes=vmem)
    single = not isinstance(out_shape, (tuple, list))
    outs = [out_shape] if single else list(out_shape)
    ospecs = [out_specs] if single else list(out_specs)
    ispecs, scratch, aliases = list(in_specs), list(scratch_shapes), dict(aliases or {})
    n_in, n_out, n_scr = len(ispecs), len(outs), len(scratch)
    kernel_body = body
    if comm is not None:
        n_ci, n_co = len(comm.ins), len(comm.out_shapes)

        def kernel_body(*refs):
            pre, rest = refs[:prefetch], refs[prefetch:]
            core_in, c_in = rest[:n_in], rest[n_in:n_in + n_ci]
            o0 = n_in + n_ci
            core_out, c_out = rest[o0:o0 + n_out], rest[o0 + n_out:o0 + n_out + n_co]
            s0 = o0 + n_out + n_co
            core_scr, c_sem = rest[s0:s0 + n_scr], rest[s0 + n_scr:]
            first = functools.reduce(jnp.logical_and, [pl.program_id(a) == 0 for a in range(len(grid))])
            last = functools.reduce(jnp.logical_and, [pl.program_id(a) == grid[a] - 1 for a in range(len(grid))])
            pl.when(first)(lambda: comm.start(c_in, c_out, c_sem))
            body(*pre, *core_in, *core_out, *core_scr)
            pl.when(last)(lambda: comm.finish(c_in, c_out, c_sem))

        for i, o in comm.aliases.items():
            aliases[prefetch + n_in + i] = n_out + o
        ispecs += [ANY] * n_ci
        ospecs += [ANY] * n_co
        outs += comm.out_shapes
        scratch += comm.sems
    if prefetch:
        spec = pltpu.PrefetchScalarGridSpec(num_scalar_prefetch=prefetch, grid=grid, in_specs=ispecs,
                                            out_specs=tuple(ospecs), scratch_shapes=scratch)
        call = pl.pallas_call(kernel_body, name=name, out_shape=tuple(outs), grid_spec=spec,
                              input_output_aliases=aliases, compiler_params=params)
    else:
        call = pl.pallas_call(kernel_body, name=name, out_shape=tuple(outs), grid=grid, in_specs=ispecs,
                              out_specs=tuple(ospecs), scratch_shapes=scratch, input_output_aliases=aliases,
                              compiler_params=params)

    def run(*args):
        res = call(*args, *(comm.ins if comm is not None else ()))
        if comm is not None:
            comm.done(res[n_out:])
        return res[0] if single else tuple(res[:n_out])

    return run


def _both(a, b):
    def split(refs, na):
        return refs[:na], refs[na:]

    def run(which):
        def go(ins, outs, sems):
            for comm, i, o, s in zip((a, b), split(ins, len(a.ins)), split(outs, len(a.out_shapes)), split(sems, len(a.sems))):
                getattr(comm, which)(i, o, s)
        return go

    def done(res):
        a.done(res[:len(a.out_shapes)])
        b.done(res[len(a.out_shapes):])

    aliases = dict(a.aliases)
    aliases.update({len(a.ins) + i: len(a.out_shapes) + o for i, o in b.aliases.items()})
    return _Comm(a.ins + b.ins, a.out_shapes + b.out_shapes, aliases, a.sems + b.sems, run("start"), run("finish"), done)


def _run_comm(comm, name):
    def body(*refs):
        n_ci, n_co = len(comm.ins), len(comm.out_shapes)
        c_in, c_out, c_sem = refs[:n_ci], refs[n_ci:n_ci + n_co], refs[n_ci + n_co:]
        comm.start(c_in, c_out, c_sem)
        comm.finish(c_in, c_out, c_sem)

    res = pl.pallas_call(body, name=name, out_shape=tuple(comm.out_shapes), in_specs=[ANY] * len(comm.ins),
                         out_specs=tuple([ANY] * len(comm.out_shapes)), scratch_shapes=comm.sems,
                         input_output_aliases=comm.aliases)(*comm.ins)
    comm.done(res)


def _dot(a, b):
    return lax.dot_general(a, b, (((1,), (0,)), ((), ())), preferred_element_type=F32)


def _dot_nt(a, b):
    return lax.dot_general(a, b, (((1,), (1,)), ((), ())), preferred_element_type=F32)


def _dot_tn(a, b):
    return lax.dot_general(a, b, (((0,), (0,)), ((), ())), preferred_element_type=F32)


def _sigmoid(x):
    return 1.0 / (1.0 + jnp.exp(-x))


def _rstd(x):
    return lax.rsqrt(jnp.mean(x * x, axis=-1, keepdims=True) + EPS)


def _rms_fwd(x, g, name):
    s, d = x.shape
    tm = ROW_CHUNK

    def body(x_ref, g_ref, o_ref):
        xv = x_ref[...]
        o_ref[...] = (xv * _rstd(xv) * g_ref[...]).astype(BF16)

    return _pcall(body, name=name, out_shape=_sds((s, d), BF16), grid=(s // tm,),
                  in_specs=[pl.BlockSpec((tm, d), lambda i: (i, 0)), pl.BlockSpec((1, d), lambda i: (0, 0))],
                  out_specs=pl.BlockSpec((tm, d), lambda i: (i, 0)))(x, g)


def _final_loss(x, g, tgt, name):
    s, d = x.shape
    tm = ROW_CHUNK

    def body(x_ref, g_ref, t_ref, loss_ref, dx_ref, dxb_ref, dg_ref):
        i = pl.program_id(0)
        xv = x_ref[...]
        r = _rstd(xv)
        xh = xv * r
        e = xh * g_ref[...] - t_ref[...]
        lpart = 0.5 * jnp.sum(jnp.mean(e * e, axis=-1, keepdims=True), axis=0, keepdims=True)
        dy = e * (1.0 / d)
        gd = dy * g_ref[...]
        dx = r * (gd - xh * jnp.mean(gd * xh, axis=-1, keepdims=True))
        dx_ref[...] = dx
        dxb_ref[...] = dx.astype(BF16)
        part = jnp.sum(dy * xh, axis=0, keepdims=True)
        lrow = jnp.broadcast_to(lpart, (1, LANES))

        @pl.when(i == 0)
        def _():
            dg_ref[...] = part
            loss_ref[...] = lrow

        @pl.when(i > 0)
        def _():
            dg_ref[...] += part
            loss_ref[...] += lrow

    row = pl.BlockSpec((tm, d), lambda i: (i, 0))
    vec = pl.BlockSpec((1, d), lambda i: (0, 0))
    return _pcall(body, name=name,
                  out_shape=(_sds((1, LANES), F32), _sds((s, d), F32), _sds((s, d), BF16), _sds((1, d), F32)),
                  grid=(s // tm,), in_specs=[row, vec, row],
                  out_specs=(pl.BlockSpec((1, LANES), lambda i: (0, 0)), row, row, vec))(x, g, tgt)


def _mm_n(a, b, layer, *, nt, tn, out_dtype, name, resid=None, b_part=0, comm=None):
    s, k = a.shape
    n = b.shape[1] if nt else b.shape[2]
    rows = 512

    def body(a_ref, b_ref, *refs):
        o_ref = refs[-1]
        bv = b_ref[...]
        for r0 in range(0, s, rows):
            av = a_ref[r0:r0 + rows, :]
            prod = _dot_nt(av, bv) if nt else _dot(av, bv)
            if resid is not None:
                prod = refs[0][r0:r0 + rows, :] + prod
            o_ref[r0:r0 + rows, :] = prod.astype(out_dtype)

    b_spec = (pl.BlockSpec((None, tn, k), lambda j: (layer, j, b_part)) if nt
              else pl.BlockSpec((None, k, tn), lambda j: (layer, b_part, j)))
    col = pl.BlockSpec((s, tn), lambda j: (0, j))
    extra = () if resid is None else (resid,)
    return _pcall(body, name=name, out_shape=_sds((s, n), out_dtype), grid=(n // tn,),
                  in_specs=[pl.BlockSpec((s, k), lambda j: (0, 0)), b_spec] + [col] * len(extra),
                  out_specs=col, comm=comm)(a, b, *extra)


def _mm_tn(a, b, *, t, name):
    s, ka = a.shape
    n = b.shape[1]

    def body(a_ref, b_ref, o_ref):
        o_ref[...] = _dot_tn(a_ref[...], b_ref[...]).astype(BF16)

    return _pcall(body, name=name, out_shape=_sds((ka, n), BF16), grid=(ka // t,),
                  in_specs=[pl.BlockSpec((s, t), lambda i: (0, i)), pl.BlockSpec((s, n), lambda i: (0, 0))],
                  out_specs=pl.BlockSpec((t, n), lambda i: (i, 0)))(a, b)


def _mm_tn_pieces(pieces, b, *, t, name):
    s, n = b.shape
    blocks = [p.shape[1] // t for p in pieces]
    starts = [sum(blocks[:i]) for i in range(len(pieces))]

    def body(*refs):
        p_refs, b_ref, o_ref = refs[:len(pieces)], refs[len(pieces)], refs[len(pieces) + 1]
        j = pl.program_id(0)
        for p_ref, start, count in zip(p_refs, starts, blocks):
            @pl.when((j >= start) & (j < start + count))
            def _(p_ref=p_ref):
                o_ref[...] = _dot_tn(p_ref[...], b_ref[...]).astype(BF16)

    specs = [pl.BlockSpec((s, t), lambda j, start=start, count=count: (0, jnp.clip(j - start, 0, count - 1)))
             for start, count in zip(starts, blocks)]
    return _pcall(body, name=name, out_shape=_sds((sum(blocks) * t, n), BF16), grid=(sum(blocks),),
                  in_specs=specs + [pl.BlockSpec((s, n), lambda j: (0, 0))],
                  out_specs=pl.BlockSpec((t, n), lambda j: (j, 0)))(*pieces, b)


def _mm_tn2(a, b_lo, b_hi, *, t, name, comm=None):
    s, ka = a.shape
    half = b_lo.shape[1]
    nb = half // t

    def body(a_ref, lo_ref, hi_ref, o_ref):
        j = pl.program_id(0)

        @pl.when(j < nb)
        def _():
            o_ref[...] = _dot_tn(a_ref[...], lo_ref[...]).astype(BF16)

        @pl.when(j >= nb)
        def _():
            o_ref[...] = _dot_tn(a_ref[...], hi_ref[...]).astype(BF16)

    return _pcall(body, name=name, out_shape=_sds((ka, 2 * half), BF16), grid=(2 * nb,),
                  in_specs=[pl.BlockSpec((s, ka), lambda j: (0, 0)),
                            pl.BlockSpec((s, t), lambda j: (0, jnp.minimum(j, nb - 1))),
                            pl.BlockSpec((s, t), lambda j: (0, jnp.maximum(j - nb, 0)))],
                  out_specs=pl.BlockSpec((ka, t), lambda j: (0, j)), comm=comm)(a, b_lo, b_hi)


SUBLANES = 8


def _tap_windows(win, width, lead, rows):
    offs = [lead + k for k in range(width)]
    if width <= SUBLANES:
        return [win[o:o + rows, :] for o in offs]
    n = win.shape[0]
    out = {}
    for r in sorted({o % SUBLANES for o in offs}):
        base = win if r == 0 else pltpu.roll(win, n - r, axis=0)
        for o in offs:
            if o % SUBLANES == r:
                out[o - lead] = base[o - r:o - r + rows, :]
    return [out[k] for k in range(width)]


def _conv_taps(taps, w_ref):
    acc = None
    for k, tap in enumerate(taps):
        term = w_ref[pl.ds(k, 1), :] * tap
        acc = term if acc is None else acc + term
    return acc


def _causal_taps(win, width, pad, rows):
    return _tap_windows(win, width, pad - (width - 1), rows)


def _anticausal_taps(win, width, rows):
    return _tap_windows(win, width, 0, rows)[::-1]


def _conv_wgrad(dw_ref, g, taps):
    for k, tap in enumerate(taps):
        dw_ref[pl.ds(k, 1), :] += jnp.sum(g * tap, axis=0, keepdims=True)


def _mixer_a_fwd(ab, taps_t, wa_ref):
    ct = _conv_taps(taps_t, wa_ref)
    return ab * ct, ct


def _mixer_c_fwd(taps_u, wc_ref, cb_ref, lg_ref, lb_ref):
    u = _conv_taps(taps_u, wc_ref) + cb_ref[...]
    mu = jnp.mean(u, axis=-1, keepdims=True)
    uc = u - mu
    rs = lax.rsqrt(jnp.mean(uc * uc, axis=-1, keepdims=True) + EPS)
    uh = uc * rs
    ln = uh * lg_ref[...] + lb_ref[...]
    sg = _sigmoid(ln)
    return ln * sg, ln, sg, uh, rs


def _mix_fwd(z, wa, wc, cb, lg, lb, ga, gc, name):
    s = z.shape[0]
    w = wa.shape[1]
    nblk = z.shape[1] // w
    rc = ROW_CHUNK

    def body(ah_ref, ab_ref, ac_ref, cv_ref, cg_ref, wa_ref, wc_ref, cb_ref, lg_ref, lb_ref, ga_ref, gc_ref,
             ya_ref, yc_ref, tpad, upad):
        tpad[pl.ds(0, PAD_SHORT), :] = jnp.zeros((PAD_SHORT, w), F32)
        upad[pl.ds(0, PAD_LONG), :] = jnp.zeros((PAD_LONG, w), F32)

        def chunk(i, carry):
            base = pl.multiple_of(i * rc, rc)
            rows = pl.ds(base, rc)
            ah, ab, ac = ah_ref[rows, :], ab_ref[rows, :], ac_ref[rows, :]
            tpad[pl.ds(base + PAD_SHORT, rc), :] = ac * ah
            ya, _ = _mixer_a_fwd(ab, _causal_taps(tpad[pl.ds(base, rc + PAD_SHORT), :], SHORT_CONV, PAD_SHORT, rc), wa_ref)
            ya_ref[rows, :] = (ya * _rstd(ya) * ga_ref[...]).astype(BF16)
            upad[pl.ds(base + PAD_LONG, rc), :] = cv_ref[rows, :] * _sigmoid(cg_ref[rows, :])
            taps_u = _causal_taps(upad[pl.ds(base, rc + PAD_LONG), :], CONFORMER_CONV, PAD_LONG, rc)
            yc = _mixer_c_fwd(taps_u, wc_ref, cb_ref, lg_ref, lb_ref)[0]
            yc_ref[rows, :] = (yc * _rstd(yc) * gc_ref[...]).astype(BF16)
            return carry

        lax.fori_loop(0, s // rc, chunk, 0)

    def zblk(j):
        return pl.BlockSpec((s, w), lambda i: (0, j))

    def whole(a):
        return pl.BlockSpec(a.shape, lambda i: (0, 0))

    return _pcall(
        body, name=name, out_shape=(_sds((s, w), BF16), _sds((s, w), BF16)), grid=(1,),
        in_specs=[zblk(0), zblk(1), zblk(2), zblk(nblk - 2), zblk(nblk - 1)] + [whole(a) for a in (wa, wc, cb, lg, lb, ga, gc)],
        out_specs=(pl.BlockSpec((s, w), lambda i: (0, 0)), pl.BlockSpec((s, w), lambda i: (0, 0))),
        scratch_shapes=[pltpu.VMEM((s + PAD_SHORT, w), F32), pltpu.VMEM((s + PAD_LONG, w), F32)],
    )(z, z, z, z, z, wa, wc, cb, lg, lb, ga, gc)


def _mix_bwd(z, dy, wa, wc, cb, lg, lb, ga, gc, name, comm=None):
    s = z.shape[0]
    w = wa.shape[1]
    nblk = z.shape[1] // w
    nyb = dy.shape[1] // w
    rc = ROW_CHUNK

    def body(ah_ref, ab_ref, ac_ref, cv_ref, cg_ref, dya_ref, dyc_ref,
             wa_ref, wc_ref, cb_ref, lg_ref, lb_ref, ga_ref, gc_ref,
             dza_ref, dzc_ref, dwa_ref, dwc_ref, dcb_ref, dlg_ref, dlb_ref, dga_ref, dgc_ref,
             tpad, upad, dctp, dup):
        tpad[pl.ds(0, PAD_SHORT), :] = jnp.zeros((PAD_SHORT, w), F32)
        upad[pl.ds(0, PAD_LONG), :] = jnp.zeros((PAD_LONG, w), F32)
        dctp[pl.ds(s, PAD_SHORT), :] = jnp.zeros((PAD_SHORT, w), F32)
        dup[pl.ds(s, PAD_LONG), :] = jnp.zeros((PAD_LONG, w), F32)
        for ref in (dwa_ref, dwc_ref, dcb_ref, dlg_ref, dlb_ref, dga_ref, dgc_ref):
            ref[...] = jnp.zeros(ref.shape, F32)

        def rms_bwd(y, g_ref, dyn, dg_ref):
            r = _rstd(y)
            yh = y * r
            gd = dyn * g_ref[...]
            dg_ref[...] += jnp.sum(dyn * yh, axis=0, keepdims=True)
            return r * (gd - yh * jnp.mean(gd * yh, axis=-1, keepdims=True))

        def first(i, carry):
            base = pl.multiple_of(i * rc, rc)
            rows = pl.ds(base, rc)
            ah, ab, ac = ah_ref[rows, :], ab_ref[rows, :], ac_ref[rows, :]
            tpad[pl.ds(base + PAD_SHORT, rc), :] = ac * ah
            taps_t = _causal_taps(tpad[pl.ds(base, rc + PAD_SHORT), :], SHORT_CONV, PAD_SHORT, rc)
            ya, ct = _mixer_a_fwd(ab, taps_t, wa_ref)
            dya = rms_bwd(ya, ga_ref, dya_ref[rows, :], dga_ref)
            dza_ref[rows, w:2 * w] = (dya * ct).astype(BF16)
            dct = dya * ab
            dctp[rows, :] = dct
            _conv_wgrad(dwa_ref, dct, taps_t)

            upad[pl.ds(base + PAD_LONG, rc), :] = cv_ref[rows, :] * _sigmoid(cg_ref[rows, :])
            taps_u = _causal_taps(upad[pl.ds(base, rc + PAD_LONG), :], CONFORMER_CONV, PAD_LONG, rc)
            yc, ln, sg, uh, rs = _mixer_c_fwd(taps_u, wc_ref, cb_ref, lg_ref, lb_ref)
            dyc = rms_bwd(yc, gc_ref, dyc_ref[rows, :], dgc_ref)
            dln = dyc * (sg * (1.0 + ln * (1.0 - sg)))
            dlg_ref[...] += jnp.sum(dln * uh, axis=0, keepdims=True)
            dlb_ref[...] += jnp.sum(dln, axis=0, keepdims=True)
            duh = dln * lg_ref[...]
            du = rs * (duh - jnp.mean(duh, axis=-1, keepdims=True) - uh * jnp.mean(duh * uh, axis=-1, keepdims=True))
            dcb_ref[...] += jnp.sum(du, axis=0, keepdims=True)
            dup[rows, :] = du
            _conv_wgrad(dwc_ref, du, taps_u)
            return carry

        lax.fori_loop(0, s // rc, first, 0)

        def second(i, carry):
            base = pl.multiple_of(i * rc, rc)
            rows = pl.ds(base, rc)
            dt = _conv_taps(_anticausal_taps(dctp[pl.ds(base, rc + PAD_SHORT), :], SHORT_CONV, rc), wa_ref)
            dza_ref[rows, 0:w] = (dt * ac_ref[rows, :]).astype(BF16)
            dza_ref[rows, 2 * w:3 * w] = (dt * ah_ref[rows, :]).astype(BF16)
            du0 = _conv_taps(_anticausal_taps(dup[pl.ds(base, rc + PAD_LONG), :], CONFORMER_CONV, rc), wc_ref)
            sg = _sigmoid(cg_ref[rows, :])
            dzc_ref[rows, 0:w] = (du0 * sg).astype(BF16)
            dzc_ref[rows, w:2 * w] = (du0 * cv_ref[rows, :] * sg * (1.0 - sg)).astype(BF16)
            return carry

        lax.fori_loop(0, s // rc, second, 0)

    def blk(j):
        return pl.BlockSpec((s, w), lambda i: (0, j))

    def whole(a):
        return pl.BlockSpec(tuple(a.shape), lambda i: (0, 0))

    params = (wa, wc, cb, lg, lb, ga, gc)
    outs = (_sds((s, 3 * w), BF16), _sds((s, 2 * w), BF16)) + tuple(_sds(p.shape, F32) for p in params)
    return _pcall(
        body, name=name, out_shape=outs, grid=(1,),
        in_specs=[blk(0), blk(1), blk(2), blk(nblk - 2), blk(nblk - 1), blk(0), blk(nyb - 1)] + [whole(p) for p in params],
        out_specs=tuple(whole(o) for o in outs),
        scratch_shapes=[pltpu.VMEM((s + PAD_SHORT, w), F32), pltpu.VMEM((s + PAD_LONG, w), F32),
                        pltpu.VMEM((s + PAD_SHORT, w), F32), pltpu.VMEM((s + PAD_LONG, w), F32)], comm=comm,
    )(z, z, z, z, z, dy, dy, *params)


def _ffn_act_fwd(up, wf, name, comm=None):
    s, f2 = up.shape
    f = f2 // 2
    tc = 256
    nb = f // tc
    rc = ROW_CHUNK

    def body(g_ref, v_ref, wg_ref, wv_ref, o_ref, gpad, vpad):
        gpad[pl.ds(0, PAD_SHORT), :] = jnp.zeros((PAD_SHORT, tc), F32)
        vpad[pl.ds(0, PAD_SHORT), :] = jnp.zeros((PAD_SHORT, tc), F32)

        def chunk(i, carry):
            base = pl.multiple_of(i * rc, rc)
            rows = pl.ds(base, rc)
            gpad[pl.ds(base + PAD_SHORT, rc), :] = g_ref[rows, :].astype(F32)
            vpad[pl.ds(base + PAD_SHORT, rc), :] = v_ref[rows, :].astype(F32)
            gc = _conv_taps(_causal_taps(gpad[pl.ds(base, rc + PAD_SHORT), :], FFN_CONV, PAD_SHORT, rc), wg_ref)
            vc = _conv_taps(_causal_taps(vpad[pl.ds(base, rc + PAD_SHORT), :], FFN_CONV, PAD_SHORT, rc), wv_ref)
            o_ref[rows, :] = (gc * _sigmoid(gc) * vc).astype(BF16)
            return carry

        lax.fori_loop(0, s // rc, chunk, 0)

    return _pcall(
        body, name=name, out_shape=_sds((s, f), BF16), grid=(nb,),
        in_specs=[pl.BlockSpec((s, tc), lambda j: (0, j)), pl.BlockSpec((s, tc), lambda j: (0, j + nb)),
                  pl.BlockSpec((FFN_CONV, tc), lambda j: (0, j)), pl.BlockSpec((FFN_CONV, tc), lambda j: (0, j + nb))],
        out_specs=pl.BlockSpec((s, tc), lambda j: (0, j)),
        scratch_shapes=[pltpu.VMEM((s + PAD_SHORT, tc), F32), pltpu.VMEM((s + PAD_SHORT, tc), F32)], comm=comm,
    )(up, up, wf, wf)


def _ffn_act_bwd(up, dact, wf, name, comm=None):
    s, f2 = up.shape
    f = f2 // 2
    tc = 256
    nb = f // tc
    rc = ROW_CHUNK

    def body(g_ref, v_ref, da_ref, wg_ref, wv_ref, act_ref, dg_ref, dv_ref, dwg_ref, dwv_ref, gpad, vpad, dgp, dvp):
        gpad[pl.ds(0, PAD_SHORT), :] = jnp.zeros((PAD_SHORT, tc), F32)
        vpad[pl.ds(0, PAD_SHORT), :] = jnp.zeros((PAD_SHORT, tc), F32)
        dgp[pl.ds(s, PAD_SHORT), :] = jnp.zeros((PAD_SHORT, tc), F32)
        dvp[pl.ds(s, PAD_SHORT), :] = jnp.zeros((PAD_SHORT, tc), F32)
        dwg_ref[...] = jnp.zeros((FFN_CONV, tc), F32)
        dwv_ref[...] = jnp.zeros((FFN_CONV, tc), F32)

        def first(i, carry):
            base = pl.multiple_of(i * rc, rc)
            rows = pl.ds(base, rc)
            gpad[pl.ds(base + PAD_SHORT, rc), :] = g_ref[rows, :].astype(F32)
            vpad[pl.ds(base + PAD_SHORT, rc), :] = v_ref[rows, :].astype(F32)
            taps_g = _causal_taps(gpad[pl.ds(base, rc + PAD_SHORT), :], FFN_CONV, PAD_SHORT, rc)
            taps_v = _causal_taps(vpad[pl.ds(base, rc + PAD_SHORT), :], FFN_CONV, PAD_SHORT, rc)
            gc = _conv_taps(taps_g, wg_ref)
            vc = _conv_taps(taps_v, wv_ref)
            sg = _sigmoid(gc)
            silu = gc * sg
            act_ref[rows, :] = (silu * vc).astype(BF16)
            da = da_ref[rows, :].astype(F32)
            dgc = da * vc * (sg * (1.0 + gc * (1.0 - sg)))
            dvc = da * silu
            dgp[rows, :] = dgc
            dvp[rows, :] = dvc
            _conv_wgrad(dwg_ref, dgc, taps_g)
            _conv_wgrad(dwv_ref, dvc, taps_v)
            return carry

        lax.fori_loop(0, s // rc, first, 0)

        def second(i, carry):
            base = pl.multiple_of(i * rc, rc)
            rows = pl.ds(base, rc)
            dg_ref[rows, :] = _conv_taps(_anticausal_taps(dgp[pl.ds(base, rc + PAD_SHORT), :], FFN_CONV, rc), wg_ref).astype(BF16)
            dv_ref[rows, :] = _conv_taps(_anticausal_taps(dvp[pl.ds(base, rc + PAD_SHORT), :], FFN_CONV, rc), wv_ref).astype(BF16)
            return carry

        lax.fori_loop(0, s // rc, second, 0)

    lo = pl.BlockSpec((s, tc), lambda j: (0, j))
    hi = pl.BlockSpec((s, tc), lambda j: (0, j + nb))
    wlo = pl.BlockSpec((FFN_CONV, tc), lambda j: (0, j))
    whi = pl.BlockSpec((FFN_CONV, tc), lambda j: (0, j + nb))
    act, dgate, dval, dwg, dwv = _pcall(
        body, name=name,
        out_shape=(_sds((s, f), BF16), _sds((s, f), BF16), _sds((s, f), BF16), _sds((FFN_CONV, f), F32), _sds((FFN_CONV, f), F32)),
        grid=(nb,), in_specs=[lo, hi, lo, wlo, whi], out_specs=(lo, lo, lo, wlo, wlo),
        scratch_shapes=[pltpu.VMEM((s + PAD_SHORT, tc), F32) for _ in range(4)], comm=comm,
    )(up, up, dact, wf, wf)
    return act, dgate, dval, jnp.concatenate([dwg, dwv], axis=1)


def _out_proj(yan, yb, ycn, gb, x, w_out, layer, g_next, name, comm=None):
    s, w = yan.shape
    wb = yb.shape[1]
    d = x.shape[1]
    tm = ROW_CHUNK

    def body(ya_ref, yb_ref, yc_ref, gb_ref, x_ref, w_ref, g_ref, y_ref, xm_ref, h_ref):
        ybv = yb_ref[...]
        y = jnp.concatenate([ya_ref[...], (ybv * _rstd(ybv) * gb_ref[...]).astype(BF16), yc_ref[...]], axis=1)
        y_ref[...] = y
        xm = x_ref[...] + _dot(y, w_ref[...])
        xm_ref[...] = xm
        h_ref[...] = (xm * _rstd(xm) * g_ref[...]).astype(BF16)

    def rows(width):
        return pl.BlockSpec((tm, width), lambda i: (i, 0))

    def vec(width):
        return pl.BlockSpec((1, width), lambda i: (0, 0))

    return _pcall(body, name=name, out_shape=(_sds((s, d), BF16), _sds((s, d), F32), _sds((s, d), BF16)), grid=(s // tm,),
                  in_specs=[rows(w), rows(wb), rows(w), vec(wb), rows(d), pl.BlockSpec((None, d, d), lambda i: (layer, 0, 0)), vec(d)],
                  out_specs=(rows(d), rows(d), rows(d)), comm=comm)(yan, yb, ycn, gb, x, w_out, g_next)


def _down_proj(act, w_down, layer, x_mid, g_next, name, comm=None):
    s, f = act.shape
    d = x_mid.shape[1]
    tm = ROW_CHUNK

    def body(a_ref, w_ref, x_ref, *refs):
        xo = x_ref[...] + _dot(a_ref[...], w_ref[...])
        refs[-2 if g_next is not None else -1][...] = xo
        if g_next is not None:
            refs[-1][...] = (xo * _rstd(xo) * refs[0][...]).astype(BF16)

    row = pl.BlockSpec((tm, d), lambda i: (i, 0))
    ins = [act, w_down, x_mid] + ([g_next] if g_next is not None else [])
    in_specs = [pl.BlockSpec((tm, f), lambda i: (i, 0)), pl.BlockSpec((None, f, d), lambda i: (layer, 0, 0)), row]
    in_specs += [pl.BlockSpec((1, d), lambda i: (0, 0))] if g_next is not None else []
    outs = (_sds((s, d), F32), _sds((s, d), BF16)) if g_next is not None else (_sds((s, d), F32),)
    res = _pcall(body, name=name, out_shape=outs, grid=(s // tm,), in_specs=in_specs, out_specs=tuple([row] * len(outs)),
                 comm=comm)(*ins)
    return (res[0], res[1]) if g_next is not None else (res[0], None)


def _proj_dx(pieces, w, layer, nt, x, g, dres, name, comm=None):
    s, d = x.shape
    tm = ROW_CHUNK
    widths = [p.shape[1] for p in pieces]

    def body(*refs):
        p_refs, (w_ref, x_ref, g_ref, dres_ref, dx_ref, dxb_ref, dg_ref) = refs[:len(pieces)], refs[len(pieces):]
        i = pl.program_id(0)
        dh, off = None, 0
        for p_ref, width in zip(p_refs, widths):
            part = _dot_nt(p_ref[...], w_ref[:, off:off + width]) if nt else _dot(p_ref[...], w_ref[off:off + width, :])
            dh = part if dh is None else dh + part
            off += width
        xv = x_ref[...]
        r = _rstd(xv)
        xh = xv * r
        gd = dh * g_ref[...]
        dx = dres_ref[...] + r * (gd - xh * jnp.mean(gd * xh, axis=-1, keepdims=True))
        dx_ref[...] = dx
        dxb_ref[...] = dx.astype(BF16)
        part = jnp.sum(dh * xh, axis=0, keepdims=True)

        @pl.when(i == 0)
        def _():
            dg_ref[...] = part

        @pl.when(i > 0)
        def _():
            dg_ref[...] += part

    row = pl.BlockSpec((tm, d), lambda i: (i, 0))
    vec = pl.BlockSpec((1, d), lambda i: (0, 0))
    w_spec = pl.BlockSpec((None,) + w.shape[1:], lambda i: (layer, 0, 0))
    return _pcall(body, name=name, out_shape=(_sds((s, d), F32), _sds((s, d), BF16), _sds((1, d), F32)), grid=(s // tm,),
                  in_specs=[pl.BlockSpec((tm, width), lambda i: (i, 0)) for width in widths] + [w_spec, row, vec, row],
                  out_specs=(row, row, vec), comm=comm)(*pieces, w, x, g, dres)


def _yb_norm_bwd(yb, dy, gb, name, comm=None):
    s, wb = yb.shape
    w = wb // 2
    heads = wb // D_HEAD
    tm = ROW_CHUNK

    def body(yb_ref, d1_ref, d2_ref, g_ref, dyb_ref, dl_ref, dg_ref):
        i = pl.program_id(0)
        y = yb_ref[...]
        dyn = jnp.concatenate([d1_ref[...], d2_ref[...]], axis=1)
        r = _rstd(y)
        yh = y * r
        gd = dyn * g_ref[...]
        dyb = r * (gd - yh * jnp.mean(gd * yh, axis=-1, keepdims=True))
        dyb_ref[...] = dyb
        part = jnp.sum(dyn * yh, axis=0, keepdims=True)
        prod = dyb * y
        even = lax.broadcasted_iota(I32, (tm, LANES), 1) < D_HEAD
        for p in range(heads // 2):
            blk = prod[:, p * LANES:(p + 1) * LANES]
            ev = jnp.sum(jnp.where(even, blk, 0.0), axis=1, keepdims=True)
            od = jnp.sum(jnp.where(even, 0.0, blk), axis=1, keepdims=True)
            dl_ref[2 * p] = jnp.broadcast_to(ev, (tm, LANES))
            dl_ref[2 * p + 1] = jnp.broadcast_to(od, (tm, LANES))

        @pl.when(i == 0)
        def _():
            dg_ref[...] = part

        @pl.when(i > 0)
        def _():
            dg_ref[...] += part

    return _pcall(
        body, name=name, out_shape=(_sds((s, wb), F32), _sds((heads, s, LANES), F32), _sds((1, wb), F32)),
        grid=(s // tm,),
        in_specs=[pl.BlockSpec((tm, wb), lambda i: (i, 0)), pl.BlockSpec((tm, w), lambda i: (i, 1)),
                  pl.BlockSpec((tm, w), lambda i: (i, 2)), pl.BlockSpec((1, wb), lambda i: (0, 0))],
        out_specs=(pl.BlockSpec((tm, wb), lambda i: (i, 0)), pl.BlockSpec((heads, tm, LANES), lambda i: (0, i, 0)),
                   pl.BlockSpec((1, wb), lambda i: (0, 0))), comm=comm,
    )(yb, dy, dy, gb)


def _t5_bucket_table():
    max_exact = NUM_BUCKETS // 2
    out = np.full((len(DILATED_BRANCHES), BLK, 2 * BLK), -1, np.int32)
    rel = np.arange(BLK)[:, None] - np.arange(2 * BLK)[None, :] + BLK
    for b, (window, dilation) in enumerate(DILATED_BRANCHES):
        n_keys = window // dilation
        dist = np.maximum(rel, 0) * dilation
        d_f = np.maximum(dist, 1).astype(np.float32)
        large = max_exact + (np.log(d_f / np.float32(max_exact)) / np.float32(math.log(MAX_DISTANCE / max_exact))
                             * np.float32(NUM_BUCKETS - max_exact)).astype(np.int32)
        large = np.minimum(large, NUM_BUCKETS - 1)
        bucket = np.where(dist < max_exact, dist, large)
        out[b] = np.where((rel >= 0) & (rel <= n_keys), bucket, -1)
    return out


def _bias_tiles(rel_bias, buckets, name):
    nbk, heads = rel_bias.shape
    nbr = buckets.shape[0]

    def body(rb_ref, bk_ref, o_ref):
        for br in range(nbr):
            bk = bk_ref[br]
            tiles = [jnp.full((BLK, 2 * BLK), NEG, F32) for _ in range(heads)]
            for b in range(nbk):
                hit = bk == b
                tiles = [jnp.where(hit, rb_ref[b, h], tiles[h]) for h in range(heads)]
            for h in range(heads):
                o_ref[br, h] = tiles[h]

    return _pcall(body, name=name, out_shape=_sds((nbr, heads, BLK, 2 * BLK), F32), grid=(1,),
                  in_specs=[pl.BlockSpec(memory_space=pltpu.SMEM), pl.BlockSpec(buckets.shape, lambda i: (0, 0, 0))],
                  out_specs=pl.BlockSpec((nbr, heads, BLK, 2 * BLK), lambda i: (0, 0, 0, 0)))(rel_bias, buckets)


def _bias_grad(dtiles, buckets, nbk, name):
    nbr, heads = dtiles.shape[:2]

    def body(dt_ref, bk_ref, o_ref):
        row = lax.broadcasted_iota(I32, (nbk, LANES), 0)
        col = lax.broadcasted_iota(I32, (nbk, LANES), 1)
        out = jnp.zeros((nbk, LANES), F32)
        for h in range(heads):
            for b in range(nbk):
                tot = jnp.zeros((), F32)
                for br in range(nbr):
                    tot = tot + jnp.sum(jnp.where(bk_ref[br] == b, dt_ref[br, h], 0.0))
                out = jnp.where((row == b) & (col == h), tot, out)
        o_ref[...] = out

    return _pcall(body, name=name, out_shape=_sds((nbk, LANES), F32), grid=(1,),
                  in_specs=[pl.BlockSpec(dtiles.shape, lambda i: (0, 0, 0, 0)), pl.BlockSpec(buckets.shape, lambda i: (0, 0, 0))],
                  out_specs=pl.BlockSpec((nbk, LANES), lambda i: (0, 0)))(dtiles, buckets)


def _largest_divisor(n, cap):
    return max(g for g in range(1, cap + 1) if n % g == 0)


def _attn_blocks(s, visit, group):
    for br, (window, d) in enumerate(DILATED_BRANCHES):
        n_blk = (s // d) // BLK
        span = BLK * d
        g1 = _largest_divisor(d, group)

        def firsts(t, carry, br=br, d=d, g1=g1):
            for j in range(g1):
                visit(br, d, t * g1 + j, False)
            return carry

        lax.fori_loop(0, d // g1, firsts, 0)
        if n_blk > 1:
            total = d * (n_blk - 1)
            g2 = _largest_divisor(total, group)

            def rest(t, carry, br=br, d=d, n_blk=n_blk, span=span, g2=g2):
                for j in range(g2):
                    idx = t * g2 + j
                    visit(br, d, idx // (n_blk - 1) + (1 + idx % (n_blk - 1)) * span, True)
                return carry

            lax.fori_loop(0, total // g2, rest, 0)


def _rows(start, size, d):
    return pl.ds(pl.multiple_of(start, BLK), size) if d == 1 else pl.ds(start, size, stride=d)


def _attn_fwd(z, btiles, col0, name, comm=None):
    s = z.shape[0]
    nbr, heads = btiles.shape[:2]
    pairs = heads // 2
    scale = D_HEAD ** -0.5
    rc = ROW_CHUNK

    def body(q_ref, k_ref, v_ref, bt_ref, yb_ref, lse_ref, acc_ref, m_ref, l_ref):
        even = lax.broadcasted_iota(I32, (BLK, LANES), 1) < D_HEAD
        even2 = lax.broadcasted_iota(I32, (2 * BLK, LANES), 1) < D_HEAD

        def visit(br, d, start, prev):
            kw = 2 * BLK if prev else BLK
            rows_q = _rows(start, BLK, d)
            rows_k = _rows(start - BLK * d, kw, d) if prev else rows_q
            qb = q_ref[rows_q, :]
            kb = k_ref[rows_k, :].astype(BF16)
            vw = v_ref[rows_k, :]
            ev_k = even2 if prev else even
            qm = jnp.concatenate([jnp.where(even, qb, 0.0), jnp.where(even, 0.0, qb)], axis=0).astype(BF16)
            bias = [bt_ref[br, e] if prev else bt_ref[br, e, :, BLK:] for e in range(2)]
            sc = _dot_nt(qm, kb) * scale + jnp.concatenate(bias, axis=0)
            m = jnp.max(sc, axis=1, keepdims=True)
            p = jnp.exp(sc - m)
            l = jnp.sum(p, axis=1, keepdims=True)
            pb = p.astype(BF16)
            vm = jnp.concatenate([jnp.where(ev_k, vw, 0.0), jnp.where(ev_k, 0.0, vw)], axis=0).astype(BF16)
            acc_ref.at[br][rows_q, :] = _dot(jnp.concatenate([pb[:BLK], pb[BLK:]], axis=1), vm)
            for e in range(2):
                m_ref.at[br, e][rows_q, :] = jnp.broadcast_to(m[e * BLK:(e + 1) * BLK], (BLK, LANES))
                l_ref.at[br, e][rows_q, :] = jnp.broadcast_to(l[e * BLK:(e + 1) * BLK], (BLK, LANES))

        _attn_blocks(s, visit, ATTN_GROUP_FWD)

        ev_c = lax.broadcasted_iota(I32, (rc, LANES), 1) < D_HEAD

        def merge(i, carry):
            rows = pl.ds(pl.multiple_of(i * rc, rc), rc)
            wts, dens = [], []
            for e in range(2):
                ms = [m_ref[br, e, rows, :] for br in range(nbr)]
                top = functools.reduce(jnp.maximum, ms)
                w = [jnp.exp(mb - top) for mb in ms]
                den = functools.reduce(lambda a, b: a + b, [w[br] * l_ref[br, e, rows, :] for br in range(nbr)])
                lse_ref[e, rows, :] = top + jnp.log(den)
                wts.append(w)
                dens.append(den)
            num = functools.reduce(lambda a, b: a + b,
                                   [jnp.where(ev_c, wts[0][br], wts[1][br]) * acc_ref[br, rows, :] for br in range(nbr)])
            yb_ref[rows, :] = num / jnp.where(ev_c, dens[0], dens[1])
            return carry

        lax.fori_loop(0, s // rc, merge, 0)

    def zcol(j):
        return pl.BlockSpec((s, LANES), lambda p, j=j: (0, col0 + j + p))

    return _pcall(
        body, name=name, out_shape=(_sds((s, pairs * LANES), F32), _sds((heads, s, LANES), F32)), grid=(pairs,),
        in_specs=[zcol(0), zcol(pairs), zcol(2 * pairs), pl.BlockSpec((nbr, 2, BLK, 2 * BLK), lambda p: (0, p, 0, 0))],
        out_specs=(pl.BlockSpec((s, LANES), lambda p: (0, p)), pl.BlockSpec((2, s, LANES), lambda p: (p, 0, 0))),
        scratch_shapes=[pltpu.VMEM((nbr, s, LANES), F32), pltpu.VMEM((nbr, 2, s, LANES), F32), pltpu.VMEM((nbr, 2, s, LANES), F32)],
        comm=comm,
    )(z, z, z, btiles)


def _attn_bwd(z, btiles, dyb, lse, delta, dbias_in, col0, name, comm=None):
    s = z.shape[0]
    nbr, heads = btiles.shape[:2]
    pairs = heads // 2
    scale = D_HEAD ** -0.5

    def body(q_ref, k_ref, v_ref, bt_ref, dy_ref, lse_ref, dl_ref, dbi_ref,
             dq_ref, dk_ref, dv_ref, db_ref, dqa, dka, dva):
        even = lax.broadcasted_iota(I32, (BLK, LANES), 1) < D_HEAD
        even2 = lax.broadcasted_iota(I32, (2 * BLK, LANES), 1) < D_HEAD
        for ref in (dqa, dka, dva):
            ref[...] = jnp.zeros((s, LANES), F32)
        db_ref[...] = dbi_ref[...]

        def visit(br, d, start, prev):
            kw = 2 * BLK if prev else BLK
            rows_q = _rows(start, BLK, d)
            rows_k = _rows(start - BLK * d, kw, d) if prev else rows_q
            qb = q_ref[rows_q, :]
            dyv = dy_ref[rows_q, :]
            kwin = k_ref[rows_k, :]
            kb = kwin.astype(BF16)
            vb = v_ref[rows_k, :].astype(BF16)
            ev_k = even2 if prev else even
            qm = jnp.concatenate([jnp.where(even, qb, 0.0), jnp.where(even, 0.0, qb)], axis=0).astype(BF16)
            dym = jnp.concatenate([jnp.where(even, dyv, 0.0), jnp.where(even, 0.0, dyv)], axis=0).astype(BF16)
            bias = [bt_ref[br, e] if prev else bt_ref[br, e, :, BLK:] for e in range(2)]
            sc = _dot_nt(qm, kb) * scale + jnp.concatenate(bias, axis=0)
            lt = jnp.concatenate([lse_ref.at[e][rows_q, :] for e in range(2)], axis=0)
            dt = jnp.concatenate([dl_ref.at[e][rows_q, :] for e in range(2)], axis=0)
            if prev:
                lt = jnp.concatenate([lt, lt], axis=1)
                dt = jnp.concatenate([dt, dt], axis=1)
            p = jnp.exp(sc - lt)
            ds = p * (_dot_nt(dym, vb) - dt)
            for e in range(2):
                if prev:
                    db_ref[br, e] += ds[e * BLK:(e + 1) * BLK]
                else:
                    db_ref[br, e, :, BLK:] += ds[e * BLK:(e + 1) * BLK]
            dsb = ds.astype(BF16)
            km = jnp.concatenate([jnp.where(ev_k, kwin, 0.0), jnp.where(ev_k, 0.0, kwin)], axis=0).astype(BF16)
            dqa[rows_q, :] += _dot(jnp.concatenate([dsb[:BLK], dsb[BLK:]], axis=1), km) * scale
            dka[rows_k, :] += _dot_tn(dsb, qm) * scale
            dva[rows_k, :] += _dot_tn(p.astype(BF16), dym)

        _attn_blocks(s, visit, ATTN_GROUP_BWD)
        dq_ref[...] = dqa[...].astype(BF16)
        dk_ref[...] = dka[...].astype(BF16)
        dv_ref[...] = dva[...].astype(BF16)

    def zcol(j):
        return pl.BlockSpec((s, LANES), lambda p, j=j: (0, col0 + j + p))

    col = pl.BlockSpec((s, LANES), lambda p: (0, p))
    stat = pl.BlockSpec((2, s, LANES), lambda p: (p, 0, 0))
    tile = pl.BlockSpec((nbr, 2, BLK, 2 * BLK), lambda p: (0, p, 0, 0))
    wide = _sds((s, pairs * LANES), BF16)
    return _pcall(
        body, name=name, out_shape=(wide, wide, wide, _sds(btiles.shape, F32)), grid=(pairs,),
        in_specs=[zcol(0), zcol(pairs), zcol(2 * pairs), tile, col, stat, stat, tile],
        out_specs=(col, col, col, tile),
        scratch_shapes=[pltpu.VMEM((s, LANES), F32) for _ in range(3)], comm=comm,
    )(z, z, z, btiles, dyb, lse, delta, dbias_in)


def _row(v):
    return v.reshape(1, -1)


class _Rides:
    def __init__(self):
        self.table, self.grads = {}, {}

    def add(self, name, build):
        self.table.setdefault(name, []).append(build)

    def get(self, name):
        comm = None
        for build in self.table.get(name, ()):
            comm = build() if comm is None else _both(comm, build())
        return comm

    def ready(self, key, g):
        self.grads[key] = g


class _LocalSchedule:
    def __init__(self):
        self.big = {}

    def fwd_comms(self, l):
        return _Rides()

    def bwd_comms(self, l):
        return _Rides()

    def after_bwd(self, l, grads):
        self.big[l] = grads


def _layer_fwd(l, x, h, wts, prm, btiles, comms):
    d = x.shape[1]
    wq = d // 4
    depth = prm["norm_mix_g"].shape[0]
    gout = prm["out_norm_g"][l]
    z = _mm_n(h, wts["in_t"], l, nt=True, tn=256, out_dtype=F32, name="in_proj", comm=comms.get("in_proj"))
    yan, ycn = _mix_fwd(z, prm["conv_a_w"][l], prm["conv_c_w"][l], _row(prm["conv_c_b"][l]), _row(prm["ln_c_g"][l]),
                        _row(prm["ln_c_b"][l]), _row(gout[:wq]), _row(gout[3 * wq:]), "mix_fwd")
    yb, lse = _attn_fwd(z, btiles, 3 * wq // LANES, "attn_fwd", comm=comms.get("attn_fwd"))
    y, x_mid, h2 = _out_proj(yan, yb, ycn, _row(gout[wq:3 * wq]), x, wts["out"], l, _row(prm["norm_ffn_g"][l]),
                             "out_proj", comm=comms.get("out_proj"))
    up = _mm_n(h2, wts["up"], l, nt=False, tn=512, out_dtype=BF16, name="up_proj", comm=comms.get("up_proj"))
    act = _ffn_act_fwd(up, prm["conv_f_w"][l], "ffn_act_fwd", comm=comms.get("ffn_act_fwd"))
    g_next = _row(prm["norm_mix_g"][l + 1]) if l + 1 < depth else None
    x_out, h_next = _down_proj(act, wts["down"], l, x_mid, g_next, "down_proj", comm=comms.get("down_proj"))
    return x_out, h_next, (x, h, z, yb, lse, y, x_mid, h2, up)


def _layer_bwd(l, dxo, dxo_b, saved, wts, prm, btiles, dbias, comms):
    x, h, z, yb, lse, y, x_mid, h2, up = saved
    d = x.shape[1]
    wq = d // 4
    gout = prm["out_norm_g"][l]
    dact = _mm_n(dxo_b, wts["down"], l, nt=True, tn=256, out_dtype=BF16, name="down_proj_dx", comm=comms.get("down_proj_dx"))
    act, dgate, dval, dwf = _ffn_act_bwd(up, dact, prm["conv_f_w"][l], "ffn_act_bwd", comm=comms.get("ffn_act_bwd"))
    g_down = _mm_tn(act, dxo_b, t=256, name="down_proj_dw")
    comms.ready("down", g_down)
    dxm, dxm_b, dg_ffn = _proj_dx([dgate, dval], wts["up"], l, True, x_mid, _row(prm["norm_ffn_g"][l]), dxo, "up_proj_dx",
                                  comm=comms.get("up_proj_dx"))
    g_up = _mm_tn2(h2, dgate, dval, t=256, name="up_proj_dw", comm=comms.get("up_proj_dw"))
    comms.ready("up", g_up)
    dy = _mm_n(dxm_b, wts["out"], l, nt=True, tn=256, out_dtype=F32, name="out_proj_dx")
    g_out = _mm_tn(y, dxm_b, t=256, name="out_proj_dw")
    comms.ready("out", g_out)
    dza, dzc, dwa, dwc, dcb, dlg, dlb, dga, dgc = _mix_bwd(
        z, dy, prm["conv_a_w"][l], prm["conv_c_w"][l], _row(prm["conv_c_b"][l]), _row(prm["ln_c_g"][l]),
        _row(prm["ln_c_b"][l]), _row(gout[:wq]), _row(gout[3 * wq:]), "mix_bwd", comm=comms.get("mix_bwd"))
    dyb, delta, dgb = _yb_norm_bwd(yb, dy, _row(gout[wq:3 * wq]), "yb_norm_bwd", comm=comms.get("yb_norm_bwd"))
    dq, dk, dv, dbias = _attn_bwd(z, btiles, dyb, lse, delta, dbias, 3 * wq // LANES, "attn_bwd",
                                  comm=comms.get("attn_bwd"))
    dz = [dza, dq, dk, dv, dzc]
    dx, dx_b, dg_mix = _proj_dx(dz, wts["in_t"], l, False, x, _row(prm["norm_mix_g"][l]), dxm, "in_proj_dx",
                                comm=comms.get("in_proj_dx"))
    g_in_t = _mm_tn_pieces(dz, h, t=256, name="in_proj_dw")
    big = {"in_t": g_in_t, "out": g_out, "up": g_up, "down": g_down}
    small = {"norm_mix_g": dg_mix[0], "conv_a_w": dwa, "conv_c_w": dwc, "conv_c_b": dcb[0], "ln_c_g": dlg[0],
             "ln_c_b": dlb[0], "out_norm_g": jnp.concatenate([dga[0], dgb[0], dgc[0]]), "norm_ffn_g": dg_ffn[0],
             "conv_f_w": dwf}
    return dx, dx_b, big, small, dbias


def _local_step(x, tgt, wts, prm, sched):
    depth = prm["norm_mix_g"].shape[0]
    buckets = jnp.asarray(_t5_bucket_table())
    btiles = _bias_tiles(prm["rel_bias"], buckets, "bias_tiles")
    saved = []
    h = _rms_fwd(x, _row(prm["norm_mix_g"][0]), "rms_mix_fwd")
    for l in range(depth):
        x, h, sv = _layer_fwd(l, x, h, wts, prm, btiles, sched.fwd_comms(l))
        saved.append(sv)
    loss, dx, dx_b, dg_final = _final_loss(x, _row(prm["final_g"]), tgt, "final_loss")
    dbias = jnp.zeros(btiles.shape, F32)
    small = [None] * depth
    for l in reversed(range(depth)):
        dx, dx_b, grads, small[l], dbias = _layer_bwd(l, dx, dx_b, saved[l], wts, prm, btiles, dbias, sched.bwd_comms(l))
        sched.after_bwd(l, grads)
    nbk, heads = prm["rel_bias"].shape
    d_rel = _bias_grad(dbias, buckets, nbk, "bias_grad")[:, :heads]
    return loss, dx, small, d_rel, dg_final[0]


BIG = ("in_t", "out", "up", "down")
COL_SHARDED = ("up",)
N_CHIPS = 4
N_DEV = 8
BF16_ROWS = 16


def _me():
    return lax.axis_index("x"), lax.axis_index("y"), lax.axis_index("c")


def _chip_of(x, y):
    return 2 * x + y


def _other_chips(x, y):
    return ((1 - x, y), (x, 1 - y), (1 - x, 1 - y))


def _remote(src, dst, send_sem, recv_sem, device):
    return pltpu.make_async_remote_copy(src_ref=src, dst_ref=dst, send_sem=send_sem, recv_sem=recv_sem,
                                        device_id=device, device_id_type=MESH)


ALL_FLIPS = (0, 1, 2)


def _ag_comm(wts, layer, ici_keys, fwd_keys):
    flips = {(k if isinstance(k, str) else k[0]): (ALL_FLIPS if isinstance(k, str) else k[1]) for k in ici_keys}
    keys = tuple(k for k in BIG if k in flips or k in fwd_keys)

    def geo(k):
        _, rows, cols = wts[k].shape
        return (rows, cols // N_CHIPS) if k in COL_SHARDED else (rows // N_CHIPS, cols)

    def copies(refs, sems):
        g = dict(zip(keys, refs))
        isend, irecv, dsend, drecv = sems
        x, y, c = _me()
        mine = _chip_of(x, y)

        def region(k, chip, half):
            r, cc = geo(k)
            h = r // 2
            if k in COL_SHARDED:
                return g[k].at[layer, pl.ds(pl.multiple_of(half * h, BF16_ROWS), h), pl.ds(pl.multiple_of(chip * cc, LANES), cc)]
            return g[k].at[layer, pl.ds(pl.multiple_of(chip * r + half * h, BF16_ROWS), h), :]

        def ici(k, f, landing):
            chip = _other_chips(x, y)[f]
            where = region(k, _chip_of(*chip) if landing else mine, c)
            i = keys.index(k)
            return _remote(where, where, isend.at[i, f], irecv.at[i, f], (*chip, c))

        def fwd(k, f, landing):
            chip = _other_chips(x, y)[f]
            where = region(k, _chip_of(*chip), 1 - c if landing else c)
            i = keys.index(k)
            return _remote(where, where, dsend.at[i, f], drecv.at[i, f], (x, y, 1 - c))

        return ici, fwd

    def start(ins, outs, sems):
        ici, fwd = copies(outs, sems)
        for k in keys:
            for f in flips.get(k, ALL_FLIPS):
                if k in flips:
                    ici(k, f, False).start()
                else:
                    fwd(k, f, False).start()

    def finish(ins, outs, sems):
        ici, fwd = copies(outs, sems)
        for k in keys:
            for f in flips.get(k, ()):
                ici(k, f, True).wait_recv()
                if k in fwd_keys:
                    fwd(k, f, False).start()
        for k in keys:
            for f in flips.get(k, ALL_FLIPS):
                if k in fwd_keys:
                    fwd(k, f, True).wait_recv()
                    fwd(k, f, False).wait_send()
                if k in flips:
                    ici(k, f, False).wait_send()

    def done(res):
        wts.update(zip(keys, res))

    n = len(keys)
    return _Comm([wts[k] for k in keys], [_sds(wts[k].shape, BF16) for k in keys], {i: i for i in range(n)},
                 [pltpu.SemaphoreType.DMA((n, 3)) for _ in range(4)], start, finish, done)


def _small_gather_comm(slab, store):
    def copies(ins, outs, sems):
        send, recv, lsem = sems
        x, y, c = _me()
        mine = _chip_of(x, y)
        own = pltpu.make_async_copy(ins[0], outs[0].at[mine], lsem)
        pairs = []
        for f, chip in enumerate(_other_chips(x, y)):
            out = _remote(ins[0], outs[0].at[mine], send.at[f], recv.at[f], (*chip, c))
            land = _remote(ins[0], outs[0].at[_chip_of(*chip)], send.at[f], recv.at[f], (*chip, c))
            pairs.append((out, land))
        return own, pairs

    def start(ins, outs, sems):
        own, pairs = copies(ins, outs, sems)
        own.start()
        for out, _ in pairs:
            out.start()

    def finish(ins, outs, sems):
        own, pairs = copies(ins, outs, sems)
        for out, land in pairs:
            land.wait_recv()
            out.wait_send()
        own.wait()

    def done(res):
        store["small"] = res[0]

    return _Comm([slab], [_sds((N_CHIPS,) + slab.shape, F32)], {},
                 [pltpu.SemaphoreType.DMA((3,)), pltpu.SemaphoreType.DMA((3,)), pltpu.SemaphoreType.DMA], start, finish, done)


def _piece_geo(g):
    geo = {}
    for k in g:
        rows, cols = g[k].shape
        geo[k] = (rows // 2, cols // N_CHIPS) if k in COL_SHARDED else (rows // (2 * N_CHIPS), cols)
    return geo


def _swap_comm(g, keys, done):
    geo = _piece_geo(g)
    n_copies = sum(N_CHIPS if k in COL_SHARDED else 1 for k in keys)

    def copies(ins, outs, sems):
        g_refs, t_refs = dict(zip(keys, ins)), dict(zip(keys, outs))
        send, recv = sems
        x, y, c = _me()
        pairs = []
        for k in keys:
            h, cc = geo[k]
            if k in COL_SHARDED:
                rows = pl.ds(pl.multiple_of((1 - c) * h, BF16_ROWS), h)
                pairs += [(g_refs[k].at[rows, pl.ds(j * cc, cc)], t_refs[k].at[j]) for j in range(N_CHIPS)]
            else:
                pairs.append((g_refs[k].at[:, 1 - c], t_refs[k]))
        return [_remote(src, dst, send.at[i], recv.at[i], (x, y, 1 - c)) for i, (src, dst) in enumerate(pairs)]

    def start(ins, outs, sems):
        for cp in copies(ins, outs, sems):
            cp.start()

    def finish(ins, outs, sems):
        for cp in copies(ins, outs, sems):
            cp.wait()

    ins = [g[k] if k in COL_SHARDED else g[k].reshape(N_CHIPS, 2, geo[k][0], geo[k][1]) for k in keys]
    return _Comm(ins, [_sds((N_CHIPS,) + geo[k], BF16) for k in keys], {},
                 [pltpu.SemaphoreType.DMA((n_copies,)) for _ in range(2)], start, finish,
                 lambda res: done(dict(zip(keys, res))))


def _pair_sum(g, theirs, c_arr, keys):
    geo = _piece_geo(g)

    def body(c_ref, *refs):
        nk = len(keys)
        for i in range(nk):
            refs[2 * nk + i][...] = (refs[i][...].astype(F32) + refs[nk + i][...].astype(F32)).astype(BF16)

    in_specs, ins = [], []
    for k in keys:
        h, cc = geo[k]
        if k in COL_SHARDED:
            in_specs.append(pl.BlockSpec((h, cc), lambda j, c_ref: (c_ref[0], j)))
            ins.append(g[k])
        else:
            in_specs.append(pl.BlockSpec((None, h, cc), lambda j, c_ref: (2 * j + c_ref[0], 0, 0)))
            ins.append(g[k].reshape(2 * N_CHIPS, h, cc))
    slab = [pl.BlockSpec((None,) + geo[k], lambda j, c_ref: (j, 0, 0)) for k in keys]
    res = _pcall(body, name="rs_pair_sum", out_shape=tuple(_sds((N_CHIPS,) + geo[k], BF16) for k in keys), grid=(N_CHIPS,),
                 in_specs=in_specs + slab, out_specs=tuple(slab), prefetch=1)(c_arr, *ins, *[theirs[k] for k in keys])
    return dict(zip(keys, res))


def _rs_comm(p, keys, store):
    def copies(ins, outs, sems):
        send, recv = sems
        x, y, c = _me()
        return [_remote(ins[i].at[_chip_of(*chip)], outs[i].at[f], send.at[i, f], recv.at[i, f], (*chip, c))
                for i in range(len(keys)) for f, chip in enumerate(_other_chips(x, y))]

    def start(ins, outs, sems):
        for cp in copies(ins, outs, sems):
            cp.start()

    def finish(ins, outs, sems):
        for cp in copies(ins, outs, sems):
            cp.wait()

    def done(res):
        store.update(zip(keys, res))

    return _Comm([p[k] for k in keys], [_sds((3,) + p[k].shape[1:], BF16) for k in keys], {},
                 [pltpu.SemaphoreType.DMA((len(keys), 3)) for _ in range(2)], start, finish, done)


def _quad_sum(p, b, where, l, full):
    parts = 2
    nk = len(BIG)

    def body(where_ref, *refs):
        for i in range(nk):
            acc = refs[i][...].astype(F32)
            for f in range(3):
                acc = acc + refs[nk + 3 * i + f][...].astype(F32)
            refs[5 * nk + i][...] = acc

    own, recv, outs = [], [], []
    for k in BIG:
        h, cc = p[k].shape[1:]
        th = h // parts
        own.append(pl.BlockSpec((None, th, cc), lambda i, w_ref: (w_ref[0], i, 0)))
        recv += [pl.BlockSpec((None, th, cc), lambda i, w_ref, f=f: (f, i, 0)) for f in range(3)]
        outs.append(pl.BlockSpec((None, None, th, cc), lambda i, w_ref: (l, w_ref[1], i, 0)))
    args = [p[k] for k in BIG] + [b[k] for k in BIG for _ in range(3)] + [full[k] for k in BIG]
    res = _pcall(body, name="rs_quad_sum", out_shape=tuple(_sds(full[k].shape, F32) for k in BIG), grid=(parts,),
                 in_specs=own + recv + [ANY] * nk, out_specs=tuple(outs), prefetch=1,
                 aliases={1 + 4 * nk + i: i for i in range(nk)})(where, *args)
    return dict(zip(BIG, res))


def _share_comm(layers, full, done):
    nk = len(BIG)

    def copies(outs, sems, landing):
        send, recv = sems
        x, y, c = _me()
        half = 1 - c if landing else c
        return [_remote(outs[i].at[l, half], outs[i].at[l, half], send.at[i, j], recv.at[i, j], (x, y, 1 - c))
                for i in range(nk) for j, l in enumerate(layers)]

    def start(ins, outs, sems):
        for cp in copies(outs, sems, False):
            cp.start()

    def finish(ins, outs, sems):
        for cp in copies(outs, sems, True):
            cp.wait_recv()
        for cp in copies(outs, sems, False):
            cp.wait_send()

    return _Comm([full[k] for k in BIG], [_sds(full[k].shape, F32) for k in BIG], {i: i for i in range(nk)},
                 [pltpu.SemaphoreType.DMA((nk, len(layers))) for _ in range(2)], start, finish,
                 lambda res: done(dict(zip(BIG, res))))


def _gather_comm(slab, done):
    def copies(ins, outs, sems, landing):
        send, recv = sems
        x, y, c = _me()
        me = 4 * x + 2 * y + c
        out = []
        for mask in range(1, N_DEV):
            peer = (x ^ (mask >> 2), y ^ ((mask >> 1) & 1), c ^ (mask & 1))
            slot = 4 * peer[0] + 2 * peer[1] + peer[2] if landing else me
            out.append(_remote(ins[0], outs[0].at[slot], send.at[mask - 1], recv.at[mask - 1], peer))
        return out

    def start(ins, outs, sems):
        for cp in copies(ins, outs, sems, False):
            cp.start()

    def finish(ins, outs, sems):
        for cp in copies(ins, outs, sems, True):
            cp.wait_recv()
        for cp in copies(ins, outs, sems, False):
            cp.wait_send()

    return _Comm([slab], [_sds((N_DEV,) + slab.shape, F32)], {},
                 [pltpu.SemaphoreType.DMA((N_DEV - 1,)), pltpu.SemaphoreType.DMA((N_DEV - 1,))], start, finish,
                 lambda res: done(res[0]))


def _sum_slabs(slabs, own, me):
    n, r, lanes = slabs.shape
    tr = r // 2

    def body(me_ref, s_ref, own_ref, o_ref):
        o_ref[...] = jnp.zeros((tr, lanes), F32)
        for i in range(n):
            @pl.when(me_ref[0] == i)
            def _():
                o_ref[...] += own_ref[...]

            @pl.when(me_ref[0] != i)
            def _():
                o_ref[...] += s_ref[i]

    return _pcall(body, name="sum_partials", out_shape=_sds((r, lanes), F32), grid=(2,),
                  in_specs=[pl.BlockSpec((n, tr, lanes), lambda i, me_ref: (0, i, 0)),
                            pl.BlockSpec((tr, lanes), lambda i, me_ref: (i, 0))],
                  out_specs=pl.BlockSpec((tr, lanes), lambda i, me_ref: (i, 0)), prefetch=1)(me, slabs, own)


def _cast_into_gathered(w, chip, by_cols, name):
    l, r, c = w.shape

    def body(chip_ref, w_ref, o_ref):
        o_ref[...] = w_ref[...].astype(BF16)

    if by_cols:
        shape, out = (l, r, N_CHIPS * c), pl.BlockSpec((None, r, c), lambda i, chip_ref: (i, 0, chip_ref[0]))
    else:
        shape, out = (l, N_CHIPS * r, c), pl.BlockSpec((None, r, c), lambda i, chip_ref: (i, chip_ref[0], 0))
    return _pcall(body, name=name, out_shape=_sds(shape, BF16), grid=(l,),
                  in_specs=[pl.BlockSpec((None, r, c), lambda i, chip_ref: (i, 0, 0))], out_specs=out, prefetch=1)(chip, w)


def _adamw_math(w, g, m, v):
    mn = ADAM_B1 * m + (1.0 - ADAM_B1) * g
    vn = ADAM_B2 * v + (1.0 - ADAM_B2) * (g * g)
    m_hat = mn / (1.0 - ADAM_B1 ** ADAM_STEP)
    v_hat = vn / (1.0 - ADAM_B2 ** ADAM_STEP)
    return -ADAM_LR * (m_hat / (jnp.sqrt(v_hat) + ADAM_EPS) + ADAM_WD * w), mn, vn


def _adamw(w, g, m, v, name, tr):
    r, c = w.shape

    def body(w_ref, g_ref, m_ref, v_ref, go_ref, d_ref, mo_ref, vo_ref):
        gv = g_ref[...]
        go_ref[...] = gv
        d_ref[...], mo_ref[...], vo_ref[...] = _adamw_math(w_ref[...], gv, m_ref[...], v_ref[...])

    blk = pl.BlockSpec((tr, c), lambda i: (i, 0))
    return _pcall(body, name=name, out_shape=tuple(_sds((r, c), F32) for _ in range(4)), grid=(r // tr,),
                  in_specs=[blk] * 4, out_specs=(blk, blk, blk, blk))(w, g, m, v)


def _adamw_small(groups, name):
    count = len(groups[0])
    shapes = [a.shape for a in groups[0]]
    as2d = [(math.prod(s[:-1]), s[-1]) for s in shapes]

    def body(*refs):
        for i in range(count):
            out = _adamw_math(*[refs[j * count + i][...] for j in range(4)])
            for j in range(3):
                refs[(4 + j) * count + i][...] = out[j]

    specs = [pl.BlockSpec(s, lambda i: (0, 0)) for s in as2d]
    res = _pcall(body, name=name, out_shape=tuple(_sds(s, F32) for _ in range(3) for s in as2d), grid=(1,),
                 in_specs=specs * 4, out_specs=tuple(specs * 3))(*[a.reshape(s) for grp in groups for a, s in zip(grp, as2d)])
    return [[res[j * count + i].reshape(shapes[i]) for i in range(count)] for j in range(3)]


AG_RIDES = {"in_proj": (0, ("out",), ("down",)), "attn_fwd": (0, (("up", (0, 1)),), ("out",)), "out_proj": (0, (), ("up",)),
            "up_proj": (1, ("in_t",), ()), "ffn_act_fwd": (1, ("down",), ()),
            "down_proj": (1, (("up", (2,)),), ("in_t",))}
AG_FIRST = ("in_t", "down")
RS_RIDES = {"down_proj_dx": "swap", "ffn_act_bwd": ("up",), "up_proj_dx": "share", "attn_bwd": ("in_t", "out", "down")}
RS_RIDES_LAST = {"down_proj_dx": "swap", "ffn_act_bwd": ("up",), "up_proj_dx": "share", "up_proj_dw": ("in_t",),
                 "mix_bwd": ("down",), "yb_norm_bwd": ("out",)}
EARLY = ("out", "up", "down")
EARLY_RIDES = {"mix_bwd": "swap", "attn_bwd": ("up", "down"), "in_proj_dx": ("out",)}


class _Reduction:
    def __init__(self, grads):
        self.grads, self.pairs, self.recv = grads, {}, {}


class _MeshSchedule:
    def __init__(self, wts, depth, c_arr, where):
        self.wts, self.depth, self.c_arr, self.where = wts, depth, c_arr, where
        self.pending, self.last, self.full, self.unshared = None, None, None, []

    def fwd_comms(self, l):
        rides = _Rides()
        for name, (off, ici, fwd) in AG_RIDES.items():
            if l + off == 0:
                ici = tuple(k if isinstance(k, str) else k[0] for k in ici)
                ici, fwd = (tuple(k for k in keys if k not in AG_FIRST) for keys in (ici, fwd))
            if l + off < self.depth and (ici or fwd):
                rides.add(name, lambda ride=(l + off, ici, fwd): _ag_comm(self.wts, *ride))
        return rides

    def _shared(self, full):
        self.full, self.unshared = full, []

    def _ride(self, red, what, swap_keys):
        if what == "swap":
            return _swap_comm(red.grads, swap_keys, lambda theirs: red.pairs.update(
                _pair_sum(red.grads, theirs, self.c_arr, swap_keys)))
        if what == "share":
            return _share_comm(self.unshared, self.full, self._shared)
        return _rs_comm(red.pairs, what, red.recv)

    def bwd_comms(self, l):
        rides = _Rides()
        if self.pending is not None:
            for name, what in (RS_RIDES_LAST if l == 0 else RS_RIDES).items():
                if what != "share" or self.unshared:
                    rides.add(name, lambda what=what, red=self.pending: self._ride(red, what, BIG))
        if l == 0:
            self.last = _Reduction(rides.grads)
            for name, what in EARLY_RIDES.items():
                rides.add(name, lambda what=what: self._ride(self.last, what, EARLY))
        return rides

    def _reduce(self, l, red):
        self.full = _quad_sum(red.pairs, red.recv, self.where, l, self.full)
        self.unshared = self.unshared + [l]

    def after_bwd(self, l, grads):
        if self.pending is not None:
            self._reduce(l + 1, self.pending)
        if self.full is None:
            geo = _piece_geo(grads)
            self.full = {k: jnp.zeros((self.depth, 2) + geo[k], F32) for k in BIG}
        self.pending = _Reduction(grads) if l > 0 else None
        if l == 0:
            self.last.grads = grads

    def finish(self, extra):
        red = self.last
        late = tuple(k for k in BIG if k not in red.pairs)
        _run_comm(self._ride(red, "swap", late), "rs_swap_halves")
        _run_comm(_both(_rs_comm(red.pairs, late, red.recv), extra), "rs_to_owners")
        self._reduce(0, red)
        _run_comm(_share_comm(self.unshared, self.full, self._shared), "rs_share")
        return self.full


SHARDED_SMALL = ("conv_a_w", "conv_c_w", "conv_f_w")
SMALL = ("norm_mix_g", "conv_a_w", "conv_c_w", "conv_c_b", "ln_c_g", "ln_c_b", "out_norm_g", "norm_ffn_g",
         "conv_f_w", "rel_bias", "final_g")
SLAB_ROWS = 16


def _pack(arrays):
    flat = jnp.concatenate([a.reshape(-1) for a in arrays])
    unit = SLAB_ROWS * LANES
    total = -(-flat.shape[0] // unit) * unit
    return jnp.pad(flat, (0, total - flat.shape[0])).reshape(-1, LANES)


def _unpack(slab, shapes):
    flat = slab.reshape(-1)
    out, off = [], 0
    for shp in shapes:
        size = math.prod(shp)
        out.append(flat[off:off + size].reshape(shp))
        off += size
    return out


def kernel(x, norm_mix_g, w_in, conv_a_w, conv_c_w, conv_c_b, ln_c_g, ln_c_b, out_norm_g, w_out, norm_ffn_g, w_up, conv_f_w, w_down, rel_bias, final_g, loss_target, m_norm_mix_g, m_w_in, m_conv_a_w, m_conv_c_w, m_conv_c_b, m_ln_c_g, m_ln_c_b, m_out_norm_g, m_w_out, m_norm_ffn_g, m_w_up, m_conv_f_w, m_w_down, m_rel_bias, m_final_g, v_norm_mix_g, v_w_in, v_conv_a_w, v_conv_c_w, v_conv_c_b, v_ln_c_g, v_ln_c_b, v_out_norm_g, v_w_out, v_norm_ffn_g, v_w_up, v_conv_f_w, v_w_down, v_rel_bias, v_final_g):
    weights = dict(norm_mix_g=norm_mix_g, w_in=w_in, conv_a_w=conv_a_w, conv_c_w=conv_c_w, conv_c_b=conv_c_b,
                   ln_c_g=ln_c_g, ln_c_b=ln_c_b, out_norm_g=out_norm_g, w_out=w_out, norm_ffn_g=norm_ffn_g, w_up=w_up,
                   conv_f_w=conv_f_w, w_down=w_down, rel_bias=rel_bias, final_g=final_g)
    mom_m = dict(norm_mix_g=m_norm_mix_g, w_in=m_w_in, conv_a_w=m_conv_a_w, conv_c_w=m_conv_c_w, conv_c_b=m_conv_c_b,
                 ln_c_g=m_ln_c_g, ln_c_b=m_ln_c_b, out_norm_g=m_out_norm_g, w_out=m_w_out, norm_ffn_g=m_norm_ffn_g,
                 w_up=m_w_up, conv_f_w=m_conv_f_w, w_down=m_w_down, rel_bias=m_rel_bias, final_g=m_final_g)
    mom_v = dict(norm_mix_g=v_norm_mix_g, w_in=v_w_in, conv_a_w=v_conv_a_w, conv_c_w=v_conv_c_w, conv_c_b=v_conv_c_b,
                 ln_c_g=v_ln_c_g, ln_c_b=v_ln_c_b, out_norm_g=v_out_norm_g, w_out=v_w_out, norm_ffn_g=v_norm_ffn_g,
                 w_up=v_w_up, conv_f_w=v_conv_f_w, w_down=v_w_down, rel_bias=v_rel_bias, final_g=v_final_g)
    xi, yi, ci = _me()
    chip = _chip_of(xi, yi)
    c_arr = jnp.reshape(ci, (1,)).astype(I32)
    chip_arr = jnp.reshape(chip, (1,)).astype(I32)
    me_arr = jnp.reshape(4 * xi + 2 * yi + ci, (1,)).astype(I32)
    where = jnp.stack([chip, ci]).astype(I32)
    depth = w_out.shape[0]

    wts = {"in_t": _cast_into_gathered(jnp.swapaxes(w_in, 1, 2), chip_arr, False, "cast_in"),
           "out": _cast_into_gathered(w_out, chip_arr, False, "cast_out"),
           "up": _cast_into_gathered(w_up, chip_arr, True, "cast_up"),
           "down": _cast_into_gathered(w_down, chip_arr, False, "cast_down")}
    store = {}
    _run_comm(_both(_ag_comm(wts, 0, AG_FIRST, AG_FIRST),
                    _small_gather_comm(_pack([weights[n] for n in SHARDED_SMALL]), store)), "ag_weights")
    prm = {n: weights[n] for n in SMALL if n not in SHARDED_SMALL}
    per_chip = [_unpack(store["small"][j], [weights[n].shape for n in SHARDED_SMALL]) for j in range(N_CHIPS)]
    for i, n in enumerate(SHARDED_SMALL):
        prm[n] = jnp.concatenate([per_chip[j][i] for j in range(N_CHIPS)], axis=-1)

    sched = _MeshSchedule(wts, depth, c_arr, where)
    loss_row, dx, small, d_rel, d_final = _local_step(x[0], loss_target[0], wts, prm, sched)
    loss = lax.psum(loss_row[0, 0], ("x", "y", "c"))

    stacked = {n: jnp.stack([small[l][n] for l in range(depth)]) for n in small[0]}
    stacked["rel_bias"] = d_rel
    stacked["final_g"] = d_final
    full_shapes = [stacked[n].shape for n in SMALL]
    partial = _pack([stacked[n] for n in SMALL])
    reduced = sched.finish(_gather_comm(partial, lambda res: store.update(partials=res)))

    grads = {}
    shard_shapes = {"in_t": jnp.swapaxes(w_in, 1, 2).shape, "out": w_out.shape, "up": w_up.shape, "down": w_down.shape}
    red = {k: reduced[k].reshape(shard_shapes[k]) for k in BIG}
    grads["w_in"] = jnp.swapaxes(red["in_t"], 1, 2)
    grads["w_out"], grads["w_up"], grads["w_down"] = red["out"], red["up"], red["down"]
    delta, new_m, new_v = {}, {}, {}
    for n in ("w_in", "w_out", "w_up", "w_down"):
        shp = weights[n].shape
        flat = lambda a, shp=shp: a.reshape(shp[0] * shp[1], shp[2])
        tile = max(t for t in range(8, 257, 8) if shp[1] % t == 0)
        g, d, mn, vn = _adamw(flat(weights[n]), flat(grads[n]), flat(mom_m[n]), flat(mom_v[n]), "adamw_" + n, tile)
        grads[n], delta[n], new_m[n], new_v[n] = g.reshape(shp), d.reshape(shp), mn.reshape(shp), vn.reshape(shp)

    summed = _unpack(_sum_slabs(store["partials"], partial, me_arr), full_shapes)
    for n, g in zip(SMALL, summed):
        if n in SHARDED_SMALL:
            width = weights[n].shape[-1]
            g = lax.dynamic_slice_in_dim(g, chip * width, width, axis=g.ndim - 1)
        grads[n] = g
    res = _adamw_small([[src[n] for n in SMALL] for src in (weights, grads, mom_m, mom_v)], "adamw_small")
    for i, n in enumerate(SMALL):
        delta[n], new_m[n], new_v[n] = res[0][i], res[1][i], res[2][i]

    order = ("norm_mix_g", "w_in", "conv_a_w", "conv_c_w", "conv_c_b", "ln_c_g", "ln_c_b", "out_norm_g", "w_out",
             "norm_ffn_g", "w_up", "conv_f_w", "w_down", "rel_bias", "final_g")
    return (loss, dx[None], *[grads[n] for n in order], *[delta[n] for n in order], *[new_m[n] for n in order],
            *[new_v[n] for n in order])
```

```python
import functools
import math

import numpy as np
import jax
import jax.numpy as jnp
from jax import lax
from jax.experimental import pallas as pl
from jax.experimental.pallas import tpu as pltpu

F32 = jnp.float32
BF16 = jnp.bfloat16
I32 = jnp.int32

EPS = 1e-6
NEG = -1e30
D_HEAD = 64
LANES = 128
BLK = 128
ATTN_GROUP_FWD = 16
ATTN_GROUP_BWD = 16
DILATED_BRANCHES = ((128, 1), (512, 4), (2048, 16))
NUM_BUCKETS = 32
MAX_DISTANCE = 2048
SHORT_CONV = 3
CONFORMER_CONV = 31
FFN_CONV = 3
PAD_SHORT = 8
PAD_LONG = 32
ROW_CHUNK = 256
PROJ_ROWS = 512
V7X_VMEM_BYTES = 64 * 1024 * 1024
VMEM_REQUEST = V7X_VMEM_BYTES * 7 // 8

ADAM_LR = 0.001
ADAM_B1 = 0.9
ADAM_B2 = 0.999
ADAM_EPS = 1e-08
ADAM_WD = 0.01
ADAM_STEP = 10

MESH = pl.DeviceIdType.MESH
ANY = pl.BlockSpec(memory_space=pl.ANY)


def _sds(shape, dtype):
    return jax.ShapeDtypeStruct(tuple(shape), dtype)


class _Comm:
    def __init__(self, ins, out_shapes, aliases, sems, start, finish, done):
        self.ins, self.out_shapes, self.aliases, self.sems = list(ins), list(out_shapes), dict(aliases), list(sems)
        self.start, self.finish, self.done = start, finish, done


def _pcall(body, *, name, out_shape, grid=(), in_specs=None, out_specs=None, scratch_shapes=(), vmem=VMEM_REQUEST,
           aliases=None, prefetch=0, comm=None):
    params = pltpu.CompilerParams(dimension_semantics=("arbitrary",) * len(grid), vmem_limit_bytes=vmem)
    single = not isinstance(out_shape, (tuple, list))
    outs = [out_shape] if single else list(out_shape)
    ospecs = [out_specs] if single else list(out_specs)
    ispecs, scratch, aliases = list(in_specs), list(scratch_shapes), dict(aliases or {})
    n_in, n_out, n_scr = len(ispecs), len(outs), len(scratch)
    kernel_body = body
    if comm is not None:
        n_ci, n_co = len(comm.ins), len(comm.out_shapes)

        def kernel_body(*refs):
            pre, rest = refs[:prefetch], refs[prefetch:]
            core_in, c_in = rest[:n_in], rest[n_in:n_in + n_ci]
            o0 = n_in + n_ci
            core_out, c_out = rest[o0:o0 + n_out], rest[o0 + n_out:o0 + n_out + n_co]
            s0 = o0 + n_out + n_co
            core_scr, c_sem = rest[s0:s0 + n_scr], rest[s0 + n_scr:]
            first = functools.reduce(jnp.logical_and, [pl.program_id(a) == 0 for a in range(len(grid))])
            last = functools.reduce(jnp.logical_and, [pl.program_id(a) == grid[a] - 1 for a in range(len(grid))])
            pl.when(first)(lambda: comm.start(c_in, c_out, c_sem))
            body(*pre, *core_in, *core_out, *core_scr)
            pl.when(last)(lambda: comm.finish(c_in, c_out, c_sem))

        for i, o in comm.aliases.items():
            aliases[prefetch + n_in + i] = n_out + o
        ispecs += [ANY] * n_ci
        ospecs += [ANY] * n_co
        outs += comm.out_shapes
        scratch += comm.sems
    if prefetch:
        spec = pltpu.PrefetchScalarGridSpec(num_scalar_prefetch=prefetch, grid=grid, in_specs=ispecs,
                                            out_specs=tuple(ospecs), scratch_shapes=scratch)
        call = pl.pallas_call(kernel_body, name=name, out_shape=tuple(outs), grid_spec=spec,
                              input_output_aliases=aliases, compiler_params=params)
    else:
        call = pl.pallas_call(kernel_body, name=name, out_shape=tuple(outs), grid=grid, in_specs=ispecs,
                              out_specs=tuple(ospecs), scratch_shapes=scratch, input_output_aliases=aliases,
                              compiler_params=params)

    def run(*args):
        res = call(*args, *(comm.ins if comm is not None else ()))
        if comm is not None:
            comm.done(res[n_out:])
        return res[0] if single else tuple(res[:n_out])

    return run


def _both(a, b):
    def split(refs, na):
        return refs[:na], refs[na:]

    def run(which):
        def go(ins, outs, sems):
            for comm, i, o, s in zip((a, b), split(ins, len(a.ins)), split(outs, len(a.out_shapes)), split(sems, len(a.sems))):
                getattr(comm, which)(i, o, s)
        return go

    def done(res):
        a.done(res[:len(a.out_shapes)])
        b.done(res[len(a.out_shapes):])

    aliases = dict(a.aliases)
    aliases.update({len(a.ins) + i: len(a.out_shapes) + o for i, o in b.aliases.items()})
    return _Comm(a.ins + b.ins, a.out_shapes + b.out_shapes, aliases, a.sems + b.sems, run("start"), run("finish"), done)


def _run_comm(comm, name):
    def body(*refs):
        n_ci, n_co = len(comm.ins), len(comm.out_shapes)
        c_in, c_out, c_sem = refs[:n_ci], refs[n_ci:n_ci + n_co], refs[n_ci + n_co:]
        comm.start(c_in, c_out, c_sem)
        comm.finish(c_in, c_out, c_sem)

    res = pl.pallas_call(body, name=name, out_shape=tuple(comm.out_shapes), in_specs=[ANY] * len(comm.ins),
                         out_specs=tuple([ANY] * len(comm.out_shapes)), scratch_shapes=comm.sems,
                         input_output_aliases=comm.aliases)(*comm.ins)
    comm.done(res)


def _dot(a, b):
    return lax.dot_general(a, b, (((1,), (0,)), ((), ())), preferred_element_type=F32)


def _dot_nt(a, b):
    return lax.dot_general(a, b, (((1,), (1,)), ((), ())), preferred_element_type=F32)


def _dot_tn(a, b):
    return lax.dot_general(a, b, (((0,), (0,)), ((), ())), preferred_element_type=F32)


def _sigmoid(x):
    return 1.0 / (1.0 + jnp.exp(-x))


def _rstd(x):
    return lax.rsqrt(jnp.mean(x * x, axis=-1, keepdims=True) + EPS)


def _rms_fwd(x, g, name):
    s, d = x.shape
    tm = ROW_CHUNK

    def body(x_ref, g_ref, o_ref):
        xv = x_ref[...]
        o_ref[...] = (xv * _rstd(xv) * g_ref[...]).astype(BF16)

    return _pcall(body, name=name, out_shape=_sds((s, d), BF16), grid=(s // tm,),
                  in_specs=[pl.BlockSpec((tm, d), lambda i: (i, 0)), pl.BlockSpec((1, d), lambda i: (0, 0))],
                  out_specs=pl.BlockSpec((tm, d), lambda i: (i, 0)))(x, g)


def _final_loss(x, g, tgt, name):
    s, d = x.shape
    tm = ROW_CHUNK

    def body(x_ref, g_ref, t_ref, loss_ref, dx_ref, dxb_ref, dg_ref):
        i = pl.program_id(0)
        xv = x_ref[...]
        r = _rstd(xv)
        xh = xv * r
        e = xh * g_ref[...] - t_ref[...]
        lpart = 0.5 * jnp.sum(jnp.mean(e * e, axis=-1, keepdims=True), axis=0, keepdims=True)
        dy = e * (1.0 / d)
        gd = dy * g_ref[...]
        dx = r * (gd - xh * jnp.mean(gd * xh, axis=-1, keepdims=True))
        dx_ref[...] = dx
        dxb_ref[...] = dx.astype(BF16)
        part = jnp.sum(dy * xh, axis=0, keepdims=True)
        lrow = jnp.broadcast_to(lpart, (1, LANES))

        @pl.when(i == 0)
        def _():
            dg_ref[...] = part
            loss_ref[...] = lrow

        @pl.when(i > 0)
        def _():
            dg_ref[...] += part
            loss_ref[...] += lrow

    row = pl.BlockSpec((tm, d), lambda i: (i, 0))
    vec = pl.BlockSpec((1, d), lambda i: (0, 0))
    return _pcall(body, name=name,
                  out_shape=(_sds((1, LANES), F32), _sds((s, d), F32), _sds((s, d), BF16), _sds((1, d), F32)),
                  grid=(s // tm,), in_specs=[row, vec, row],
                  out_specs=(pl.BlockSpec((1, LANES), lambda i: (0, 0)), row, row, vec))(x, g, tgt)


def _mm_n(a, b, layer, *, nt, tn, out_dtype, name, resid=None, b_part=0, comm=None):
    s, k = a.shape
    n = b.shape[1] if nt else b.shape[2]
    rows = 512

    def body(a_ref, b_ref, *refs):
        o_ref = refs[-1]
        bv = b_ref[...]
        for r0 in range(0, s, rows):
            av = a_ref[r0:r0 + rows, :]
            prod = _dot_nt(av, bv) if nt else _dot(av, bv)
            if resid is not None:
                prod = refs[0][r0:r0 + rows, :] + prod
            o_ref[r0:r0 + rows, :] = prod.astype(out_dtype)

    b_spec = (pl.BlockSpec((None, tn, k), lambda j: (layer, j, b_part)) if nt
              else pl.BlockSpec((None, k, tn), lambda j: (layer, b_part, j)))
    col = pl.BlockSpec((s, tn), lambda j: (0, j))
    extra = () if resid is None else (resid,)
    return _pcall(body, name=name, out_shape=_sds((s, n), out_dtype), grid=(n // tn,),
                  in_specs=[pl.BlockSpec((s, k), lambda j: (0, 0)), b_spec] + [col] * len(extra),
                  out_specs=col, comm=comm)(a, b, *extra)


def _mm_tn(a, b, *, t, name):
    s, ka = a.shape
    n = b.shape[1]

    def body(a_ref, b_ref, o_ref):
        o_ref[...] = _dot_tn(a_ref[...], b_ref[...]).astype(BF16)

    return _pcall(body, name=name, out_shape=_sds((ka, n), BF16), grid=(ka // t,),
                  in_specs=[pl.BlockSpec((s, t), lambda i: (0, i)), pl.BlockSpec((s, n), lambda i: (0, 0))],
                  out_specs=pl.BlockSpec((t, n), lambda i: (i, 0)))(a, b)


def _mm_tn_pieces(pieces, b, *, t, name):
    s, n = b.shape
    blocks = [p.shape[1] // t for p in pieces]
    starts = [sum(blocks[:i]) for i in range(len(pieces))]

    def body(*refs):
        p_refs, b_ref, o_ref = refs[:len(pieces)], refs[len(pieces)], refs[len(pieces) + 1]
        j = pl.program_id(0)
        for p_ref, start, count in zip(p_refs, starts, blocks):
            @pl.when((j >= start) & (j < start + count))
            def _(p_ref=p_ref):
                o_ref[...] = _dot_tn(p_ref[...], b_ref[...]).astype(BF16)

    specs = [pl.BlockSpec((s, t), lambda j, start=start, count=count: (0, jnp.clip(j - start, 0, count - 1)))
             for start, count in zip(starts, blocks)]
    return _pcall(body, name=name, out_shape=_sds((sum(blocks) * t, n), BF16), grid=(sum(blocks),),
                  in_specs=specs + [pl.BlockSpec((s, n), lambda j: (0, 0))],
                  out_specs=pl.BlockSpec((t, n), lambda j: (j, 0)))(*pieces, b)


def _mm_tn2(a, b_lo, b_hi, *, t, name, comm=None):
    s, ka = a.shape
    half = b_lo.shape[1]
    nb = half // t

    def body(a_ref, lo_ref, hi_ref, o_ref):
        j = pl.program_id(0)

        @pl.when(j < nb)
        def _():
            o_ref[...] = _dot_tn(a_ref[...], lo_ref[...]).astype(BF16)

        @pl.when(j >= nb)
        def _():
            o_ref[...] = _dot_tn(a_ref[...], hi_ref[...]).astype(BF16)

    return _pcall(body, name=name, out_shape=_sds((ka, 2 * half), BF16), grid=(2 * nb,),
                  in_specs=[pl.BlockSpec((s, ka), lambda j: (0, 0)),
                            pl.BlockSpec((s, t), lambda j: (0, jnp.minimum(j, nb - 1))),
                            pl.BlockSpec((s, t), lambda j: (0, jnp.maximum(j - nb, 0)))],
                  out_specs=pl.BlockSpec((ka, t), lambda j: (0, j)), comm=comm)(a, b_lo, b_hi)


SUBLANES = 8


def _tap_windows(win, width, lead, rows):
    offs = [lead + k for k in range(width)]
    if width <= SUBLANES:
        return [win[o:o + rows, :] for o in offs]
    n = win.shape[0]
    out = {}
    for r in sorted({o % SUBLANES for o in offs}):
        base = win if r == 0 else pltpu.roll(win, n - r, axis=0)
        for o in offs:
            if o % SUBLANES == r:
                out[o - lead] = base[o - r:o - r + rows, :]
    return [out[k] for k in range(width)]


def _conv_taps(taps, w_ref):
    acc = None
    for k, tap in enumerate(taps):
        term = w_ref[pl.ds(k, 1), :] * tap
        acc = term if acc is None else acc + term
    return acc


def _causal_taps(win, width, pad, rows):
    return _tap_windows(win, width, pad - (width - 1), rows)


def _anticausal_taps(win, width, rows):
    return _tap_windows(win, width, 0, rows)[::-1]


def _conv_wgrad(dw_ref, g, taps):
    for k, tap in enumerate(taps):
        dw_ref[pl.ds(k, 1), :] += jnp.sum(g * tap, axis=0, keepdims=True)


def _mixer_a_fwd(ab, taps_t, wa_ref):
    ct = _conv_taps(taps_t, wa_ref)
    return ab * ct, ct


def _mixer_c_fwd(taps_u, wc_ref, cb_ref, lg_ref, lb_ref):
    u = _conv_taps(taps_u, wc_ref) + cb_ref[...]
    mu = jnp.mean(u, axis=-1, keepdims=True)
    uc = u - mu
    rs = lax.rsqrt(jnp.mean(uc * uc, axis=-1, keepdims=True) + EPS)
    uh = uc * rs
    ln = uh * lg_ref[...] + lb_ref[...]
    sg = _sigmoid(ln)
    return ln * sg, ln, sg, uh, rs


def _mix_fwd(z, wa, wc, cb, lg, lb, ga, gc, name):
    s = z.shape[0]
    w = wa.shape[1]
    nblk = z.shape[1] // w
    rc = ROW_CHUNK

    def body(ah_ref, ab_ref, ac_ref, cv_ref, cg_ref, wa_ref, wc_ref, cb_ref, lg_ref, lb_ref, ga_ref, gc_ref,
             ya_ref, yc_ref, tpad, upad):
        tpad[pl.ds(0, PAD_SHORT), :] = jnp.zeros((PAD_SHORT, w), F32)
        upad[pl.ds(0, PAD_LONG), :] = jnp.zeros((PAD_LONG, w), F32)

        def chunk(i, carry):
            base = pl.multiple_of(i * rc, rc)
            rows = pl.ds(base, rc)
            ah, ab, ac = ah_ref[rows, :], ab_ref[rows, :], ac_ref[rows, :]
            tpad[pl.ds(base + PAD_SHORT, rc), :] = ac * ah
            ya, _ = _mixer_a_fwd(ab, _causal_taps(tpad[pl.ds(base, rc + PAD_SHORT), :], SHORT_CONV, PAD_SHORT, rc), wa_ref)
            ya_ref[rows, :] = (ya * _rstd(ya) * ga_ref[...]).astype(BF16)
            upad[pl.ds(base + PAD_LONG, rc), :] = cv_ref[rows, :] * _sigmoid(cg_ref[rows, :])
            taps_u = _causal_taps(upad[pl.ds(base, rc + PAD_LONG), :], CONFORMER_CONV, PAD_LONG, rc)
            yc = _mixer_c_fwd(taps_u, wc_ref, cb_ref, lg_ref, lb_ref)[0]
            yc_ref[rows, :] = (yc * _rstd(yc) * gc_ref[...]).astype(BF16)
            return carry

        lax.fori_loop(0, s // rc, chunk, 0)

    def zblk(j):
        return pl.BlockSpec((s, w), lambda i: (0, j))

    def whole(a):
        return pl.BlockSpec(a.shape, lambda i: (0, 0))

    return _pcall(
        body, name=name, out_shape=(_sds((s, w), BF16), _sds((s, w), BF16)), grid=(1,),
        in_specs=[zblk(0), zblk(1), zblk(2), zblk(nblk - 2), zblk(nblk - 1)] + [whole(a) for a in (wa, wc, cb, lg, lb, ga, gc)],
        out_specs=(pl.BlockSpec((s, w), lambda i: (0, 0)), pl.BlockSpec((s, w), lambda i: (0, 0))),
        scratch_shapes=[pltpu.VMEM((s + PAD_SHORT, w), F32), pltpu.VMEM((s + PAD_LONG, w), F32)],
    )(z, z, z, z, z, wa, wc, cb, lg, lb, ga, gc)


def _mix_bwd(z, dy, wa, wc, cb, lg, lb, ga, gc, name, comm=None):
    s = z.shape[0]
    w = wa.shape[1]
    nblk = z.shape[1] // w
    nyb = dy.shape[1] // w
    rc = ROW_CHUNK

    def body(ah_ref, ab_ref, ac_ref, cv_ref, cg_ref, dya_ref, dyc_ref,
             wa_ref, wc_ref, cb_ref, lg_ref, lb_ref, ga_ref, gc_ref,
             dza_ref, dzc_ref, dwa_ref, dwc_ref, dcb_ref, dlg_ref, dlb_ref, dga_ref, dgc_ref,
             tpad, upad, dctp, dup):
        tpad[pl.ds(0, PAD_SHORT), :] = jnp.zeros((PAD_SHORT, w), F32)
        upad[pl.ds(0, PAD_LONG), :] = jnp.zeros((PAD_LONG, w), F32)
        dctp[pl.ds(s, PAD_SHORT), :] = jnp.zeros((PAD_SHORT, w), F32)
        dup[pl.ds(s, PAD_LONG), :] = jnp.zeros((PAD_LONG, w), F32)
        for ref in (dwa_ref, dwc_ref, dcb_ref, dlg_ref, dlb_ref, dga_ref, dgc_ref):
            ref[...] = jnp.zeros(ref.shape, F32)

        def rms_bwd(y, g_ref, dyn, dg_ref):
            r = _rstd(y)
            yh = y * r
            gd = dyn * g_ref[...]
            dg_ref[...] += jnp.sum(dyn * yh, axis=0, keepdims=True)
            return r * (gd - yh * jnp.mean(gd * yh, axis=-1, keepdims=True))

        def first(i, carry):
            base = pl.multiple_of(i * rc, rc)
            rows = pl.ds(base, rc)
            ah, ab, ac = ah_ref[rows, :], ab_ref[rows, :], ac_ref[rows, :]
            tpad[pl.ds(base + PAD_SHORT, rc), :] = ac * ah
            taps_t = _causal_taps(tpad[pl.ds(base, rc + PAD_SHORT), :], SHORT_CONV, PAD_SHORT, rc)
            ya, ct = _mixer_a_fwd(ab, taps_t, wa_ref)
            dya = rms_bwd(ya, ga_ref, dya_ref[rows, :], dga_ref)
            dza_ref[rows, w:2 * w] = (dya * ct).astype(BF16)
            dct = dya * ab
            dctp[rows, :] = dct
            _conv_wgrad(dwa_ref, dct, taps_t)

            upad[pl.ds(base + PAD_LONG, rc), :] = cv_ref[rows, :] * _sigmoid(cg_ref[rows, :])
            taps_u = _causal_taps(upad[pl.ds(base, rc + PAD_LONG), :], CONFORMER_CONV, PAD_LONG, rc)
            yc, ln, sg, uh, rs = _mixer_c_fwd(taps_u, wc_ref, cb_ref, lg_ref, lb_ref)
            dyc = rms_bwd(yc, gc_ref, dyc_ref[rows, :], dgc_ref)
            dln = dyc * (sg * (1.0 + ln * (1.0 - sg)))
            dlg_ref[...] += jnp.sum(dln * uh, axis=0, keepdims=True)
            dlb_ref[...] += jnp.sum(dln, axis=0, keepdims=True)
            duh = dln * lg_ref[...]
            du = rs * (duh - jnp.mean(duh, axis=-1, keepdims=True) - uh * jnp.mean(duh * uh, axis=-1, keepdims=True))
            dcb_ref[...] += jnp.sum(du, axis=0, keepdims=True)
            dup[rows, :] = du
            _conv_wgrad(dwc_ref, du, taps_u)
            return carry

        lax.fori_loop(0, s // rc, first, 0)

        def second(i, carry):
            base = pl.multiple_of(i * rc, rc)
            rows = pl.ds(base, rc)
            dt = _conv_taps(_anticausal_taps(dctp[pl.ds(base, rc + PAD_SHORT), :], SHORT_CONV, rc), wa_ref)
            dza_ref[rows, 0:w] = (dt * ac_ref[rows, :]).astype(BF16)
            dza_ref[rows, 2 * w:3 * w] = (dt * ah_ref[rows, :]).astype(BF16)
            du0 = _conv_taps(_anticausal_taps(dup[pl.ds(base, rc + PAD_LONG), :], CONFORMER_CONV, rc), wc_ref)
            sg = _sigmoid(cg_ref[rows, :])
            dzc_ref[rows, 0:w] = (du0 * sg).astype(BF16)
            dzc_ref[rows, w:2 * w] = (du0 * cv_ref[rows, :] * sg * (1.0 - sg)).astype(BF16)
            return carry

        lax.fori_loop(0, s // rc, second, 0)

    def blk(j):
        return pl.BlockSpec((s, w), lambda i: (0, j))

    def whole(a):
        return pl.BlockSpec(tuple(a.shape), lambda i: (0, 0))

    params = (wa, wc, cb, lg, lb, ga, gc)
    outs = (_sds((s, 3 * w), BF16), _sds((s, 2 * w), BF16)) + tuple(_sds(p.shape, F32) for p in params)
    return _pcall(
        body, name=name, out_shape=outs, grid=(1,),
        in_specs=[blk(0), blk(1), blk(2), blk(nblk - 2), blk(nblk - 1), blk(0), blk(nyb - 1)] + [whole(p) for p in params],
        out_specs=tuple(whole(o) for o in outs),
        scratch_shapes=[pltpu.VMEM((s + PAD_SHORT, w), F32), pltpu.VMEM((s + PAD_LONG, w), F32),
                        pltpu.VMEM((s + PAD_SHORT, w), F32), pltpu.VMEM((s + PAD_LONG, w), F32)], comm=comm,
    )(z, z, z, z, z, dy, dy, *params)


def _ffn_act_fwd(up, wf, name, comm=None):
    s, f2 = up.shape
    f = f2 // 2
    tc = 256
    nb = f // tc
    rc = ROW_CHUNK

    def body(g_ref, v_ref, wg_ref, wv_ref, o_ref, gpad, vpad):
        gpad[pl.ds(0, PAD_SHORT), :] = jnp.zeros((PAD_SHORT, tc), F32)
        vpad[pl.ds(0, PAD_SHORT), :] = jnp.zeros((PAD_SHORT, tc), F32)

        def chunk(i, carry):
            base = pl.multiple_of(i * rc, rc)
            rows = pl.ds(base, rc)
            gpad[pl.ds(base + PAD_SHORT, rc), :] = g_ref[rows, :].astype(F32)
            vpad[pl.ds(base + PAD_SHORT, rc), :] = v_ref[rows, :].astype(F32)
            gc = _conv_taps(_causal_taps(gpad[pl.ds(base, rc + PAD_SHORT), :], FFN_CONV, PAD_SHORT, rc), wg_ref)
            vc = _conv_taps(_causal_taps(vpad[pl.ds(base, rc + PAD_SHORT), :], FFN_CONV, PAD_SHORT, rc), wv_ref)
            o_ref[rows, :] = (gc * _sigmoid(gc) * vc).astype(BF16)
            return carry

        lax.fori_loop(0, s // rc, chunk, 0)

    return _pcall(
        body, name=name, out_shape=_sds((s, f), BF16), grid=(nb,),
        in_specs=[pl.BlockSpec((s, tc), lambda j: (0, j)), pl.BlockSpec((s, tc), lambda j: (0, j + nb)),
                  pl.BlockSpec((FFN_CONV, tc), lambda j: (0, j)), pl.BlockSpec((FFN_CONV, tc), lambda j: (0, j + nb))],
        out_specs=pl.BlockSpec((s, tc), lambda j: (0, j)),
        scratch_shapes=[pltpu.VMEM((s + PAD_SHORT, tc), F32), pltpu.VMEM((s + PAD_SHORT, tc), F32)], comm=comm,
    )(up, up, wf, wf)


def _ffn_act_bwd(up, dact, wf, name, comm=None):
    s, f2 = up.shape
    f = f2 // 2
    tc = 256
    nb = f // tc
    rc = ROW_CHUNK

    def body(g_ref, v_ref, da_ref, wg_ref, wv_ref, act_ref, dg_ref, dv_ref, dwg_ref, dwv_ref, gpad, vpad, dgp, dvp):
        gpad[pl.ds(0, PAD_SHORT), :] = jnp.zeros((PAD_SHORT, tc), F32)
        vpad[pl.ds(0, PAD_SHORT), :] = jnp.zeros((PAD_SHORT, tc), F32)
        dgp[pl.ds(s, PAD_SHORT), :] = jnp.zeros((PAD_SHORT, tc), F32)
        dvp[pl.ds(s, PAD_SHORT), :] = jnp.zeros((PAD_SHORT, tc), F32)
        dwg_ref[...] = jnp.zeros((FFN_CONV, tc), F32)
        dwv_ref[...] = jnp.zeros((FFN_CONV, tc), F32)

        def first(i, carry):
            base = pl.multiple_of(i * rc, rc)
            rows = pl.ds(base, rc)
            gpad[pl.ds(base + PAD_SHORT, rc), :] = g_ref[rows, :].astype(F32)
            vpad[pl.ds(base + PAD_SHORT, rc), :] = v_ref[rows, :].astype(F32)
            taps_g = _causal_taps(gpad[pl.ds(base, rc + PAD_SHORT), :], FFN_CONV, PAD_SHORT, rc)
            taps_v = _causal_taps(vpad[pl.ds(base, rc + PAD_SHORT), :], FFN_CONV, PAD_SHORT, rc)
            gc = _conv_taps(taps_g, wg_ref)
            vc = _conv_taps(taps_v, wv_ref)
            sg = _sigmoid(gc)
            silu = gc * sg
            act_ref[rows, :] = (silu * vc).astype(BF16)
            da = da_ref[rows, :].astype(F32)
            dgc = da * vc * (sg * (1.0 + gc * (1.0 - sg)))
            dvc = da * silu
            dgp[rows, :] = dgc
            dvp[rows, :] = dvc
            _conv_wgrad(dwg_ref, dgc, taps_g)
            _conv_wgrad(dwv_ref, dvc, taps_v)
            return carry

        lax.fori_loop(0, s // rc, first, 0)

        def second(i, carry):
            base = pl.multiple_of(i * rc, rc)
            rows = pl.ds(base, rc)
            dg_ref[rows, :] = _conv_taps(_anticausal_taps(dgp[pl.ds(base, rc + PAD_SHORT), :], FFN_CONV, rc), wg_ref).astype(BF16)
            dv_ref[rows, :] = _conv_taps(_anticausal_taps(dvp[pl.ds(base, rc + PAD_SHORT), :], FFN_CONV, rc), wv_ref).astype(BF16)
            return carry

        lax.fori_loop(0, s // rc, second, 0)

    lo = pl.BlockSpec((s, tc), lambda j: (0, j))
    hi = pl.BlockSpec((s, tc), lambda j: (0, j + nb))
    wlo = pl.BlockSpec((FFN_CONV, tc), lambda j: (0, j))
    whi = pl.BlockSpec((FFN_CONV, tc), lambda j: (0, j + nb))
    act, dgate, dval, dwg, dwv = _pcall(
        body, name=name,
        out_shape=(_sds((s, f), BF16), _sds((s, f), BF16), _sds((s, f), BF16), _sds((FFN_CONV, f), F32), _sds((FFN_CONV, f), F32)),
        grid=(nb,), in_specs=[lo, hi, lo, wlo, whi], out_specs=(lo, lo, lo, wlo, wlo),
        scratch_shapes=[pltpu.VMEM((s + PAD_SHORT, tc), F32) for _ in range(4)], comm=comm,
    )(up, up, dact, wf, wf)
    return act, dgate, dval, jnp.concatenate([dwg, dwv], axis=1)


def _out_proj(yan, yb, ycn, gb, x, w_out, layer, g_next, name, comm=None):
    s, w = yan.shape
    wb = yb.shape[1]
    d = x.shape[1]
    tm = PROJ_ROWS

    def body(ya_ref, yb_ref, yc_ref, gb_ref, x_ref, w_ref, g_ref, y_ref, xm_ref, h_ref):
        ybv = yb_ref[...]
        y = jnp.concatenate([ya_ref[...], (ybv * _rstd(ybv) * gb_ref[...]).astype(BF16), yc_ref[...]], axis=1)
        y_ref[...] = y
        xm = x_ref[...] + _dot(y, w_ref[...])
        xm_ref[...] = xm
        h_ref[...] = (xm * _rstd(xm) * g_ref[...]).astype(BF16)

    def rows(width):
        return pl.BlockSpec((tm, width), lambda i: (i, 0))

    def vec(width):
        return pl.BlockSpec((1, width), lambda i: (0, 0))

    return _pcall(body, name=name, out_shape=(_sds((s, d), BF16), _sds((s, d), F32), _sds((s, d), BF16)), grid=(s // tm,),
                  in_specs=[rows(w), rows(wb), rows(w), vec(wb), rows(d), pl.BlockSpec((None, d, d), lambda i: (layer, 0, 0)), vec(d)],
                  out_specs=(rows(d), rows(d), rows(d)), comm=comm)(yan, yb, ycn, gb, x, w_out, g_next)


def _down_proj(act, w_down, layer, x_mid, g_next, name, comm=None):
    s, f = act.shape
    d = x_mid.shape[1]
    tm = PROJ_ROWS

    def body(a_ref, w_ref, x_ref, *refs):
        xo = x_ref[...] + _dot(a_ref[...], w_ref[...])
        refs[-2 if g_next is not None else -1][...] = xo
        if g_next is not None:
            refs[-1][...] = (xo * _rstd(xo) * refs[0][...]).astype(BF16)

    row = pl.BlockSpec((tm, d), lambda i: (i, 0))
    ins = [act, w_down, x_mid] + ([g_next] if g_next is not None else [])
    in_specs = [pl.BlockSpec((tm, f), lambda i: (i, 0)), pl.BlockSpec((None, f, d), lambda i: (layer, 0, 0)), row]
    in_specs += [pl.BlockSpec((1, d), lambda i: (0, 0))] if g_next is not None else []
    outs = (_sds((s, d), F32), _sds((s, d), BF16)) if g_next is not None else (_sds((s, d), F32),)
    res = _pcall(body, name=name, out_shape=outs, grid=(s // tm,), in_specs=in_specs, out_specs=tuple([row] * len(outs)),
                 comm=comm)(*ins)
    return (res[0], res[1]) if g_next is not None else (res[0], None)


def _proj_dx(pieces, w, layer, nt, x, g, dres, name, tm, comm=None):
    s, d = x.shape
    widths = [p.shape[1] for p in pieces]

    def body(*refs):
        p_refs, (w_ref, x_ref, g_ref, dres_ref, dx_ref, dxb_ref, dg_ref) = refs[:len(pieces)], refs[len(pieces):]
        i = pl.program_id(0)
        dh, off = None, 0
        for p_ref, width in zip(p_refs, widths):
            part = _dot_nt(p_ref[...], w_ref[:, off:off + width]) if nt else _dot(p_ref[...], w_ref[off:off + width, :])
            dh = part if dh is None else dh + part
            off += width
        xv = x_ref[...]
        r = _rstd(xv)
        xh = xv * r
        gd = dh * g_ref[...]
        dx = dres_ref[...] + r * (gd - xh * jnp.mean(gd * xh, axis=-1, keepdims=True))
        dx_ref[...] = dx
        dxb_ref[...] = dx.astype(BF16)
        part = jnp.sum(dh * xh, axis=0, keepdims=True)

        @pl.when(i == 0)
        def _():
            dg_ref[...] = part

        @pl.when(i > 0)
        def _():
            dg_ref[...] += part

    row = pl.BlockSpec((tm, d), lambda i: (i, 0))
    vec = pl.BlockSpec((1, d), lambda i: (0, 0))
    w_spec = pl.BlockSpec((None,) + w.shape[1:], lambda i: (layer, 0, 0))
    return _pcall(body, name=name, out_shape=(_sds((s, d), F32), _sds((s, d), BF16), _sds((1, d), F32)), grid=(s // tm,),
                  in_specs=[pl.BlockSpec((tm, width), lambda i: (i, 0)) for width in widths] + [w_spec, row, vec, row],
                  out_specs=(row, row, vec), comm=comm)(*pieces, w, x, g, dres)


def _yb_norm_bwd(yb, dy, gb, name, comm=None):
    s, wb = yb.shape
    w = wb // 2
    heads = wb // D_HEAD
    tm = ROW_CHUNK

    def body(yb_ref, d1_ref, d2_ref, g_ref, dyb_ref, dl_ref, dg_ref):
        i = pl.program_id(0)
        y = yb_ref[...]
        dyn = jnp.concatenate([d1_ref[...], d2_ref[...]], axis=1)
        r = _rstd(y)
        yh = y * r
        gd = dyn * g_ref[...]
        dyb = r * (gd - yh * jnp.mean(gd * yh, axis=-1, keepdims=True))
        dyb_ref[...] = dyb
        part = jnp.sum(dyn * yh, axis=0, keepdims=True)
        prod = dyb * y
        even = lax.broadcasted_iota(I32, (tm, LANES), 1) < D_HEAD
        for p in range(heads // 2):
            blk = prod[:, p * LANES:(p + 1) * LANES]
            ev = jnp.sum(jnp.where(even, blk, 0.0), axis=1, keepdims=True)
            od = jnp.sum(jnp.where(even, 0.0, blk), axis=1, keepdims=True)
            dl_ref[2 * p] = jnp.broadcast_to(ev, (tm, LANES))
            dl_ref[2 * p + 1] = jnp.broadcast_to(od, (tm, LANES))

        @pl.when(i == 0)
        def _():
            dg_ref[...] = part

        @pl.when(i > 0)
        def _():
            dg_ref[...] += part

    return _pcall(
        body, name=name, out_shape=(_sds((s, wb), F32), _sds((heads, s, LANES), F32), _sds((1, wb), F32)),
        grid=(s // tm,),
        in_specs=[pl.BlockSpec((tm, wb), lambda i: (i, 0)), pl.BlockSpec((tm, w), lambda i: (i, 1)),
                  pl.BlockSpec((tm, w), lambda i: (i, 2)), pl.BlockSpec((1, wb), lambda i: (0, 0))],
        out_specs=(pl.BlockSpec((tm, wb), lambda i: (i, 0)), pl.BlockSpec((heads, tm, LANES), lambda i: (0, i, 0)),
                   pl.BlockSpec((1, wb), lambda i: (0, 0))), comm=comm,
    )(yb, dy, dy, gb)


def _t5_bucket_table():
    max_exact = NUM_BUCKETS // 2
    out = np.full((len(DILATED_BRANCHES), BLK, 2 * BLK), -1, np.int32)
    rel = np.arange(BLK)[:, None] - np.arange(2 * BLK)[None, :] + BLK
    for b, (window, dilation) in enumerate(DILATED_BRANCHES):
        n_keys = window // dilation
        dist = np.maximum(rel, 0) * dilation
        d_f = np.maximum(dist, 1).astype(np.float32)
        large = max_exact + (np.log(d_f / np.float32(max_exact)) / np.float32(math.log(MAX_DISTANCE / max_exact))
                             * np.float32(NUM_BUCKETS - max_exact)).astype(np.int32)
        large = np.minimum(large, NUM_BUCKETS - 1)
        bucket = np.where(dist < max_exact, dist, large)
        out[b] = np.where((rel >= 0) & (rel <= n_keys), bucket, -1)
    return out


def _bias_tiles(rel_bias, buckets, name):
    nbk, heads = rel_bias.shape
    nbr = buckets.shape[0]

    def body(rb_ref, bk_ref, o_ref):
        for br in range(nbr):
            bk = bk_ref[br]
            tiles = [jnp.full((BLK, 2 * BLK), NEG, F32) for _ in range(heads)]
            for b in range(nbk):
                hit = bk == b
                tiles = [jnp.where(hit, rb_ref[b, h], tiles[h]) for h in range(heads)]
            for h in range(heads):
                o_ref[br, h] = tiles[h]

    return _pcall(body, name=name, out_shape=_sds((nbr, heads, BLK, 2 * BLK), F32), grid=(1,),
                  in_specs=[pl.BlockSpec(memory_space=pltpu.SMEM), pl.BlockSpec(buckets.shape, lambda i: (0, 0, 0))],
                  out_specs=pl.BlockSpec((nbr, heads, BLK, 2 * BLK), lambda i: (0, 0, 0, 0)))(rel_bias, buckets)


def _bias_grad(dtiles, buckets, nbk, name):
    nbr, heads = dtiles.shape[:2]

    def body(dt_ref, bk_ref, o_ref):
        row = lax.broadcasted_iota(I32, (nbk, LANES), 0)
        col = lax.broadcasted_iota(I32, (nbk, LANES), 1)
        out = jnp.zeros((nbk, LANES), F32)
        for h in range(heads):
            for b in range(nbk):
                tot = jnp.zeros((), F32)
                for br in range(nbr):
                    tot = tot + jnp.sum(jnp.where(bk_ref[br] == b, dt_ref[br, h], 0.0))
                out = jnp.where((row == b) & (col == h), tot, out)
        o_ref[...] = out

    return _pcall(body, name=name, out_shape=_sds((nbk, LANES), F32), grid=(1,),
                  in_specs=[pl.BlockSpec(dtiles.shape, lambda i: (0, 0, 0, 0)), pl.BlockSpec(buckets.shape, lambda i: (0, 0, 0))],
                  out_specs=pl.BlockSpec((nbk, LANES), lambda i: (0, 0)))(dtiles, buckets)


def _largest_divisor(n, cap):
    return max(g for g in range(1, cap + 1) if n % g == 0)


def _attn_blocks(s, visit, group):
    for br, (window, d) in enumerate(DILATED_BRANCHES):
        n_blk = (s // d) // BLK
        span = BLK * d
        g1 = _largest_divisor(d, group)

        def firsts(t, carry, br=br, d=d, g1=g1):
            for j in range(g1):
                visit(br, d, t * g1 + j, False)
            return carry

        lax.fori_loop(0, d // g1, firsts, 0)
        if n_blk > 1:
            total = d * (n_blk - 1)
            g2 = _largest_divisor(total, group)

            def rest(t, carry, br=br, d=d, n_blk=n_blk, span=span, g2=g2):
                for j in range(g2):
                    idx = t * g2 + j
                    visit(br, d, idx // (n_blk - 1) + (1 + idx % (n_blk - 1)) * span, True)
                return carry

            lax.fori_loop(0, total // g2, rest, 0)


def _rows(start, size, d):
    return pl.ds(pl.multiple_of(start, BLK), size) if d == 1 else pl.ds(start, size, stride=d)


def _attn_fwd(z, btiles, col0, name, comm=None):
    s = z.shape[0]
    nbr, heads = btiles.shape[:2]
    pairs = heads // 2
    scale = D_HEAD ** -0.5
    rc = ROW_CHUNK

    def body(q_ref, k_ref, v_ref, bt_ref, yb_ref, lse_ref, acc_ref, m_ref, l_ref):
        even = lax.broadcasted_iota(I32, (BLK, LANES), 1) < D_HEAD
        even2 = lax.broadcasted_iota(I32, (2 * BLK, LANES), 1) < D_HEAD

        def visit(br, d, start, prev):
            kw = 2 * BLK if prev else BLK
            rows_q = _rows(start, BLK, d)
            rows_k = _rows(start - BLK * d, kw, d) if prev else rows_q
            qb = q_ref[rows_q, :]
            kb = k_ref[rows_k, :].astype(BF16)
            vw = v_ref[rows_k, :]
            ev_k = even2 if prev else even
            qm = jnp.concatenate([jnp.where(even, qb, 0.0), jnp.where(even, 0.0, qb)], axis=0).astype(BF16)
            bias = [bt_ref[br, e] if prev else bt_ref[br, e, :, BLK:] for e in range(2)]
            sc = _dot_nt(qm, kb) * scale + jnp.concatenate(bias, axis=0)
            m = jnp.max(sc, axis=1, keepdims=True)
            p = jnp.exp(sc - m)
            l = jnp.sum(p, axis=1, keepdims=True)
            pb = p.astype(BF16)
            vm = jnp.concatenate([jnp.where(ev_k, vw, 0.0), jnp.where(ev_k, 0.0, vw)], axis=0).astype(BF16)
            acc_ref.at[br][rows_q, :] = _dot(jnp.concatenate([pb[:BLK], pb[BLK:]], axis=1), vm)
            for e in range(2):
                m_ref.at[br, e][rows_q, :] = jnp.broadcast_to(m[e * BLK:(e + 1) * BLK], (BLK, LANES))
                l_ref.at[br, e][rows_q, :] = jnp.broadcast_to(l[e * BLK:(e + 1) * BLK], (BLK, LANES))

        _attn_blocks(s, visit, ATTN_GROUP_FWD)

        ev_c = lax.broadcasted_iota(I32, (rc, LANES), 1) < D_HEAD

        def merge(i, carry):
            rows = pl.ds(pl.multiple_of(i * rc, rc), rc)
            wts, dens = [], []
            for e in range(2):
                ms = [m_ref[br, e, rows, :] for br in range(nbr)]
                top = functools.reduce(jnp.maximum, ms)
                w = [jnp.exp(mb - top) for mb in ms]
                den = functools.reduce(lambda a, b: a + b, [w[br] * l_ref[br, e, rows, :] for br in range(nbr)])
                lse_ref[e, rows, :] = top + jnp.log(den)
                wts.append(w)
                dens.append(den)
            num = functools.reduce(lambda a, b: a + b,
                                   [jnp.where(ev_c, wts[0][br], wts[1][br]) * acc_ref[br, rows, :] for br in range(nbr)])
            yb_ref[rows, :] = num / jnp.where(ev_c, dens[0], dens[1])
            return carry

        lax.fori_loop(0, s // rc, merge, 0)

    def zcol(j):
        return pl.BlockSpec((s, LANES), lambda p, j=j: (0, col0 + j + p))

    return _pcall(
        body, name=name, out_shape=(_sds((s, pairs * LANES), F32), _sds((heads, s, LANES), F32)), grid=(pairs,),
        in_specs=[zcol(0), zcol(pairs), zcol(2 * pairs), pl.BlockSpec((nbr, 2, BLK, 2 * BLK), lambda p: (0, p, 0, 0))],
        out_specs=(pl.BlockSpec((s, LANES), lambda p: (0, p)), pl.BlockSpec((2, s, LANES), lambda p: (p, 0, 0))),
        scratch_shapes=[pltpu.VMEM((nbr, s, LANES), F32), pltpu.VMEM((nbr, 2, s, LANES), F32), pltpu.VMEM((nbr, 2, s, LANES), F32)],
        comm=comm,
    )(z, z, z, btiles)


def _attn_bwd(z, btiles, dyb, lse, delta, dbias_in, col0, name, comm=None):
    s = z.shape[0]
    nbr, heads = btiles.shape[:2]
    pairs = heads // 2
    scale = D_HEAD ** -0.5

    def body(q_ref, k_ref, v_ref, bt_ref, dy_ref, lse_ref, dl_ref, dbi_ref,
             dq_ref, dk_ref, dv_ref, db_ref, dqa, dka, dva):
        even = lax.broadcasted_iota(I32, (BLK, LANES), 1) < D_HEAD
        even2 = lax.broadcasted_iota(I32, (2 * BLK, LANES), 1) < D_HEAD
        for ref in (dqa, dka, dva):
            ref[...] = jnp.zeros((s, LANES), F32)
        db_ref[...] = dbi_ref[...]

        def visit(br, d, start, prev):
            kw = 2 * BLK if prev else BLK
            rows_q = _rows(start, BLK, d)
            rows_k = _rows(start - BLK * d, kw, d) if prev else rows_q
            qb = q_ref[rows_q, :]
            dyv = dy_ref[rows_q, :]
            kwin = k_ref[rows_k, :]
            kb = kwin.astype(BF16)
            vb = v_ref[rows_k, :].astype(BF16)
            ev_k = even2 if prev else even
            qm = jnp.concatenate([jnp.where(even, qb, 0.0), jnp.where(even, 0.0, qb)], axis=0).astype(BF16)
            dym = jnp.concatenate([jnp.where(even, dyv, 0.0), jnp.where(even, 0.0, dyv)], axis=0).astype(BF16)
            bias = [bt_ref[br, e] if prev else bt_ref[br, e, :, BLK:] for e in range(2)]
            sc = _dot_nt(qm, kb) * scale + jnp.concatenate(bias, axis=0)
            lt = jnp.concatenate([lse_ref.at[e][rows_q, :] for e in range(2)], axis=0)
            dt = jnp.concatenate([dl_ref.at[e][rows_q, :] for e in range(2)], axis=0)
            if prev:
                lt = jnp.concatenate([lt, lt], axis=1)
                dt = jnp.concatenate([dt, dt], axis=1)
            p = jnp.exp(sc - lt)
            ds = p * (_dot_nt(dym, vb) - dt)
            for e in range(2):
                if prev:
                    db_ref[br, e] += ds[e * BLK:(e + 1) * BLK]
                else:
                    db_ref[br, e, :, BLK:] += ds[e * BLK:(e + 1) * BLK]
            dsb = ds.astype(BF16)
            km = jnp.concatenate([jnp.where(ev_k, kwin, 0.0), jnp.where(ev_k, 0.0, kwin)], axis=0).astype(BF16)
            dqa[rows_q, :] += _dot(jnp.concatenate([dsb[:BLK], dsb[BLK:]], axis=1), km) * scale
            dka[rows_k, :] += _dot_tn(dsb, qm) * scale
            dva[rows_k, :] += _dot_tn(p.astype(BF16), dym)

        _attn_blocks(s, visit, ATTN_GROUP_BWD)
        dq_ref[...] = dqa[...].astype(BF16)
        dk_ref[...] = dka[...].astype(BF16)
        dv_ref[...] = dva[...].astype(BF16)

    def zcol(j):
        return pl.BlockSpec((s, LANES), lambda p, j=j: (0, col0 + j + p))

    col = pl.BlockSpec((s, LANES), lambda p: (0, p))
    stat = pl.BlockSpec((2, s, LANES), lambda p: (p, 0, 0))
    tile = pl.BlockSpec((nbr, 2, BLK, 2 * BLK), lambda p: (0, p, 0, 0))
    wide = _sds((s, pairs * LANES), BF16)
    return _pcall(
        body, name=name, out_shape=(wide, wide, wide, _sds(btiles.shape, F32)), grid=(pairs,),
        in_specs=[zcol(0), zcol(pairs), zcol(2 * pairs), tile, col, stat, stat, tile],
        out_specs=(col, col, col, tile),
        scratch_shapes=[pltpu.VMEM((s, LANES), F32) for _ in range(3)], comm=comm,
    )(z, z, z, btiles, dyb, lse, delta, dbias_in)


def _row(v):
    return v.reshape(1, -1)


class _Rides:
    def __init__(self):
        self.table, self.grads = {}, {}

    def add(self, name, build):
        self.table.setdefault(name, []).append(build)

    def get(self, name):
        comm = None
        for build in self.table.get(name, ()):
            comm = build() if comm is None else _both(comm, build())
        return comm

    def ready(self, key, g):
        self.grads[key] = g


class _LocalSchedule:
    def __init__(self):
        self.big = {}

    def fwd_comms(self, l):
        return _Rides()

    def bwd_comms(self, l):
        return _Rides()

    def after_bwd(self, l, grads):
        self.big[l] = grads


def _layer_fwd(l, x, h, wts, prm, btiles, comms):
    d = x.shape[1]
    wq = d // 4
    depth = prm["norm_mix_g"].shape[0]
    gout = prm["out_norm_g"][l]
    z = _mm_n(h, wts["in_t"], l, nt=True, tn=256, out_dtype=F32, name="in_proj", comm=comms.get("in_proj"))
    yan, ycn = _mix_fwd(z, prm["conv_a_w"][l], prm["conv_c_w"][l], _row(prm["conv_c_b"][l]), _row(prm["ln_c_g"][l]),
                        _row(prm["ln_c_b"][l]), _row(gout[:wq]), _row(gout[3 * wq:]), "mix_fwd")
    yb, lse = _attn_fwd(z, btiles, 3 * wq // LANES, "attn_fwd", comm=comms.get("attn_fwd"))
    y, x_mid, h2 = _out_proj(yan, yb, ycn, _row(gout[wq:3 * wq]), x, wts["out"], l, _row(prm["norm_ffn_g"][l]),
                             "out_proj", comm=comms.get("out_proj"))
    up = _mm_n(h2, wts["up"], l, nt=False, tn=512, out_dtype=BF16, name="up_proj", comm=comms.get("up_proj"))
    act = _ffn_act_fwd(up, prm["conv_f_w"][l], "ffn_act_fwd", comm=comms.get("ffn_act_fwd"))
    g_next = _row(prm["norm_mix_g"][l + 1]) if l + 1 < depth else None
    x_out, h_next = _down_proj(act, wts["down"], l, x_mid, g_next, "down_proj", comm=comms.get("down_proj"))
    return x_out, h_next, (x, h, z, yb, lse, y, x_mid, h2, up)


def _layer_bwd(l, dxo, dxo_b, saved, wts, prm, btiles, dbias, comms):
    x, h, z, yb, lse, y, x_mid, h2, up = saved
    d = x.shape[1]
    wq = d // 4
    gout = prm["out_norm_g"][l]
    dact = _mm_n(dxo_b, wts["down"], l, nt=True, tn=256, out_dtype=BF16, name="down_proj_dx", comm=comms.get("down_proj_dx"))
    act, dgate, dval, dwf = _ffn_act_bwd(up, dact, prm["conv_f_w"][l], "ffn_act_bwd", comm=comms.get("ffn_act_bwd"))
    g_down = _mm_tn(act, dxo_b, t=256, name="down_proj_dw")
    comms.ready("down", g_down)
    dxm, dxm_b, dg_ffn = _proj_dx([dgate, dval], wts["up"], l, True, x_mid, _row(prm["norm_ffn_g"][l]), dxo, "up_proj_dx",
                                  ROW_CHUNK, comm=comms.get("up_proj_dx"))
    g_up = _mm_tn2(h2, dgate, dval, t=256, name="up_proj_dw", comm=comms.get("up_proj_dw"))
    comms.ready("up", g_up)
    dy = _mm_n(dxm_b, wts["out"], l, nt=True, tn=256, out_dtype=F32, name="out_proj_dx")
    g_out = _mm_tn(y, dxm_b, t=256, name="out_proj_dw")
    comms.ready("out", g_out)
    dza, dzc, dwa, dwc, dcb, dlg, dlb, dga, dgc = _mix_bwd(
        z, dy, prm["conv_a_w"][l], prm["conv_c_w"][l], _row(prm["conv_c_b"][l]), _row(prm["ln_c_g"][l]),
        _row(prm["ln_c_b"][l]), _row(gout[:wq]), _row(gout[3 * wq:]), "mix_bwd", comm=comms.get("mix_bwd"))
    dyb, delta, dgb = _yb_norm_bwd(yb, dy, _row(gout[wq:3 * wq]), "yb_norm_bwd", comm=comms.get("yb_norm_bwd"))
    dq, dk, dv, dbias = _attn_bwd(z, btiles, dyb, lse, delta, dbias, 3 * wq // LANES, "attn_bwd",
                                  comm=comms.get("attn_bwd"))
    dz = [dza, dq, dk, dv, dzc]
    dx, dx_b, dg_mix = _proj_dx(dz, wts["in_t"], l, False, x, _row(prm["norm_mix_g"][l]), dxm, "in_proj_dx",
                                PROJ_ROWS, comm=comms.get("in_proj_dx"))
    g_in_t = _mm_tn_pieces(dz, h, t=256, name="in_proj_dw")
    big = {"in_t": g_in_t, "out": g_out, "up": g_up, "down": g_down}
    small = {"norm_mix_g": dg_mix[0], "conv_a_w": dwa, "conv_c_w": dwc, "conv_c_b": dcb[0], "ln_c_g": dlg[0],
             "ln_c_b": dlb[0], "out_norm_g": jnp.concatenate([dga[0], dgb[0], dgc[0]]), "norm_ffn_g": dg_ffn[0],
             "conv_f_w": dwf}
    return dx, dx_b, big, small, dbias


def _local_step(x, tgt, wts, prm, sched):
    depth = prm["norm_mix_g"].shape[0]
    buckets = jnp.asarray(_t5_bucket_table())
    btiles = _bias_tiles(prm["rel_bias"], buckets, "bias_tiles")
    saved = []
    h = _rms_fwd(x, _row(prm["norm_mix_g"][0]), "rms_mix_fwd")
    for l in range(depth):
        x, h, sv = _layer_fwd(l, x, h, wts, prm, btiles, sched.fwd_comms(l))
        saved.append(sv)
    loss, dx, dx_b, dg_final = _final_loss(x, _row(prm["final_g"]), tgt, "final_loss")
    dbias = jnp.zeros(btiles.shape, F32)
    small = [None] * depth
    for l in reversed(range(depth)):
        dx, dx_b, grads, small[l], dbias = _layer_bwd(l, dx, dx_b, saved[l], wts, prm, btiles, dbias, sched.bwd_comms(l))
        sched.after_bwd(l, grads)
    nbk, heads = prm["rel_bias"].shape
    d_rel = _bias_grad(dbias, buckets, nbk, "bias_grad")[:, :heads]
    return loss, dx, small, d_rel, dg_final[0]


BIG = ("in_t", "out", "up", "down")
COL_SHARDED = ("up",)
N_CHIPS = 4
N_DEV = 8
BF16_ROWS = 16


def _me():
    return lax.axis_index("x"), lax.axis_index("y"), lax.axis_index("c")


def _chip_of(x, y):
    return 2 * x + y


def _other_chips(x, y):
    return ((1 - x, y), (x, 1 - y), (1 - x, 1 - y))


def _remote(src, dst, send_sem, recv_sem, device):
    return pltpu.make_async_remote_copy(src_ref=src, dst_ref=dst, send_sem=send_sem, recv_sem=recv_sem,
                                        device_id=device, device_id_type=MESH)


ALL_FLIPS = (0, 1, 2)


def _ag_comm(wts, layer, ici_keys, fwd_keys):
    flips = {(k if isinstance(k, str) else k[0]): (ALL_FLIPS if isinstance(k, str) else k[1]) for k in ici_keys}
    keys = tuple(k for k in BIG if k in flips or k in fwd_keys)

    def geo(k):
        _, rows, cols = wts[k].shape
        return (rows, cols // N_CHIPS) if k in COL_SHARDED else (rows // N_CHIPS, cols)

    def copies(refs, sems):
        g = dict(zip(keys, refs))
        isend, irecv, dsend, drecv = sems
        x, y, c = _me()
        mine = _chip_of(x, y)

        def region(k, chip, half):
            r, cc = geo(k)
            h = r // 2
            if k in COL_SHARDED:
                return g[k].at[layer, pl.ds(pl.multiple_of(half * h, BF16_ROWS), h), pl.ds(pl.multiple_of(chip * cc, LANES), cc)]
            return g[k].at[layer, pl.ds(pl.multiple_of(chip * r + half * h, BF16_ROWS), h), :]

        def ici(k, f, landing):
            chip = _other_chips(x, y)[f]
            where = region(k, _chip_of(*chip) if landing else mine, c)
            i = keys.index(k)
            return _remote(where, where, isend.at[i, f], irecv.at[i, f], (*chip, c))

        def fwd(k, f, landing):
            chip = _other_chips(x, y)[f]
            where = region(k, _chip_of(*chip), 1 - c if landing else c)
            i = keys.index(k)
            return _remote(where, where, dsend.at[i, f], drecv.at[i, f], (x, y, 1 - c))

        return ici, fwd

    def start(ins, outs, sems):
        ici, fwd = copies(outs, sems)
        for k in keys:
            for f in flips.get(k, ALL_FLIPS):
                if k in flips:
                    ici(k, f, False).start()
                else:
                    fwd(k, f, False).start()

    def finish(ins, outs, sems):
        ici, fwd = copies(outs, sems)
        for k in keys:
            for f in flips.get(k, ()):
                ici(k, f, True).wait_recv()
                if k in fwd_keys:
                    fwd(k, f, False).start()
        for k in keys:
            for f in flips.get(k, ALL_FLIPS):
                if k in fwd_keys:
                    fwd(k, f, True).wait_recv()
                    fwd(k, f, False).wait_send()
                if k in flips:
                    ici(k, f, False).wait_send()

    def done(res):
        wts.update(zip(keys, res))

    n = len(keys)
    return _Comm([wts[k] for k in keys], [_sds(wts[k].shape, BF16) for k in keys], {i: i for i in range(n)},
                 [pltpu.SemaphoreType.DMA((n, 3)) for _ in range(4)], start, finish, done)


def _small_gather_comm(slab, store):
    def copies(ins, outs, sems):
        send, recv, lsem = sems
        x, y, c = _me()
        mine = _chip_of(x, y)
        own = pltpu.make_async_copy(ins[0], outs[0].at[mine], lsem)
        pairs = []
        for f, chip in enumerate(_other_chips(x, y)):
            out = _remote(ins[0], outs[0].at[mine], send.at[f], recv.at[f], (*chip, c))
            land = _remote(ins[0], outs[0].at[_chip_of(*chip)], send.at[f], recv.at[f], (*chip, c))
            pairs.append((out, land))
        return own, pairs

    def start(ins, outs, sems):
        own, pairs = copies(ins, outs, sems)
        own.start()
        for out, _ in pairs:
            out.start()

    def finish(ins, outs, sems):
        own, pairs = copies(ins, outs, sems)
        for out, land in pairs:
            land.wait_recv()
            out.wait_send()
        own.wait()

    def done(res):
        store["small"] = res[0]

    return _Comm([slab], [_sds((N_CHIPS,) + slab.shape, F32)], {},
                 [pltpu.SemaphoreType.DMA((3,)), pltpu.SemaphoreType.DMA((3,)), pltpu.SemaphoreType.DMA], start, finish, done)


def _piece_geo(g):
    geo = {}
    for k in g:
        rows, cols = g[k].shape
        geo[k] = (rows // 2, cols // N_CHIPS) if k in COL_SHARDED else (rows // (2 * N_CHIPS), cols)
    return geo


def _swap_comm(g, keys, done):
    geo = _piece_geo(g)
    n_copies = sum(N_CHIPS if k in COL_SHARDED else 1 for k in keys)

    def copies(ins, outs, sems):
        g_refs, t_refs = dict(zip(keys, ins)), dict(zip(keys, outs))
        send, recv = sems
        x, y, c = _me()
        pairs = []
        for k in keys:
            h, cc = geo[k]
            if k in COL_SHARDED:
                rows = pl.ds(pl.multiple_of((1 - c) * h, BF16_ROWS), h)
                pairs += [(g_refs[k].at[rows, pl.ds(j * cc, cc)], t_refs[k].at[j]) for j in range(N_CHIPS)]
            else:
                pairs.append((g_refs[k].at[:, 1 - c], t_refs[k]))
        return [_remote(src, dst, send.at[i], recv.at[i], (x, y, 1 - c)) for i, (src, dst) in enumerate(pairs)]

    def start(ins, outs, sems):
        for cp in copies(ins, outs, sems):
            cp.start()

    def finish(ins, outs, sems):
        for cp in copies(ins, outs, sems):
            cp.wait()

    ins = [g[k] if k in COL_SHARDED else g[k].reshape(N_CHIPS, 2, geo[k][0], geo[k][1]) for k in keys]
    return _Comm(ins, [_sds((N_CHIPS,) + geo[k], BF16) for k in keys], {},
                 [pltpu.SemaphoreType.DMA((n_copies,)) for _ in range(2)], start, finish,
                 lambda res: done(dict(zip(keys, res))))


def _pair_sum(g, theirs, c_arr, keys):
    geo = _piece_geo(g)

    def body(c_ref, *refs):
        nk = len(keys)
        for i in range(nk):
            refs[2 * nk + i][...] = (refs[i][...].astype(F32) + refs[nk + i][...].astype(F32)).astype(BF16)

    in_specs, ins = [], []
    for k in keys:
        h, cc = geo[k]
        if k in COL_SHARDED:
            in_specs.append(pl.BlockSpec((h, cc), lambda j, c_ref: (c_ref[0], j)))
            ins.append(g[k])
        else:
            in_specs.append(pl.BlockSpec((None, h, cc), lambda j, c_ref: (2 * j + c_ref[0], 0, 0)))
            ins.append(g[k].reshape(2 * N_CHIPS, h, cc))
    slab = [pl.BlockSpec((None,) + geo[k], lambda j, c_ref: (j, 0, 0)) for k in keys]
    res = _pcall(body, name="rs_pair_sum", out_shape=tuple(_sds((N_CHIPS,) + geo[k], BF16) for k in keys), grid=(N_CHIPS,),
                 in_specs=in_specs + slab, out_specs=tuple(slab), prefetch=1)(c_arr, *ins, *[theirs[k] for k in keys])
    return dict(zip(keys, res))


def _rs_comm(p, keys, store):
    def copies(ins, outs, sems):
        send, recv = sems
        x, y, c = _me()
        return [_remote(ins[i].at[_chip_of(*chip)], outs[i].at[f], send.at[i, f], recv.at[i, f], (*chip, c))
                for i in range(len(keys)) for f, chip in enumerate(_other_chips(x, y))]

    def start(ins, outs, sems):
        for cp in copies(ins, outs, sems):
            cp.start()

    def finish(ins, outs, sems):
        for cp in copies(ins, outs, sems):
            cp.wait()

    def done(res):
        store.update(zip(keys, res))

    return _Comm([p[k] for k in keys], [_sds((3,) + p[k].shape[1:], BF16) for k in keys], {},
                 [pltpu.SemaphoreType.DMA((len(keys), 3)) for _ in range(2)], start, finish, done)


def _quad_sum(p, b, where, l, full):
    parts = 2
    nk = len(BIG)

    def body(where_ref, *refs):
        for i in range(nk):
            acc = refs[i][...].astype(F32)
            for f in range(3):
                acc = acc + refs[nk + 3 * i + f][...].astype(F32)
            refs[5 * nk + i][...] = acc

    own, recv, outs = [], [], []
    for k in BIG:
        h, cc = p[k].shape[1:]
        th = h // parts
        own.append(pl.BlockSpec((None, th, cc), lambda i, w_ref: (w_ref[0], i, 0)))
        recv += [pl.BlockSpec((None, th, cc), lambda i, w_ref, f=f: (f, i, 0)) for f in range(3)]
        outs.append(pl.BlockSpec((None, None, th, cc), lambda i, w_ref: (l, w_ref[1], i, 0)))
    args = [p[k] for k in BIG] + [b[k] for k in BIG for _ in range(3)] + [full[k] for k in BIG]
    res = _pcall(body, name="rs_quad_sum", out_shape=tuple(_sds(full[k].shape, F32) for k in BIG), grid=(parts,),
                 in_specs=own + recv + [ANY] * nk, out_specs=tuple(outs), prefetch=1,
                 aliases={1 + 4 * nk + i: i for i in range(nk)})(where, *args)
    return dict(zip(BIG, res))


def _share_comm(layers, full, done):
    nk = len(BIG)

    def copies(outs, sems, landing):
        send, recv = sems
        x, y, c = _me()
        half = 1 - c if landing else c
        return [_remote(outs[i].at[l, half], outs[i].at[l, half], send.at[i, j], recv.at[i, j], (x, y, 1 - c))
                for i in range(nk) for j, l in enumerate(layers)]

    def start(ins, outs, sems):
        for cp in copies(outs, sems, False):
            cp.start()

    def finish(ins, outs, sems):
        for cp in copies(outs, sems, True):
            cp.wait_recv()
        for cp in copies(outs, sems, False):
            cp.wait_send()

    return _Comm([full[k] for k in BIG], [_sds(full[k].shape, F32) for k in BIG], {i: i for i in range(nk)},
                 [pltpu.SemaphoreType.DMA((nk, len(layers))) for _ in range(2)], start, finish,
                 lambda res: done(dict(zip(BIG, res))))


def _gather_comm(slab, done):
    def copies(ins, outs, sems, landing):
        send, recv = sems
        x, y, c = _me()
        me = 4 * x + 2 * y + c
        out = []
        for mask in range(1, N_DEV):
            peer = (x ^ (mask >> 2), y ^ ((mask >> 1) & 1), c ^ (mask & 1))
            slot = 4 * peer[0] + 2 * peer[1] + peer[2] if landing else me
            out.append(_remote(ins[0], outs[0].at[slot], send.at[mask - 1], recv.at[mask - 1], peer))
        return out

    def start(ins, outs, sems):
        for cp in copies(ins, outs, sems, False):
            cp.start()

    def finish(ins, outs, sems):
        for cp in copies(ins, outs, sems, True):
            cp.wait_recv()
        for cp in copies(ins, outs, sems, False):
            cp.wait_send()

    return _Comm([slab], [_sds((N_DEV,) + slab.shape, F32)], {},
                 [pltpu.SemaphoreType.DMA((N_DEV - 1,)), pltpu.SemaphoreType.DMA((N_DEV - 1,))], start, finish,
                 lambda res: done(res[0]))


def _sum_slabs(slabs, own, me):
    n, r, lanes = slabs.shape
    tr = r // 2

    def body(me_ref, s_ref, own_ref, o_ref):
        o_ref[...] = jnp.zeros((tr, lanes), F32)
        for i in range(n):
            @pl.when(me_ref[0] == i)
            def _():
                o_ref[...] += own_ref[...]

            @pl.when(me_ref[0] != i)
            def _():
                o_ref[...] += s_ref[i]

    return _pcall(body, name="sum_partials", out_shape=_sds((r, lanes), F32), grid=(2,),
                  in_specs=[pl.BlockSpec((n, tr, lanes), lambda i, me_ref: (0, i, 0)),
                            pl.BlockSpec((tr, lanes), lambda i, me_ref: (i, 0))],
                  out_specs=pl.BlockSpec((tr, lanes), lambda i, me_ref: (i, 0)), prefetch=1)(me, slabs, own)


def _cast_into_gathered(w, chip, by_cols, name):
    l, r, c = w.shape

    def body(chip_ref, w_ref, o_ref):
        o_ref[...] = w_ref[...].astype(BF16)

    if by_cols:
        shape, out = (l, r, N_CHIPS * c), pl.BlockSpec((None, r, c), lambda i, chip_ref: (i, 0, chip_ref[0]))
    else:
        shape, out = (l, N_CHIPS * r, c), pl.BlockSpec((None, r, c), lambda i, chip_ref: (i, chip_ref[0], 0))
    return _pcall(body, name=name, out_shape=_sds(shape, BF16), grid=(l,),
                  in_specs=[pl.BlockSpec((None, r, c), lambda i, chip_ref: (i, 0, 0))], out_specs=out, prefetch=1)(chip, w)


def _adamw_math(w, g, m, v):
    mn = ADAM_B1 * m + (1.0 - ADAM_B1) * g
    vn = ADAM_B2 * v + (1.0 - ADAM_B2) * (g * g)
    m_hat = mn / (1.0 - ADAM_B1 ** ADAM_STEP)
    v_hat = vn / (1.0 - ADAM_B2 ** ADAM_STEP)
    return -ADAM_LR * (m_hat / (jnp.sqrt(v_hat) + ADAM_EPS) + ADAM_WD * w), mn, vn


def _adamw(w, g, m, v, name, tr):
    r, c = w.shape

    def body(w_ref, g_ref, m_ref, v_ref, go_ref, d_ref, mo_ref, vo_ref):
        gv = g_ref[...]
        go_ref[...] = gv
        d_ref[...], mo_ref[...], vo_ref[...] = _adamw_math(w_ref[...], gv, m_ref[...], v_ref[...])

    blk = pl.BlockSpec((tr, c), lambda i: (i, 0))
    return _pcall(body, name=name, out_shape=tuple(_sds((r, c), F32) for _ in range(4)), grid=(r // tr,),
                  in_specs=[blk] * 4, out_specs=(blk, blk, blk, blk))(w, g, m, v)


def _adamw_small(groups, name):
    count = len(groups[0])
    shapes = [a.shape for a in groups[0]]
    as2d = [(math.prod(s[:-1]), s[-1]) for s in shapes]

    def body(*refs):
        for i in range(count):
            out = _adamw_math(*[refs[j * count + i][...] for j in range(4)])
            for j in range(3):
                refs[(4 + j) * count + i][...] = out[j]

    specs = [pl.BlockSpec(s, lambda i: (0, 0)) for s in as2d]
    res = _pcall(body, name=name, out_shape=tuple(_sds(s, F32) for _ in range(3) for s in as2d), grid=(1,),
                 in_specs=specs * 4, out_specs=tuple(specs * 3))(*[a.reshape(s) for grp in groups for a, s in zip(grp, as2d)])
    return [[res[j * count + i].reshape(shapes[i]) for i in range(count)] for j in range(3)]


AG_RIDES = {"in_proj": (0, ("out",), ("down",)), "attn_fwd": (0, (("up", (0, 1)),), ("out",)), "out_proj": (0, (), ("up",)),
            "up_proj": (1, ("in_t",), ()), "ffn_act_fwd": (1, ("down",), ()),
            "down_proj": (1, (("up", (2,)),), ("in_t",))}
AG_FIRST = ("in_t", "down")
RS_RIDES = {"down_proj_dx": "swap", "ffn_act_bwd": ("up",), "up_proj_dx": "share", "attn_bwd": ("in_t", "out", "down")}
RS_RIDES_LAST = {"down_proj_dx": "swap", "ffn_act_bwd": ("up",), "up_proj_dx": "share", "up_proj_dw": ("in_t",),
                 "mix_bwd": ("down",), "yb_norm_bwd": ("out",)}
EARLY = ("out", "up", "down")
EARLY_RIDES = {"mix_bwd": "swap", "attn_bwd": ("up", "down"), "in_proj_dx": ("out",)}


class _Reduction:
    def __init__(self, grads):
        self.grads, self.pairs, self.recv = grads, {}, {}


class _MeshSchedule:
    def __init__(self, wts, depth, c_arr, where):
        self.wts, self.depth, self.c_arr, self.where = wts, depth, c_arr, where
        self.pending, self.last, self.full, self.unshared = None, None, None, []

    def fwd_comms(self, l):
        rides = _Rides()
        for name, (off, ici, fwd) in AG_RIDES.items():
            if l + off == 0:
                ici = tuple(k if isinstance(k, str) else k[0] for k in ici)
                ici, fwd = (tuple(k for k in keys if k not in AG_FIRST) for keys in (ici, fwd))
            if l + off < self.depth and (ici or fwd):
                rides.add(name, lambda ride=(l + off, ici, fwd): _ag_comm(self.wts, *ride))
        return rides

    def _shared(self, full):
        self.full, self.unshared = full, []

    def _ride(self, red, what, swap_keys):
        if what == "swap":
            return _swap_comm(red.grads, swap_keys, lambda theirs: red.pairs.update(
                _pair_sum(red.grads, theirs, self.c_arr, swap_keys)))
        if what == "share":
            return _share_comm(self.unshared, self.full, self._shared)
        return _rs_comm(red.pairs, what, red.recv)

    def bwd_comms(self, l):
        rides = _Rides()
        if self.pending is not None:
            for name, what in (RS_RIDES_LAST if l == 0 else RS_RIDES).items():
                if what != "share" or self.unshared:
                    rides.add(name, lambda what=what, red=self.pending: self._ride(red, what, BIG))
        if l == 0:
            self.last = _Reduction(rides.grads)
            for name, what in EARLY_RIDES.items():
                rides.add(name, lambda what=what: self._ride(self.last, what, EARLY))
        return rides

    def _reduce(self, l, red):
        self.full = _quad_sum(red.pairs, red.recv, self.where, l, self.full)
        self.unshared = self.unshared + [l]

    def after_bwd(self, l, grads):
        if self.pending is not None:
            self._reduce(l + 1, self.pending)
        if self.full is None:
            geo = _piece_geo(grads)
            self.full = {k: jnp.zeros((self.depth, 2) + geo[k], F32) for k in BIG}
        self.pending = _Reduction(grads) if l > 0 else None
        if l == 0:
            self.last.grads = grads

    def finish(self, extra):
        red = self.last
        late = tuple(k for k in BIG if k not in red.pairs)
        _run_comm(self._ride(red, "swap", late), "rs_swap_halves")
        _run_comm(_both(_rs_comm(red.pairs, late, red.recv), extra), "rs_to_owners")
        self._reduce(0, red)
        _run_comm(_share_comm(self.unshared, self.full, self._shared), "rs_share")
        return self.full


SHARDED_SMALL = ("conv_a_w", "conv_c_w", "conv_f_w")
SMALL = ("norm_mix_g", "conv_a_w", "conv_c_w", "conv_c_b", "ln_c_g", "ln_c_b", "out_norm_g", "norm_ffn_g",
         "conv_f_w", "rel_bias", "final_g")
SLAB_ROWS = 16


def _pack(arrays):
    flat = jnp.concatenate([a.reshape(-1) for a in arrays])
    unit = SLAB_ROWS * LANES
    total = -(-flat.shape[0] // unit) * unit
    return jnp.pad(flat, (0, total - flat.shape[0])).reshape(-1, LANES)


def _unpack(slab, shapes):
    flat = slab.reshape(-1)
    out, off = [], 0
    for shp in shapes:
        size = math.prod(shp)
        out.append(flat[off:off + size].reshape(shp))
        off += size
    return out


def kernel(x, norm_mix_g, w_in, conv_a_w, conv_c_w, conv_c_b, ln_c_g, ln_c_b, out_norm_g, w_out, norm_ffn_g, w_up, conv_f_w, w_down, rel_bias, final_g, loss_target, m_norm_mix_g, m_w_in, m_conv_a_w, m_conv_c_w, m_conv_c_b, m_ln_c_g, m_ln_c_b, m_out_norm_g, m_w_out, m_norm_ffn_g, m_w_up, m_conv_f_w, m_w_down, m_rel_bias, m_final_g, v_norm_mix_g, v_w_in, v_conv_a_w, v_conv_c_w, v_conv_c_b, v_ln_c_g, v_ln_c_b, v_out_norm_g, v_w_out, v_norm_ffn_g, v_w_up, v_conv_f_w, v_w_down, v_rel_bias, v_final_g):
    weights = dict(norm_mix_g=norm_mix_g, w_in=w_in, conv_a_w=conv_a_w, conv_c_w=conv_c_w, conv_c_b=conv_c_b,
                   ln_c_g=ln_c_g, ln_c_b=ln_c_b, out_norm_g=out_norm_g, w_out=w_out, norm_ffn_g=norm_ffn_g, w_up=w_up,
                   conv_f_w=conv_f_w, w_down=w_down, rel_bias=rel_bias, final_g=final_g)
    mom_m = dict(norm_mix_g=m_norm_mix_g, w_in=m_w_in, conv_a_w=m_conv_a_w, conv_c_w=m_conv_c_w, conv_c_b=m_conv_c_b,
                 ln_c_g=m_ln_c_g, ln_c_b=m_ln_c_b, out_norm_g=m_out_norm_g, w_out=m_w_out, norm_ffn_g=m_norm_ffn_g,
                 w_up=m_w_up, conv_f_w=m_conv_f_w, w_down=m_w_down, rel_bias=m_rel_bias, final_g=m_final_g)
    mom_v = dict(norm_mix_g=v_norm_mix_g, w_in=v_w_in, conv_a_w=v_conv_a_w, conv_c_w=v_conv_c_w, conv_c_b=v_conv_c_b,
                 ln_c_g=v_ln_c_g, ln_c_b=v_ln_c_b, out_norm_g=v_out_norm_g, w_out=v_w_out, norm_ffn_g=v_norm_ffn_g,
                 w_up=v_w_up, conv_f_w=v_conv_f_w, w_down=v_w_down, rel_bias=v_rel_bias, final_g=v_final_g)
    xi, yi, ci = _me()
    chip = _chip_of(xi, yi)
    c_arr = jnp.reshape(ci, (1,)).astype(I32)
    chip_arr = jnp.reshape(chip, (1,)).astype(I32)
    me_arr = jnp.reshape(4 * xi + 2 * yi + ci, (1,)).astype(I32)
    where = jnp.stack([chip, ci]).astype(I32)
    depth = w_out.shape[0]

    wts = {"in_t": _cast_into_gathered(jnp.swapaxes(w_in, 1, 2), chip_arr, False, "cast_in"),
           "out": _cast_into_gathered(w_out, chip_arr, False, "cast_out"),
           "up": _cast_into_gathered(w_up, chip_arr, True, "cast_up"),
           "down": _cast_into_gathered(w_down, chip_arr, False, "cast_down")}
    store = {}
    _run_comm(_both(_ag_comm(wts, 0, AG_FIRST, AG_FIRST),
                    _small_gather_comm(_pack([weights[n] for n in SHARDED_SMALL]), store)), "ag_weights")
    prm = {n: weights[n] for n in SMALL if n not in SHARDED_SMALL}
    per_chip = [_unpack(store["small"][j], [weights[n].shape for n in SHARDED_SMALL]) for j in range(N_CHIPS)]
    for i, n in enumerate(SHARDED_SMALL):
        prm[n] = jnp.concatenate([per_chip[j][i] for j in range(N_CHIPS)], axis=-1)

    sched = _MeshSchedule(wts, depth, c_arr, where)
    loss_row, dx, small, d_rel, d_final = _local_step(x[0], loss_target[0], wts, prm, sched)
    loss = lax.psum(loss_row[0, 0], ("x", "y", "c"))

    stacked = {n: jnp.stack([small[l][n] for l in range(depth)]) for n in small[0]}
    stacked["rel_bias"] = d_rel
    stacked["final_g"] = d_final
    full_shapes = [stacked[n].shape for n in SMALL]
    partial = _pack([stacked[n] for n in SMALL])
    reduced = sched.finish(_gather_comm(partial, lambda res: store.update(partials=res)))

    grads = {}
    shard_shapes = {"in_t": jnp.swapaxes(w_in, 1, 2).shape, "out": w_out.shape, "up": w_up.shape, "down": w_down.shape}
    red = {k: reduced[k].reshape(shard_shapes[k]) for k in BIG}
    grads["w_in"] = jnp.swapaxes(red["in_t"], 1, 2)
    grads["w_out"], grads["w_up"], grads["w_down"] = red["out"], red["up"], red["down"]
    delta, new_m, new_v = {}, {}, {}
    for n in ("w_in", "w_out", "w_up", "w_down"):
        shp = weights[n].shape
        flat = lambda a, shp=shp: a.reshape(shp[0] * shp[1], shp[2])
        tile = max(t for t in range(8, 257, 8) if shp[1] % t == 0)
        g, d, mn, vn = _adamw(flat(weights[n]), flat(grads[n]), flat(mom_m[n]), flat(mom_v[n]), "adamw_" + n, tile)
        grads[n], delta[n], new_m[n], new_v[n] = g.reshape(shp), d.reshape(shp), mn.reshape(shp), vn.reshape(shp)

    summed = _unpack(_sum_slabs(store["partials"], partial, me_arr), full_shapes)
    for n, g in zip(SMALL, summed):
        if n in SHARDED_SMALL:
            width = weights[n].shape[-1]
            g = lax.dynamic_slice_in_dim(g, chip * width, width, axis=g.ndim - 1)
        grads[n] = g
    res = _adamw_small([[src[n] for n in SMALL] for src in (weights, grads, mom_m, mom_v)], "adamw_small")
    for i, n in enumerate(SMALL):
        delta[n], new_m[n], new_v[n] = res[0][i], res[1][i], res[2][i]

    order = ("norm_mix_g", "w_in", "conv_a_w", "conv_c_w", "conv_c_b", "ln_c_g", "ln_c_b", "out_norm_g", "w_out",
             "norm_ffn_g", "w_up", "conv_f_w", "w_down", "rel_bias", "final_g")
    return (loss, dx[None], *[grads[n] for n in order], *[delta[n] for n in order], *[new_m[n] for n in order],
            *[new_v[n] for n in order])
```

```python
import functools
import math

import numpy as np
import jax
import jax.numpy as jnp
from jax import lax
from jax.experimental import pallas as pl
from jax.experimental.pallas import tpu as pltpu

F32 = jnp.float32
BF16 = jnp.bfloat16
I32 = jnp.int32

EPS = 1e-6
NEG = -1e30
D_HEAD = 64
LANES = 128
BLK = 128
ATTN_GROUP_FWD = 16
ATTN_GROUP_BWD = 16
DILATED_BRANCHES = ((128, 1), (512, 4), (2048, 16))
NUM_BUCKETS = 32
MAX_DISTANCE = 2048
SHORT_CONV = 3
CONFORMER_CONV = 31
FFN_CONV = 3
PAD_SHORT = 8
PAD_LONG = 32
ROW_CHUNK = 256
PROJ_ROWS = 512
FFN_ROWS = 64
V7X_VMEM_BYTES = 64 * 1024 * 1024
VMEM_REQUEST = V7X_VMEM_BYTES * 7 // 8

ADAM_LR = 0.001
ADAM_B1 = 0.9
ADAM_B2 = 0.999
ADAM_EPS = 1e-08
ADAM_WD = 0.01
ADAM_STEP = 10

MESH = pl.DeviceIdType.MESH
ANY = pl.BlockSpec(memory_space=pl.ANY)


def _sds(shape, dtype):
    return jax.ShapeDtypeStruct(tuple(shape), dtype)


class _Comm:
    def __init__(self, ins, out_shapes, aliases, sems, start, finish, done):
        self.ins, self.out_shapes, self.aliases, self.sems = list(ins), list(out_shapes), dict(aliases), list(sems)
        self.start, self.finish, self.done = start, finish, done


def _pcall(body, *, name, out_shape, grid=(), in_specs=None, out_specs=None, scratch_shapes=(), vmem=VMEM_REQUEST,
           aliases=None, prefetch=0, comm=None):
    params = pltpu.CompilerParams(dimension_semantics=("arbitrary",) * len(grid), vmem_limit_bytes=vmem)
    single = not isinstance(out_shape, (tuple, list))
    outs = [out_shape] if single else list(out_shape)
    ospecs = [out_specs] if single else list(out_specs)
    ispecs, scratch, aliases = list(in_specs), list(scratch_shapes), dict(aliases or {})
    n_in, n_out, n_scr = len(ispecs), len(outs), len(scratch)
    kernel_body = body
    if comm is not None:
        n_ci, n_co = len(comm.ins), len(comm.out_shapes)

        def kernel_body(*refs):
            pre, rest = refs[:prefetch], refs[prefetch:]
            core_in, c_in = rest[:n_in], rest[n_in:n_in + n_ci]
            o0 = n_in + n_ci
            core_out, c_out = rest[o0:o0 + n_out], rest[o0 + n_out:o0 + n_out + n_co]
            s0 = o0 + n_out + n_co
            core_scr, c_sem = rest[s0:s0 + n_scr], rest[s0 + n_scr:]
            first = functools.reduce(jnp.logical_and, [pl.program_id(a) == 0 for a in range(len(grid))])
            last = functools.reduce(jnp.logical_and, [pl.program_id(a) == grid[a] - 1 for a in range(len(grid))])
            pl.when(first)(lambda: comm.start(c_in, c_out, c_sem))
            body(*pre, *core_in, *core_out, *core_scr)
            pl.when(last)(lambda: comm.finish(c_in, c_out, c_sem))

        for i, o in comm.aliases.items():
            aliases[prefetch + n_in + i] = n_out + o
        ispecs += [ANY] * n_ci
        ospecs += [ANY] * n_co
        outs += comm.out_shapes
        scratch += comm.sems
    if prefetch:
        spec = pltpu.PrefetchScalarGridSpec(num_scalar_prefetch=prefetch, grid=grid, in_specs=ispecs,
                                            out_specs=tuple(ospecs), scratch_shapes=scratch)
        call = pl.pallas_call(kernel_body, name=name, out_shape=tuple(outs), grid_spec=spec,
                              input_output_aliases=aliases, compiler_params=params)
    else:
        call = pl.pallas_call(kernel_body, name=name, out_shape=tuple(outs), grid=grid, in_specs=ispecs,
                              out_specs=tuple(ospecs), scratch_shapes=scratch, input_output_aliases=aliases,
                              compiler_params=params)

    def run(*args):
        res = call(*args, *(comm.ins if comm is not None else ()))
        if comm is not None:
            comm.done(res[n_out:])
        return res[0] if single else tuple(res[:n_out])

    return run


def _both(a, b):
    def split(refs, na):
        return refs[:na], refs[na:]

    def run(which):
        def go(ins, outs, sems):
            for comm, i, o, s in zip((a, b), split(ins, len(a.ins)), split(outs, len(a.out_shapes)), split(sems, len(a.sems))):
                getattr(comm, which)(i, o, s)
        return go

    def done(res):
        a.done(res[:len(a.out_shapes)])
        b.done(res[len(a.out_shapes):])

    aliases = dict(a.aliases)
    aliases.update({len(a.ins) + i: len(a.out_shapes) + o for i, o in b.aliases.items()})
    return _Comm(a.ins + b.ins, a.out_shapes + b.out_shapes, aliases, a.sems + b.sems, run("start"), run("finish"), done)


def _run_comm(comm, name):
    def body(*refs):
        n_ci, n_co = len(comm.ins), len(comm.out_shapes)
        c_in, c_out, c_sem = refs[:n_ci], refs[n_ci:n_ci + n_co], refs[n_ci + n_co:]
        comm.start(c_in, c_out, c_sem)
        comm.finish(c_in, c_out, c_sem)

    res = pl.pallas_call(body, name=name, out_shape=tuple(comm.out_shapes), in_specs=[ANY] * len(comm.ins),
                         out_specs=tuple([ANY] * len(comm.out_shapes)), scratch_shapes=comm.sems,
                         input_output_aliases=comm.aliases)(*comm.ins)
    comm.done(res)


def _dot(a, b):
    return lax.dot_general(a, b, (((1,), (0,)), ((), ())), preferred_element_type=F32)


def _dot_nt(a, b):
    return lax.dot_general(a, b, (((1,), (1,)), ((), ())), preferred_element_type=F32)


def _dot_tn(a, b):
    return lax.dot_general(a, b, (((0,), (0,)), ((), ())), preferred_element_type=F32)


def _sigmoid(x):
    return 1.0 / (1.0 + jnp.exp(-x))


def _rstd(x):
    return lax.rsqrt(jnp.mean(x * x, axis=-1, keepdims=True) + EPS)


def _rms_fwd(x, g, name):
    s, d = x.shape
    tm = ROW_CHUNK

    def body(x_ref, g_ref, o_ref):
        xv = x_ref[...]
        o_ref[...] = (xv * _rstd(xv) * g_ref[...]).astype(BF16)

    return _pcall(body, name=name, out_shape=_sds((s, d), BF16), grid=(s // tm,),
                  in_specs=[pl.BlockSpec((tm, d), lambda i: (i, 0)), pl.BlockSpec((1, d), lambda i: (0, 0))],
                  out_specs=pl.BlockSpec((tm, d), lambda i: (i, 0)))(x, g)


def _final_loss(x, g, tgt, name):
    s, d = x.shape
    tm = ROW_CHUNK

    def body(x_ref, g_ref, t_ref, loss_ref, dx_ref, dxb_ref, dg_ref):
        i = pl.program_id(0)
        xv = x_ref[...]
        r = _rstd(xv)
        xh = xv * r
        e = xh * g_ref[...] - t_ref[...]
        lpart = 0.5 * jnp.sum(jnp.mean(e * e, axis=-1, keepdims=True), axis=0, keepdims=True)
        dy = e * (1.0 / d)
        gd = dy * g_ref[...]
        dx = r * (gd - xh * jnp.mean(gd * xh, axis=-1, keepdims=True))
        dx_ref[...] = dx
        dxb_ref[...] = dx.astype(BF16)
        part = jnp.sum(dy * xh, axis=0, keepdims=True)
        lrow = jnp.broadcast_to(lpart, (1, LANES))

        @pl.when(i == 0)
        def _():
            dg_ref[...] = part
            loss_ref[...] = lrow

        @pl.when(i > 0)
        def _():
            dg_ref[...] += part
            loss_ref[...] += lrow

    row = pl.BlockSpec((tm, d), lambda i: (i, 0))
    vec = pl.BlockSpec((1, d), lambda i: (0, 0))
    return _pcall(body, name=name,
                  out_shape=(_sds((1, LANES), F32), _sds((s, d), F32), _sds((s, d), BF16), _sds((1, d), F32)),
                  grid=(s // tm,), in_specs=[row, vec, row],
                  out_specs=(pl.BlockSpec((1, LANES), lambda i: (0, 0)), row, row, vec))(x, g, tgt)


def _mm_n(a, b, layer, *, nt, tn, out_dtype, name, resid=None, b_part=0, comm=None):
    s, k = a.shape
    n = b.shape[1] if nt else b.shape[2]
    rows = 512

    def body(a_ref, b_ref, *refs):
        o_ref = refs[-1]
        bv = b_ref[...]
        for r0 in range(0, s, rows):
            av = a_ref[r0:r0 + rows, :]
            prod = _dot_nt(av, bv) if nt else _dot(av, bv)
            if resid is not None:
                prod = refs[0][r0:r0 + rows, :] + prod
            o_ref[r0:r0 + rows, :] = prod.astype(out_dtype)

    b_spec = (pl.BlockSpec((None, tn, k), lambda j: (layer, j, b_part)) if nt
              else pl.BlockSpec((None, k, tn), lambda j: (layer, b_part, j)))
    col = pl.BlockSpec((s, tn), lambda j: (0, j))
    extra = () if resid is None else (resid,)
    return _pcall(body, name=name, out_shape=_sds((s, n), out_dtype), grid=(n // tn,),
                  in_specs=[pl.BlockSpec((s, k), lambda j: (0, 0)), b_spec] + [col] * len(extra),
                  out_specs=col, comm=comm)(a, b, *extra)


def _mm_tn(a, b, *, t, name):
    s, ka = a.shape
    n = b.shape[1]

    def body(a_ref, b_ref, o_ref):
        o_ref[...] = _dot_tn(a_ref[...], b_ref[...]).astype(BF16)

    return _pcall(body, name=name, out_shape=_sds((ka, n), BF16), grid=(ka // t,),
                  in_specs=[pl.BlockSpec((s, t), lambda i: (0, i)), pl.BlockSpec((s, n), lambda i: (0, 0))],
                  out_specs=pl.BlockSpec((t, n), lambda i: (i, 0)))(a, b)


def _mm_tn_pieces(pieces, b, *, t, name):
    s, n = b.shape
    blocks = [p.shape[1] // t for p in pieces]
    starts = [sum(blocks[:i]) for i in range(len(pieces))]

    def body(*refs):
        p_refs, b_ref, o_ref = refs[:len(pieces)], refs[len(pieces)], refs[len(pieces) + 1]
        j = pl.program_id(0)
        for p_ref, start, count in zip(p_refs, starts, blocks):
            @pl.when((j >= start) & (j < start + count))
            def _(p_ref=p_ref):
                o_ref[...] = _dot_tn(p_ref[...], b_ref[...]).astype(BF16)

    specs = [pl.BlockSpec((s, t), lambda j, start=start, count=count: (0, jnp.clip(j - start, 0, count - 1)))
             for start, count in zip(starts, blocks)]
    return _pcall(body, name=name, out_shape=_sds((sum(blocks) * t, n), BF16), grid=(sum(blocks),),
                  in_specs=specs + [pl.BlockSpec((s, n), lambda j: (0, 0))],
                  out_specs=pl.BlockSpec((t, n), lambda j: (j, 0)))(*pieces, b)


def _mm_tn2(a, b_lo, b_hi, *, t, name, comm=None):
    s, ka = a.shape
    half = b_lo.shape[1]
    nb = half // t

    def body(a_ref, lo_ref, hi_ref, o_ref):
        j = pl.program_id(0)

        @pl.when(j < nb)
        def _():
            o_ref[...] = _dot_tn(a_ref[...], lo_ref[...]).astype(BF16)

        @pl.when(j >= nb)
        def _():
            o_ref[...] = _dot_tn(a_ref[...], hi_ref[...]).astype(BF16)

    return _pcall(body, name=name, out_shape=_sds((ka, 2 * half), BF16), grid=(2 * nb,),
                  in_specs=[pl.BlockSpec((s, ka), lambda j: (0, 0)),
                            pl.BlockSpec((s, t), lambda j: (0, jnp.minimum(j, nb - 1))),
                            pl.BlockSpec((s, t), lambda j: (0, jnp.maximum(j - nb, 0)))],
                  out_specs=pl.BlockSpec((ka, t), lambda j: (0, j)), comm=comm)(a, b_lo, b_hi)


SUBLANES = 8


def _tap_windows(win, width, lead, rows):
    offs = [lead + k for k in range(width)]
    if width <= SUBLANES:
        return [win[o:o + rows, :] for o in offs]
    n = win.shape[0]
    out = {}
    for r in sorted({o % SUBLANES for o in offs}):
        base = win if r == 0 else pltpu.roll(win, n - r, axis=0)
        for o in offs:
            if o % SUBLANES == r:
                out[o - lead] = base[o - r:o - r + rows, :]
    return [out[k] for k in range(width)]


def _conv_taps(taps, w_ref):
    acc = None
    for k, tap in enumerate(taps):
        term = w_ref[pl.ds(k, 1), :] * tap
        acc = term if acc is None else acc + term
    return acc


def _causal_taps(win, width, pad, rows):
    return _tap_windows(win, width, pad - (width - 1), rows)


def _anticausal_taps(win, width, rows):
    return _tap_windows(win, width, 0, rows)[::-1]


def _conv_wgrad(dw_ref, g, taps):
    for k, tap in enumerate(taps):
        dw_ref[pl.ds(k, 1), :] += jnp.sum(g * tap, axis=0, keepdims=True)


def _mixer_a_fwd(ab, taps_t, wa_ref):
    ct = _conv_taps(taps_t, wa_ref)
    return ab * ct, ct


def _mixer_c_fwd(taps_u, wc_ref, cb_ref, lg_ref, lb_ref):
    u = _conv_taps(taps_u, wc_ref) + cb_ref[...]
    mu = jnp.mean(u, axis=-1, keepdims=True)
    uc = u - mu
    rs = lax.rsqrt(jnp.mean(uc * uc, axis=-1, keepdims=True) + EPS)
    uh = uc * rs
    ln = uh * lg_ref[...] + lb_ref[...]
    sg = _sigmoid(ln)
    return ln * sg, ln, sg, uh, rs


def _mix_fwd(z, wa, wc, cb, lg, lb, ga, gc, name):
    s = z.shape[0]
    w = wa.shape[1]
    nblk = z.shape[1] // w
    rc = ROW_CHUNK

    def body(ah_ref, ab_ref, ac_ref, cv_ref, cg_ref, wa_ref, wc_ref, cb_ref, lg_ref, lb_ref, ga_ref, gc_ref,
             ya_ref, yc_ref, tpad, upad):
        tpad[pl.ds(0, PAD_SHORT), :] = jnp.zeros((PAD_SHORT, w), F32)
        upad[pl.ds(0, PAD_LONG), :] = jnp.zeros((PAD_LONG, w), F32)

        def chunk(i, carry):
            base = pl.multiple_of(i * rc, rc)
            rows = pl.ds(base, rc)
            ah, ab, ac = ah_ref[rows, :], ab_ref[rows, :], ac_ref[rows, :]
            tpad[pl.ds(base + PAD_SHORT, rc), :] = ac * ah
            ya, _ = _mixer_a_fwd(ab, _causal_taps(tpad[pl.ds(base, rc + PAD_SHORT), :], SHORT_CONV, PAD_SHORT, rc), wa_ref)
            ya_ref[rows, :] = (ya * _rstd(ya) * ga_ref[...]).astype(BF16)
            upad[pl.ds(base + PAD_LONG, rc), :] = cv_ref[rows, :] * _sigmoid(cg_ref[rows, :])
            taps_u = _causal_taps(upad[pl.ds(base, rc + PAD_LONG), :], CONFORMER_CONV, PAD_LONG, rc)
            yc = _mixer_c_fwd(taps_u, wc_ref, cb_ref, lg_ref, lb_ref)[0]
            yc_ref[rows, :] = (yc * _rstd(yc) * gc_ref[...]).astype(BF16)
            return carry

        lax.fori_loop(0, s // rc, chunk, 0)

    def zblk(j):
        return pl.BlockSpec((s, w), lambda i: (0, j))

    def whole(a):
        return pl.BlockSpec(a.shape, lambda i: (0, 0))

    return _pcall(
        body, name=name, out_shape=(_sds((s, w), BF16), _sds((s, w), BF16)), grid=(1,),
        in_specs=[zblk(0), zblk(1), zblk(2), zblk(nblk - 2), zblk(nblk - 1)] + [whole(a) for a in (wa, wc, cb, lg, lb, ga, gc)],
        out_specs=(pl.BlockSpec((s, w), lambda i: (0, 0)), pl.BlockSpec((s, w), lambda i: (0, 0))),
        scratch_shapes=[pltpu.VMEM((s + PAD_SHORT, w), F32), pltpu.VMEM((s + PAD_LONG, w), F32)],
    )(z, z, z, z, z, wa, wc, cb, lg, lb, ga, gc)


def _mix_bwd(z, dy, wa, wc, cb, lg, lb, ga, gc, name, comm=None):
    s = z.shape[0]
    w = wa.shape[1]
    nblk = z.shape[1] // w
    nyb = dy.shape[1] // w
    rc = ROW_CHUNK

    def body(ah_ref, ab_ref, ac_ref, cv_ref, cg_ref, dya_ref, dyc_ref,
             wa_ref, wc_ref, cb_ref, lg_ref, lb_ref, ga_ref, gc_ref,
             dza_ref, dzc_ref, dwa_ref, dwc_ref, dcb_ref, dlg_ref, dlb_ref, dga_ref, dgc_ref,
             tpad, upad, dctp, dup):
        tpad[pl.ds(0, PAD_SHORT), :] = jnp.zeros((PAD_SHORT, w), F32)
        upad[pl.ds(0, PAD_LONG), :] = jnp.zeros((PAD_LONG, w), F32)
        dctp[pl.ds(s, PAD_SHORT), :] = jnp.zeros((PAD_SHORT, w), F32)
        dup[pl.ds(s, PAD_LONG), :] = jnp.zeros((PAD_LONG, w), F32)
        for ref in (dwa_ref, dwc_ref, dcb_ref, dlg_ref, dlb_ref, dga_ref, dgc_ref):
            ref[...] = jnp.zeros(ref.shape, F32)

        def rms_bwd(y, g_ref, dyn, dg_ref):
            r = _rstd(y)
            yh = y * r
            gd = dyn * g_ref[...]
            dg_ref[...] += jnp.sum(dyn * yh, axis=0, keepdims=True)
            return r * (gd - yh * jnp.mean(gd * yh, axis=-1, keepdims=True))

        def first(i, carry):
            base = pl.multiple_of(i * rc, rc)
            rows = pl.ds(base, rc)
            ah, ab, ac = ah_ref[rows, :], ab_ref[rows, :], ac_ref[rows, :]
            tpad[pl.ds(base + PAD_SHORT, rc), :] = ac * ah
            taps_t = _causal_taps(tpad[pl.ds(base, rc + PAD_SHORT), :], SHORT_CONV, PAD_SHORT, rc)
            ya, ct = _mixer_a_fwd(ab, taps_t, wa_ref)
            dya = rms_bwd(ya, ga_ref, dya_ref[rows, :], dga_ref)
            dza_ref[rows, w:2 * w] = (dya * ct).astype(BF16)
            dct = dya * ab
            dctp[rows, :] = dct
            _conv_wgrad(dwa_ref, dct, taps_t)

            upad[pl.ds(base + PAD_LONG, rc), :] = cv_ref[rows, :] * _sigmoid(cg_ref[rows, :])
            taps_u = _causal_taps(upad[pl.ds(base, rc + PAD_LONG), :], CONFORMER_CONV, PAD_LONG, rc)
            yc, ln, sg, uh, rs = _mixer_c_fwd(taps_u, wc_ref, cb_ref, lg_ref, lb_ref)
            dyc = rms_bwd(yc, gc_ref, dyc_ref[rows, :], dgc_ref)
            dln = dyc * (sg * (1.0 + ln * (1.0 - sg)))
            dlg_ref[...] += jnp.sum(dln * uh, axis=0, keepdims=True)
            dlb_ref[...] += jnp.sum(dln, axis=0, keepdims=True)
            duh = dln * lg_ref[...]
            du = rs * (duh - jnp.mean(duh, axis=-1, keepdims=True) - uh * jnp.mean(duh * uh, axis=-1, keepdims=True))
            dcb_ref[...] += jnp.sum(du, axis=0, keepdims=True)
            dup[rows, :] = du
            _conv_wgrad(dwc_ref, du, taps_u)
            return carry

        lax.fori_loop(0, s // rc, first, 0)

        def second(i, carry):
            base = pl.multiple_of(i * rc, rc)
            rows = pl.ds(base, rc)
            dt = _conv_taps(_anticausal_taps(dctp[pl.ds(base, rc + PAD_SHORT), :], SHORT_CONV, rc), wa_ref)
            dza_ref[rows, 0:w] = (dt * ac_ref[rows, :]).astype(BF16)
            dza_ref[rows, 2 * w:3 * w] = (dt * ah_ref[rows, :]).astype(BF16)
            du0 = _conv_taps(_anticausal_taps(dup[pl.ds(base, rc + PAD_LONG), :], CONFORMER_CONV, rc), wc_ref)
            sg = _sigmoid(cg_ref[rows, :])
            dzc_ref[rows, 0:w] = (du0 * sg).astype(BF16)
            dzc_ref[rows, w:2 * w] = (du0 * cv_ref[rows, :] * sg * (1.0 - sg)).astype(BF16)
            return carry

        lax.fori_loop(0, s // rc, second, 0)

    def blk(j):
        return pl.BlockSpec((s, w), lambda i: (0, j))

    def whole(a):
        return pl.BlockSpec(tuple(a.shape), lambda i: (0, 0))

    params = (wa, wc, cb, lg, lb, ga, gc)
    outs = (_sds((s, 3 * w), BF16), _sds((s, 2 * w), BF16)) + tuple(_sds(p.shape, F32) for p in params)
    return _pcall(
        body, name=name, out_shape=outs, grid=(1,),
        in_specs=[blk(0), blk(1), blk(2), blk(nblk - 2), blk(nblk - 1), blk(0), blk(nyb - 1)] + [whole(p) for p in params],
        out_specs=tuple(whole(o) for o in outs),
        scratch_shapes=[pltpu.VMEM((s + PAD_SHORT, w), F32), pltpu.VMEM((s + PAD_LONG, w), F32),
                        pltpu.VMEM((s + PAD_SHORT, w), F32), pltpu.VMEM((s + PAD_LONG, w), F32)], comm=comm,
    )(z, z, z, z, z, dy, dy, *params)


def _ffn_act_fwd(up, wf, name, comm=None):
    s, f2 = up.shape
    f = f2 // 2
    tc = 256
    nb = f // tc
    rc = FFN_ROWS

    def body(g_ref, v_ref, wg_ref, wv_ref, o_ref, gpad, vpad):
        gpad[pl.ds(0, PAD_SHORT), :] = jnp.zeros((PAD_SHORT, tc), F32)
        vpad[pl.ds(0, PAD_SHORT), :] = jnp.zeros((PAD_SHORT, tc), F32)

        def chunk(i, carry):
            base = pl.multiple_of(i * rc, rc)
            rows = pl.ds(base, rc)
            gpad[pl.ds(base + PAD_SHORT, rc), :] = g_ref[rows, :].astype(F32)
            vpad[pl.ds(base + PAD_SHORT, rc), :] = v_ref[rows, :].astype(F32)
            gc = _conv_taps(_causal_taps(gpad[pl.ds(base, rc + PAD_SHORT), :], FFN_CONV, PAD_SHORT, rc), wg_ref)
            vc = _conv_taps(_causal_taps(vpad[pl.ds(base, rc + PAD_SHORT), :], FFN_CONV, PAD_SHORT, rc), wv_ref)
            o_ref[rows, :] = (gc * _sigmoid(gc) * vc).astype(BF16)
            return carry

        lax.fori_loop(0, s // rc, chunk, 0)

    return _pcall(
        body, name=name, out_shape=_sds((s, f), BF16), grid=(nb,),
        in_specs=[pl.BlockSpec((s, tc), lambda j: (0, j)), pl.BlockSpec((s, tc), lambda j: (0, j + nb)),
                  pl.BlockSpec((FFN_CONV, tc), lambda j: (0, j)), pl.BlockSpec((FFN_CONV, tc), lambda j: (0, j + nb))],
        out_specs=pl.BlockSpec((s, tc), lambda j: (0, j)),
        scratch_shapes=[pltpu.VMEM((s + PAD_SHORT, tc), F32), pltpu.VMEM((s + PAD_SHORT, tc), F32)], comm=comm,
    )(up, up, wf, wf)


def _ffn_act_bwd(up, dact, wf, name, comm=None):
    s, f2 = up.shape
    f = f2 // 2
    tc = 256
    nb = f // tc
    rc = FFN_ROWS

    def body(g_ref, v_ref, da_ref, wg_ref, wv_ref, act_ref, dg_ref, dv_ref, dwg_ref, dwv_ref, gpad, vpad, dgp, dvp):
        gpad[pl.ds(0, PAD_SHORT), :] = jnp.zeros((PAD_SHORT, tc), F32)
        vpad[pl.ds(0, PAD_SHORT), :] = jnp.zeros((PAD_SHORT, tc), F32)
        dgp[pl.ds(s, PAD_SHORT), :] = jnp.zeros((PAD_SHORT, tc), F32)
        dvp[pl.ds(s, PAD_SHORT), :] = jnp.zeros((PAD_SHORT, tc), F32)
        dwg_ref[...] = jnp.zeros((FFN_CONV, tc), F32)
        dwv_ref[...] = jnp.zeros((FFN_CONV, tc), F32)

        def first(i, carry):
            base = pl.multiple_of(i * rc, rc)
            rows = pl.ds(base, rc)
            gpad[pl.ds(base + PAD_SHORT, rc), :] = g_ref[rows, :].astype(F32)
            vpad[pl.ds(base + PAD_SHORT, rc), :] = v_ref[rows, :].astype(F32)
            taps_g = _causal_taps(gpad[pl.ds(base, rc + PAD_SHORT), :], FFN_CONV, PAD_SHORT, rc)
            taps_v = _causal_taps(vpad[pl.ds(base, rc + PAD_SHORT), :], FFN_CONV, PAD_SHORT, rc)
            gc = _conv_taps(taps_g, wg_ref)
            vc = _conv_taps(taps_v, wv_ref)
            sg = _sigmoid(gc)
            silu = gc * sg
            act_ref[rows, :] = (silu * vc).astype(BF16)
            da = da_ref[rows, :].astype(F32)
            dgc = da * vc * (sg * (1.0 + gc * (1.0 - sg)))
            dvc = da * silu
            dgp[rows, :] = dgc
            dvp[rows, :] = dvc
            _conv_wgrad(dwg_ref, dgc, taps_g)
            _conv_wgrad(dwv_ref, dvc, taps_v)
            return carry

        lax.fori_loop(0, s // rc, first, 0)

        def second(i, carry):
            base = pl.multiple_of(i * rc, rc)
            rows = pl.ds(base, rc)
            dg_ref[rows, :] = _conv_taps(_anticausal_taps(dgp[pl.ds(base, rc + PAD_SHORT), :], FFN_CONV, rc), wg_ref).astype(BF16)
            dv_ref[rows, :] = _conv_taps(_anticausal_taps(dvp[pl.ds(base, rc + PAD_SHORT), :], FFN_CONV, rc), wv_ref).astype(BF16)
            return carry

        lax.fori_loop(0, s // rc, second, 0)

    lo = pl.BlockSpec((s, tc), lambda j: (0, j))
    hi = pl.BlockSpec((s, tc), lambda j: (0, j + nb))
    wlo = pl.BlockSpec((FFN_CONV, tc), lambda j: (0, j))
    whi = pl.BlockSpec((FFN_CONV, tc), lambda j: (0, j + nb))
    act, dgate, dval, dwg, dwv = _pcall(
        body, name=name,
        out_shape=(_sds((s, f), BF16), _sds((s, f), BF16), _sds((s, f), BF16), _sds((FFN_CONV, f), F32), _sds((FFN_CONV, f), F32)),
        grid=(nb,), in_specs=[lo, hi, lo, wlo, whi], out_specs=(lo, lo, lo, wlo, wlo),
        scratch_shapes=[pltpu.VMEM((s + PAD_SHORT, tc), F32) for _ in range(4)], comm=comm,
    )(up, up, dact, wf, wf)
    return act, dgate, dval, jnp.concatenate([dwg, dwv], axis=1)


def _out_proj(yan, yb, ycn, gb, x, w_out, layer, g_next, name, comm=None):
    s, w = yan.shape
    wb = yb.shape[1]
    d = x.shape[1]
    tm = PROJ_ROWS

    def body(ya_ref, yb_ref, yc_ref, gb_ref, x_ref, w_ref, g_ref, y_ref, xm_ref, h_ref):
        ybv = yb_ref[...]
        y = jnp.concatenate([ya_ref[...], (ybv * _rstd(ybv) * gb_ref[...]).astype(BF16), yc_ref[...]], axis=1)
        y_ref[...] = y
        xm = x_ref[...] + _dot(y, w_ref[...])
        xm_ref[...] = xm
        h_ref[...] = (xm * _rstd(xm) * g_ref[...]).astype(BF16)

    def rows(width):
        return pl.BlockSpec((tm, width), lambda i: (i, 0))

    def vec(width):
        return pl.BlockSpec((1, width), lambda i: (0, 0))

    return _pcall(body, name=name, out_shape=(_sds((s, d), BF16), _sds((s, d), F32), _sds((s, d), BF16)), grid=(s // tm,),
                  in_specs=[rows(w), rows(wb), rows(w), vec(wb), rows(d), pl.BlockSpec((None, d, d), lambda i: (layer, 0, 0)), vec(d)],
                  out_specs=(rows(d), rows(d), rows(d)), comm=comm)(yan, yb, ycn, gb, x, w_out, g_next)


def _down_proj(act, w_down, layer, x_mid, g_next, name, comm=None):
    s, f = act.shape
    d = x_mid.shape[1]
    tm = PROJ_ROWS

    def body(a_ref, w_ref, x_ref, *refs):
        xo = x_ref[...] + _dot(a_ref[...], w_ref[...])
        refs[-2 if g_next is not None else -1][...] = xo
        if g_next is not None:
            refs[-1][...] = (xo * _rstd(xo) * refs[0][...]).astype(BF16)

    row = pl.BlockSpec((tm, d), lambda i: (i, 0))
    ins = [act, w_down, x_mid] + ([g_next] if g_next is not None else [])
    in_specs = [pl.BlockSpec((tm, f), lambda i: (i, 0)), pl.BlockSpec((None, f, d), lambda i: (layer, 0, 0)), row]
    in_specs += [pl.BlockSpec((1, d), lambda i: (0, 0))] if g_next is not None else []
    outs = (_sds((s, d), F32), _sds((s, d), BF16)) if g_next is not None else (_sds((s, d), F32),)
    res = _pcall(body, name=name, out_shape=outs, grid=(s // tm,), in_specs=in_specs, out_specs=tuple([row] * len(outs)),
                 comm=comm)(*ins)
    return (res[0], res[1]) if g_next is not None else (res[0], None)


def _proj_dx(pieces, w, layer, nt, x, g, dres, name, tm, comm=None):
    s, d = x.shape
    widths = [p.shape[1] for p in pieces]

    def body(*refs):
        p_refs, (w_ref, x_ref, g_ref, dres_ref, dx_ref, dxb_ref, dg_ref) = refs[:len(pieces)], refs[len(pieces):]
        i = pl.program_id(0)
        dh, off = None, 0
        for p_ref, width in zip(p_refs, widths):
            part = _dot_nt(p_ref[...], w_ref[:, off:off + width]) if nt else _dot(p_ref[...], w_ref[off:off + width, :])
            dh = part if dh is None else dh + part
            off += width
        xv = x_ref[...]
        r = _rstd(xv)
        xh = xv * r
        gd = dh * g_ref[...]
        dx = dres_ref[...] + r * (gd - xh * jnp.mean(gd * xh, axis=-1, keepdims=True))
        dx_ref[...] = dx
        dxb_ref[...] = dx.astype(BF16)
        part = jnp.sum(dh * xh, axis=0, keepdims=True)

        @pl.when(i == 0)
        def _():
            dg_ref[...] = part

        @pl.when(i > 0)
        def _():
            dg_ref[...] += part

    row = pl.BlockSpec((tm, d), lambda i: (i, 0))
    vec = pl.BlockSpec((1, d), lambda i: (0, 0))
    w_spec = pl.BlockSpec((None,) + w.shape[1:], lambda i: (layer, 0, 0))
    return _pcall(body, name=name, out_shape=(_sds((s, d), F32), _sds((s, d), BF16), _sds((1, d), F32)), grid=(s // tm,),
                  in_specs=[pl.BlockSpec((tm, width), lambda i: (i, 0)) for width in widths] + [w_spec, row, vec, row],
                  out_specs=(row, row, vec), comm=comm)(*pieces, w, x, g, dres)


def _yb_norm_bwd(yb, dy, gb, name, comm=None):
    s, wb = yb.shape
    w = wb // 2
    heads = wb // D_HEAD
    tm = ROW_CHUNK

    def body(yb_ref, d1_ref, d2_ref, g_ref, dyb_ref, dl_ref, dg_ref):
        i = pl.program_id(0)
        y = yb_ref[...]
        dyn = jnp.concatenate([d1_ref[...], d2_ref[...]], axis=1)
        r = _rstd(y)
        yh = y * r
        gd = dyn * g_ref[...]
        dyb = r * (gd - yh * jnp.mean(gd * yh, axis=-1, keepdims=True))
        dyb_ref[...] = dyb
        part = jnp.sum(dyn * yh, axis=0, keepdims=True)
        prod = dyb * y
        even = lax.broadcasted_iota(I32, (tm, LANES), 1) < D_HEAD
        for p in range(heads // 2):
            blk = prod[:, p * LANES:(p + 1) * LANES]
            ev = jnp.sum(jnp.where(even, blk, 0.0), axis=1, keepdims=True)
            od = jnp.sum(jnp.where(even, 0.0, blk), axis=1, keepdims=True)
            dl_ref[2 * p] = jnp.broadcast_to(ev, (tm, LANES))
            dl_ref[2 * p + 1] = jnp.broadcast_to(od, (tm, LANES))

        @pl.when(i == 0)
        def _():
            dg_ref[...] = part

        @pl.when(i > 0)
        def _():
            dg_ref[...] += part

    return _pcall(
        body, name=name, out_shape=(_sds((s, wb), F32), _sds((heads, s, LANES), F32), _sds((1, wb), F32)),
        grid=(s // tm,),
        in_specs=[pl.BlockSpec((tm, wb), lambda i: (i, 0)), pl.BlockSpec((tm, w), lambda i: (i, 1)),
                  pl.BlockSpec((tm, w), lambda i: (i, 2)), pl.BlockSpec((1, wb), lambda i: (0, 0))],
        out_specs=(pl.BlockSpec((tm, wb), lambda i: (i, 0)), pl.BlockSpec((heads, tm, LANES), lambda i: (0, i, 0)),
                   pl.BlockSpec((1, wb), lambda i: (0, 0))), comm=comm,
    )(yb, dy, dy, gb)


def _t5_bucket_table():
    max_exact = NUM_BUCKETS // 2
    out = np.full((len(DILATED_BRANCHES), BLK, 2 * BLK), -1, np.int32)
    rel = np.arange(BLK)[:, None] - np.arange(2 * BLK)[None, :] + BLK
    for b, (window, dilation) in enumerate(DILATED_BRANCHES):
        n_keys = window // dilation
        dist = np.maximum(rel, 0) * dilation
        d_f = np.maximum(dist, 1).astype(np.float32)
        large = max_exact + (np.log(d_f / np.float32(max_exact)) / np.float32(math.log(MAX_DISTANCE / max_exact))
                             * np.float32(NUM_BUCKETS - max_exact)).astype(np.int32)
        large = np.minimum(large, NUM_BUCKETS - 1)
        bucket = np.where(dist < max_exact, dist, large)
        out[b] = np.where((rel >= 0) & (rel <= n_keys), bucket, -1)
    return out


def _bias_tiles(rel_bias, buckets, name):
    nbk, heads = rel_bias.shape
    nbr = buckets.shape[0]

    def body(rb_ref, bk_ref, o_ref):
        for br in range(nbr):
            bk = bk_ref[br]
            tiles = [jnp.full((BLK, 2 * BLK), NEG, F32) for _ in range(heads)]
            for b in range(nbk):
                hit = bk == b
                tiles = [jnp.where(hit, rb_ref[b, h], tiles[h]) for h in range(heads)]
            for h in range(heads):
                o_ref[br, h] = tiles[h]

    return _pcall(body, name=name, out_shape=_sds((nbr, heads, BLK, 2 * BLK), F32), grid=(1,),
                  in_specs=[pl.BlockSpec(memory_space=pltpu.SMEM), pl.BlockSpec(buckets.shape, lambda i: (0, 0, 0))],
                  out_specs=pl.BlockSpec((nbr, heads, BLK, 2 * BLK), lambda i: (0, 0, 0, 0)))(rel_bias, buckets)


def _bias_grad(dtiles, buckets, nbk, name):
    nbr, heads = dtiles.shape[:2]

    def body(dt_ref, bk_ref, o_ref):
        row = lax.broadcasted_iota(I32, (nbk, LANES), 0)
        col = lax.broadcasted_iota(I32, (nbk, LANES), 1)
        out = jnp.zeros((nbk, LANES), F32)
        for h in range(heads):
            for b in range(nbk):
                tot = jnp.zeros((), F32)
                for br in range(nbr):
                    tot = tot + jnp.sum(jnp.where(bk_ref[br] == b, dt_ref[br, h], 0.0))
                out = jnp.where((row == b) & (col == h), tot, out)
        o_ref[...] = out

    return _pcall(body, name=name, out_shape=_sds((nbk, LANES), F32), grid=(1,),
                  in_specs=[pl.BlockSpec(dtiles.shape, lambda i: (0, 0, 0, 0)), pl.BlockSpec(buckets.shape, lambda i: (0, 0, 0))],
                  out_specs=pl.BlockSpec((nbk, LANES), lambda i: (0, 0)))(dtiles, buckets)


def _largest_divisor(n, cap):
    return max(g for g in range(1, cap + 1) if n % g == 0)


def _attn_blocks(s, visit, group):
    for br, (window, d) in enumerate(DILATED_BRANCHES):
        n_blk = (s // d) // BLK
        span = BLK * d
        g1 = _largest_divisor(d, group)

        def firsts(t, carry, br=br, d=d, g1=g1):
            for j in range(g1):
                visit(br, d, t * g1 + j, False)
            return carry

        lax.fori_loop(0, d // g1, firsts, 0)
        if n_blk > 1:
            total = d * (n_blk - 1)
            g2 = _largest_divisor(total, group)

            def rest(t, carry, br=br, d=d, n_blk=n_blk, span=span, g2=g2):
                for j in range(g2):
                    idx = t * g2 + j
                    visit(br, d, idx // (n_blk - 1) + (1 + idx % (n_blk - 1)) * span, True)
                return carry

            lax.fori_loop(0, total // g2, rest, 0)


def _rows(start, size, d):
    return pl.ds(pl.multiple_of(start, BLK), size) if d == 1 else pl.ds(start, size, stride=d)


def _attn_fwd(z, btiles, col0, name, comm=None):
    s = z.shape[0]
    nbr, heads = btiles.shape[:2]
    pairs = heads // 2
    scale = D_HEAD ** -0.5
    rc = ROW_CHUNK

    def body(q_ref, k_ref, v_ref, bt_ref, yb_ref, lse_ref, acc_ref, m_ref, l_ref):
        even = lax.broadcasted_iota(I32, (BLK, LANES), 1) < D_HEAD
        even2 = lax.broadcasted_iota(I32, (2 * BLK, LANES), 1) < D_HEAD

        def visit(br, d, start, prev):
            kw = 2 * BLK if prev else BLK
            rows_q = _rows(start, BLK, d)
            rows_k = _rows(start - BLK * d, kw, d) if prev else rows_q
            qb = q_ref[rows_q, :]
            kb = k_ref[rows_k, :].astype(BF16)
            vw = v_ref[rows_k, :]
            ev_k = even2 if prev else even
            qm = jnp.concatenate([jnp.where(even, qb, 0.0), jnp.where(even, 0.0, qb)], axis=0).astype(BF16)
            bias = [bt_ref[br, e] if prev else bt_ref[br, e, :, BLK:] for e in range(2)]
            sc = _dot_nt(qm, kb) * scale + jnp.concatenate(bias, axis=0)
            m = jnp.max(sc, axis=1, keepdims=True)
            p = jnp.exp(sc - m)
            l = jnp.sum(p, axis=1, keepdims=True)
            pb = p.astype(BF16)
            vm = jnp.concatenate([jnp.where(ev_k, vw, 0.0), jnp.where(ev_k, 0.0, vw)], axis=0).astype(BF16)
            acc_ref.at[br][rows_q, :] = _dot(jnp.concatenate([pb[:BLK], pb[BLK:]], axis=1), vm)
            for e in range(2):
                m_ref.at[br, e][rows_q, :] = jnp.broadcast_to(m[e * BLK:(e + 1) * BLK], (BLK, LANES))
                l_ref.at[br, e][rows_q, :] = jnp.broadcast_to(l[e * BLK:(e + 1) * BLK], (BLK, LANES))

        _attn_blocks(s, visit, ATTN_GROUP_FWD)

        ev_c = lax.broadcasted_iota(I32, (rc, LANES), 1) < D_HEAD

        def merge(i, carry):
            rows = pl.ds(pl.multiple_of(i * rc, rc), rc)
            wts, dens = [], []
            for e in range(2):
                ms = [m_ref[br, e, rows, :] for br in range(nbr)]
                top = functools.reduce(jnp.maximum, ms)
                w = [jnp.exp(mb - top) for mb in ms]
                den = functools.reduce(lambda a, b: a + b, [w[br] * l_ref[br, e, rows, :] for br in range(nbr)])
                lse_ref[e, rows, :] = top + jnp.log(den)
                wts.append(w)
                dens.append(den)
            num = functools.reduce(lambda a, b: a + b,
                                   [jnp.where(ev_c, wts[0][br], wts[1][br]) * acc_ref[br, rows, :] for br in range(nbr)])
            yb_ref[rows, :] = num / jnp.where(ev_c, dens[0], dens[1])
            return carry

        lax.fori_loop(0, s // rc, merge, 0)

    def zcol(j):
        return pl.BlockSpec((s, LANES), lambda p, j=j: (0, col0 + j + p))

    return _pcall(
        body, name=name, out_shape=(_sds((s, pairs * LANES), F32), _sds((heads, s, LANES), F32)), grid=(pairs,),
        in_specs=[zcol(0), zcol(pairs), zcol(2 * pairs), pl.BlockSpec((nbr, 2, BLK, 2 * BLK), lambda p: (0, p, 0, 0))],
        out_specs=(pl.BlockSpec((s, LANES), lambda p: (0, p)), pl.BlockSpec((2, s, LANES), lambda p: (p, 0, 0))),
        scratch_shapes=[pltpu.VMEM((nbr, s, LANES), F32), pltpu.VMEM((nbr, 2, s, LANES), F32), pltpu.VMEM((nbr, 2, s, LANES), F32)],
        comm=comm,
    )(z, z, z, btiles)


def _attn_bwd(z, btiles, dyb, lse, delta, dbias_in, col0, name, comm=None):
    s = z.shape[0]
    nbr, heads = btiles.shape[:2]
    pairs = heads // 2
    scale = D_HEAD ** -0.5

    def body(q_ref, k_ref, v_ref, bt_ref, dy_ref, lse_ref, dl_ref, dbi_ref,
             dq_ref, dk_ref, dv_ref, db_ref, dqa, dka, dva):
        even = lax.broadcasted_iota(I32, (BLK, LANES), 1) < D_HEAD
        even2 = lax.broadcasted_iota(I32, (2 * BLK, LANES), 1) < D_HEAD
        for ref in (dqa, dka, dva):
            ref[...] = jnp.zeros((s, LANES), F32)
        db_ref[...] = dbi_ref[...]

        def visit(br, d, start, prev):
            kw = 2 * BLK if prev else BLK
            rows_q = _rows(start, BLK, d)
            rows_k = _rows(start - BLK * d, kw, d) if prev else rows_q
            qb = q_ref[rows_q, :]
            dyv = dy_ref[rows_q, :]
            kwin = k_ref[rows_k, :]
            kb = kwin.astype(BF16)
            vb = v_ref[rows_k, :].astype(BF16)
            ev_k = even2 if prev else even
            qm = jnp.concatenate([jnp.where(even, qb, 0.0), jnp.where(even, 0.0, qb)], axis=0).astype(BF16)
            dym = jnp.concatenate([jnp.where(even, dyv, 0.0), jnp.where(even, 0.0, dyv)], axis=0).astype(BF16)
            bias = [bt_ref[br, e] if prev else bt_ref[br, e, :, BLK:] for e in range(2)]
            sc = _dot_nt(qm, kb) * scale + jnp.concatenate(bias, axis=0)
            lt = jnp.concatenate([lse_ref.at[e][rows_q, :] for e in range(2)], axis=0)
            dt = jnp.concatenate([dl_ref.at[e][rows_q, :] for e in range(2)], axis=0)
            if prev:
                lt = jnp.concatenate([lt, lt], axis=1)
                dt = jnp.concatenate([dt, dt], axis=1)
            p = jnp.exp(sc - lt)
            ds = p * (_dot_nt(dym, vb) - dt)
            for e in range(2):
                if prev:
                    db_ref[br, e] += ds[e * BLK:(e + 1) * BLK]
                else:
                    db_ref[br, e, :, BLK:] += ds[e * BLK:(e + 1) * BLK]
            dsb = ds.astype(BF16)
            km = jnp.concatenate([jnp.where(ev_k, kwin, 0.0), jnp.where(ev_k, 0.0, kwin)], axis=0).astype(BF16)
            dqa[rows_q, :] += _dot(jnp.concatenate([dsb[:BLK], dsb[BLK:]], axis=1), km) * scale
            dka[rows_k, :] += _dot_tn(dsb, qm) * scale
            dva[rows_k, :] += _dot_tn(p.astype(BF16), dym)

        _attn_blocks(s, visit, ATTN_GROUP_BWD)
        dq_ref[...] = dqa[...].astype(BF16)
        dk_ref[...] = dka[...].astype(BF16)
        dv_ref[...] = dva[...].astype(BF16)

    def zcol(j):
        return pl.BlockSpec((s, LANES), lambda p, j=j: (0, col0 + j + p))

    col = pl.BlockSpec((s, LANES), lambda p: (0, p))
    stat = pl.BlockSpec((2, s, LANES), lambda p: (p, 0, 0))
    tile = pl.BlockSpec((nbr, 2, BLK, 2 * BLK), lambda p: (0, p, 0, 0))
    wide = _sds((s, pairs * LANES), BF16)
    return _pcall(
        body, name=name, out_shape=(wide, wide, wide, _sds(btiles.shape, F32)), grid=(pairs,),
        in_specs=[zcol(0), zcol(pairs), zcol(2 * pairs), tile, col, stat, stat, tile],
        out_specs=(col, col, col, tile),
        scratch_shapes=[pltpu.VMEM((s, LANES), F32) for _ in range(3)], comm=comm,
    )(z, z, z, btiles, dyb, lse, delta, dbias_in)


def _row(v):
    return v.reshape(1, -1)


class _Rides:
    def __init__(self):
        self.table, self.grads = {}, {}

    def add(self, name, build):
        self.table.setdefault(name, []).append(build)

    def get(self, name):
        comm = None
        for build in self.table.get(name, ()):
            comm = build() if comm is None else _both(comm, build())
        return comm

    def ready(self, key, g):
        self.grads[key] = g


class _LocalSchedule:
    def __init__(self):
        self.big = {}

    def fwd_comms(self, l):
        return _Rides()

    def bwd_comms(self, l):
        return _Rides()

    def after_bwd(self, l, grads):
        self.big[l] = grads


def _layer_fwd(l, x, h, wts, prm, btiles, comms):
    d = x.shape[1]
    wq = d // 4
    depth = prm["norm_mix_g"].shape[0]
    gout = prm["out_norm_g"][l]
    z = _mm_n(h, wts["in_t"], l, nt=True, tn=256, out_dtype=F32, name="in_proj", comm=comms.get("in_proj"))
    yan, ycn = _mix_fwd(z, prm["conv_a_w"][l], prm["conv_c_w"][l], _row(prm["conv_c_b"][l]), _row(prm["ln_c_g"][l]),
                        _row(prm["ln_c_b"][l]), _row(gout[:wq]), _row(gout[3 * wq:]), "mix_fwd")
    yb, lse = _attn_fwd(z, btiles, 3 * wq // LANES, "attn_fwd", comm=comms.get("attn_fwd"))
    y, x_mid, h2 = _out_proj(yan, yb, ycn, _row(gout[wq:3 * wq]), x, wts["out"], l, _row(prm["norm_ffn_g"][l]),
                             "out_proj", comm=comms.get("out_proj"))
    up = _mm_n(h2, wts["up"], l, nt=False, tn=512, out_dtype=BF16, name="up_proj", comm=comms.get("up_proj"))
    act = _ffn_act_fwd(up, prm["conv_f_w"][l], "ffn_act_fwd", comm=comms.get("ffn_act_fwd"))
    g_next = _row(prm["norm_mix_g"][l + 1]) if l + 1 < depth else None
    x_out, h_next = _down_proj(act, wts["down"], l, x_mid, g_next, "down_proj", comm=comms.get("down_proj"))
    return x_out, h_next, (x, h, z, yb, lse, y, x_mid, h2, up)


def _layer_bwd(l, dxo, dxo_b, saved, wts, prm, btiles, dbias, comms):
    x, h, z, yb, lse, y, x_mid, h2, up = saved
    d = x.shape[1]
    wq = d // 4
    gout = prm["out_norm_g"][l]
    dact = _mm_n(dxo_b, wts["down"], l, nt=True, tn=256, out_dtype=BF16, name="down_proj_dx", comm=comms.get("down_proj_dx"))
    act, dgate, dval, dwf = _ffn_act_bwd(up, dact, prm["conv_f_w"][l], "ffn_act_bwd", comm=comms.get("ffn_act_bwd"))
    g_down = _mm_tn(act, dxo_b, t=256, name="down_proj_dw")
    comms.ready("down", g_down)
    dxm, dxm_b, dg_ffn = _proj_dx([dgate, dval], wts["up"], l, True, x_mid, _row(prm["norm_ffn_g"][l]), dxo, "up_proj_dx",
                                  ROW_CHUNK, comm=comms.get("up_proj_dx"))
    g_up = _mm_tn2(h2, dgate, dval, t=256, name="up_proj_dw", comm=comms.get("up_proj_dw"))
    comms.ready("up", g_up)
    dy = _mm_n(dxm_b, wts["out"], l, nt=True, tn=256, out_dtype=F32, name="out_proj_dx")
    g_out = _mm_tn(y, dxm_b, t=256, name="out_proj_dw")
    comms.ready("out", g_out)
    dza, dzc, dwa, dwc, dcb, dlg, dlb, dga, dgc = _mix_bwd(
        z, dy, prm["conv_a_w"][l], prm["conv_c_w"][l], _row(prm["conv_c_b"][l]), _row(prm["ln_c_g"][l]),
        _row(prm["ln_c_b"][l]), _row(gout[:wq]), _row(gout[3 * wq:]), "mix_bwd", comm=comms.get("mix_bwd"))
    dyb, delta, dgb = _yb_norm_bwd(yb, dy, _row(gout[wq:3 * wq]), "yb_norm_bwd", comm=comms.get("yb_norm_bwd"))
    dq, dk, dv, dbias = _attn_bwd(z, btiles, dyb, lse, delta, dbias, 3 * wq // LANES, "attn_bwd",
                                  comm=comms.get("attn_bwd"))
    dz = [dza, dq, dk, dv, dzc]
    dx, dx_b, dg_mix = _proj_dx(dz, wts["in_t"], l, False, x, _row(prm["norm_mix_g"][l]), dxm, "in_proj_dx",
                                PROJ_ROWS, comm=comms.get("in_proj_dx"))
    g_in_t = _mm_tn_pieces(dz, h, t=256, name="in_proj_dw")
    big = {"in_t": g_in_t, "out": g_out, "up": g_up, "down": g_down}
    small = {"norm_mix_g": dg_mix[0], "conv_a_w": dwa, "conv_c_w": dwc, "conv_c_b": dcb[0], "ln_c_g": dlg[0],
             "ln_c_b": dlb[0], "out_norm_g": jnp.concatenate([dga[0], dgb[0], dgc[0]]), "norm_ffn_g": dg_ffn[0],
             "conv_f_w": dwf}
    return dx, dx_b, big, small, dbias


def _local_step(x, tgt, wts, prm, sched):
    depth = prm["norm_mix_g"].shape[0]
    buckets = jnp.asarray(_t5_bucket_table())
    btiles = _bias_tiles(prm["rel_bias"], buckets, "bias_tiles")
    saved = []
    h = _rms_fwd(x, _row(prm["norm_mix_g"][0]), "rms_mix_fwd")
    for l in range(depth):
        x, h, sv = _layer_fwd(l, x, h, wts, prm, btiles, sched.fwd_comms(l))
        saved.append(sv)
    loss, dx, dx_b, dg_final = _final_loss(x, _row(prm["final_g"]), tgt, "final_loss")
    dbias = jnp.zeros(btiles.shape, F32)
    small = [None] * depth
    for l in reversed(range(depth)):
        dx, dx_b, grads, small[l], dbias = _layer_bwd(l, dx, dx_b, saved[l], wts, prm, btiles, dbias, sched.bwd_comms(l))
        sched.after_bwd(l, grads)
    nbk, heads = prm["rel_bias"].shape
    d_rel = _bias_grad(dbias, buckets, nbk, "bias_grad")[:, :heads]
    return loss, dx, small, d_rel, dg_final[0]


BIG = ("in_t", "out", "up", "down")
COL_SHARDED = ("up",)
N_CHIPS = 4
N_DEV = 8
BF16_ROWS = 16


def _me():
    return lax.axis_index("x"), lax.axis_index("y"), lax.axis_index("c")


def _chip_of(x, y):
    return 2 * x + y


def _other_chips(x, y):
    return ((1 - x, y), (x, 1 - y), (1 - x, 1 - y))


def _remote(src, dst, send_sem, recv_sem, device):
    return pltpu.make_async_remote_copy(src_ref=src, dst_ref=dst, send_sem=send_sem, recv_sem=recv_sem,
                                        device_id=device, device_id_type=MESH)


ALL_FLIPS = (0, 1, 2)


def _ag_comm(wts, layer, ici_keys, fwd_keys):
    flips = {(k if isinstance(k, str) else k[0]): (ALL_FLIPS if isinstance(k, str) else k[1]) for k in ici_keys}
    keys = tuple(k for k in BIG if k in flips or k in fwd_keys)

    def geo(k):
        _, rows, cols = wts[k].shape
        return (rows, cols // N_CHIPS) if k in COL_SHARDED else (rows // N_CHIPS, cols)

    def copies(refs, sems):
        g = dict(zip(keys, refs))
        isend, irecv, dsend, drecv = sems
        x, y, c = _me()
        mine = _chip_of(x, y)

        def region(k, chip, half):
            r, cc = geo(k)
            h = r // 2
            if k in COL_SHARDED:
                return g[k].at[layer, pl.ds(pl.multiple_of(half * h, BF16_ROWS), h), pl.ds(pl.multiple_of(chip * cc, LANES), cc)]
            return g[k].at[layer, pl.ds(pl.multiple_of(chip * r + half * h, BF16_ROWS), h), :]

        def ici(k, f, landing):
            chip = _other_chips(x, y)[f]
            where = region(k, _chip_of(*chip) if landing else mine, c)
            i = keys.index(k)
            return _remote(where, where, isend.at[i, f], irecv.at[i, f], (*chip, c))

        def fwd(k, f, landing):
            chip = _other_chips(x, y)[f]
            where = region(k, _chip_of(*chip), 1 - c if landing else c)
            i = keys.index(k)
            return _remote(where, where, dsend.at[i, f], drecv.at[i, f], (x, y, 1 - c))

        return ici, fwd

    def start(ins, outs, sems):
        ici, fwd = copies(outs, sems)
        for k in keys:
            for f in flips.get(k, ALL_FLIPS):
                if k in flips:
                    ici(k, f, False).start()
                else:
                    fwd(k, f, False).start()

    def finish(ins, outs, sems):
        ici, fwd = copies(outs, sems)
        for k in keys:
            for f in flips.get(k, ()):
                ici(k, f, True).wait_recv()
                if k in fwd_keys:
                    fwd(k, f, False).start()
        for k in keys:
            for f in flips.get(k, ALL_FLIPS):
                if k in fwd_keys:
                    fwd(k, f, True).wait_recv()
                    fwd(k, f, False).wait_send()
                if k in flips:
                    ici(k, f, False).wait_send()

    def done(res):
        wts.update(zip(keys, res))

    n = len(keys)
    return _Comm([wts[k] for k in keys], [_sds(wts[k].shape, BF16) for k in keys], {i: i for i in range(n)},
                 [pltpu.SemaphoreType.DMA((n, 3)) for _ in range(4)], start, finish, done)


def _small_gather_comm(slab, store):
    def copies(ins, outs, sems):
        send, recv, lsem = sems
        x, y, c = _me()
        mine = _chip_of(x, y)
        own = pltpu.make_async_copy(ins[0], outs[0].at[mine], lsem)
        pairs = []
        for f, chip in enumerate(_other_chips(x, y)):
            out = _remote(ins[0], outs[0].at[mine], send.at[f], recv.at[f], (*chip, c))
            land = _remote(ins[0], outs[0].at[_chip_of(*chip)], send.at[f], recv.at[f], (*chip, c))
            pairs.append((out, land))
        return own, pairs

    def start(ins, outs, sems):
        own, pairs = copies(ins, outs, sems)
        own.start()
        for out, _ in pairs:
            out.start()

    def finish(ins, outs, sems):
        own, pairs = copies(ins, outs, sems)
        for out, land in pairs:
            land.wait_recv()
            out.wait_send()
        own.wait()

    def done(res):
        store["small"] = res[0]

    return _Comm([slab], [_sds((N_CHIPS,) + slab.shape, F32)], {},
                 [pltpu.SemaphoreType.DMA((3,)), pltpu.SemaphoreType.DMA((3,)), pltpu.SemaphoreType.DMA], start, finish, done)


def _piece_geo(g):
    geo = {}
    for k in g:
        rows, cols = g[k].shape
        geo[k] = (rows // 2, cols // N_CHIPS) if k in COL_SHARDED else (rows // (2 * N_CHIPS), cols)
    return geo


def _swap_comm(g, keys, done):
    geo = _piece_geo(g)
    n_copies = sum(N_CHIPS if k in COL_SHARDED else 1 for k in keys)

    def copies(ins, outs, sems):
        g_refs, t_refs = dict(zip(keys, ins)), dict(zip(keys, outs))
        send, recv = sems
        x, y, c = _me()
        pairs = []
        for k in keys:
            h, cc = geo[k]
            if k in COL_SHARDED:
                rows = pl.ds(pl.multiple_of((1 - c) * h, BF16_ROWS), h)
                pairs += [(g_refs[k].at[rows, pl.ds(j * cc, cc)], t_refs[k].at[j]) for j in range(N_CHIPS)]
            else:
                pairs.append((g_refs[k].at[:, 1 - c], t_refs[k]))
        return [_remote(src, dst, send.at[i], recv.at[i], (x, y, 1 - c)) for i, (src, dst) in enumerate(pairs)]

    def start(ins, outs, sems):
        for cp in copies(ins, outs, sems):
            cp.start()

    def finish(ins, outs, sems):
        for cp in copies(ins, outs, sems):
            cp.wait()

    ins = [g[k] if k in COL_SHARDED else g[k].reshape(N_CHIPS, 2, geo[k][0], geo[k][1]) for k in keys]
    return _Comm(ins, [_sds((N_CHIPS,) + geo[k], BF16) for k in keys], {},
                 [pltpu.SemaphoreType.DMA((n_copies,)) for _ in range(2)], start, finish,
                 lambda res: done(dict(zip(keys, res))))


def _pair_sum(g, theirs, c_arr, keys):
    geo = _piece_geo(g)

    def body(c_ref, *refs):
        nk = len(keys)
        for i in range(nk):
            refs[2 * nk + i][...] = (refs[i][...].astype(F32) + refs[nk + i][...].astype(F32)).astype(BF16)

    in_specs, ins = [], []
    for k in keys:
        h, cc = geo[k]
        if k in COL_SHARDED:
            in_specs.append(pl.BlockSpec((h, cc), lambda j, c_ref: (c_ref[0], j)))
            ins.append(g[k])
        else:
            in_specs.append(pl.BlockSpec((None, h, cc), lambda j, c_ref: (2 * j + c_ref[0], 0, 0)))
            ins.append(g[k].reshape(2 * N_CHIPS, h, cc))
    slab = [pl.BlockSpec((None,) + geo[k], lambda j, c_ref: (j, 0, 0)) for k in keys]
    res = _pcall(body, name="rs_pair_sum", out_shape=tuple(_sds((N_CHIPS,) + geo[k], BF16) for k in keys), grid=(N_CHIPS,),
                 in_specs=in_specs + slab, out_specs=tuple(slab), prefetch=1)(c_arr, *ins, *[theirs[k] for k in keys])
    return dict(zip(keys, res))


def _rs_comm(p, keys, store):
    def copies(ins, outs, sems):
        send, recv = sems
        x, y, c = _me()
        return [_remote(ins[i].at[_chip_of(*chip)], outs[i].at[f], send.at[i, f], recv.at[i, f], (*chip, c))
                for i in range(len(keys)) for f, chip in enumerate(_other_chips(x, y))]

    def start(ins, outs, sems):
        for cp in copies(ins, outs, sems):
            cp.start()

    def finish(ins, outs, sems):
        for cp in copies(ins, outs, sems):
            cp.wait()

    def done(res):
        store.update(zip(keys, res))

    return _Comm([p[k] for k in keys], [_sds((3,) + p[k].shape[1:], BF16) for k in keys], {},
                 [pltpu.SemaphoreType.DMA((len(keys), 3)) for _ in range(2)], start, finish, done)


def _quad_sum(p, b, where, l, full):
    parts = 2
    nk = len(BIG)

    def body(where_ref, *refs):
        for i in range(nk):
            acc = refs[i][...].astype(F32)
            for f in range(3):
                acc = acc + refs[nk + 3 * i + f][...].astype(F32)
            refs[5 * nk + i][...] = acc

    own, recv, outs = [], [], []
    for k in BIG:
        h, cc = p[k].shape[1:]
        th = h // parts
        own.append(pl.BlockSpec((None, th, cc), lambda i, w_ref: (w_ref[0], i, 0)))
        recv += [pl.BlockSpec((None, th, cc), lambda i, w_ref, f=f: (f, i, 0)) for f in range(3)]
        outs.append(pl.BlockSpec((None, None, th, cc), lambda i, w_ref: (l, w_ref[1], i, 0)))
    args = [p[k] for k in BIG] + [b[k] for k in BIG for _ in range(3)] + [full[k] for k in BIG]
    res = _pcall(body, name="rs_quad_sum", out_shape=tuple(_sds(full[k].shape, F32) for k in BIG), grid=(parts,),
                 in_specs=own + recv + [ANY] * nk, out_specs=tuple(outs), prefetch=1,
                 aliases={1 + 4 * nk + i: i for i in range(nk)})(where, *args)
    return dict(zip(BIG, res))


def _share_comm(layers, full, done):
    nk = len(BIG)

    def copies(outs, sems, landing):
        send, recv = sems
        x, y, c = _me()
        half = 1 - c if landing else c
        return [_remote(outs[i].at[l, half], outs[i].at[l, half], send.at[i, j], recv.at[i, j], (x, y, 1 - c))
                for i in range(nk) for j, l in enumerate(layers)]

    def start(ins, outs, sems):
        for cp in copies(outs, sems, False):
            cp.start()

    def finish(ins, outs, sems):
        for cp in copies(outs, sems, True):
            cp.wait_recv()
        for cp in copies(outs, sems, False):
            cp.wait_send()

    return _Comm([full[k] for k in BIG], [_sds(full[k].shape, F32) for k in BIG], {i: i for i in range(nk)},
                 [pltpu.SemaphoreType.DMA((nk, len(layers))) for _ in range(2)], start, finish,
                 lambda res: done(dict(zip(BIG, res))))


def _gather_comm(slab, done):
    def copies(ins, outs, sems, landing):
        send, recv = sems
        x, y, c = _me()
        me = 4 * x + 2 * y + c
        out = []
        for mask in range(1, N_DEV):
            peer = (x ^ (mask >> 2), y ^ ((mask >> 1) & 1), c ^ (mask & 1))
            slot = 4 * peer[0] + 2 * peer[1] + peer[2] if landing else me
            out.append(_remote(ins[0], outs[0].at[slot], send.at[mask - 1], recv.at[mask - 1], peer))
        return out

    def start(ins, outs, sems):
        for cp in copies(ins, outs, sems, False):
            cp.start()

    def finish(ins, outs, sems):
        for cp in copies(ins, outs, sems, True):
            cp.wait_recv()
        for cp in copies(ins, outs, sems, False):
            cp.wait_send()

    return _Comm([slab], [_sds((N_DEV,) + slab.shape, F32)], {},
                 [pltpu.SemaphoreType.DMA((N_DEV - 1,)), pltpu.SemaphoreType.DMA((N_DEV - 1,))], start, finish,
                 lambda res: done(res[0]))


def _sum_slabs(slabs, own, me):
    n, r, lanes = slabs.shape
    tr = r // 2

    def body(me_ref, s_ref, own_ref, o_ref):
        o_ref[...] = jnp.zeros((tr, lanes), F32)
        for i in range(n):
            @pl.when(me_ref[0] == i)
            def _():
                o_ref[...] += own_ref[...]

            @pl.when(me_ref[0] != i)
            def _():
                o_ref[...] += s_ref[i]

    return _pcall(body, name="sum_partials", out_shape=_sds((r, lanes), F32), grid=(2,),
                  in_specs=[pl.BlockSpec((n, tr, lanes), lambda i, me_ref: (0, i, 0)),
                            pl.BlockSpec((tr, lanes), lambda i, me_ref: (i, 0))],
                  out_specs=pl.BlockSpec((tr, lanes), lambda i, me_ref: (i, 0)), prefetch=1)(me, slabs, own)


def _cast_into_gathered(w, chip, by_cols, name):
    l, r, c = w.shape

    def body(chip_ref, w_ref, o_ref):
        o_ref[...] = w_ref[...].astype(BF16)

    if by_cols:
        shape, out = (l, r, N_CHIPS * c), pl.BlockSpec((None, r, c), lambda i, chip_ref: (i, 0, chip_ref[0]))
    else:
        shape, out = (l, N_CHIPS * r, c), pl.BlockSpec((None, r, c), lambda i, chip_ref: (i, chip_ref[0], 0))
    return _pcall(body, name=name, out_shape=_sds(shape, BF16), grid=(l,),
                  in_specs=[pl.BlockSpec((None, r, c), lambda i, chip_ref: (i, 0, 0))], out_specs=out, prefetch=1)(chip, w)


def _adamw_math(w, g, m, v):
    mn = ADAM_B1 * m + (1.0 - ADAM_B1) * g
    vn = ADAM_B2 * v + (1.0 - ADAM_B2) * (g * g)
    m_hat = mn / (1.0 - ADAM_B1 ** ADAM_STEP)
    v_hat = vn / (1.0 - ADAM_B2 ** ADAM_STEP)
    return -ADAM_LR * (m_hat / (jnp.sqrt(v_hat) + ADAM_EPS) + ADAM_WD * w), mn, vn


def _adamw(w, g, m, v, name, tr):
    r, c = w.shape

    def body(w_ref, g_ref, m_ref, v_ref, go_ref, d_ref, mo_ref, vo_ref):
        gv = g_ref[...]
        go_ref[...] = gv
        d_ref[...], mo_ref[...], vo_ref[...] = _adamw_math(w_ref[...], gv, m_ref[...], v_ref[...])

    blk = pl.BlockSpec((tr, c), lambda i: (i, 0))
    return _pcall(body, name=name, out_shape=tuple(_sds((r, c), F32) for _ in range(4)), grid=(r // tr,),
                  in_specs=[blk] * 4, out_specs=(blk, blk, blk, blk))(w, g, m, v)


def _adamw_small(groups, name):
    count = len(groups[0])
    shapes = [a.shape for a in groups[0]]
    as2d = [(math.prod(s[:-1]), s[-1]) for s in shapes]

    def body(*refs):
        for i in range(count):
            out = _adamw_math(*[refs[j * count + i][...] for j in range(4)])
            for j in range(3):
                refs[(4 + j) * count + i][...] = out[j]

    specs = [pl.BlockSpec(s, lambda i: (0, 0)) for s in as2d]
    res = _pcall(body, name=name, out_shape=tuple(_sds(s, F32) for _ in range(3) for s in as2d), grid=(1,),
                 in_specs=specs * 4, out_specs=tuple(specs * 3))(*[a.reshape(s) for grp in groups for a, s in zip(grp, as2d)])
    return [[res[j * count + i].reshape(shapes[i]) for i in range(count)] for j in range(3)]


AG_RIDES = {"in_proj": (0, ("out",), ("down",)), "attn_fwd": (0, (("up", (0, 1)),), ("out",)), "out_proj": (0, (), ("up",)),
            "up_proj": (1, ("in_t",), ()), "ffn_act_fwd": (1, ("down",), ()),
            "down_proj": (1, (("up", (2,)),), ("in_t",))}
AG_FIRST = ("in_t", "down")
RS_RIDES = {"down_proj_dx": "swap", "ffn_act_bwd": ("up",), "up_proj_dx": "share", "attn_bwd": ("in_t", "out", "down")}
RS_RIDES_LAST = {"down_proj_dx": "swap", "ffn_act_bwd": ("up",), "up_proj_dx": "share", "up_proj_dw": ("in_t",),
                 "mix_bwd": ("down",), "yb_norm_bwd": ("out",)}
EARLY = ("out", "up", "down")
EARLY_RIDES = {"mix_bwd": "swap", "attn_bwd": ("up", "down"), "in_proj_dx": ("out",)}


class _Reduction:
    def __init__(self, grads):
        self.grads, self.pairs, self.recv = grads, {}, {}


class _MeshSchedule:
    def __init__(self, wts, depth, c_arr, where):
        self.wts, self.depth, self.c_arr, self.where = wts, depth, c_arr, where
        self.pending, self.last, self.full, self.unshared = None, None, None, []

    def fwd_comms(self, l):
        rides = _Rides()
        for name, (off, ici, fwd) in AG_RIDES.items():
            if l + off == 0:
                ici = tuple(k if isinstance(k, str) else k[0] for k in ici)
                ici, fwd = (tuple(k for k in keys if k not in AG_FIRST) for keys in (ici, fwd))
            if l + off < self.depth and (ici or fwd):
                rides.add(name, lambda ride=(l + off, ici, fwd): _ag_comm(self.wts, *ride))
        return rides

    def _shared(self, full):
        self.full, self.unshared = full, []

    def _ride(self, red, what, swap_keys):
        if what == "swap":
            return _swap_comm(red.grads, swap_keys, lambda theirs: red.pairs.update(
                _pair_sum(red.grads, theirs, self.c_arr, swap_keys)))
        if what == "share":
            return _share_comm(self.unshared, self.full, self._shared)
        return _rs_comm(red.pairs, what, red.recv)

    def bwd_comms(self, l):
        rides = _Rides()
        if self.pending is not None:
            for name, what in (RS_RIDES_LAST if l == 0 else RS_RIDES).items():
                if what != "share" or self.unshared:
                    rides.add(name, lambda what=what, red=self.pending: self._ride(red, what, BIG))
        if l == 0:
            self.last = _Reduction(rides.grads)
            for name, what in EARLY_RIDES.items():
                rides.add(name, lambda what=what: self._ride(self.last, what, EARLY))
        return rides

    def _reduce(self, l, red):
        self.full = _quad_sum(red.pairs, red.recv, self.where, l, self.full)
        self.unshared = self.unshared + [l]

    def after_bwd(self, l, grads):
        if self.pending is not None:
            self._reduce(l + 1, self.pending)
        if self.full is None:
            geo = _piece_geo(grads)
            self.full = {k: jnp.zeros((self.depth, 2) + geo[k], F32) for k in BIG}
        self.pending = _Reduction(grads) if l > 0 else None
        if l == 0:
            self.last.grads = grads

    def finish(self, extra):
        red = self.last
        late = tuple(k for k in BIG if k not in red.pairs)
        _run_comm(self._ride(red, "swap", late), "rs_swap_halves")
        _run_comm(_both(_rs_comm(red.pairs, late, red.recv), extra), "rs_to_owners")
        self._reduce(0, red)
        _run_comm(_share_comm(self.unshared, self.full, self._shared), "rs_share")
        return self.full


SHARDED_SMALL = ("conv_a_w", "conv_c_w", "conv_f_w")
SMALL = ("norm_mix_g", "conv_a_w", "conv_c_w", "conv_c_b", "ln_c_g", "ln_c_b", "out_norm_g", "norm_ffn_g",
         "conv_f_w", "rel_bias", "final_g")
SLAB_ROWS = 16


def _pack(arrays):
    flat = jnp.concatenate([a.reshape(-1) for a in arrays])
    unit = SLAB_ROWS * LANES
    total = -(-flat.shape[0] // unit) * unit
    return jnp.pad(flat, (0, total - flat.shape[0])).reshape(-1, LANES)


def _unpack(slab, shapes):
    flat = slab.reshape(-1)
    out, off = [], 0
    for shp in shapes:
        size = math.prod(shp)
        out.append(flat[off:off + size].reshape(shp))
        off += size
    return out


def kernel(x, norm_mix_g, w_in, conv_a_w, conv_c_w, conv_c_b, ln_c_g, ln_c_b, out_norm_g, w_out, norm_ffn_g, w_up, conv_f_w, w_down, rel_bias, final_g, loss_target, m_norm_mix_g, m_w_in, m_conv_a_w, m_conv_c_w, m_conv_c_b, m_ln_c_g, m_ln_c_b, m_out_norm_g, m_w_out, m_norm_ffn_g, m_w_up, m_conv_f_w, m_w_down, m_rel_bias, m_final_g, v_norm_mix_g, v_w_in, v_conv_a_w, v_conv_c_w, v_conv_c_b, v_ln_c_g, v_ln_c_b, v_out_norm_g, v_w_out, v_norm_ffn_g, v_w_up, v_conv_f_w, v_w_down, v_rel_bias, v_final_g):
    weights = dict(norm_mix_g=norm_mix_g, w_in=w_in, conv_a_w=conv_a_w, conv_c_w=conv_c_w, conv_c_b=conv_c_b,
                   ln_c_g=ln_c_g, ln_c_b=ln_c_b, out_norm_g=out_norm_g, w_out=w_out, norm_ffn_g=norm_ffn_g, w_up=w_up,
                   conv_f_w=conv_f_w, w_down=w_down, rel_bias=rel_bias, final_g=final_g)
    mom_m = dict(norm_mix_g=m_norm_mix_g, w_in=m_w_in, conv_a_w=m_conv_a_w, conv_c_w=m_conv_c_w, conv_c_b=m_conv_c_b,
                 ln_c_g=m_ln_c_g, ln_c_b=m_ln_c_b, out_norm_g=m_out_norm_g, w_out=m_w_out, norm_ffn_g=m_norm_ffn_g,
                 w_up=m_w_up, conv_f_w=m_conv_f_w, w_down=m_w_down, rel_bias=m_rel_bias, final_g=m_final_g)
    mom_v = dict(norm_mix_g=v_norm_mix_g, w_in=v_w_in, conv_a_w=v_conv_a_w, conv_c_w=v_conv_c_w, conv_c_b=v_conv_c_b,
                 ln_c_g=v_ln_c_g, ln_c_b=v_ln_c_b, out_norm_g=v_out_norm_g, w_out=v_w_out, norm_ffn_g=v_norm_ffn_g,
                 w_up=v_w_up, conv_f_w=v_conv_f_w, w_down=v_w_down, rel_bias=v_rel_bias, final_g=v_final_g)
    xi, yi, ci = _me()
    chip = _chip_of(xi, yi)
    c_arr = jnp.reshape(ci, (1,)).astype(I32)
    chip_arr = jnp.reshape(chip, (1,)).astype(I32)
    me_arr = jnp.reshape(4 * xi + 2 * yi + ci, (1,)).astype(I32)
    where = jnp.stack([chip, ci]).astype(I32)
    depth = w_out.shape[0]

    wts = {"in_t": _cast_into_gathered(jnp.swapaxes(w_in, 1, 2), chip_arr, False, "cast_in"),
           "out": _cast_into_gathered(w_out, chip_arr, False, "cast_out"),
           "up": _cast_into_gathered(w_up, chip_arr, True, "cast_up"),
           "down": _cast_into_gathered(w_down, chip_arr, False, "cast_down")}
    store = {}
    _run_comm(_both(_ag_comm(wts, 0, AG_FIRST, AG_FIRST),
                    _small_gather_comm(_pack([weights[n] for n in SHARDED_SMALL]), store)), "ag_weights")
    prm = {n: weights[n] for n in SMALL if n not in SHARDED_SMALL}
    per_chip = [_unpack(store["small"][j], [weights[n].shape for n in SHARDED_SMALL]) for j in range(N_CHIPS)]
    for i, n in enumerate(SHARDED_SMALL):
        prm[n] = jnp.concatenate([per_chip[j][i] for j in range(N_CHIPS)], axis=-1)

    sched = _MeshSchedule(wts, depth, c_arr, where)
    loss_row, dx, small, d_rel, d_final = _local_step(x[0], loss_target[0], wts, prm, sched)
    loss = lax.psum(loss_row[0, 0], ("x", "y", "c"))

    stacked = {n: jnp.stack([small[l][n] for l in range(depth)]) for n in small[0]}
    stacked["rel_bias"] = d_rel
    stacked["final_g"] = d_final
    full_shapes = [stacked[n].shape for n in SMALL]
    partial = _pack([stacked[n] for n in SMALL])
    reduced = sched.finish(_gather_comm(partial, lambda res: store.update(partials=res)))

    grads = {}
    shard_shapes = {"in_t": jnp.swapaxes(w_in, 1, 2).shape, "out": w_out.shape, "up": w_up.shape, "down": w_down.shape}
    red = {k: reduced[k].reshape(shard_shapes[k]) for k in BIG}
    grads["w_in"] = jnp.swapaxes(red["in_t"], 1, 2)
    grads["w_out"], grads["w_up"], grads["w_down"] = red["out"], red["up"], red["down"]
    delta, new_m, new_v = {}, {}, {}
    for n in ("w_in", "w_out", "w_up", "w_down"):
        shp = weights[n].shape
        flat = lambda a, shp=shp: a.reshape(shp[0] * shp[1], shp[2])
        tile = max(t for t in range(8, 257, 8) if shp[1] % t == 0)
        g, d, mn, vn = _adamw(flat(weights[n]), flat(grads[n]), flat(mom_m[n]), flat(mom_v[n]), "adamw_" + n, tile)
        grads[n], delta[n], new_m[n], new_v[n] = g.reshape(shp), d.reshape(shp), mn.reshape(shp), vn.reshape(shp)

    summed = _unpack(_sum_slabs(store["partials"], partial, me_arr), full_shapes)
    for n, g in zip(SMALL, summed):
        if n in SHARDED_SMALL:
            width = weights[n].shape[-1]
            g = lax.dynamic_slice_in_dim(g, chip * width, width, axis=g.ndim - 1)
        grads[n] = g
    res = _adamw_small([[src[n] for n in SMALL] for src in (weights, grads, mom_m, mom_v)], "adamw_small")
    for i, n in enumerate(SMALL):
        delta[n], new_m[n], new_v[n] = res[0][i], res[1][i], res[2][i]

    order = ("norm_mix_g", "w_in", "conv_a_w", "conv_c_w", "conv_c_b", "ln_c_g", "ln_c_b", "out_norm_g", "w_out",
             "norm_ffn_g", "w_up", "conv_f_w", "w_down", "rel_bias", "final_g")
    return (loss, dx[None], *[grads[n] for n in order], *[delta[n] for n in order], *[new_m[n] for n in order],
            *[new_v[n] for n in order])
```

```python
import functools
import math

import numpy as np
import jax
import jax.numpy as jnp
from jax import lax
from jax.experimental import pallas as pl
from jax.experimental.pallas import tpu as pltpu

F32 = jnp.float32
BF16 = jnp.bfloat16
I32 = jnp.int32

EPS = 1e-6
NEG = -1e30
D_HEAD = 64
LANES = 128
BLK = 128
ATTN_GROUP_FWD = 16
ATTN_GROUP_BWD = 16
DILATED_BRANCHES = ((128, 1), (512, 4), (2048, 16))
NUM_BUCKETS = 32
MAX_DISTANCE = 2048
SHORT_CONV = 3
CONFORMER_CONV = 31
FFN_CONV = 3
PAD_SHORT = 8
PAD_LONG = 32
ROW_CHUNK = 256
PROJ_ROWS = 512
FFN_ROWS = 64
V7X_VMEM_BYTES = 64 * 1024 * 1024
VMEM_REQUEST = V7X_VMEM_BYTES * 7 // 8

ADAM_LR = 0.001
ADAM_B1 = 0.9
ADAM_B2 = 0.999
ADAM_EPS = 1e-08
ADAM_WD = 0.01
ADAM_STEP = 10

MESH = pl.DeviceIdType.MESH
ANY = pl.BlockSpec(memory_space=pl.ANY)


def _sds(shape, dtype):
    return jax.ShapeDtypeStruct(tuple(shape), dtype)


class _Comm:
    def __init__(self, ins, out_shapes, aliases, sems, start, finish, done):
        self.ins, self.out_shapes, self.aliases, self.sems = list(ins), list(out_shapes), dict(aliases), list(sems)
        self.start, self.finish, self.done = start, finish, done


def _pcall(body, *, name, out_shape, grid=(), in_specs=None, out_specs=None, scratch_shapes=(), vmem=VMEM_REQUEST,
           aliases=None, prefetch=0, comm=None):
    params = pltpu.CompilerParams(dimension_semantics=("arbitrary",) * len(grid), vmem_limit_bytes=vmem)
    single = not isinstance(out_shape, (tuple, list))
    outs = [out_shape] if single else list(out_shape)
    ospecs = [out_specs] if single else list(out_specs)
    ispecs, scratch, aliases = list(in_specs), list(scratch_shapes), dict(aliases or {})
    n_in, n_out, n_scr = len(ispecs), len(outs), len(scratch)
    kernel_body = body
    if comm is not None:
        n_ci, n_co = len(comm.ins), len(comm.out_shapes)

        def kernel_body(*refs):
            pre, rest = refs[:prefetch], refs[prefetch:]
            core_in, c_in = rest[:n_in], rest[n_in:n_in + n_ci]
            o0 = n_in + n_ci
            core_out, c_out = rest[o0:o0 + n_out], rest[o0 + n_out:o0 + n_out + n_co]
            s0 = o0 + n_out + n_co
            core_scr, c_sem = rest[s0:s0 + n_scr], rest[s0 + n_scr:]
            first = functools.reduce(jnp.logical_and, [pl.program_id(a) == 0 for a in range(len(grid))])
            last = functools.reduce(jnp.logical_and, [pl.program_id(a) == grid[a] - 1 for a in range(len(grid))])
            pl.when(first)(lambda: comm.start(c_in, c_out, c_sem))
            body(*pre, *core_in, *core_out, *core_scr)
            pl.when(last)(lambda: comm.finish(c_in, c_out, c_sem))

        for i, o in comm.aliases.items():
            aliases[prefetch + n_in + i] = n_out + o
        ispecs += [ANY] * n_ci
        ospecs += [ANY] * n_co
        outs += comm.out_shapes
        scratch += comm.sems
    if prefetch:
        spec = pltpu.PrefetchScalarGridSpec(num_scalar_prefetch=prefetch, grid=grid, in_specs=ispecs,
                                            out_specs=tuple(ospecs), scratch_shapes=scratch)
        call = pl.pallas_call(kernel_body, name=name, out_shape=tuple(outs), grid_spec=spec,
                              input_output_aliases=aliases, compiler_params=params)
    else:
        call = pl.pallas_call(kernel_body, name=name, out_shape=tuple(outs), grid=grid, in_specs=ispecs,
                              out_specs=tuple(ospecs), scratch_shapes=scratch, input_output_aliases=aliases,
                              compiler_params=params)

    def run(*args):
        res = call(*args, *(comm.ins if comm is not None else ()))
        if comm is not None:
            comm.done(res[n_out:])
        return res[0] if single else tuple(res[:n_out])

    return run


def _both(a, b):
    def split(refs, na):
        return refs[:na], refs[na:]

    def run(which):
        def go(ins, outs, sems):
            for comm, i, o, s in zip((a, b), split(ins, len(a.ins)), split(outs, len(a.out_shapes)), split(sems, len(a.sems))):
                getattr(comm, which)(i, o, s)
        return go

    def done(res):
        a.done(res[:len(a.out_shapes)])
        b.done(res[len(a.out_shapes):])

    aliases = dict(a.aliases)
    aliases.update({len(a.ins) + i: len(a.out_shapes) + o for i, o in b.aliases.items()})
    return _Comm(a.ins + b.ins, a.out_shapes + b.out_shapes, aliases, a.sems + b.sems, run("start"), run("finish"), done)


def _run_comm(comm, name):
    def body(*refs):
        n_ci, n_co = len(comm.ins), len(comm.out_shapes)
        c_in, c_out, c_sem = refs[:n_ci], refs[n_ci:n_ci + n_co], refs[n_ci + n_co:]
        comm.start(c_in, c_out, c_sem)
        comm.finish(c_in, c_out, c_sem)

    res = pl.pallas_call(body, name=name, out_shape=tuple(comm.out_shapes), in_specs=[ANY] * len(comm.ins),
                         out_specs=tuple([ANY] * len(comm.out_shapes)), scratch_shapes=comm.sems,
                         input_output_aliases=comm.aliases)(*comm.ins)
    comm.done(res)


def _dot(a, b):
    return lax.dot_general(a, b, (((1,), (0,)), ((), ())), preferred_element_type=F32)


def _dot_nt(a, b):
    return lax.dot_general(a, b, (((1,), (1,)), ((), ())), preferred_element_type=F32)


def _dot_tn(a, b):
    return lax.dot_general(a, b, (((0,), (0,)), ((), ())), preferred_element_type=F32)


def _sigmoid(x):
    return 1.0 / (1.0 + jnp.exp(-x))


def _rstd(x):
    return lax.rsqrt(jnp.mean(x * x, axis=-1, keepdims=True) + EPS)


def _rms_fwd(x, g, name):
    s, d = x.shape
    tm = ROW_CHUNK

    def body(x_ref, g_ref, o_ref):
        xv = x_ref[...]
        o_ref[...] = (xv * _rstd(xv) * g_ref[...]).astype(BF16)

    return _pcall(body, name=name, out_shape=_sds((s, d), BF16), grid=(s // tm,),
                  in_specs=[pl.BlockSpec((tm, d), lambda i: (i, 0)), pl.BlockSpec((1, d), lambda i: (0, 0))],
                  out_specs=pl.BlockSpec((tm, d), lambda i: (i, 0)))(x, g)


def _final_loss(x, g, tgt, name):
    s, d = x.shape
    tm = ROW_CHUNK

    def body(x_ref, g_ref, t_ref, loss_ref, dx_ref, dxb_ref, dg_ref):
        i = pl.program_id(0)
        xv = x_ref[...]
        r = _rstd(xv)
        xh = xv * r
        e = xh * g_ref[...] - t_ref[...]
        lpart = 0.5 * jnp.sum(jnp.mean(e * e, axis=-1, keepdims=True), axis=0, keepdims=True)
        dy = e * (1.0 / d)
        gd = dy * g_ref[...]
        dx = r * (gd - xh * jnp.mean(gd * xh, axis=-1, keepdims=True))
        dx_ref[...] = dx
        dxb_ref[...] = dx.astype(BF16)
        part = jnp.sum(dy * xh, axis=0, keepdims=True)
        lrow = jnp.broadcast_to(lpart, (1, LANES))

        @pl.when(i == 0)
        def _():
            dg_ref[...] = part
            loss_ref[...] = lrow

        @pl.when(i > 0)
        def _():
            dg_ref[...] += part
            loss_ref[...] += lrow

    row = pl.BlockSpec((tm, d), lambda i: (i, 0))
    vec = pl.BlockSpec((1, d), lambda i: (0, 0))
    return _pcall(body, name=name,
                  out_shape=(_sds((1, LANES), F32), _sds((s, d), F32), _sds((s, d), BF16), _sds((1, d), F32)),
                  grid=(s // tm,), in_specs=[row, vec, row],
                  out_specs=(pl.BlockSpec((1, LANES), lambda i: (0, 0)), row, row, vec))(x, g, tgt)


def _mm_n(a, b, layer, *, nt, tn, out_dtype, name, resid=None, b_part=0, comm=None):
    s, k = a.shape
    n = b.shape[1] if nt else b.shape[2]
    rows = 512

    def body(a_ref, b_ref, *refs):
        o_ref = refs[-1]
        bv = b_ref[...]
        for r0 in range(0, s, rows):
            av = a_ref[r0:r0 + rows, :]
            prod = _dot_nt(av, bv) if nt else _dot(av, bv)
            if resid is not None:
                prod = refs[0][r0:r0 + rows, :] + prod
            o_ref[r0:r0 + rows, :] = prod.astype(out_dtype)

    b_spec = (pl.BlockSpec((None, tn, k), lambda j: (layer, j, b_part)) if nt
              else pl.BlockSpec((None, k, tn), lambda j: (layer, b_part, j)))
    col = pl.BlockSpec((s, tn), lambda j: (0, j))
    extra = () if resid is None else (resid,)
    return _pcall(body, name=name, out_shape=_sds((s, n), out_dtype), grid=(n // tn,),
                  in_specs=[pl.BlockSpec((s, k), lambda j: (0, 0)), b_spec] + [col] * len(extra),
                  out_specs=col, comm=comm)(a, b, *extra)


def _mm_tn(a, b, *, t, name):
    s, ka = a.shape
    n = b.shape[1]

    def body(a_ref, b_ref, o_ref):
        o_ref[...] = _dot_tn(a_ref[...], b_ref[...]).astype(BF16)

    return _pcall(body, name=name, out_shape=_sds((ka, n), BF16), grid=(ka // t,),
                  in_specs=[pl.BlockSpec((s, t), lambda i: (0, i)), pl.BlockSpec((s, n), lambda i: (0, 0))],
                  out_specs=pl.BlockSpec((t, n), lambda i: (i, 0)))(a, b)


def _mm_tn_pieces(pieces, b, *, t, name):
    s, n = b.shape
    blocks = [p.shape[1] // t for p in pieces]
    starts = [sum(blocks[:i]) for i in range(len(pieces))]

    def body(*refs):
        p_refs, b_ref, o_ref = refs[:len(pieces)], refs[len(pieces)], refs[len(pieces) + 1]
        j = pl.program_id(0)
        for p_ref, start, count in zip(p_refs, starts, blocks):
            @pl.when((j >= start) & (j < start + count))
            def _(p_ref=p_ref):
                o_ref[...] = _dot_tn(p_ref[...], b_ref[...]).astype(BF16)

    specs = [pl.BlockSpec((s, t), lambda j, start=start, count=count: (0, jnp.clip(j - start, 0, count - 1)))
             for start, count in zip(starts, blocks)]
    return _pcall(body, name=name, out_shape=_sds((sum(blocks) * t, n), BF16), grid=(sum(blocks),),
                  in_specs=specs + [pl.BlockSpec((s, n), lambda j: (0, 0))],
                  out_specs=pl.BlockSpec((t, n), lambda j: (j, 0)))(*pieces, b)


def _mm_tn2(a, b_lo, b_hi, *, t, name, comm=None):
    s, ka = a.shape
    half = b_lo.shape[1]
    nb = half // t

    def body(a_ref, lo_ref, hi_ref, o_ref):
        j = pl.program_id(0)

        @pl.when(j < nb)
        def _():
            o_ref[...] = _dot_tn(a_ref[...], lo_ref[...]).astype(BF16)

        @pl.when(j >= nb)
        def _():
            o_ref[...] = _dot_tn(a_ref[...], hi_ref[...]).astype(BF16)

    return _pcall(body, name=name, out_shape=_sds((ka, 2 * half), BF16), grid=(2 * nb,),
                  in_specs=[pl.BlockSpec((s, ka), lambda j: (0, 0)),
                            pl.BlockSpec((s, t), lambda j: (0, jnp.minimum(j, nb - 1))),
                            pl.BlockSpec((s, t), lambda j: (0, jnp.maximum(j - nb, 0)))],
                  out_specs=pl.BlockSpec((ka, t), lambda j: (0, j)), comm=comm)(a, b_lo, b_hi)


SUBLANES = 8


def _tap_windows(win, width, lead, rows):
    offs = [lead + k for k in range(width)]
    if width <= SUBLANES:
        return [win[o:o + rows, :] for o in offs]
    n = win.shape[0]
    out = {}
    for r in sorted({o % SUBLANES for o in offs}):
        base = win if r == 0 else pltpu.roll(win, n - r, axis=0)
        for o in offs:
            if o % SUBLANES == r:
                out[o - lead] = base[o - r:o - r + rows, :]
    return [out[k] for k in range(width)]


def _conv_taps(taps, w_ref):
    acc = None
    for k, tap in enumerate(taps):
        term = w_ref[pl.ds(k, 1), :] * tap
        acc = term if acc is None else acc + term
    return acc


def _causal_taps(win, width, pad, rows):
    return _tap_windows(win, width, pad - (width - 1), rows)


def _anticausal_taps(win, width, rows):
    return _tap_windows(win, width, 0, rows)[::-1]


def _conv_wgrad(dw_ref, g, taps):
    for k, tap in enumerate(taps):
        dw_ref[pl.ds(k, 1), :] += jnp.sum(g * tap, axis=0, keepdims=True)


def _mixer_a_fwd(ab, taps_t, wa_ref):
    ct = _conv_taps(taps_t, wa_ref)
    return ab * ct, ct


def _mixer_c_fwd(taps_u, wc_ref, cb_ref, lg_ref, lb_ref):
    u = _conv_taps(taps_u, wc_ref) + cb_ref[...]
    mu = jnp.mean(u, axis=-1, keepdims=True)
    uc = u - mu
    rs = lax.rsqrt(jnp.mean(uc * uc, axis=-1, keepdims=True) + EPS)
    uh = uc * rs
    ln = uh * lg_ref[...] + lb_ref[...]
    sg = _sigmoid(ln)
    return ln * sg, ln, sg, uh, rs


def _mix_fwd(z, wa, wc, cb, lg, lb, ga, gc, name):
    s = z.shape[0]
    w = wa.shape[1]
    nblk = z.shape[1] // w
    rc = ROW_CHUNK

    def body(ah_ref, ab_ref, ac_ref, cv_ref, cg_ref, wa_ref, wc_ref, cb_ref, lg_ref, lb_ref, ga_ref, gc_ref,
             ya_ref, yc_ref, tpad, upad):
        tpad[pl.ds(0, PAD_SHORT), :] = jnp.zeros((PAD_SHORT, w), F32)
        upad[pl.ds(0, PAD_LONG), :] = jnp.zeros((PAD_LONG, w), F32)

        def chunk(i, carry):
            base = pl.multiple_of(i * rc, rc)
            rows = pl.ds(base, rc)
            ah, ab, ac = ah_ref[rows, :], ab_ref[rows, :], ac_ref[rows, :]
            tpad[pl.ds(base + PAD_SHORT, rc), :] = ac * ah
            ya, _ = _mixer_a_fwd(ab, _causal_taps(tpad[pl.ds(base, rc + PAD_SHORT), :], SHORT_CONV, PAD_SHORT, rc), wa_ref)
            ya_ref[rows, :] = (ya * _rstd(ya) * ga_ref[...]).astype(BF16)
            upad[pl.ds(base + PAD_LONG, rc), :] = cv_ref[rows, :] * _sigmoid(cg_ref[rows, :])
            taps_u = _causal_taps(upad[pl.ds(base, rc + PAD_LONG), :], CONFORMER_CONV, PAD_LONG, rc)
            yc = _mixer_c_fwd(taps_u, wc_ref, cb_ref, lg_ref, lb_ref)[0]
            yc_ref[rows, :] = (yc * _rstd(yc) * gc_ref[...]).astype(BF16)
            return carry

        lax.fori_loop(0, s // rc, chunk, 0)

    def zblk(j):
        return pl.BlockSpec((s, w), lambda i: (0, j))

    def whole(a):
        return pl.BlockSpec(a.shape, lambda i: (0, 0))

    return _pcall(
        body, name=name, out_shape=(_sds((s, w), BF16), _sds((s, w), BF16)), grid=(1,),
        in_specs=[zblk(0), zblk(1), zblk(2), zblk(nblk - 2), zblk(nblk - 1)] + [whole(a) for a in (wa, wc, cb, lg, lb, ga, gc)],
        out_specs=(pl.BlockSpec((s, w), lambda i: (0, 0)), pl.BlockSpec((s, w), lambda i: (0, 0))),
        scratch_shapes=[pltpu.VMEM((s + PAD_SHORT, w), F32), pltpu.VMEM((s + PAD_LONG, w), F32)],
    )(z, z, z, z, z, wa, wc, cb, lg, lb, ga, gc)


def _mix_bwd(z, dy, wa, wc, cb, lg, lb, ga, gc, name, comm=None):
    s = z.shape[0]
    w = wa.shape[1]
    nblk = z.shape[1] // w
    nyb = dy.shape[1] // w
    rc = ROW_CHUNK

    def body(ah_ref, ab_ref, ac_ref, cv_ref, cg_ref, dya_ref, dyc_ref,
             wa_ref, wc_ref, cb_ref, lg_ref, lb_ref, ga_ref, gc_ref,
             dza_ref, dzc_ref, dwa_ref, dwc_ref, dcb_ref, dlg_ref, dlb_ref, dga_ref, dgc_ref,
             tpad, upad, dctp, dup):
        tpad[pl.ds(0, PAD_SHORT), :] = jnp.zeros((PAD_SHORT, w), F32)
        upad[pl.ds(0, PAD_LONG), :] = jnp.zeros((PAD_LONG, w), F32)
        dctp[pl.ds(s, PAD_SHORT), :] = jnp.zeros((PAD_SHORT, w), F32)
        dup[pl.ds(s, PAD_LONG), :] = jnp.zeros((PAD_LONG, w), F32)
        for ref in (dwa_ref, dwc_ref, dcb_ref, dlg_ref, dlb_ref, dga_ref, dgc_ref):
            ref[...] = jnp.zeros(ref.shape, F32)

        def rms_bwd(y, g_ref, dyn, dg_ref):
            r = _rstd(y)
            yh = y * r
            gd = dyn * g_ref[...]
            dg_ref[...] += jnp.sum(dyn * yh, axis=0, keepdims=True)
            return r * (gd - yh * jnp.mean(gd * yh, axis=-1, keepdims=True))

        def first(i, carry):
            base = pl.multiple_of(i * rc, rc)
            rows = pl.ds(base, rc)
            ah, ab, ac = ah_ref[rows, :], ab_ref[rows, :], ac_ref[rows, :]
            tpad[pl.ds(base + PAD_SHORT, rc), :] = ac * ah
            taps_t = _causal_taps(tpad[pl.ds(base, rc + PAD_SHORT), :], SHORT_CONV, PAD_SHORT, rc)
            ya, ct = _mixer_a_fwd(ab, taps_t, wa_ref)
            dya = rms_bwd(ya, ga_ref, dya_ref[rows, :], dga_ref)
            dza_ref[rows, w:2 * w] = (dya * ct).astype(BF16)
            dct = dya * ab
            dctp[rows, :] = dct
            _conv_wgrad(dwa_ref, dct, taps_t)

            upad[pl.ds(base + PAD_LONG, rc), :] = cv_ref[rows, :] * _sigmoid(cg_ref[rows, :])
            taps_u = _causal_taps(upad[pl.ds(base, rc + PAD_LONG), :], CONFORMER_CONV, PAD_LONG, rc)
            yc, ln, sg, uh, rs = _mixer_c_fwd(taps_u, wc_ref, cb_ref, lg_ref, lb_ref)
            dyc = rms_bwd(yc, gc_ref, dyc_ref[rows, :], dgc_ref)
            dln = dyc * (sg * (1.0 + ln * (1.0 - sg)))
            dlg_ref[...] += jnp.sum(dln * uh, axis=0, keepdims=True)
            dlb_ref[...] += jnp.sum(dln, axis=0, keepdims=True)
            duh = dln * lg_ref[...]
            du = rs * (duh - jnp.mean(duh, axis=-1, keepdims=True) - uh * jnp.mean(duh * uh, axis=-1, keepdims=True))
            dcb_ref[...] += jnp.sum(du, axis=0, keepdims=True)
            dup[rows, :] = du
            _conv_wgrad(dwc_ref, du, taps_u)
            return carry

        lax.fori_loop(0, s // rc, first, 0)

        def second(i, carry):
            base = pl.multiple_of(i * rc, rc)
            rows = pl.ds(base, rc)
            dt = _conv_taps(_anticausal_taps(dctp[pl.ds(base, rc + PAD_SHORT), :], SHORT_CONV, rc), wa_ref)
            dza_ref[rows, 0:w] = (dt * ac_ref[rows, :]).astype(BF16)
            dza_ref[rows, 2 * w:3 * w] = (dt * ah_ref[rows, :]).astype(BF16)
            du0 = _conv_taps(_anticausal_taps(dup[pl.ds(base, rc + PAD_LONG), :], CONFORMER_CONV, rc), wc_ref)
            sg = _sigmoid(cg_ref[rows, :])
            dzc_ref[rows, 0:w] = (du0 * sg).astype(BF16)
            dzc_ref[rows, w:2 * w] = (du0 * cv_ref[rows, :] * sg * (1.0 - sg)).astype(BF16)
            return carry

        lax.fori_loop(0, s // rc, second, 0)

    def blk(j):
        return pl.BlockSpec((s, w), lambda i: (0, j))

    def whole(a):
        return pl.BlockSpec(tuple(a.shape), lambda i: (0, 0))

    params = (wa, wc, cb, lg, lb, ga, gc)
    outs = (_sds((s, 3 * w), BF16), _sds((s, 2 * w), BF16)) + tuple(_sds(p.shape, F32) for p in params)
    return _pcall(
        body, name=name, out_shape=outs, grid=(1,),
        in_specs=[blk(0), blk(1), blk(2), blk(nblk - 2), blk(nblk - 1), blk(0), blk(nyb - 1)] + [whole(p) for p in params],
        out_specs=tuple(whole(o) for o in outs),
        scratch_shapes=[pltpu.VMEM((s + PAD_SHORT, w), F32), pltpu.VMEM((s + PAD_LONG, w), F32),
                        pltpu.VMEM((s + PAD_SHORT, w), F32), pltpu.VMEM((s + PAD_LONG, w), F32)], comm=comm,
    )(z, z, z, z, z, dy, dy, *params)


def _ffn_act_fwd(up, wf, name, comm=None):
    s, f2 = up.shape
    f = f2 // 2
    tc = 256
    nb = f // tc
    rc = FFN_ROWS

    def body(g_ref, v_ref, wg_ref, wv_ref, o_ref, gpad, vpad):
        gpad[pl.ds(0, PAD_SHORT), :] = jnp.zeros((PAD_SHORT, tc), F32)
        vpad[pl.ds(0, PAD_SHORT), :] = jnp.zeros((PAD_SHORT, tc), F32)

        def chunk(i, carry):
            base = pl.multiple_of(i * rc, rc)
            rows = pl.ds(base, rc)
            gpad[pl.ds(base + PAD_SHORT, rc), :] = g_ref[rows, :].astype(F32)
            vpad[pl.ds(base + PAD_SHORT, rc), :] = v_ref[rows, :].astype(F32)
            gc = _conv_taps(_causal_taps(gpad[pl.ds(base, rc + PAD_SHORT), :], FFN_CONV, PAD_SHORT, rc), wg_ref)
            vc = _conv_taps(_causal_taps(vpad[pl.ds(base, rc + PAD_SHORT), :], FFN_CONV, PAD_SHORT, rc), wv_ref)
            o_ref[rows, :] = (gc * _sigmoid(gc) * vc).astype(BF16)
            return carry

        lax.fori_loop(0, s // rc, chunk, 0)

    return _pcall(
        body, name=name, out_shape=_sds((s, f), BF16), grid=(nb,),
        in_specs=[pl.BlockSpec((s, tc), lambda j: (0, j)), pl.BlockSpec((s, tc), lambda j: (0, j + nb)),
                  pl.BlockSpec((FFN_CONV, tc), lambda j: (0, j)), pl.BlockSpec((FFN_CONV, tc), lambda j: (0, j + nb))],
        out_specs=pl.BlockSpec((s, tc), lambda j: (0, j)),
        scratch_shapes=[pltpu.VMEM((s + PAD_SHORT, tc), F32), pltpu.VMEM((s + PAD_SHORT, tc), F32)], comm=comm,
    )(up, up, wf, wf)


def _ffn_act_bwd(up, dact, wf, name, comm=None):
    s, f2 = up.shape
    f = f2 // 2
    tc = 256
    nb = f // tc
    rc = FFN_ROWS

    def body(g_ref, v_ref, da_ref, wg_ref, wv_ref, act_ref, dg_ref, dv_ref, dwg_ref, dwv_ref, gpad, vpad, dgp, dvp):
        gpad[pl.ds(0, PAD_SHORT), :] = jnp.zeros((PAD_SHORT, tc), F32)
        vpad[pl.ds(0, PAD_SHORT), :] = jnp.zeros((PAD_SHORT, tc), F32)
        dgp[pl.ds(s, PAD_SHORT), :] = jnp.zeros((PAD_SHORT, tc), F32)
        dvp[pl.ds(s, PAD_SHORT), :] = jnp.zeros((PAD_SHORT, tc), F32)
        dwg_ref[...] = jnp.zeros((FFN_CONV, tc), F32)
        dwv_ref[...] = jnp.zeros((FFN_CONV, tc), F32)

        def first(i, carry):
            base = pl.multiple_of(i * rc, rc)
            rows = pl.ds(base, rc)
            gpad[pl.ds(base + PAD_SHORT, rc), :] = g_ref[rows, :].astype(F32)
            vpad[pl.ds(base + PAD_SHORT, rc), :] = v_ref[rows, :].astype(F32)
            taps_g = _causal_taps(gpad[pl.ds(base, rc + PAD_SHORT), :], FFN_CONV, PAD_SHORT, rc)
            taps_v = _causal_taps(vpad[pl.ds(base, rc + PAD_SHORT), :], FFN_CONV, PAD_SHORT, rc)
            gc = _conv_taps(taps_g, wg_ref)
            vc = _conv_taps(taps_v, wv_ref)
            sg = _sigmoid(gc)
            silu = gc * sg
            act_ref[rows, :] = (silu * vc).astype(BF16)
            da = da_ref[rows, :].astype(F32)
            dgc = da * vc * (sg * (1.0 + gc * (1.0 - sg)))
            dvc = da * silu
            dgp[rows, :] = dgc
            dvp[rows, :] = dvc
            _conv_wgrad(dwg_ref, dgc, taps_g)
            _conv_wgrad(dwv_ref, dvc, taps_v)
            return carry

        lax.fori_loop(0, s // rc, first, 0)

        def second(i, carry):
            base = pl.multiple_of(i * rc, rc)
            rows = pl.ds(base, rc)
            dg_ref[rows, :] = _conv_taps(_anticausal_taps(dgp[pl.ds(base, rc + PAD_SHORT), :], FFN_CONV, rc), wg_ref).astype(BF16)
            dv_ref[rows, :] = _conv_taps(_anticausal_taps(dvp[pl.ds(base, rc + PAD_SHORT), :], FFN_CONV, rc), wv_ref).astype(BF16)
            return carry

        lax.fori_loop(0, s // rc, second, 0)

    lo = pl.BlockSpec((s, tc), lambda j: (0, j))
    hi = pl.BlockSpec((s, tc), lambda j: (0, j + nb))
    wlo = pl.BlockSpec((FFN_CONV, tc), lambda j: (0, j))
    whi = pl.BlockSpec((FFN_CONV, tc), lambda j: (0, j + nb))
    act, dgate, dval, dwg, dwv = _pcall(
        body, name=name,
        out_shape=(_sds((s, f), BF16), _sds((s, f), BF16), _sds((s, f), BF16), _sds((FFN_CONV, f), F32), _sds((FFN_CONV, f), F32)),
        grid=(nb,), in_specs=[lo, hi, lo, wlo, whi], out_specs=(lo, lo, lo, wlo, wlo),
        scratch_shapes=[pltpu.VMEM((s + PAD_SHORT, tc), F32) for _ in range(4)], comm=comm,
    )(up, up, dact, wf, wf)
    return act, dgate, dval, jnp.concatenate([dwg, dwv], axis=1)


def _out_proj(yan, yb, ycn, gb, x, w_out, layer, g_next, name, comm=None):
    s, w = yan.shape
    wb = yb.shape[1]
    d = x.shape[1]
    tm = PROJ_ROWS

    def body(ya_ref, yb_ref, yc_ref, gb_ref, x_ref, w_ref, g_ref, y_ref, xm_ref, h_ref):
        ybv = yb_ref[...]
        y = jnp.concatenate([ya_ref[...], (ybv * _rstd(ybv) * gb_ref[...]).astype(BF16), yc_ref[...]], axis=1)
        y_ref[...] = y
        xm = x_ref[...] + _dot(y, w_ref[...])
        xm_ref[...] = xm
        h_ref[...] = (xm * _rstd(xm) * g_ref[...]).astype(BF16)

    def rows(width):
        return pl.BlockSpec((tm, width), lambda i: (i, 0))

    def vec(width):
        return pl.BlockSpec((1, width), lambda i: (0, 0))

    return _pcall(body, name=name, out_shape=(_sds((s, d), BF16), _sds((s, d), F32), _sds((s, d), BF16)), grid=(s // tm,),
                  in_specs=[rows(w), rows(wb), rows(w), vec(wb), rows(d), pl.BlockSpec((None, d, d), lambda i: (layer, 0, 0)), vec(d)],
                  out_specs=(rows(d), rows(d), rows(d)), comm=comm)(yan, yb, ycn, gb, x, w_out, g_next)


def _down_proj(act, w_down, layer, x_mid, g_next, name, comm=None):
    s, f = act.shape
    d = x_mid.shape[1]
    tm = PROJ_ROWS

    def body(a_ref, w_ref, x_ref, *refs):
        xo = x_ref[...] + _dot(a_ref[...], w_ref[...])
        refs[-2 if g_next is not None else -1][...] = xo
        if g_next is not None:
            refs[-1][...] = (xo * _rstd(xo) * refs[0][...]).astype(BF16)

    row = pl.BlockSpec((tm, d), lambda i: (i, 0))
    ins = [act, w_down, x_mid] + ([g_next] if g_next is not None else [])
    in_specs = [pl.BlockSpec((tm, f), lambda i: (i, 0)), pl.BlockSpec((None, f, d), lambda i: (layer, 0, 0)), row]
    in_specs += [pl.BlockSpec((1, d), lambda i: (0, 0))] if g_next is not None else []
    outs = (_sds((s, d), F32), _sds((s, d), BF16)) if g_next is not None else (_sds((s, d), F32),)
    res = _pcall(body, name=name, out_shape=outs, grid=(s // tm,), in_specs=in_specs, out_specs=tuple([row] * len(outs)),
                 comm=comm)(*ins)
    return (res[0], res[1]) if g_next is not None else (res[0], None)


def _proj_dx(pieces, w, layer, nt, x, g, dres, name, tm, comm=None):
    s, d = x.shape
    widths = [p.shape[1] for p in pieces]

    def body(*refs):
        p_refs, (w_ref, x_ref, g_ref, dres_ref, dx_ref, dxb_ref, dg_ref) = refs[:len(pieces)], refs[len(pieces):]
        i = pl.program_id(0)
        dh, off = None, 0
        for p_ref, width in zip(p_refs, widths):
            part = _dot_nt(p_ref[...], w_ref[:, off:off + width]) if nt else _dot(p_ref[...], w_ref[off:off + width, :])
            dh = part if dh is None else dh + part
            off += width
        xv = x_ref[...]
        r = _rstd(xv)
        xh = xv * r
        gd = dh * g_ref[...]
        dx = dres_ref[...] + r * (gd - xh * jnp.mean(gd * xh, axis=-1, keepdims=True))
        dx_ref[...] = dx
        dxb_ref[...] = dx.astype(BF16)
        part = jnp.sum(dh * xh, axis=0, keepdims=True)

        @pl.when(i == 0)
        def _():
            dg_ref[...] = part

        @pl.when(i > 0)
        def _():
            dg_ref[...] += part

    row = pl.BlockSpec((tm, d), lambda i: (i, 0))
    vec = pl.BlockSpec((1, d), lambda i: (0, 0))
    w_spec = pl.BlockSpec((None,) + w.shape[1:], lambda i: (layer, 0, 0))
    return _pcall(body, name=name, out_shape=(_sds((s, d), F32), _sds((s, d), BF16), _sds((1, d), F32)), grid=(s // tm,),
                  in_specs=[pl.BlockSpec((tm, width), lambda i: (i, 0)) for width in widths] + [w_spec, row, vec, row],
                  out_specs=(row, row, vec), comm=comm)(*pieces, w, x, g, dres)


def _yb_norm_bwd(yb, dy, gb, name, comm=None):
    s, wb = yb.shape
    w = wb // 2
    heads = wb // D_HEAD
    tm = ROW_CHUNK

    def body(yb_ref, d1_ref, d2_ref, g_ref, dyb_ref, dl_ref, dg_ref):
        i = pl.program_id(0)
        y = yb_ref[...]
        dyn = jnp.concatenate([d1_ref[...], d2_ref[...]], axis=1)
        r = _rstd(y)
        yh = y * r
        gd = dyn * g_ref[...]
        dyb = r * (gd - yh * jnp.mean(gd * yh, axis=-1, keepdims=True))
        dyb_ref[...] = dyb
        part = jnp.sum(dyn * yh, axis=0, keepdims=True)
        prod = dyb * y
        even = lax.broadcasted_iota(I32, (tm, LANES), 1) < D_HEAD
        for p in range(heads // 2):
            blk = prod[:, p * LANES:(p + 1) * LANES]
            ev = jnp.sum(jnp.where(even, blk, 0.0), axis=1, keepdims=True)
            od = jnp.sum(jnp.where(even, 0.0, blk), axis=1, keepdims=True)
            dl_ref[2 * p] = jnp.broadcast_to(ev, (tm, LANES))
            dl_ref[2 * p + 1] = jnp.broadcast_to(od, (tm, LANES))

        @pl.when(i == 0)
        def _():
            dg_ref[...] = part

        @pl.when(i > 0)
        def _():
            dg_ref[...] += part

    return _pcall(
        body, name=name, out_shape=(_sds((s, wb), F32), _sds((heads, s, LANES), F32), _sds((1, wb), F32)),
        grid=(s // tm,),
        in_specs=[pl.BlockSpec((tm, wb), lambda i: (i, 0)), pl.BlockSpec((tm, w), lambda i: (i, 1)),
                  pl.BlockSpec((tm, w), lambda i: (i, 2)), pl.BlockSpec((1, wb), lambda i: (0, 0))],
        out_specs=(pl.BlockSpec((tm, wb), lambda i: (i, 0)), pl.BlockSpec((heads, tm, LANES), lambda i: (0, i, 0)),
                   pl.BlockSpec((1, wb), lambda i: (0, 0))), comm=comm,
    )(yb, dy, dy, gb)


def _t5_bucket_table():
    max_exact = NUM_BUCKETS // 2
    out = np.full((len(DILATED_BRANCHES), BLK, 2 * BLK), -1, np.int32)
    rel = np.arange(BLK)[:, None] - np.arange(2 * BLK)[None, :] + BLK
    for b, (window, dilation) in enumerate(DILATED_BRANCHES):
        n_keys = window // dilation
        dist = np.maximum(rel, 0) * dilation
        d_f = np.maximum(dist, 1).astype(np.float32)
        large = max_exact + (np.log(d_f / np.float32(max_exact)) / np.float32(math.log(MAX_DISTANCE / max_exact))
                             * np.float32(NUM_BUCKETS - max_exact)).astype(np.int32)
        large = np.minimum(large, NUM_BUCKETS - 1)
        bucket = np.where(dist < max_exact, dist, large)
        out[b] = np.where((rel >= 0) & (rel <= n_keys), bucket, -1)
    return out


def _bias_tiles(rel_bias, buckets, name):
    nbk, heads = rel_bias.shape
    nbr = buckets.shape[0]

    def body(rb_ref, bk_ref, o_ref):
        for br in range(nbr):
            bk = bk_ref[br]
            tiles = [jnp.full((BLK, 2 * BLK), NEG, F32) for _ in range(heads)]
            for b in range(nbk):
                hit = bk == b
                tiles = [jnp.where(hit, rb_ref[b, h], tiles[h]) for h in range(heads)]
            for h in range(heads):
                o_ref[br, h] = tiles[h]

    return _pcall(body, name=name, out_shape=_sds((nbr, heads, BLK, 2 * BLK), F32), grid=(1,),
                  in_specs=[pl.BlockSpec(memory_space=pltpu.SMEM), pl.BlockSpec(buckets.shape, lambda i: (0, 0, 0))],
                  out_specs=pl.BlockSpec((nbr, heads, BLK, 2 * BLK), lambda i: (0, 0, 0, 0)))(rel_bias, buckets)


def _bias_grad(dtiles, buckets, nbk, name):
    nbr, heads = dtiles.shape[:2]

    def body(dt_ref, bk_ref, o_ref):
        row = lax.broadcasted_iota(I32, (nbk, LANES), 0)
        col = lax.broadcasted_iota(I32, (nbk, LANES), 1)
        out = jnp.zeros((nbk, LANES), F32)
        for h in range(heads):
            for b in range(nbk):
                tot = jnp.zeros((), F32)
                for br in range(nbr):
                    tot = tot + jnp.sum(jnp.where(bk_ref[br] == b, dt_ref[br, h], 0.0))
                out = jnp.where((row == b) & (col == h), tot, out)
        o_ref[...] = out

    return _pcall(body, name=name, out_shape=_sds((nbk, LANES), F32), grid=(1,),
                  in_specs=[pl.BlockSpec(dtiles.shape, lambda i: (0, 0, 0, 0)), pl.BlockSpec(buckets.shape, lambda i: (0, 0, 0))],
                  out_specs=pl.BlockSpec((nbk, LANES), lambda i: (0, 0)))(dtiles, buckets)


def _largest_divisor(n, cap):
    return max(g for g in range(1, cap + 1) if n % g == 0)


def _attn_blocks(s, visit, group):
    for br, (window, d) in enumerate(DILATED_BRANCHES):
        n_blk = (s // d) // BLK
        span = BLK * d
        g1 = _largest_divisor(d, group)

        def firsts(t, carry, br=br, d=d, g1=g1):
            for j in range(g1):
                visit(br, d, t * g1 + j, False)
            return carry

        lax.fori_loop(0, d // g1, firsts, 0)
        if n_blk > 1:
            total = d * (n_blk - 1)
            g2 = _largest_divisor(total, group)

            def rest(t, carry, br=br, d=d, n_blk=n_blk, span=span, g2=g2):
                for j in range(g2):
                    idx = t * g2 + j
                    visit(br, d, idx // (n_blk - 1) + (1 + idx % (n_blk - 1)) * span, True)
                return carry

            lax.fori_loop(0, total // g2, rest, 0)


def _rows(start, size, d):
    return pl.ds(pl.multiple_of(start, BLK), size) if d == 1 else pl.ds(start, size, stride=d)


def _attn_fwd(z, btiles, col0, name, comm=None):
    s = z.shape[0]
    nbr, heads = btiles.shape[:2]
    pairs = heads // 2
    scale = D_HEAD ** -0.5
    rc = ROW_CHUNK

    def body(q_ref, k_ref, v_ref, bt_ref, yb_ref, lse_ref, acc_ref, m_ref, l_ref):
        even = lax.broadcasted_iota(I32, (BLK, LANES), 1) < D_HEAD
        even2 = lax.broadcasted_iota(I32, (2 * BLK, LANES), 1) < D_HEAD

        def visit(br, d, start, prev):
            kw = 2 * BLK if prev else BLK
            rows_q = _rows(start, BLK, d)
            rows_k = _rows(start - BLK * d, kw, d) if prev else rows_q
            qb = q_ref[rows_q, :]
            kb = k_ref[rows_k, :].astype(BF16)
            vw = v_ref[rows_k, :]
            ev_k = even2 if prev else even
            qm = jnp.concatenate([jnp.where(even, qb, 0.0), jnp.where(even, 0.0, qb)], axis=0).astype(BF16)
            bias = [bt_ref[br, e] if prev else bt_ref[br, e, :, BLK:] for e in range(2)]
            sc = _dot_nt(qm, kb) * scale + jnp.concatenate(bias, axis=0)
            m = jnp.max(sc, axis=1, keepdims=True)
            p = jnp.exp(sc - m)
            l = jnp.sum(p, axis=1, keepdims=True)
            pb = p.astype(BF16)
            vm = jnp.concatenate([jnp.where(ev_k, vw, 0.0), jnp.where(ev_k, 0.0, vw)], axis=0).astype(BF16)
            acc_ref.at[br][rows_q, :] = _dot(jnp.concatenate([pb[:BLK], pb[BLK:]], axis=1), vm)
            for e in range(2):
                m_ref.at[br, e][rows_q, :] = jnp.broadcast_to(m[e * BLK:(e + 1) * BLK], (BLK, LANES))
                l_ref.at[br, e][rows_q, :] = jnp.broadcast_to(l[e * BLK:(e + 1) * BLK], (BLK, LANES))

        _attn_blocks(s, visit, ATTN_GROUP_FWD)

        ev_c = lax.broadcasted_iota(I32, (rc, LANES), 1) < D_HEAD

        def merge(i, carry):
            rows = pl.ds(pl.multiple_of(i * rc, rc), rc)
            wts, dens = [], []
            for e in range(2):
                ms = [m_ref[br, e, rows, :] for br in range(nbr)]
                top = functools.reduce(jnp.maximum, ms)
                w = [jnp.exp(mb - top) for mb in ms]
                den = functools.reduce(lambda a, b: a + b, [w[br] * l_ref[br, e, rows, :] for br in range(nbr)])
                lse_ref[e, rows, :] = top + jnp.log(den)
                wts.append(w)
                dens.append(den)
            num = functools.reduce(lambda a, b: a + b,
                                   [jnp.where(ev_c, wts[0][br], wts[1][br]) * acc_ref[br, rows, :] for br in range(nbr)])
            yb_ref[rows, :] = num / jnp.where(ev_c, dens[0], dens[1])
            return carry

        lax.fori_loop(0, s // rc, merge, 0)

    def zcol(j):
        return pl.BlockSpec((s, LANES), lambda p, j=j: (0, col0 + j + p))

    return _pcall(
        body, name=name, out_shape=(_sds((s, pairs * LANES), F32), _sds((heads, s, LANES), F32)), grid=(pairs,),
        in_specs=[zcol(0), zcol(pairs), zcol(2 * pairs), pl.BlockSpec((nbr, 2, BLK, 2 * BLK), lambda p: (0, p, 0, 0))],
        out_specs=(pl.BlockSpec((s, LANES), lambda p: (0, p)), pl.BlockSpec((2, s, LANES), lambda p: (p, 0, 0))),
        scratch_shapes=[pltpu.VMEM((nbr, s, LANES), F32), pltpu.VMEM((nbr, 2, s, LANES), F32), pltpu.VMEM((nbr, 2, s, LANES), F32)],
        comm=comm,
    )(z, z, z, btiles)


def _attn_bwd(z, btiles, dyb, lse, delta, dbias_in, col0, name, comm=None):
    s = z.shape[0]
    nbr, heads = btiles.shape[:2]
    pairs = heads // 2
    scale = D_HEAD ** -0.5

    def body(q_ref, k_ref, v_ref, bt_ref, dy_ref, lse_ref, dl_ref, dbi_ref,
             dq_ref, dk_ref, dv_ref, db_ref, dqa, dka, dva):
        even = lax.broadcasted_iota(I32, (BLK, LANES), 1) < D_HEAD
        even2 = lax.broadcasted_iota(I32, (2 * BLK, LANES), 1) < D_HEAD
        for ref in (dqa, dka, dva):
            ref[...] = jnp.zeros((s, LANES), F32)
        db_ref[...] = dbi_ref[...]

        def visit(br, d, start, prev):
            kw = 2 * BLK if prev else BLK
            rows_q = _rows(start, BLK, d)
            rows_k = _rows(start - BLK * d, kw, d) if prev else rows_q
            qb = q_ref[rows_q, :]
            dyv = dy_ref[rows_q, :]
            kwin = k_ref[rows_k, :]
            kb = kwin.astype(BF16)
            vb = v_ref[rows_k, :].astype(BF16)
            ev_k = even2 if prev else even
            qm = jnp.concatenate([jnp.where(even, qb, 0.0), jnp.where(even, 0.0, qb)], axis=0).astype(BF16)
            dym = jnp.concatenate([jnp.where(even, dyv, 0.0), jnp.where(even, 0.0, dyv)], axis=0).astype(BF16)
            bias = [bt_ref[br, e] if prev else bt_ref[br, e, :, BLK:] for e in range(2)]
            sc = _dot_nt(qm, kb) * scale + jnp.concatenate(bias, axis=0)
            lt = jnp.concatenate([lse_ref.at[e][rows_q, :] for e in range(2)], axis=0)
            dt = jnp.concatenate([dl_ref.at[e][rows_q, :] for e in range(2)], axis=0)
            if prev:
                lt = jnp.concatenate([lt, lt], axis=1)
                dt = jnp.concatenate([dt, dt], axis=1)
            p = jnp.exp(sc - lt)
            ds = p * (_dot_nt(dym, vb) - dt)
            for e in range(2):
                if prev:
                    db_ref[br, e] += ds[e * BLK:(e + 1) * BLK]
                else:
                    db_ref[br, e, :, BLK:] += ds[e * BLK:(e + 1) * BLK]
            dsb = ds.astype(BF16)
            km = jnp.concatenate([jnp.where(ev_k, kwin, 0.0), jnp.where(ev_k, 0.0, kwin)], axis=0).astype(BF16)
            dqa[rows_q, :] += _dot(jnp.concatenate([dsb[:BLK], dsb[BLK:]], axis=1), km) * scale
            dka[rows_k, :] += _dot_tn(dsb, qm) * scale
            dva[rows_k, :] += _dot_tn(p.astype(BF16), dym)

        _attn_blocks(s, visit, ATTN_GROUP_BWD)
        dq_ref[...] = dqa[...].astype(BF16)
        dk_ref[...] = dka[...].astype(BF16)
        dv_ref[...] = dva[...].astype(BF16)

    def zcol(j):
        return pl.BlockSpec((s, LANES), lambda p, j=j: (0, col0 + j + p))

    col = pl.BlockSpec((s, LANES), lambda p: (0, p))
    stat = pl.BlockSpec((2, s, LANES), lambda p: (p, 0, 0))
    tile = pl.BlockSpec((nbr, 2, BLK, 2 * BLK), lambda p: (0, p, 0, 0))
    wide = _sds((s, pairs * LANES), BF16)
    return _pcall(
        body, name=name, out_shape=(wide, wide, wide, _sds(btiles.shape, F32)), grid=(pairs,),
        in_specs=[zcol(0), zcol(pairs), zcol(2 * pairs), tile, col, stat, stat, tile],
        out_specs=(col, col, col, tile),
        scratch_shapes=[pltpu.VMEM((s, LANES), F32) for _ in range(3)], comm=comm,
    )(z, z, z, btiles, dyb, lse, delta, dbias_in)


def _row(v):
    return v.reshape(1, -1)


class _Rides:
    def __init__(self):
        self.table, self.grads = {}, {}

    def add(self, name, build):
        self.table.setdefault(name, []).append(build)

    def get(self, name):
        comm = None
        for build in self.table.get(name, ()):
            comm = build() if comm is None else _both(comm, build())
        return comm

    def ready(self, key, g):
        self.grads[key] = g


class _LocalSchedule:
    def __init__(self):
        self.big = {}

    def fwd_comms(self, l):
        return _Rides()

    def bwd_comms(self, l):
        return _Rides()

    def after_bwd(self, l, grads):
        self.big[l] = grads


def _layer_fwd(l, x, h, wts, prm, btiles, comms):
    d = x.shape[1]
    wq = d // 4
    depth = prm["norm_mix_g"].shape[0]
    gout = prm["out_norm_g"][l]
    z = _mm_n(h, wts["in_t"], l, nt=True, tn=256, out_dtype=F32, name="in_proj", comm=comms.get("in_proj"))
    yan, ycn = _mix_fwd(z, prm["conv_a_w"][l], prm["conv_c_w"][l], _row(prm["conv_c_b"][l]), _row(prm["ln_c_g"][l]),
                        _row(prm["ln_c_b"][l]), _row(gout[:wq]), _row(gout[3 * wq:]), "mix_fwd")
    yb, lse = _attn_fwd(z, btiles, 3 * wq // LANES, "attn_fwd", comm=comms.get("attn_fwd"))
    y, x_mid, h2 = _out_proj(yan, yb, ycn, _row(gout[wq:3 * wq]), x, wts["out"], l, _row(prm["norm_ffn_g"][l]),
                             "out_proj", comm=comms.get("out_proj"))
    up = _mm_n(h2, wts["up"], l, nt=False, tn=512, out_dtype=BF16, name="up_proj", comm=comms.get("up_proj"))
    act = _ffn_act_fwd(up, prm["conv_f_w"][l], "ffn_act_fwd", comm=comms.get("ffn_act_fwd"))
    g_next = _row(prm["norm_mix_g"][l + 1]) if l + 1 < depth else None
    x_out, h_next = _down_proj(act, wts["down"], l, x_mid, g_next, "down_proj", comm=comms.get("down_proj"))
    return x_out, h_next, (x, h, z, yb, lse, y, x_mid, h2, up)


def _layer_bwd(l, dxo, dxo_b, saved, wts, prm, btiles, dbias, comms):
    x, h, z, yb, lse, y, x_mid, h2, up = saved
    d = x.shape[1]
    wq = d // 4
    gout = prm["out_norm_g"][l]
    dact = _mm_n(dxo_b, wts["down"], l, nt=True, tn=256, out_dtype=BF16, name="down_proj_dx", comm=comms.get("down_proj_dx"))
    act, dgate, dval, dwf = _ffn_act_bwd(up, dact, prm["conv_f_w"][l], "ffn_act_bwd", comm=comms.get("ffn_act_bwd"))
    g_down = _mm_tn(act, dxo_b, t=256, name="down_proj_dw")
    comms.ready("down", g_down)
    dxm, dxm_b, dg_ffn = _proj_dx([dgate, dval], wts["up"], l, True, x_mid, _row(prm["norm_ffn_g"][l]), dxo, "up_proj_dx",
                                  ROW_CHUNK, comm=comms.get("up_proj_dx"))
    g_up = _mm_tn2(h2, dgate, dval, t=256, name="up_proj_dw", comm=comms.get("up_proj_dw"))
    comms.ready("up", g_up)
    dy = _mm_n(dxm_b, wts["out"], l, nt=True, tn=256, out_dtype=F32, name="out_proj_dx")
    g_out = _mm_tn(y, dxm_b, t=256, name="out_proj_dw")
    comms.ready("out", g_out)
    dza, dzc, dwa, dwc, dcb, dlg, dlb, dga, dgc = _mix_bwd(
        z, dy, prm["conv_a_w"][l], prm["conv_c_w"][l], _row(prm["conv_c_b"][l]), _row(prm["ln_c_g"][l]),
        _row(prm["ln_c_b"][l]), _row(gout[:wq]), _row(gout[3 * wq:]), "mix_bwd", comm=comms.get("mix_bwd"))
    dyb, delta, dgb = _yb_norm_bwd(yb, dy, _row(gout[wq:3 * wq]), "yb_norm_bwd", comm=comms.get("yb_norm_bwd"))
    dq, dk, dv, dbias = _attn_bwd(z, btiles, dyb, lse, delta, dbias, 3 * wq // LANES, "attn_bwd",
                                  comm=comms.get("attn_bwd"))
    dz = [dza, dq, dk, dv, dzc]
    dx, dx_b, dg_mix = _proj_dx(dz, wts["in_t"], l, False, x, _row(prm["norm_mix_g"][l]), dxm, "in_proj_dx",
                                PROJ_ROWS, comm=comms.get("in_proj_dx"))
    g_in_t = _mm_tn_pieces(dz, h, t=256, name="in_proj_dw")
    big = {"in_t": g_in_t, "out": g_out, "up": g_up, "down": g_down}
    small = {"norm_mix_g": dg_mix[0], "conv_a_w": dwa, "conv_c_w": dwc, "conv_c_b": dcb[0], "ln_c_g": dlg[0],
             "ln_c_b": dlb[0], "out_norm_g": jnp.concatenate([dga[0], dgb[0], dgc[0]]), "norm_ffn_g": dg_ffn[0],
             "conv_f_w": dwf}
    return dx, dx_b, big, small, dbias


def _local_step(x, tgt, wts, prm, sched):
    depth = prm["norm_mix_g"].shape[0]
    buckets = jnp.asarray(_t5_bucket_table())
    btiles = _bias_tiles(prm["rel_bias"], buckets, "bias_tiles")
    saved = []
    h = _rms_fwd(x, _row(prm["norm_mix_g"][0]), "rms_mix_fwd")
    for l in range(depth):
        x, h, sv = _layer_fwd(l, x, h, wts, prm, btiles, sched.fwd_comms(l))
        saved.append(sv)
    loss, dx, dx_b, dg_final = _final_loss(x, _row(prm["final_g"]), tgt, "final_loss")
    dbias = jnp.zeros(btiles.shape, F32)
    small = [None] * depth
    for l in reversed(range(depth)):
        dx, dx_b, grads, small[l], dbias = _layer_bwd(l, dx, dx_b, saved[l], wts, prm, btiles, dbias, sched.bwd_comms(l))
        sched.after_bwd(l, grads)
    nbk, heads = prm["rel_bias"].shape
    d_rel = _bias_grad(dbias, buckets, nbk, "bias_grad")[:, :heads]
    return loss, dx, small, d_rel, dg_final[0]


BIG = ("in_t", "out", "up", "down")
COL_SHARDED = ("up",)
N_CHIPS = 4
N_DEV = 8
BF16_ROWS = 16


def _me():
    return lax.axis_index("x"), lax.axis_index("y"), lax.axis_index("c")


def _chip_of(x, y):
    return 2 * x + y


def _other_chips(x, y):
    return ((1 - x, y), (x, 1 - y), (1 - x, 1 - y))


def _remote(src, dst, send_sem, recv_sem, device):
    return pltpu.make_async_remote_copy(src_ref=src, dst_ref=dst, send_sem=send_sem, recv_sem=recv_sem,
                                        device_id=device, device_id_type=MESH)


ALL_FLIPS = (0, 1, 2)


def _ag_comm(wts, layer, ici_keys, fwd_keys):
    flips = {(k if isinstance(k, str) else k[0]): (ALL_FLIPS if isinstance(k, str) else k[1]) for k in ici_keys}
    keys = tuple(k for k in BIG if k in flips or k in fwd_keys)

    def geo(k):
        _, rows, cols = wts[k].shape
        return (rows, cols // N_CHIPS) if k in COL_SHARDED else (rows // N_CHIPS, cols)

    def copies(refs, sems):
        g = dict(zip(keys, refs))
        isend, irecv, dsend, drecv = sems
        x, y, c = _me()
        mine = _chip_of(x, y)

        def region(k, chip, half):
            r, cc = geo(k)
            h = r // 2
            if k in COL_SHARDED:
                return g[k].at[layer, pl.ds(pl.multiple_of(half * h, BF16_ROWS), h), pl.ds(pl.multiple_of(chip * cc, LANES), cc)]
            return g[k].at[layer, pl.ds(pl.multiple_of(chip * r + half * h, BF16_ROWS), h), :]

        def ici(k, f, landing):
            chip = _other_chips(x, y)[f]
            where = region(k, _chip_of(*chip) if landing else mine, c)
            i = keys.index(k)
            return _remote(where, where, isend.at[i, f], irecv.at[i, f], (*chip, c))

        def fwd(k, f, landing):
            chip = _other_chips(x, y)[f]
            where = region(k, _chip_of(*chip), 1 - c if landing else c)
            i = keys.index(k)
            return _remote(where, where, dsend.at[i, f], drecv.at[i, f], (x, y, 1 - c))

        return ici, fwd

    def start(ins, outs, sems):
        ici, fwd = copies(outs, sems)
        for k in keys:
            for f in flips.get(k, ALL_FLIPS):
                if k in flips:
                    ici(k, f, False).start()
                else:
                    fwd(k, f, False).start()

    def finish(ins, outs, sems):
        ici, fwd = copies(outs, sems)
        for k in keys:
            for f in flips.get(k, ()):
                ici(k, f, True).wait_recv()
                if k in fwd_keys:
                    fwd(k, f, False).start()
        for k in keys:
            for f in flips.get(k, ALL_FLIPS):
                if k in fwd_keys:
                    fwd(k, f, True).wait_recv()
                    fwd(k, f, False).wait_send()
                if k in flips:
                    ici(k, f, False).wait_send()

    def done(res):
        wts.update(zip(keys, res))

    n = len(keys)
    return _Comm([wts[k] for k in keys], [_sds(wts[k].shape, BF16) for k in keys], {i: i for i in range(n)},
                 [pltpu.SemaphoreType.DMA((n, 3)) for _ in range(4)], start, finish, done)


def _small_gather_comm(slab, store):
    def copies(ins, outs, sems):
        send, recv, lsem = sems
        x, y, c = _me()
        mine = _chip_of(x, y)
        own = pltpu.make_async_copy(ins[0], outs[0].at[mine], lsem)
        pairs = []
        for f, chip in enumerate(_other_chips(x, y)):
            out = _remote(ins[0], outs[0].at[mine], send.at[f], recv.at[f], (*chip, c))
            land = _remote(ins[0], outs[0].at[_chip_of(*chip)], send.at[f], recv.at[f], (*chip, c))
            pairs.append((out, land))
        return own, pairs

    def start(ins, outs, sems):
        own, pairs = copies(ins, outs, sems)
        own.start()
        for out, _ in pairs:
            out.start()

    def finish(ins, outs, sems):
        own, pairs = copies(ins, outs, sems)
        for out, land in pairs:
            land.wait_recv()
            out.wait_send()
        own.wait()

    def done(res):
        store["small"] = res[0]

    return _Comm([slab], [_sds((N_CHIPS,) + slab.shape, F32)], {},
                 [pltpu.SemaphoreType.DMA((3,)), pltpu.SemaphoreType.DMA((3,)), pltpu.SemaphoreType.DMA], start, finish, done)


def _piece_geo(g):
    geo = {}
    for k in g:
        rows, cols = g[k].shape
        geo[k] = (rows // 2, cols // N_CHIPS) if k in COL_SHARDED else (rows // (2 * N_CHIPS), cols)
    return geo


def _swap_comm(g, keys, done):
    geo = _piece_geo(g)
    n_copies = sum(N_CHIPS if k in COL_SHARDED else 1 for k in keys)

    def copies(ins, outs, sems):
        g_refs, t_refs = dict(zip(keys, ins)), dict(zip(keys, outs))
        send, recv = sems
        x, y, c = _me()
        pairs = []
        for k in keys:
            h, cc = geo[k]
            if k in COL_SHARDED:
                rows = pl.ds(pl.multiple_of((1 - c) * h, BF16_ROWS), h)
                pairs += [(g_refs[k].at[rows, pl.ds(j * cc, cc)], t_refs[k].at[j]) for j in range(N_CHIPS)]
            else:
                pairs.append((g_refs[k].at[:, 1 - c], t_refs[k]))
        return [_remote(src, dst, send.at[i], recv.at[i], (x, y, 1 - c)) for i, (src, dst) in enumerate(pairs)]

    def start(ins, outs, sems):
        for cp in copies(ins, outs, sems):
            cp.start()

    def finish(ins, outs, sems):
        for cp in copies(ins, outs, sems):
            cp.wait()

    ins = [g[k] if k in COL_SHARDED else g[k].reshape(N_CHIPS, 2, geo[k][0], geo[k][1]) for k in keys]
    return _Comm(ins, [_sds((N_CHIPS,) + geo[k], BF16) for k in keys], {},
                 [pltpu.SemaphoreType.DMA((n_copies,)) for _ in range(2)], start, finish,
                 lambda res: done(dict(zip(keys, res))))


def _pair_sum(g, theirs, c_arr, keys):
    geo = _piece_geo(g)

    def body(c_ref, *refs):
        nk = len(keys)
        for i in range(nk):
            refs[2 * nk + i][...] = (refs[i][...].astype(F32) + refs[nk + i][...].astype(F32)).astype(BF16)

    in_specs, ins = [], []
    for k in keys:
        h, cc = geo[k]
        if k in COL_SHARDED:
            in_specs.append(pl.BlockSpec((h, cc), lambda j, c_ref: (c_ref[0], j)))
            ins.append(g[k])
        else:
            in_specs.append(pl.BlockSpec((None, h, cc), lambda j, c_ref: (2 * j + c_ref[0], 0, 0)))
            ins.append(g[k].reshape(2 * N_CHIPS, h, cc))
    slab = [pl.BlockSpec((None,) + geo[k], lambda j, c_ref: (j, 0, 0)) for k in keys]
    res = _pcall(body, name="rs_pair_sum", out_shape=tuple(_sds((N_CHIPS,) + geo[k], BF16) for k in keys), grid=(N_CHIPS,),
                 in_specs=in_specs + slab, out_specs=tuple(slab), prefetch=1)(c_arr, *ins, *[theirs[k] for k in keys])
    return dict(zip(keys, res))


def _rs_comm(p, keys, store):
    def copies(ins, outs, sems):
        send, recv = sems
        x, y, c = _me()
        return [_remote(ins[i].at[_chip_of(*chip)], outs[i].at[f], send.at[i, f], recv.at[i, f], (*chip, c))
                for i in range(len(keys)) for f, chip in enumerate(_other_chips(x, y))]

    def start(ins, outs, sems):
        for cp in copies(ins, outs, sems):
            cp.start()

    def finish(ins, outs, sems):
        for cp in copies(ins, outs, sems):
            cp.wait()

    def done(res):
        store.update(zip(keys, res))

    return _Comm([p[k] for k in keys], [_sds((3,) + p[k].shape[1:], BF16) for k in keys], {},
                 [pltpu.SemaphoreType.DMA((len(keys), 3)) for _ in range(2)], start, finish, done)


def _quad_sum(p, b, where, l, full):
    parts = 2
    nk = len(BIG)

    def body(where_ref, *refs):
        for i in range(nk):
            acc = refs[i][...].astype(F32)
            for f in range(3):
                acc = acc + refs[nk + 3 * i + f][...].astype(F32)
            refs[5 * nk + i][...] = acc

    own, recv, outs = [], [], []
    for k in BIG:
        h, cc = p[k].shape[1:]
        th = h // parts
        own.append(pl.BlockSpec((None, th, cc), lambda i, w_ref: (w_ref[0], i, 0)))
        recv += [pl.BlockSpec((None, th, cc), lambda i, w_ref, f=f: (f, i, 0)) for f in range(3)]
        outs.append(pl.BlockSpec((None, None, th, cc), lambda i, w_ref: (l, w_ref[1], i, 0)))
    args = [p[k] for k in BIG] + [b[k] for k in BIG for _ in range(3)] + [full[k] for k in BIG]
    res = _pcall(body, name="rs_quad_sum", out_shape=tuple(_sds(full[k].shape, F32) for k in BIG), grid=(parts,),
                 in_specs=own + recv + [ANY] * nk, out_specs=tuple(outs), prefetch=1,
                 aliases={1 + 4 * nk + i: i for i in range(nk)})(where, *args)
    return dict(zip(BIG, res))


def _share_comm(layers, full, done):
    nk = len(BIG)

    def copies(outs, sems, landing):
        send, recv = sems
        x, y, c = _me()
        half = 1 - c if landing else c
        return [_remote(outs[i].at[l, half], outs[i].at[l, half], send.at[i, j], recv.at[i, j], (x, y, 1 - c))
                for i in range(nk) for j, l in enumerate(layers)]

    def start(ins, outs, sems):
        for cp in copies(outs, sems, False):
            cp.start()

    def finish(ins, outs, sems):
        for cp in copies(outs, sems, True):
            cp.wait_recv()
        for cp in copies(outs, sems, False):
            cp.wait_send()

    return _Comm([full[k] for k in BIG], [_sds(full[k].shape, F32) for k in BIG], {i: i for i in range(nk)},
                 [pltpu.SemaphoreType.DMA((nk, len(layers))) for _ in range(2)], start, finish,
                 lambda res: done(dict(zip(BIG, res))))


def _gather_comm(slab, done):
    def copies(ins, outs, sems, landing):
        send, recv = sems
        x, y, c = _me()
        me = 4 * x + 2 * y + c
        out = []
        for mask in range(1, N_DEV):
            peer = (x ^ (mask >> 2), y ^ ((mask >> 1) & 1), c ^ (mask & 1))
            slot = 4 * peer[0] + 2 * peer[1] + peer[2] if landing else me
            out.append(_remote(ins[0], outs[0].at[slot], send.at[mask - 1], recv.at[mask - 1], peer))
        return out

    def start(ins, outs, sems):
        for cp in copies(ins, outs, sems, False):
            cp.start()

    def finish(ins, outs, sems):
        for cp in copies(ins, outs, sems, True):
            cp.wait_recv()
        for cp in copies(ins, outs, sems, False):
            cp.wait_send()

    return _Comm([slab], [_sds((N_DEV,) + slab.shape, F32)], {},
                 [pltpu.SemaphoreType.DMA((N_DEV - 1,)), pltpu.SemaphoreType.DMA((N_DEV - 1,))], start, finish,
                 lambda res: done(res[0]))


def _sum_slabs(slabs, own, me):
    n, r, lanes = slabs.shape
    tr = r // 2

    def body(me_ref, s_ref, own_ref, o_ref):
        o_ref[...] = jnp.zeros((tr, lanes), F32)
        for i in range(n):
            @pl.when(me_ref[0] == i)
            def _():
                o_ref[...] += own_ref[...]

            @pl.when(me_ref[0] != i)
            def _():
                o_ref[...] += s_ref[i]

    return _pcall(body, name="sum_partials", out_shape=_sds((r, lanes), F32), grid=(2,),
                  in_specs=[pl.BlockSpec((n, tr, lanes), lambda i, me_ref: (0, i, 0)),
                            pl.BlockSpec((tr, lanes), lambda i, me_ref: (i, 0))],
                  out_specs=pl.BlockSpec((tr, lanes), lambda i, me_ref: (i, 0)), prefetch=1)(me, slabs, own)


def _cast_into_gathered(w, chip, by_cols, name):
    l, r, c = w.shape

    def body(chip_ref, w_ref, o_ref):
        o_ref[...] = w_ref[...].astype(BF16)

    if by_cols:
        shape, out = (l, r, N_CHIPS * c), pl.BlockSpec((None, r, c), lambda i, chip_ref: (i, 0, chip_ref[0]))
    else:
        shape, out = (l, N_CHIPS * r, c), pl.BlockSpec((None, r, c), lambda i, chip_ref: (i, chip_ref[0], 0))
    return _pcall(body, name=name, out_shape=_sds(shape, BF16), grid=(l,),
                  in_specs=[pl.BlockSpec((None, r, c), lambda i, chip_ref: (i, 0, 0))], out_specs=out, prefetch=1)(chip, w)


def _adamw_math(w, g, m, v):
    mn = ADAM_B1 * m + (1.0 - ADAM_B1) * g
    vn = ADAM_B2 * v + (1.0 - ADAM_B2) * (g * g)
    m_hat = mn / (1.0 - ADAM_B1 ** ADAM_STEP)
    v_hat = vn / (1.0 - ADAM_B2 ** ADAM_STEP)
    return -ADAM_LR * (m_hat / (jnp.sqrt(v_hat) + ADAM_EPS) + ADAM_WD * w), mn, vn


def _adamw(w, g, m, v, name, tr):
    r, c = w.shape

    def body(w_ref, g_ref, m_ref, v_ref, go_ref, d_ref, mo_ref, vo_ref):
        gv = g_ref[...]
        go_ref[...] = gv
        d_ref[...], mo_ref[...], vo_ref[...] = _adamw_math(w_ref[...], gv, m_ref[...], v_ref[...])

    blk = pl.BlockSpec((tr, c), lambda i: (i, 0))
    return _pcall(body, name=name, out_shape=tuple(_sds((r, c), F32) for _ in range(4)), grid=(r // tr,),
                  in_specs=[blk] * 4, out_specs=(blk, blk, blk, blk))(w, g, m, v)


def _adamw_small(groups, name):
    count = len(groups[0])
    shapes = [a.shape for a in groups[0]]
    as2d = [(math.prod(s[:-1]), s[-1]) for s in shapes]

    def body(*refs):
        for i in range(count):
            out = _adamw_math(*[refs[j * count + i][...] for j in range(4)])
            for j in range(3):
                refs[(4 + j) * count + i][...] = out[j]

    specs = [pl.BlockSpec(s, lambda i: (0, 0)) for s in as2d]
    res = _pcall(body, name=name, out_shape=tuple(_sds(s, F32) for _ in range(3) for s in as2d), grid=(1,),
                 in_specs=specs * 4, out_specs=tuple(specs * 3))(*[a.reshape(s) for grp in groups for a, s in zip(grp, as2d)])
    return [[res[j * count + i].reshape(shapes[i]) for i in range(count)] for j in range(3)]


AG_RIDES = {"in_proj": (0, ("out",), ("down",)), "attn_fwd": (0, (("up", (0, 1)),), ("out",)), "out_proj": (0, (), ("up",)),
            "up_proj": (1, ("in_t",), ()), "ffn_act_fwd": (1, ("down",), ()),
            "down_proj": (1, (("up", (2,)),), ("in_t",))}
AG_FIRST = ("in_t", "down")
RS_RIDES = {"down_proj_dx": "swap", "ffn_act_bwd": ("up",), "up_proj_dx": "share", "attn_bwd": ("in_t", "out", "down")}
RS_RIDES_LAST = {"down_proj_dx": "swap", "ffn_act_bwd": ("up",), "up_proj_dx": "share", "up_proj_dw": ("in_t",),
                 "mix_bwd": ("down",), "yb_norm_bwd": ("out",)}
EARLY = ("out", "up", "down")
EARLY_RIDES = {"mix_bwd": "swap", "attn_bwd": ("up", "down"), "in_proj_dx": ("out",)}


class _Reduction:
    def __init__(self, grads):
        self.grads, self.pairs, self.recv = grads, {}, {}


class _MeshSchedule:
    def __init__(self, wts, depth, c_arr, where):
        self.wts, self.depth, self.c_arr, self.where = wts, depth, c_arr, where
        self.pending, self.last, self.full, self.unshared = None, None, None, []

    def fwd_comms(self, l):
        rides = _Rides()
        for name, (off, ici, fwd) in AG_RIDES.items():
            if l + off == 0:
                ici = tuple(k if isinstance(k, str) else k[0] for k in ici)
                ici, fwd = (tuple(k for k in keys if k not in AG_FIRST) for keys in (ici, fwd))
            if l + off < self.depth and (ici or fwd):
                rides.add(name, lambda ride=(l + off, ici, fwd): _ag_comm(self.wts, *ride))
        return rides

    def _shared(self, full):
        self.full, self.unshared = full, []

    def _ride(self, red, what, swap_keys):
        if what == "swap":
            return _swap_comm(red.grads, swap_keys, lambda theirs: red.pairs.update(
                _pair_sum(red.grads, theirs, self.c_arr, swap_keys)))
        if what == "share":
            return _share_comm(self.unshared, self.full, self._shared)
        return _rs_comm(red.pairs, what, red.recv)

    def bwd_comms(self, l):
        rides = _Rides()
        if self.pending is not None:
            for name, what in (RS_RIDES_LAST if l == 0 else RS_RIDES).items():
                if what != "share" or self.unshared:
                    rides.add(name, lambda what=what, red=self.pending: self._ride(red, what, BIG))
        if l == 0:
            self.last = _Reduction(rides.grads)
            for name, what in EARLY_RIDES.items():
                rides.add(name, lambda what=what: self._ride(self.last, what, EARLY))
        return rides

    def _reduce(self, l, red):
        self.full = _quad_sum(red.pairs, red.recv, self.where, l, self.full)
        self.unshared = self.unshared + [l]

    def after_bwd(self, l, grads):
        if self.pending is not None:
            self._reduce(l + 1, self.pending)
        if self.full is None:
            geo = _piece_geo(grads)
            self.full = {k: jnp.zeros((self.depth, 2) + geo[k], F32) for k in BIG}
        self.pending = _Reduction(grads) if l > 0 else None
        if l == 0:
            self.last.grads = grads

    def finish(self, extra):
        red = self.last
        late = tuple(k for k in BIG if k not in red.pairs)
        _run_comm(self._ride(red, "swap", late), "rs_swap_halves")
        _run_comm(_rs_comm(red.pairs, late, red.recv), "rs_to_owners")
        self._reduce(0, red)
        _run_comm(_both(_share_comm(self.unshared, self.full, self._shared), extra), "rs_share")
        return self.full


SHARDED_SMALL = ("conv_a_w", "conv_c_w", "conv_f_w")
SMALL = ("norm_mix_g", "conv_a_w", "conv_c_w", "conv_c_b", "ln_c_g", "ln_c_b", "out_norm_g", "norm_ffn_g",
         "conv_f_w", "rel_bias", "final_g")
SLAB_ROWS = 16


def _pack(arrays):
    flat = jnp.concatenate([a.reshape(-1) for a in arrays])
    unit = SLAB_ROWS * LANES
    total = -(-flat.shape[0] // unit) * unit
    return jnp.pad(flat, (0, total - flat.shape[0])).reshape(-1, LANES)


def _unpack(slab, shapes):
    flat = slab.reshape(-1)
    out, off = [], 0
    for shp in shapes:
        size = math.prod(shp)
        out.append(flat[off:off + size].reshape(shp))
        off += size
    return out


def kernel(x, norm_mix_g, w_in, conv_a_w, conv_c_w, conv_c_b, ln_c_g, ln_c_b, out_norm_g, w_out, norm_ffn_g, w_up, conv_f_w, w_down, rel_bias, final_g, loss_target, m_norm_mix_g, m_w_in, m_conv_a_w, m_conv_c_w, m_conv_c_b, m_ln_c_g, m_ln_c_b, m_out_norm_g, m_w_out, m_norm_ffn_g, m_w_up, m_conv_f_w, m_w_down, m_rel_bias, m_final_g, v_norm_mix_g, v_w_in, v_conv_a_w, v_conv_c_w, v_conv_c_b, v_ln_c_g, v_ln_c_b, v_out_norm_g, v_w_out, v_norm_ffn_g, v_w_up, v_conv_f_w, v_w_down, v_rel_bias, v_final_g):
    weights = dict(norm_mix_g=norm_mix_g, w_in=w_in, conv_a_w=conv_a_w, conv_c_w=conv_c_w, conv_c_b=conv_c_b,
                   ln_c_g=ln_c_g, ln_c_b=ln_c_b, out_norm_g=out_norm_g, w_out=w_out, norm_ffn_g=norm_ffn_g, w_up=w_up,
                   conv_f_w=conv_f_w, w_down=w_down, rel_bias=rel_bias, final_g=final_g)
    mom_m = dict(norm_mix_g=m_norm_mix_g, w_in=m_w_in, conv_a_w=m_conv_a_w, conv_c_w=m_conv_c_w, conv_c_b=m_conv_c_b,
                 ln_c_g=m_ln_c_g, ln_c_b=m_ln_c_b, out_norm_g=m_out_norm_g, w_out=m_w_out, norm_ffn_g=m_norm_ffn_g,
                 w_up=m_w_up, conv_f_w=m_conv_f_w, w_down=m_w_down, rel_bias=m_rel_bias, final_g=m_final_g)
    mom_v = dict(norm_mix_g=v_norm_mix_g, w_in=v_w_in, conv_a_w=v_conv_a_w, conv_c_w=v_conv_c_w, conv_c_b=v_conv_c_b,
                 ln_c_g=v_ln_c_g, ln_c_b=v_ln_c_b, out_norm_g=v_out_norm_g, w_out=v_w_out, norm_ffn_g=v_norm_ffn_g,
                 w_up=v_w_up, conv_f_w=v_conv_f_w, w_down=v_w_down, rel_bias=v_rel_bias, final_g=v_final_g)
    xi, yi, ci = _me()
    chip = _chip_of(xi, yi)
    c_arr = jnp.reshape(ci, (1,)).astype(I32)
    chip_arr = jnp.reshape(chip, (1,)).astype(I32)
    me_arr = jnp.reshape(4 * xi + 2 * yi + ci, (1,)).astype(I32)
    where = jnp.stack([chip, ci]).astype(I32)
    depth = w_out.shape[0]

    wts = {"in_t": _cast_into_gathered(jnp.swapaxes(w_in, 1, 2), chip_arr, False, "cast_in"),
           "out": _cast_into_gathered(w_out, chip_arr, False, "cast_out"),
           "up": _cast_into_gathered(w_up, chip_arr, True, "cast_up"),
           "down": _cast_into_gathered(w_down, chip_arr, False, "cast_down")}
    store = {}
    _run_comm(_both(_ag_comm(wts, 0, AG_FIRST, AG_FIRST),
                    _small_gather_comm(_pack([weights[n] for n in SHARDED_SMALL]), store)), "ag_weights")
    prm = {n: weights[n] for n in SMALL if n not in SHARDED_SMALL}
    per_chip = [_unpack(store["small"][j], [weights[n].shape for n in SHARDED_SMALL]) for j in range(N_CHIPS)]
    for i, n in enumerate(SHARDED_SMALL):
        prm[n] = jnp.concatenate([per_chip[j][i] for j in range(N_CHIPS)], axis=-1)

    sched = _MeshSchedule(wts, depth, c_arr, where)
    loss_row, dx, small, d_rel, d_final = _local_step(x[0], loss_target[0], wts, prm, sched)
    loss = lax.psum(loss_row[0, 0], ("x", "y", "c"))

    stacked = {n: jnp.stack([small[l][n] for l in range(depth)]) for n in small[0]}
    stacked["rel_bias"] = d_rel
    stacked["final_g"] = d_final
    full_shapes = [stacked[n].shape for n in SMALL]
    partial = _pack([stacked[n] for n in SMALL])
    reduced = sched.finish(_gather_comm(partial, lambda res: store.update(partials=res)))

    grads = {}
    shard_shapes = {"in_t": jnp.swapaxes(w_in, 1, 2).shape, "out": w_out.shape, "up": w_up.shape, "down": w_down.shape}
    red = {k: reduced[k].reshape(shard_shapes[k]) for k in BIG}
    grads["w_in"] = jnp.swapaxes(red["in_t"], 1, 2)
    grads["w_out"], grads["w_up"], grads["w_down"] = red["out"], red["up"], red["down"]
    delta, new_m, new_v = {}, {}, {}
    for n in ("w_in", "w_out", "w_up", "w_down"):
        shp = weights[n].shape
        flat = lambda a, shp=shp: a.reshape(shp[0] * shp[1], shp[2])
        tile = max(t for t in range(8, 257, 8) if shp[1] % t == 0)
        g, d, mn, vn = _adamw(flat(weights[n]), flat(grads[n]), flat(mom_m[n]), flat(mom_v[n]), "adamw_" + n, tile)
        grads[n], delta[n], new_m[n], new_v[n] = g.reshape(shp), d.reshape(shp), mn.reshape(shp), vn.reshape(shp)

    summed = _unpack(_sum_slabs(store["partials"], partial, me_arr), full_shapes)
    for n, g in zip(SMALL, summed):
        if n in SHARDED_SMALL:
            width = weights[n].shape[-1]
            g = lax.dynamic_slice_in_dim(g, chip * width, width, axis=g.ndim - 1)
        grads[n] = g
    res = _adamw_small([[src[n] for n in SMALL] for src in (weights, grads, mom_m, mom_v)], "adamw_small")
    for i, n in enumerate(SMALL):
        delta[n], new_m[n], new_v[n] = res[0][i], res[1][i], res[2][i]

    order = ("norm_mix_g", "w_in", "conv_a_w", "conv_c_w", "conv_c_b", "ln_c_g", "ln_c_b", "out_norm_g", "w_out",
             "norm_ffn_g", "w_up", "conv_f_w", "w_down", "rel_bias", "final_g")
    return (loss, dx[None], *[grads[n] for n in order], *[delta[n] for n in order], *[new_m[n] for n in order],
            *[new_v[n] for n in order])
```

```python
import functools
import math

import numpy as np
import jax
import jax.numpy as jnp
from jax import lax
from jax.experimental import pallas as pl
from jax.experimental.pallas import tpu as pltpu

F32 = jnp.float32
BF16 = jnp.bfloat16
I32 = jnp.int32

EPS = 1e-6
NEG = -1e30
D_HEAD = 64
LANES = 128
BLK = 128
ATTN_GROUP_FWD = 16
ATTN_GROUP_BWD = 16
DILATED_BRANCHES = ((128, 1), (512, 4), (2048, 16))
NUM_BUCKETS = 32
MAX_DISTANCE = 2048
SHORT_CONV = 3
CONFORMER_CONV = 31
FFN_CONV = 3
PAD_SHORT = 8
PAD_LONG = 32
ROW_CHUNK = 256
PROJ_ROWS = 512
FFN_ROWS = 64
V7X_VMEM_BYTES = 64 * 1024 * 1024
VMEM_REQUEST = V7X_VMEM_BYTES * 7 // 8

ADAM_LR = 0.001
ADAM_B1 = 0.9
ADAM_B2 = 0.999
ADAM_EPS = 1e-08
ADAM_WD = 0.01
ADAM_STEP = 10

MESH = pl.DeviceIdType.MESH
ANY = pl.BlockSpec(memory_space=pl.ANY)


def _sds(shape, dtype):
    return jax.ShapeDtypeStruct(tuple(shape), dtype)


class _Comm:
    def __init__(self, ins, out_shapes, aliases, sems, start, finish, done):
        self.ins, self.out_shapes, self.aliases, self.sems = list(ins), list(out_shapes), dict(aliases), list(sems)
        self.start, self.finish, self.done = start, finish, done


def _pcall(body, *, name, out_shape, grid=(), in_specs=None, out_specs=None, scratch_shapes=(), vmem=VMEM_REQUEST,
           aliases=None, prefetch=0, comm=None):
    params = pltpu.CompilerParams(dimension_semantics=("arbitrary",) * len(grid), vmem_limit_bytes=vmem)
    single = not isinstance(out_shape, (tuple, list))
    outs = [out_shape] if single else list(out_shape)
    ospecs = [out_specs] if single else list(out_specs)
    ispecs, scratch, aliases = list(in_specs), list(scratch_shapes), dict(aliases or {})
    n_in, n_out, n_scr = len(ispecs), len(outs), len(scratch)
    kernel_body = body
    if comm is not None:
        n_ci, n_co = len(comm.ins), len(comm.out_shapes)

        def kernel_body(*refs):
            pre, rest = refs[:prefetch], refs[prefetch:]
            core_in, c_in = rest[:n_in], rest[n_in:n_in + n_ci]
            o0 = n_in + n_ci
            core_out, c_out = rest[o0:o0 + n_out], rest[o0 + n_out:o0 + n_out + n_co]
            s0 = o0 + n_out + n_co
            core_scr, c_sem = rest[s0:s0 + n_scr], rest[s0 + n_scr:]
            first = functools.reduce(jnp.logical_and, [pl.program_id(a) == 0 for a in range(len(grid))])
            last = functools.reduce(jnp.logical_and, [pl.program_id(a) == grid[a] - 1 for a in range(len(grid))])
            pl.when(first)(lambda: comm.start(c_in, c_out, c_sem))
            body(*pre, *core_in, *core_out, *core_scr)
            pl.when(last)(lambda: comm.finish(c_in, c_out, c_sem))

        for i, o in comm.aliases.items():
            aliases[prefetch + n_in + i] = n_out + o
        ispecs += [ANY] * n_ci
        ospecs += [ANY] * n_co
        outs += comm.out_shapes
        scratch += comm.sems
    if prefetch:
        spec = pltpu.PrefetchScalarGridSpec(num_scalar_prefetch=prefetch, grid=grid, in_specs=ispecs,
                                            out_specs=tuple(ospecs), scratch_shapes=scratch)
        call = pl.pallas_call(kernel_body, name=name, out_shape=tuple(outs), grid_spec=spec,
                              input_output_aliases=aliases, compiler_params=params)
    else:
        call = pl.pallas_call(kernel_body, name=name, out_shape=tuple(outs), grid=grid, in_specs=ispecs,
                              out_specs=tuple(ospecs), scratch_shapes=scratch, input_output_aliases=aliases,
                              compiler_params=params)

    def run(*args):
        res = call(*args, *(comm.ins if comm is not None else ()))
        if comm is not None:
            comm.done(res[n_out:])
        return res[0] if single else tuple(res[:n_out])

    return run


def _both(a, b):
    def split(refs, na):
        return refs[:na], refs[na:]

    def run(which):
        def go(ins, outs, sems):
            for comm, i, o, s in zip((a, b), split(ins, len(a.ins)), split(outs, len(a.out_shapes)), split(sems, len(a.sems))):
                getattr(comm, which)(i, o, s)
        return go

    def done(res):
        a.done(res[:len(a.out_shapes)])
        b.done(res[len(a.out_shapes):])

    aliases = dict(a.aliases)
    aliases.update({len(a.ins) + i: len(a.out_shapes) + o for i, o in b.aliases.items()})
    return _Comm(a.ins + b.ins, a.out_shapes + b.out_shapes, aliases, a.sems + b.sems, run("start"), run("finish"), done)


def _run_comm(comm, name):
    def body(*refs):
        n_ci, n_co = len(comm.ins), len(comm.out_shapes)
        c_in, c_out, c_sem = refs[:n_ci], refs[n_ci:n_ci + n_co], refs[n_ci + n_co:]
        comm.start(c_in, c_out, c_sem)
        comm.finish(c_in, c_out, c_sem)

    res = pl.pallas_call(body, name=name, out_shape=tuple(comm.out_shapes), in_specs=[ANY] * len(comm.ins),
                         out_specs=tuple([ANY] * len(comm.out_shapes)), scratch_shapes=comm.sems,
                         input_output_aliases=comm.aliases)(*comm.ins)
    comm.done(res)


def _dot(a, b):
    return lax.dot_general(a, b, (((1,), (0,)), ((), ())), preferred_element_type=F32)


def _dot_nt(a, b):
    return lax.dot_general(a, b, (((1,), (1,)), ((), ())), preferred_element_type=F32)


def _dot_tn(a, b):
    return lax.dot_general(a, b, (((0,), (0,)), ((), ())), preferred_element_type=F32)


def _sigmoid(x):
    return 1.0 / (1.0 + jnp.exp(-x))


def _rstd(x):
    return lax.rsqrt(jnp.mean(x * x, axis=-1, keepdims=True) + EPS)


def _rms_fwd(x, g, name):
    s, d = x.shape
    tm = ROW_CHUNK

    def body(x_ref, g_ref, o_ref):
        xv = x_ref[...]
        o_ref[...] = (xv * _rstd(xv) * g_ref[...]).astype(BF16)

    return _pcall(body, name=name, out_shape=_sds((s, d), BF16), grid=(s // tm,),
                  in_specs=[pl.BlockSpec((tm, d), lambda i: (i, 0)), pl.BlockSpec((1, d), lambda i: (0, 0))],
                  out_specs=pl.BlockSpec((tm, d), lambda i: (i, 0)))(x, g)


def _final_loss(x, g, tgt, name):
    s, d = x.shape
    tm = ROW_CHUNK

    def body(x_ref, g_ref, t_ref, loss_ref, dx_ref, dxb_ref, dg_ref):
        i = pl.program_id(0)
        xv = x_ref[...]
        r = _rstd(xv)
        xh = xv * r
        e = xh * g_ref[...] - t_ref[...]
        lpart = 0.5 * jnp.sum(jnp.mean(e * e, axis=-1, keepdims=True), axis=0, keepdims=True)
        dy = e * (1.0 / d)
        gd = dy * g_ref[...]
        dx = r * (gd - xh * jnp.mean(gd * xh, axis=-1, keepdims=True))
        dx_ref[...] = dx
        dxb_ref[...] = dx.astype(BF16)
        part = jnp.sum(dy * xh, axis=0, keepdims=True)
        lrow = jnp.broadcast_to(lpart, (1, LANES))

        @pl.when(i == 0)
        def _():
            dg_ref[...] = part
            loss_ref[...] = lrow

        @pl.when(i > 0)
        def _():
            dg_ref[...] += part
            loss_ref[...] += lrow

    row = pl.BlockSpec((tm, d), lambda i: (i, 0))
    vec = pl.BlockSpec((1, d), lambda i: (0, 0))
    return _pcall(body, name=name,
                  out_shape=(_sds((1, LANES), F32), _sds((s, d), F32), _sds((s, d), BF16), _sds((1, d), F32)),
                  grid=(s // tm,), in_specs=[row, vec, row],
                  out_specs=(pl.BlockSpec((1, LANES), lambda i: (0, 0)), row, row, vec))(x, g, tgt)


def _mm_n(a, b, layer, *, nt, tn, out_dtype, name, resid=None, b_part=0, comm=None):
    s, k = a.shape
    n = b.shape[1] if nt else b.shape[2]
    rows = 512

    def body(a_ref, b_ref, *refs):
        o_ref = refs[-1]
        bv = b_ref[...]
        for r0 in range(0, s, rows):
            av = a_ref[r0:r0 + rows, :]
            prod = _dot_nt(av, bv) if nt else _dot(av, bv)
            if resid is not None:
                prod = refs[0][r0:r0 + rows, :] + prod
            o_ref[r0:r0 + rows, :] = prod.astype(out_dtype)

    b_spec = (pl.BlockSpec((None, tn, k), lambda j: (layer, j, b_part)) if nt
              else pl.BlockSpec((None, k, tn), lambda j: (layer, b_part, j)))
    col = pl.BlockSpec((s, tn), lambda j: (0, j))
    extra = () if resid is None else (resid,)
    return _pcall(body, name=name, out_shape=_sds((s, n), out_dtype), grid=(n // tn,),
                  in_specs=[pl.BlockSpec((s, k), lambda j: (0, 0)), b_spec] + [col] * len(extra),
                  out_specs=col, comm=comm)(a, b, *extra)


def _mm_tn(a, b, *, t, name):
    s, ka = a.shape
    n = b.shape[1]

    def body(a_ref, b_ref, o_ref):
        o_ref[...] = _dot_tn(a_ref[...], b_ref[...]).astype(BF16)

    return _pcall(body, name=name, out_shape=_sds((ka, n), BF16), grid=(ka // t,),
                  in_specs=[pl.BlockSpec((s, t), lambda i: (0, i)), pl.BlockSpec((s, n), lambda i: (0, 0))],
                  out_specs=pl.BlockSpec((t, n), lambda i: (i, 0)))(a, b)


def _mm_tn_pieces(pieces, b, *, t, name):
    s, n = b.shape
    blocks = [p.shape[1] // t for p in pieces]
    starts = [sum(blocks[:i]) for i in range(len(pieces))]

    def body(*refs):
        p_refs, b_ref, o_ref = refs[:len(pieces)], refs[len(pieces)], refs[len(pieces) + 1]
        j = pl.program_id(0)
        for p_ref, start, count in zip(p_refs, starts, blocks):
            @pl.when((j >= start) & (j < start + count))
            def _(p_ref=p_ref):
                o_ref[...] = _dot_tn(p_ref[...], b_ref[...]).astype(BF16)

    specs = [pl.BlockSpec((s, t), lambda j, start=start, count=count: (0, jnp.clip(j - start, 0, count - 1)))
             for start, count in zip(starts, blocks)]
    return _pcall(body, name=name, out_shape=_sds((sum(blocks) * t, n), BF16), grid=(sum(blocks),),
                  in_specs=specs + [pl.BlockSpec((s, n), lambda j: (0, 0))],
                  out_specs=pl.BlockSpec((t, n), lambda j: (j, 0)))(*pieces, b)


def _mm_tn2(a, b_lo, b_hi, *, t, name, comm=None):
    s, ka = a.shape
    half = b_lo.shape[1]
    nb = half // t

    def body(a_ref, lo_ref, hi_ref, o_ref):
        j = pl.program_id(0)

        @pl.when(j < nb)
        def _():
            o_ref[...] = _dot_tn(a_ref[...], lo_ref[...]).astype(BF16)

        @pl.when(j >= nb)
        def _():
            o_ref[...] = _dot_tn(a_ref[...], hi_ref[...]).astype(BF16)

    return _pcall(body, name=name, out_shape=_sds((ka, 2 * half), BF16), grid=(2 * nb,),
                  in_specs=[pl.BlockSpec((s, ka), lambda j: (0, 0)),
                            pl.BlockSpec((s, t), lambda j: (0, jnp.minimum(j, nb - 1))),
                            pl.BlockSpec((s, t), lambda j: (0, jnp.maximum(j - nb, 0)))],
                  out_specs=pl.BlockSpec((ka, t), lambda j: (0, j)), comm=comm)(a, b_lo, b_hi)


SUBLANES = 8


def _tap_windows(win, width, lead, rows):
    offs = [lead + k for k in range(width)]
    if width <= SUBLANES:
        return [win[o:o + rows, :] for o in offs]
    n = win.shape[0]
    out = {}
    for r in sorted({o % SUBLANES for o in offs}):
        base = win if r == 0 else pltpu.roll(win, n - r, axis=0)
        for o in offs:
            if o % SUBLANES == r:
                out[o - lead] = base[o - r:o - r + rows, :]
    return [out[k] for k in range(width)]


def _conv_taps(taps, w_ref):
    acc = None
    for k, tap in enumerate(taps):
        term = w_ref[pl.ds(k, 1), :] * tap
        acc = term if acc is None else acc + term
    return acc


def _causal_taps(win, width, pad, rows):
    return _tap_windows(win, width, pad - (width - 1), rows)


def _anticausal_taps(win, width, rows):
    return _tap_windows(win, width, 0, rows)[::-1]


def _conv_wgrad(dw_ref, g, taps):
    for k, tap in enumerate(taps):
        dw_ref[pl.ds(k, 1), :] += jnp.sum(g * tap, axis=0, keepdims=True)


def _mixer_a_fwd(ab, taps_t, wa_ref):
    ct = _conv_taps(taps_t, wa_ref)
    return ab * ct, ct


def _mixer_c_fwd(taps_u, wc_ref, cb_ref, lg_ref, lb_ref):
    u = _conv_taps(taps_u, wc_ref) + cb_ref[...]
    mu = jnp.mean(u, axis=-1, keepdims=True)
    uc = u - mu
    rs = lax.rsqrt(jnp.mean(uc * uc, axis=-1, keepdims=True) + EPS)
    uh = uc * rs
    ln = uh * lg_ref[...] + lb_ref[...]
    sg = _sigmoid(ln)
    return ln * sg, ln, sg, uh, rs


def _mix_fwd(z, wa, wc, cb, lg, lb, ga, gc, name):
    s = z.shape[0]
    w = wa.shape[1]
    nblk = z.shape[1] // w
    rc = ROW_CHUNK

    def body(ah_ref, ab_ref, ac_ref, cv_ref, cg_ref, wa_ref, wc_ref, cb_ref, lg_ref, lb_ref, ga_ref, gc_ref,
             ya_ref, yc_ref, tpad, upad):
        tpad[pl.ds(0, PAD_SHORT), :] = jnp.zeros((PAD_SHORT, w), F32)
        upad[pl.ds(0, PAD_LONG), :] = jnp.zeros((PAD_LONG, w), F32)

        def chunk(i, carry):
            base = pl.multiple_of(i * rc, rc)
            rows = pl.ds(base, rc)
            ah, ab, ac = ah_ref[rows, :], ab_ref[rows, :], ac_ref[rows, :]
            tpad[pl.ds(base + PAD_SHORT, rc), :] = ac * ah
            ya, _ = _mixer_a_fwd(ab, _causal_taps(tpad[pl.ds(base, rc + PAD_SHORT), :], SHORT_CONV, PAD_SHORT, rc), wa_ref)
            ya_ref[rows, :] = (ya * _rstd(ya) * ga_ref[...]).astype(BF16)
            upad[pl.ds(base + PAD_LONG, rc), :] = cv_ref[rows, :] * _sigmoid(cg_ref[rows, :])
            taps_u = _causal_taps(upad[pl.ds(base, rc + PAD_LONG), :], CONFORMER_CONV, PAD_LONG, rc)
            yc = _mixer_c_fwd(taps_u, wc_ref, cb_ref, lg_ref, lb_ref)[0]
            yc_ref[rows, :] = (yc * _rstd(yc) * gc_ref[...]).astype(BF16)
            return carry

        lax.fori_loop(0, s // rc, chunk, 0)

    def zblk(j):
        return pl.BlockSpec((s, w), lambda i: (0, j))

    def whole(a):
        return pl.BlockSpec(a.shape, lambda i: (0, 0))

    return _pcall(
        body, name=name, out_shape=(_sds((s, w), BF16), _sds((s, w), BF16)), grid=(1,),
        in_specs=[zblk(0), zblk(1), zblk(2), zblk(nblk - 2), zblk(nblk - 1)] + [whole(a) for a in (wa, wc, cb, lg, lb, ga, gc)],
        out_specs=(pl.BlockSpec((s, w), lambda i: (0, 0)), pl.BlockSpec((s, w), lambda i: (0, 0))),
        scratch_shapes=[pltpu.VMEM((s + PAD_SHORT, w), F32), pltpu.VMEM((s + PAD_LONG, w), F32)],
    )(z, z, z, z, z, wa, wc, cb, lg, lb, ga, gc)


def _mix_bwd(z, dy, wa, wc, cb, lg, lb, ga, gc, name, comm=None):
    s = z.shape[0]
    w = wa.shape[1]
    nblk = z.shape[1] // w
    nyb = dy.shape[1] // w
    rc = ROW_CHUNK

    def body(ah_ref, ab_ref, ac_ref, cv_ref, cg_ref, dya_ref, dyc_ref,
             wa_ref, wc_ref, cb_ref, lg_ref, lb_ref, ga_ref, gc_ref,
             dza_ref, dzc_ref, dwa_ref, dwc_ref, dcb_ref, dlg_ref, dlb_ref, dga_ref, dgc_ref,
             tpad, upad, dctp, dup):
        tpad[pl.ds(0, PAD_SHORT), :] = jnp.zeros((PAD_SHORT, w), F32)
        upad[pl.ds(0, PAD_LONG), :] = jnp.zeros((PAD_LONG, w), F32)
        dctp[pl.ds(s, PAD_SHORT), :] = jnp.zeros((PAD_SHORT, w), F32)
        dup[pl.ds(s, PAD_LONG), :] = jnp.zeros((PAD_LONG, w), F32)
        for ref in (dwa_ref, dwc_ref, dcb_ref, dlg_ref, dlb_ref, dga_ref, dgc_ref):
            ref[...] = jnp.zeros(ref.shape, F32)

        def rms_bwd(y, g_ref, dyn, dg_ref):
            r = _rstd(y)
            yh = y * r
            gd = dyn * g_ref[...]
            dg_ref[...] += jnp.sum(dyn * yh, axis=0, keepdims=True)
            return r * (gd - yh * jnp.mean(gd * yh, axis=-1, keepdims=True))

        def first(i, carry):
            base = pl.multiple_of(i * rc, rc)
            rows = pl.ds(base, rc)
            ah, ab, ac = ah_ref[rows, :], ab_ref[rows, :], ac_ref[rows, :]
            tpad[pl.ds(base + PAD_SHORT, rc), :] = ac * ah
            taps_t = _causal_taps(tpad[pl.ds(base, rc + PAD_SHORT), :], SHORT_CONV, PAD_SHORT, rc)
            ya, ct = _mixer_a_fwd(ab, taps_t, wa_ref)
            dya = rms_bwd(ya, ga_ref, dya_ref[rows, :], dga_ref)
            dza_ref[rows, w:2 * w] = (dya * ct).astype(BF16)
            dct = dya * ab
            dctp[rows, :] = dct
            _conv_wgrad(dwa_ref, dct, taps_t)

            upad[pl.ds(base + PAD_LONG, rc), :] = cv_ref[rows, :] * _sigmoid(cg_ref[rows, :])
            taps_u = _causal_taps(upad[pl.ds(base, rc + PAD_LONG), :], CONFORMER_CONV, PAD_LONG, rc)
            yc, ln, sg, uh, rs = _mixer_c_fwd(taps_u, wc_ref, cb_ref, lg_ref, lb_ref)
            dyc = rms_bwd(yc, gc_ref, dyc_ref[rows, :], dgc_ref)
            dln = dyc * (sg * (1.0 + ln * (1.0 - sg)))
            dlg_ref[...] += jnp.sum(dln * uh, axis=0, keepdims=True)
            dlb_ref[...] += jnp.sum(dln, axis=0, keepdims=True)
            duh = dln * lg_ref[...]
            du = rs * (duh - jnp.mean(duh, axis=-1, keepdims=True) - uh * jnp.mean(duh * uh, axis=-1, keepdims=True))
            dcb_ref[...] += jnp.sum(du, axis=0, keepdims=True)
            dup[rows, :] = du
            _conv_wgrad(dwc_ref, du, taps_u)
            return carry

        lax.fori_loop(0, s // rc, first, 0)

        def second(i, carry):
            base = pl.multiple_of(i * rc, rc)
            rows = pl.ds(base, rc)
            dt = _conv_taps(_anticausal_taps(dctp[pl.ds(base, rc + PAD_SHORT), :], SHORT_CONV, rc), wa_ref)
            dza_ref[rows, 0:w] = (dt * ac_ref[rows, :]).astype(BF16)
            dza_ref[rows, 2 * w:3 * w] = (dt * ah_ref[rows, :]).astype(BF16)
            du0 = _conv_taps(_anticausal_taps(dup[pl.ds(base, rc + PAD_LONG), :], CONFORMER_CONV, rc), wc_ref)
            sg = _sigmoid(cg_ref[rows, :])
            dzc_ref[rows, 0:w] = (du0 * sg).astype(BF16)
            dzc_ref[rows, w:2 * w] = (du0 * cv_ref[rows, :] * sg * (1.0 - sg)).astype(BF16)
            return carry

        lax.fori_loop(0, s // rc, second, 0)

    def blk(j):
        return pl.BlockSpec((s, w), lambda i: (0, j))

    def whole(a):
        return pl.BlockSpec(tuple(a.shape), lambda i: (0, 0))

    params = (wa, wc, cb, lg, lb, ga, gc)
    outs = (_sds((s, 3 * w), BF16), _sds((s, 2 * w), BF16)) + tuple(_sds(p.shape, F32) for p in params)
    return _pcall(
        body, name=name, out_shape=outs, grid=(1,),
        in_specs=[blk(0), blk(1), blk(2), blk(nblk - 2), blk(nblk - 1), blk(0), blk(nyb - 1)] + [whole(p) for p in params],
        out_specs=tuple(whole(o) for o in outs),
        scratch_shapes=[pltpu.VMEM((s + PAD_SHORT, w), F32), pltpu.VMEM((s + PAD_LONG, w), F32),
                        pltpu.VMEM((s + PAD_SHORT, w), F32), pltpu.VMEM((s + PAD_LONG, w), F32)], comm=comm,
    )(z, z, z, z, z, dy, dy, *params)


def _ffn_act_fwd(up, wf, name, comm=None):
    s, f2 = up.shape
    f = f2 // 2
    tc = 256
    nb = f // tc
    rc = FFN_ROWS

    def body(g_ref, v_ref, wg_ref, wv_ref, o_ref, gpad, vpad):
        gpad[pl.ds(0, PAD_SHORT), :] = jnp.zeros((PAD_SHORT, tc), F32)
        vpad[pl.ds(0, PAD_SHORT), :] = jnp.zeros((PAD_SHORT, tc), F32)

        def chunk(i, carry):
            base = pl.multiple_of(i * rc, rc)
            rows = pl.ds(base, rc)
            gpad[pl.ds(base + PAD_SHORT, rc), :] = g_ref[rows, :].astype(F32)
            vpad[pl.ds(base + PAD_SHORT, rc), :] = v_ref[rows, :].astype(F32)
            gc = _conv_taps(_causal_taps(gpad[pl.ds(base, rc + PAD_SHORT), :], FFN_CONV, PAD_SHORT, rc), wg_ref)
            vc = _conv_taps(_causal_taps(vpad[pl.ds(base, rc + PAD_SHORT), :], FFN_CONV, PAD_SHORT, rc), wv_ref)
            o_ref[rows, :] = (gc * _sigmoid(gc) * vc).astype(BF16)
            return carry

        lax.fori_loop(0, s // rc, chunk, 0)

    return _pcall(
        body, name=name, out_shape=_sds((s, f), BF16), grid=(nb,),
        in_specs=[pl.BlockSpec((s, tc), lambda j: (0, j)), pl.BlockSpec((s, tc), lambda j: (0, j + nb)),
                  pl.BlockSpec((FFN_CONV, tc), lambda j: (0, j)), pl.BlockSpec((FFN_CONV, tc), lambda j: (0, j + nb))],
        out_specs=pl.BlockSpec((s, tc), lambda j: (0, j)),
        scratch_shapes=[pltpu.VMEM((s + PAD_SHORT, tc), F32), pltpu.VMEM((s + PAD_SHORT, tc), F32)], comm=comm,
    )(up, up, wf, wf)


def _ffn_act_bwd(up, dact, wf, name, comm=None):
    s, f2 = up.shape
    f = f2 // 2
    tc = 256
    nb = f // tc
    rc = FFN_ROWS

    def body(g_ref, v_ref, da_ref, wg_ref, wv_ref, act_ref, dg_ref, dv_ref, dwg_ref, dwv_ref, gpad, vpad, dgp, dvp):
        gpad[pl.ds(0, PAD_SHORT), :] = jnp.zeros((PAD_SHORT, tc), F32)
        vpad[pl.ds(0, PAD_SHORT), :] = jnp.zeros((PAD_SHORT, tc), F32)
        dgp[pl.ds(s, PAD_SHORT), :] = jnp.zeros((PAD_SHORT, tc), F32)
        dvp[pl.ds(s, PAD_SHORT), :] = jnp.zeros((PAD_SHORT, tc), F32)
        dwg_ref[...] = jnp.zeros((FFN_CONV, tc), F32)
        dwv_ref[...] = jnp.zeros((FFN_CONV, tc), F32)

        def first(i, carry):
            base = pl.multiple_of(i * rc, rc)
            rows = pl.ds(base, rc)
            gpad[pl.ds(base + PAD_SHORT, rc), :] = g_ref[rows, :].astype(F32)
            vpad[pl.ds(base + PAD_SHORT, rc), :] = v_ref[rows, :].astype(F32)
            taps_g = _causal_taps(gpad[pl.ds(base, rc + PAD_SHORT), :], FFN_CONV, PAD_SHORT, rc)
            taps_v = _causal_taps(vpad[pl.ds(base, rc + PAD_SHORT), :], FFN_CONV, PAD_SHORT, rc)
            gc = _conv_taps(taps_g, wg_ref)
            vc = _conv_taps(taps_v, wv_ref)
            sg = _sigmoid(gc)
            silu = gc * sg
            act_ref[rows, :] = (silu * vc).astype(BF16)
            da = da_ref[rows, :].astype(F32)
            dgc = da * vc * (sg * (1.0 + gc * (1.0 - sg)))
            dvc = da * silu
            dgp[rows, :] = dgc
            dvp[rows, :] = dvc
            _conv_wgrad(dwg_ref, dgc, taps_g)
            _conv_wgrad(dwv_ref, dvc, taps_v)
            return carry

        lax.fori_loop(0, s // rc, first, 0)

        def second(i, carry):
            base = pl.multiple_of(i * rc, rc)
            rows = pl.ds(base, rc)
            dg_ref[rows, :] = _conv_taps(_anticausal_taps(dgp[pl.ds(base, rc + PAD_SHORT), :], FFN_CONV, rc), wg_ref).astype(BF16)
            dv_ref[rows, :] = _conv_taps(_anticausal_taps(dvp[pl.ds(base, rc + PAD_SHORT), :], FFN_CONV, rc), wv_ref).astype(BF16)
            return carry

        lax.fori_loop(0, s // rc, second, 0)

    lo = pl.BlockSpec((s, tc), lambda j: (0, j))
    hi = pl.BlockSpec((s, tc), lambda j: (0, j + nb))
    wlo = pl.BlockSpec((FFN_CONV, tc), lambda j: (0, j))
    whi = pl.BlockSpec((FFN_CONV, tc), lambda j: (0, j + nb))
    act, dgate, dval, dwg, dwv = _pcall(
        body, name=name,
        out_shape=(_sds((s, f), BF16), _sds((s, f), BF16), _sds((s, f), BF16), _sds((FFN_CONV, f), F32), _sds((FFN_CONV, f), F32)),
        grid=(nb,), in_specs=[lo, hi, lo, wlo, whi], out_specs=(lo, lo, lo, wlo, wlo),
        scratch_shapes=[pltpu.VMEM((s + PAD_SHORT, tc), F32) for _ in range(4)], comm=comm,
    )(up, up, dact, wf, wf)
    return act, dgate, dval, jnp.concatenate([dwg, dwv], axis=1)


def _out_proj(yan, yb, ycn, gb, x, w_out, layer, g_next, name, comm=None):
    s, w = yan.shape
    wb = yb.shape[1]
    d = x.shape[1]
    tm = PROJ_ROWS

    def body(ya_ref, yb_ref, yc_ref, gb_ref, x_ref, w_ref, g_ref, y_ref, xm_ref, h_ref):
        ybv = yb_ref[...]
        y = jnp.concatenate([ya_ref[...], (ybv * _rstd(ybv) * gb_ref[...]).astype(BF16), yc_ref[...]], axis=1)
        y_ref[...] = y
        xm = x_ref[...] + _dot(y, w_ref[...])
        xm_ref[...] = xm
        h_ref[...] = (xm * _rstd(xm) * g_ref[...]).astype(BF16)

    def rows(width):
        return pl.BlockSpec((tm, width), lambda i: (i, 0))

    def vec(width):
        return pl.BlockSpec((1, width), lambda i: (0, 0))

    return _pcall(body, name=name, out_shape=(_sds((s, d), BF16), _sds((s, d), F32), _sds((s, d), BF16)), grid=(s // tm,),
                  in_specs=[rows(w), rows(wb), rows(w), vec(wb), rows(d), pl.BlockSpec((None, d, d), lambda i: (layer, 0, 0)), vec(d)],
                  out_specs=(rows(d), rows(d), rows(d)), comm=comm)(yan, yb, ycn, gb, x, w_out, g_next)


def _down_proj(act, w_down, layer, x_mid, g_next, name, comm=None):
    s, f = act.shape
    d = x_mid.shape[1]
    tm = PROJ_ROWS

    def body(a_ref, w_ref, x_ref, *refs):
        xo = x_ref[...] + _dot(a_ref[...], w_ref[...])
        refs[-2 if g_next is not None else -1][...] = xo
        if g_next is not None:
            refs[-1][...] = (xo * _rstd(xo) * refs[0][...]).astype(BF16)

    row = pl.BlockSpec((tm, d), lambda i: (i, 0))
    ins = [act, w_down, x_mid] + ([g_next] if g_next is not None else [])
    in_specs = [pl.BlockSpec((tm, f), lambda i: (i, 0)), pl.BlockSpec((None, f, d), lambda i: (layer, 0, 0)), row]
    in_specs += [pl.BlockSpec((1, d), lambda i: (0, 0))] if g_next is not None else []
    outs = (_sds((s, d), F32), _sds((s, d), BF16)) if g_next is not None else (_sds((s, d), F32),)
    res = _pcall(body, name=name, out_shape=outs, grid=(s // tm,), in_specs=in_specs, out_specs=tuple([row] * len(outs)),
                 comm=comm)(*ins)
    return (res[0], res[1]) if g_next is not None else (res[0], None)


def _proj_dx(pieces, w, layer, nt, x, g, dres, name, tm, comm=None):
    s, d = x.shape
    widths = [p.shape[1] for p in pieces]

    def body(*refs):
        p_refs, (w_ref, x_ref, g_ref, dres_ref, dx_ref, dxb_ref, dg_ref) = refs[:len(pieces)], refs[len(pieces):]
        i = pl.program_id(0)
        dh, off = None, 0
        for p_ref, width in zip(p_refs, widths):
            part = _dot_nt(p_ref[...], w_ref[:, off:off + width]) if nt else _dot(p_ref[...], w_ref[off:off + width, :])
            dh = part if dh is None else dh + part
            off += width
        xv = x_ref[...]
        r = _rstd(xv)
        xh = xv * r
        gd = dh * g_ref[...]
        dx = dres_ref[...] + r * (gd - xh * jnp.mean(gd * xh, axis=-1, keepdims=True))
        dx_ref[...] = dx
        dxb_ref[...] = dx.astype(BF16)
        part = jnp.sum(dh * xh, axis=0, keepdims=True)

        @pl.when(i == 0)
        def _():
            dg_ref[...] = part

        @pl.when(i > 0)
        def _():
            dg_ref[...] += part

    row = pl.BlockSpec((tm, d), lambda i: (i, 0))
    vec = pl.BlockSpec((1, d), lambda i: (0, 0))
    w_spec = pl.BlockSpec((None,) + w.shape[1:], lambda i: (layer, 0, 0))
    return _pcall(body, name=name, out_shape=(_sds((s, d), F32), _sds((s, d), BF16), _sds((1, d), F32)), grid=(s // tm,),
                  in_specs=[pl.BlockSpec((tm, width), lambda i: (i, 0)) for width in widths] + [w_spec, row, vec, row],
                  out_specs=(row, row, vec), comm=comm)(*pieces, w, x, g, dres)


def _yb_norm_bwd(yb, dy, gb, name, comm=None):
    s, wb = yb.shape
    w = wb // 2
    heads = wb // D_HEAD
    tm = ROW_CHUNK

    def body(yb_ref, d1_ref, d2_ref, g_ref, dyb_ref, dl_ref, dg_ref):
        i = pl.program_id(0)
        y = yb_ref[...]
        dyn = jnp.concatenate([d1_ref[...], d2_ref[...]], axis=1)
        r = _rstd(y)
        yh = y * r
        gd = dyn * g_ref[...]
        dyb = r * (gd - yh * jnp.mean(gd * yh, axis=-1, keepdims=True))
        dyb_ref[...] = dyb
        part = jnp.sum(dyn * yh, axis=0, keepdims=True)
        prod = dyb * y
        even = lax.broadcasted_iota(I32, (tm, LANES), 1) < D_HEAD
        for p in range(heads // 2):
            blk = prod[:, p * LANES:(p + 1) * LANES]
            ev = jnp.sum(jnp.where(even, blk, 0.0), axis=1, keepdims=True)
            od = jnp.sum(jnp.where(even, 0.0, blk), axis=1, keepdims=True)
            dl_ref[2 * p] = jnp.broadcast_to(ev, (tm, LANES))
            dl_ref[2 * p + 1] = jnp.broadcast_to(od, (tm, LANES))

        @pl.when(i == 0)
        def _():
            dg_ref[...] = part

        @pl.when(i > 0)
        def _():
            dg_ref[...] += part

    return _pcall(
        body, name=name, out_shape=(_sds((s, wb), F32), _sds((heads, s, LANES), F32), _sds((1, wb), F32)),
        grid=(s // tm,),
        in_specs=[pl.BlockSpec((tm, wb), lambda i: (i, 0)), pl.BlockSpec((tm, w), lambda i: (i, 1)),
                  pl.BlockSpec((tm, w), lambda i: (i, 2)), pl.BlockSpec((1, wb), lambda i: (0, 0))],
        out_specs=(pl.BlockSpec((tm, wb), lambda i: (i, 0)), pl.BlockSpec((heads, tm, LANES), lambda i: (0, i, 0)),
                   pl.BlockSpec((1, wb), lambda i: (0, 0))), comm=comm,
    )(yb, dy, dy, gb)


def _t5_bucket_table():
    max_exact = NUM_BUCKETS // 2
    out = np.full((len(DILATED_BRANCHES), BLK, 2 * BLK), -1, np.int32)
    rel = np.arange(BLK)[:, None] - np.arange(2 * BLK)[None, :] + BLK
    for b, (window, dilation) in enumerate(DILATED_BRANCHES):
        n_keys = window // dilation
        dist = np.maximum(rel, 0) * dilation
        d_f = np.maximum(dist, 1).astype(np.float32)
        large = max_exact + (np.log(d_f / np.float32(max_exact)) / np.float32(math.log(MAX_DISTANCE / max_exact))
                             * np.float32(NUM_BUCKETS - max_exact)).astype(np.int32)
        large = np.minimum(large, NUM_BUCKETS - 1)
        bucket = np.where(dist < max_exact, dist, large)
        out[b] = np.where((rel >= 0) & (rel <= n_keys), bucket, -1)
    return out


def _bias_tiles(rel_bias, buckets, name):
    nbk, heads = rel_bias.shape
    nbr = buckets.shape[0]

    def body(rb_ref, bk_ref, o_ref):
        for br in range(nbr):
            bk = bk_ref[br]
            tiles = [jnp.full((BLK, 2 * BLK), NEG, F32) for _ in range(heads)]
            for b in range(nbk):
                hit = bk == b
                tiles = [jnp.where(hit, rb_ref[b, h], tiles[h]) for h in range(heads)]
            for h in range(heads):
                o_ref[br, h] = tiles[h]

    return _pcall(body, name=name, out_shape=_sds((nbr, heads, BLK, 2 * BLK), F32), grid=(1,),
                  in_specs=[pl.BlockSpec(memory_space=pltpu.SMEM), pl.BlockSpec(buckets.shape, lambda i: (0, 0, 0))],
                  out_specs=pl.BlockSpec((nbr, heads, BLK, 2 * BLK), lambda i: (0, 0, 0, 0)))(rel_bias, buckets)


def _bias_grad(dtiles, buckets, nbk, name):
    nbr, heads = dtiles.shape[:2]

    def body(dt_ref, bk_ref, o_ref):
        row = lax.broadcasted_iota(I32, (nbk, LANES), 0)
        col = lax.broadcasted_iota(I32, (nbk, LANES), 1)
        out = jnp.zeros((nbk, LANES), F32)
        for h in range(heads):
            for b in range(nbk):
                tot = jnp.zeros((), F32)
                for br in range(nbr):
                    tot = tot + jnp.sum(jnp.where(bk_ref[br] == b, dt_ref[br, h], 0.0))
                out = jnp.where((row == b) & (col == h), tot, out)
        o_ref[...] = out

    return _pcall(body, name=name, out_shape=_sds((nbk, LANES), F32), grid=(1,),
                  in_specs=[pl.BlockSpec(dtiles.shape, lambda i: (0, 0, 0, 0)), pl.BlockSpec(buckets.shape, lambda i: (0, 0, 0))],
                  out_specs=pl.BlockSpec((nbk, LANES), lambda i: (0, 0)))(dtiles, buckets)


def _largest_divisor(n, cap):
    return max(g for g in range(1, cap + 1) if n % g == 0)


def _attn_blocks(s, visit, group):
    for br, (window, d) in enumerate(DILATED_BRANCHES):
        n_blk = (s // d) // BLK
        span = BLK * d
        g1 = _largest_divisor(d, group)

        def firsts(t, carry, br=br, d=d, g1=g1):
            for j in range(g1):
                visit(br, d, t * g1 + j, False)
            return carry

        lax.fori_loop(0, d // g1, firsts, 0)
        if n_blk > 1:
            total = d * (n_blk - 1)
            g2 = _largest_divisor(total, group)

            def rest(t, carry, br=br, d=d, n_blk=n_blk, span=span, g2=g2):
                for j in range(g2):
                    idx = t * g2 + j
                    visit(br, d, idx // (n_blk - 1) + (1 + idx % (n_blk - 1)) * span, True)
                return carry

            lax.fori_loop(0, total // g2, rest, 0)


def _rows(start, size, d):
    return pl.ds(pl.multiple_of(start, BLK), size) if d == 1 else pl.ds(start, size, stride=d)


def _attn_fwd(z, btiles, col0, name, comm=None):
    s = z.shape[0]
    nbr, heads = btiles.shape[:2]
    pairs = heads // 2
    scale = D_HEAD ** -0.5
    rc = ROW_CHUNK

    def body(q_ref, k_ref, v_ref, bt_ref, yb_ref, lse_ref, acc_ref, m_ref, l_ref):
        even = lax.broadcasted_iota(I32, (BLK, LANES), 1) < D_HEAD
        even2 = lax.broadcasted_iota(I32, (2 * BLK, LANES), 1) < D_HEAD

        def visit(br, d, start, prev):
            kw = 2 * BLK if prev else BLK
            rows_q = _rows(start, BLK, d)
            rows_k = _rows(start - BLK * d, kw, d) if prev else rows_q
            qb = q_ref[rows_q, :]
            kb = k_ref[rows_k, :].astype(BF16)
            vw = v_ref[rows_k, :]
            ev_k = even2 if prev else even
            qm = jnp.concatenate([jnp.where(even, qb, 0.0), jnp.where(even, 0.0, qb)], axis=0).astype(BF16)
            bias = [bt_ref[br, e] if prev else bt_ref[br, e, :, BLK:] for e in range(2)]
            sc = _dot_nt(qm, kb) * scale + jnp.concatenate(bias, axis=0)
            m = jnp.max(sc, axis=1, keepdims=True)
            p = jnp.exp(sc - m)
            l = jnp.sum(p, axis=1, keepdims=True)
            pb = p.astype(BF16)
            vm = jnp.concatenate([jnp.where(ev_k, vw, 0.0), jnp.where(ev_k, 0.0, vw)], axis=0).astype(BF16)
            acc_ref.at[br][rows_q, :] = _dot(jnp.concatenate([pb[:BLK], pb[BLK:]], axis=1), vm)
            for e in range(2):
                m_ref.at[br, e][rows_q, :] = jnp.broadcast_to(m[e * BLK:(e + 1) * BLK], (BLK, LANES))
                l_ref.at[br, e][rows_q, :] = jnp.broadcast_to(l[e * BLK:(e + 1) * BLK], (BLK, LANES))

        _attn_blocks(s, visit, ATTN_GROUP_FWD)

        ev_c = lax.broadcasted_iota(I32, (rc, LANES), 1) < D_HEAD

        def merge(i, carry):
            rows = pl.ds(pl.multiple_of(i * rc, rc), rc)
            wts, dens = [], []
            for e in range(2):
                ms = [m_ref[br, e, rows, :] for br in range(nbr)]
                top = functools.reduce(jnp.maximum, ms)
                w = [jnp.exp(mb - top) for mb in ms]
                den = functools.reduce(lambda a, b: a + b, [w[br] * l_ref[br, e, rows, :] for br in range(nbr)])
                lse_ref[e, rows, :] = top + jnp.log(den)
                wts.append(w)
                dens.append(den)
            num = functools.reduce(lambda a, b: a + b,
                                   [jnp.where(ev_c, wts[0][br], wts[1][br]) * acc_ref[br, rows, :] for br in range(nbr)])
            yb_ref[rows, :] = num / jnp.where(ev_c, dens[0], dens[1])
            return carry

        lax.fori_loop(0, s // rc, merge, 0)

    def zcol(j):
        return pl.BlockSpec((s, LANES), lambda p, j=j: (0, col0 + j + p))

    return _pcall(
        body, name=name, out_shape=(_sds((s, pairs * LANES), F32), _sds((heads, s, LANES), F32)), grid=(pairs,),
        in_specs=[zcol(0), zcol(pairs), zcol(2 * pairs), pl.BlockSpec((nbr, 2, BLK, 2 * BLK), lambda p: (0, p, 0, 0))],
        out_specs=(pl.BlockSpec((s, LANES), lambda p: (0, p)), pl.BlockSpec((2, s, LANES), lambda p: (p, 0, 0))),
        scratch_shapes=[pltpu.VMEM((nbr, s, LANES), F32), pltpu.VMEM((nbr, 2, s, LANES), F32), pltpu.VMEM((nbr, 2, s, LANES), F32)],
        comm=comm,
    )(z, z, z, btiles)


def _attn_bwd(z, btiles, dyb, lse, delta, dbias_in, col0, name, comm=None):
    s = z.shape[0]
    nbr, heads = btiles.shape[:2]
    pairs = heads // 2
    scale = D_HEAD ** -0.5

    def body(q_ref, k_ref, v_ref, bt_ref, dy_ref, lse_ref, dl_ref, dbi_ref,
             dq_ref, dk_ref, dv_ref, db_ref, dqa, dka, dva):
        even = lax.broadcasted_iota(I32, (BLK, LANES), 1) < D_HEAD
        even2 = lax.broadcasted_iota(I32, (2 * BLK, LANES), 1) < D_HEAD
        for ref in (dqa, dka, dva):
            ref[...] = jnp.zeros((s, LANES), F32)
        db_ref[...] = dbi_ref[...]

        def visit(br, d, start, prev):
            kw = 2 * BLK if prev else BLK
            rows_q = _rows(start, BLK, d)
            rows_k = _rows(start - BLK * d, kw, d) if prev else rows_q
            qb = q_ref[rows_q, :]
            dyv = dy_ref[rows_q, :]
            kwin = k_ref[rows_k, :]
            kb = kwin.astype(BF16)
            vb = v_ref[rows_k, :].astype(BF16)
            ev_k = even2 if prev else even
            qm = jnp.concatenate([jnp.where(even, qb, 0.0), jnp.where(even, 0.0, qb)], axis=0).astype(BF16)
            dym = jnp.concatenate([jnp.where(even, dyv, 0.0), jnp.where(even, 0.0, dyv)], axis=0).astype(BF16)
            bias = [bt_ref[br, e] if prev else bt_ref[br, e, :, BLK:] for e in range(2)]
            sc = _dot_nt(qm, kb) * scale + jnp.concatenate(bias, axis=0)
            lt = jnp.concatenate([lse_ref.at[e][rows_q, :] for e in range(2)], axis=0)
            dt = jnp.concatenate([dl_ref.at[e][rows_q, :] for e in range(2)], axis=0)
            if prev:
                lt = jnp.concatenate([lt, lt], axis=1)
                dt = jnp.concatenate([dt, dt], axis=1)
            p = jnp.exp(sc - lt)
            ds = p * (_dot_nt(dym, vb) - dt)
            for e in range(2):
                if prev:
                    db_ref[br, e] += ds[e * BLK:(e + 1) * BLK]
                else:
                    db_ref[br, e, :, BLK:] += ds[e * BLK:(e + 1) * BLK]
            dsb = ds.astype(BF16)
            km = jnp.concatenate([jnp.where(ev_k, kwin, 0.0), jnp.where(ev_k, 0.0, kwin)], axis=0).astype(BF16)
            dqa[rows_q, :] += _dot(jnp.concatenate([dsb[:BLK], dsb[BLK:]], axis=1), km) * scale
            dka[rows_k, :] += _dot_tn(dsb, qm) * scale
            dva[rows_k, :] += _dot_tn(p.astype(BF16), dym)

        _attn_blocks(s, visit, ATTN_GROUP_BWD)
        dq_ref[...] = dqa[...].astype(BF16)
        dk_ref[...] = dka[...].astype(BF16)
        dv_ref[...] = dva[...].astype(BF16)

    def zcol(j):
        return pl.BlockSpec((s, LANES), lambda p, j=j: (0, col0 + j + p))

    col = pl.BlockSpec((s, LANES), lambda p: (0, p))
    stat = pl.BlockSpec((2, s, LANES), lambda p: (p, 0, 0))
    tile = pl.BlockSpec((nbr, 2, BLK, 2 * BLK), lambda p: (0, p, 0, 0))
    wide = _sds((s, pairs * LANES), BF16)
    return _pcall(
        body, name=name, out_shape=(wide, wide, wide, _sds(btiles.shape, F32)), grid=(pairs,),
        in_specs=[zcol(0), zcol(pairs), zcol(2 * pairs), tile, col, stat, stat, tile],
        out_specs=(col, col, col, tile),
        scratch_shapes=[pltpu.VMEM((s, LANES), F32) for _ in range(3)], comm=comm,
    )(z, z, z, btiles, dyb, lse, delta, dbias_in)


def _row(v):
    return v.reshape(1, -1)


class _Rides:
    def __init__(self):
        self.table, self.grads = {}, {}

    def add(self, name, build):
        self.table.setdefault(name, []).append(build)

    def get(self, name):
        comm = None
        for build in self.table.get(name, ()):
            comm = build() if comm is None else _both(comm, build())
        return comm

    def ready(self, key, g):
        self.grads[key] = g


class _LocalSchedule:
    def __init__(self):
        self.big = {}

    def fwd_comms(self, l):
        return _Rides()

    def bwd_comms(self, l):
        return _Rides()

    def after_bwd(self, l, grads):
        self.big[l] = grads


def _layer_fwd(l, x, h, wts, prm, btiles, comms):
    d = x.shape[1]
    wq = d // 4
    depth = prm["norm_mix_g"].shape[0]
    gout = prm["out_norm_g"][l]
    z = _mm_n(h, wts["in_t"], l, nt=True, tn=256, out_dtype=F32, name="in_proj", comm=comms.get("in_proj"))
    yan, ycn = _mix_fwd(z, prm["conv_a_w"][l], prm["conv_c_w"][l], _row(prm["conv_c_b"][l]), _row(prm["ln_c_g"][l]),
                        _row(prm["ln_c_b"][l]), _row(gout[:wq]), _row(gout[3 * wq:]), "mix_fwd")
    yb, lse = _attn_fwd(z, btiles, 3 * wq // LANES, "attn_fwd", comm=comms.get("attn_fwd"))
    y, x_mid, h2 = _out_proj(yan, yb, ycn, _row(gout[wq:3 * wq]), x, wts["out"], l, _row(prm["norm_ffn_g"][l]),
                             "out_proj", comm=comms.get("out_proj"))
    up = _mm_n(h2, wts["up"], l, nt=False, tn=512, out_dtype=BF16, name="up_proj", comm=comms.get("up_proj"))
    act = _ffn_act_fwd(up, prm["conv_f_w"][l], "ffn_act_fwd", comm=comms.get("ffn_act_fwd"))
    g_next = _row(prm["norm_mix_g"][l + 1]) if l + 1 < depth else None
    x_out, h_next = _down_proj(act, wts["down"], l, x_mid, g_next, "down_proj", comm=comms.get("down_proj"))
    return x_out, h_next, (x, h, z, yb, lse, y, x_mid, h2, up)


def _layer_bwd(l, dxo, dxo_b, saved, wts, prm, btiles, dbias, comms):
    x, h, z, yb, lse, y, x_mid, h2, up = saved
    d = x.shape[1]
    wq = d // 4
    gout = prm["out_norm_g"][l]
    dact = _mm_n(dxo_b, wts["down"], l, nt=True, tn=256, out_dtype=BF16, name="down_proj_dx", comm=comms.get("down_proj_dx"))
    act, dgate, dval, dwf = _ffn_act_bwd(up, dact, prm["conv_f_w"][l], "ffn_act_bwd", comm=comms.get("ffn_act_bwd"))
    g_down = _mm_tn(act, dxo_b, t=256, name="down_proj_dw")
    comms.ready("down", g_down)
    dxm, dxm_b, dg_ffn = _proj_dx([dgate, dval], wts["up"], l, True, x_mid, _row(prm["norm_ffn_g"][l]), dxo, "up_proj_dx",
                                  ROW_CHUNK, comm=comms.get("up_proj_dx"))
    g_up = _mm_tn2(h2, dgate, dval, t=256, name="up_proj_dw", comm=comms.get("up_proj_dw"))
    comms.ready("up", g_up)
    dy = _mm_n(dxm_b, wts["out"], l, nt=True, tn=256, out_dtype=F32, name="out_proj_dx")
    g_out = _mm_tn(y, dxm_b, t=256, name="out_proj_dw")
    comms.ready("out", g_out)
    dza, dzc, dwa, dwc, dcb, dlg, dlb, dga, dgc = _mix_bwd(
        z, dy, prm["conv_a_w"][l], prm["conv_c_w"][l], _row(prm["conv_c_b"][l]), _row(prm["ln_c_g"][l]),
        _row(prm["ln_c_b"][l]), _row(gout[:wq]), _row(gout[3 * wq:]), "mix_bwd", comm=comms.get("mix_bwd"))
    dyb, delta, dgb = _yb_norm_bwd(yb, dy, _row(gout[wq:3 * wq]), "yb_norm_bwd", comm=comms.get("yb_norm_bwd"))
    dq, dk, dv, dbias = _attn_bwd(z, btiles, dyb, lse, delta, dbias, 3 * wq // LANES, "attn_bwd",
                                  comm=comms.get("attn_bwd"))
    dz = [dza, dq, dk, dv, dzc]
    dx, dx_b, dg_mix = _proj_dx(dz, wts["in_t"], l, False, x, _row(prm["norm_mix_g"][l]), dxm, "in_proj_dx",
                                PROJ_ROWS, comm=comms.get("in_proj_dx"))
    g_in_t = _mm_tn_pieces(dz, h, t=256, name="in_proj_dw")
    big = {"in_t": g_in_t, "out": g_out, "up": g_up, "down": g_down}
    small = {"norm_mix_g": dg_mix[0], "conv_a_w": dwa, "conv_c_w": dwc, "conv_c_b": dcb[0], "ln_c_g": dlg[0],
             "ln_c_b": dlb[0], "out_norm_g": jnp.concatenate([dga[0], dgb[0], dgc[0]]), "norm_ffn_g": dg_ffn[0],
             "conv_f_w": dwf}
    return dx, dx_b, big, small, dbias


def _local_step(x, tgt, wts, prm, sched):
    depth = prm["norm_mix_g"].shape[0]
    buckets = jnp.asarray(_t5_bucket_table())
    btiles = _bias_tiles(prm["rel_bias"], buckets, "bias_tiles")
    saved = []
    h = _rms_fwd(x, _row(prm["norm_mix_g"][0]), "rms_mix_fwd")
    for l in range(depth):
        x, h, sv = _layer_fwd(l, x, h, wts, prm, btiles, sched.fwd_comms(l))
        saved.append(sv)
    loss, dx, dx_b, dg_final = _final_loss(x, _row(prm["final_g"]), tgt, "final_loss")
    dbias = jnp.zeros(btiles.shape, F32)
    small = [None] * depth
    for l in reversed(range(depth)):
        dx, dx_b, grads, small[l], dbias = _layer_bwd(l, dx, dx_b, saved[l], wts, prm, btiles, dbias, sched.bwd_comms(l))
        sched.after_bwd(l, grads)
    nbk, heads = prm["rel_bias"].shape
    d_rel = _bias_grad(dbias, buckets, nbk, "bias_grad")[:, :heads]
    return loss, dx, small, d_rel, dg_final[0]


BIG = ("in_t", "out", "up", "down")
COL_SHARDED = ("up",)
N_CHIPS = 4
N_DEV = 8
BF16_ROWS = 16


def _me():
    return lax.axis_index("x"), lax.axis_index("y"), lax.axis_index("c")


def _chip_of(x, y):
    return 2 * x + y


def _other_chips(x, y):
    return ((1 - x, y), (x, 1 - y), (1 - x, 1 - y))


def _remote(src, dst, send_sem, recv_sem, device):
    return pltpu.make_async_remote_copy(src_ref=src, dst_ref=dst, send_sem=send_sem, recv_sem=recv_sem,
                                        device_id=device, device_id_type=MESH)


ALL_FLIPS = (0, 1, 2)


def _ag_comm(wts, layer, ici_keys, fwd_keys):
    flips = {(k if isinstance(k, str) else k[0]): (ALL_FLIPS if isinstance(k, str) else k[1]) for k in ici_keys}
    keys = tuple(k for k in BIG if k in flips or k in fwd_keys)

    def geo(k):
        _, rows, cols = wts[k].shape
        return (rows, cols // N_CHIPS) if k in COL_SHARDED else (rows // N_CHIPS, cols)

    def copies(refs, sems):
        g = dict(zip(keys, refs))
        isend, irecv, dsend, drecv = sems
        x, y, c = _me()
        mine = _chip_of(x, y)

        def region(k, chip, half):
            r, cc = geo(k)
            h = r // 2
            if k in COL_SHARDED:
                return g[k].at[layer, pl.ds(pl.multiple_of(half * h, BF16_ROWS), h), pl.ds(pl.multiple_of(chip * cc, LANES), cc)]
            return g[k].at[layer, pl.ds(pl.multiple_of(chip * r + half * h, BF16_ROWS), h), :]

        def ici(k, f, landing):
            chip = _other_chips(x, y)[f]
            where = region(k, _chip_of(*chip) if landing else mine, c)
            i = keys.index(k)
            return _remote(where, where, isend.at[i, f], irecv.at[i, f], (*chip, c))

        def fwd(k, f, landing):
            chip = _other_chips(x, y)[f]
            where = region(k, _chip_of(*chip), 1 - c if landing else c)
            i = keys.index(k)
            return _remote(where, where, dsend.at[i, f], drecv.at[i, f], (x, y, 1 - c))

        return ici, fwd

    def start(ins, outs, sems):
        ici, fwd = copies(outs, sems)
        for k in keys:
            for f in flips.get(k, ALL_FLIPS):
                if k in flips:
                    ici(k, f, False).start()
                else:
                    fwd(k, f, False).start()

    def finish(ins, outs, sems):
        ici, fwd = copies(outs, sems)
        for k in keys:
            for f in flips.get(k, ()):
                ici(k, f, True).wait_recv()
                if k in fwd_keys:
                    fwd(k, f, False).start()
        for k in keys:
            for f in flips.get(k, ALL_FLIPS):
                if k in fwd_keys:
                    fwd(k, f, True).wait_recv()
                    fwd(k, f, False).wait_send()
                if k in flips:
                    ici(k, f, False).wait_send()

    def done(res):
        wts.update(zip(keys, res))

    n = len(keys)
    return _Comm([wts[k] for k in keys], [_sds(wts[k].shape, BF16) for k in keys], {i: i for i in range(n)},
                 [pltpu.SemaphoreType.DMA((n, 3)) for _ in range(4)], start, finish, done)


def _small_gather_comm(slab, store):
    def copies(ins, outs, sems):
        send, recv, lsem = sems
        x, y, c = _me()
        mine = _chip_of(x, y)
        own = pltpu.make_async_copy(ins[0], outs[0].at[mine], lsem)
        pairs = []
        for f, chip in enumerate(_other_chips(x, y)):
            out = _remote(ins[0], outs[0].at[mine], send.at[f], recv.at[f], (*chip, c))
            land = _remote(ins[0], outs[0].at[_chip_of(*chip)], send.at[f], recv.at[f], (*chip, c))
            pairs.append((out, land))
        return own, pairs

    def start(ins, outs, sems):
        own, pairs = copies(ins, outs, sems)
        own.start()
        for out, _ in pairs:
            out.start()

    def finish(ins, outs, sems):
        own, pairs = copies(ins, outs, sems)
        for out, land in pairs:
            land.wait_recv()
            out.wait_send()
        own.wait()

    def done(res):
        store["small"] = res[0]

    return _Comm([slab], [_sds((N_CHIPS,) + slab.shape, F32)], {},
                 [pltpu.SemaphoreType.DMA((3,)), pltpu.SemaphoreType.DMA((3,)), pltpu.SemaphoreType.DMA], start, finish, done)


def _piece_geo(g):
    geo = {}
    for k in g:
        rows, cols = g[k].shape
        geo[k] = (rows // 2, cols // N_CHIPS) if k in COL_SHARDED else (rows // (2 * N_CHIPS), cols)
    return geo


def _swap_comm(g, keys, done):
    geo = _piece_geo(g)
    n_copies = sum(N_CHIPS if k in COL_SHARDED else 1 for k in keys)

    def copies(ins, outs, sems):
        g_refs, t_refs = dict(zip(keys, ins)), dict(zip(keys, outs))
        send, recv = sems
        x, y, c = _me()
        pairs = []
        for k in keys:
            h, cc = geo[k]
            if k in COL_SHARDED:
                rows = pl.ds(pl.multiple_of((1 - c) * h, BF16_ROWS), h)
                pairs += [(g_refs[k].at[rows, pl.ds(j * cc, cc)], t_refs[k].at[j]) for j in range(N_CHIPS)]
            else:
                pairs.append((g_refs[k].at[:, 1 - c], t_refs[k]))
        return [_remote(src, dst, send.at[i], recv.at[i], (x, y, 1 - c)) for i, (src, dst) in enumerate(pairs)]

    def start(ins, outs, sems):
        for cp in copies(ins, outs, sems):
            cp.start()

    def finish(ins, outs, sems):
        for cp in copies(ins, outs, sems):
            cp.wait()

    ins = [g[k] if k in COL_SHARDED else g[k].reshape(N_CHIPS, 2, geo[k][0], geo[k][1]) for k in keys]
    return _Comm(ins, [_sds((N_CHIPS,) + geo[k], BF16) for k in keys], {},
                 [pltpu.SemaphoreType.DMA((n_copies,)) for _ in range(2)], start, finish,
                 lambda res: done(dict(zip(keys, res))))


def _pair_sum(g, theirs, c_arr, keys):
    geo = _piece_geo(g)

    def body(c_ref, *refs):
        nk = len(keys)
        for i in range(nk):
            refs[2 * nk + i][...] = (refs[i][...].astype(F32) + refs[nk + i][...].astype(F32)).astype(BF16)

    in_specs, ins = [], []
    for k in keys:
        h, cc = geo[k]
        if k in COL_SHARDED:
            in_specs.append(pl.BlockSpec((h, cc), lambda j, c_ref: (c_ref[0], j)))
            ins.append(g[k])
        else:
            in_specs.append(pl.BlockSpec((None, h, cc), lambda j, c_ref: (2 * j + c_ref[0], 0, 0)))
            ins.append(g[k].reshape(2 * N_CHIPS, h, cc))
    slab = [pl.BlockSpec((None,) + geo[k], lambda j, c_ref: (j, 0, 0)) for k in keys]
    res = _pcall(body, name="rs_pair_sum", out_shape=tuple(_sds((N_CHIPS,) + geo[k], BF16) for k in keys), grid=(N_CHIPS,),
                 in_specs=in_specs + slab, out_specs=tuple(slab), prefetch=1)(c_arr, *ins, *[theirs[k] for k in keys])
    return dict(zip(keys, res))


def _rs_comm(p, keys, store):
    def copies(ins, outs, sems):
        send, recv = sems
        x, y, c = _me()
        return [_remote(ins[i].at[_chip_of(*chip)], outs[i].at[f], send.at[i, f], recv.at[i, f], (*chip, c))
                for i in range(len(keys)) for f, chip in enumerate(_other_chips(x, y))]

    def start(ins, outs, sems):
        for cp in copies(ins, outs, sems):
            cp.start()

    def finish(ins, outs, sems):
        for cp in copies(ins, outs, sems):
            cp.wait()

    def done(res):
        store.update(zip(keys, res))

    return _Comm([p[k] for k in keys], [_sds((3,) + p[k].shape[1:], BF16) for k in keys], {},
                 [pltpu.SemaphoreType.DMA((len(keys), 3)) for _ in range(2)], start, finish, done)


def _quad_sum(p, b, where, l, full):
    parts = 2
    nk = len(BIG)

    def body(where_ref, *refs):
        for i in range(nk):
            acc = refs[i][...].astype(F32)
            for f in range(3):
                acc = acc + refs[nk + 3 * i + f][...].astype(F32)
            refs[5 * nk + i][...] = acc

    own, recv, outs = [], [], []
    for k in BIG:
        h, cc = p[k].shape[1:]
        th = h // parts
        own.append(pl.BlockSpec((None, th, cc), lambda i, w_ref: (w_ref[0], i, 0)))
        recv += [pl.BlockSpec((None, th, cc), lambda i, w_ref, f=f: (f, i, 0)) for f in range(3)]
        outs.append(pl.BlockSpec((None, None, th, cc), lambda i, w_ref: (l, w_ref[1], i, 0)))
    args = [p[k] for k in BIG] + [b[k] for k in BIG for _ in range(3)] + [full[k] for k in BIG]
    res = _pcall(body, name="rs_quad_sum", out_shape=tuple(_sds(full[k].shape, F32) for k in BIG), grid=(parts,),
                 in_specs=own + recv + [ANY] * nk, out_specs=tuple(outs), prefetch=1,
                 aliases={1 + 4 * nk + i: i for i in range(nk)})(where, *args)
    return dict(zip(BIG, res))


def _share_comm(layers, full, done):
    nk = len(BIG)

    def copies(outs, sems, landing):
        send, recv = sems
        x, y, c = _me()
        half = 1 - c if landing else c
        return [_remote(outs[i].at[l, half], outs[i].at[l, half], send.at[i, j], recv.at[i, j], (x, y, 1 - c))
                for i in range(nk) for j, l in enumerate(layers)]

    def start(ins, outs, sems):
        for cp in copies(outs, sems, False):
            cp.start()

    def finish(ins, outs, sems):
        for cp in copies(outs, sems, True):
            cp.wait_recv()
        for cp in copies(outs, sems, False):
            cp.wait_send()

    return _Comm([full[k] for k in BIG], [_sds(full[k].shape, F32) for k in BIG], {i: i for i in range(nk)},
                 [pltpu.SemaphoreType.DMA((nk, len(layers))) for _ in range(2)], start, finish,
                 lambda res: done(dict(zip(BIG, res))))


def _gather_comm(slab, done):
    def copies(ins, outs, sems, landing):
        send, recv = sems
        x, y, c = _me()
        me = 4 * x + 2 * y + c
        out = []
        for mask in range(1, N_DEV):
            peer = (x ^ (mask >> 2), y ^ ((mask >> 1) & 1), c ^ (mask & 1))
            slot = 4 * peer[0] + 2 * peer[1] + peer[2] if landing else me
            out.append(_remote(ins[0], outs[0].at[slot], send.at[mask - 1], recv.at[mask - 1], peer))
        return out

    def start(ins, outs, sems):
        for cp in copies(ins, outs, sems, False):
            cp.start()

    def finish(ins, outs, sems):
        for cp in copies(ins, outs, sems, True):
            cp.wait_recv()
        for cp in copies(ins, outs, sems, False):
            cp.wait_send()

    return _Comm([slab], [_sds((N_DEV,) + slab.shape, F32)], {},
                 [pltpu.SemaphoreType.DMA((N_DEV - 1,)), pltpu.SemaphoreType.DMA((N_DEV - 1,))], start, finish,
                 lambda res: done(res[0]))


def _sum_slabs(slabs, own, me):
    n, r, lanes = slabs.shape
    tr = r // 2

    def body(me_ref, s_ref, own_ref, o_ref):
        o_ref[...] = jnp.zeros((tr, lanes), F32)
        for i in range(n):
            @pl.when(me_ref[0] == i)
            def _():
                o_ref[...] += own_ref[...]

            @pl.when(me_ref[0] != i)
            def _():
                o_ref[...] += s_ref[i]

    return _pcall(body, name="sum_partials", out_shape=_sds((r, lanes), F32), grid=(2,),
                  in_specs=[pl.BlockSpec((n, tr, lanes), lambda i, me_ref: (0, i, 0)),
                            pl.BlockSpec((tr, lanes), lambda i, me_ref: (i, 0))],
                  out_specs=pl.BlockSpec((tr, lanes), lambda i, me_ref: (i, 0)), prefetch=1)(me, slabs, own)


def _cast_into_gathered(w, chip, by_cols, name):
    l, r, c = w.shape

    def body(chip_ref, w_ref, o_ref):
        o_ref[...] = w_ref[...].astype(BF16)

    if by_cols:
        shape, out = (l, r, N_CHIPS * c), pl.BlockSpec((None, r, c), lambda i, chip_ref: (i, 0, chip_ref[0]))
    else:
        shape, out = (l, N_CHIPS * r, c), pl.BlockSpec((None, r, c), lambda i, chip_ref: (i, chip_ref[0], 0))
    return _pcall(body, name=name, out_shape=_sds(shape, BF16), grid=(l,),
                  in_specs=[pl.BlockSpec((None, r, c), lambda i, chip_ref: (i, 0, 0))], out_specs=out, prefetch=1)(chip, w)


def _adamw_math(w, g, m, v):
    mn = ADAM_B1 * m + (1.0 - ADAM_B1) * g
    vn = ADAM_B2 * v + (1.0 - ADAM_B2) * (g * g)
    m_hat = mn / (1.0 - ADAM_B1 ** ADAM_STEP)
    v_hat = vn / (1.0 - ADAM_B2 ** ADAM_STEP)
    return -ADAM_LR * (m_hat / (jnp.sqrt(v_hat) + ADAM_EPS) + ADAM_WD * w), mn, vn


def _adamw(w, g, m, v, name, tr):
    r, c = w.shape

    def body(w_ref, g_ref, m_ref, v_ref, go_ref, d_ref, mo_ref, vo_ref):
        gv = g_ref[...]
        go_ref[...] = gv
        d_ref[...], mo_ref[...], vo_ref[...] = _adamw_math(w_ref[...], gv, m_ref[...], v_ref[...])

    blk = pl.BlockSpec((tr, c), lambda i: (i, 0))
    return _pcall(body, name=name, out_shape=tuple(_sds((r, c), F32) for _ in range(4)), grid=(r // tr,),
                  in_specs=[blk] * 4, out_specs=(blk, blk, blk, blk))(w, g, m, v)


def _adamw_small(groups, name):
    count = len(groups[0])
    shapes = [a.shape for a in groups[0]]
    as2d = [(math.prod(s[:-1]), s[-1]) for s in shapes]

    def body(*refs):
        for i in range(count):
            out = _adamw_math(*[refs[j * count + i][...] for j in range(4)])
            for j in range(3):
                refs[(4 + j) * count + i][...] = out[j]

    specs = [pl.BlockSpec(s, lambda i: (0, 0)) for s in as2d]
    res = _pcall(body, name=name, out_shape=tuple(_sds(s, F32) for _ in range(3) for s in as2d), grid=(1,),
                 in_specs=specs * 4, out_specs=tuple(specs * 3))(*[a.reshape(s) for grp in groups for a, s in zip(grp, as2d)])
    return [[res[j * count + i].reshape(shapes[i]) for i in range(count)] for j in range(3)]


AG_RIDES = {"in_proj": (0, ("out",), ("down",)), "attn_fwd": (0, (("up", (0, 1)),), ("out",)), "out_proj": (0, (), ("up",)),
            "up_proj": (1, ("in_t",), ()), "ffn_act_fwd": (1, ("down",), ()),
            "down_proj": (1, (("up", (2,)),), ("in_t",))}
AG_FIRST = ("in_t", "down")
RS_RIDES = {"down_proj_dx": "swap", "ffn_act_bwd": ("up",), "up_proj_dx": "share", "attn_bwd": ("in_t", "down"),
            "mix_bwd": ("out",)}
RS_RIDES_LAST = {"down_proj_dx": "swap", "ffn_act_bwd": ("up",), "up_proj_dx": "share", "up_proj_dw": ("in_t",),
                 "mix_bwd": ("down",), "yb_norm_bwd": ("out",)}
EARLY = ("out", "up", "down")
EARLY_RIDES = {"mix_bwd": "swap", "attn_bwd": ("up", "down"), "in_proj_dx": ("out",)}


class _Reduction:
    def __init__(self, grads):
        self.grads, self.pairs, self.recv = grads, {}, {}


class _MeshSchedule:
    def __init__(self, wts, depth, c_arr, where):
        self.wts, self.depth, self.c_arr, self.where = wts, depth, c_arr, where
        self.pending, self.last, self.full, self.unshared = None, None, None, []

    def fwd_comms(self, l):
        rides = _Rides()
        for name, (off, ici, fwd) in AG_RIDES.items():
            if l + off == 0:
                ici = tuple(k if isinstance(k, str) else k[0] for k in ici)
                ici, fwd = (tuple(k for k in keys if k not in AG_FIRST) for keys in (ici, fwd))
            if l + off < self.depth and (ici or fwd):
                rides.add(name, lambda ride=(l + off, ici, fwd): _ag_comm(self.wts, *ride))
        return rides

    def _shared(self, full):
        self.full, self.unshared = full, []

    def _ride(self, red, what, swap_keys):
        if what == "swap":
            return _swap_comm(red.grads, swap_keys, lambda theirs: red.pairs.update(
                _pair_sum(red.grads, theirs, self.c_arr, swap_keys)))
        if what == "share":
            return _share_comm(self.unshared, self.full, self._shared)
        return _rs_comm(red.pairs, what, red.recv)

    def bwd_comms(self, l):
        rides = _Rides()
        if self.pending is not None:
            for name, what in (RS_RIDES_LAST if l == 0 else RS_RIDES).items():
                if what != "share" or self.unshared:
                    rides.add(name, lambda what=what, red=self.pending: self._ride(red, what, BIG))
        if l == 0:
            self.last = _Reduction(rides.grads)
            for name, what in EARLY_RIDES.items():
                rides.add(name, lambda what=what: self._ride(self.last, what, EARLY))
        return rides

    def _reduce(self, l, red):
        self.full = _quad_sum(red.pairs, red.recv, self.where, l, self.full)
        self.unshared = self.unshared + [l]

    def after_bwd(self, l, grads):
        if self.pending is not None:
            self._reduce(l + 1, self.pending)
        if self.full is None:
            geo = _piece_geo(grads)
            self.full = {k: jnp.zeros((self.depth, 2) + geo[k], F32) for k in BIG}
        self.pending = _Reduction(grads) if l > 0 else None
        if l == 0:
            self.last.grads = grads

    def finish(self, extra):
        red = self.last
        late = tuple(k for k in BIG if k not in red.pairs)
        _run_comm(self._ride(red, "swap", late), "rs_swap_halves")
        _run_comm(_rs_comm(red.pairs, late, red.recv), "rs_to_owners")
        self._reduce(0, red)
        _run_comm(_both(_share_comm(self.unshared, self.full, self._shared), extra), "rs_share")
        return self.full


SHARDED_SMALL = ("conv_a_w", "conv_c_w", "conv_f_w")
SMALL = ("norm_mix_g", "conv_a_w", "conv_c_w", "conv_c_b", "ln_c_g", "ln_c_b", "out_norm_g", "norm_ffn_g",
         "conv_f_w", "rel_bias", "final_g")
SLAB_ROWS = 16


def _pack(arrays):
    flat = jnp.concatenate([a.reshape(-1) for a in arrays])
    unit = SLAB_ROWS * LANES
    total = -(-flat.shape[0] // unit) * unit
    return jnp.pad(flat, (0, total - flat.shape[0])).reshape(-1, LANES)


def _unpack(slab, shapes):
    flat = slab.reshape(-1)
    out, off = [], 0
    for shp in shapes:
        size = math.prod(shp)
        out.append(flat[off:off + size].reshape(shp))
        off += size
    return out


def kernel(x, norm_mix_g, w_in, conv_a_w, conv_c_w, conv_c_b, ln_c_g, ln_c_b, out_norm_g, w_out, norm_ffn_g, w_up, conv_f_w, w_down, rel_bias, final_g, loss_target, m_norm_mix_g, m_w_in, m_conv_a_w, m_conv_c_w, m_conv_c_b, m_ln_c_g, m_ln_c_b, m_out_norm_g, m_w_out, m_norm_ffn_g, m_w_up, m_conv_f_w, m_w_down, m_rel_bias, m_final_g, v_norm_mix_g, v_w_in, v_conv_a_w, v_conv_c_w, v_conv_c_b, v_ln_c_g, v_ln_c_b, v_out_norm_g, v_w_out, v_norm_ffn_g, v_w_up, v_conv_f_w, v_w_down, v_rel_bias, v_final_g):
    weights = dict(norm_mix_g=norm_mix_g, w_in=w_in, conv_a_w=conv_a_w, conv_c_w=conv_c_w, conv_c_b=conv_c_b,
                   ln_c_g=ln_c_g, ln_c_b=ln_c_b, out_norm_g=out_norm_g, w_out=w_out, norm_ffn_g=norm_ffn_g, w_up=w_up,
                   conv_f_w=conv_f_w, w_down=w_down, rel_bias=rel_bias, final_g=final_g)
    mom_m = dict(norm_mix_g=m_norm_mix_g, w_in=m_w_in, conv_a_w=m_conv_a_w, conv_c_w=m_conv_c_w, conv_c_b=m_conv_c_b,
                 ln_c_g=m_ln_c_g, ln_c_b=m_ln_c_b, out_norm_g=m_out_norm_g, w_out=m_w_out, norm_ffn_g=m_norm_ffn_g,
                 w_up=m_w_up, conv_f_w=m_conv_f_w, w_down=m_w_down, rel_bias=m_rel_bias, final_g=m_final_g)
    mom_v = dict(norm_mix_g=v_norm_mix_g, w_in=v_w_in, conv_a_w=v_conv_a_w, conv_c_w=v_conv_c_w, conv_c_b=v_conv_c_b,
                 ln_c_g=v_ln_c_g, ln_c_b=v_ln_c_b, out_norm_g=v_out_norm_g, w_out=v_w_out, norm_ffn_g=v_norm_ffn_g,
                 w_up=v_w_up, conv_f_w=v_conv_f_w, w_down=v_w_down, rel_bias=v_rel_bias, final_g=v_final_g)
    xi, yi, ci = _me()
    chip = _chip_of(xi, yi)
    c_arr = jnp.reshape(ci, (1,)).astype(I32)
    chip_arr = jnp.reshape(chip, (1,)).astype(I32)
    me_arr = jnp.reshape(4 * xi + 2 * yi + ci, (1,)).astype(I32)
    where = jnp.stack([chip, ci]).astype(I32)
    depth = w_out.shape[0]

    wts = {"in_t": _cast_into_gathered(jnp.swapaxes(w_in, 1, 2), chip_arr, False, "cast_in"),
           "out": _cast_into_gathered(w_out, chip_arr, False, "cast_out"),
           "up": _cast_into_gathered(w_up, chip_arr, True, "cast_up"),
           "down": _cast_into_gathered(w_down, chip_arr, False, "cast_down")}
    store = {}
    _run_comm(_both(_ag_comm(wts, 0, AG_FIRST, AG_FIRST),
                    _small_gather_comm(_pack([weights[n] for n in SHARDED_SMALL]), store)), "ag_weights")
    prm = {n: weights[n] for n in SMALL if n not in SHARDED_SMALL}
    per_chip = [_unpack(store["small"][j], [weights[n].shape for n in SHARDED_SMALL]) for j in range(N_CHIPS)]
    for i, n in enumerate(SHARDED_SMALL):
        prm[n] = jnp.concatenate([per_chip[j][i] for j in range(N_CHIPS)], axis=-1)

    sched = _MeshSchedule(wts, depth, c_arr, where)
    loss_row, dx, small, d_rel, d_final = _local_step(x[0], loss_target[0], wts, prm, sched)
    loss = lax.psum(loss_row[0, 0], ("x", "y", "c"))

    stacked = {n: jnp.stack([small[l][n] for l in range(depth)]) for n in small[0]}
    stacked["rel_bias"] = d_rel
    stacked["final_g"] = d_final
    full_shapes = [stacked[n].shape for n in SMALL]
    partial = _pack([stacked[n] for n in SMALL])
    reduced = sched.finish(_gather_comm(partial, lambda res: store.update(partials=res)))

    grads = {}
    shard_shapes = {"in_t": jnp.swapaxes(w_in, 1, 2).shape, "out": w_out.shape, "up": w_up.shape, "down": w_down.shape}
    red = {k: reduced[k].reshape(shard_shapes[k]) for k in BIG}
    grads["w_in"] = jnp.swapaxes(red["in_t"], 1, 2)
    grads["w_out"], grads["w_up"], grads["w_down"] = red["out"], red["up"], red["down"]
    delta, new_m, new_v = {}, {}, {}
    for n in ("w_in", "w_out", "w_up", "w_down"):
        shp = weights[n].shape
        flat = lambda a, shp=shp: a.reshape(shp[0] * shp[1], shp[2])
        tile = max(t for t in range(8, 257, 8) if shp[1] % t == 0)
        g, d, mn, vn = _adamw(flat(weights[n]), flat(grads[n]), flat(mom_m[n]), flat(mom_v[n]), "adamw_" + n, tile)
        grads[n], delta[n], new_m[n], new_v[n] = g.reshape(shp), d.reshape(shp), mn.reshape(shp), vn.reshape(shp)

    summed = _unpack(_sum_slabs(store["partials"], partial, me_arr), full_shapes)
    for n, g in zip(SMALL, summed):
        if n in SHARDED_SMALL:
            width = weights[n].shape[-1]
            g = lax.dynamic_slice_in_dim(g, chip * width, width, axis=g.ndim - 1)
        grads[n] = g
    res = _adamw_small([[src[n] for n in SMALL] for src in (weights, grads, mom_m, mom_v)], "adamw_small")
    for i, n in enumerate(SMALL):
        delta[n], new_m[n], new_v[n] = res[0][i], res[1][i], res[2][i]

    order = ("norm_mix_g", "w_in", "conv_a_w", "conv_c_w", "conv_c_b", "ln_c_g", "ln_c_b", "out_norm_g", "w_out",
             "norm_ffn_g", "w_up", "conv_f_w", "w_down", "rel_bias", "final_g")
    return (loss, dx[None], *[grads[n] for n in order], *[delta[n] for n in order], *[new_m[n] for n in order],
            *[new_v[n] for n in order])
```
